```python
import jax, jax.numpy as jnp
from jax import lax
import numpy as np

D_MODEL = 1024
BATCH = 8
SEQ = 8192
DEPTH = 4

CHUNK = 64
D_MIX = D_MODEL
D_CONV = D_MIX // 4
D_POOL = D_MIX // 4
D_ATTN = D_MIX // 2
HEAD_DIM = 64
N_HEADS = D_ATTN // HEAD_DIM
CONV_W = 3
POOL_WINDOWS = (2, 4, 8, 16)
N_POOL = len(POOL_WINDOWS)
POOL_GC = D_POOL // N_POOL
LEFT_CHUNKS = 8
BAND = (LEFT_CHUNKS + 1) * CHUNK
REL_CLIP = 128
D_FF = ((8 * D_MODEL // 3 + 255) // 256) * 256
IN_COLS = 3 * D_CONV + D_POOL + 3 * D_ATTN
EPS = 1e-6

kernel_name = "hybrid_conv_pool_chunkattn_trunk"


def rmsnorm(x, g):
    xf = x.astype(jnp.float32)
    y = xf * lax.rsqrt(jnp.mean(xf * xf, axis=-1, keepdims=True) + EPS)
    return (y * g.astype(jnp.float32)).astype(x.dtype)


def shift_right(u, n):
    s = u.shape[1]
    return jnp.pad(u, ((0, 0), (n, 0), (0, 0)))[:, :s]


def short_conv_mixer(gb, gc, u, conv_w):
    z = gc * u
    conv = conv_w[2] * z + conv_w[1] * shift_right(z, 1) + conv_w[0] * shift_right(z, 2)
    return gb * conv


def pool_mixer(u, pool_w, pool_scale):
    b, s, _ = u.shape
    uf = u.astype(jnp.float32)
    cs = jnp.cumsum(uf, axis=1)
    t = jnp.arange(s)
    outs = []
    for gi, w in enumerate(POOL_WINDOWS):
        sl = slice(gi * POOL_GC, (gi + 1) * POOL_GC)
        c = cs[..., sl]
        cnt = jnp.minimum(t + 1, w).astype(jnp.float32)[None, :, None]
        outs.append((c - shift_right(c, w)) / cnt - uf[..., sl])
    d = jnp.concatenate(outs, axis=-1).astype(u.dtype).reshape(b, s, N_POOL, POOL_GC)
    y = jnp.einsum('bsgc,gcd->bsgd', d, pool_w).reshape(b, s, D_POOL)
    return y * pool_scale


def chunk_attention(q, k, v, rel_bias):
    b, s, h, dh = q.shape
    n_chunks = s // CHUNK
    pad = ((0, 0), (LEFT_CHUNKS * CHUNK, 0), (0, 0), (0, 0))
    kpad = jnp.pad(k, pad)
    vpad = jnp.pad(v, pad)
    qi = jnp.arange(CHUNK)[:, None]
    kj = jnp.arange(BAND)[None, :]
    rel = LEFT_CHUNKS * CHUNK + qi - kj
    idx = jnp.clip(rel, -REL_CLIP, REL_CLIP) + REL_CLIP
    bias = rel_bias[:, idx].astype(jnp.float32)
    key_off = jnp.arange(BAND)
    scale = HEAD_DIM ** -0.5

    def one_chunk(c):
        qc = lax.dynamic_slice_in_dim(q, c * CHUNK, CHUNK, axis=1)
        kb = lax.dynamic_slice_in_dim(kpad, c * CHUNK, BAND, axis=1)
        vb = lax.dynamic_slice_in_dim(vpad, c * CHUNK, BAND, axis=1)
        sc = jnp.einsum('bqhd,bkhd->bhqk', qc, kb).astype(jnp.float32) * scale + bias[None]
        valid = key_off >= (LEFT_CHUNKS - c) * CHUNK
        sc = jnp.where(valid, sc, jnp.finfo(jnp.float32).min)
        p = jax.nn.softmax(sc, axis=-1).astype(vb.dtype)
        return jnp.einsum('bhqk,bkhd->bqhd', p, vb)

    out = lax.map(one_chunk, jnp.arange(n_chunks))
    return out.transpose(1, 0, 2, 3, 4).reshape(b, s, h * dh)


def _fwd_setup_inputs(seed: int = 0) -> dict:
    key = jax.random.key(seed)
    ks = jax.random.split(key, 16)
    f32 = jnp.float32
    nrm = lambda k, shp, sc: jax.random.normal(k, shp, f32) * sc
    gain = lambda k, shp: 1.0 + 0.05 * jax.random.normal(k, shp, f32)
    return {
        "x": jax.random.normal(ks[0], (BATCH, SEQ, D_MODEL), f32),
        "w_in": nrm(ks[1], (DEPTH, D_MODEL, IN_COLS), D_MODEL ** -0.5),
        "w_out": nrm(ks[2], (DEPTH, D_MIX, D_MODEL), D_MIX ** -0.5),
        "conv_w": nrm(ks[3], (DEPTH, CONV_W, D_CONV), CONV_W ** -0.5),
        "pool_w": nrm(ks[4], (DEPTH, N_POOL, POOL_GC, POOL_GC), POOL_GC ** -0.5),
        "pool_scale": gain(ks[5], (DEPTH, D_POOL)),
        "rel_bias": nrm(ks[6], (DEPTH, N_HEADS, 2 * REL_CLIP + 1), 0.1),
        "group_gain": gain(ks[7], (DEPTH, D_MIX)),
        "pre_mix_g": gain(ks[8], (DEPTH, D_MODEL)),
        "post_mix_g": gain(ks[9], (DEPTH, D_MODEL)),
        "pre_ffn_g": gain(ks[10], (DEPTH, D_MODEL)),
        "post_ffn_g": gain(ks[11], (DEPTH, D_MODEL)),
        "w_gate_up": nrm(ks[12], (DEPTH, D_MODEL, 2 * D_FF), D_MODEL ** -0.5),
        "w_down": nrm(ks[13], (DEPTH, D_FF, D_MODEL), D_FF ** -0.5),
    }


def _fwd_reference(x, w_in, w_out, conv_w, pool_w, pool_scale, rel_bias, group_gain,
              pre_mix_g, post_mix_g, pre_ffn_g, post_ffn_g, w_gate_up, w_down):
    b, s, _ = x.shape
    h = x
    for l in range(DEPTH):
        xn = rmsnorm(h, pre_mix_g[l])
        proj = jnp.einsum('bsd,dc->bsc', xn, w_in[l])
        o = 0
        gb = proj[..., o:o + D_CONV]; o += D_CONV
        gc = proj[..., o:o + D_CONV]; o += D_CONV
        u = proj[..., o:o + D_CONV]; o += D_CONV
        pu = proj[..., o:o + D_POOL]; o += D_POOL
        q = proj[..., o:o + D_ATTN].reshape(b, s, N_HEADS, HEAD_DIM); o += D_ATTN
        k = proj[..., o:o + D_ATTN].reshape(b, s, N_HEADS, HEAD_DIM); o += D_ATTN
        v = proj[..., o:o + D_ATTN].reshape(b, s, N_HEADS, HEAD_DIM)

        ya = short_conv_mixer(gb, gc, u, conv_w[l])
        yb = pool_mixer(pu, pool_w[l], pool_scale[l])
        yc = chunk_attention(q, k, v, rel_bias[l])

        gg = group_gain[l]
        ya = rmsnorm(ya, gg[:D_CONV])
        yb = rmsnorm(yb, gg[D_CONV:D_CONV + D_POOL])
        yc = rmsnorm(yc, gg[D_CONV + D_POOL:])
        y = jnp.concatenate([ya, yb, yc], axis=-1)
        mix = jnp.einsum('bsc,cd->bsd', y, w_out[l])
        h = h + rmsnorm(mix, post_mix_g[l])

        hn = rmsnorm(h, pre_ffn_g[l])
        gu = jnp.einsum('bsd,df->bsf', hn, w_gate_up[l])
        ff = jax.nn.silu(gu[..., :D_FF]) * gu[..., D_FF:]
        ffo = jnp.einsum('bsf,fd->bsd', ff, w_down[l])
        h = h + rmsnorm(ffo, post_ffn_g[l])
    return h


import jax as _jax
import jax.numpy as _jnp

TWIN_FORMAT = 'train_step'
FWD_PARAMS = ['x', 'w_in', 'w_out', 'conv_w', 'pool_w', 'pool_scale', 'rel_bias', 'group_gain', 'pre_mix_g', 'post_mix_g', 'pre_ffn_g', 'post_ffn_g', 'w_gate_up', 'w_down']
TWIN_WEIGHTS = ['w_in', 'w_out', 'conv_w', 'pool_w', 'pool_scale', 'rel_bias', 'group_gain', 'pre_mix_g', 'post_mix_g', 'pre_ffn_g', 'post_ffn_g', 'w_gate_up', 'w_down']
TWIN_DIFF_INPUT = 'x'
TWIN_INPUTS = ['x', 'w_in', 'w_out', 'conv_w', 'pool_w', 'pool_scale', 'rel_bias', 'group_gain', 'pre_mix_g', 'post_mix_g', 'pre_ffn_g', 'post_ffn_g', 'w_gate_up', 'w_down', 'loss_target', 'm_w_in', 'm_w_out', 'm_conv_w', 'm_pool_w', 'm_pool_scale', 'm_rel_bias', 'm_group_gain', 'm_pre_mix_g', 'm_post_mix_g', 'm_pre_ffn_g', 'm_post_ffn_g', 'm_w_gate_up', 'm_w_down', 'v_w_in', 'v_w_out', 'v_conv_w', 'v_pool_w', 'v_pool_scale', 'v_rel_bias', 'v_group_gain', 'v_pre_mix_g', 'v_post_mix_g', 'v_pre_ffn_g', 'v_post_ffn_g', 'v_w_gate_up', 'v_w_down']
TWIN_OUTPUTS = ['loss', 'grad_x', 'grad_w_in', 'grad_w_out', 'grad_conv_w', 'grad_pool_w', 'grad_pool_scale', 'grad_rel_bias', 'grad_group_gain', 'grad_pre_mix_g', 'grad_post_mix_g', 'grad_pre_ffn_g', 'grad_post_ffn_g', 'grad_w_gate_up', 'grad_w_down', 'delta_w_in', 'delta_w_out', 'delta_conv_w', 'delta_pool_w', 'delta_pool_scale', 'delta_rel_bias', 'delta_group_gain', 'delta_pre_mix_g', 'delta_post_mix_g', 'delta_pre_ffn_g', 'delta_post_ffn_g', 'delta_w_gate_up', 'delta_w_down', 'new_m_w_in', 'new_m_w_out', 'new_m_conv_w', 'new_m_pool_w', 'new_m_pool_scale', 'new_m_rel_bias', 'new_m_group_gain', 'new_m_pre_mix_g', 'new_m_post_mix_g', 'new_m_pre_ffn_g', 'new_m_post_ffn_g', 'new_m_w_gate_up', 'new_m_w_down', 'new_v_w_in', 'new_v_w_out', 'new_v_conv_w', 'new_v_pool_w', 'new_v_pool_scale', 'new_v_rel_bias', 'new_v_group_gain', 'new_v_pre_mix_g', 'new_v_post_mix_g', 'new_v_pre_ffn_g', 'new_v_post_ffn_g', 'new_v_w_gate_up', 'new_v_w_down']
TWIN_LEAF_KINDS = {'loss': 'loss', 'grad_x': 'grad_x', 'grad_w_in': 'grad_w', 'grad_w_out': 'grad_w', 'grad_conv_w': 'grad_w', 'grad_pool_w': 'grad_w', 'grad_pool_scale': 'grad_w', 'grad_rel_bias': 'grad_w', 'grad_group_gain': 'grad_w', 'grad_pre_mix_g': 'grad_w', 'grad_post_mix_g': 'grad_w', 'grad_pre_ffn_g': 'grad_w', 'grad_post_ffn_g': 'grad_w', 'grad_w_gate_up': 'grad_w', 'grad_w_down': 'grad_w', 'delta_w_in': 'delta_w', 'delta_w_out': 'delta_w', 'delta_conv_w': 'delta_w', 'delta_pool_w': 'delta_w', 'delta_pool_scale': 'delta_w', 'delta_rel_bias': 'delta_w', 'delta_group_gain': 'delta_w', 'delta_pre_mix_g': 'delta_w', 'delta_post_mix_g': 'delta_w', 'delta_pre_ffn_g': 'delta_w', 'delta_post_ffn_g': 'delta_w', 'delta_w_gate_up': 'delta_w', 'delta_w_down': 'delta_w', 'new_m_w_in': 'new_m', 'new_m_w_out': 'new_m', 'new_m_conv_w': 'new_m', 'new_m_pool_w': 'new_m', 'new_m_pool_scale': 'new_m', 'new_m_rel_bias': 'new_m', 'new_m_group_gain': 'new_m', 'new_m_pre_mix_g': 'new_m', 'new_m_post_mix_g': 'new_m', 'new_m_pre_ffn_g': 'new_m', 'new_m_post_ffn_g': 'new_m', 'new_m_w_gate_up': 'new_m', 'new_m_w_down': 'new_m', 'new_v_w_in': 'new_v', 'new_v_w_out': 'new_v', 'new_v_conv_w': 'new_v', 'new_v_pool_w': 'new_v', 'new_v_pool_scale': 'new_v', 'new_v_rel_bias': 'new_v', 'new_v_group_gain': 'new_v', 'new_v_pre_mix_g': 'new_v', 'new_v_post_mix_g': 'new_v', 'new_v_pre_ffn_g': 'new_v', 'new_v_post_ffn_g': 'new_v', 'new_v_w_gate_up': 'new_v', 'new_v_w_down': 'new_v'}


def _forward(args):
    return _fwd_reference(*[args[k] for k in FWD_PARAMS])


def _output_shape():
    out = _jax.eval_shape(lambda: _forward(_fwd_setup_inputs(0)))
    return out.shape, out.dtype

N_MICROBATCH = 1
ADAM_LR = 0.001
ADAM_B1 = 0.9
ADAM_B2 = 0.999
ADAM_EPS = 1e-08
ADAM_WD = 0.01
ADAM_STEP = 10
PER_EXAMPLE_BATCH_AXIS = {'x': 0, 'loss_target': 0}
SHARED_INPUTS = []
_WEIGHT_DTYPES = {'w_in': _jnp.float32, 'w_out': _jnp.float32, 'conv_w': _jnp.float32, 'pool_w': _jnp.float32, 'pool_scale': _jnp.float32, 'rel_bias': _jnp.float32, 'group_gain': _jnp.float32, 'pre_mix_g': _jnp.float32, 'post_mix_g': _jnp.float32, 'pre_ffn_g': _jnp.float32, 'post_ffn_g': _jnp.float32, 'w_gate_up': _jnp.float32, 'w_down': _jnp.float32}
MOMENT_SCALE = {'w_in': 1.343900e+01, 'w_out': 2.279591e+01, 'conv_w': 2.175814e+00, 'pool_w': 3.023218e+00, 'pool_scale': 2.874401e+00, 'rel_bias': 1.457543e+00, 'group_gain': 2.314001e+01, 'pre_mix_g': 2.113622e+01, 'post_mix_g': 6.583633e+01, 'pre_ffn_g': 6.203339e+00, 'post_ffn_g': 6.352067e+01, 'w_gate_up': 2.659672e+00, 'w_down': 5.426057e+00}


def _to_microbatches(a, axis):
    t = _jnp.moveaxis(a, axis, 0)
    t = t.reshape((N_MICROBATCH, t.shape[0] // N_MICROBATCH) + t.shape[1:])
    return _jnp.moveaxis(t, 1, axis + 1)


def setup_inputs(seed: int = 0) -> dict:
    inp = _fwd_setup_inputs(seed)
    key = _jax.random.fold_in(_jax.random.key(seed), 7919)
    shape, _ = _output_shape()
    out = dict(inp)
    out["loss_target"] = _jax.random.normal(_jax.random.fold_in(key, 0), shape, _jnp.float32)
    for i, name in enumerate(TWIN_WEIGHTS):
        w = inp[name].astype(_jnp.float32)
        if MOMENT_SCALE is None:
            s = _jnp.sqrt(_jnp.mean(_jnp.square(w)) + 1e-30)
        else:
            s = MOMENT_SCALE[name]
        km, kv = _jax.random.split(_jax.random.fold_in(key, i + 1))
        out[name] = w
        out["m_" + name] = s * _jax.random.normal(km, w.shape, _jnp.float32)
        out["v_" + name] = (s * s) * _jax.random.uniform(kv, w.shape, _jnp.float32, 0.5, 1.5)
    if N_MICROBATCH > 1:
        for name, axis in PER_EXAMPLE_BATCH_AXIS.items():
            out[name] = _to_microbatches(out[name], axis)
    return {'x': out['x'], 'w_in': out['w_in'], 'w_out': out['w_out'], 'conv_w': out['conv_w'], 'pool_w': out['pool_w'], 'pool_scale': out['pool_scale'], 'rel_bias': out['rel_bias'], 'group_gain': out['group_gain'], 'pre_mix_g': out['pre_mix_g'], 'post_mix_g': out['post_mix_g'], 'pre_ffn_g': out['pre_ffn_g'], 'post_ffn_g': out['post_ffn_g'], 'w_gate_up': out['w_gate_up'], 'w_down': out['w_down'], 'loss_target': out['loss_target'], 'm_w_in': out['m_w_in'], 'm_w_out': out['m_w_out'], 'm_conv_w': out['m_conv_w'], 'm_pool_w': out['m_pool_w'], 'm_pool_scale': out['m_pool_scale'], 'm_rel_bias': out['m_rel_bias'], 'm_group_gain': out['m_group_gain'], 'm_pre_mix_g': out['m_pre_mix_g'], 'm_post_mix_g': out['m_post_mix_g'], 'm_pre_ffn_g': out['m_pre_ffn_g'], 'm_post_ffn_g': out['m_post_ffn_g'], 'm_w_gate_up': out['m_w_gate_up'], 'm_w_down': out['m_w_down'], 'v_w_in': out['v_w_in'], 'v_w_out': out['v_w_out'], 'v_conv_w': out['v_conv_w'], 'v_pool_w': out['v_pool_w'], 'v_pool_scale': out['v_pool_scale'], 'v_rel_bias': out['v_rel_bias'], 'v_group_gain': out['v_group_gain'], 'v_pre_mix_g': out['v_pre_mix_g'], 'v_post_mix_g': out['v_post_mix_g'], 'v_pre_ffn_g': out['v_pre_ffn_g'], 'v_post_ffn_g': out['v_post_ffn_g'], 'v_w_gate_up': out['v_w_gate_up'], 'v_w_down': out['v_w_down']}


def _loss(weights, diff, rest, loss_target):
    with _jax.named_scope("forward"):
        args = {**rest, TWIN_DIFF_INPUT: diff, **{k: w.astype(_WEIGHT_DTYPES[k]) for k, w in weights.items()}}
        y = _forward(args)
    with _jax.named_scope("loss_head"):
        err = _jnp.square(y.astype(_jnp.float32) - loss_target)
        return 0.5 * _jnp.sum(_jnp.mean(err, axis=-1)) if err.ndim else 0.5 * err


def _adamw(w, g, m, v):
    m = ADAM_B1 * m + (1.0 - ADAM_B1) * g
    v = ADAM_B2 * v + (1.0 - ADAM_B2) * _jnp.square(g)
    m_hat = m / (1.0 - ADAM_B1 ** ADAM_STEP)
    v_hat = v / (1.0 - ADAM_B2 ** ADAM_STEP)
    delta = -ADAM_LR * (m_hat / (_jnp.sqrt(v_hat) + ADAM_EPS) + ADAM_WD * w)
    return delta, m, v


def reference(x, w_in, w_out, conv_w, pool_w, pool_scale, rel_bias, group_gain, pre_mix_g, post_mix_g, pre_ffn_g, post_ffn_g, w_gate_up, w_down, loss_target, m_w_in, m_w_out, m_conv_w, m_pool_w, m_pool_scale, m_rel_bias, m_group_gain, m_pre_mix_g, m_post_mix_g, m_pre_ffn_g, m_post_ffn_g, m_w_gate_up, m_w_down, v_w_in, v_w_out, v_conv_w, v_pool_w, v_pool_scale, v_rel_bias, v_group_gain, v_pre_mix_g, v_post_mix_g, v_pre_ffn_g, v_post_ffn_g, v_w_gate_up, v_w_down):
    given = dict(x=x, w_in=w_in, w_out=w_out, conv_w=conv_w, pool_w=pool_w, pool_scale=pool_scale, rel_bias=rel_bias, group_gain=group_gain, pre_mix_g=pre_mix_g, post_mix_g=post_mix_g, pre_ffn_g=pre_ffn_g, post_ffn_g=post_ffn_g, w_gate_up=w_gate_up, w_down=w_down, loss_target=loss_target, m_w_in=m_w_in, m_w_out=m_w_out, m_conv_w=m_conv_w, m_pool_w=m_pool_w, m_pool_scale=m_pool_scale, m_rel_bias=m_rel_bias, m_group_gain=m_group_gain, m_pre_mix_g=m_pre_mix_g, m_post_mix_g=m_post_mix_g, m_pre_ffn_g=m_pre_ffn_g, m_post_ffn_g=m_post_ffn_g, m_w_gate_up=m_w_gate_up, m_w_down=m_w_down, v_w_in=v_w_in, v_w_out=v_w_out, v_conv_w=v_conv_w, v_pool_w=v_pool_w, v_pool_scale=v_pool_scale, v_rel_bias=v_rel_bias, v_group_gain=v_group_gain, v_pre_mix_g=v_pre_mix_g, v_post_mix_g=v_post_mix_g, v_pre_ffn_g=v_pre_ffn_g, v_post_ffn_g=v_post_ffn_g, v_w_gate_up=v_w_gate_up, v_w_down=v_w_down)
    weights = {n: given[n] for n in TWIN_WEIGHTS}
    shared = {n: given[n] for n in SHARED_INPUTS}
    per_example = {n: given[n] for n in ['x']}
    grad_fn = _jax.value_and_grad(_loss, argnums=(0, 1))

    def one_microbatch(ex, loss_target):
        ex = dict(ex)
        diff = ex.pop(TWIN_DIFF_INPUT)
        return grad_fn(weights, diff, {**shared, **ex}, loss_target)

    if N_MICROBATCH == 1:
        loss, (grad_w, grad_x) = one_microbatch(per_example, given["loss_target"])
    else:
        def body(carry, xs):
            loss_sum, grad_sum = carry
            l_k, (gw_k, gx_k) = one_microbatch(xs[0], xs[1])
            with _jax.named_scope("update"):
                return (loss_sum + l_k, _jax.tree.map(_jnp.add, grad_sum, gw_k)), gx_k

        init = (_jnp.zeros((), _jnp.float32), _jax.tree.map(_jnp.zeros_like, weights))
        (loss, grad_w), grad_x = _jax.lax.scan(body, init, (per_example, given["loss_target"]))
    with _jax.named_scope("update"):
        delta_w, new_m, new_v = {}, {}, {}
        for n in TWIN_WEIGHTS:
            delta_w[n], new_m[n], new_v[n] = _adamw(weights[n], grad_w[n], given["m_" + n], given["v_" + n])
    return (loss, grad_x, *[grad_w[n] for n in TWIN_WEIGHTS], *[delta_w[n] for n in TWIN_WEIGHTS],
            *[new_m[n] for n in TWIN_WEIGHTS], *[new_v[n] for n in TWIN_WEIGHTS])
```

```python
import functools

import numpy as np
import jax
import jax.numpy as jnp
from jax import lax
from jax.experimental import pallas as pl
from jax.experimental.pallas import tpu as pltpu

F32 = jnp.float32
BF16 = jnp.bfloat16

CHUNK = 64
D_MODEL = 1024
D_CONV = 256
D_POOL = 256
D_ATTN = 512
HEAD_DIM = 64
N_HEADS = 8
N_POOL = 4
POOL_GC = 64
POOL_WINDOWS = (2, 4, 8, 16)
LEFT_CHUNKS = 8
REL_CLIP = 128
D_FF = 2816
EPS = 1e-6
ADAM_LR, ADAM_B1, ADAM_B2, ADAM_EPS, ADAM_WD, ADAM_STEP = 0.001, 0.9, 0.999, 1e-08, 0.01, 10

QB = 256
KB = 3 * QB
HALO = 16
T_CP = 512
T_ROW = 512
NEG = -1e30
VMEM_MB = 1 << 20
MESH = pl.DeviceIdType.MESH


def _call(body, **kw):
    return pl.pallas_call(body, **kw)


def _params(sem, vmem_mb=48):
    return pltpu.CompilerParams(dimension_semantics=sem, vmem_limit_bytes=vmem_mb * VMEM_MB)


def _coords():
    return lax.axis_index("x"), lax.axis_index("y"), lax.axis_index("c")


_CHIP_FLIPS = ((1, 0), (0, 1), (1, 1))


def _flip(v, f):
    return 1 - v if f else v


def _exchange(name, arrays, out_shapes, plan, n_remote, n_local):
    n = len(arrays)

    def body(*refs):
        ins, outs = refs[:n], refs[n:2 * n]
        send_sems, recv_sems, local_sems = refs[2 * n:]
        x, y, c = _coords()
        local, remote = plan(ins, outs, x, y, c)
        assert len(local) == n_local and len(remote) == n_remote
        copies = [pltpu.make_async_copy(s, d, local_sems.at[j]) for j, (s, d) in enumerate(local)]
        sends = [pltpu.make_async_remote_copy(src_ref=s, dst_ref=d, send_sem=send_sems.at[k], recv_sem=recv_sems.at[k],
                                              device_id=peer, device_id_type=MESH)
                 for k, (s, d, peer, _) in enumerate(remote)]
        lands = [pltpu.make_async_remote_copy(src_ref=s, dst_ref=land, send_sem=send_sems.at[k], recv_sem=recv_sems.at[k],
                                              device_id=peer, device_id_type=MESH)
                 for k, (s, _, peer, land) in enumerate(remote)]
        for cp in copies:
            cp.start()
        for cp in sends:
            cp.start()
        for cp in lands:
            cp.wait_recv()
        for cp in sends:
            cp.wait_send()
        for cp in copies:
            cp.wait()

    any_spec = pl.BlockSpec(memory_space=pl.ANY)
    return _call(
        body, name=name,
        out_shape=[jax.ShapeDtypeStruct(s, a.dtype) for s, a in zip(out_shapes, arrays)],
        in_specs=[any_spec] * n, out_specs=[any_spec] * n,
        scratch_shapes=[pltpu.SemaphoreType.DMA((n_remote,)), pltpu.SemaphoreType.DMA((n_remote,)),
                        pltpu.SemaphoreType.DMA((max(n_local, 1),))],
    )(*arrays)


def gather_chips(name, arrays):
    def plan(ins, outs, x, y, c):
        q = 2 * x + y
        local, remote = [], []
        for a_in, a_out in zip(ins, outs):
            local.append((a_in, a_out.at[q]))
            for fx, fy in _CHIP_FLIPS:
                px, py = _flip(x, fx), _flip(y, fy)
                remote.append((a_in, a_out.at[q], (px, py, c), a_out.at[2 * px + py]))
        return local, remote
    n = len(arrays)
    return _exchange(name, arrays, [(4,) + a.shape for a in arrays], plan, 3 * n, n)


def gather_cores(name, arrays):
    def plan(ins, outs, x, y, c):
        local, remote = [], []
        for a_in, a_out in zip(ins, outs):
            local.append((a_in, a_out.at[c]))
            remote.append((a_in, a_out.at[c], (x, y, 1 - c), a_out.at[1 - c]))
        return local, remote
    n = len(arrays)
    return _exchange(name, arrays, [(2,) + a.shape for a in arrays], plan, n, n)


def swap_cores(name, arrays):
    def plan(ins, outs, x, y, c):
        return [], [(a_in, a_out, (x, y, 1 - c), a_out) for a_in, a_out in zip(ins, outs)]
    n = len(arrays)
    return _exchange(name, arrays, [a.shape for a in arrays], plan, n, 0)


def scatter_chips(name, arrays):
    def plan(ins, outs, x, y, c):
        q = 2 * x + y
        local, remote = [], []
        for a_in, a_out in zip(ins, outs):
            local.append((a_in.at[q], a_out.at[q]))
            for fx, fy in _CHIP_FLIPS:
                px, py = _flip(x, fx), _flip(y, fy)
                qp = 2 * px + py
                remote.append((a_in.at[qp], a_out.at[q], (px, py, c), a_out.at[qp]))
        return local, remote
    n = len(arrays)
    return _exchange(name, arrays, [a.shape for a in arrays], plan, 3 * n, n)


def all_gather(name, arrays):
    return gather_cores(name + "_cores", gather_chips(name + "_chips", arrays))


def _rms(v):
    return lax.rsqrt(jnp.mean(v * v, axis=-1, keepdims=True) + EPS)


def rmsnorm_fwd(name, x, g, res, out_dtype):
    s, d = x.shape
    has_res = res is not None

    def body(*refs):
        if has_res:
            x_ref, g_ref, r_ref, o_ref = refs
        else:
            x_ref, g_ref, o_ref = refs
        xv = x_ref[...]
        y = (xv * _rms(xv)) * g_ref[...]
        if has_res:
            y = y + r_ref[...]
        o_ref[...] = y.astype(o_ref.dtype)

    row = pl.BlockSpec((T_ROW, d), lambda i: (i, 0))
    vec = pl.BlockSpec((1, d), lambda i: (0, 0))
    return _call(
        body, name=name, grid=(s // T_ROW,),
        in_specs=[row, vec] + ([row] if has_res else []), out_specs=row,
        out_shape=jax.ShapeDtypeStruct((s, d), out_dtype),
        compiler_params=_params(("parallel",)),
    )(*((x, g, res) if has_res else (x, g)))


def _norm_bwd(xv, r, g, dy):
    a = dy * g
    dx = r * (a - xv * ((r * r) * jnp.mean(a * xv, axis=-1, keepdims=True)))
    return dx, dy * (xv * r)


def rmsnorm_bwd(name, x, g, dy, res, out_dtype):
    s, d = x.shape
    has_res = res is not None

    def body(*refs):
        if has_res:
            x_ref, g_ref, dy_ref, r_ref, dx_ref, dg_ref = refs
        else:
            x_ref, g_ref, dy_ref, dx_ref, dg_ref = refs
        xv = x_ref[...]
        dx, dgt = _norm_bwd(xv, _rms(xv), g_ref[...], dy_ref[...].astype(F32))
        if has_res:
            dx = dx + r_ref[...]
        dx_ref[...] = dx.astype(dx_ref.dtype)

        @pl.when(pl.program_id(0) == 0)
        def _():
            dg_ref[...] = jnp.zeros_like(dg_ref)
        dg_ref[...] += jnp.sum(dgt, axis=0, keepdims=True)

    row = pl.BlockSpec((T_ROW, d), lambda i: (i, 0))
    vec = pl.BlockSpec((1, d), lambda i: (0, 0))
    return _call(
        body, name=name, grid=(s // T_ROW,),
        in_specs=[row, vec, row] + ([row] if has_res else []), out_specs=[row, vec],
        out_shape=[jax.ShapeDtypeStruct((s, d), out_dtype), jax.ShapeDtypeStruct((1, d), F32)],
        compiler_params=_params(("arbitrary",)),
    )(*((x, g, dy, res) if has_res else (x, g, dy)))


def loss_head(name, h, tgt):
    s, d = h.shape

    def body(h_ref, t_ref, dh_ref, l_ref):
        e = h_ref[...] - t_ref[...]
        dh_ref[...] = e * (1.0 / d)

        @pl.when(pl.program_id(0) == 0)
        def _():
            l_ref[...] = jnp.zeros_like(l_ref)
        part = 0.5 * jnp.sum(jnp.mean(e * e, axis=-1, keepdims=True), axis=0, keepdims=True)
        l_ref[...] += jnp.broadcast_to(part, l_ref.shape)

    row = pl.BlockSpec((T_ROW, d), lambda i: (i, 0))
    return _call(
        body, name=name, grid=(s // T_ROW,),
        in_specs=[row, row], out_specs=[row, pl.BlockSpec((1, 128), lambda i: (0, 0))],
        out_shape=[jax.ShapeDtypeStruct((s, d), F32), jax.ShapeDtypeStruct((1, 128), F32)],
        compiler_params=_params(("arbitrary",)),
    )(h, tgt)


def _row_tile(rows, limit=512):
    t = min(rows, limit)
    while rows % t or (t % 8 and t != rows):
        t -= 1
    return t


def add_pairs(name, a, b):
    shape = a.shape
    a2, b2 = a.reshape(-1, shape[-1]), b.reshape(-1, shape[-1])
    rows, cols = a2.shape
    t = _row_tile(rows)

    def body(a_ref, b_ref, o_ref):
        o_ref[...] = (a_ref[...].astype(F32) + b_ref[...].astype(F32)).astype(o_ref.dtype)

    blk = pl.BlockSpec((t, cols), lambda i: (i, 0))
    out = _call(
        body, name=name, grid=(rows // t,), in_specs=[blk, blk], out_specs=blk,
        out_shape=jax.ShapeDtypeStruct((rows, cols), a.dtype), compiler_params=_params(("parallel",)),
    )(a2, b2)
    return out.reshape(shape)


def adamw(name, w, m, v, g_slabs):
    shape = w.shape
    n = g_slabs.shape[0]
    w2, m2, v2 = (t.reshape(-1, shape[-1]) for t in (w, m, v))
    g3 = g_slabs.reshape(n, -1, shape[-1])
    rows, cols = w2.shape
    t = _row_tile(rows, 256)
    c1 = 1.0 - ADAM_B1 ** ADAM_STEP
    c2 = 1.0 - ADAM_B2 ** ADAM_STEP

    def body(w_ref, m_ref, v_ref, g_ref, go_ref, d_ref, mo_ref, vo_ref):
        g = g_ref[0].astype(F32)
        for j in range(1, n):
            g = g + g_ref[j].astype(F32)
        mn = ADAM_B1 * m_ref[...] + (1.0 - ADAM_B1) * g
        vn = ADAM_B2 * v_ref[...] + (1.0 - ADAM_B2) * (g * g)
        m_hat = mn / c1
        v_hat = vn / c2
        go_ref[...] = g
        d_ref[...] = -ADAM_LR * (m_hat / (jnp.sqrt(v_hat) + ADAM_EPS) + ADAM_WD * w_ref[...])
        mo_ref[...] = mn
        vo_ref[...] = vn

    blk = pl.BlockSpec((t, cols), lambda i: (i, 0))
    outs = _call(
        body, name=name, grid=(rows // t,),
        in_specs=[blk, blk, blk, pl.BlockSpec((n, t, cols), lambda i: (0, i, 0))], out_specs=[blk] * 4,
        out_shape=[jax.ShapeDtypeStruct((rows, cols), F32)] * 4, compiler_params=_params(("parallel",)),
    )(w2, m2, v2, g3)
    return tuple(o.reshape(shape) for o in outs)


_DIMS = {"nn": (((1,), (0,)), ((), ())), "nt": (((1,), (1,)), ((), ())), "tn": (((0,), (0,)), ((), ()))}


def _dot(a, b, mode="nn"):
    return lax.dot_general(a, b, _DIMS[mode], preferred_element_type=F32)


def matmul(name, a, b, mode, out_dtype, m, n, k, tm, tn, tk, b_off=(0, 0), acc_in=None):
    gm, gn, gk = m // tm, n // tn, k // tk
    assert gm * tm == m and gn * tn == n and gk * tk == k
    r0, c0 = b_off
    if mode == "nn":
        a_spec = pl.BlockSpec((tm, tk), lambda i, j, kk: (i, kk))
        b_spec = pl.BlockSpec((tk, tn), lambda i, j, kk: (kk + r0, j + c0))
    elif mode == "nt":
        a_spec = pl.BlockSpec((tm, tk), lambda i, j, kk: (i, kk))
        b_spec = pl.BlockSpec((tn, tk), lambda i, j, kk: (j + r0, kk + c0))
    else:
        a_spec = pl.BlockSpec((tk, tm), lambda i, j, kk: (kk, i))
        b_spec = pl.BlockSpec((tk, tn), lambda i, j, kk: (kk + r0, j + c0))
    o_spec = pl.BlockSpec((tm, tn), lambda i, j, kk: (i, j))
    has_acc = acc_in is not None

    def body(*refs):
        if has_acc:
            a_ref, b_ref, c_ref, o_ref = refs[:4]
        else:
            a_ref, b_ref, o_ref = refs[:3]
        part = _dot(a_ref[...], b_ref[...], mode)
        if gk == 1:
            if has_acc:
                part = part + c_ref[...]
            o_ref[...] = part.astype(o_ref.dtype)
            return
        acc_ref = refs[-1]
        kk = pl.program_id(2)

        @pl.when(kk == 0)
        def _():
            acc_ref[...] = part + c_ref[...] if has_acc else part

        @pl.when(kk > 0)
        def _():
            acc_ref[...] += part

        @pl.when(kk == gk - 1)
        def _():
            o_ref[...] = acc_ref[...].astype(o_ref.dtype)

    return _call(
        body, name=name, grid=(gm, gn, gk),
        in_specs=[a_spec, b_spec] + ([o_spec] if has_acc else []), out_specs=o_spec,
        out_shape=jax.ShapeDtypeStruct((m, n), out_dtype),
        scratch_shapes=[pltpu.VMEM((tm, tn), F32)] if gk > 1 else [],
        compiler_params=_params(("parallel", "parallel", "arbitrary")),
    )(*((a, b, acc_in) if has_acc else (a, b)))


def matmul_swiglu(name, a, w_gu):
    s, d = a.shape
    tm, tn = 512, 1408
    gn = D_FF // tn

    def body(a_ref, wg_ref, wu_ref, g_ref, u_ref, f_ref):
        av = a_ref[...]
        g = _dot(av, wg_ref[...])
        u = _dot(av, wu_ref[...])
        g_ref[...] = g.astype(g_ref.dtype)
        u_ref[...] = u.astype(u_ref.dtype)
        f_ref[...] = ((g * (1.0 / (1.0 + jnp.exp(-g)))) * u).astype(f_ref.dtype)

    o_spec = pl.BlockSpec((tm, tn), lambda i, j: (i, j))
    return _call(
        body, name=name, grid=(s // tm, gn),
        in_specs=[pl.BlockSpec((tm, d), lambda i, j: (i, 0)), pl.BlockSpec((d, tn), lambda i, j: (0, j)),
                  pl.BlockSpec((d, tn), lambda i, j: (0, j + gn))],
        out_specs=[o_spec] * 3, out_shape=[jax.ShapeDtypeStruct((s, D_FF), BF16)] * 3,
        compiler_params=_params(("parallel", "parallel")),
    )(a, w_gu, w_gu)


def matmul_swiglu_bwd(name, dffo, w_down, g, u):
    s, d = dffo.shape
    tm, tn = 512, 1408

    def body(a_ref, w_ref, g_ref, u_ref, dg_ref, du_ref):
        dff = _dot(a_ref[...], w_ref[...], "nt")
        gv = g_ref[...].astype(F32)
        sig = 1.0 / (1.0 + jnp.exp(-gv))
        du_ref[...] = (dff * (gv * sig)).astype(du_ref.dtype)
        dg_ref[...] = (dff * u_ref[...].astype(F32) * (sig * (1.0 + gv * (1.0 - sig)))).astype(dg_ref.dtype)

    o_spec = pl.BlockSpec((tm, tn), lambda i, j: (i, j))
    return _call(
        body, name=name, grid=(s // tm, D_FF // tn),
        in_specs=[pl.BlockSpec((tm, d), lambda i, j: (i, 0)), pl.BlockSpec((tn, d), lambda i, j: (j, 0)), o_spec, o_spec],
        out_specs=[o_spec] * 2, out_shape=[jax.ShapeDtypeStruct((s, D_FF), BF16)] * 2,
        compiler_params=_params(("parallel", "parallel")),
    )(dffo, w_down, g, u)


def _down(v, n):
    return pltpu.roll(v, n, 0)


def _up(v, n):
    return pltpu.roll(v, v.shape[0] - n, 0)


def _by_window(lane, v2, v4, v8, v16):
    return jnp.where(lane < POOL_GC, v2, jnp.where(lane < 2 * POOL_GC, v4, jnp.where(lane < 3 * POOL_GC, v8, v16)))


def _taps(cw_ref):
    return cw_ref[0:1, :], cw_ref[1:2, :], cw_ref[2:3, :]


def _conv_pool_forward(ext, t0, cw, bd):
    n_out = ext.shape[0] - HALO
    gb = ext[HALO:, 0:D_CONV]
    z = ext[:, D_CONV:2 * D_CONV] * ext[:, 2 * D_CONV:3 * D_CONV]
    z0, z1, z2 = z[HALO:], _down(z, 1)[HALO:], _down(z, 2)[HALO:]
    conv = cw[2] * z0 + cw[1] * z1 + cw[0] * z2
    x = ext[:, 3 * D_CONV:]
    w2 = x + _down(x, 1)
    w4 = w2 + _down(w2, 2)
    w8 = w4 + _down(w4, 4)
    w16 = w8 + _down(w8, 8)
    lane = lax.broadcasted_iota(jnp.int32, (1, D_POOL), 1)
    win = _by_window(lane, 2.0, 4.0, 8.0, 16.0)
    pos = (t0 + lax.broadcasted_iota(jnp.int32, (n_out, 1), 0) + 1).astype(F32)
    cnt = jnp.minimum(pos, win)
    d = _by_window(lane, w2, w4, w8, w16)[HALO:] / cnt - x[HALO:]
    ybp = _dot(d.astype(BF16), bd)
    return gb, z0, z1, z2, conv, d, cnt, ybp, lane


def conv_pool_fwd(name, proj_a, cw, bd, ps, gg):
    s = proj_a.shape[0]
    t = T_CP
    hb = t // HALO

    def body(main_ref, prev_ref, cw_ref, bd_ref, ps_ref, gg_ref, y_ref):
        i = pl.program_id(0)
        prev = jnp.where(i > 0, prev_ref[...], 0.0)
        ext = jnp.concatenate([prev, main_ref[...]], axis=0)
        gb, _, _, _, conv, _, _, ybp, _ = _conv_pool_forward(ext, i * t, _taps(cw_ref), bd_ref[...])
        ya = gb * conv
        yb = ybp * ps_ref[...]
        ggv = gg_ref[...]
        y_ref[:, 0:D_CONV] = ((ya * _rms(ya)) * ggv[:, 0:D_CONV]).astype(y_ref.dtype)
        y_ref[:, D_CONV:] = ((yb * _rms(yb)) * ggv[:, D_CONV:]).astype(y_ref.dtype)

    full = lambda shape: pl.BlockSpec(shape, lambda i: (0,) * len(shape))
    return _call(
        body, name=name, grid=(s // t,),
        in_specs=[pl.BlockSpec((t, D_MODEL), lambda i: (i, 0)),
                  pl.BlockSpec((HALO, D_MODEL), lambda i: (jnp.maximum(i * hb - 1, 0), 0)),
                  full((3, D_CONV)), full((D_POOL, D_POOL)), full((1, D_POOL)), full((1, 2 * D_CONV))],
        out_specs=pl.BlockSpec((t, 2 * D_CONV), lambda i: (i, 0)),
        out_shape=jax.ShapeDtypeStruct((s, D_MODEL), BF16),
        compiler_params=_params(("parallel",)),
    )(proj_a, proj_a, cw, bd, ps, gg)


def conv_pool_bwd(name, proj_a, dy, cw, bd, ps, gg):
    s = proj_a.shape[0]
    t = T_CP
    hb = t // HALO
    nblk = s // t
    last_halo = s // HALO - 1

    def body(main_ref, prev_ref, next_ref, dy_ref, dyn_ref, cw_ref, bd_ref, ps_ref, gg_ref,
             dp_ref, dcw_ref, dbd_ref, dps_ref, dgg_ref):
        i = pl.program_id(0)
        prev = jnp.where(i > 0, prev_ref[...], 0.0)
        ext = jnp.concatenate([prev, main_ref[...], next_ref[...]], axis=0)
        cwv, bdv, psv, ggv = _taps(cw_ref), bd_ref[...], ps_ref[...], gg_ref[...]
        gb, z0, z1, z2, conv, d, cnt, ybp, lane = _conv_pool_forward(ext, i * t, cwv, bdv)
        dyn = jnp.where(i < nblk - 1, dyn_ref[...].astype(F32), 0.0)
        dyv = jnp.concatenate([dy_ref[...].astype(F32), dyn], axis=0)
        ya = gb * conv
        yb = ybp * psv
        dya, dgg_a = _norm_bwd(ya, _rms(ya), ggv[:, 0:D_CONV], dyv[:, 0:D_CONV])
        dyb, dgg_b = _norm_bwd(yb, _rms(yb), ggv[:, D_CONV:], dyv[:, D_CONV:])

        dconv = dya * gb
        dz = (cwv[2] * dconv + cwv[1] * _up(dconv, 1) + cwv[0] * _up(dconv, 2))[:t]
        main = main_ref[...]
        dp_ref[:, 0:D_CONV] = (dya * conv)[:t].astype(dp_ref.dtype)
        dp_ref[:, D_CONV:2 * D_CONV] = (dz * main[:, 2 * D_CONV:3 * D_CONV]).astype(dp_ref.dtype)
        dp_ref[:, 2 * D_CONV:3 * D_CONV] = (dz * main[:, D_CONV:2 * D_CONV]).astype(dp_ref.dtype)

        dybs = dyb * psv
        dd = _dot(dybs.astype(BF16), bdv, "nt")
        e = dd / cnt
        a2 = e + _up(e, 1)
        a4 = a2 + _up(a2, 2)
        a8 = a4 + _up(a4, 4)
        a16 = a8 + _up(a8, 8)
        dp_ref[:, 3 * D_CONV:] = (_by_window(lane, a2, a4, a8, a16) - dd)[:t].astype(dp_ref.dtype)

        @pl.when(i == 0)
        def _():
            dcw_ref[...] = jnp.zeros_like(dcw_ref)
            dbd_ref[...] = jnp.zeros_like(dbd_ref)
            dps_ref[...] = jnp.zeros_like(dps_ref)
            dgg_ref[...] = jnp.zeros_like(dgg_ref)

        rsum = lambda v: jnp.sum(v[:t], axis=0, keepdims=True)
        dcw_ref[0:1, :] += rsum(dconv * z2)
        dcw_ref[1:2, :] += rsum(dconv * z1)
        dcw_ref[2:3, :] += rsum(dconv * z0)
        dbd_ref[...] += _dot(d[:t].astype(BF16), dybs[:t].astype(BF16), "tn")
        dps_ref[...] += rsum(dyb * ybp)
        dgg_ref[:, 0:D_CONV] += rsum(dgg_a)
        dgg_ref[:, D_CONV:] += rsum(dgg_b)

    full = lambda shape: pl.BlockSpec(shape, lambda i: (0,) * len(shape))
    next_halo = lambda i: (jnp.minimum((i + 1) * hb, last_halo), 0)
    return _call(
        body, name=name, grid=(nblk,),
        in_specs=[pl.BlockSpec((t, D_MODEL), lambda i: (i, 0)),
                  pl.BlockSpec((HALO, D_MODEL), lambda i: (jnp.maximum(i * hb - 1, 0), 0)),
                  pl.BlockSpec((HALO, D_MODEL), next_halo),
                  pl.BlockSpec((t, 2 * D_CONV), lambda i: (i, 0)),
                  pl.BlockSpec((HALO, 2 * D_CONV), next_halo),
                  full((3, D_CONV)), full((D_POOL, D_POOL)), full((1, D_POOL)), full((1, 2 * D_CONV))],
        out_specs=[pl.BlockSpec((t, D_MODEL), lambda i: (i, 0)),
                   full((3, D_CONV)), full((D_POOL, D_POOL)), full((1, D_POOL)), full((1, 2 * D_CONV))],
        out_shape=[jax.ShapeDtypeStruct((s, D_MODEL), BF16), jax.ShapeDtypeStruct((3, D_CONV), F32),
                   jax.ShapeDtypeStruct((D_POOL, D_POOL), F32), jax.ShapeDtypeStruct((1, D_POOL), F32),
                   jax.ShapeDtypeStruct((1, 2 * D_CONV), F32)],
        compiler_params=_params(("arbitrary",)),
    )(proj_a, proj_a, proj_a, dy, dy, cw, bd, ps, gg)


def _band_mask():
    qa = np.arange(QB)[:, None] // CHUNK
    km = np.arange(KB)[None, :] // CHUNK
    return (km - qa >= 0) & (km - qa <= LEFT_CHUNKS)


def attention_bias(rel_bias):
    n_far = 2 * QB - REL_CLIP + 1
    far = jnp.broadcast_to(rel_bias[:, 2 * REL_CLIP:], (N_HEADS, n_far))
    mid = rel_bias[:, 1:2 * REL_CLIP][:, ::-1]
    near = jnp.broadcast_to(rel_bias[:, 0:1], (N_HEADS, KB - n_far - (2 * REL_CLIP - 1)))
    wrap = jnp.broadcast_to(rel_bias[:, 2 * REL_CLIP:], (N_HEADS, 4 * QB - KB))
    by_offset = jnp.concatenate([far, mid, near, wrap], axis=1)
    width = 4 * QB
    toeplitz = jnp.tile(by_offset, (1, QB))[:, :QB * (width - 1)].reshape(N_HEADS, QB, width - 1)[:, :, :KB]
    return jnp.where(jnp.asarray(_band_mask())[None], toeplitz, NEG)


def rel_bias_grad(by_offset):
    n_far = 2 * QB - REL_CLIP + 1
    hi = jnp.sum(by_offset[:, :n_far], axis=1, keepdims=True) + jnp.sum(by_offset[:, KB:], axis=1, keepdims=True)
    mid = by_offset[:, n_far:n_far + 2 * REL_CLIP - 1][:, ::-1]
    lo = jnp.sum(by_offset[:, n_far + 2 * REL_CLIP - 1:KB], axis=1, keepdims=True)
    return jnp.concatenate([lo, mid, hi], axis=1)


def _head_masks():
    first = lax.broadcasted_iota(jnp.int32, (1, 2 * HEAD_DIM), 1) < HEAD_DIM
    return first, jnp.logical_not(first)


def _pick_lane(tile, h):
    lane = lax.broadcasted_iota(jnp.int32, (1, tile.shape[1]), 1)
    return jnp.sum(jnp.where(lane == h, tile, 0.0), axis=-1, keepdims=True)


def attention_fwd(name, qkv, bias, gg, y_ab):
    s = qkv.shape[0]
    nq = s // QB
    scale = HEAD_DIM ** -0.5

    def body(q_ref, k0, k1, k2, v0, v1, v2, b_ref, gg_ref, y_in, y_ref, o_ref, lse_ref):
        del y_in
        i = pl.program_id(0)
        kb = jnp.concatenate([k0[...], k1[...], k2[...]], axis=0)
        vb = jnp.concatenate([v0[...], v1[...], v2[...]], axis=0)
        valid = lax.broadcasted_iota(jnp.int32, (1, KB), 1) >= (2 - i) * QB
        lane = lax.broadcasted_iota(jnp.int32, (1, 128), 1)
        masks = _head_masks()
        lse = jnp.zeros((QB, 128), F32)
        outs = []
        for hp in range(N_HEADS // 2):
            sl = slice(2 * HEAD_DIM * hp, 2 * HEAD_DIM * (hp + 1))
            q_p, k_p, v_p = q_ref[:, sl], kb[:, sl], vb[:, sl]
            o_pair = jnp.zeros((QB, 2 * HEAD_DIM), F32)
            for a in range(2):
                h = 2 * hp + a
                sc = _dot(jnp.where(masks[a], q_p, 0), k_p, "nt") * scale + b_ref[h]
                sc = jnp.where(valid, sc, NEG)
                mx = jnp.max(sc, axis=-1, keepdims=True)
                e = jnp.exp(sc - mx)
                l = jnp.sum(e, axis=-1, keepdims=True)
                o_pair = o_pair + _dot(e.astype(BF16), jnp.where(masks[a], v_p, 0)) * (1.0 / l)
                lse = jnp.where(lane == h, mx + jnp.log(l), lse)
            outs.append(o_pair)
        o = jnp.concatenate(outs, axis=1)
        o_ref[...] = o.astype(o_ref.dtype)
        lse_ref[...] = lse
        y_ref[...] = ((o * _rms(o)) * gg_ref[...]).astype(y_ref.dtype)

    blk = lambda col, back: pl.BlockSpec((QB, D_ATTN), lambda i: (jnp.maximum(i - back, 0), col))
    return _call(
        body, name=name, grid=(nq,),
        in_specs=[blk(0, 0), blk(1, 2), blk(1, 1), blk(1, 0), blk(2, 2), blk(2, 1), blk(2, 0),
                  pl.BlockSpec((N_HEADS, QB, KB), lambda i: (0, 0, 0)), pl.BlockSpec((1, D_ATTN), lambda i: (0, 0)),
                  pl.BlockSpec(memory_space=pl.ANY)],
        out_specs=[pl.BlockSpec((QB, D_ATTN), lambda i: (i, 1)), pl.BlockSpec((QB, D_ATTN), lambda i: (i, 0)),
                   pl.BlockSpec((QB, 128), lambda i: (i, 0))],
        out_shape=[jax.ShapeDtypeStruct((s, D_MODEL), BF16), jax.ShapeDtypeStruct((s, D_ATTN), BF16),
                   jax.ShapeDtypeStruct((s, 128), F32)],
        input_output_aliases={9: 0},
        compiler_params=_params(("arbitrary",), 56),
    )(qkv, qkv, qkv, qkv, qkv, qkv, qkv, bias, gg, y_ab)


def attention_bwd(name, qkv, o, lse, dy, bias, gg):
    s = qkv.shape[0]
    nq = s // QB
    scale = HEAD_DIM ** -0.5
    width = 4 * QB

    def body(q_ref, k0, k1, k2, v0, v1, v2, o_ref, lse_ref, dy_ref, b_ref, gg_ref,
             dq_ref, dk_ref, dv_ref, off_ref, dgg_ref, dk_acc, dv_acc, db_acc):
        i = pl.program_id(0)

        @pl.when(i == 0)
        def _():
            dk_acc[...] = jnp.zeros_like(dk_acc)
            dv_acc[...] = jnp.zeros_like(dv_acc)
            db_acc[...] = jnp.zeros_like(db_acc)
            dgg_ref[...] = jnp.zeros_like(dgg_ref)

        @pl.when(i > 0)
        def _():
            for acc in (dk_acc, dv_acc):
                kept = acc[QB:, :]
                acc[0:2 * QB, :] = kept
                acc[2 * QB:, :] = jnp.zeros((QB, D_ATTN), F32)

        @pl.when(i < nq)
        def _():
            ov = o_ref[...].astype(F32)
            dyv = dy_ref[...].astype(F32)
            do, dgg_t = _norm_bwd(ov, _rms(ov), gg_ref[...], dyv)
            dgg_ref[...] += jnp.sum(dgg_t, axis=0, keepdims=True)
            kb = jnp.concatenate([k0[...], k1[...], k2[...]], axis=0)
            vb = jnp.concatenate([v0[...], v1[...], v2[...]], axis=0)
            valid = lax.broadcasted_iota(jnp.int32, (1, KB), 1) >= (2 - i) * QB
            masks = _head_masks()
            lse_t = lse_ref[...]
            for hp in range(N_HEADS // 2):
                sl = slice(2 * HEAD_DIM * hp, 2 * HEAD_DIM * (hp + 1))
                q_p, k_p, v_p = q_ref[:, sl], kb[:, sl], vb[:, sl]
                do_p = do[:, sl]
                prod = do_p * ov[:, sl]
                do_b = do_p.astype(BF16)
                dq_pair = jnp.zeros((QB, 2 * HEAD_DIM), F32)
                dk_pair = jnp.zeros((KB, 2 * HEAD_DIM), F32)
                dv_pair = jnp.zeros((KB, 2 * HEAD_DIM), F32)
                for a in range(2):
                    h = 2 * hp + a
                    q_m = jnp.where(masks[a], q_p, 0)
                    do_m = jnp.where(masks[a], do_b, 0)
                    sc = _dot(q_m, k_p, "nt") * scale + b_ref[h]
                    sc = jnp.where(valid, sc, NEG)
                    p = jnp.exp(sc - _pick_lane(lse_t, h))
                    dp = _dot(do_m, v_p, "nt")
                    delta = jnp.sum(jnp.where(masks[a], prod, 0.0), axis=-1, keepdims=True)
                    ds = p * (dp - delta)
                    db_acc[h] += ds
                    ds_b = (ds * scale).astype(BF16)
                    dq_pair = dq_pair + _dot(ds_b, jnp.where(masks[a], k_p, 0))
                    dk_pair = dk_pair + _dot(ds_b, q_m, "tn")
                    dv_pair = dv_pair + _dot(p.astype(BF16), do_m, "tn")
                dq_ref[:, sl] = dq_pair.astype(dq_ref.dtype)
                dk_acc[:, sl] += dk_pair
                dv_acc[:, sl] += dv_pair

        dk_ref[...] = dk_acc[0:QB, :].astype(dk_ref.dtype)
        dv_ref[...] = dv_acc[0:QB, :].astype(dv_ref.dtype)

        @pl.when(i == nq + 1)
        def _():
            row = lax.broadcasted_iota(jnp.int32, (QB, 1), 0)
            for h in range(N_HEADS):
                v = jnp.concatenate([db_acc[h], jnp.zeros((QB, width - KB), F32)], axis=1)
                for bit in range(QB.bit_length() - 1):
                    v = jnp.where(((row >> bit) & 1) == 1, pltpu.roll(v, width - (1 << bit), 1), v)
                off_ref[h:h + 1, :] = jnp.sum(v, axis=0, keepdims=True)

    qi = lambda i: jnp.minimum(i, nq - 1)
    kblk = lambda col, back: pl.BlockSpec((QB, D_ATTN), lambda i: (jnp.clip(i - back, 0, nq - 1), col))
    qblk = lambda col: pl.BlockSpec((QB, D_ATTN), lambda i: (qi(i), col))
    done = pl.BlockSpec((QB, D_ATTN), lambda i: (jnp.clip(i - 2, 0, nq - 1), 0))
    return _call(
        body, name=name, grid=(nq + 2,),
        in_specs=[qblk(0), kblk(1, 2), kblk(1, 1), kblk(1, 0), kblk(2, 2), kblk(2, 1), kblk(2, 0),
                  qblk(0), pl.BlockSpec((QB, 128), lambda i: (qi(i), 0)), qblk(1),
                  pl.BlockSpec((N_HEADS, QB, KB), lambda i: (0, 0, 0)), pl.BlockSpec((1, D_ATTN), lambda i: (0, 0))],
        out_specs=[qblk(0), done, done, pl.BlockSpec((N_HEADS, width), lambda i: (0, 0)),
                   pl.BlockSpec((1, D_ATTN), lambda i: (0, 0))],
        out_shape=[jax.ShapeDtypeStruct((s, D_ATTN), BF16)] * 3
        + [jax.ShapeDtypeStruct((N_HEADS, width), F32), jax.ShapeDtypeStruct((1, D_ATTN), F32)],
        scratch_shapes=[pltpu.VMEM((KB, D_ATTN), F32), pltpu.VMEM((KB, D_ATTN), F32), pltpu.VMEM((N_HEADS, QB, KB), F32)],
        compiler_params=_params(("arbitrary",), 56),
    )(qkv, qkv, qkv, qkv, qkv, qkv, qkv, o, lse, dy, bias, gg)


def _full_cols(g):
    c2, q4, nl, r, c = g.shape
    return g.transpose(2, 3, 1, 0, 4).reshape(nl, r, q4 * c2 * c)


def _full_rows(g):
    c2, q4, nl, r, c = g.shape
    return g.transpose(2, 1, 0, 3, 4).reshape(nl, q4 * c2 * r, c)


def _split_cols(dw, my_c):
    nl, r, c8 = dw.shape
    parts = dw.astype(BF16).reshape(nl, r, 4, 2, c8 // 8).transpose(3, 2, 0, 1, 4)
    return lax.dynamic_index_in_dim(parts, my_c, 0, False), lax.dynamic_index_in_dim(parts, 1 - my_c, 0, False)


def _split_rows(dw, my_c):
    nl, r8, c = dw.shape
    parts = dw.astype(BF16).reshape(nl, 4, 2, r8 // 8, c).transpose(2, 1, 0, 3, 4)
    return lax.dynamic_index_in_dim(parts, my_c, 0, False), lax.dynamic_index_in_dim(parts, 1 - my_c, 0, False)


def _block_diag(pw):
    out = jnp.zeros((D_POOL, D_POOL), pw.dtype)
    for gi in range(N_POOL):
        out = lax.dynamic_update_slice(out, pw[gi], (gi * POOL_GC, gi * POOL_GC))
    return out


_SMALL = ("pool_w", "pool_scale", "rel_bias", "group_gain", "pre_mix_g", "post_mix_g", "pre_ffn_g", "post_ffn_g")


def _pack(parts, rows):
    flat = jnp.concatenate([p.reshape(-1).astype(F32) for p in parts])
    return jnp.pad(flat, (0, rows * D_MODEL - flat.shape[0])).reshape(rows, D_MODEL)


def _unpack(packed, shapes):
    flat = packed.reshape(-1)
    out, at = [], 0
    for shp in shapes:
        size = int(np.prod(shp))
        out.append(flat[at:at + size].reshape(shp))
        at += size
    return out


def kernel(x, w_in, w_out, conv_w, pool_w, pool_scale, rel_bias, group_gain, pre_mix_g, post_mix_g, pre_ffn_g, post_ffn_g, w_gate_up, w_down, loss_target, m_w_in, m_w_out, m_conv_w, m_pool_w, m_pool_scale, m_rel_bias, m_group_gain, m_pre_mix_g, m_post_mix_g, m_pre_ffn_g, m_post_ffn_g, m_w_gate_up, m_w_down, v_w_in, v_w_out, v_conv_w, v_pool_w, v_pool_scale, v_rel_bias, v_group_gain, v_pre_mix_g, v_post_mix_g, v_pre_ffn_g, v_post_ffn_g, v_w_gate_up, v_w_down):
    depth = w_in.shape[0]
    s = x.shape[1]
    my_c = lax.axis_index("c")
    dev = 2 * (2 * lax.axis_index("x") + lax.axis_index("y")) + my_c

    gathered = all_gather("gather_w", [w_in.astype(BF16), w_out.astype(BF16), w_gate_up.astype(BF16),
                                       w_down.astype(BF16), conv_w])
    wf_in, wf_gu = _full_cols(gathered[0]), _full_cols(gathered[2])
    wf_out, wf_dn = _full_rows(gathered[1]), _full_rows(gathered[3])
    cw_full = _full_cols(gathered[4])

    h = x.reshape(s, D_MODEL)
    saved = []
    for l in range(depth):
        vec = lambda p: p[l].reshape(1, -1)
        xn = rmsnorm_fwd("norm_mix", h, vec(pre_mix_g), None, BF16)
        proj_a = matmul("proj_a", xn, wf_in[l], "nn", F32, s, D_MODEL, D_MODEL, 1024, 512, D_MODEL)
        qkv = matmul("proj_qkv", xn, wf_in[l], "nn", BF16, s, 3 * D_ATTN, D_MODEL, 1024, 512, D_MODEL, b_off=(0, 2))
        bd = _block_diag(pool_w[l]).astype(BF16)
        bias = attention_bias(rel_bias[l])
        gg = vec(group_gain)
        y_ab = conv_pool_fwd("conv_pool_fwd", proj_a, cw_full[l], bd, vec(pool_scale), gg[:, :2 * D_CONV])
        y, o, lse = attention_fwd("attention_fwd", qkv, bias, gg[:, 2 * D_CONV:], y_ab)
        mix = matmul("mix_out", y, wf_out[l], "nn", F32, s, D_MODEL, D_MODEL, 1024, 512, D_MODEL)
        h_mid = rmsnorm_fwd("norm_post", mix, vec(post_mix_g), h, F32)
        hn = rmsnorm_fwd("norm_ffn", h_mid, vec(pre_ffn_g), None, BF16)
        g, u, ff = matmul_swiglu("gate_up", hn, wf_gu[l])
        ffo = matmul("ffn_down", ff, wf_dn[l], "nn", F32, s, D_MODEL, D_FF, 512, 512, D_FF)
        h_out = rmsnorm_fwd("norm_post", ffo, vec(post_ffn_g), h_mid, F32)
        saved.append((h, xn, proj_a, qkv, bd, bias, y, o, lse, mix, h_mid, hn, g, u, ff, ffo))
        h = h_out

    dh, loss_part = loss_head("loss_head", h, loss_target.reshape(s, D_MODEL))

    big = {k: [None] * depth for k in ("w_in", "w_out", "w_gate_up", "w_down")}
    small = {k: [None] * depth for k in _SMALL + ("conv_w",)}
    for l in reversed(range(depth)):
        vec = lambda p: p[l].reshape(1, -1)
        h_in, xn, proj_a, qkv, bd, bias, y, o, lse, mix, h_mid, hn, g, u, ff, ffo = saved[l]
        gg = vec(group_gain)
        dffo, small["post_ffn_g"][l] = rmsnorm_bwd("norm_bwd_b", ffo, vec(post_ffn_g), dh, None, BF16)
        big["w_down"][l] = matmul("wgrad_down", ff, dffo, "tn", BF16, D_FF, D_MODEL, s, 1408, D_MODEL, 512)
        dg, du = matmul_swiglu_bwd("dgrad_down", dffo, wf_dn[l], g, u)
        dgu = jnp.concatenate([dg, du], axis=1)
        big["w_gate_up"][l] = matmul("wgrad_gu", hn, dgu, "tn", BF16, D_MODEL, 2 * D_FF, s, D_MODEL, 1408, 512)
        dhn = matmul("dgrad_gu", dgu, wf_gu[l], "nt", F32, s, D_MODEL, 2 * D_FF, 1024, 512, 1408)
        dh_mid, small["pre_ffn_g"][l] = rmsnorm_bwd("norm_bwd_r", h_mid, vec(pre_ffn_g), dhn, dh, F32)
        dmix, small["post_mix_g"][l] = rmsnorm_bwd("norm_bwd_b", mix, vec(post_mix_g), dh_mid, None, BF16)
        big["w_out"][l] = matmul("wgrad_out", y, dmix, "tn", BF16, D_MODEL, D_MODEL, s, D_MODEL, D_MODEL, 1024)
        dy = matmul("dgrad_out", dmix, wf_out[l], "nt", BF16, s, D_MODEL, D_MODEL, 1024, 512, D_MODEL)
        dpa, dcw, dbd, dps, dgg_ab = conv_pool_bwd("conv_pool_bwd", proj_a, dy, cw_full[l], bd, vec(pool_scale),
                                                   gg[:, :2 * D_CONV])
        dq, dk, dv, by_off, dgg_c = attention_bwd("attention_bwd", qkv, o, lse, dy, bias, gg[:, 2 * D_CONV:])
        dproj = jnp.concatenate([dpa, dq, dk, dv], axis=1)
        big["w_in"][l] = matmul("wgrad_in", xn, dproj, "tn", BF16, D_MODEL, 5 * D_ATTN, s, D_MODEL, 1280, 512)
        dxn = matmul("dgrad_in", dproj, wf_in[l], "nt", F32, s, D_MODEL, 5 * D_ATTN, 1024, 512, 1280)
        dh, small["pre_mix_g"][l] = rmsnorm_bwd("norm_bwd_r", h_in, vec(pre_mix_g), dxn, dh_mid, F32)
        small["conv_w"][l] = dcw
        small["pool_w"][l] = jnp.stack([dbd[gi * POOL_GC:(gi + 1) * POOL_GC, gi * POOL_GC:(gi + 1) * POOL_GC]
                                        for gi in range(N_POOL)])
        small["pool_scale"][l] = dps
        small["rel_bias"][l] = rel_bias_grad(by_off)
        small["group_gain"][l] = jnp.concatenate([dgg_ab, dgg_c], axis=1)
    grad_x = dh.reshape(x.shape)

    keep_in, send_in = _split_cols(jnp.stack(big["w_in"]), my_c)
    keep_gu, send_gu = _split_cols(jnp.stack(big["w_gate_up"]), my_c)
    keep_out, send_out = _split_rows(jnp.stack(big["w_out"]), my_c)
    keep_dn, send_dn = _split_rows(jnp.stack(big["w_down"]), my_c)
    keeps = [keep_in, keep_out, keep_gu, keep_dn]
    from_core = swap_cores("reduce_w_cores", [send_in, send_out, send_gu, send_dn])
    chip_sums = [add_pairs("reduce_w_add", k, r) for k, r in zip(keeps, from_core)]
    slabs = scatter_chips("reduce_w_chips", chip_sums)
    res_in = adamw("adamw_in", w_in, m_w_in, v_w_in, slabs[0])
    res_out = adamw("adamw_out", w_out, m_w_out, v_w_out, slabs[1])
    res_gu = adamw("adamw_gu", w_gate_up, m_w_gate_up, v_w_gate_up, slabs[2])
    res_dn = adamw("adamw_down", w_down, m_w_down, v_w_down, slabs[3])

    small_params = dict(pool_w=pool_w, pool_scale=pool_scale, rel_bias=rel_bias, group_gain=group_gain,
                        pre_mix_g=pre_mix_g, post_mix_g=post_mix_g, pre_ffn_g=pre_ffn_g, post_ffn_g=post_ffn_g)
    small_m = dict(pool_w=m_pool_w, pool_scale=m_pool_scale, rel_bias=m_rel_bias, group_gain=m_group_gain,
                   pre_mix_g=m_pre_mix_g, post_mix_g=m_post_mix_g, pre_ffn_g=m_pre_ffn_g, post_ffn_g=m_post_ffn_g)
    small_v = dict(pool_w=v_pool_w, pool_scale=v_pool_scale, rel_bias=v_rel_bias, group_gain=v_group_gain,
                   pre_mix_g=v_pre_mix_g, post_mix_g=v_post_mix_g, pre_ffn_g=v_pre_ffn_g, post_ffn_g=v_post_ffn_g)
    shapes = [small_params[k].shape for k in _SMALL] + [(depth, 3, D_CONV), (1,)]
    n_small = sum(int(np.prod(shp)) for shp in shapes)
    rows = -(-n_small // (8 * D_MODEL)) * 8
    extra = [jnp.zeros((depth, 3, D_CONV), F32), jnp.zeros((1,), F32)]
    grads_packed = _pack([jnp.stack(small[k]) for k in _SMALL] + [jnp.stack(small["conv_w"]), loss_part[0, 0:1]], rows)
    all_small = all_gather("gather_small", [grads_packed])[0].reshape(8, rows, D_MODEL)
    res_small = adamw("adamw_small", _pack([small_params[k] for k in _SMALL] + extra, rows),
                      _pack([small_m[k] for k in _SMALL] + extra, rows),
                      _pack([small_v[k] for k in _SMALL] + extra, rows), all_small)
    g_s, d_s, m_s, v_s = (_unpack(r, shapes) for r in res_small)
    loss = g_s[-1][0]
    g_conv = lax.dynamic_slice_in_dim(g_s[-2], dev * conv_w.shape[2], conv_w.shape[2], axis=2)
    res_cw = adamw("adamw_conv", conv_w, m_conv_w, v_conv_w, g_conv[None])

    names = ("w_in", "w_out", "conv_w") + _SMALL + ("w_gate_up", "w_down")
    results = dict(w_in=res_in, w_out=res_out, conv_w=res_cw, w_gate_up=res_gu, w_down=res_dn)
    for j, k in enumerate(_SMALL):
        results[k] = (g_s[j], d_s[j], m_s[j], v_s[j])
    return (loss, grad_x, *[results[k][0] for k in names], *[results[k][1] for k in names],
            *[results[k][2] for k in names], *[results[k][3] for k in names])
```

```python
import numpy as np
import jax
import jax.numpy as jnp
from jax import lax
from jax.experimental import pallas as pl
from jax.experimental.pallas import tpu as pltpu

F32 = jnp.float32
BF16 = jnp.bfloat16

CHUNK = 64
D_MODEL = 1024
D_CONV = 256
D_POOL = 256
D_ATTN = 512
HEAD_DIM = 64
N_HEADS = 8
N_POOL = 4
POOL_GC = 64
POOL_WINDOWS = (2, 4, 8, 16)
LEFT_CHUNKS = 8
REL_CLIP = 128
D_FF = 2816
EPS = 1e-6
ADAM_LR, ADAM_B1, ADAM_B2, ADAM_EPS, ADAM_WD, ADAM_STEP = 0.001, 0.9, 0.999, 1e-08, 0.01, 10

QB = 256
KB = 3 * QB
HALO = 16
T_CP = 512
T_ROW = 512
NEG = -1e30
VMEM_MB = 1 << 20
MESH = pl.DeviceIdType.MESH


def _call(body, **kw):
    return pl.pallas_call(body, **kw)


def _call_after(deps, n_in, body, **kw):
    deps = tuple(deps)
    if not deps:
        return _call(body, **kw)

    def ordered(*refs):
        body(*refs[:n_in], *refs[n_in + len(deps):])

    kw["in_specs"] = list(kw["in_specs"]) + [pl.BlockSpec(memory_space=pl.ANY)] * len(deps)
    call = _call(ordered, **kw)
    return lambda *args: call(*args, *deps)


def _params(sem, vmem_mb=48):
    return pltpu.CompilerParams(dimension_semantics=sem, vmem_limit_bytes=vmem_mb * VMEM_MB)


_CHIP_FLIPS = ((1, 0), (0, 1), (1, 1))
_HBM = pl.BlockSpec(memory_space=pltpu.HBM)
_SEM = pl.BlockSpec(memory_space=pltpu.SEMAPHORE)
_EFFECT = pltpu.SideEffectType.DATAFLOW_SIDE_EFFECTING


def _flip(v, f):
    return 1 - v if f else v


def _descriptors(plan, srcs, lands, send_sems, recv_sems):
    x, y, c = lax.axis_index("x"), lax.axis_index("y"), lax.axis_index("c")
    out = []
    for k, (src, dst, peer, land) in enumerate(plan(srcs, lands, x, y, c)):
        kw = dict(send_sem=send_sems.at[k], recv_sem=recv_sems.at[k], device_id=peer, device_id_type=MESH)
        out.append((pltpu.make_async_remote_copy(src_ref=src, dst_ref=dst, **kw),
                    pltpu.make_async_remote_copy(src_ref=src, dst_ref=land, **kw)))
    return out


def exchange_start(name, plan, n_copies, srcs, land_shapes):
    ns, nl = len(srcs), len(land_shapes)

    def body(*refs):
        src_refs, land_refs = refs[:ns], refs[ns:ns + nl]
        send_sems, recv_sems = refs[ns + nl], refs[ns + nl + 1]
        for send, _ in _descriptors(plan, src_refs, land_refs, send_sems, recv_sems):
            send.start()
        refs[-1][...] = jnp.zeros_like(refs[-1])

    lands = [pltpu.with_memory_space_constraint(lax.empty(shape, dtype), pltpu.HBM) for shape, dtype in land_shapes]
    outs = _call(
        body, name=name,
        out_shape=(pltpu.SemaphoreType.DMA((n_copies,)), pltpu.SemaphoreType.DMA((n_copies,)),
                   *[pltpu.HBM(a.shape, a.dtype) for a in srcs], *[pltpu.HBM(shape, dtype) for shape, dtype in land_shapes],
                   jax.ShapeDtypeStruct((8, 128), F32)),
        in_specs=[_HBM] * (ns + nl),
        out_specs=(_SEM, _SEM, *[_HBM] * (ns + nl), pl.BlockSpec(memory_space=pltpu.VMEM)),
        input_output_aliases={j: 2 + j for j in range(ns + nl)},
        compiler_params=pltpu.CompilerParams(has_side_effects=_EFFECT),
    )(*[pltpu.with_memory_space_constraint(a, pltpu.HBM) for a in srcs], *lands)
    return dict(plan=plan, sems=outs[:2], srcs=outs[2:2 + ns], lands=outs[2 + ns:2 + ns + nl], token=outs[-1])


def exchange_wait(name, started, after):
    srcs, lands = started["srcs"], started["lands"]
    ns, nl = len(srcs), len(lands)

    def body(*refs):
        src_refs, land_refs = refs[:ns], refs[ns:ns + nl]
        send_sems, recv_sems = refs[ns + nl], refs[ns + nl + 1]
        for _, wait in _descriptors(started["plan"], src_refs, land_refs, send_sems, recv_sems):
            wait.wait_send()
            wait.wait_recv()

    outs = _call(
        body, name=name,
        out_shape=tuple(pltpu.HBM(a.shape, a.dtype) for a in (*srcs, *lands)),
        in_specs=[_HBM] * (ns + nl) + [_SEM, _SEM, pl.BlockSpec(memory_space=pl.ANY)],
        out_specs=tuple([_HBM] * (ns + nl)),
        input_output_aliases={j: j for j in range(ns + nl)},
        compiler_params=pltpu.CompilerParams(has_side_effects=_EFFECT),
    )(*srcs, *lands, *started["sems"], after)
    return list(outs[:ns]), list(outs[ns:])


def plan_from_chips(srcs, lands, x, y, c):
    q = 2 * x + y
    out = []
    for src, land in zip(srcs, lands):
        for fx, fy in _CHIP_FLIPS:
            px, py = _flip(x, fx), _flip(y, fy)
            out.append((src, land.at[q], (px, py, c), land.at[2 * px + py]))
    return out


def plan_to_core(srcs, lands, x, y, c):
    n = len(lands)
    q = 2 * x + y
    out = []
    for own, chips, land in zip(srcs[:n], srcs[n:], lands):
        out.append((own, land.at[q], (x, y, 1 - c), land.at[q]))
        for fx, fy in _CHIP_FLIPS:
            qp = 2 * _flip(x, fx) + _flip(y, fy)
            out.append((chips.at[qp], land.at[qp], (x, y, 1 - c), land.at[qp]))
    return out


def plan_swap_cores(srcs, lands, x, y, c):
    return [(src, land, (x, y, 1 - c), land) for src, land in zip(srcs, lands)]


def plan_to_chips(srcs, lands, x, y, c):
    q = 2 * x + y
    out = []
    for src, land in zip(srcs, lands):
        for fx, fy in _CHIP_FLIPS:
            px, py = _flip(x, fx), _flip(y, fy)
            out.append((src.at[2 * px + py], land.at[q], (px, py, c), land.at[2 * px + py]))
    return out


def _slots(a):
    return ((4,) + a.shape, a.dtype)


def gather_start(name, arrays):
    return exchange_start(name, plan_from_chips, 3 * len(arrays), arrays, [_slots(a) for a in arrays])


def gather_relay(name, started, after):
    own, chips = exchange_wait(name + "_w", started, after)
    return exchange_start(name + "_s", plan_to_core, 4 * len(own), own + chips, [_slots(a) for a in own])


def gather_finish(name, relayed, after):
    srcs, cores = exchange_wait(name, relayed, after)
    n = len(cores)
    return list(zip(srcs[:n], srcs[n:], cores))


def _place():
    return 2 * lax.axis_index("x") + lax.axis_index("y"), lax.axis_index("c")


def assemble_cols(name, own, chips, core):
    r, c = own.shape
    t = _row_tile(r, 256)

    def body(own_ref, chips_ref, core_ref, o_ref):
        q_me, c_me = _place()
        for q in range(4):
            mine = jnp.where(q == q_me, own_ref[...], chips_ref[q])
            other = core_ref[q]
            o_ref[:, 2 * q * c:(2 * q + 1) * c] = jnp.where(c_me == 0, mine, other)
            o_ref[:, (2 * q + 1) * c:(2 * q + 2) * c] = jnp.where(c_me == 0, other, mine)

    slots = pl.BlockSpec((4, t, c), lambda i: (0, i, 0))
    return _call(
        body, name=name, grid=(r // t,),
        in_specs=[pl.BlockSpec((t, c), lambda i: (i, 0)), slots, slots],
        out_specs=pl.BlockSpec((t, 8 * c), lambda i: (i, 0)),
        out_shape=jax.ShapeDtypeStruct((r, 8 * c), own.dtype), compiler_params=_params(("parallel",)),
    )(own, chips, core)


def assemble_rows(name, own, chips, core):
    r, c = own.shape

    def body(own_ref, chips_ref, core_ref, o_ref):
        q_me, c_me = _place()
        d = pl.program_id(0)
        mine = jnp.where(d // 2 == q_me, own_ref[...], chips_ref[...])
        o_ref[...] = jnp.where(d % 2 == c_me, mine, core_ref[...])

    slot = pl.BlockSpec((None, r, c), lambda d: (d // 2, 0, 0))
    return _call(
        body, name=name, grid=(8,),
        in_specs=[pl.BlockSpec((r, c), lambda d: (0, 0)), slot, slot],
        out_specs=pl.BlockSpec((r, c), lambda d: (d, 0)),
        out_shape=jax.ShapeDtypeStruct((8 * r, c), own.dtype), compiler_params=_params(("parallel",)),
    )(own, chips, core)


def split_cols(name, dw):
    r, c8 = dw.shape
    c = c8 // 8
    t = _row_tile(r, 256)

    def body(dw_ref, keep_ref, send_ref):
        _, c_me = _place()
        for q in range(4):
            even = dw_ref[:, 2 * q * c:(2 * q + 1) * c]
            odd = dw_ref[:, (2 * q + 1) * c:(2 * q + 2) * c]
            keep_ref[q] = jnp.where(c_me == 0, even, odd)
            send_ref[q] = jnp.where(c_me == 0, odd, even)

    slots = pl.BlockSpec((4, t, c), lambda i: (0, i, 0))
    return _call(
        body, name=name, grid=(r // t,),
        in_specs=[pl.BlockSpec((t, c8), lambda i: (i, 0))], out_specs=[slots, slots],
        out_shape=[jax.ShapeDtypeStruct((4, r, c), dw.dtype)] * 2, compiler_params=_params(("parallel",)),
    )(dw)


def split_rows(name, dw):
    r8, c = dw.shape
    r = r8 // 8

    def body(dw_ref, keep_ref, send_ref):
        _, c_me = _place()
        d = pl.program_id(0)

        @pl.when(d % 2 == c_me)
        def _():
            keep_ref[...] = dw_ref[...]

        @pl.when(d % 2 != c_me)
        def _():
            send_ref[...] = dw_ref[...]

    slot = pl.BlockSpec((None, r, c), lambda d: (d // 2, 0, 0))
    return _call(
        body, name=name, grid=(8,),
        in_specs=[pl.BlockSpec((r, c), lambda d: (d, 0))], out_specs=[slot, slot],
        out_shape=[jax.ShapeDtypeStruct((4, r, c), dw.dtype)] * 2, compiler_params=_params(("arbitrary",)),
    )(dw)


def _rms(v):
    return lax.rsqrt(jnp.mean(v * v, axis=-1, keepdims=True) + EPS)


def rmsnorm_fwd(name, x, g, res, out_dtype, deps=()):
    s, d = x.shape
    has_res = res is not None

    def body(*refs):
        if has_res:
            x_ref, g_ref, r_ref, o_ref = refs
        else:
            x_ref, g_ref, o_ref = refs
        xv = x_ref[...]
        y = (xv * _rms(xv)) * g_ref[...]
        if has_res:
            y = y + r_ref[...]
        o_ref[...] = y.astype(o_ref.dtype)

    row = pl.BlockSpec((T_ROW, d), lambda i: (i, 0))
    vec = pl.BlockSpec((1, d), lambda i: (0, 0))
    args = (x, g, res) if has_res else (x, g)
    return _call_after(
        deps, len(args), body, name=name, grid=(s // T_ROW,),
        in_specs=[row, vec] + ([row] if has_res else []), out_specs=row,
        out_shape=jax.ShapeDtypeStruct((s, d), out_dtype),
        compiler_params=_params(("parallel",)),
    )(*args)


def _norm_bwd(xv, r, g, dy):
    a = dy * g
    dx = r * (a - xv * ((r * r) * jnp.mean(a * xv, axis=-1, keepdims=True)))
    return dx, dy * (xv * r)


def rmsnorm_bwd(name, x, g, dy, res, out_dtype, deps=()):
    s, d = x.shape
    has_res = res is not None

    def body(*refs):
        if has_res:
            x_ref, g_ref, dy_ref, r_ref, dx_ref, dg_ref = refs
        else:
            x_ref, g_ref, dy_ref, dx_ref, dg_ref = refs
        xv = x_ref[...]
        dx, dgt = _norm_bwd(xv, _rms(xv), g_ref[...], dy_ref[...].astype(F32))
        if has_res:
            dx = dx + r_ref[...]
        dx_ref[...] = dx.astype(dx_ref.dtype)

        @pl.when(pl.program_id(0) == 0)
        def _():
            dg_ref[...] = jnp.zeros_like(dg_ref)
        dg_ref[...] += jnp.sum(dgt, axis=0, keepdims=True)

    row = pl.BlockSpec((T_ROW, d), lambda i: (i, 0))
    vec = pl.BlockSpec((1, d), lambda i: (0, 0))
    args = (x, g, dy, res) if has_res else (x, g, dy)
    return _call_after(
        deps, len(args), body, name=name, grid=(s // T_ROW,),
        in_specs=[row, vec, row] + ([row] if has_res else []), out_specs=[row, vec],
        out_shape=[jax.ShapeDtypeStruct((s, d), out_dtype), jax.ShapeDtypeStruct((1, d), F32)],
        compiler_params=_params(("arbitrary",)),
    )(*args)


def loss_head(name, h, tgt):
    s, d = h.shape

    def body(h_ref, t_ref, dh_ref, l_ref):
        e = h_ref[...] - t_ref[...]
        dh_ref[...] = e * (1.0 / d)

        @pl.when(pl.program_id(0) == 0)
        def _():
            l_ref[...] = jnp.zeros_like(l_ref)
        part = 0.5 * jnp.sum(jnp.mean(e * e, axis=-1, keepdims=True), axis=0, keepdims=True)
        l_ref[...] += jnp.broadcast_to(part, l_ref.shape)

    row = pl.BlockSpec((T_ROW, d), lambda i: (i, 0))
    return _call(
        body, name=name, grid=(s // T_ROW,),
        in_specs=[row, row], out_specs=[row, pl.BlockSpec((1, 128), lambda i: (0, 0))],
        out_shape=[jax.ShapeDtypeStruct((s, d), F32), jax.ShapeDtypeStruct((1, 128), F32)],
        compiler_params=_params(("arbitrary",)),
    )(h, tgt)


def _row_tile(rows, limit=512):
    t = min(rows, limit)
    while rows % t or (t % 8 and t != rows):
        t -= 1
    return t


def add_pairs(name, a, b):
    shape = a.shape
    a2, b2 = a.reshape(-1, shape[-1]), b.reshape(-1, shape[-1])
    rows, cols = a2.shape
    t = _row_tile(rows)

    def body(a_ref, b_ref, o_ref):
        o_ref[...] = (a_ref[...].astype(F32) + b_ref[...].astype(F32)).astype(o_ref.dtype)

    blk = pl.BlockSpec((t, cols), lambda i: (i, 0))
    out = _call(
        body, name=name, grid=(rows // t,), in_specs=[blk, blk], out_specs=blk,
        out_shape=jax.ShapeDtypeStruct((rows, cols), a.dtype), compiler_params=_params(("parallel",)),
    )(a2, b2)
    return out.reshape(shape)


def _adamw_math(w, m, v, g):
    c1 = 1.0 - ADAM_B1 ** ADAM_STEP
    c2 = 1.0 - ADAM_B2 ** ADAM_STEP
    mn = ADAM_B1 * m + (1.0 - ADAM_B1) * g
    vn = ADAM_B2 * v + (1.0 - ADAM_B2) * (g * g)
    delta = -ADAM_LR * ((mn / c1) / (jnp.sqrt(vn / c2) + ADAM_EPS) + ADAM_WD * w)
    return delta, mn, vn


def _slab_sum(g_ref, n):
    g = g_ref[0].astype(F32)
    for j in range(1, n):
        g = g + g_ref[j].astype(F32)
    return g


def adamw(name, w, m, v, g_slabs, deps=()):
    shape = w.shape
    n = g_slabs.shape[0]
    w2, m2, v2 = (t.reshape(-1, shape[-1]) for t in (w, m, v))
    g3 = g_slabs.reshape(n, -1, shape[-1])
    rows, cols = w2.shape
    t = _row_tile(rows, 256)

    def body(w_ref, m_ref, v_ref, g_ref, go_ref, d_ref, mo_ref, vo_ref):
        g = _slab_sum(g_ref, n)
        go_ref[...] = g
        d_ref[...], mo_ref[...], vo_ref[...] = _adamw_math(w_ref[...], m_ref[...], v_ref[...], g)

    blk = pl.BlockSpec((t, cols), lambda i: (i, 0))
    outs = _call_after(
        deps, 4, body, name=name, grid=(rows // t,),
        in_specs=[blk, blk, blk, pl.BlockSpec((n, t, cols), lambda i: (0, i, 0))], out_specs=[blk] * 4,
        out_shape=[jax.ShapeDtypeStruct((rows, cols), F32)] * 4, compiler_params=_params(("parallel",)),
    )(w2, m2, v2, g3)
    return tuple(o.reshape(shape) for o in outs)


def adamw_layer(name, l, w, m, v, own_sums, chip_sums, into):
    _, rows, cols = w.shape
    t = _row_tile(rows, 256)
    if into is None:
        into = tuple(lax.empty(w.shape, F32) for _ in range(4))

    def body(w_ref, m_ref, v_ref, own_ref, far_ref, a0, a1, a2, a3, go_ref, d_ref, mo_ref, vo_ref):
        del a0, a1, a2, a3
        q_me, _ = _place()
        g = jnp.zeros((t, cols), F32)
        for q in range(4):
            g = g + jnp.where(q == q_me, own_ref[q], far_ref[q]).astype(F32)
        go_ref[...] = g
        d_ref[...], mo_ref[...], vo_ref[...] = _adamw_math(w_ref[...], m_ref[...], v_ref[...], g)

    blk = pl.BlockSpec((None, t, cols), lambda i: (l, i, 0))
    slabs = pl.BlockSpec((4, t, cols), lambda i: (0, i, 0))
    anyw = pl.BlockSpec(memory_space=pl.ANY)
    return _call(
        body, name=name, grid=(rows // t,),
        in_specs=[blk, blk, blk, slabs, slabs] + [anyw] * 4, out_specs=[blk] * 4,
        out_shape=[jax.ShapeDtypeStruct(w.shape, F32)] * 4, input_output_aliases={5: 0, 6: 1, 7: 2, 8: 3},
        compiler_params=_params(("parallel",)),
    )(w, m, v, own_sums, chip_sums, *into)


_DIMS = {"nn": (((1,), (0,)), ((), ())), "nt": (((1,), (1,)), ((), ())), "tn": (((0,), (0,)), ((), ()))}


def _dot(a, b, mode="nn"):
    return lax.dot_general(a, b, _DIMS[mode], preferred_element_type=F32)


def matmul(name, a, b, mode, out_dtype, m, n, k, tm, tn, tk, b_off=(0, 0), deps=()):
    gm, gn, gk = m // tm, n // tn, k // tk
    assert gm * tm == m and gn * tn == n and gk * tk == k
    r0, c0 = b_off
    a_parts = list(a) if isinstance(a, (list, tuple)) else [a]
    b_parts = list(b) if isinstance(b, (list, tuple)) else [b]
    assert len(a_parts) == 1 or mode == "nt"
    assert len(b_parts) == 1 or mode == "tn"

    def spans(parts, axis, tile):
        out, at = [], 0
        for p in parts:
            nb = p.shape[axis] // tile
            assert nb * tile == p.shape[axis]
            out.append((at, nb))
            at += nb
        return out

    a_spans = spans(a_parts, 1, tk) if len(a_parts) > 1 else [(0, gk)]
    b_spans = spans(b_parts, 1, tn) if len(b_parts) > 1 else [(0, gn)]

    def within(t, span):
        return (t >= span[0]) & (t < span[0] + span[1])

    def local(t, span):
        return jnp.clip(t - span[0], 0, span[1] - 1)

    a_specs, b_specs = [], []
    for sp in a_spans:
        if mode == "tn":
            a_specs.append(pl.BlockSpec((tk, tm), lambda i, j, kk: (kk, i)))
        elif len(a_parts) == 1:
            a_specs.append(pl.BlockSpec((tm, tk), lambda i, j, kk: (i, kk)))
        else:
            a_specs.append(pl.BlockSpec((tm, tk), lambda i, j, kk, sp=sp: (i, local(kk, sp))))
    for sp in b_spans:
        if mode == "nn":
            b_specs.append(pl.BlockSpec((tk, tn), lambda i, j, kk: (kk + r0, j + c0)))
        elif mode == "nt":
            b_specs.append(pl.BlockSpec((tn, tk), lambda i, j, kk: (j + r0, kk + c0)))
        elif len(b_parts) == 1:
            b_specs.append(pl.BlockSpec((tk, tn), lambda i, j, kk: (kk + r0, j + c0)))
        else:
            b_specs.append(pl.BlockSpec((tk, tn), lambda i, j, kk, sp=sp: (jnp.where(within(j, sp), kk, 0), local(j, sp))))
    o_spec = pl.BlockSpec((tm, tn), lambda i, j, kk: (i, j))
    na, nb = len(a_parts), len(b_parts)
    single = na == 1 and nb == 1 and gk == 1

    def body(*refs):
        a_refs, b_refs, o_ref = refs[:na], refs[na:na + nb], refs[na + nb]
        if single:
            o_ref[...] = _dot(a_refs[0][...], b_refs[0][...], mode).astype(o_ref.dtype)
            return
        acc_ref = refs[-1]
        j, kk = pl.program_id(1), pl.program_id(2)

        @pl.when(kk == 0)
        def _():
            acc_ref[...] = jnp.zeros_like(acc_ref)

        for pa, spa in enumerate(a_spans):
            for pb, spb in enumerate(b_spans):
                def add(pa=pa, pb=pb):
                    acc_ref[...] += _dot(a_refs[pa][...], b_refs[pb][...], mode)
                if na > 1:
                    pl.when(within(kk, spa))(add)
                elif nb > 1:
                    pl.when(within(j, spb))(add)
                else:
                    add()

        @pl.when(kk == gk - 1)
        def _():
            o_ref[...] = acc_ref[...].astype(o_ref.dtype)

    return _call_after(
        deps, na + nb, body, name=name, grid=(gm, gn, gk),
        in_specs=a_specs + b_specs, out_specs=o_spec,
        out_shape=jax.ShapeDtypeStruct((m, n), out_dtype),
        scratch_shapes=[] if single else [pltpu.VMEM((tm, tn), F32)],
        compiler_params=_params(("parallel", "parallel", "arbitrary")),
    )(*a_parts, *b_parts)


def matmul_swiglu(name, a, w_gu, deps=()):
    s, d = a.shape
    tm, tn = 512, 1408
    gn = D_FF // tn

    def body(a_ref, wg_ref, wu_ref, g_ref, u_ref, f_ref):
        av = a_ref[...]
        g = _dot(av, wg_ref[...])
        u = _dot(av, wu_ref[...])
        g_ref[...] = g.astype(g_ref.dtype)
        u_ref[...] = u.astype(u_ref.dtype)
        f_ref[...] = ((g * (1.0 / (1.0 + jnp.exp(-g)))) * u).astype(f_ref.dtype)

    o_spec = pl.BlockSpec((tm, tn), lambda i, j: (i, j))
    return _call_after(
        deps, 3, body, name=name, grid=(s // tm, gn),
        in_specs=[pl.BlockSpec((tm, d), lambda i, j: (i, 0)), pl.BlockSpec((d, tn), lambda i, j: (0, j)),
                  pl.BlockSpec((d, tn), lambda i, j: (0, j + gn))],
        out_specs=[o_spec] * 3, out_shape=[jax.ShapeDtypeStruct((s, D_FF), BF16)] * 3,
        compiler_params=_params(("parallel", "parallel")),
    )(a, w_gu, w_gu)


def matmul_swiglu_bwd(name, dffo, w_down, g, u):
    s, d = dffo.shape
    tm, tn = 512, 1408

    def body(a_ref, w_ref, g_ref, u_ref, dg_ref, du_ref):
        dff = _dot(a_ref[...], w_ref[...], "nt")
        gv = g_ref[...].astype(F32)
        sig = 1.0 / (1.0 + jnp.exp(-gv))
        du_ref[...] = (dff * (gv * sig)).astype(du_ref.dtype)
        dg_ref[...] = (dff * u_ref[...].astype(F32) * (sig * (1.0 + gv * (1.0 - sig)))).astype(dg_ref.dtype)

    o_spec = pl.BlockSpec((tm, tn), lambda i, j: (i, j))
    return _call(
        body, name=name, grid=(s // tm, D_FF // tn),
        in_specs=[pl.BlockSpec((tm, d), lambda i, j: (i, 0)), pl.BlockSpec((tn, d), lambda i, j: (j, 0)), o_spec, o_spec],
        out_specs=[o_spec] * 2, out_shape=[jax.ShapeDtypeStruct((s, D_FF), BF16)] * 2,
        compiler_params=_params(("parallel", "parallel")),
    )(dffo, w_down, g, u)


def _down(v, n):
    return pltpu.roll(v, n, 0)


def _up(v, n):
    return pltpu.roll(v, v.shape[0] - n, 0)


def _by_window(lane, v2, v4, v8, v16):
    return jnp.where(lane < POOL_GC, v2, jnp.where(lane < 2 * POOL_GC, v4, jnp.where(lane < 3 * POOL_GC, v8, v16)))


def _taps(cw_ref):
    return cw_ref[0:1, :], cw_ref[1:2, :], cw_ref[2:3, :]


def _conv_pool_forward(ext, t0, cw, bd):
    n_out = ext.shape[0] - HALO
    gb = ext[HALO:, 0:D_CONV]
    z = ext[:, D_CONV:2 * D_CONV] * ext[:, 2 * D_CONV:3 * D_CONV]
    z0, z1, z2 = z[HALO:], _down(z, 1)[HALO:], _down(z, 2)[HALO:]
    conv = cw[2] * z0 + cw[1] * z1 + cw[0] * z2
    x = ext[:, 3 * D_CONV:]
    w2 = x + _down(x, 1)
    w4 = w2 + _down(w2, 2)
    w8 = w4 + _down(w4, 4)
    w16 = w8 + _down(w8, 8)
    lane = lax.broadcasted_iota(jnp.int32, (1, D_POOL), 1)
    win = _by_window(lane, 2.0, 4.0, 8.0, 16.0)
    pos = (t0 + lax.broadcasted_iota(jnp.int32, (n_out, 1), 0) + 1).astype(F32)
    cnt = jnp.minimum(pos, win)
    d = _by_window(lane, w2, w4, w8, w16)[HALO:] / cnt - x[HALO:]
    ybp = _dot(d.astype(BF16), bd)
    return gb, z0, z1, z2, conv, d, cnt, ybp, lane


def conv_pool_fwd(name, proj_a, cw, bd, ps, gg, deps=()):
    s = proj_a.shape[0]
    t = T_CP
    hb = t // HALO

    def body(main_ref, prev_ref, cw_ref, bd_ref, ps_ref, gg_ref, y_ref):
        i = pl.program_id(0)
        prev = jnp.where(i > 0, prev_ref[...], 0.0)
        ext = jnp.concatenate([prev, main_ref[...]], axis=0)
        gb, _, _, _, conv, _, _, ybp, _ = _conv_pool_forward(ext, i * t, _taps(cw_ref), bd_ref[...])
        ya = gb * conv
        yb = ybp * ps_ref[...]
        ggv = gg_ref[...]
        y_ref[:, 0:D_CONV] = ((ya * _rms(ya)) * ggv[:, 0:D_CONV]).astype(y_ref.dtype)
        y_ref[:, D_CONV:] = ((yb * _rms(yb)) * ggv[:, D_CONV:]).astype(y_ref.dtype)

    full = lambda shape: pl.BlockSpec(shape, lambda i: (0,) * len(shape))
    return _call_after(
        deps, 6, body, name=name, grid=(s // t,),
        in_specs=[pl.BlockSpec((t, D_MODEL), lambda i: (i, 0)),
                  pl.BlockSpec((HALO, D_MODEL), lambda i: (jnp.maximum(i * hb - 1, 0), 0)),
                  full((3, D_CONV)), full((D_POOL, D_POOL)), full((1, D_POOL)), full((1, 2 * D_CONV))],
        out_specs=pl.BlockSpec((t, 2 * D_CONV), lambda i: (i, 0)),
        out_shape=jax.ShapeDtypeStruct((s, D_MODEL), BF16),
        compiler_params=_params(("parallel",)),
    )(proj_a, proj_a, cw, bd, ps, gg)


def conv_pool_bwd(name, proj_a, dy, cw, bd, ps, gg):
    s = proj_a.shape[0]
    t = T_CP
    hb = t // HALO
    nblk = s // t
    last_halo = s // HALO - 1

    def body(main_ref, prev_ref, next_ref, dy_ref, dyn_ref, cw_ref, bd_ref, ps_ref, gg_ref,
             dp_ref, dcw_ref, dbd_ref, dps_ref, dgg_ref):
        i = pl.program_id(0)
        prev = jnp.where(i > 0, prev_ref[...], 0.0)
        ext = jnp.concatenate([prev, main_ref[...], next_ref[...]], axis=0)
        cwv, bdv, psv, ggv = _taps(cw_ref), bd_ref[...], ps_ref[...], gg_ref[...]
        gb, z0, z1, z2, conv, d, cnt, ybp, lane = _conv_pool_forward(ext, i * t, cwv, bdv)
        dyn = jnp.where(i < nblk - 1, dyn_ref[...].astype(F32), 0.0)
        dyv = jnp.concatenate([dy_ref[...].astype(F32), dyn], axis=0)
        ya = gb * conv
        yb = ybp * psv
        dya, dgg_a = _norm_bwd(ya, _rms(ya), ggv[:, 0:D_CONV], dyv[:, 0:D_CONV])
        dyb, dgg_b = _norm_bwd(yb, _rms(yb), ggv[:, D_CONV:], dyv[:, D_CONV:])

        dconv = dya * gb
        dz = (cwv[2] * dconv + cwv[1] * _up(dconv, 1) + cwv[0] * _up(dconv, 2))[:t]
        main = main_ref[...]
        dp_ref[:, 0:D_CONV] = (dya * conv)[:t].astype(dp_ref.dtype)
        dp_ref[:, D_CONV:2 * D_CONV] = (dz * main[:, 2 * D_CONV:3 * D_CONV]).astype(dp_ref.dtype)
        dp_ref[:, 2 * D_CONV:3 * D_CONV] = (dz * main[:, D_CONV:2 * D_CONV]).astype(dp_ref.dtype)

        dybs = dyb * psv
        dd = _dot(dybs.astype(BF16), bdv, "nt")
        e = dd / cnt
        a2 = e + _up(e, 1)
        a4 = a2 + _up(a2, 2)
        a8 = a4 + _up(a4, 4)
        a16 = a8 + _up(a8, 8)
        dp_ref[:, 3 * D_CONV:] = (_by_window(lane, a2, a4, a8, a16) - dd)[:t].astype(dp_ref.dtype)

        @pl.when(i == 0)
        def _():
            dcw_ref[...] = jnp.zeros_like(dcw_ref)
            dbd_ref[...] = jnp.zeros_like(dbd_ref)
            dps_ref[...] = jnp.zeros_like(dps_ref)
            dgg_ref[...] = jnp.zeros_like(dgg_ref)

        rsum = lambda v: jnp.sum(v[:t], axis=0, keepdims=True)
        dcw_ref[0:1, :] += rsum(dconv * z2)
        dcw_ref[1:2, :] += rsum(dconv * z1)
        dcw_ref[2:3, :] += rsum(dconv * z0)
        dbd_ref[...] += _dot(d[:t].astype(BF16), dybs[:t].astype(BF16), "tn")
        dps_ref[...] += rsum(dyb * ybp)
        dgg_ref[:, 0:D_CONV] += rsum(dgg_a)
        dgg_ref[:, D_CONV:] += rsum(dgg_b)

    full = lambda shape: pl.BlockSpec(shape, lambda i: (0,) * len(shape))
    next_halo = lambda i: (jnp.minimum((i + 1) * hb, last_halo), 0)
    return _call(
        body, name=name, grid=(nblk,),
        in_specs=[pl.BlockSpec((t, D_MODEL), lambda i: (i, 0)),
                  pl.BlockSpec((HALO, D_MODEL), lambda i: (jnp.maximum(i * hb - 1, 0), 0)),
                  pl.BlockSpec((HALO, D_MODEL), next_halo),
                  pl.BlockSpec((t, 2 * D_CONV), lambda i: (i, 0)),
                  pl.BlockSpec((HALO, 2 * D_CONV), next_halo),
                  full((3, D_CONV)), full((D_POOL, D_POOL)), full((1, D_POOL)), full((1, 2 * D_CONV))],
        out_specs=[pl.BlockSpec((t, D_MODEL), lambda i: (i, 0)),
                   full((3, D_CONV)), full((D_POOL, D_POOL)), full((1, D_POOL)), full((1, 2 * D_CONV))],
        out_shape=[jax.ShapeDtypeStruct((s, D_MODEL), BF16), jax.ShapeDtypeStruct((3, D_CONV), F32),
                   jax.ShapeDtypeStruct((D_POOL, D_POOL), F32), jax.ShapeDtypeStruct((1, D_POOL), F32),
                   jax.ShapeDtypeStruct((1, 2 * D_CONV), F32)],
        compiler_params=_params(("arbitrary",)),
    )(proj_a, proj_a, proj_a, dy, dy, cw, bd, ps, gg)


def _band_mask():
    qa = np.arange(QB)[:, None] // CHUNK
    km = np.arange(KB)[None, :] // CHUNK
    return (km - qa >= 0) & (km - qa <= LEFT_CHUNKS)


def attention_bias(rel_bias):
    n_far = 2 * QB - REL_CLIP + 1
    far = jnp.broadcast_to(rel_bias[:, 2 * REL_CLIP:], (N_HEADS, n_far))
    mid = rel_bias[:, 1:2 * REL_CLIP][:, ::-1]
    near = jnp.broadcast_to(rel_bias[:, 0:1], (N_HEADS, KB - n_far - (2 * REL_CLIP - 1)))
    wrap = jnp.broadcast_to(rel_bias[:, 2 * REL_CLIP:], (N_HEADS, 4 * QB - KB))
    by_offset = jnp.concatenate([far, mid, near, wrap], axis=1)
    width = 4 * QB
    toeplitz = jnp.tile(by_offset, (1, QB))[:, :QB * (width - 1)].reshape(N_HEADS, QB, width - 1)[:, :, :KB]
    return jnp.where(jnp.asarray(_band_mask())[None], toeplitz, NEG)


def rel_bias_grad(by_offset):
    n_far = 2 * QB - REL_CLIP + 1
    hi = jnp.sum(by_offset[:, :n_far], axis=1, keepdims=True) + jnp.sum(by_offset[:, KB:], axis=1, keepdims=True)
    mid = by_offset[:, n_far:n_far + 2 * REL_CLIP - 1][:, ::-1]
    lo = jnp.sum(by_offset[:, n_far + 2 * REL_CLIP - 1:KB], axis=1, keepdims=True)
    return jnp.concatenate([lo, mid, hi], axis=1)


def _head_masks():
    first = lax.broadcasted_iota(jnp.int32, (1, 2 * HEAD_DIM), 1) < HEAD_DIM
    return first, jnp.logical_not(first)


def _pick_lane(tile, h):
    lane = lax.broadcasted_iota(jnp.int32, (1, tile.shape[1]), 1)
    return jnp.sum(jnp.where(lane == h, tile, 0.0), axis=-1, keepdims=True)


def attention_fwd(name, qkv, bias, gg, y_ab):
    s = qkv.shape[0]
    nq = s // QB
    scale = HEAD_DIM ** -0.5

    def body(q_ref, k0, k1, k2, v0, v1, v2, b_ref, gg_ref, y_in, y_ref, o_ref, lse_ref):
        del y_in
        i = pl.program_id(0)
        kb = jnp.concatenate([k0[...], k1[...], k2[...]], axis=0)
        vb = jnp.concatenate([v0[...], v1[...], v2[...]], axis=0)
        valid = lax.broadcasted_iota(jnp.int32, (1, KB), 1) >= (2 - i) * QB
        lane = lax.broadcasted_iota(jnp.int32, (1, 128), 1)
        masks = _head_masks()
        lse = jnp.zeros((QB, 128), F32)
        outs = []
        for hp in range(N_HEADS // 2):
            sl = slice(2 * HEAD_DIM * hp, 2 * HEAD_DIM * (hp + 1))
            q_p, k_p, v_p = q_ref[:, sl], kb[:, sl], vb[:, sl]
            o_pair = jnp.zeros((QB, 2 * HEAD_DIM), F32)
            for a in range(2):
                h = 2 * hp + a
                sc = _dot(jnp.where(masks[a], q_p, 0), k_p, "nt") * scale + b_ref[h]
                sc = jnp.where(valid, sc, NEG)
                mx = jnp.max(sc, axis=-1, keepdims=True)
                e = jnp.exp(sc - mx)
                l = jnp.sum(e, axis=-1, keepdims=True)
                o_pair = o_pair + _dot(e.astype(BF16), jnp.where(masks[a], v_p, 0)) * (1.0 / l)
                lse = jnp.where(lane == h, mx + jnp.log(l), lse)
            outs.append(o_pair)
        o = jnp.concatenate(outs, axis=1)
        o_ref[...] = o.astype(o_ref.dtype)
        lse_ref[...] = lse
        y_ref[...] = ((o * _rms(o)) * gg_ref[...]).astype(y_ref.dtype)

    blk = lambda col, back: pl.BlockSpec((QB, D_ATTN), lambda i: (jnp.maximum(i - back, 0), col))
    return _call(
        body, name=name, grid=(nq,),
        in_specs=[blk(0, 0), blk(1, 2), blk(1, 1), blk(1, 0), blk(2, 2), blk(2, 1), blk(2, 0),
                  pl.BlockSpec((N_HEADS, QB, KB), lambda i: (0, 0, 0)), pl.BlockSpec((1, D_ATTN), lambda i: (0, 0)),
                  pl.BlockSpec(memory_space=pl.ANY)],
        out_specs=[pl.BlockSpec((QB, D_ATTN), lambda i: (i, 1)), pl.BlockSpec((QB, D_ATTN), lambda i: (i, 0)),
                   pl.BlockSpec((QB, 128), lambda i: (i, 0))],
        out_shape=[jax.ShapeDtypeStruct((s, D_MODEL), BF16), jax.ShapeDtypeStruct((s, D_ATTN), BF16),
                   jax.ShapeDtypeStruct((s, 128), F32)],
        input_output_aliases={9: 0},
        compiler_params=_params(("arbitrary",), 56),
    )(qkv, qkv, qkv, qkv, qkv, qkv, qkv, bias, gg, y_ab)


def attention_bwd(name, qkv, o, lse, dy, bias, gg):
    s = qkv.shape[0]
    nq = s // QB
    scale = HEAD_DIM ** -0.5
    width = 4 * QB

    def body(q_ref, k0, k1, k2, v0, v1, v2, o_ref, lse_ref, dy_ref, b_ref, gg_ref,
             dq_ref, dk_ref, dv_ref, off_ref, dgg_ref, dk_acc, dv_acc, db_acc):
        i = pl.program_id(0)

        @pl.when(i == 0)
        def _():
            dk_acc[...] = jnp.zeros_like(dk_acc)
            dv_acc[...] = jnp.zeros_like(dv_acc)
            db_acc[...] = jnp.zeros_like(db_acc)
            dgg_ref[...] = jnp.zeros_like(dgg_ref)

        @pl.when(i > 0)
        def _():
            for acc in (dk_acc, dv_acc):
                kept = acc[QB:, :]
                acc[0:2 * QB, :] = kept
                acc[2 * QB:, :] = jnp.zeros((QB, D_ATTN), F32)

        @pl.when(i < nq)
        def _():
            ov = o_ref[...].astype(F32)
            dyv = dy_ref[...].astype(F32)
            do, dgg_t = _norm_bwd(ov, _rms(ov), gg_ref[...], dyv)
            dgg_ref[...] += jnp.sum(dgg_t, axis=0, keepdims=True)
            kb = jnp.concatenate([k0[...], k1[...], k2[...]], axis=0)
            vb = jnp.concatenate([v0[...], v1[...], v2[...]], axis=0)
            valid = lax.broadcasted_iota(jnp.int32, (1, KB), 1) >= (2 - i) * QB
            masks = _head_masks()
            lse_t = lse_ref[...]
            for hp in range(N_HEADS // 2):
                sl = slice(2 * HEAD_DIM * hp, 2 * HEAD_DIM * (hp + 1))
                q_p, k_p, v_p = q_ref[:, sl], kb[:, sl], vb[:, sl]
                do_p = do[:, sl]
                prod = do_p * ov[:, sl]
                do_b = do_p.astype(BF16)
                dq_pair = jnp.zeros((QB, 2 * HEAD_DIM), F32)
                dk_pair = jnp.zeros((KB, 2 * HEAD_DIM), F32)
                dv_pair = jnp.zeros((KB, 2 * HEAD_DIM), F32)
                for a in range(2):
                    h = 2 * hp + a
                    q_m = jnp.where(masks[a], q_p, 0)
                    do_m = jnp.where(masks[a], do_b, 0)
                    sc = _dot(q_m, k_p, "nt") * scale + b_ref[h]
                    sc = jnp.where(valid, sc, NEG)
                    p = jnp.exp(sc - _pick_lane(lse_t, h))
                    dp = _dot(do_m, v_p, "nt")
                    delta = jnp.sum(jnp.where(masks[a], prod, 0.0), axis=-1, keepdims=True)
                    ds = p * (dp - delta)
                    db_acc[h] += ds
                    ds_b = (ds * scale).astype(BF16)
                    dq_pair = dq_pair + _dot(ds_b, jnp.where(masks[a], k_p, 0))
                    dk_pair = dk_pair + _dot(ds_b, q_m, "tn")
                    dv_pair = dv_pair + _dot(p.astype(BF16), do_m, "tn")
                dq_ref[:, sl] = dq_pair.astype(dq_ref.dtype)
                dk_acc[:, sl] += dk_pair
                dv_acc[:, sl] += dv_pair

        dk_ref[...] = dk_acc[0:QB, :].astype(dk_ref.dtype)
        dv_ref[...] = dv_acc[0:QB, :].astype(dv_ref.dtype)

        @pl.when(i == nq + 1)
        def _():
            row = lax.broadcasted_iota(jnp.int32, (QB, 1), 0)
            for h in range(N_HEADS):
                v = jnp.concatenate([db_acc[h], jnp.zeros((QB, width - KB), F32)], axis=1)
                for bit in range(QB.bit_length() - 1):
                    v = jnp.where(((row >> bit) & 1) == 1, pltpu.roll(v, width - (1 << bit), 1), v)
                off_ref[h:h + 1, :] = jnp.sum(v, axis=0, keepdims=True)

    qi = lambda i: jnp.minimum(i, nq - 1)
    kblk = lambda col, back: pl.BlockSpec((QB, D_ATTN), lambda i: (jnp.clip(i - back, 0, nq - 1), col))
    qblk = lambda col: pl.BlockSpec((QB, D_ATTN), lambda i: (qi(i), col))
    done = pl.BlockSpec((QB, D_ATTN), lambda i: (jnp.clip(i - 2, 0, nq - 1), 0))
    return _call(
        body, name=name, grid=(nq + 2,),
        in_specs=[qblk(0), kblk(1, 2), kblk(1, 1), kblk(1, 0), kblk(2, 2), kblk(2, 1), kblk(2, 0),
                  qblk(0), pl.BlockSpec((QB, 128), lambda i: (qi(i), 0)), qblk(1),
                  pl.BlockSpec((N_HEADS, QB, KB), lambda i: (0, 0, 0)), pl.BlockSpec((1, D_ATTN), lambda i: (0, 0))],
        out_specs=[qblk(0), done, done, pl.BlockSpec((N_HEADS, width), lambda i: (0, 0)),
                   pl.BlockSpec((1, D_ATTN), lambda i: (0, 0))],
        out_shape=[jax.ShapeDtypeStruct((s, D_ATTN), BF16)] * 3
        + [jax.ShapeDtypeStruct((N_HEADS, width), F32), jax.ShapeDtypeStruct((1, D_ATTN), F32)],
        scratch_shapes=[pltpu.VMEM((KB, D_ATTN), F32), pltpu.VMEM((KB, D_ATTN), F32), pltpu.VMEM((N_HEADS, QB, KB), F32)],
        compiler_params=_params(("arbitrary",), 56),
    )(qkv, qkv, qkv, qkv, qkv, qkv, qkv, o, lse, dy, bias, gg)


def _block_diag(pw):
    out = jnp.zeros((D_POOL, D_POOL), pw.dtype)
    for gi in range(N_POOL):
        out = lax.dynamic_update_slice(out, pw[gi], (gi * POOL_GC, gi * POOL_GC))
    return out


_SMALL = ("pool_w", "pool_scale", "rel_bias", "group_gain", "pre_mix_g", "post_mix_g", "pre_ffn_g", "post_ffn_g")


def _pack(parts, rows):
    flat = jnp.concatenate([p.reshape(-1).astype(F32) for p in parts])
    return jnp.pad(flat, (0, rows * D_MODEL - flat.shape[0])).reshape(rows, D_MODEL)


def _unpack(packed, shapes):
    flat = packed.reshape(-1)
    out, at = [], 0
    for shp in shapes:
        size = int(np.prod(shp))
        out.append(flat[at:at + size].reshape(shp))
        at += size
    return out


def kernel(x, w_in, w_out, conv_w, pool_w, pool_scale, rel_bias, group_gain, pre_mix_g, post_mix_g, pre_ffn_g, post_ffn_g, w_gate_up, w_down, loss_target, m_w_in, m_w_out, m_conv_w, m_pool_w, m_pool_scale, m_rel_bias, m_group_gain, m_pre_mix_g, m_post_mix_g, m_pre_ffn_g, m_post_ffn_g, m_w_gate_up, m_w_down, v_w_in, v_w_out, v_conv_w, v_pool_w, v_pool_scale, v_rel_bias, v_group_gain, v_pre_mix_g, v_post_mix_g, v_pre_ffn_g, v_post_ffn_g, v_w_gate_up, v_w_down):
    depth = w_in.shape[0]
    s = x.shape[1]
    c_me = lax.axis_index("c")
    q_me = 2 * lax.axis_index("x") + lax.axis_index("y")
    dev = 2 * q_me + c_me

    started = {}
    for l in range(depth):
        started[l, "a"] = gather_start(f"gather_a{l}_s", [w_in[l].astype(BF16), w_out[l].astype(BF16), conv_w[l]])
        started[l, "b"] = gather_start(f"gather_b{l}_s", [w_gate_up[l].astype(BF16), w_down[l].astype(BF16)])

    def weights_a(relayed, l, after):
        g_in, g_out, g_cw = gather_finish(f"gather_a{l}_f", relayed, after)
        return (assemble_cols("assemble_in", *g_in), assemble_rows("assemble_out", *g_out),
                assemble_cols("assemble_conv", *g_cw))

    def weights_b(relayed, l, after):
        g_gu, g_dn = gather_finish(f"gather_b{l}_f", relayed, after)
        return assemble_cols("assemble_gu", *g_gu), assemble_rows("assemble_down", *g_dn)

    h = x.reshape(s, D_MODEL)
    relay_a = gather_relay("gather_a0_r", started[0, "a"], h)
    wa = weights_a(relay_a, 0, relay_a["token"])
    saved = []
    for l in range(depth):
        vec = lambda p: p[l].reshape(1, -1)
        wf_in, wf_out, cw_full = wa
        xn = rmsnorm_fwd("norm_mix", h, vec(pre_mix_g), None, BF16)
        proj_a = matmul("proj_a", xn, wf_in, "nn", F32, s, D_MODEL, D_MODEL, 1024, 512, D_MODEL)
        qkv = matmul("proj_qkv", xn, wf_in, "nn", BF16, s, 3 * D_ATTN, D_MODEL, 1024, 512, D_MODEL, b_off=(0, 2))
        relay_b = gather_relay(f"gather_b{l}_r", started[l, "b"], qkv)
        bd = _block_diag(pool_w[l]).astype(BF16)
        bias = attention_bias(rel_bias[l])
        gg = vec(group_gain)
        y_ab = conv_pool_fwd("conv_pool_fwd", proj_a, cw_full, bd, vec(pool_scale), gg[:, :2 * D_CONV],
                             deps=[relay_b["token"]])
        y, o, lse = attention_fwd("attention_fwd", qkv, bias, gg[:, 2 * D_CONV:], y_ab)
        wf_gu, wf_dn = weights_b(relay_b, l, y)
        mix = matmul("mix_out", y, wf_out, "nn", F32, s, D_MODEL, D_MODEL, 1024, 512, D_MODEL)
        if l + 1 < depth:
            relay_a = gather_relay(f"gather_a{l + 1}_r", started[l + 1, "a"], mix)
        h_mid = rmsnorm_fwd("norm_post", mix, vec(post_mix_g), h, F32,
                            deps=[relay_a["token"]] if l + 1 < depth else [])
        hn = rmsnorm_fwd("norm_ffn", h_mid, vec(pre_ffn_g), None, BF16)
        g, u, ff = matmul_swiglu("gate_up", hn, wf_gu)
        if l + 1 < depth:
            wa = weights_a(relay_a, l + 1, ff)
        ffo = matmul("ffn_down", ff, wf_dn, "nn", F32, s, D_MODEL, D_FF, 512, 512, D_FF)
        h_out = rmsnorm_fwd("norm_post", ffo, vec(post_ffn_g), h_mid, F32)
        saved.append((h, xn, proj_a, qkv, bd, bias, y, o, lse, mix, h_mid, hn, g, u, ff, ffo,
                      wf_in, wf_out, cw_full, wf_gu, wf_dn))
        h = h_out

    dh, loss_part = loss_head("loss_head", h, loss_target.reshape(s, D_MODEL))

    def reduce_begin(name, halves):
        return [k for k, _ in halves], exchange_start(name, plan_swap_cores, len(halves), [snd for _, snd in halves],
                                                      [(k.shape, k.dtype) for k, _ in halves])

    def reduce_relay(name, kept, swapped, after):
        _, got = exchange_wait(name + "_w", swapped, after)
        sums = [add_pairs("reduce_add", k, r) for k, r in zip(kept, got)]
        return exchange_start(name + "_s", plan_to_chips, 3 * len(sums), sums, [(a.shape, a.dtype) for a in sums])

    def reduce_finish(name, relayed, after):
        sums, got = exchange_wait(name, relayed, after)
        return list(zip(sums, got))

    small = {k: [None] * depth for k in _SMALL + ("conv_w",)}
    relayed = {}
    for l in reversed(range(depth)):
        vec = lambda p: p[l].reshape(1, -1)
        (h_in, xn, proj_a, qkv, bd, bias, y, o, lse, mix, h_mid, hn, g, u, ff, ffo,
         wf_in, wf_out, cw_full, wf_gu, wf_dn) = saved[l]
        gg = vec(group_gain)
        dffo, small["post_ffn_g"][l] = rmsnorm_bwd("norm_bwd_b", ffo, vec(post_ffn_g), dh, None, BF16)
        gw_dn = matmul("wgrad_down", ff, dffo, "tn", BF16, D_FF, D_MODEL, s, 1408, D_MODEL, 1024)
        dg, du = matmul_swiglu_bwd("dgrad_down", dffo, wf_dn, g, u)
        gw_gu = matmul("wgrad_gu", hn, [dg, du], "tn", BF16, D_MODEL, 2 * D_FF, s, D_MODEL, 1408, 1024)
        kept_b, swap_b = reduce_begin(f"reduce_b{l}_s", [split_cols("split_gu", gw_gu), split_rows("split_down", gw_dn)])
        dhn = matmul("dgrad_gu", [dg, du], wf_gu, "nt", F32, s, D_MODEL, 2 * D_FF, 1024, 1024, 1408,
                     deps=[swap_b["token"]])
        relayed[l, "b"] = reduce_relay(f"reduce_b{l}_r", kept_b, swap_b, dhn)
        dh_mid, small["pre_ffn_g"][l] = rmsnorm_bwd("norm_bwd_r", h_mid, vec(pre_ffn_g), dhn, dh, F32,
                                                    deps=[relayed[l, "b"]["token"]])
        dmix, small["post_mix_g"][l] = rmsnorm_bwd("norm_bwd_b", mix, vec(post_mix_g), dh_mid, None, BF16)
        gw_out = matmul("wgrad_out", y, dmix, "tn", BF16, D_MODEL, D_MODEL, s, D_MODEL, D_MODEL, 1024)
        dy = matmul("dgrad_out", dmix, wf_out, "nt", BF16, s, D_MODEL, D_MODEL, 1024, 512, D_MODEL)
        dpa, dcw, dbd, dps, dgg_ab = conv_pool_bwd("conv_pool_bwd", proj_a, dy, cw_full, bd, vec(pool_scale),
                                                   gg[:, :2 * D_CONV])
        dq, dk, dv, by_off, dgg_c = attention_bwd("attention_bwd", qkv, o, lse, dy, bias, gg[:, 2 * D_CONV:])
        dparts = [dpa, dq, dk, dv]
        gw_in = matmul("wgrad_in", xn, dparts, "tn", BF16, D_MODEL, 5 * D_ATTN, s, D_MODEL, 512, min(s, 2048))
        kept_a, swap_a = reduce_begin(f"reduce_a{l}_s", [split_cols("split_in", gw_in), split_rows("split_out", gw_out)])
        dxn = matmul("dgrad_in", dparts, wf_in, "nt", F32, s, D_MODEL, 5 * D_ATTN, 1024, 1024, 512,
                     deps=[swap_a["token"]])
        relayed[l, "a"] = reduce_relay(f"reduce_a{l}_r", kept_a, swap_a, dxn)
        dh, small["pre_mix_g"][l] = rmsnorm_bwd("norm_bwd_r", h_in, vec(pre_mix_g), dxn, dh_mid, F32,
                                                deps=[relayed[l, "a"]["token"]])
        small["conv_w"][l] = dcw
        small["pool_w"][l] = jnp.stack([dbd[gi * POOL_GC:(gi + 1) * POOL_GC, gi * POOL_GC:(gi + 1) * POOL_GC]
                                        for gi in range(N_POOL)])
        small["pool_scale"][l] = dps
        small["rel_bias"][l] = rel_bias_grad(by_off)
        small["group_gain"][l] = jnp.concatenate([dgg_ab, dgg_c], axis=1)
    grad_x = dh.reshape(x.shape)

    small_params = dict(pool_w=pool_w, pool_scale=pool_scale, rel_bias=rel_bias, group_gain=group_gain,
                        pre_mix_g=pre_mix_g, post_mix_g=post_mix_g, pre_ffn_g=pre_ffn_g, post_ffn_g=post_ffn_g)
    small_m = dict(pool_w=m_pool_w, pool_scale=m_pool_scale, rel_bias=m_rel_bias, group_gain=m_group_gain,
                   pre_mix_g=m_pre_mix_g, post_mix_g=m_post_mix_g, pre_ffn_g=m_pre_ffn_g, post_ffn_g=m_post_ffn_g)
    small_v = dict(pool_w=v_pool_w, pool_scale=v_pool_scale, rel_bias=v_rel_bias, group_gain=v_group_gain,
                   pre_mix_g=v_pre_mix_g, post_mix_g=v_post_mix_g, pre_ffn_g=v_pre_ffn_g, post_ffn_g=v_post_ffn_g)
    shapes = [small_params[k].shape for k in _SMALL] + [(depth, 3, D_CONV), (1,)]
    n_small = sum(int(np.prod(shp)) for shp in shapes)
    rows = -(-n_small // (8 * D_MODEL)) * 8
    extra = [jnp.zeros((depth, 3, D_CONV), F32), jnp.zeros((1,), F32)]
    grads_packed = _pack([jnp.stack(small[k]) for k in _SMALL] + [jnp.stack(small["conv_w"]), loss_part[0, 0:1]], rows)
    small_started = gather_start("gather_small_s", [grads_packed])

    big = dict(w_in=(w_in, m_w_in, v_w_in), w_out=(w_out, m_w_out, v_w_out),
               w_gate_up=(w_gate_up, m_w_gate_up, v_w_gate_up), w_down=(w_down, m_w_down, v_w_down))
    results = dict(w_in=None, w_out=None, w_gate_up=None, w_down=None)
    after = small_started["token"]
    small_relay = None
    for l in reversed(range(depth)):
        slabs_gu, slabs_dn = reduce_finish(f"reduce_b{l}_f", relayed[l, "b"], after)
        slabs_in, slabs_out = reduce_finish(f"reduce_a{l}_f", relayed[l, "a"], slabs_dn[1])
        for k, slabs in (("w_gate_up", slabs_gu), ("w_down", slabs_dn), ("w_in", slabs_in), ("w_out", slabs_out)):
            results[k] = adamw_layer("adamw_" + k, l, *big[k], *slabs, results[k])
        after = results["w_out"][0]
        if small_relay is None:
            small_relay = gather_relay("gather_small_r", small_started, after)
    small_parts = gather_finish("gather_small_f", small_relay, after)[0]
    all_small = assemble_rows("assemble_small", *small_parts).reshape(8, rows, D_MODEL)
    res_small = adamw("adamw_small", _pack([small_params[k] for k in _SMALL] + extra, rows),
                      _pack([small_m[k] for k in _SMALL] + extra, rows),
                      _pack([small_v[k] for k in _SMALL] + extra, rows), all_small)
    g_s, d_s, m_s, v_s = (_unpack(r, shapes) for r in res_small)
    loss = g_s[-1][0]
    g_conv = lax.dynamic_slice_in_dim(g_s[-2], dev * conv_w.shape[2], conv_w.shape[2], axis=2)
    results["conv_w"] = adamw("adamw_conv", conv_w, m_conv_w, v_conv_w, g_conv[None])

    names = ("w_in", "w_out", "conv_w") + _SMALL + ("w_gate_up", "w_down")
    for j, k in enumerate(_SMALL):
        results[k] = (g_s[j], d_s[j], m_s[j], v_s[j])
    return (loss, grad_x, *[results[k][0] for k in names], *[results[k][1] for k in names],
            *[results[k][2] for k in names], *[results[k][3] for k in names])
```

```python
import numpy as np
import jax
import jax.numpy as jnp
from jax import lax
from jax.experimental import pallas as pl
from jax.experimental.pallas import tpu as pltpu

F32 = jnp.float32
BF16 = jnp.bfloat16

CHUNK = 64
D_MODEL = 1024
D_CONV = 256
D_POOL = 256
D_ATTN = 512
HEAD_DIM = 64
N_HEADS = 8
N_POOL = 4
POOL_GC = 64
POOL_WINDOWS = (2, 4, 8, 16)
LEFT_CHUNKS = 8
REL_CLIP = 128
D_FF = 2816
EPS = 1e-6
ADAM_LR, ADAM_B1, ADAM_B2, ADAM_EPS, ADAM_WD, ADAM_STEP = 0.001, 0.9, 0.999, 1e-08, 0.01, 10

QB = 256
KB = 3 * QB
HALO = 16
T_CP = 512
T_ROW = 512
NEG = -1e30
VMEM_MB = 1 << 20
MESH = pl.DeviceIdType.MESH


def _call(body, **kw):
    return pl.pallas_call(body, **kw)


def _call_after(deps, n_in, body, **kw):
    deps = tuple(deps)
    if not deps:
        return _call(body, **kw)

    def ordered(*refs):
        body(*refs[:n_in], *refs[n_in + len(deps):])

    kw["in_specs"] = list(kw["in_specs"]) + [pl.BlockSpec(memory_space=pl.ANY)] * len(deps)
    call = _call(ordered, **kw)
    return lambda *args: call(*args, *deps)


def _params(sem, vmem_mb=48):
    return pltpu.CompilerParams(dimension_semantics=sem, vmem_limit_bytes=vmem_mb * VMEM_MB)


_CHIP_FLIPS = ((1, 0), (0, 1), (1, 1))
_HBM = pl.BlockSpec(memory_space=pltpu.HBM)
_SEM = pl.BlockSpec(memory_space=pltpu.SEMAPHORE)
_EFFECT = pltpu.SideEffectType.DATAFLOW_SIDE_EFFECTING


def _flip(v, f):
    return 1 - v if f else v


def _descriptors(plan, srcs, lands, send_sems, recv_sems):
    x, y, c = lax.axis_index("x"), lax.axis_index("y"), lax.axis_index("c")
    out = []
    for k, (src, dst, peer, land) in enumerate(plan(srcs, lands, x, y, c)):
        kw = dict(send_sem=send_sems.at[k], recv_sem=recv_sems.at[k], device_id=peer, device_id_type=MESH)
        out.append((pltpu.make_async_remote_copy(src_ref=src, dst_ref=dst, **kw),
                    pltpu.make_async_remote_copy(src_ref=src, dst_ref=land, **kw)))
    return out


def exchange_start(name, plan, n_copies, srcs, land_shapes):
    ns, nl = len(srcs), len(land_shapes)

    def body(*refs):
        src_refs, land_refs = refs[:ns], refs[ns:ns + nl]
        send_sems, recv_sems = refs[ns + nl], refs[ns + nl + 1]
        for send, _ in _descriptors(plan, src_refs, land_refs, send_sems, recv_sems):
            send.start()
        refs[-1][...] = jnp.zeros_like(refs[-1])

    lands = [pltpu.with_memory_space_constraint(lax.empty(shape, dtype), pltpu.HBM) for shape, dtype in land_shapes]
    outs = _call(
        body, name=name,
        out_shape=(pltpu.SemaphoreType.DMA((n_copies,)), pltpu.SemaphoreType.DMA((n_copies,)),
                   *[pltpu.HBM(a.shape, a.dtype) for a in srcs], *[pltpu.HBM(shape, dtype) for shape, dtype in land_shapes],
                   jax.ShapeDtypeStruct((8, 128), F32)),
        in_specs=[_HBM] * (ns + nl),
        out_specs=(_SEM, _SEM, *[_HBM] * (ns + nl), pl.BlockSpec(memory_space=pltpu.VMEM)),
        input_output_aliases={j: 2 + j for j in range(ns + nl)},
        compiler_params=pltpu.CompilerParams(has_side_effects=_EFFECT),
    )(*[pltpu.with_memory_space_constraint(a, pltpu.HBM) for a in srcs], *lands)
    return dict(plan=plan, sems=outs[:2], srcs=outs[2:2 + ns], lands=outs[2 + ns:2 + ns + nl], token=outs[-1])


def exchange_wait(name, started, after):
    srcs, lands = started["srcs"], started["lands"]
    ns, nl = len(srcs), len(lands)

    def body(*refs):
        src_refs, land_refs = refs[:ns], refs[ns:ns + nl]
        send_sems, recv_sems = refs[ns + nl], refs[ns + nl + 1]
        for _, wait in _descriptors(started["plan"], src_refs, land_refs, send_sems, recv_sems):
            wait.wait_send()
            wait.wait_recv()

    outs = _call(
        body, name=name,
        out_shape=tuple(pltpu.HBM(a.shape, a.dtype) for a in (*srcs, *lands)),
        in_specs=[_HBM] * (ns + nl) + [_SEM, _SEM, pl.BlockSpec(memory_space=pl.ANY)],
        out_specs=tuple([_HBM] * (ns + nl)),
        input_output_aliases={j: j for j in range(ns + nl)},
        compiler_params=pltpu.CompilerParams(has_side_effects=_EFFECT),
    )(*srcs, *lands, *started["sems"], after)
    return list(outs[:ns]), list(outs[ns:])


def plan_from_chips(srcs, lands, x, y, c):
    q = 2 * x + y
    out = []
    for src, land in zip(srcs, lands):
        for fx, fy in _CHIP_FLIPS:
            px, py = _flip(x, fx), _flip(y, fy)
            out.append((src, land.at[q], (px, py, c), land.at[2 * px + py]))
    return out


def plan_to_core(srcs, lands, x, y, c):
    n = len(lands)
    q = 2 * x + y
    out = []
    for own, chips, land in zip(srcs[:n], srcs[n:], lands):
        out.append((own, land.at[q], (x, y, 1 - c), land.at[q]))
        for fx, fy in _CHIP_FLIPS:
            qp = 2 * _flip(x, fx) + _flip(y, fy)
            out.append((chips.at[qp], land.at[qp], (x, y, 1 - c), land.at[qp]))
    return out


def plan_swap_cores(srcs, lands, x, y, c):
    return [(src, land, (x, y, 1 - c), land) for src, land in zip(srcs, lands)]


def plan_to_chips(srcs, lands, x, y, c):
    q = 2 * x + y
    out = []
    for src, land in zip(srcs, lands):
        for fx, fy in _CHIP_FLIPS:
            px, py = _flip(x, fx), _flip(y, fy)
            out.append((src.at[2 * px + py], land.at[q], (px, py, c), land.at[2 * px + py]))
    return out


def _slots(a):
    return ((4,) + a.shape, a.dtype)


def gather_start(name, arrays):
    return exchange_start(name, plan_from_chips, 3 * len(arrays), arrays, [_slots(a) for a in arrays])


def gather_relay(name, started, after):
    own, chips = exchange_wait(name + "_w", started, after)
    return exchange_start(name + "_s", plan_to_core, 4 * len(own), own + chips, [_slots(a) for a in own])


def gather_finish(name, relayed, after):
    srcs, cores = exchange_wait(name, relayed, after)
    n = len(cores)
    return list(zip(srcs[:n], srcs[n:], cores))


def _place():
    return 2 * lax.axis_index("x") + lax.axis_index("y"), lax.axis_index("c")


def assemble_cols(name, own, chips, core):
    r, c = own.shape
    t = _row_tile(r, 256)

    def body(own_ref, chips_ref, core_ref, o_ref):
        q_me, c_me = _place()
        for q in range(4):
            mine = jnp.where(q == q_me, own_ref[...], chips_ref[q])
            other = core_ref[q]
            o_ref[:, 2 * q * c:(2 * q + 1) * c] = jnp.where(c_me == 0, mine, other)
            o_ref[:, (2 * q + 1) * c:(2 * q + 2) * c] = jnp.where(c_me == 0, other, mine)

    slots = pl.BlockSpec((4, t, c), lambda i: (0, i, 0))
    return _call(
        body, name=name, grid=(r // t,),
        in_specs=[pl.BlockSpec((t, c), lambda i: (i, 0)), slots, slots],
        out_specs=pl.BlockSpec((t, 8 * c), lambda i: (i, 0)),
        out_shape=jax.ShapeDtypeStruct((r, 8 * c), own.dtype), compiler_params=_params(("parallel",)),
    )(own, chips, core)


def assemble_rows(name, own, chips, core):
    r, c = own.shape

    def body(own_ref, chips_ref, core_ref, o_ref):
        q_me, c_me = _place()
        d = pl.program_id(0)
        mine = jnp.where(d // 2 == q_me, own_ref[...], chips_ref[...])
        o_ref[...] = jnp.where(d % 2 == c_me, mine, core_ref[...])

    slot = pl.BlockSpec((None, r, c), lambda d: (d // 2, 0, 0))
    return _call(
        body, name=name, grid=(8,),
        in_specs=[pl.BlockSpec((r, c), lambda d: (0, 0)), slot, slot],
        out_specs=pl.BlockSpec((r, c), lambda d: (d, 0)),
        out_shape=jax.ShapeDtypeStruct((8 * r, c), own.dtype), compiler_params=_params(("parallel",)),
    )(own, chips, core)


def split_rows(name, dw):
    r8, c = dw.shape
    r = r8 // 8

    def body(dw_ref, keep_ref, send_ref):
        _, c_me = _place()
        d = pl.program_id(0)

        @pl.when(d % 2 == c_me)
        def _():
            keep_ref[...] = dw_ref[...]

        @pl.when(d % 2 != c_me)
        def _():
            send_ref[...] = dw_ref[...]

    slot = pl.BlockSpec((None, r, c), lambda d: (d // 2, 0, 0))
    return _call(
        body, name=name, grid=(8,),
        in_specs=[pl.BlockSpec((r, c), lambda d: (d, 0))], out_specs=[slot, slot],
        out_shape=[jax.ShapeDtypeStruct((4, r, c), dw.dtype)] * 2, compiler_params=_params(("arbitrary",)),
    )(dw)


def _rms(v):
    return lax.rsqrt(jnp.mean(v * v, axis=-1, keepdims=True) + EPS)


def rmsnorm_fwd(name, x, g, res, out_dtype, deps=()):
    s, d = x.shape
    has_res = res is not None

    def body(*refs):
        if has_res:
            x_ref, g_ref, r_ref, o_ref = refs
        else:
            x_ref, g_ref, o_ref = refs
        xv = x_ref[...]
        y = (xv * _rms(xv)) * g_ref[...]
        if has_res:
            y = y + r_ref[...]
        o_ref[...] = y.astype(o_ref.dtype)

    row = pl.BlockSpec((T_ROW, d), lambda i: (i, 0))
    vec = pl.BlockSpec((1, d), lambda i: (0, 0))
    args = (x, g, res) if has_res else (x, g)
    return _call_after(
        deps, len(args), body, name=name, grid=(s // T_ROW,),
        in_specs=[row, vec] + ([row] if has_res else []), out_specs=row,
        out_shape=jax.ShapeDtypeStruct((s, d), out_dtype),
        compiler_params=_params(("parallel",)),
    )(*args)


def _norm_bwd(xv, r, g, dy):
    a = dy * g
    dx = r * (a - xv * ((r * r) * jnp.mean(a * xv, axis=-1, keepdims=True)))
    return dx, dy * (xv * r)


def rmsnorm_bwd(name, x, g, dy, res, out_dtype, deps=()):
    s, d = x.shape
    has_res = res is not None

    def body(*refs):
        if has_res:
            x_ref, g_ref, dy_ref, r_ref, dx_ref, dg_ref = refs
        else:
            x_ref, g_ref, dy_ref, dx_ref, dg_ref = refs
        xv = x_ref[...]
        dx, dgt = _norm_bwd(xv, _rms(xv), g_ref[...], dy_ref[...].astype(F32))
        if has_res:
            dx = dx + r_ref[...]
        dx_ref[...] = dx.astype(dx_ref.dtype)

        @pl.when(pl.program_id(0) == 0)
        def _():
            dg_ref[...] = jnp.zeros_like(dg_ref)
        dg_ref[...] += jnp.sum(dgt, axis=0, keepdims=True)

    row = pl.BlockSpec((T_ROW, d), lambda i: (i, 0))
    vec = pl.BlockSpec((1, d), lambda i: (0, 0))
    args = (x, g, dy, res) if has_res else (x, g, dy)
    return _call_after(
        deps, len(args), body, name=name, grid=(s // T_ROW,),
        in_specs=[row, vec, row] + ([row] if has_res else []), out_specs=[row, vec],
        out_shape=[jax.ShapeDtypeStruct((s, d), out_dtype), jax.ShapeDtypeStruct((1, d), F32)],
        compiler_params=_params(("arbitrary",)),
    )(*args)


def loss_head(name, h, tgt):
    s, d = h.shape

    def body(h_ref, t_ref, dh_ref, l_ref):
        e = h_ref[...] - t_ref[...]
        dh_ref[...] = e * (1.0 / d)

        @pl.when(pl.program_id(0) == 0)
        def _():
            l_ref[...] = jnp.zeros_like(l_ref)
        part = 0.5 * jnp.sum(jnp.mean(e * e, axis=-1, keepdims=True), axis=0, keepdims=True)
        l_ref[...] += jnp.broadcast_to(part, l_ref.shape)

    row = pl.BlockSpec((T_ROW, d), lambda i: (i, 0))
    return _call(
        body, name=name, grid=(s // T_ROW,),
        in_specs=[row, row], out_specs=[row, pl.BlockSpec((1, 128), lambda i: (0, 0))],
        out_shape=[jax.ShapeDtypeStruct((s, d), F32), jax.ShapeDtypeStruct((1, 128), F32)],
        compiler_params=_params(("arbitrary",)),
    )(h, tgt)


def _row_tile(rows, limit=512):
    t = min(rows, limit)
    while rows % t or (t % 8 and t != rows):
        t -= 1
    return t


def add_pairs(name, a, b):
    shape = a.shape
    a2, b2 = a.reshape(-1, shape[-1]), b.reshape(-1, shape[-1])
    rows, cols = a2.shape
    t = _row_tile(rows)

    def body(a_ref, b_ref, o_ref):
        o_ref[...] = (a_ref[...].astype(F32) + b_ref[...].astype(F32)).astype(o_ref.dtype)

    blk = pl.BlockSpec((t, cols), lambda i: (i, 0))
    out = _call(
        body, name=name, grid=(rows // t,), in_specs=[blk, blk], out_specs=blk,
        out_shape=jax.ShapeDtypeStruct((rows, cols), a.dtype), compiler_params=_params(("parallel",)),
    )(a2, b2)
    return out.reshape(shape)


def _adamw_math(w, m, v, g):
    c1 = 1.0 - ADAM_B1 ** ADAM_STEP
    c2 = 1.0 - ADAM_B2 ** ADAM_STEP
    mn = ADAM_B1 * m + (1.0 - ADAM_B1) * g
    vn = ADAM_B2 * v + (1.0 - ADAM_B2) * (g * g)
    delta = -ADAM_LR * ((mn / c1) / (jnp.sqrt(vn / c2) + ADAM_EPS) + ADAM_WD * w)
    return delta, mn, vn


def _slab_sum(g_ref, n):
    g = g_ref[0].astype(F32)
    for j in range(1, n):
        g = g + g_ref[j].astype(F32)
    return g


def adamw(name, w, m, v, g_slabs, deps=()):
    shape = w.shape
    n = g_slabs.shape[0]
    w2, m2, v2 = (t.reshape(-1, shape[-1]) for t in (w, m, v))
    g3 = g_slabs.reshape(n, -1, shape[-1])
    rows, cols = w2.shape
    t = _row_tile(rows, 256)

    def body(w_ref, m_ref, v_ref, g_ref, go_ref, d_ref, mo_ref, vo_ref):
        g = _slab_sum(g_ref, n)
        go_ref[...] = g
        d_ref[...], mo_ref[...], vo_ref[...] = _adamw_math(w_ref[...], m_ref[...], v_ref[...], g)

    blk = pl.BlockSpec((t, cols), lambda i: (i, 0))
    outs = _call_after(
        deps, 4, body, name=name, grid=(rows // t,),
        in_specs=[blk, blk, blk, pl.BlockSpec((n, t, cols), lambda i: (0, i, 0))], out_specs=[blk] * 4,
        out_shape=[jax.ShapeDtypeStruct((rows, cols), F32)] * 4, compiler_params=_params(("parallel",)),
    )(w2, m2, v2, g3)
    return tuple(o.reshape(shape) for o in outs)


def adamw_layer(name, l, w, m, v, own_sums, chip_sums, into):
    _, rows, cols = w.shape
    t = _row_tile(rows, 256)
    if into is None:
        into = tuple(lax.empty(w.shape, F32) for _ in range(4))

    def body(w_ref, m_ref, v_ref, own_ref, far_ref, a0, a1, a2, a3, go_ref, d_ref, mo_ref, vo_ref):
        del a0, a1, a2, a3
        q_me, _ = _place()
        g = jnp.zeros((t, cols), F32)
        for q in range(4):
            g = g + jnp.where(q == q_me, own_ref[q], far_ref[q]).astype(F32)
        go_ref[...] = g
        d_ref[...], mo_ref[...], vo_ref[...] = _adamw_math(w_ref[...], m_ref[...], v_ref[...], g)

    blk = pl.BlockSpec((None, t, cols), lambda i: (l, i, 0))
    slabs = pl.BlockSpec((4, t, cols), lambda i: (0, i, 0))
    anyw = pl.BlockSpec(memory_space=pl.ANY)
    return _call(
        body, name=name, grid=(rows // t,),
        in_specs=[blk, blk, blk, slabs, slabs] + [anyw] * 4, out_specs=[blk] * 4,
        out_shape=[jax.ShapeDtypeStruct(w.shape, F32)] * 4, input_output_aliases={5: 0, 6: 1, 7: 2, 8: 3},
        compiler_params=_params(("parallel",)),
    )(w, m, v, own_sums, chip_sums, *into)


_DIMS = {"nn": (((1,), (0,)), ((), ())), "nt": (((1,), (1,)), ((), ())), "tn": (((0,), (0,)), ((), ()))}


def _dot(a, b, mode="nn"):
    return lax.dot_general(a, b, _DIMS[mode], preferred_element_type=F32)


def matmul(name, a, b, mode, out_dtype, m, n, k, tm, tn, tk, b_off=(0, 0), deps=()):
    gm, gn, gk = m // tm, n // tn, k // tk
    assert gm * tm == m and gn * tn == n and gk * tk == k
    r0, c0 = b_off
    a_parts = list(a) if isinstance(a, (list, tuple)) else [a]
    na = len(a_parts)
    tile = tm if mode == "tn" else tk
    spans, at = [], 0
    for p in a_parts:
        nblk = p.shape[1] // tile
        assert nblk * tile == p.shape[1]
        spans.append((at, nblk))
        at += nblk
    assert at == (gm if mode == "tn" else gk)

    def within(t, span):
        return (t >= span[0]) & (t < span[0] + span[1])

    def local(t, span):
        return jnp.clip(t - span[0], 0, span[1] - 1)

    a_specs = []
    for sp in spans:
        if mode == "tn":
            a_specs.append(pl.BlockSpec((tk, tm), lambda i, j, kk, sp=sp: (jnp.where(within(i, sp), kk, 0), local(i, sp))))
        else:
            a_specs.append(pl.BlockSpec((tm, tk), lambda i, j, kk, sp=sp: (i, local(kk, sp))))
    if mode == "nt":
        b_spec = pl.BlockSpec((tn, tk), lambda i, j, kk: (j + r0, kk + c0))
    else:
        b_spec = pl.BlockSpec((tk, tn), lambda i, j, kk: (kk + r0, j + c0))
    o_spec = pl.BlockSpec((tm, tn), lambda i, j, kk: (i, j))
    single = na == 1 and gk == 1

    def body(*refs):
        a_refs, b_ref, o_ref = refs[:na], refs[na], refs[na + 1]
        if single:
            o_ref[...] = _dot(a_refs[0][...], b_ref[...], mode).astype(o_ref.dtype)
            return
        acc_ref = refs[-1]
        i, kk = pl.program_id(0), pl.program_id(2)

        @pl.when(kk == 0)
        def _():
            acc_ref[...] = jnp.zeros_like(acc_ref)

        for pa, sp in enumerate(spans):
            def add(pa=pa):
                acc_ref[...] += _dot(a_refs[pa][...], b_ref[...], mode)
            if na > 1:
                pl.when(within(i if mode == "tn" else kk, sp))(add)
            else:
                add()

        @pl.when(kk == gk - 1)
        def _():
            o_ref[...] = acc_ref[...].astype(o_ref.dtype)

    return _call_after(
        deps, na + 1, body, name=name, grid=(gm, gn, gk),
        in_specs=a_specs + [b_spec], out_specs=o_spec,
        out_shape=jax.ShapeDtypeStruct((m, n), out_dtype),
        scratch_shapes=[] if single else [pltpu.VMEM((tm, tn), F32)],
        compiler_params=_params(("parallel", "parallel", "arbitrary")),
    )(*a_parts, b)


def matmul_swiglu(name, a, w_gu_t, deps=()):
    s, d = a.shape
    tm, tn = 512, 1408
    gn = D_FF // tn

    def body(a_ref, wg_ref, wu_ref, g_ref, u_ref, f_ref):
        av = a_ref[...]
        g = _dot(av, wg_ref[...], "nt")
        u = _dot(av, wu_ref[...], "nt")
        g_ref[...] = g.astype(g_ref.dtype)
        u_ref[...] = u.astype(u_ref.dtype)
        f_ref[...] = ((g * (1.0 / (1.0 + jnp.exp(-g)))) * u).astype(f_ref.dtype)

    o_spec = pl.BlockSpec((tm, tn), lambda i, j: (i, j))
    return _call_after(
        deps, 3, body, name=name, grid=(s // tm, gn),
        in_specs=[pl.BlockSpec((tm, d), lambda i, j: (i, 0)), pl.BlockSpec((tn, d), lambda i, j: (j, 0)),
                  pl.BlockSpec((tn, d), lambda i, j: (j + gn, 0))],
        out_specs=[o_spec] * 3, out_shape=[jax.ShapeDtypeStruct((s, D_FF), BF16)] * 3,
        compiler_params=_params(("parallel", "parallel")),
    )(a, w_gu_t, w_gu_t)


def matmul_swiglu_bwd(name, dffo, w_down, g, u):
    s, d = dffo.shape
    tm, tn = 512, 1408

    def body(a_ref, w_ref, g_ref, u_ref, dg_ref, du_ref):
        dff = _dot(a_ref[...], w_ref[...], "nt")
        gv = g_ref[...].astype(F32)
        sig = 1.0 / (1.0 + jnp.exp(-gv))
        du_ref[...] = (dff * (gv * sig)).astype(du_ref.dtype)
        dg_ref[...] = (dff * u_ref[...].astype(F32) * (sig * (1.0 + gv * (1.0 - sig)))).astype(dg_ref.dtype)

    o_spec = pl.BlockSpec((tm, tn), lambda i, j: (i, j))
    return _call(
        body, name=name, grid=(s // tm, D_FF // tn),
        in_specs=[pl.BlockSpec((tm, d), lambda i, j: (i, 0)), pl.BlockSpec((tn, d), lambda i, j: (j, 0)), o_spec, o_spec],
        out_specs=[o_spec] * 2, out_shape=[jax.ShapeDtypeStruct((s, D_FF), BF16)] * 2,
        compiler_params=_params(("parallel", "parallel")),
    )(dffo, w_down, g, u)


def _down(v, n):
    return pltpu.roll(v, n, 0)


def _up(v, n):
    return pltpu.roll(v, v.shape[0] - n, 0)


def _by_window(lane, v2, v4, v8, v16):
    return jnp.where(lane < POOL_GC, v2, jnp.where(lane < 2 * POOL_GC, v4, jnp.where(lane < 3 * POOL_GC, v8, v16)))


def _taps(cw_ref):
    return cw_ref[0:1, :], cw_ref[1:2, :], cw_ref[2:3, :]


def _conv_pool_forward(ext, t0, cw, bd):
    n_out = ext.shape[0] - HALO
    gb = ext[HALO:, 0:D_CONV]
    z = ext[:, D_CONV:2 * D_CONV] * ext[:, 2 * D_CONV:3 * D_CONV]
    z0, z1, z2 = z[HALO:], _down(z, 1)[HALO:], _down(z, 2)[HALO:]
    conv = cw[2] * z0 + cw[1] * z1 + cw[0] * z2
    x = ext[:, 3 * D_CONV:]
    w2 = x + _down(x, 1)
    w4 = w2 + _down(w2, 2)
    w8 = w4 + _down(w4, 4)
    w16 = w8 + _down(w8, 8)
    lane = lax.broadcasted_iota(jnp.int32, (1, D_POOL), 1)
    win = _by_window(lane, 2.0, 4.0, 8.0, 16.0)
    pos = (t0 + lax.broadcasted_iota(jnp.int32, (n_out, 1), 0) + 1).astype(F32)
    cnt = jnp.minimum(pos, win)
    d = _by_window(lane, w2, w4, w8, w16)[HALO:] / cnt - x[HALO:]
    ybp = _dot(d.astype(BF16), bd)
    return gb, z0, z1, z2, conv, d, cnt, ybp, lane


def conv_pool_fwd(name, proj_a, cw, bd, ps, gg, deps=()):
    s = proj_a.shape[0]
    t = T_CP
    hb = t // HALO

    def body(main_ref, prev_ref, cw_ref, bd_ref, ps_ref, gg_ref, y_ref):
        i = pl.program_id(0)
        prev = jnp.where(i > 0, prev_ref[...], 0.0)
        ext = jnp.concatenate([prev, main_ref[...]], axis=0)
        gb, _, _, _, conv, _, _, ybp, _ = _conv_pool_forward(ext, i * t, _taps(cw_ref), bd_ref[...])
        ya = gb * conv
        yb = ybp * ps_ref[...]
        ggv = gg_ref[...]
        y_ref[:, 0:D_CONV] = ((ya * _rms(ya)) * ggv[:, 0:D_CONV]).astype(y_ref.dtype)
        y_ref[:, D_CONV:] = ((yb * _rms(yb)) * ggv[:, D_CONV:]).astype(y_ref.dtype)

    full = lambda shape: pl.BlockSpec(shape, lambda i: (0,) * len(shape))
    return _call_after(
        deps, 6, body, name=name, grid=(s // t,),
        in_specs=[pl.BlockSpec((t, D_MODEL), lambda i: (i, 0)),
                  pl.BlockSpec((HALO, D_MODEL), lambda i: (jnp.maximum(i * hb - 1, 0), 0)),
                  full((3, D_CONV)), full((D_POOL, D_POOL)), full((1, D_POOL)), full((1, 2 * D_CONV))],
        out_specs=pl.BlockSpec((t, 2 * D_CONV), lambda i: (i, 0)),
        out_shape=jax.ShapeDtypeStruct((s, D_MODEL), BF16),
        compiler_params=_params(("parallel",)),
    )(proj_a, proj_a, cw, bd, ps, gg)


def conv_pool_bwd(name, proj_a, dy, cw, bd, ps, gg):
    s = proj_a.shape[0]
    t = T_CP
    hb = t // HALO
    nblk = s // t
    last_halo = s // HALO - 1

    def body(main_ref, prev_ref, next_ref, dy_ref, dyn_ref, cw_ref, bd_ref, ps_ref, gg_ref,
             dp_ref, dcw_ref, dbd_ref, dps_ref, dgg_ref):
        i = pl.program_id(0)
        prev = jnp.where(i > 0, prev_ref[...], 0.0)
        ext = jnp.concatenate([prev, main_ref[...], next_ref[...]], axis=0)
        cwv, bdv, psv, ggv = _taps(cw_ref), bd_ref[...], ps_ref[...], gg_ref[...]
        gb, z0, z1, z2, conv, d, cnt, ybp, lane = _conv_pool_forward(ext, i * t, cwv, bdv)
        dyn = jnp.where(i < nblk - 1, dyn_ref[...].astype(F32), 0.0)
        dyv = jnp.concatenate([dy_ref[...].astype(F32), dyn], axis=0)
        ya = gb * conv
        yb = ybp * psv
        dya, dgg_a = _norm_bwd(ya, _rms(ya), ggv[:, 0:D_CONV], dyv[:, 0:D_CONV])
        dyb, dgg_b = _norm_bwd(yb, _rms(yb), ggv[:, D_CONV:], dyv[:, D_CONV:])

        dconv = dya * gb
        dz = (cwv[2] * dconv + cwv[1] * _up(dconv, 1) + cwv[0] * _up(dconv, 2))[:t]
        main = main_ref[...]
        dp_ref[:, 0:D_CONV] = (dya * conv)[:t].astype(dp_ref.dtype)
        dp_ref[:, D_CONV:2 * D_CONV] = (dz * main[:, 2 * D_CONV:3 * D_CONV]).astype(dp_ref.dtype)
        dp_ref[:, 2 * D_CONV:3 * D_CONV] = (dz * main[:, D_CONV:2 * D_CONV]).astype(dp_ref.dtype)

        dybs = dyb * psv
        dd = _dot(dybs.astype(BF16), bdv, "nt")
        e = dd / cnt
        a2 = e + _up(e, 1)
        a4 = a2 + _up(a2, 2)
        a8 = a4 + _up(a4, 4)
        a16 = a8 + _up(a8, 8)
        dp_ref[:, 3 * D_CONV:] = (_by_window(lane, a2, a4, a8, a16) - dd)[:t].astype(dp_ref.dtype)

        @pl.when(i == 0)
        def _():
            dcw_ref[...] = jnp.zeros_like(dcw_ref)
            dbd_ref[...] = jnp.zeros_like(dbd_ref)
            dps_ref[...] = jnp.zeros_like(dps_ref)
            dgg_ref[...] = jnp.zeros_like(dgg_ref)

        rsum = lambda v: jnp.sum(v[:t], axis=0, keepdims=True)
        dcw_ref[0:1, :] += rsum(dconv * z2)
        dcw_ref[1:2, :] += rsum(dconv * z1)
        dcw_ref[2:3, :] += rsum(dconv * z0)
        dbd_ref[...] += _dot(d[:t].astype(BF16), dybs[:t].astype(BF16), "tn")
        dps_ref[...] += rsum(dyb * ybp)
        dgg_ref[:, 0:D_CONV] += rsum(dgg_a)
        dgg_ref[:, D_CONV:] += rsum(dgg_b)

    full = lambda shape: pl.BlockSpec(shape, lambda i: (0,) * len(shape))
    next_halo = lambda i: (jnp.minimum((i + 1) * hb, last_halo), 0)
    return _call(
        body, name=name, grid=(nblk,),
        in_specs=[pl.BlockSpec((t, D_MODEL), lambda i: (i, 0)),
                  pl.BlockSpec((HALO, D_MODEL), lambda i: (jnp.maximum(i * hb - 1, 0), 0)),
                  pl.BlockSpec((HALO, D_MODEL), next_halo),
                  pl.BlockSpec((t, 2 * D_CONV), lambda i: (i, 0)),
                  pl.BlockSpec((HALO, 2 * D_CONV), next_halo),
                  full((3, D_CONV)), full((D_POOL, D_POOL)), full((1, D_POOL)), full((1, 2 * D_CONV))],
        out_specs=[pl.BlockSpec((t, D_MODEL), lambda i: (i, 0)),
                   full((3, D_CONV)), full((D_POOL, D_POOL)), full((1, D_POOL)), full((1, 2 * D_CONV))],
        out_shape=[jax.ShapeDtypeStruct((s, D_MODEL), BF16), jax.ShapeDtypeStruct((3, D_CONV), F32),
                   jax.ShapeDtypeStruct((D_POOL, D_POOL), F32), jax.ShapeDtypeStruct((1, D_POOL), F32),
                   jax.ShapeDtypeStruct((1, 2 * D_CONV), F32)],
        compiler_params=_params(("arbitrary",)),
    )(proj_a, proj_a, proj_a, dy, dy, cw, bd, ps, gg)


def _band_mask():
    qa = np.arange(QB)[:, None] // CHUNK
    km = np.arange(KB)[None, :] // CHUNK
    return (km - qa >= 0) & (km - qa <= LEFT_CHUNKS)


def attention_bias(rel_bias):
    n_far = 2 * QB - REL_CLIP + 1
    far = jnp.broadcast_to(rel_bias[:, 2 * REL_CLIP:], (N_HEADS, n_far))
    mid = rel_bias[:, 1:2 * REL_CLIP][:, ::-1]
    near = jnp.broadcast_to(rel_bias[:, 0:1], (N_HEADS, KB - n_far - (2 * REL_CLIP - 1)))
    wrap = jnp.broadcast_to(rel_bias[:, 2 * REL_CLIP:], (N_HEADS, 4 * QB - KB))
    by_offset = jnp.concatenate([far, mid, near, wrap], axis=1)
    width = 4 * QB
    toeplitz = jnp.tile(by_offset, (1, QB))[:, :QB * (width - 1)].reshape(N_HEADS, QB, width - 1)[:, :, :KB]
    return jnp.where(jnp.asarray(_band_mask())[None], toeplitz, NEG)


def rel_bias_grad(by_offset):
    n_far = 2 * QB - REL_CLIP + 1
    hi = jnp.sum(by_offset[:, :n_far], axis=1, keepdims=True) + jnp.sum(by_offset[:, KB:], axis=1, keepdims=True)
    mid = by_offset[:, n_far:n_far + 2 * REL_CLIP - 1][:, ::-1]
    lo = jnp.sum(by_offset[:, n_far + 2 * REL_CLIP - 1:KB], axis=1, keepdims=True)
    return jnp.concatenate([lo, mid, hi], axis=1)


def _head_masks():
    first = lax.broadcasted_iota(jnp.int32, (1, 2 * HEAD_DIM), 1) < HEAD_DIM
    return first, jnp.logical_not(first)


def _pick_lane(tile, h):
    lane = lax.broadcasted_iota(jnp.int32, (1, tile.shape[1]), 1)
    return jnp.sum(jnp.where(lane == h, tile, 0.0), axis=-1, keepdims=True)


def attention_fwd(name, qkv, bias, gg, y_ab):
    s = qkv.shape[0]
    nq = s // QB
    scale = HEAD_DIM ** -0.5

    def body(q_ref, k0, k1, k2, v0, v1, v2, b_ref, gg_ref, y_in, y_ref, o_ref, lse_ref):
        del y_in
        i = pl.program_id(0)
        kb = jnp.concatenate([k0[...], k1[...], k2[...]], axis=0)
        vb = jnp.concatenate([v0[...], v1[...], v2[...]], axis=0)
        valid = lax.broadcasted_iota(jnp.int32, (1, KB), 1) >= (2 - i) * QB
        lane = lax.broadcasted_iota(jnp.int32, (1, 128), 1)
        masks = _head_masks()
        lse = jnp.zeros((QB, 128), F32)
        outs = []
        for hp in range(N_HEADS // 2):
            sl = slice(2 * HEAD_DIM * hp, 2 * HEAD_DIM * (hp + 1))
            q_p, k_p, v_p = q_ref[:, sl], kb[:, sl], vb[:, sl]
            o_pair = jnp.zeros((QB, 2 * HEAD_DIM), F32)
            for a in range(2):
                h = 2 * hp + a
                sc = _dot(jnp.where(masks[a], q_p, 0), k_p, "nt") * scale + b_ref[h]
                sc = jnp.where(valid, sc, NEG)
                mx = jnp.max(sc, axis=-1, keepdims=True)
                e = jnp.exp(sc - mx)
                l = jnp.sum(e, axis=-1, keepdims=True)
                o_pair = o_pair + _dot(e.astype(BF16), jnp.where(masks[a], v_p, 0)) * (1.0 / l)
                lse = jnp.where(lane == h, mx + jnp.log(l), lse)
            outs.append(o_pair)
        o = jnp.concatenate(outs, axis=1)
        o_ref[...] = o.astype(o_ref.dtype)
        lse_ref[...] = lse
        y_ref[...] = ((o * _rms(o)) * gg_ref[...]).astype(y_ref.dtype)

    blk = lambda col, back: pl.BlockSpec((QB, D_ATTN), lambda i: (jnp.maximum(i - back, 0), col))
    return _call(
        body, name=name, grid=(nq,),
        in_specs=[blk(0, 0), blk(1, 2), blk(1, 1), blk(1, 0), blk(2, 2), blk(2, 1), blk(2, 0),
                  pl.BlockSpec((N_HEADS, QB, KB), lambda i: (0, 0, 0)), pl.BlockSpec((1, D_ATTN), lambda i: (0, 0)),
                  pl.BlockSpec(memory_space=pl.ANY)],
        out_specs=[pl.BlockSpec((QB, D_ATTN), lambda i: (i, 1)), pl.BlockSpec((QB, D_ATTN), lambda i: (i, 0)),
                   pl.BlockSpec((QB, 128), lambda i: (i, 0))],
        out_shape=[jax.ShapeDtypeStruct((s, D_MODEL), BF16), jax.ShapeDtypeStruct((s, D_ATTN), BF16),
                   jax.ShapeDtypeStruct((s, 128), F32)],
        input_output_aliases={9: 0},
        compiler_params=_params(("arbitrary",), 56),
    )(qkv, qkv, qkv, qkv, qkv, qkv, qkv, bias, gg, y_ab)


def attention_bwd(name, qkv, o, lse, dy, bias, gg):
    s = qkv.shape[0]
    nq = s // QB
    scale = HEAD_DIM ** -0.5
    width = 4 * QB

    def body(q_ref, k0, k1, k2, v0, v1, v2, o_ref, lse_ref, dy_ref, b_ref, gg_ref,
             dq_ref, dk_ref, dv_ref, off_ref, dgg_ref, dk_acc, dv_acc, db_acc):
        i = pl.program_id(0)

        @pl.when(i == 0)
        def _():
            dk_acc[...] = jnp.zeros_like(dk_acc)
            dv_acc[...] = jnp.zeros_like(dv_acc)
            db_acc[...] = jnp.zeros_like(db_acc)
            dgg_ref[...] = jnp.zeros_like(dgg_ref)

        @pl.when(i > 0)
        def _():
            for acc in (dk_acc, dv_acc):
                kept = acc[QB:, :]
                acc[0:2 * QB, :] = kept
                acc[2 * QB:, :] = jnp.zeros((QB, D_ATTN), F32)

        @pl.when(i < nq)
        def _():
            ov = o_ref[...].astype(F32)
            dyv = dy_ref[...].astype(F32)
            do, dgg_t = _norm_bwd(ov, _rms(ov), gg_ref[...], dyv)
            dgg_ref[...] += jnp.sum(dgg_t, axis=0, keepdims=True)
            kb = jnp.concatenate([k0[...], k1[...], k2[...]], axis=0)
            vb = jnp.concatenate([v0[...], v1[...], v2[...]], axis=0)
            valid = lax.broadcasted_iota(jnp.int32, (1, KB), 1) >= (2 - i) * QB
            masks = _head_masks()
            lse_t = lse_ref[...]
            for hp in range(N_HEADS // 2):
                sl = slice(2 * HEAD_DIM * hp, 2 * HEAD_DIM * (hp + 1))
                q_p, k_p, v_p = q_ref[:, sl], kb[:, sl], vb[:, sl]
                do_p = do[:, sl]
                prod = do_p * ov[:, sl]
                do_b = do_p.astype(BF16)
                dq_pair = jnp.zeros((QB, 2 * HEAD_DIM), F32)
                dk_pair = jnp.zeros((KB, 2 * HEAD_DIM), F32)
                dv_pair = jnp.zeros((KB, 2 * HEAD_DIM), F32)
                for a in range(2):
                    h = 2 * hp + a
                    q_m = jnp.where(masks[a], q_p, 0)
                    do_m = jnp.where(masks[a], do_b, 0)
                    sc = _dot(q_m, k_p, "nt") * scale + b_ref[h]
                    sc = jnp.where(valid, sc, NEG)
                    p = jnp.exp(sc - _pick_lane(lse_t, h))
                    dp = _dot(do_m, v_p, "nt")
                    delta = jnp.sum(jnp.where(masks[a], prod, 0.0), axis=-1, keepdims=True)
                    ds = p * (dp - delta)
                    db_acc[h] += ds
                    ds_b = (ds * scale).astype(BF16)
                    dq_pair = dq_pair + _dot(ds_b, jnp.where(masks[a], k_p, 0))
                    dk_pair = dk_pair + _dot(ds_b, q_m, "tn")
                    dv_pair = dv_pair + _dot(p.astype(BF16), do_m, "tn")
                dq_ref[:, sl] = dq_pair.astype(dq_ref.dtype)
                dk_acc[:, sl] += dk_pair
                dv_acc[:, sl] += dv_pair

        dk_ref[...] = dk_acc[0:QB, :].astype(dk_ref.dtype)
        dv_ref[...] = dv_acc[0:QB, :].astype(dv_ref.dtype)

        @pl.when(i == nq + 1)
        def _():
            row = lax.broadcasted_iota(jnp.int32, (QB, 1), 0)
            for h in range(N_HEADS):
                v = jnp.concatenate([db_acc[h], jnp.zeros((QB, width - KB), F32)], axis=1)
                for bit in range(QB.bit_length() - 1):
                    v = jnp.where(((row >> bit) & 1) == 1, pltpu.roll(v, width - (1 << bit), 1), v)
                off_ref[h:h + 1, :] = jnp.sum(v, axis=0, keepdims=True)

    qi = lambda i: jnp.minimum(i, nq - 1)
    kblk = lambda col, back: pl.BlockSpec((QB, D_ATTN), lambda i: (jnp.clip(i - back, 0, nq - 1), col))
    qblk = lambda col: pl.BlockSpec((QB, D_ATTN), lambda i: (qi(i), col))
    done = pl.BlockSpec((QB, D_ATTN), lambda i: (jnp.clip(i - 2, 0, nq - 1), 0))
    return _call(
        body, name=name, grid=(nq + 2,),
        in_specs=[qblk(0), kblk(1, 2), kblk(1, 1), kblk(1, 0), kblk(2, 2), kblk(2, 1), kblk(2, 0),
                  qblk(0), pl.BlockSpec((QB, 128), lambda i: (qi(i), 0)), qblk(1),
                  pl.BlockSpec((N_HEADS, QB, KB), lambda i: (0, 0, 0)), pl.BlockSpec((1, D_ATTN), lambda i: (0, 0))],
        out_specs=[qblk(0), done, done, pl.BlockSpec((N_HEADS, width), lambda i: (0, 0)),
                   pl.BlockSpec((1, D_ATTN), lambda i: (0, 0))],
        out_shape=[jax.ShapeDtypeStruct((s, D_ATTN), BF16)] * 3
        + [jax.ShapeDtypeStruct((N_HEADS, width), F32), jax.ShapeDtypeStruct((1, D_ATTN), F32)],
        scratch_shapes=[pltpu.VMEM((KB, D_ATTN), F32), pltpu.VMEM((KB, D_ATTN), F32), pltpu.VMEM((N_HEADS, QB, KB), F32)],
        compiler_params=_params(("arbitrary",), 56),
    )(qkv, qkv, qkv, qkv, qkv, qkv, qkv, o, lse, dy, bias, gg)


def _block_diag(pw):
    out = jnp.zeros((D_POOL, D_POOL), pw.dtype)
    for gi in range(N_POOL):
        out = lax.dynamic_update_slice(out, pw[gi], (gi * POOL_GC, gi * POOL_GC))
    return out


_SMALL = ("pool_w", "pool_scale", "rel_bias", "group_gain", "pre_mix_g", "post_mix_g", "pre_ffn_g", "post_ffn_g")


def _pack(parts, rows):
    flat = jnp.concatenate([p.reshape(-1).astype(F32) for p in parts])
    return jnp.pad(flat, (0, rows * D_MODEL - flat.shape[0])).reshape(rows, D_MODEL)


def _unpack(packed, shapes):
    flat = packed.reshape(-1)
    out, at = [], 0
    for shp in shapes:
        size = int(np.prod(shp))
        out.append(flat[at:at + size].reshape(shp))
        at += size
    return out


def kernel(x, w_in, w_out, conv_w, pool_w, pool_scale, rel_bias, group_gain, pre_mix_g, post_mix_g, pre_ffn_g, post_ffn_g, w_gate_up, w_down, loss_target, m_w_in, m_w_out, m_conv_w, m_pool_w, m_pool_scale, m_rel_bias, m_group_gain, m_pre_mix_g, m_post_mix_g, m_pre_ffn_g, m_post_ffn_g, m_w_gate_up, m_w_down, v_w_in, v_w_out, v_conv_w, v_pool_w, v_pool_scale, v_rel_bias, v_group_gain, v_pre_mix_g, v_post_mix_g, v_pre_ffn_g, v_post_ffn_g, v_w_gate_up, v_w_down):
    depth = w_in.shape[0]
    s = x.shape[1]
    c_me = lax.axis_index("c")
    q_me = 2 * lax.axis_index("x") + lax.axis_index("y")
    dev = 2 * q_me + c_me

    tr = lambda a: jnp.swapaxes(a, 1, 2)
    w_in_t, w_gu_t = tr(w_in), tr(w_gate_up)

    def gather_layer(l):
        return (gather_start(f"gather_a{l}_s", [w_in_t[l].astype(BF16), w_out[l].astype(BF16), conv_w[l]]),
                gather_start(f"gather_b{l}_s", [w_gu_t[l].astype(BF16), w_down[l].astype(BF16)]))

    def weights_a(relayed, l, after):
        g_in, g_out, g_cw = gather_finish(f"gather_a{l}_f", relayed, after)
        return (assemble_rows("assemble_in", *g_in), assemble_rows("assemble_out", *g_out),
                assemble_cols("assemble_conv", *g_cw))

    def weights_b(relayed, l, after):
        g_gu, g_dn = gather_finish(f"gather_b{l}_f", relayed, after)
        return assemble_rows("assemble_gu", *g_gu), assemble_rows("assemble_down", *g_dn)

    h = x.reshape(s, D_MODEL)
    started = {0: gather_layer(0)}
    relay_a = gather_relay("gather_a0_r", started[0][0], h)
    wa = weights_a(relay_a, 0, relay_a["token"])
    saved = []
    for l in range(depth):
        vec = lambda p: p[l].reshape(1, -1)
        wt_in, wf_out, cw_full = wa
        ahead = [started[l][1]["token"]]
        if l + 1 < depth:
            started[l + 1] = gather_layer(l + 1)
            ahead += [st["token"] for st in started[l + 1]]
        xn = rmsnorm_fwd("norm_mix", h, vec(pre_mix_g), None, BF16, deps=ahead)
        proj_a = matmul("proj_a", xn, wt_in, "nt", F32, s, D_MODEL, D_MODEL, 1024, 512, D_MODEL)
        qkv = matmul("proj_qkv", xn, wt_in, "nt", BF16, s, 3 * D_ATTN, D_MODEL, 1024, 512, D_MODEL, b_off=(2, 0))
        relay_b = gather_relay(f"gather_b{l}_r", started[l][1], qkv)
        bd = _block_diag(pool_w[l]).astype(BF16)
        bias = attention_bias(rel_bias[l])
        gg = vec(group_gain)
        y_ab = conv_pool_fwd("conv_pool_fwd", proj_a, cw_full, bd, vec(pool_scale), gg[:, :2 * D_CONV],
                             deps=[relay_b["token"]])
        y, o, lse = attention_fwd("attention_fwd", qkv, bias, gg[:, 2 * D_CONV:], y_ab)
        wt_gu, wf_dn = weights_b(relay_b, l, y)
        mix = matmul("mix_out", y, wf_out, "nn", F32, s, D_MODEL, D_MODEL, 1024, 512, D_MODEL)
        if l + 1 < depth:
            relay_a = gather_relay(f"gather_a{l + 1}_r", started[l + 1][0], mix)
        h_mid = rmsnorm_fwd("norm_post", mix, vec(post_mix_g), h, F32,
                            deps=[relay_a["token"]] if l + 1 < depth else [])
        hn = rmsnorm_fwd("norm_ffn", h_mid, vec(pre_ffn_g), None, BF16)
        g, u, ff = matmul_swiglu("gate_up", hn, wt_gu)
        if l + 1 < depth:
            wa = weights_a(relay_a, l + 1, ff)
        ffo = matmul("ffn_down", ff, wf_dn, "nn", F32, s, D_MODEL, D_FF, 512, 512, D_FF)
        h_out = rmsnorm_fwd("norm_post", ffo, vec(post_ffn_g), h_mid, F32)
        saved.append((h, xn, proj_a, qkv, bd, bias, y, o, lse, mix, h_mid, hn, g, u, ff, ffo,
                      wt_in, wf_out, cw_full, wt_gu, wf_dn))
        h = h_out

    dh, loss_part = loss_head("loss_head", h, loss_target.reshape(s, D_MODEL))

    def reduce_begin(name, halves):
        return [k for k, _ in halves], exchange_start(name, plan_swap_cores, len(halves), [snd for _, snd in halves],
                                                      [(k.shape, k.dtype) for k, _ in halves])

    def reduce_relay(name, kept, swapped, after):
        _, got = exchange_wait(name + "_w", swapped, after)
        sums = [add_pairs("reduce_add", k, r) for k, r in zip(kept, got)]
        return exchange_start(name + "_s", plan_to_chips, 3 * len(sums), sums, [(a.shape, a.dtype) for a in sums])

    def reduce_finish(name, relayed, after):
        sums, got = exchange_wait(name, relayed, after)
        return list(zip(sums, got))

    small = {k: [None] * depth for k in _SMALL + ("conv_w",)}
    relayed = {}
    for l in reversed(range(depth)):
        vec = lambda p: p[l].reshape(1, -1)
        (h_in, xn, proj_a, qkv, bd, bias, y, o, lse, mix, h_mid, hn, g, u, ff, ffo,
         wt_in, wf_out, cw_full, wt_gu, wf_dn) = saved[l]
        gg = vec(group_gain)
        dffo, small["post_ffn_g"][l] = rmsnorm_bwd("norm_bwd_b", ffo, vec(post_ffn_g), dh, None, BF16)
        gw_dn = matmul("wgrad_down", ff, dffo, "tn", BF16, D_FF, D_MODEL, s, 1408, D_MODEL, 1024)
        dg, du = matmul_swiglu_bwd("dgrad_down", dffo, wf_dn, g, u)
        gw_gu = matmul("wgrad_gu", [dg, du], hn, "tn", BF16, 2 * D_FF, D_MODEL, s, 1408, D_MODEL, 1024)
        kept_b, swap_b = reduce_begin(f"reduce_b{l}_s", [split_rows("split_gu", gw_gu), split_rows("split_down", gw_dn)])
        dhn = matmul("dgrad_gu", [dg, du], wt_gu, "nn", F32, s, D_MODEL, 2 * D_FF, 1024, 1024, 1408,
                     deps=[swap_b["token"]])
        relayed[l, "b"] = reduce_relay(f"reduce_b{l}_r", kept_b, swap_b, dhn)
        dh_mid, small["pre_ffn_g"][l] = rmsnorm_bwd("norm_bwd_r", h_mid, vec(pre_ffn_g), dhn, dh, F32,
                                                    deps=[relayed[l, "b"]["token"]])
        dmix, small["post_mix_g"][l] = rmsnorm_bwd("norm_bwd_b", mix, vec(post_mix_g), dh_mid, None, BF16)
        gw_out = matmul("wgrad_out", y, dmix, "tn", BF16, D_MODEL, D_MODEL, s, D_MODEL, D_MODEL, 1024)
        dy = matmul("dgrad_out", dmix, wf_out, "nt", BF16, s, D_MODEL, D_MODEL, 1024, 512, D_MODEL)
        dpa, dcw, dbd, dps, dgg_ab = conv_pool_bwd("conv_pool_bwd", proj_a, dy, cw_full, bd, vec(pool_scale),
                                                   gg[:, :2 * D_CONV])
        dq, dk, dv, by_off, dgg_c = attention_bwd("attention_bwd", qkv, o, lse, dy, bias, gg[:, 2 * D_CONV:])
        dparts = [dpa, dq, dk, dv]
        gw_in = matmul("wgrad_in", dparts, xn, "tn", BF16, 5 * D_ATTN, D_MODEL, s, 512, D_MODEL, min(s, 2048))
        kept_a, swap_a = reduce_begin(f"reduce_a{l}_s", [split_rows("split_in", gw_in), split_rows("split_out", gw_out)])
        dxn = matmul("dgrad_in", dparts, wt_in, "nn", F32, s, D_MODEL, 5 * D_ATTN, 1024, 1024, 512,
                     deps=[swap_a["token"]])
        relayed[l, "a"] = reduce_relay(f"reduce_a{l}_r", kept_a, swap_a, dxn)
        dh, small["pre_mix_g"][l] = rmsnorm_bwd("norm_bwd_r", h_in, vec(pre_mix_g), dxn, dh_mid, F32,
                                                deps=[relayed[l, "a"]["token"]])
        small["conv_w"][l] = dcw
        small["pool_w"][l] = jnp.stack([dbd[gi * POOL_GC:(gi + 1) * POOL_GC, gi * POOL_GC:(gi + 1) * POOL_GC]
                                        for gi in range(N_POOL)])
        small["pool_scale"][l] = dps
        small["rel_bias"][l] = rel_bias_grad(by_off)
        small["group_gain"][l] = jnp.concatenate([dgg_ab, dgg_c], axis=1)
    grad_x = dh.reshape(x.shape)

    small_params = dict(pool_w=pool_w, pool_scale=pool_scale, rel_bias=rel_bias, group_gain=group_gain,
                        pre_mix_g=pre_mix_g, post_mix_g=post_mix_g, pre_ffn_g=pre_ffn_g, post_ffn_g=post_ffn_g)
    small_m = dict(pool_w=m_pool_w, pool_scale=m_pool_scale, rel_bias=m_rel_bias, group_gain=m_group_gain,
                   pre_mix_g=m_pre_mix_g, post_mix_g=m_post_mix_g, pre_ffn_g=m_pre_ffn_g, post_ffn_g=m_post_ffn_g)
    small_v = dict(pool_w=v_pool_w, pool_scale=v_pool_scale, rel_bias=v_rel_bias, group_gain=v_group_gain,
                   pre_mix_g=v_pre_mix_g, post_mix_g=v_post_mix_g, pre_ffn_g=v_pre_ffn_g, post_ffn_g=v_post_ffn_g)
    shapes = [small_params[k].shape for k in _SMALL] + [(depth, 3, D_CONV), (1,)]
    n_small = sum(int(np.prod(shp)) for shp in shapes)
    rows = -(-n_small // (8 * D_MODEL)) * 8
    extra = [jnp.zeros((depth, 3, D_CONV), F32), jnp.zeros((1,), F32)]
    grads_packed = _pack([jnp.stack(small[k]) for k in _SMALL] + [jnp.stack(small["conv_w"]), loss_part[0, 0:1]], rows)
    small_started = gather_start("gather_small_s", [grads_packed])

    big = dict(w_in=(w_in_t, tr(m_w_in), tr(v_w_in)), w_out=(w_out, m_w_out, v_w_out),
               w_gate_up=(w_gu_t, tr(m_w_gate_up), tr(v_w_gate_up)), w_down=(w_down, m_w_down, v_w_down))
    results = dict(w_in=None, w_out=None, w_gate_up=None, w_down=None)
    after = small_started["token"]
    small_relay = None
    for l in reversed(range(depth)):
        slabs_gu, slabs_dn = reduce_finish(f"reduce_b{l}_f", relayed[l, "b"], after)
        slabs_in, slabs_out = reduce_finish(f"reduce_a{l}_f", relayed[l, "a"], slabs_dn[1])
        for k, slabs in (("w_gate_up", slabs_gu), ("w_down", slabs_dn), ("w_in", slabs_in), ("w_out", slabs_out)):
            results[k] = adamw_layer("adamw_" + k, l, *big[k], *slabs, results[k])
        after = results["w_out"][0]
        if small_relay is None:
            small_relay = gather_relay("gather_small_r", small_started, after)
    small_parts = gather_finish("gather_small_f", small_relay, after)[0]
    all_small = assemble_rows("assemble_small", *small_parts).reshape(8, rows, D_MODEL)
    res_small = adamw("adamw_small", _pack([small_params[k] for k in _SMALL] + extra, rows),
                      _pack([small_m[k] for k in _SMALL] + extra, rows),
                      _pack([small_v[k] for k in _SMALL] + extra, rows), all_small)
    g_s, d_s, m_s, v_s = (_unpack(r, shapes) for r in res_small)
    loss = g_s[-1][0]
    g_conv = lax.dynamic_slice_in_dim(g_s[-2], dev * conv_w.shape[2], conv_w.shape[2], axis=2)
    results["conv_w"] = adamw("adamw_conv", conv_w, m_conv_w, v_conv_w, g_conv[None])

    names = ("w_in", "w_out", "conv_w") + _SMALL + ("w_gate_up", "w_down")
    for k in ("w_in", "w_gate_up"):
        results[k] = tuple(tr(r) for r in results[k])
    for j, k in enumerate(_SMALL):
        results[k] = (g_s[j], d_s[j], m_s[j], v_s[j])
    return (loss, grad_x, *[results[k][0] for k in names], *[results[k][1] for k in names],
            *[results[k][2] for k in names], *[results[k][3] for k in names])
```

```python
import numpy as np
import jax
import jax.numpy as jnp
from jax import lax
from jax.experimental import pallas as pl
from jax.experimental.pallas import tpu as pltpu

F32 = jnp.float32
BF16 = jnp.bfloat16

CHUNK = 64
D_MODEL = 1024
D_CONV = 256
D_POOL = 256
D_ATTN = 512
HEAD_DIM = 64
N_HEADS = 8
N_POOL = 4
POOL_GC = 64
POOL_WINDOWS = (2, 4, 8, 16)
LEFT_CHUNKS = 8
REL_CLIP = 128
D_FF = 2816
EPS = 1e-6
ADAM_LR, ADAM_B1, ADAM_B2, ADAM_EPS, ADAM_WD, ADAM_STEP = 0.001, 0.9, 0.999, 1e-08, 0.01, 10

QB = 256
KB = 3 * QB
HALO = 16
T_CP = 512
T_ROW = 512
NEG = -1e30
VMEM_MB = 1 << 20
MESH = pl.DeviceIdType.MESH


def _call(body, **kw):
    return pl.pallas_call(body, **kw)


def _call_after(deps, n_in, body, **kw):
    deps = tuple(deps)
    if not deps:
        return _call(body, **kw)

    def ordered(*refs):
        body(*refs[:n_in], *refs[n_in + len(deps):])

    kw["in_specs"] = list(kw["in_specs"]) + [pl.BlockSpec(memory_space=pl.ANY)] * len(deps)
    call = _call(ordered, **kw)
    return lambda *args: call(*args, *deps)


def _params(sem, vmem_mb=48):
    return pltpu.CompilerParams(dimension_semantics=sem, vmem_limit_bytes=vmem_mb * VMEM_MB)


_CHIP_FLIPS = ((1, 0), (0, 1), (1, 1))
_HBM = pl.BlockSpec(memory_space=pltpu.HBM)
_SEM = pl.BlockSpec(memory_space=pltpu.SEMAPHORE)
_EFFECT = pltpu.SideEffectType.DATAFLOW_SIDE_EFFECTING


def _flip(v, f):
    return 1 - v if f else v


def _descriptors(plan, srcs, lands, send_sems, recv_sems, sending):
    x, y, c = lax.axis_index("x"), lax.axis_index("y"), lax.axis_index("c")
    return [pltpu.make_async_remote_copy(src_ref=src, dst_ref=dst if sending else land, send_sem=send_sems.at[k],
                                         recv_sem=recv_sems.at[k], device_id=peer, device_id_type=MESH)
            for k, (src, dst, peer, land) in enumerate(plan(srcs, lands, x, y, c))]


def exchange_start(name, plan, n_copies, srcs, land_shapes, deps=()):
    ns, nl = len(srcs), len(land_shapes)

    def body(*refs):
        src_refs, land_refs = refs[:ns], refs[ns:ns + nl]
        send_sems, recv_sems = refs[ns + nl], refs[ns + nl + 1]
        for send in _descriptors(plan, src_refs, land_refs, send_sems, recv_sems, True):
            send.start()
        refs[-1][...] = jnp.zeros_like(refs[-1])

    lands = [pltpu.with_memory_space_constraint(lax.empty(shape, dtype), pltpu.HBM) for shape, dtype in land_shapes]
    outs = _call_after(
        deps, ns + nl, body, name=name,
        out_shape=(pltpu.SemaphoreType.DMA((n_copies,)), pltpu.SemaphoreType.DMA((n_copies,)),
                   *[pltpu.HBM(a.shape, a.dtype) for a in srcs], *[pltpu.HBM(shape, dtype) for shape, dtype in land_shapes],
                   jax.ShapeDtypeStruct((8, 128), F32)),
        in_specs=[_HBM] * (ns + nl),
        out_specs=(_SEM, _SEM, *[_HBM] * (ns + nl), pl.BlockSpec(memory_space=pltpu.VMEM)),
        input_output_aliases={j: 2 + j for j in range(ns + nl)},
        compiler_params=pltpu.CompilerParams(has_side_effects=_EFFECT),
    )(*[pltpu.with_memory_space_constraint(a, pltpu.HBM) for a in srcs], *lands)
    return dict(plan=plan, sems=outs[:2], srcs=outs[2:2 + ns], lands=outs[2 + ns:2 + ns + nl], token=outs[-1])


def exchange_wait(name, started, after):
    srcs, lands = started["srcs"], started["lands"]
    ns, nl = len(srcs), len(lands)

    def body(*refs):
        src_refs, land_refs = refs[:ns], refs[ns:ns + nl]
        send_sems, recv_sems = refs[ns + nl], refs[ns + nl + 1]
        for wait in _descriptors(started["plan"], src_refs, land_refs, send_sems, recv_sems, False):
            wait.wait_send()
            wait.wait_recv()

    outs = _call(
        body, name=name,
        out_shape=tuple(pltpu.HBM(a.shape, a.dtype) for a in (*srcs, *lands)),
        in_specs=[_HBM] * (ns + nl) + [_SEM, _SEM, pl.BlockSpec(memory_space=pl.ANY)],
        out_specs=tuple([_HBM] * (ns + nl)),
        input_output_aliases={j: j for j in range(ns + nl)},
        compiler_params=pltpu.CompilerParams(has_side_effects=_EFFECT),
    )(*srcs, *lands, *started["sems"], after)
    return list(outs[:ns]), list(outs[ns:])


def plan_from_chips(srcs, lands, x, y, c):
    q = 2 * x + y
    out = []
    for src, land in zip(srcs, lands):
        for fx, fy in _CHIP_FLIPS:
            px, py = _flip(x, fx), _flip(y, fy)
            out.append((src, land.at[q], (px, py, c), land.at[2 * px + py]))
    return out


def plan_to_core(srcs, lands, x, y, c):
    n = len(lands)
    q = 2 * x + y
    out = []
    for own, chips, land in zip(srcs[:n], srcs[n:], lands):
        out.append((own, land.at[q], (x, y, 1 - c), land.at[q]))
        for fx, fy in _CHIP_FLIPS:
            qp = 2 * _flip(x, fx) + _flip(y, fy)
            out.append((chips.at[qp], land.at[qp], (x, y, 1 - c), land.at[qp]))
    return out


def plan_swap_cores(srcs, lands, x, y, c):
    return [(src, land, (x, y, 1 - c), land) for src, land in zip(srcs, lands)]


def plan_to_chips(srcs, lands, x, y, c):
    q = 2 * x + y
    out = []
    for src, land in zip(srcs, lands):
        for fx, fy in _CHIP_FLIPS:
            px, py = _flip(x, fx), _flip(y, fy)
            out.append((src.at[2 * px + py], land.at[q], (px, py, c), land.at[2 * px + py]))
    return out


def _slots(a):
    return ((4,) + a.shape, a.dtype)


def gather_start(name, arrays, deps=()):
    return exchange_start(name, plan_from_chips, 3 * len(arrays), arrays, [_slots(a) for a in arrays], deps)


def gather_relay(name, started, after):
    own, chips = exchange_wait(name + "_w", started, after)
    return exchange_start(name + "_s", plan_to_core, 4 * len(own), own + chips, [_slots(a) for a in own])


def gather_finish(name, relayed, after):
    srcs, cores = exchange_wait(name, relayed, after)
    n = len(cores)
    return list(zip(srcs[:n], srcs[n:], cores))


def _place():
    return 2 * lax.axis_index("x") + lax.axis_index("y"), lax.axis_index("c")


def assemble_cols(name, own, chips, core):
    r, c = own.shape
    t = _row_tile(r, 256)

    def body(own_ref, chips_ref, core_ref, o_ref):
        q_me, c_me = _place()
        for q in range(4):
            mine = jnp.where(q == q_me, own_ref[...], chips_ref[q])
            other = core_ref[q]
            o_ref[:, 2 * q * c:(2 * q + 1) * c] = jnp.where(c_me == 0, mine, other)
            o_ref[:, (2 * q + 1) * c:(2 * q + 2) * c] = jnp.where(c_me == 0, other, mine)

    slots = pl.BlockSpec((4, t, c), lambda i: (0, i, 0))
    return _call(
        body, name=name, grid=(r // t,),
        in_specs=[pl.BlockSpec((t, c), lambda i: (i, 0)), slots, slots],
        out_specs=pl.BlockSpec((t, 8 * c), lambda i: (i, 0)),
        out_shape=jax.ShapeDtypeStruct((r, 8 * c), own.dtype), compiler_params=_params(("parallel",)),
    )(own, chips, core)


def assemble_rows(name, own, chips, core):
    r, c = own.shape

    def body(own_ref, chips_ref, core_ref, o_ref):
        q_me, c_me = _place()
        d = pl.program_id(0)
        mine = jnp.where(d // 2 == q_me, own_ref[...], chips_ref[...])
        o_ref[...] = jnp.where(d % 2 == c_me, mine, core_ref[...])

    slot = pl.BlockSpec((None, r, c), lambda d: (d // 2, 0, 0))
    return _call(
        body, name=name, grid=(8,),
        in_specs=[pl.BlockSpec((r, c), lambda d: (0, 0)), slot, slot],
        out_specs=pl.BlockSpec((r, c), lambda d: (d, 0)),
        out_shape=jax.ShapeDtypeStruct((8 * r, c), own.dtype), compiler_params=_params(("parallel",)),
    )(own, chips, core)


def split_rows(name, dw):
    r8, c = dw.shape
    r = r8 // 8

    def body(dw_ref, keep_ref, send_ref):
        _, c_me = _place()
        d = pl.program_id(0)

        @pl.when(d % 2 == c_me)
        def _():
            keep_ref[...] = dw_ref[...]

        @pl.when(d % 2 != c_me)
        def _():
            send_ref[...] = dw_ref[...]

    slot = pl.BlockSpec((None, r, c), lambda d: (d // 2, 0, 0))
    return _call(
        body, name=name, grid=(8,),
        in_specs=[pl.BlockSpec((r, c), lambda d: (d, 0))], out_specs=[slot, slot],
        out_shape=[jax.ShapeDtypeStruct((4, r, c), dw.dtype)] * 2, compiler_params=_params(("arbitrary",)),
    )(dw)


def _rms(v):
    return lax.rsqrt(jnp.mean(v * v, axis=-1, keepdims=True) + EPS)


def rmsnorm_fwd(name, x, g, res, out_dtype, then=None, deps=()):
    s, d = x.shape
    has_res = res is not None

    def body(*refs):
        x_ref, g_ref = refs[:2]
        xv = x_ref[...]
        y = (xv * _rms(xv)) * g_ref[...]
        if has_res:
            y = y + refs[2][...]
        if then is None:
            refs[-1][...] = y.astype(refs[-1].dtype)
        else:
            refs[-2][...] = y.astype(refs[-2].dtype)
            refs[-1][...] = ((y * _rms(y)) * refs[-3][...]).astype(refs[-1].dtype)

    row = pl.BlockSpec((T_ROW, d), lambda i: (i, 0))
    vec = pl.BlockSpec((1, d), lambda i: (0, 0))
    args = (x, g) + ((res,) if has_res else ()) + ((then[0],) if then else ())
    shapes = [jax.ShapeDtypeStruct((s, d), out_dtype)] + ([jax.ShapeDtypeStruct((s, d), then[1])] if then else [])
    return _call_after(
        deps, len(args), body, name=name, grid=(s // T_ROW,),
        in_specs=[row, vec] + ([row] if has_res else []) + ([vec] if then else []),
        out_specs=[row] * len(shapes) if then else row,
        out_shape=shapes if then else shapes[0],
        compiler_params=_params(("parallel",)),
    )(*args)


def _norm_bwd(xv, r, g, dy):
    a = dy * g
    dx = r * (a - xv * ((r * r) * jnp.mean(a * xv, axis=-1, keepdims=True)))
    return dx, dy * (xv * r)


def rmsnorm_bwd(name, x, g, dy, res, out_dtype, then=None, deps=()):
    s, d = x.shape
    has_res = res is not None
    n_in = 3 + has_res + (2 if then else 0)

    def body(*refs):
        x_ref, g_ref, dy_ref = refs[:3]
        outs = refs[n_in:]
        xv = x_ref[...]
        dx, dgt = _norm_bwd(xv, _rms(xv), g_ref[...], dy_ref[...].astype(F32))
        if has_res:
            dx = dx + refs[3][...]
        outs[0][...] = dx.astype(outs[0].dtype)
        sums = [(outs[1], dgt)]
        if then is not None:
            x2 = refs[n_in - 2][...]
            dx2, dgt2 = _norm_bwd(x2, _rms(x2), refs[n_in - 1][...], dx)
            outs[2][...] = dx2.astype(outs[2].dtype)
            sums.append((outs[3], dgt2))

        @pl.when(pl.program_id(0) == 0)
        def _():
            for dg_ref, _ in sums:
                dg_ref[...] = jnp.zeros_like(dg_ref)
        for dg_ref, terms in sums:
            dg_ref[...] += jnp.sum(terms, axis=0, keepdims=True)

    row = pl.BlockSpec((T_ROW, d), lambda i: (i, 0))
    vec = pl.BlockSpec((1, d), lambda i: (0, 0))
    args = (x, g, dy) + ((res,) if has_res else ()) + ((then[0], then[1]) if then else ())
    shapes = [jax.ShapeDtypeStruct((s, d), out_dtype), jax.ShapeDtypeStruct((1, d), F32)]
    if then:
        shapes += [jax.ShapeDtypeStruct((s, d), then[2]), jax.ShapeDtypeStruct((1, d), F32)]
    return _call_after(
        deps, len(args), body, name=name, grid=(s // T_ROW,),
        in_specs=[row, vec, row] + ([row] if has_res else []) + ([row, vec] if then else []),
        out_specs=[row, vec] * (2 if then else 1), out_shape=shapes,
        compiler_params=_params(("arbitrary",)),
    )(*args)


def loss_head(name, h, tgt):
    s, d = h.shape

    def body(h_ref, t_ref, dh_ref, l_ref):
        e = h_ref[...] - t_ref[...]
        dh_ref[...] = e * (1.0 / d)

        @pl.when(pl.program_id(0) == 0)
        def _():
            l_ref[...] = jnp.zeros_like(l_ref)
        part = 0.5 * jnp.sum(jnp.mean(e * e, axis=-1, keepdims=True), axis=0, keepdims=True)
        l_ref[...] += jnp.broadcast_to(part, l_ref.shape)

    row = pl.BlockSpec((T_ROW, d), lambda i: (i, 0))
    return _call(
        body, name=name, grid=(s // T_ROW,),
        in_specs=[row, row], out_specs=[row, pl.BlockSpec((1, 128), lambda i: (0, 0))],
        out_shape=[jax.ShapeDtypeStruct((s, d), F32), jax.ShapeDtypeStruct((1, 128), F32)],
        compiler_params=_params(("arbitrary",)),
    )(h, tgt)


def _row_tile(rows, limit=512):
    t = min(rows, limit)
    while rows % t or (t % 8 and t != rows):
        t -= 1
    return t


def add_pairs(name, a, b):
    shape = a.shape
    a2, b2 = a.reshape(-1, shape[-1]), b.reshape(-1, shape[-1])
    rows, cols = a2.shape
    t = _row_tile(rows)

    def body(a_ref, b_ref, o_ref):
        o_ref[...] = (a_ref[...].astype(F32) + b_ref[...].astype(F32)).astype(o_ref.dtype)

    blk = pl.BlockSpec((t, cols), lambda i: (i, 0))
    out = _call(
        body, name=name, grid=(rows // t,), in_specs=[blk, blk], out_specs=blk,
        out_shape=jax.ShapeDtypeStruct((rows, cols), a.dtype), compiler_params=_params(("parallel",)),
    )(a2, b2)
    return out.reshape(shape)


def _adamw_math(w, m, v, g):
    c1 = 1.0 - ADAM_B1 ** ADAM_STEP
    c2 = 1.0 - ADAM_B2 ** ADAM_STEP
    mn = ADAM_B1 * m + (1.0 - ADAM_B1) * g
    vn = ADAM_B2 * v + (1.0 - ADAM_B2) * (g * g)
    delta = -ADAM_LR * ((mn / c1) / (jnp.sqrt(vn / c2) + ADAM_EPS) + ADAM_WD * w)
    return delta, mn, vn


def _slab_sum(g_ref, n):
    g = g_ref[0].astype(F32)
    for j in range(1, n):
        g = g + g_ref[j].astype(F32)
    return g


def adamw(name, w, m, v, g_slabs, deps=()):
    shape = w.shape
    n = g_slabs.shape[0]
    w2, m2, v2 = (t.reshape(-1, shape[-1]) for t in (w, m, v))
    g3 = g_slabs.reshape(n, -1, shape[-1])
    rows, cols = w2.shape
    t = _row_tile(rows, 256)

    def body(w_ref, m_ref, v_ref, g_ref, go_ref, d_ref, mo_ref, vo_ref):
        g = _slab_sum(g_ref, n)
        go_ref[...] = g
        d_ref[...], mo_ref[...], vo_ref[...] = _adamw_math(w_ref[...], m_ref[...], v_ref[...], g)

    blk = pl.BlockSpec((t, cols), lambda i: (i, 0))
    outs = _call_after(
        deps, 4, body, name=name, grid=(rows // t,),
        in_specs=[blk, blk, blk, pl.BlockSpec((n, t, cols), lambda i: (0, i, 0))], out_specs=[blk] * 4,
        out_shape=[jax.ShapeDtypeStruct((rows, cols), F32)] * 4, compiler_params=_params(("parallel",)),
    )(w2, m2, v2, g3)
    return tuple(o.reshape(shape) for o in outs)


def adamw_layer(name, l, w, m, v, own_sums, chip_sums, into):
    _, rows, cols = w.shape
    t = _row_tile(rows, 256)
    if into is None:
        into = tuple(lax.empty(w.shape, F32) for _ in range(4))

    def body(w_ref, m_ref, v_ref, own_ref, far_ref, a0, a1, a2, a3, go_ref, d_ref, mo_ref, vo_ref):
        del a0, a1, a2, a3
        q_me, _ = _place()
        g = jnp.zeros((t, cols), F32)
        for q in range(4):
            g = g + jnp.where(q == q_me, own_ref[q], far_ref[q]).astype(F32)
        go_ref[...] = g
        d_ref[...], mo_ref[...], vo_ref[...] = _adamw_math(w_ref[...], m_ref[...], v_ref[...], g)

    blk = pl.BlockSpec((None, t, cols), lambda i: (l, i, 0))
    slabs = pl.BlockSpec((4, t, cols), lambda i: (0, i, 0))
    anyw = pl.BlockSpec(memory_space=pl.ANY)
    return _call(
        body, name=name, grid=(rows // t,),
        in_specs=[blk, blk, blk, slabs, slabs] + [anyw] * 4, out_specs=[blk] * 4,
        out_shape=[jax.ShapeDtypeStruct(w.shape, F32)] * 4, input_output_aliases={5: 0, 6: 1, 7: 2, 8: 3},
        compiler_params=_params(("parallel",)),
    )(w, m, v, own_sums, chip_sums, *into)


_DIMS = {"nn": (((1,), (0,)), ((), ())), "nt": (((1,), (1,)), ((), ())), "tn": (((0,), (0,)), ((), ()))}


def _dot(a, b, mode="nn"):
    return lax.dot_general(a, b, _DIMS[mode], preferred_element_type=F32)


def matmul(name, a, b, mode, out_dtype, m, n, k, tm, tn, tk, b_off=(0, 0), deps=()):
    gm, gn, gk = m // tm, n // tn, k // tk
    assert gm * tm == m and gn * tn == n and gk * tk == k
    r0, c0 = b_off
    a_parts = list(a) if isinstance(a, (list, tuple)) else [a]
    na = len(a_parts)
    tile = tm if mode == "tn" else tk
    spans, at = [], 0
    for p in a_parts:
        nblk = p.shape[1] // tile
        assert nblk * tile == p.shape[1]
        spans.append((at, nblk))
        at += nblk
    assert at == (gm if mode == "tn" else gk)

    def within(t, span):
        return (t >= span[0]) & (t < span[0] + span[1])

    def local(t, span):
        return jnp.clip(t - span[0], 0, span[1] - 1)

    a_specs = []
    for sp in spans:
        if mode == "tn":
            a_specs.append(pl.BlockSpec((tk, tm), lambda i, j, kk, sp=sp: (jnp.where(within(i, sp), kk, 0), local(i, sp))))
        else:
            a_specs.append(pl.BlockSpec((tm, tk), lambda i, j, kk, sp=sp: (i, local(kk, sp))))
    if mode == "nt":
        b_spec = pl.BlockSpec((tn, tk), lambda i, j, kk: (j + r0, kk + c0))
    else:
        b_spec = pl.BlockSpec((tk, tn), lambda i, j, kk: (kk + r0, j + c0))
    o_spec = pl.BlockSpec((tm, tn), lambda i, j, kk: (i, j))
    single = na == 1 and gk == 1

    def body(*refs):
        a_refs, b_ref, o_ref = refs[:na], refs[na], refs[na + 1]
        if single:
            o_ref[...] = _dot(a_refs[0][...], b_ref[...], mode).astype(o_ref.dtype)
            return
        acc_ref = refs[-1]
        i, kk = pl.program_id(0), pl.program_id(2)

        @pl.when(kk == 0)
        def _():
            acc_ref[...] = jnp.zeros_like(acc_ref)

        for pa, sp in enumerate(spans):
            def add(pa=pa):
                acc_ref[...] += _dot(a_refs[pa][...], b_ref[...], mode)
            if na > 1:
                pl.when(within(i if mode == "tn" else kk, sp))(add)
            else:
                add()

        @pl.when(kk == gk - 1)
        def _():
            o_ref[...] = acc_ref[...].astype(o_ref.dtype)

    return _call_after(
        deps, na + 1, body, name=name, grid=(gm, gn, gk),
        in_specs=a_specs + [b_spec], out_specs=o_spec,
        out_shape=jax.ShapeDtypeStruct((m, n), out_dtype),
        scratch_shapes=[] if single else [pltpu.VMEM((tm, tn), F32)],
        compiler_params=_params(("parallel", "parallel", "arbitrary")),
    )(*a_parts, b)


def matmul_swiglu(name, a, w_gu_t, deps=()):
    s, d = a.shape
    tm, tn = 512, 1408
    gn = D_FF // tn

    def body(a_ref, wg_ref, wu_ref, g_ref, u_ref, f_ref):
        av = a_ref[...]
        g = _dot(av, wg_ref[...], "nt")
        u = _dot(av, wu_ref[...], "nt")
        g_ref[...] = g.astype(g_ref.dtype)
        u_ref[...] = u.astype(u_ref.dtype)
        f_ref[...] = ((g * (1.0 / (1.0 + jnp.exp(-g)))) * u).astype(f_ref.dtype)

    o_spec = pl.BlockSpec((tm, tn), lambda j, i: (i, j))
    return _call_after(
        deps, 3, body, name=name, grid=(gn, s // tm),
        in_specs=[pl.BlockSpec((tm, d), lambda j, i: (i, 0)), pl.BlockSpec((tn, d), lambda j, i: (j, 0)),
                  pl.BlockSpec((tn, d), lambda j, i: (j + gn, 0))],
        out_specs=[o_spec] * 3, out_shape=[jax.ShapeDtypeStruct((s, D_FF), BF16)] * 3,
        compiler_params=_params(("parallel", "parallel")),
    )(a, w_gu_t, w_gu_t)


def matmul_swiglu_bwd(name, dffo, w_down, g, u):
    s, d = dffo.shape
    tm, tn = 512, 1408

    def body(a_ref, w_ref, g_ref, u_ref, dg_ref, du_ref):
        dff = _dot(a_ref[...], w_ref[...], "nt")
        gv = g_ref[...].astype(F32)
        sig = 1.0 / (1.0 + jnp.exp(-gv))
        du_ref[...] = (dff * (gv * sig)).astype(du_ref.dtype)
        dg_ref[...] = (dff * u_ref[...].astype(F32) * (sig * (1.0 + gv * (1.0 - sig)))).astype(dg_ref.dtype)

    o_spec = pl.BlockSpec((tm, tn), lambda j, i: (i, j))
    return _call(
        body, name=name, grid=(D_FF // tn, s // tm),
        in_specs=[pl.BlockSpec((tm, d), lambda j, i: (i, 0)), pl.BlockSpec((tn, d), lambda j, i: (j, 0)), o_spec, o_spec],
        out_specs=[o_spec] * 2, out_shape=[jax.ShapeDtypeStruct((s, D_FF), BF16)] * 2,
        compiler_params=_params(("parallel", "parallel")),
    )(dffo, w_down, g, u)


def _down(v, n):
    return pltpu.roll(v, n, 0)


def _up(v, n):
    return pltpu.roll(v, v.shape[0] - n, 0)


def _by_window(lane, v2, v4, v8, v16):
    return jnp.where(lane < POOL_GC, v2, jnp.where(lane < 2 * POOL_GC, v4, jnp.where(lane < 3 * POOL_GC, v8, v16)))


def _taps(cw_ref):
    return cw_ref[0:1, :], cw_ref[1:2, :], cw_ref[2:3, :]


def _conv_pool_forward(ext, t0, cw, bd):
    n_out = ext.shape[0] - HALO
    gb = ext[HALO:, 0:D_CONV]
    z = ext[:, D_CONV:2 * D_CONV] * ext[:, 2 * D_CONV:3 * D_CONV]
    z0, z1, z2 = z[HALO:], _down(z, 1)[HALO:], _down(z, 2)[HALO:]
    conv = cw[2] * z0 + cw[1] * z1 + cw[0] * z2
    x = ext[:, 3 * D_CONV:]
    w2 = x + _down(x, 1)
    w4 = w2 + _down(w2, 2)
    w8 = w4 + _down(w4, 4)
    w16 = w8 + _down(w8, 8)
    lane = lax.broadcasted_iota(jnp.int32, (1, D_POOL), 1)
    win = _by_window(lane, 2.0, 4.0, 8.0, 16.0)
    pos = (t0 + lax.broadcasted_iota(jnp.int32, (n_out, 1), 0) + 1).astype(F32)
    cnt = jnp.minimum(pos, win)
    d = _by_window(lane, w2, w4, w8, w16)[HALO:] / cnt - x[HALO:]
    ybp = _dot(d.astype(BF16), bd)
    return gb, z0, z1, z2, conv, d, cnt, ybp, lane


def conv_pool_fwd(name, proj_a, cw, bd, ps, gg, deps=()):
    s = proj_a.shape[0]
    t = T_CP
    hb = t // HALO

    def body(main_ref, prev_ref, cw_ref, bd_ref, ps_ref, gg_ref, y_ref):
        i = pl.program_id(0)
        prev = jnp.where(i > 0, prev_ref[...], 0.0)
        ext = jnp.concatenate([prev, main_ref[...]], axis=0)
        gb, _, _, _, conv, _, _, ybp, _ = _conv_pool_forward(ext, i * t, _taps(cw_ref), bd_ref[...])
        ya = gb * conv
        yb = ybp * ps_ref[...]
        ggv = gg_ref[...]
        y_ref[:, 0:D_CONV] = ((ya * _rms(ya)) * ggv[:, 0:D_CONV]).astype(y_ref.dtype)
        y_ref[:, D_CONV:] = ((yb * _rms(yb)) * ggv[:, D_CONV:]).astype(y_ref.dtype)

    full = lambda shape: pl.BlockSpec(shape, lambda i: (0,) * len(shape))
    return _call_after(
        deps, 6, body, name=name, grid=(s // t,),
        in_specs=[pl.BlockSpec((t, D_MODEL), lambda i: (i, 0)),
                  pl.BlockSpec((HALO, D_MODEL), lambda i: (jnp.maximum(i * hb - 1, 0), 0)),
                  full((3, D_CONV)), full((D_POOL, D_POOL)), full((1, D_POOL)), full((1, 2 * D_CONV))],
        out_specs=pl.BlockSpec((t, 2 * D_CONV), lambda i: (i, 0)),
        out_shape=jax.ShapeDtypeStruct((s, D_MODEL), BF16),
        compiler_params=_params(("parallel",)),
    )(proj_a, proj_a, cw, bd, ps, gg)


def conv_pool_bwd(name, proj_a, dy, cw, bd, ps, gg):
    s = proj_a.shape[0]
    t = T_CP
    hb = t // HALO
    nblk = s // t
    last_halo = s // HALO - 1

    def body(main_ref, prev_ref, next_ref, dy_ref, dyn_ref, cw_ref, bd_ref, ps_ref, gg_ref,
             dp_ref, dcw_ref, dbd_ref, dps_ref, dgg_ref):
        i = pl.program_id(0)
        prev = jnp.where(i > 0, prev_ref[...], 0.0)
        ext = jnp.concatenate([prev, main_ref[...], next_ref[...]], axis=0)
        cwv, bdv, psv, ggv = _taps(cw_ref), bd_ref[...], ps_ref[...], gg_ref[...]
        gb, z0, z1, z2, conv, d, cnt, ybp, lane = _conv_pool_forward(ext, i * t, cwv, bdv)
        dyn = jnp.where(i < nblk - 1, dyn_ref[...].astype(F32), 0.0)
        dyv = jnp.concatenate([dy_ref[...].astype(F32), dyn], axis=0)
        ya = gb * conv
        yb = ybp * psv
        dya, dgg_a = _norm_bwd(ya, _rms(ya), ggv[:, 0:D_CONV], dyv[:, 0:D_CONV])
        dyb, dgg_b = _norm_bwd(yb, _rms(yb), ggv[:, D_CONV:], dyv[:, D_CONV:])

        dconv = dya * gb
        dz = (cwv[2] * dconv + cwv[1] * _up(dconv, 1) + cwv[0] * _up(dconv, 2))[:t]
        main = main_ref[...]
        dp_ref[:, 0:D_CONV] = (dya * conv)[:t].astype(dp_ref.dtype)
        dp_ref[:, D_CONV:2 * D_CONV] = (dz * main[:, 2 * D_CONV:3 * D_CONV]).astype(dp_ref.dtype)
        dp_ref[:, 2 * D_CONV:3 * D_CONV] = (dz * main[:, D_CONV:2 * D_CONV]).astype(dp_ref.dtype)

        dybs = dyb * psv
        dd = _dot(dybs.astype(BF16), bdv, "nt")
        e = dd / cnt
        a2 = e + _up(e, 1)
        a4 = a2 + _up(a2, 2)
        a8 = a4 + _up(a4, 4)
        a16 = a8 + _up(a8, 8)
        dp_ref[:, 3 * D_CONV:] = (_by_window(lane, a2, a4, a8, a16) - dd)[:t].astype(dp_ref.dtype)

        @pl.when(i == 0)
        def _():
            dcw_ref[...] = jnp.zeros_like(dcw_ref)
            dbd_ref[...] = jnp.zeros_like(dbd_ref)
            dps_ref[...] = jnp.zeros_like(dps_ref)
            dgg_ref[...] = jnp.zeros_like(dgg_ref)

        rsum = lambda v: jnp.sum(v[:t], axis=0, keepdims=True)
        dcw_ref[0:1, :] += rsum(dconv * z2)
        dcw_ref[1:2, :] += rsum(dconv * z1)
        dcw_ref[2:3, :] += rsum(dconv * z0)
        dbd_ref[...] += _dot(d[:t].astype(BF16), dybs[:t].astype(BF16), "tn")
        dps_ref[...] += rsum(dyb * ybp)
        dgg_ref[:, 0:D_CONV] += rsum(dgg_a)
        dgg_ref[:, D_CONV:] += rsum(dgg_b)

    full = lambda shape: pl.BlockSpec(shape, lambda i: (0,) * len(shape))
    next_halo = lambda i: (jnp.minimum((i + 1) * hb, last_halo), 0)
    return _call(
        body, name=name, grid=(nblk,),
        in_specs=[pl.BlockSpec((t, D_MODEL), lambda i: (i, 0)),
                  pl.BlockSpec((HALO, D_MODEL), lambda i: (jnp.maximum(i * hb - 1, 0), 0)),
                  pl.BlockSpec((HALO, D_MODEL), next_halo),
                  pl.BlockSpec((t, 2 * D_CONV), lambda i: (i, 0)),
                  pl.BlockSpec((HALO, 2 * D_CONV), next_halo),
                  full((3, D_CONV)), full((D_POOL, D_POOL)), full((1, D_POOL)), full((1, 2 * D_CONV))],
        out_specs=[pl.BlockSpec((t, D_MODEL), lambda i: (i, 0)),
                   full((3, D_CONV)), full((D_POOL, D_POOL)), full((1, D_POOL)), full((1, 2 * D_CONV))],
        out_shape=[jax.ShapeDtypeStruct((s, D_MODEL), BF16), jax.ShapeDtypeStruct((3, D_CONV), F32),
                   jax.ShapeDtypeStruct((D_POOL, D_POOL), F32), jax.ShapeDtypeStruct((1, D_POOL), F32),
                   jax.ShapeDtypeStruct((1, 2 * D_CONV), F32)],
        compiler_params=_params(("arbitrary",)),
    )(proj_a, proj_a, proj_a, dy, dy, cw, bd, ps, gg)


def _band_mask():
    qa = np.arange(QB)[:, None] // CHUNK
    km = np.arange(KB)[None, :] // CHUNK
    return (km - qa >= 0) & (km - qa <= LEFT_CHUNKS)


def attention_bias(rel_bias):
    n_far = 2 * QB - REL_CLIP + 1
    far = jnp.broadcast_to(rel_bias[:, 2 * REL_CLIP:], (N_HEADS, n_far))
    mid = rel_bias[:, 1:2 * REL_CLIP][:, ::-1]
    near = jnp.broadcast_to(rel_bias[:, 0:1], (N_HEADS, KB - n_far - (2 * REL_CLIP - 1)))
    wrap = jnp.broadcast_to(rel_bias[:, 2 * REL_CLIP:], (N_HEADS, 4 * QB - KB))
    by_offset = jnp.concatenate([far, mid, near, wrap], axis=1)
    width = 4 * QB
    toeplitz = jnp.tile(by_offset, (1, QB))[:, :QB * (width - 1)].reshape(N_HEADS, QB, width - 1)[:, :, :KB]
    return jnp.where(jnp.asarray(_band_mask())[None], toeplitz, NEG)


def rel_bias_grad(by_offset):
    n_far = 2 * QB - REL_CLIP + 1
    hi = jnp.sum(by_offset[:, :n_far], axis=1, keepdims=True) + jnp.sum(by_offset[:, KB:], axis=1, keepdims=True)
    mid = by_offset[:, n_far:n_far + 2 * REL_CLIP - 1][:, ::-1]
    lo = jnp.sum(by_offset[:, n_far + 2 * REL_CLIP - 1:KB], axis=1, keepdims=True)
    return jnp.concatenate([lo, mid, hi], axis=1)


def _head_masks():
    first = lax.broadcasted_iota(jnp.int32, (1, 2 * HEAD_DIM), 1) < HEAD_DIM
    return first, jnp.logical_not(first)


def _pick_lane(tile, h):
    lane = lax.broadcasted_iota(jnp.int32, (1, tile.shape[1]), 1)
    return jnp.sum(jnp.where(lane == h, tile, 0.0), axis=-1, keepdims=True)


def attention_fwd(name, qkv, bias, gg, y_ab):
    s = qkv.shape[0]
    nq = s // QB
    scale = HEAD_DIM ** -0.5

    def body(q_ref, k0, k1, k2, v0, v1, v2, b_ref, gg_ref, y_in, y_ref, o_ref, lse_ref):
        del y_in
        i = pl.program_id(0)
        kb = jnp.concatenate([k0[...], k1[...], k2[...]], axis=0)
        vb = jnp.concatenate([v0[...], v1[...], v2[...]], axis=0)
        valid = lax.broadcasted_iota(jnp.int32, (1, KB), 1) >= (2 - i) * QB
        lane = lax.broadcasted_iota(jnp.int32, (1, 128), 1)
        masks = _head_masks()
        lse = jnp.zeros((QB, 128), F32)
        outs = []
        for hp in range(N_HEADS // 2):
            sl = slice(2 * HEAD_DIM * hp, 2 * HEAD_DIM * (hp + 1))
            q_p, k_p, v_p = q_ref[:, sl], kb[:, sl], vb[:, sl]
            o_pair = jnp.zeros((QB, 2 * HEAD_DIM), F32)
            for a in range(2):
                h = 2 * hp + a
                sc = _dot(jnp.where(masks[a], q_p, 0), k_p, "nt") * scale + b_ref[h]
                sc = jnp.where(valid, sc, NEG)
                mx = jnp.max(sc, axis=-1, keepdims=True)
                e = jnp.exp(sc - mx)
                l = jnp.sum(e, axis=-1, keepdims=True)
                o_pair = o_pair + _dot(e.astype(BF16), jnp.where(masks[a], v_p, 0)) * (1.0 / l)
                lse = jnp.where(lane == h, mx + jnp.log(l), lse)
            outs.append(o_pair)
        o = jnp.concatenate(outs, axis=1)
        o_ref[...] = o.astype(o_ref.dtype)
        lse_ref[...] = lse
        y_ref[...] = ((o * _rms(o)) * gg_ref[...]).astype(y_ref.dtype)

    blk = lambda col, back: pl.BlockSpec((QB, D_ATTN), lambda i: (jnp.maximum(i - back, 0), col))
    return _call(
        body, name=name, grid=(nq,),
        in_specs=[blk(0, 0), blk(1, 2), blk(1, 1), blk(1, 0), blk(2, 2), blk(2, 1), blk(2, 0),
                  pl.BlockSpec((N_HEADS, QB, KB), lambda i: (0, 0, 0)), pl.BlockSpec((1, D_ATTN), lambda i: (0, 0)),
                  pl.BlockSpec(memory_space=pl.ANY)],
        out_specs=[pl.BlockSpec((QB, D_ATTN), lambda i: (i, 1)), pl.BlockSpec((QB, D_ATTN), lambda i: (i, 0)),
                   pl.BlockSpec((QB, 128), lambda i: (i, 0))],
        out_shape=[jax.ShapeDtypeStruct((s, D_MODEL), BF16), jax.ShapeDtypeStruct((s, D_ATTN), BF16),
                   jax.ShapeDtypeStruct((s, 128), F32)],
        input_output_aliases={9: 0},
        compiler_params=_params(("arbitrary",), 56),
    )(qkv, qkv, qkv, qkv, qkv, qkv, qkv, bias, gg, y_ab)


def attention_bwd(name, qkv, o, lse, dy, bias, gg):
    s = qkv.shape[0]
    nq = s // QB
    scale = HEAD_DIM ** -0.5
    width = 4 * QB

    def body(q_ref, k0, k1, k2, v0, v1, v2, o_ref, lse_ref, dy_ref, b_ref, gg_ref,
             dq_ref, dk_ref, dv_ref, off_ref, dgg_ref, dk_acc, dv_acc, db_acc):
        i = pl.program_id(0)

        @pl.when(i == 0)
        def _():
            dk_acc[...] = jnp.zeros_like(dk_acc)
            dv_acc[...] = jnp.zeros_like(dv_acc)
            db_acc[...] = jnp.zeros_like(db_acc)
            dgg_ref[...] = jnp.zeros_like(dgg_ref)

        @pl.when(i > 0)
        def _():
            for acc in (dk_acc, dv_acc):
                kept = acc[QB:, :]
                acc[0:2 * QB, :] = kept
                acc[2 * QB:, :] = jnp.zeros((QB, D_ATTN), F32)

        @pl.when(i < nq)
        def _():
            ov = o_ref[...].astype(F32)
            dyv = dy_ref[...].astype(F32)
            do, dgg_t = _norm_bwd(ov, _rms(ov), gg_ref[...], dyv)
            dgg_ref[...] += jnp.sum(dgg_t, axis=0, keepdims=True)
            kb = jnp.concatenate([k0[...], k1[...], k2[...]], axis=0)
            vb = jnp.concatenate([v0[...], v1[...], v2[...]], axis=0)
            valid = lax.broadcasted_iota(jnp.int32, (1, KB), 1) >= (2 - i) * QB
            masks = _head_masks()
            lse_t = lse_ref[...]
            for hp in range(N_HEADS // 2):
                sl = slice(2 * HEAD_DIM * hp, 2 * HEAD_DIM * (hp + 1))
                q_p, k_p, v_p = q_ref[:, sl], kb[:, sl], vb[:, sl]
                do_p = do[:, sl]
                prod = do_p * ov[:, sl]
                do_b = do_p.astype(BF16)
                dq_pair = jnp.zeros((QB, 2 * HEAD_DIM), F32)
                dk_pair = jnp.zeros((KB, 2 * HEAD_DIM), F32)
                dv_pair = jnp.zeros((KB, 2 * HEAD_DIM), F32)
                for a in range(2):
                    h = 2 * hp + a
                    q_m = jnp.where(masks[a], q_p, 0)
                    do_m = jnp.where(masks[a], do_b, 0)
                    sc = _dot(q_m, k_p, "nt") * scale + b_ref[h]
                    sc = jnp.where(valid, sc, NEG)
                    p = jnp.exp(sc - _pick_lane(lse_t, h))
                    dp = _dot(do_m, v_p, "nt")
                    delta = jnp.sum(jnp.where(masks[a], prod, 0.0), axis=-1, keepdims=True)
                    ds = p * (dp - delta)
                    db_acc[h] += ds
                    ds_b = (ds * scale).astype(BF16)
                    dq_pair = dq_pair + _dot(ds_b, jnp.where(masks[a], k_p, 0))
                    dk_pair = dk_pair + _dot(ds_b, q_m, "tn")
                    dv_pair = dv_pair + _dot(p.astype(BF16), do_m, "tn")
                dq_ref[:, sl] = dq_pair.astype(dq_ref.dtype)
                dk_acc[:, sl] += dk_pair
                dv_acc[:, sl] += dv_pair

        dk_ref[...] = dk_acc[0:QB, :].astype(dk_ref.dtype)
        dv_ref[...] = dv_acc[0:QB, :].astype(dv_ref.dtype)

        @pl.when(i == nq + 1)
        def _():
            sub = lax.broadcasted_iota(jnp.int32, (8, 1), 0)
            pad = jnp.zeros((8, width - KB), F32)
            for h in range(N_HEADS):
                v = jnp.concatenate([db_acc[h, 0:8, :], pad], axis=1)
                for a in range(1, QB // 8):
                    grp = jnp.concatenate([db_acc[h, 8 * a:8 * a + 8, :], pad], axis=1)
                    v = v + pltpu.roll(grp, width - 8 * a, 1)
                for bit in range(3):
                    v = jnp.where(((sub >> bit) & 1) == 1, pltpu.roll(v, width - (1 << bit), 1), v)
                off_ref[h:h + 1, :] = jnp.sum(v, axis=0, keepdims=True)

    qi = lambda i: jnp.minimum(i, nq - 1)
    kblk = lambda col, back: pl.BlockSpec((QB, D_ATTN), lambda i: (jnp.clip(i - back, 0, nq - 1), col))
    qblk = lambda col: pl.BlockSpec((QB, D_ATTN), lambda i: (qi(i), col))
    done = pl.BlockSpec((QB, D_ATTN), lambda i: (jnp.clip(i - 2, 0, nq - 1), 0))
    return _call(
        body, name=name, grid=(nq + 2,),
        in_specs=[qblk(0), kblk(1, 2), kblk(1, 1), kblk(1, 0), kblk(2, 2), kblk(2, 1), kblk(2, 0),
                  qblk(0), pl.BlockSpec((QB, 128), lambda i: (qi(i), 0)), qblk(1),
                  pl.BlockSpec((N_HEADS, QB, KB), lambda i: (0, 0, 0)), pl.BlockSpec((1, D_ATTN), lambda i: (0, 0))],
        out_specs=[qblk(0), done, done, pl.BlockSpec((N_HEADS, width), lambda i: (0, 0)),
                   pl.BlockSpec((1, D_ATTN), lambda i: (0, 0))],
        out_shape=[jax.ShapeDtypeStruct((s, D_ATTN), BF16)] * 3
        + [jax.ShapeDtypeStruct((N_HEADS, width), F32), jax.ShapeDtypeStruct((1, D_ATTN), F32)],
        scratch_shapes=[pltpu.VMEM((KB, D_ATTN), F32), pltpu.VMEM((KB, D_ATTN), F32), pltpu.VMEM((N_HEADS, QB, KB), F32)],
        compiler_params=_params(("arbitrary",), 56),
    )(qkv, qkv, qkv, qkv, qkv, qkv, qkv, o, lse, dy, bias, gg)


def _block_diag(pw):
    out = jnp.zeros((D_POOL, D_POOL), pw.dtype)
    for gi in range(N_POOL):
        out = lax.dynamic_update_slice(out, pw[gi], (gi * POOL_GC, gi * POOL_GC))
    return out


_SMALL = ("pool_w", "pool_scale", "rel_bias", "group_gain", "pre_mix_g", "post_mix_g", "pre_ffn_g", "post_ffn_g")


def _pack(parts, rows):
    flat = jnp.concatenate([p.reshape(-1).astype(F32) for p in parts])
    return jnp.pad(flat, (0, rows * D_MODEL - flat.shape[0])).reshape(rows, D_MODEL)


def _unpack(packed, shapes):
    flat = packed.reshape(-1)
    out, at = [], 0
    for shp in shapes:
        size = int(np.prod(shp))
        out.append(flat[at:at + size].reshape(shp))
        at += size
    return out


def kernel(x, w_in, w_out, conv_w, pool_w, pool_scale, rel_bias, group_gain, pre_mix_g, post_mix_g, pre_ffn_g, post_ffn_g, w_gate_up, w_down, loss_target, m_w_in, m_w_out, m_conv_w, m_pool_w, m_pool_scale, m_rel_bias, m_group_gain, m_pre_mix_g, m_post_mix_g, m_pre_ffn_g, m_post_ffn_g, m_w_gate_up, m_w_down, v_w_in, v_w_out, v_conv_w, v_pool_w, v_pool_scale, v_rel_bias, v_group_gain, v_pre_mix_g, v_post_mix_g, v_pre_ffn_g, v_post_ffn_g, v_w_gate_up, v_w_down):
    depth = w_in.shape[0]
    s = x.shape[1]
    c_me = lax.axis_index("c")
    q_me = 2 * lax.axis_index("x") + lax.axis_index("y")
    dev = 2 * q_me + c_me

    tr = lambda a: jnp.swapaxes(a, 1, 2)
    w_in_t, w_gu_t = tr(w_in), tr(w_gate_up)

    def gather_layer(l, deps):
        first = gather_start(f"gather_a{l}_s", [w_in_t[l].astype(BF16), w_out[l].astype(BF16), conv_w[l]], deps)
        return first, gather_start(f"gather_b{l}_s", [w_gu_t[l].astype(BF16), w_down[l].astype(BF16)], [first["token"]])

    def weights_a(relayed, l, after):
        g_in, g_out, g_cw = gather_finish(f"gather_a{l}_f", relayed, after)
        return (assemble_rows("assemble_in", *g_in), assemble_rows("assemble_out", *g_out),
                assemble_cols("assemble_conv", *g_cw))

    def weights_b(relayed, l, after):
        g_gu, g_dn = gather_finish(f"gather_b{l}_f", relayed, after)
        return assemble_rows("assemble_gu", *g_gu), assemble_rows("assemble_down", *g_dn)

    h = x.reshape(s, D_MODEL)
    started = {0: gather_layer(0, [])}
    relay_a = gather_relay("gather_a0_r", started[0][0], h)
    wa = weights_a(relay_a, 0, relay_a["token"])
    xn = rmsnorm_fwd("norm_mix", h, pre_mix_g[0].reshape(1, -1), None, BF16)
    saved = []
    for l in range(depth):
        vec = lambda p: p[l].reshape(1, -1)
        wt_in, wf_out, cw_full = wa
        ahead = [started[l][1]["token"]]
        if l + 1 < depth:
            started[l + 1] = gather_layer(l + 1, ahead)
            ahead = [started[l + 1][1]["token"]]
        proj_a = matmul("proj_a", xn, wt_in, "nt", F32, s, D_MODEL, D_MODEL, 1024, 512, D_MODEL, deps=ahead)
        qkv = matmul("proj_qkv", xn, wt_in, "nt", BF16, s, 3 * D_ATTN, D_MODEL, 1024, 512, D_MODEL, b_off=(2, 0))
        relay_b = gather_relay(f"gather_b{l}_r", started[l][1], qkv)
        bd = _block_diag(pool_w[l]).astype(BF16)
        bias = attention_bias(rel_bias[l])
        gg = vec(group_gain)
        y_ab = conv_pool_fwd("conv_pool_fwd", proj_a, cw_full, bd, vec(pool_scale), gg[:, :2 * D_CONV],
                             deps=[relay_b["token"]])
        y, o, lse = attention_fwd("attention_fwd", qkv, bias, gg[:, 2 * D_CONV:], y_ab)
        wt_gu, wf_dn = weights_b(relay_b, l, y)
        mix = matmul("mix_out", y, wf_out, "nn", F32, s, D_MODEL, D_MODEL, 1024, 512, D_MODEL)
        if l + 1 < depth:
            relay_a = gather_relay(f"gather_a{l + 1}_r", started[l + 1][0], mix)
        h_mid, hn = rmsnorm_fwd("norm_post_mix", mix, vec(post_mix_g), h, F32, then=(vec(pre_ffn_g), BF16),
                                deps=[relay_a["token"]] if l + 1 < depth else [])
        g, u, ff = matmul_swiglu("gate_up", hn, wt_gu)
        if l + 1 < depth:
            wa = weights_a(relay_a, l + 1, ff)
        ffo = matmul("ffn_down", ff, wf_dn, "nn", F32, s, D_MODEL, D_FF, 512, D_MODEL, D_FF)
        saved.append((h, xn, proj_a, qkv, bd, bias, y, o, lse, mix, h_mid, hn, g, u, ff, ffo,
                      wt_in, wf_out, cw_full, wt_gu, wf_dn))
        if l + 1 < depth:
            h, xn = rmsnorm_fwd("norm_post_ffn", ffo, vec(post_ffn_g), h_mid, F32,
                                then=(pre_mix_g[l + 1].reshape(1, -1), BF16))
        else:
            h = rmsnorm_fwd("norm_post_last", ffo, vec(post_ffn_g), h_mid, F32)

    dh, loss_part = loss_head("loss_head", h, loss_target.reshape(s, D_MODEL))

    def reduce_begin(name, halves):
        return [k for k, _ in halves], exchange_start(name, plan_swap_cores, len(halves), [snd for _, snd in halves],
                                                      [(k.shape, k.dtype) for k, _ in halves])

    def reduce_relay(name, kept, swapped, after):
        _, got = exchange_wait(name + "_w", swapped, after)
        sums = [add_pairs("reduce_add", k, r) for k, r in zip(kept, got)]
        return exchange_start(name + "_s", plan_to_chips, 3 * len(sums), sums, [(a.shape, a.dtype) for a in sums])

    def reduce_finish(name, relayed, after):
        sums, got = exchange_wait(name, relayed, after)
        return list(zip(sums, got))

    small = {k: [None] * depth for k in _SMALL + ("conv_w",)}
    relayed = {}
    dffo, small["post_ffn_g"][depth - 1] = rmsnorm_bwd("norm_bwd_top", saved[-1][15], post_ffn_g[depth - 1].reshape(1, -1),
                                                       dh, None, BF16)
    for l in reversed(range(depth)):
        vec = lambda p: p[l].reshape(1, -1)
        (h_in, xn, proj_a, qkv, bd, bias, y, o, lse, mix, h_mid, hn, g, u, ff, ffo,
         wt_in, wf_out, cw_full, wt_gu, wf_dn) = saved[l]
        gg = vec(group_gain)
        gw_dn = matmul("wgrad_down", ff, dffo, "tn", BF16, D_FF, D_MODEL, s, 1408, D_MODEL, 1024)
        dg, du = matmul_swiglu_bwd("dgrad_down", dffo, wf_dn, g, u)
        gw_gu = matmul("wgrad_gu", [dg, du], hn, "tn", BF16, 2 * D_FF, D_MODEL, s, 1408, D_MODEL, 1024)
        kept_b, swap_b = reduce_begin(f"reduce_b{l}_s", [split_rows("split_gu", gw_gu), split_rows("split_down", gw_dn)])
        dhn = matmul("dgrad_gu", [dg, du], wt_gu, "nn", F32, s, D_MODEL, 2 * D_FF, 1024, 1024, 1408,
                     deps=[swap_b["token"]])
        relayed[l, "b"] = reduce_relay(f"reduce_b{l}_r", kept_b, swap_b, dhn)
        dh_mid, small["pre_ffn_g"][l], dmix, small["post_mix_g"][l] = rmsnorm_bwd(
            "norm_bwd_mid", h_mid, vec(pre_ffn_g), dhn, dh, F32, then=(mix, vec(post_mix_g), BF16),
            deps=[relayed[l, "b"]["token"]])
        gw_out = matmul("wgrad_out", y, dmix, "tn", BF16, D_MODEL, D_MODEL, s, D_MODEL, D_MODEL, 1024)
        dy = matmul("dgrad_out", dmix, wf_out, "nt", BF16, s, D_MODEL, D_MODEL, 1024, 512, D_MODEL)
        dpa, dcw, dbd, dps, dgg_ab = conv_pool_bwd("conv_pool_bwd", proj_a, dy, cw_full, bd, vec(pool_scale),
                                                   gg[:, :2 * D_CONV])
        dq, dk, dv, by_off, dgg_c = attention_bwd("attention_bwd", qkv, o, lse, dy, bias, gg[:, 2 * D_CONV:])
        dparts = [dpa, dq, dk, dv]
        gw_in = matmul("wgrad_in", dparts, xn, "tn", BF16, 5 * D_ATTN, D_MODEL, s, 512, D_MODEL, min(s, 2048))
        kept_a, swap_a = reduce_begin(f"reduce_a{l}_s", [split_rows("split_in", gw_in), split_rows("split_out", gw_out)])
        dxn = matmul("dgrad_in", dparts, wt_in, "nn", F32, s, D_MODEL, 5 * D_ATTN, 1024, 1024, 512,
                     deps=[swap_a["token"]])
        relayed[l, "a"] = reduce_relay(f"reduce_a{l}_r", kept_a, swap_a, dxn)
        if l > 0:
            dh, small["pre_mix_g"][l], dffo, small["post_ffn_g"][l - 1] = rmsnorm_bwd(
                "norm_bwd_in", h_in, vec(pre_mix_g), dxn, dh_mid, F32,
                then=(saved[l - 1][15], post_ffn_g[l - 1].reshape(1, -1), BF16), deps=[relayed[l, "a"]["token"]])
        else:
            dh, small["pre_mix_g"][l] = rmsnorm_bwd("norm_bwd_first", h_in, vec(pre_mix_g), dxn, dh_mid, F32,
                                                    deps=[relayed[l, "a"]["token"]])
        small["conv_w"][l] = dcw
        small["pool_w"][l] = jnp.stack([dbd[gi * POOL_GC:(gi + 1) * POOL_GC, gi * POOL_GC:(gi + 1) * POOL_GC]
                                        for gi in range(N_POOL)])
        small["pool_scale"][l] = dps
        small["rel_bias"][l] = rel_bias_grad(by_off)
        small["group_gain"][l] = jnp.concatenate([dgg_ab, dgg_c], axis=1)
    grad_x = dh.reshape(x.shape)

    small_params = dict(pool_w=pool_w, pool_scale=pool_scale, rel_bias=rel_bias, group_gain=group_gain,
                        pre_mix_g=pre_mix_g, post_mix_g=post_mix_g, pre_ffn_g=pre_ffn_g, post_ffn_g=post_ffn_g)
    small_m = dict(pool_w=m_pool_w, pool_scale=m_pool_scale, rel_bias=m_rel_bias, group_gain=m_group_gain,
                   pre_mix_g=m_pre_mix_g, post_mix_g=m_post_mix_g, pre_ffn_g=m_pre_ffn_g, post_ffn_g=m_post_ffn_g)
    small_v = dict(pool_w=v_pool_w, pool_scale=v_pool_scale, rel_bias=v_rel_bias, group_gain=v_group_gain,
                   pre_mix_g=v_pre_mix_g, post_mix_g=v_post_mix_g, pre_ffn_g=v_pre_ffn_g, post_ffn_g=v_post_ffn_g)
    shapes = [small_params[k].shape for k in _SMALL] + [(depth, 3, D_CONV), (1,)]
    n_small = sum(int(np.prod(shp)) for shp in shapes)
    rows = -(-n_small // (8 * D_MODEL)) * 8
    extra = [jnp.zeros((depth, 3, D_CONV), F32), jnp.zeros((1,), F32)]
    grads_packed = _pack([jnp.stack(small[k]) for k in _SMALL] + [jnp.stack(small["conv_w"]), loss_part[0, 0:1]], rows)
    small_started = gather_start("gather_small_s", [grads_packed])

    big = dict(w_in=(w_in_t, tr(m_w_in), tr(v_w_in)), w_out=(w_out, m_w_out, v_w_out),
               w_gate_up=(w_gu_t, tr(m_w_gate_up), tr(v_w_gate_up)), w_down=(w_down, m_w_down, v_w_down))
    results = dict(w_in=None, w_out=None, w_gate_up=None, w_down=None)
    after = small_started["token"]
    small_relay = None
    for l in reversed(range(depth)):
        slabs_gu, slabs_dn = reduce_finish(f"reduce_b{l}_f", relayed[l, "b"], after)
        slabs_in, slabs_out = reduce_finish(f"reduce_a{l}_f", relayed[l, "a"], slabs_dn[1])
        for k, slabs in (("w_gate_up", slabs_gu), ("w_down", slabs_dn), ("w_in", slabs_in), ("w_out", slabs_out)):
            results[k] = adamw_layer("adamw_" + k, l, *big[k], *slabs, results[k])
        after = results["w_out"][0]
        if small_relay is None:
            small_relay = gather_relay("gather_small_r", small_started, after)
    small_parts = gather_finish("gather_small_f", small_relay, after)[0]
    all_small = assemble_rows("assemble_small", *small_parts).reshape(8, rows, D_MODEL)
    res_small = adamw("adamw_small", _pack([small_params[k] for k in _SMALL] + extra, rows),
                      _pack([small_m[k] for k in _SMALL] + extra, rows),
                      _pack([small_v[k] for k in _SMALL] + extra, rows), all_small)
    g_s, d_s, m_s, v_s = (_unpack(r, shapes) for r in res_small)
    loss = g_s[-1][0]
    g_conv = lax.dynamic_slice_in_dim(g_s[-2], dev * conv_w.shape[2], conv_w.shape[2], axis=2)
    results["conv_w"] = adamw("adamw_conv", conv_w, m_conv_w, v_conv_w, g_conv[None])

    names = ("w_in", "w_out", "conv_w") + _SMALL + ("w_gate_up", "w_down")
    for k in ("w_in", "w_gate_up"):
        results[k] = tuple(tr(r) for r in results[k])
    for j, k in enumerate(_SMALL):
        results[k] = (g_s[j], d_s[j], m_s[j], v_s[j])
    return (loss, grad_x, *[results[k][0] for k in names], *[results[k][1] for k in names],
            *[results[k][2] for k in names], *[results[k][3] for k in names])
```

```python
import numpy as np
import jax
import jax.numpy as jnp
from jax import lax
from jax.experimental import pallas as pl
from jax.experimental.pallas import tpu as pltpu

F32 = jnp.float32
BF16 = jnp.bfloat16

CHUNK = 64
D_MODEL = 1024
D_CONV = 256
D_POOL = 256
D_ATTN = 512
HEAD_DIM = 64
N_HEADS = 8
N_POOL = 4
POOL_GC = 64
POOL_WINDOWS = (2, 4, 8, 16)
LEFT_CHUNKS = 8
REL_CLIP = 128
D_FF = 2816
EPS = 1e-6
ADAM_LR, ADAM_B1, ADAM_B2, ADAM_EPS, ADAM_WD, ADAM_STEP = 0.001, 0.9, 0.999, 1e-08, 0.01, 10

QB = 256
KB = 3 * QB
HALO = 16
T_CP = 512
T_ROW = 512
QKV_COL = 2
NEG = -1e30
VMEM_MB = 1 << 20
MESH = pl.DeviceIdType.MESH


def _call(body, **kw):
    call = pl.pallas_call(body, **kw)
    return lambda *args: call(*[_in_hbm(a) for a in args])


def _in_hbm(a):
    return pltpu.with_memory_space_constraint(a, pltpu.HBM) if jnp.issubdtype(a.dtype, jnp.number) else a


def _call_after(deps, n_in, body, **kw):
    deps = tuple(deps)
    if not deps:
        return _call(body, **kw)

    def ordered(*refs):
        body(*refs[:n_in], *refs[n_in + len(deps):])

    kw["in_specs"] = list(kw["in_specs"]) + [pl.BlockSpec(memory_space=pl.ANY)] * len(deps)
    call = _call(ordered, **kw)
    return lambda *args: call(*args, *deps)


def _params(sem, vmem_mb=48):
    return pltpu.CompilerParams(dimension_semantics=sem, vmem_limit_bytes=vmem_mb * VMEM_MB)


_CHIP_FLIPS = ((1, 0), (0, 1), (1, 1))
_HBM = pl.BlockSpec(memory_space=pltpu.HBM)
_SEM = pl.BlockSpec(memory_space=pltpu.SEMAPHORE)
_EFFECT = pltpu.SideEffectType.DATAFLOW_SIDE_EFFECTING


def _flip(v, f):
    return 1 - v if f else v


def _descriptors(plan, srcs, lands, send_sems, recv_sems, sending):
    x, y, c = lax.axis_index("x"), lax.axis_index("y"), lax.axis_index("c")
    return [pltpu.make_async_remote_copy(src_ref=src, dst_ref=dst if sending else land, send_sem=send_sems.at[k],
                                         recv_sem=recv_sems.at[k], device_id=peer, device_id_type=MESH)
            for k, (src, dst, peer, land) in enumerate(plan(srcs, lands, x, y, c))]


def exchange_start(name, plan, n_copies, srcs, land_shapes, deps=()):
    ns, nl = len(srcs), len(land_shapes)

    def body(*refs):
        src_refs, land_refs = refs[:ns], refs[ns:ns + nl]
        send_sems, recv_sems = refs[ns + nl], refs[ns + nl + 1]
        for send in _descriptors(plan, src_refs, land_refs, send_sems, recv_sems, True):
            send.start()
        refs[-1][...] = jnp.zeros_like(refs[-1])

    lands = [lax.empty(shape, dtype) for shape, dtype in land_shapes]
    outs = _call_after(
        deps, ns + nl, body, name=name,
        out_shape=(pltpu.SemaphoreType.DMA((n_copies,)), pltpu.SemaphoreType.DMA((n_copies,)),
                   *[pltpu.HBM(a.shape, a.dtype) for a in srcs], *[pltpu.HBM(shape, dtype) for shape, dtype in land_shapes],
                   jax.ShapeDtypeStruct((8, 128), F32)),
        in_specs=[_HBM] * (ns + nl),
        out_specs=(_SEM, _SEM, *[_HBM] * (ns + nl), pl.BlockSpec(memory_space=pltpu.VMEM)),
        input_output_aliases={j: 2 + j for j in range(ns + nl)},
        compiler_params=pltpu.CompilerParams(has_side_effects=_EFFECT),
    )(*srcs, *lands)
    return dict(plan=plan, sems=outs[:2], srcs=outs[2:2 + ns], lands=outs[2 + ns:2 + ns + nl], token=outs[-1])


def exchange_wait(name, started, after):
    srcs, lands = started["srcs"], started["lands"]
    ns, nl = len(srcs), len(lands)

    def body(*refs):
        src_refs, land_refs = refs[:ns], refs[ns:ns + nl]
        send_sems, recv_sems = refs[ns + nl], refs[ns + nl + 1]
        for wait in _descriptors(started["plan"], src_refs, land_refs, send_sems, recv_sems, False):
            wait.wait_send()
            wait.wait_recv()

    outs = _call(
        body, name=name,
        out_shape=tuple(pltpu.HBM(a.shape, a.dtype) for a in (*srcs, *lands)),
        in_specs=[_HBM] * (ns + nl) + [_SEM, _SEM, pl.BlockSpec(memory_space=pl.ANY)],
        out_specs=tuple([_HBM] * (ns + nl)),
        input_output_aliases={j: j for j in range(ns + nl)},
        compiler_params=pltpu.CompilerParams(has_side_effects=_EFFECT),
    )(*srcs, *lands, *started["sems"], after)
    return list(outs[:ns]), list(outs[ns:])


def plan_from_chips(srcs, lands, x, y, c):
    q = 2 * x + y
    out = []
    for src, land in zip(srcs, lands):
        for fx, fy in _CHIP_FLIPS:
            px, py = _flip(x, fx), _flip(y, fy)
            out.append((src, land.at[q], (px, py, c), land.at[2 * px + py]))
    return out


def plan_to_core(srcs, lands, x, y, c):
    n = len(lands)
    q = 2 * x + y
    out = []
    for own, chips, land in zip(srcs[:n], srcs[n:], lands):
        out.append((own, land.at[q], (x, y, 1 - c), land.at[q]))
        for fx, fy in _CHIP_FLIPS:
            qp = 2 * _flip(x, fx) + _flip(y, fy)
            out.append((chips.at[qp], land.at[qp], (x, y, 1 - c), land.at[qp]))
    return out


def plan_swap_cores(srcs, lands, x, y, c):
    return [(src, land, (x, y, 1 - c), land) for src, land in zip(srcs, lands)]


def plan_to_chips(srcs, lands, x, y, c):
    q = 2 * x + y
    out = []
    for src, land in zip(srcs, lands):
        for fx, fy in _CHIP_FLIPS:
            px, py = _flip(x, fx), _flip(y, fy)
            out.append((src.at[2 * px + py], land.at[q], (px, py, c), land.at[2 * px + py]))
    return out


def _slots(a):
    return ((4,) + a.shape, a.dtype)


def gather_start(name, arrays, deps=()):
    return exchange_start(name, plan_from_chips, 3 * len(arrays), arrays, [_slots(a) for a in arrays], deps)


def gather_relay(name, started, after):
    own, chips = exchange_wait(name + "_w", started, after)
    return exchange_start(name + "_s", plan_to_core, 4 * len(own), own + chips, [_slots(a) for a in own])


def gather_finish(name, relayed, after):
    srcs, cores = exchange_wait(name, relayed, after)
    n = len(cores)
    return list(zip(srcs[:n], srcs[n:], cores))


def _place():
    return 2 * lax.axis_index("x") + lax.axis_index("y"), lax.axis_index("c")


def assemble_cols(name, own, chips, core):
    r, c = own.shape
    t = _row_tile(r, 256)

    def body(own_ref, chips_ref, core_ref, o_ref):
        q_me, c_me = _place()
        for q in range(4):
            mine = jnp.where(q == q_me, own_ref[...], chips_ref[q])
            other = core_ref[q]
            o_ref[:, 2 * q * c:(2 * q + 1) * c] = jnp.where(c_me == 0, mine, other)
            o_ref[:, (2 * q + 1) * c:(2 * q + 2) * c] = jnp.where(c_me == 0, other, mine)

    slots = pl.BlockSpec((4, t, c), lambda i: (0, i, 0))
    return _call(
        body, name=name, grid=(r // t,),
        in_specs=[pl.BlockSpec((t, c), lambda i: (i, 0)), slots, slots],
        out_specs=pl.BlockSpec((t, 8 * c), lambda i: (i, 0)),
        out_shape=jax.ShapeDtypeStruct((r, 8 * c), own.dtype), compiler_params=_params(("parallel",)),
    )(own, chips, core)


def assemble_rows(name, own, chips, core):
    r, c = own.shape

    def body(own_ref, chips_ref, core_ref, o_ref):
        q_me, c_me = _place()
        d = pl.program_id(0)
        mine = jnp.where(d // 2 == q_me, own_ref[...], chips_ref[...])
        o_ref[...] = jnp.where(d % 2 == c_me, mine, core_ref[...])

    slot = pl.BlockSpec((None, r, c), lambda d: (d // 2, 0, 0))
    return _call(
        body, name=name, grid=(8,),
        in_specs=[pl.BlockSpec((r, c), lambda d: (0, 0)), slot, slot],
        out_specs=pl.BlockSpec((r, c), lambda d: (d, 0)),
        out_shape=jax.ShapeDtypeStruct((8 * r, c), own.dtype), compiler_params=_params(("parallel",)),
    )(own, chips, core)


def split_rows(name, dw):
    r8, c = dw.shape
    r = r8 // 8

    def body(dw_ref, keep_ref, send_ref):
        _, c_me = _place()
        d = pl.program_id(0)

        @pl.when(d % 2 == c_me)
        def _():
            keep_ref[...] = dw_ref[...]

        @pl.when(d % 2 != c_me)
        def _():
            send_ref[...] = dw_ref[...]

    slot = pl.BlockSpec((None, r, c), lambda d: (d // 2, 0, 0))
    return _call(
        body, name=name, grid=(8,),
        in_specs=[pl.BlockSpec((r, c), lambda d: (d, 0))], out_specs=[slot, slot],
        out_shape=[jax.ShapeDtypeStruct((4, r, c), dw.dtype)] * 2, compiler_params=_params(("arbitrary",)),
    )(dw)


def _rms(v):
    return lax.rsqrt(jnp.mean(v * v, axis=-1, keepdims=True) + EPS)


def rmsnorm_fwd(name, x, g, out_dtype):
    s, d = x.shape

    def body(x_ref, g_ref, o_ref):
        xv = x_ref[...]
        o_ref[...] = ((xv * _rms(xv)) * g_ref[...]).astype(o_ref.dtype)

    row = pl.BlockSpec((T_ROW, d), lambda i: (i, 0))
    return _call(
        body, name=name, grid=(s // T_ROW,),
        in_specs=[row, pl.BlockSpec((1, d), lambda i: (0, 0))], out_specs=row,
        out_shape=jax.ShapeDtypeStruct((s, d), out_dtype), compiler_params=_params(("parallel",)),
    )(x, g)


def matmul_norm(name, a, b, res, g, then=None, deps=()):
    s, k = a.shape
    d = b.shape[1]
    tm = 512

    def body(a_ref, b_ref, r_ref, g_ref, *rest):
        z = _dot(a_ref[...], b_ref[...])
        z_ref, y_ref = rest[-3:-1] if then else rest
        z_ref[...] = z.astype(z_ref.dtype)
        y = r_ref[...] + (z * _rms(z)) * g_ref[...]
        y_ref[...] = y
        if then:
            rest[-1][...] = ((y * _rms(y)) * rest[0][...]).astype(rest[-1].dtype)

    row = pl.BlockSpec((tm, d), lambda i: (i, 0))
    vec = pl.BlockSpec((1, d), lambda i: (0, 0))
    args = (a, b, res, g) + ((then[0],) if then else ())
    shapes = [jax.ShapeDtypeStruct((s, d), BF16), jax.ShapeDtypeStruct((s, d), F32)]
    return _call_after(
        deps, len(args), body, name=name, grid=(s // tm,),
        in_specs=[pl.BlockSpec((tm, k), lambda i: (i, 0)), pl.BlockSpec((k, d), lambda i: (0, 0)), row, vec]
        + ([vec] if then else []),
        out_specs=[row] * (3 if then else 2),
        out_shape=shapes + ([jax.ShapeDtypeStruct((s, d), then[1])] if then else []),
        compiler_params=_params(("parallel",)),
    )(*args)


def _norm_bwd(xv, r, g, dy):
    a = dy * g
    dx = r * (a - xv * ((r * r) * jnp.mean(a * xv, axis=-1, keepdims=True)))
    return dx, dy * (xv * r)


def rmsnorm_bwd(name, x, g, dy, res, out_dtype, then=None, deps=()):
    s, d = x.shape
    has_res = res is not None
    n_in = 3 + has_res + (2 if then else 0)

    def body(*refs):
        x_ref, g_ref, dy_ref = refs[:3]
        outs = refs[n_in:]
        xv = x_ref[...].astype(F32)
        dx, dgt = _norm_bwd(xv, _rms(xv), g_ref[...], dy_ref[...].astype(F32))
        if has_res:
            dx = dx + refs[3][...]
        outs[0][...] = dx.astype(outs[0].dtype)
        sums = [(outs[1], dgt)]
        if then is not None:
            x2 = refs[n_in - 2][...].astype(F32)
            dx2, dgt2 = _norm_bwd(x2, _rms(x2), refs[n_in - 1][...], dx)
            outs[2][...] = dx2.astype(outs[2].dtype)
            sums.append((outs[3], dgt2))

        @pl.when(pl.program_id(0) == 0)
        def _():
            for dg_ref, _ in sums:
                dg_ref[...] = jnp.zeros_like(dg_ref)
        for dg_ref, terms in sums:
            dg_ref[...] += jnp.sum(terms, axis=0, keepdims=True)

    row = pl.BlockSpec((T_ROW, d), lambda i: (i, 0))
    vec = pl.BlockSpec((1, d), lambda i: (0, 0))
    args = (x, g, dy) + ((res,) if has_res else ()) + ((then[0], then[1]) if then else ())
    shapes = [jax.ShapeDtypeStruct((s, d), out_dtype), jax.ShapeDtypeStruct((1, d), F32)]
    if then:
        shapes += [jax.ShapeDtypeStruct((s, d), then[2]), jax.ShapeDtypeStruct((1, d), F32)]
    return _call_after(
        deps, len(args), body, name=name, grid=(s // T_ROW,),
        in_specs=[row, vec, row] + ([row] if has_res else []) + ([row, vec] if then else []),
        out_specs=[row, vec] * (2 if then else 1), out_shape=shapes,
        compiler_params=_params(("arbitrary",)),
    )(*args)


def loss_head(name, h, tgt):
    s, d = h.shape

    def body(h_ref, t_ref, dh_ref, l_ref):
        e = h_ref[...] - t_ref[...]
        dh_ref[...] = e * (1.0 / d)

        @pl.when(pl.program_id(0) == 0)
        def _():
            l_ref[...] = jnp.zeros_like(l_ref)
        part = 0.5 * jnp.sum(jnp.mean(e * e, axis=-1, keepdims=True), axis=0, keepdims=True)
        l_ref[...] += jnp.broadcast_to(part, l_ref.shape)

    row = pl.BlockSpec((T_ROW, d), lambda i: (i, 0))
    return _call(
        body, name=name, grid=(s // T_ROW,),
        in_specs=[row, row], out_specs=[row, pl.BlockSpec((1, 128), lambda i: (0, 0))],
        out_shape=[jax.ShapeDtypeStruct((s, d), F32), jax.ShapeDtypeStruct((1, 128), F32)],
        compiler_params=_params(("arbitrary",)),
    )(h, tgt)


def _row_tile(rows, limit=512):
    t = min(rows, limit)
    while rows % t or (t % 8 and t != rows):
        t -= 1
    return t


def add_pairs(name, a, b):
    shape = a.shape
    a2, b2 = a.reshape(-1, shape[-1]), b.reshape(-1, shape[-1])
    rows, cols = a2.shape
    t = _row_tile(rows)

    def body(a_ref, b_ref, o_ref):
        o_ref[...] = (a_ref[...].astype(F32) + b_ref[...].astype(F32)).astype(o_ref.dtype)

    blk = pl.BlockSpec((t, cols), lambda i: (i, 0))
    out = _call(
        body, name=name, grid=(rows // t,), in_specs=[blk, blk], out_specs=blk,
        out_shape=jax.ShapeDtypeStruct((rows, cols), a.dtype), compiler_params=_params(("parallel",)),
    )(a2, b2)
    return out.reshape(shape)


def _adamw_math(w, m, v, g):
    c1 = 1.0 - ADAM_B1 ** ADAM_STEP
    c2 = 1.0 - ADAM_B2 ** ADAM_STEP
    mn = ADAM_B1 * m + (1.0 - ADAM_B1) * g
    vn = ADAM_B2 * v + (1.0 - ADAM_B2) * (g * g)
    delta = -ADAM_LR * ((mn / c1) / (jnp.sqrt(vn / c2) + ADAM_EPS) + ADAM_WD * w)
    return delta, mn, vn


def _slab_sum(g_ref, n):
    g = g_ref[0].astype(F32)
    for j in range(1, n):
        g = g + g_ref[j].astype(F32)
    return g


def adamw(name, w, m, v, g_slabs, deps=()):
    shape = w.shape
    n = g_slabs.shape[0]
    w2, m2, v2 = (t.reshape(-1, shape[-1]) for t in (w, m, v))
    g3 = g_slabs.reshape(n, -1, shape[-1])
    rows, cols = w2.shape
    t = _row_tile(rows, 256)

    def body(w_ref, m_ref, v_ref, g_ref, go_ref, d_ref, mo_ref, vo_ref):
        g = _slab_sum(g_ref, n)
        go_ref[...] = g
        d_ref[...], mo_ref[...], vo_ref[...] = _adamw_math(w_ref[...], m_ref[...], v_ref[...], g)

    blk = pl.BlockSpec((t, cols), lambda i: (i, 0))
    outs = _call_after(
        deps, 4, body, name=name, grid=(rows // t,),
        in_specs=[blk, blk, blk, pl.BlockSpec((n, t, cols), lambda i: (0, i, 0))], out_specs=[blk] * 4,
        out_shape=[jax.ShapeDtypeStruct((rows, cols), F32)] * 4, compiler_params=_params(("parallel",)),
    )(w2, m2, v2, g3)
    return tuple(o.reshape(shape) for o in outs)


def adamw_layer(name, l, w, m, v, own_sums, chip_sums, into):
    _, rows, cols = w.shape
    t = _row_tile(rows, 256)
    if into is None:
        into = tuple(lax.empty(w.shape, F32) for _ in range(4))

    def body(w_ref, m_ref, v_ref, own_ref, far_ref, a0, a1, a2, a3, go_ref, d_ref, mo_ref, vo_ref):
        del a0, a1, a2, a3
        q_me, _ = _place()
        g = jnp.zeros((t, cols), F32)
        for q in range(4):
            g = g + jnp.where(q == q_me, own_ref[q], far_ref[q]).astype(F32)
        go_ref[...] = g
        d_ref[...], mo_ref[...], vo_ref[...] = _adamw_math(w_ref[...], m_ref[...], v_ref[...], g)

    blk = pl.BlockSpec((None, t, cols), lambda i: (l, i, 0))
    slabs = pl.BlockSpec((4, t, cols), lambda i: (0, i, 0))
    anyw = pl.BlockSpec(memory_space=pl.ANY)
    return _call(
        body, name=name, grid=(rows // t,),
        in_specs=[blk, blk, blk, slabs, slabs] + [anyw] * 4, out_specs=[blk] * 4,
        out_shape=[jax.ShapeDtypeStruct(w.shape, F32)] * 4, input_output_aliases={5: 0, 6: 1, 7: 2, 8: 3},
        compiler_params=_params(("parallel",)),
    )(w, m, v, own_sums, chip_sums, *into)


_DIMS = {"nn": (((1,), (0,)), ((), ())), "nt": (((1,), (1,)), ((), ())), "tn": (((0,), (0,)), ((), ()))}


def _dot(a, b, mode="nn"):
    return lax.dot_general(a, b, _DIMS[mode], preferred_element_type=F32)


def matmul(name, a, b, mode, out_dtype, m, n, k, tm, tn, tk, b_off=(0, 0), deps=()):
    gm, gn, gk = m // tm, n // tn, k // tk
    assert gm * tm == m and gn * tn == n and gk * tk == k
    r0, c0 = b_off
    a_parts = list(a) if isinstance(a, (list, tuple)) else [a]
    na = len(a_parts)
    tile = tm if mode == "tn" else tk
    spans, at = [], 0
    for p in a_parts:
        nblk = p.shape[1] // tile
        assert nblk * tile == p.shape[1]
        spans.append((at, nblk))
        at += nblk
    assert at == (gm if mode == "tn" else gk)

    def within(t, span):
        return (t >= span[0]) & (t < span[0] + span[1])

    def local(t, span):
        return jnp.clip(t - span[0], 0, span[1] - 1)

    a_specs = []
    for sp in spans:
        if mode == "tn":
            a_specs.append(pl.BlockSpec((tk, tm), lambda i, j, kk, sp=sp: (jnp.where(within(i, sp), kk, 0), local(i, sp))))
        else:
            a_specs.append(pl.BlockSpec((tm, tk), lambda i, j, kk, sp=sp: (i, local(kk, sp))))
    if mode == "nt":
        b_spec = pl.BlockSpec((tn, tk), lambda i, j, kk: (j + r0, kk + c0))
    else:
        b_spec = pl.BlockSpec((tk, tn), lambda i, j, kk: (kk + r0, j + c0))
    o_spec = pl.BlockSpec((tm, tn), lambda i, j, kk: (i, j))
    single = na == 1 and gk == 1

    def body(*refs):
        a_refs, b_ref, o_ref = refs[:na], refs[na], refs[na + 1]
        if single:
            o_ref[...] = _dot(a_refs[0][...], b_ref[...], mode).astype(o_ref.dtype)
            return
        acc_ref = refs[-1]
        i, kk = pl.program_id(0), pl.program_id(2)

        @pl.when(kk == 0)
        def _():
            acc_ref[...] = jnp.zeros_like(acc_ref)

        for pa, sp in enumerate(spans):
            def add(pa=pa):
                acc_ref[...] += _dot(a_refs[pa][...], b_ref[...], mode)
            if na > 1:
                pl.when(within(i if mode == "tn" else kk, sp))(add)
            else:
                add()

        @pl.when(kk == gk - 1)
        def _():
            o_ref[...] = acc_ref[...].astype(o_ref.dtype)

    return _call_after(
        deps, na + 1, body, name=name, grid=(gm, gn, gk),
        in_specs=a_specs + [b_spec], out_specs=o_spec,
        out_shape=jax.ShapeDtypeStruct((m, n), out_dtype),
        scratch_shapes=[] if single else [pltpu.VMEM((tm, tn), F32)],
        compiler_params=_params(("parallel", "parallel", "arbitrary")),
    )(*a_parts, b)


def matmul_swiglu(name, a, w_gu_t, deps=()):
    s, d = a.shape
    tm, tn = 512, 1408
    gn = D_FF // tn

    def body(a_ref, wg_ref, wu_ref, g_ref, u_ref, f_ref):
        av = a_ref[...]
        g = _dot(av, wg_ref[...], "nt")
        u = _dot(av, wu_ref[...], "nt")
        g_ref[...] = g.astype(g_ref.dtype)
        u_ref[...] = u.astype(u_ref.dtype)
        f_ref[...] = ((g * (1.0 / (1.0 + jnp.exp(-g)))) * u).astype(f_ref.dtype)

    o_spec = pl.BlockSpec((tm, tn), lambda j, i: (i, j))
    return _call_after(
        deps, 3, body, name=name, grid=(gn, s // tm),
        in_specs=[pl.BlockSpec((tm, d), lambda j, i: (i, 0)), pl.BlockSpec((tn, d), lambda j, i: (j, 0)),
                  pl.BlockSpec((tn, d), lambda j, i: (j + gn, 0))],
        out_specs=[o_spec] * 3, out_shape=[jax.ShapeDtypeStruct((s, D_FF), BF16)] * 3,
        compiler_params=_params(("parallel", "parallel")),
    )(a, w_gu_t, w_gu_t)


def matmul_swiglu_bwd(name, dffo, w_down, g, u):
    s, d = dffo.shape
    tm, tn = 512, 1408

    def body(a_ref, w_ref, g_ref, u_ref, dg_ref, du_ref):
        dff = _dot(a_ref[...], w_ref[...], "nt")
        gv = g_ref[...].astype(F32)
        sig = 1.0 / (1.0 + jnp.exp(-gv))
        du_ref[...] = (dff * (gv * sig)).astype(du_ref.dtype)
        dg_ref[...] = (dff * u_ref[...].astype(F32) * (sig * (1.0 + gv * (1.0 - sig)))).astype(dg_ref.dtype)

    o_spec = pl.BlockSpec((tm, tn), lambda j, i: (i, j))
    return _call(
        body, name=name, grid=(D_FF // tn, s // tm),
        in_specs=[pl.BlockSpec((tm, d), lambda j, i: (i, 0)), pl.BlockSpec((tn, d), lambda j, i: (j, 0)), o_spec, o_spec],
        out_specs=[o_spec] * 2, out_shape=[jax.ShapeDtypeStruct((s, D_FF), BF16)] * 2,
        compiler_params=_params(("parallel", "parallel")),
    )(dffo, w_down, g, u)


def _down(v, n):
    return pltpu.roll(v, n, 0)


def _up(v, n):
    return pltpu.roll(v, v.shape[0] - n, 0)


def _by_window(lane, v2, v4, v8, v16):
    return jnp.where(lane < POOL_GC, v2, jnp.where(lane < 2 * POOL_GC, v4, jnp.where(lane < 3 * POOL_GC, v8, v16)))


def _taps(cw_ref):
    return cw_ref[0:1, :], cw_ref[1:2, :], cw_ref[2:3, :]


def _conv_pool_forward(ext, t0, cw, bd):
    n_out = ext.shape[0] - HALO
    gb = ext[HALO:, 0:D_CONV]
    z = ext[:, D_CONV:2 * D_CONV] * ext[:, 2 * D_CONV:3 * D_CONV]
    z0, z1, z2 = z[HALO:], _down(z, 1)[HALO:], _down(z, 2)[HALO:]
    conv = cw[2] * z0 + cw[1] * z1 + cw[0] * z2
    x = ext[:, 3 * D_CONV:]
    w2 = x + _down(x, 1)
    w4 = w2 + _down(w2, 2)
    w8 = w4 + _down(w4, 4)
    w16 = w8 + _down(w8, 8)
    lane = lax.broadcasted_iota(jnp.int32, (1, D_POOL), 1)
    win = _by_window(lane, 2.0, 4.0, 8.0, 16.0)
    pos = (t0 + lax.broadcasted_iota(jnp.int32, (n_out, 1), 0) + 1).astype(F32)
    cnt = jnp.minimum(pos, win)
    d = _by_window(lane, w2, w4, w8, w16)[HALO:] / cnt - x[HALO:]
    ybp = _dot(d.astype(BF16), bd)
    return gb, z0, z1, z2, conv, d, cnt, ybp, lane


def conv_pool_fwd(name, proj_a, cw, bd, ps, gg, deps=()):
    s = proj_a.shape[0]
    t = T_CP
    hb = t // HALO

    def body(main_ref, prev_ref, cw_ref, bd_ref, ps_ref, gg_ref, y_ref):
        i = pl.program_id(0)
        prev = jnp.where(i > 0, prev_ref[...].astype(F32), 0.0)
        ext = jnp.concatenate([prev, main_ref[...].astype(F32)], axis=0)
        gb, _, _, _, conv, _, _, ybp, _ = _conv_pool_forward(ext, i * t, _taps(cw_ref), bd_ref[...])
        ya = gb * conv
        yb = ybp * ps_ref[...]
        ggv = gg_ref[...]
        y_ref[:, 0:D_CONV] = ((ya * _rms(ya)) * ggv[:, 0:D_CONV]).astype(y_ref.dtype)
        y_ref[:, D_CONV:] = ((yb * _rms(yb)) * ggv[:, D_CONV:]).astype(y_ref.dtype)

    full = lambda shape: pl.BlockSpec(shape, lambda i: (0,) * len(shape))
    return _call_after(
        deps, 6, body, name=name, grid=(s // t,),
        in_specs=[pl.BlockSpec((t, D_MODEL), lambda i: (i, 0)),
                  pl.BlockSpec((HALO, D_MODEL), lambda i: (jnp.maximum(i * hb - 1, 0), 0)),
                  full((3, D_CONV)), full((D_POOL, D_POOL)), full((1, D_POOL)), full((1, 2 * D_CONV))],
        out_specs=pl.BlockSpec((t, 2 * D_CONV), lambda i: (i, 0)),
        out_shape=jax.ShapeDtypeStruct((s, D_MODEL), BF16),
        compiler_params=_params(("parallel",)),
    )(proj_a, proj_a, cw, bd, ps, gg)


def conv_pool_bwd(name, proj_a, dy, cw, bd, ps, gg):
    s = proj_a.shape[0]
    t = T_CP
    hb = t // HALO
    nblk = s // t
    last_halo = s // HALO - 1

    def body(main_ref, prev_ref, next_ref, dy_ref, dyn_ref, cw_ref, bd_ref, ps_ref, gg_ref,
             dp_ref, dcw_ref, dbd_ref, dps_ref, dgg_ref):
        i = pl.program_id(0)
        prev = jnp.where(i > 0, prev_ref[...].astype(F32), 0.0)
        main = main_ref[...].astype(F32)
        ext = jnp.concatenate([prev, main, next_ref[...].astype(F32)], axis=0)
        cwv, bdv, psv, ggv = _taps(cw_ref), bd_ref[...], ps_ref[...], gg_ref[...]
        gb, z0, z1, z2, conv, d, cnt, ybp, lane = _conv_pool_forward(ext, i * t, cwv, bdv)
        dyn = jnp.where(i < nblk - 1, dyn_ref[...].astype(F32), 0.0)
        dyv = jnp.concatenate([dy_ref[...].astype(F32), dyn], axis=0)
        ya = gb * conv
        yb = ybp * psv
        dya, dgg_a = _norm_bwd(ya, _rms(ya), ggv[:, 0:D_CONV], dyv[:, 0:D_CONV])
        dyb, dgg_b = _norm_bwd(yb, _rms(yb), ggv[:, D_CONV:], dyv[:, D_CONV:])

        dconv = dya * gb
        dz = (cwv[2] * dconv + cwv[1] * _up(dconv, 1) + cwv[0] * _up(dconv, 2))[:t]
        dp_ref[:, 0:D_CONV] = (dya * conv)[:t].astype(dp_ref.dtype)
        dp_ref[:, D_CONV:2 * D_CONV] = (dz * main[:, 2 * D_CONV:3 * D_CONV]).astype(dp_ref.dtype)
        dp_ref[:, 2 * D_CONV:3 * D_CONV] = (dz * main[:, D_CONV:2 * D_CONV]).astype(dp_ref.dtype)

        dybs = dyb * psv
        dd = _dot(dybs.astype(BF16), bdv, "nt")
        e = dd / cnt
        a2 = e + _up(e, 1)
        a4 = a2 + _up(a2, 2)
        a8 = a4 + _up(a4, 4)
        a16 = a8 + _up(a8, 8)
        dp_ref[:, 3 * D_CONV:] = (_by_window(lane, a2, a4, a8, a16) - dd)[:t].astype(dp_ref.dtype)

        @pl.when(i == 0)
        def _():
            dcw_ref[...] = jnp.zeros_like(dcw_ref)
            dbd_ref[...] = jnp.zeros_like(dbd_ref)
            dps_ref[...] = jnp.zeros_like(dps_ref)
            dgg_ref[...] = jnp.zeros_like(dgg_ref)

        rsum = lambda v: jnp.sum(v[:t], axis=0, keepdims=True)
        dcw_ref[0:1, :] += rsum(dconv * z2)
        dcw_ref[1:2, :] += rsum(dconv * z1)
        dcw_ref[2:3, :] += rsum(dconv * z0)
        dbd_ref[...] += _dot(d[:t].astype(BF16), dybs[:t].astype(BF16), "tn")
        dps_ref[...] += rsum(dyb * ybp)
        dgg_ref[:, 0:D_CONV] += rsum(dgg_a)
        dgg_ref[:, D_CONV:] += rsum(dgg_b)

    full = lambda shape: pl.BlockSpec(shape, lambda i: (0,) * len(shape))
    next_halo = lambda i: (jnp.minimum((i + 1) * hb, last_halo), 0)
    return _call(
        body, name=name, grid=(nblk,),
        in_specs=[pl.BlockSpec((t, D_MODEL), lambda i: (i, 0)),
                  pl.BlockSpec((HALO, D_MODEL), lambda i: (jnp.maximum(i * hb - 1, 0), 0)),
                  pl.BlockSpec((HALO, D_MODEL), next_halo),
                  pl.BlockSpec((t, 2 * D_CONV), lambda i: (i, 0)),
                  pl.BlockSpec((HALO, 2 * D_CONV), next_halo),
                  full((3, D_CONV)), full((D_POOL, D_POOL)), full((1, D_POOL)), full((1, 2 * D_CONV))],
        out_specs=[pl.BlockSpec((t, D_MODEL), lambda i: (i, 0)),
                   full((3, D_CONV)), full((D_POOL, D_POOL)), full((1, D_POOL)), full((1, 2 * D_CONV))],
        out_shape=[jax.ShapeDtypeStruct((s, D_MODEL), BF16), jax.ShapeDtypeStruct((3, D_CONV), F32),
                   jax.ShapeDtypeStruct((D_POOL, D_POOL), F32), jax.ShapeDtypeStruct((1, D_POOL), F32),
                   jax.ShapeDtypeStruct((1, 2 * D_CONV), F32)],
        compiler_params=_params(("arbitrary",)),
    )(proj_a, proj_a, proj_a, dy, dy, cw, bd, ps, gg)


def _band_mask():
    qa = np.arange(QB)[:, None] // CHUNK
    km = np.arange(KB)[None, :] // CHUNK
    return (km - qa >= 0) & (km - qa <= LEFT_CHUNKS)


def attention_bias(rel_bias):
    n_far = 2 * QB - REL_CLIP + 1
    far = jnp.broadcast_to(rel_bias[:, 2 * REL_CLIP:], (N_HEADS, n_far))
    mid = rel_bias[:, 1:2 * REL_CLIP][:, ::-1]
    near = jnp.broadcast_to(rel_bias[:, 0:1], (N_HEADS, KB - n_far - (2 * REL_CLIP - 1)))
    wrap = jnp.broadcast_to(rel_bias[:, 2 * REL_CLIP:], (N_HEADS, 4 * QB - KB))
    by_offset = jnp.concatenate([far, mid, near, wrap], axis=1)
    width = 4 * QB
    toeplitz = jnp.tile(by_offset, (1, QB))[:, :QB * (width - 1)].reshape(N_HEADS, QB, width - 1)[:, :, :KB]
    return jnp.where(jnp.asarray(_band_mask())[None], toeplitz, NEG)


def rel_bias_grad(by_offset):
    n_far = 2 * QB - REL_CLIP + 1
    hi = jnp.sum(by_offset[:, :n_far], axis=1, keepdims=True) + jnp.sum(by_offset[:, KB:], axis=1, keepdims=True)
    mid = by_offset[:, n_far:n_far + 2 * REL_CLIP - 1][:, ::-1]
    lo = jnp.sum(by_offset[:, n_far + 2 * REL_CLIP - 1:KB], axis=1, keepdims=True)
    return jnp.concatenate([lo, mid, hi], axis=1)


def _head_masks():
    first = lax.broadcasted_iota(jnp.int32, (1, 2 * HEAD_DIM), 1) < HEAD_DIM
    return first, jnp.logical_not(first)


def _pick_lane(tile, h):
    lane = lax.broadcasted_iota(jnp.int32, (1, tile.shape[1]), 1)
    return jnp.sum(jnp.where(lane == h, tile, 0.0), axis=-1, keepdims=True)


def attention_fwd(name, qkv, bias, gg, y_ab):
    s = qkv.shape[0]
    nq = s // QB
    scale = HEAD_DIM ** -0.5

    def body(q_ref, k0, k1, k2, v0, v1, v2, b_ref, gg_ref, y_in, y_ref, o_ref, lse_ref):
        del y_in
        i = pl.program_id(0)
        kb = jnp.concatenate([k0[...], k1[...], k2[...]], axis=0)
        vb = jnp.concatenate([v0[...], v1[...], v2[...]], axis=0)
        valid = lax.broadcasted_iota(jnp.int32, (1, KB), 1) >= (2 - i) * QB
        lane = lax.broadcasted_iota(jnp.int32, (1, 128), 1)
        masks = _head_masks()
        lse = jnp.zeros((QB, 128), F32)
        outs = []
        for hp in range(N_HEADS // 2):
            sl = slice(2 * HEAD_DIM * hp, 2 * HEAD_DIM * (hp + 1))
            q_p, k_p, v_p = q_ref[:, sl], kb[:, sl], vb[:, sl]
            o_pair = jnp.zeros((QB, 2 * HEAD_DIM), F32)
            for a in range(2):
                h = 2 * hp + a
                sc = _dot(jnp.where(masks[a], q_p, 0), k_p, "nt") * scale + b_ref[h]
                sc = jnp.where(valid, sc, NEG)
                mx = jnp.max(sc, axis=-1, keepdims=True)
                e = jnp.exp(sc - mx)
                l = jnp.sum(e, axis=-1, keepdims=True)
                o_pair = o_pair + _dot(e.astype(BF16), jnp.where(masks[a], v_p, 0)) * (1.0 / l)
                lse = jnp.where(lane == h, mx + jnp.log(l), lse)
            outs.append(o_pair)
        o = jnp.concatenate(outs, axis=1)
        o_ref[...] = o.astype(o_ref.dtype)
        lse_ref[...] = lse
        y_ref[...] = ((o * _rms(o)) * gg_ref[...]).astype(y_ref.dtype)

    blk = lambda col, back: pl.BlockSpec((QB, D_ATTN), lambda i: (jnp.maximum(i - back, 0), QKV_COL + col))
    return _call(
        body, name=name, grid=(nq,),
        in_specs=[blk(0, 0), blk(1, 2), blk(1, 1), blk(1, 0), blk(2, 2), blk(2, 1), blk(2, 0),
                  pl.BlockSpec((N_HEADS, QB, KB), lambda i: (0, 0, 0)), pl.BlockSpec((1, D_ATTN), lambda i: (0, 0)),
                  pl.BlockSpec(memory_space=pl.ANY)],
        out_specs=[pl.BlockSpec((QB, D_ATTN), lambda i: (i, 1)), pl.BlockSpec((QB, D_ATTN), lambda i: (i, 0)),
                   pl.BlockSpec((QB, 128), lambda i: (i, 0))],
        out_shape=[jax.ShapeDtypeStruct((s, D_MODEL), BF16), jax.ShapeDtypeStruct((s, D_ATTN), BF16),
                   jax.ShapeDtypeStruct((s, 128), F32)],
        input_output_aliases={9: 0},
        compiler_params=_params(("arbitrary",), 56),
    )(qkv, qkv, qkv, qkv, qkv, qkv, qkv, bias, gg, y_ab)


def attention_bwd(name, qkv, o, lse, dy, bias, gg):
    s = qkv.shape[0]
    nq = s // QB
    scale = HEAD_DIM ** -0.5
    width = 4 * QB

    def body(q_ref, k0, k1, k2, v0, v1, v2, o_ref, lse_ref, dy_ref, b_ref, gg_ref,
             dq_ref, dk_ref, dv_ref, off_ref, dgg_ref, dk_acc, dv_acc, db_acc):
        i = pl.program_id(0)

        @pl.when(i == 0)
        def _():
            dk_acc[...] = jnp.zeros_like(dk_acc)
            dv_acc[...] = jnp.zeros_like(dv_acc)
            db_acc[...] = jnp.zeros_like(db_acc)
            dgg_ref[...] = jnp.zeros_like(dgg_ref)

        @pl.when(i > 0)
        def _():
            for acc in (dk_acc, dv_acc):
                kept = acc[QB:, :]
                acc[0:2 * QB, :] = kept
                acc[2 * QB:, :] = jnp.zeros((QB, D_ATTN), F32)

        @pl.when(i < nq)
        def _():
            ov = o_ref[...].astype(F32)
            dyv = dy_ref[...].astype(F32)
            do, dgg_t = _norm_bwd(ov, _rms(ov), gg_ref[...], dyv)
            dgg_ref[...] += jnp.sum(dgg_t, axis=0, keepdims=True)
            kb = jnp.concatenate([k0[...], k1[...], k2[...]], axis=0)
            vb = jnp.concatenate([v0[...], v1[...], v2[...]], axis=0)
            valid = lax.broadcasted_iota(jnp.int32, (1, KB), 1) >= (2 - i) * QB
            masks = _head_masks()
            lse_t = lse_ref[...]
            for hp in range(N_HEADS // 2):
                sl = slice(2 * HEAD_DIM * hp, 2 * HEAD_DIM * (hp + 1))
                q_p, k_p, v_p = q_ref[:, sl], kb[:, sl], vb[:, sl]
                do_p = do[:, sl]
                prod = do_p * ov[:, sl]
                do_b = do_p.astype(BF16)
                dq_pair = jnp.zeros((QB, 2 * HEAD_DIM), F32)
                dk_pair = jnp.zeros((KB, 2 * HEAD_DIM), F32)
                dv_pair = jnp.zeros((KB, 2 * HEAD_DIM), F32)
                for a in range(2):
                    h = 2 * hp + a
                    q_m = jnp.where(masks[a], q_p, 0)
                    do_m = jnp.where(masks[a], do_b, 0)
                    sc = _dot(q_m, k_p, "nt") * scale + b_ref[h]
                    sc = jnp.where(valid, sc, NEG)
                    p = jnp.exp(sc - _pick_lane(lse_t, h))
                    dp = _dot(do_m, v_p, "nt")
                    delta = jnp.sum(jnp.where(masks[a], prod, 0.0), axis=-1, keepdims=True)
                    ds = p * (dp - delta)
                    db_acc[h] += ds
                    ds_b = (ds * scale).astype(BF16)
                    dq_pair = dq_pair + _dot(ds_b, jnp.where(masks[a], k_p, 0))
                    dk_pair = dk_pair + _dot(ds_b, q_m, "tn")
                    dv_pair = dv_pair + _dot(p.astype(BF16), do_m, "tn")
                dq_ref[:, sl] = dq_pair.astype(dq_ref.dtype)
                dk_acc[:, sl] += dk_pair
                dv_acc[:, sl] += dv_pair

        dk_ref[...] = dk_acc[0:QB, :].astype(dk_ref.dtype)
        dv_ref[...] = dv_acc[0:QB, :].astype(dv_ref.dtype)

        @pl.when(i == nq + 1)
        def _():
            sub = lax.broadcasted_iota(jnp.int32, (8, 1), 0)
            pad = jnp.zeros((8, width - KB), F32)
            for h in range(N_HEADS):
                v = jnp.concatenate([db_acc[h, 0:8, :], pad], axis=1)
                for a in range(1, QB // 8):
                    grp = jnp.concatenate([db_acc[h, 8 * a:8 * a + 8, :], pad], axis=1)
                    v = v + pltpu.roll(grp, width - 8 * a, 1)
                for bit in range(3):
                    v = jnp.where(((sub >> bit) & 1) == 1, pltpu.roll(v, width - (1 << bit), 1), v)
                off_ref[h:h + 1, :] = jnp.sum(v, axis=0, keepdims=True)

    qi = lambda i: jnp.minimum(i, nq - 1)
    kblk = lambda col, back: pl.BlockSpec((QB, D_ATTN), lambda i: (jnp.clip(i - back, 0, nq - 1), QKV_COL + col))
    qblk = lambda col: pl.BlockSpec((QB, D_ATTN), lambda i: (qi(i), col))
    done = pl.BlockSpec((QB, D_ATTN), lambda i: (jnp.clip(i - 2, 0, nq - 1), 0))
    return _call(
        body, name=name, grid=(nq + 2,),
        in_specs=[qblk(QKV_COL), kblk(1, 2), kblk(1, 1), kblk(1, 0), kblk(2, 2), kblk(2, 1), kblk(2, 0),
                  qblk(0), pl.BlockSpec((QB, 128), lambda i: (qi(i), 0)), qblk(1),
                  pl.BlockSpec((N_HEADS, QB, KB), lambda i: (0, 0, 0)), pl.BlockSpec((1, D_ATTN), lambda i: (0, 0))],
        out_specs=[qblk(0), done, done, pl.BlockSpec((N_HEADS, width), lambda i: (0, 0)),
                   pl.BlockSpec((1, D_ATTN), lambda i: (0, 0))],
        out_shape=[jax.ShapeDtypeStruct((s, D_ATTN), BF16)] * 3
        + [jax.ShapeDtypeStruct((N_HEADS, width), F32), jax.ShapeDtypeStruct((1, D_ATTN), F32)],
        scratch_shapes=[pltpu.VMEM((KB, D_ATTN), F32), pltpu.VMEM((KB, D_ATTN), F32), pltpu.VMEM((N_HEADS, QB, KB), F32)],
        compiler_params=_params(("arbitrary",), 56),
    )(qkv, qkv, qkv, qkv, qkv, qkv, qkv, o, lse, dy, bias, gg)


def _block_diag(pw):
    out = jnp.zeros((D_POOL, D_POOL), pw.dtype)
    for gi in range(N_POOL):
        out = lax.dynamic_update_slice(out, pw[gi], (gi * POOL_GC, gi * POOL_GC))
    return out


_SMALL = ("pool_w", "pool_scale", "rel_bias", "group_gain", "pre_mix_g", "post_mix_g", "pre_ffn_g", "post_ffn_g")


def _pack(parts, rows):
    flat = jnp.concatenate([p.reshape(-1).astype(F32) for p in parts])
    return jnp.pad(flat, (0, rows * D_MODEL - flat.shape[0])).reshape(rows, D_MODEL)


def _unpack(packed, shapes):
    flat = packed.reshape(-1)
    out, at = [], 0
    for shp in shapes:
        size = int(np.prod(shp))
        out.append(flat[at:at + size].reshape(shp))
        at += size
    return out


def kernel(x, w_in, w_out, conv_w, pool_w, pool_scale, rel_bias, group_gain, pre_mix_g, post_mix_g, pre_ffn_g, post_ffn_g, w_gate_up, w_down, loss_target, m_w_in, m_w_out, m_conv_w, m_pool_w, m_pool_scale, m_rel_bias, m_group_gain, m_pre_mix_g, m_post_mix_g, m_pre_ffn_g, m_post_ffn_g, m_w_gate_up, m_w_down, v_w_in, v_w_out, v_conv_w, v_pool_w, v_pool_scale, v_rel_bias, v_group_gain, v_pre_mix_g, v_post_mix_g, v_pre_ffn_g, v_post_ffn_g, v_w_gate_up, v_w_down):
    depth = w_in.shape[0]
    s = x.shape[1]
    c_me = lax.axis_index("c")
    q_me = 2 * lax.axis_index("x") + lax.axis_index("y")
    dev = 2 * q_me + c_me

    tr = lambda a: jnp.swapaxes(a, 1, 2)
    w_in_t, w_gu_t = tr(w_in), tr(w_gate_up)

    def gather_layer(l, deps):
        first = gather_start(f"gather_a{l}_s", [w_in_t[l].astype(BF16), w_out[l].astype(BF16), conv_w[l]], deps)
        return first, gather_start(f"gather_b{l}_s", [w_gu_t[l].astype(BF16), w_down[l].astype(BF16)], [first["token"]])

    def weights_a(relayed, l, after):
        g_in, g_out, g_cw = gather_finish(f"gather_a{l}_f", relayed, after)
        return (assemble_rows("assemble_in", *g_in), assemble_rows("assemble_out", *g_out),
                assemble_cols("assemble_conv", *g_cw))

    def weights_b(relayed, l, after):
        g_gu, g_dn = gather_finish(f"gather_b{l}_f", relayed, after)
        return assemble_rows("assemble_gu", *g_gu), assemble_rows("assemble_down", *g_dn)

    h = x.reshape(s, D_MODEL)
    started = {0: gather_layer(0, [])}
    xn = rmsnorm_fwd("norm_mix", h, pre_mix_g[0].reshape(1, -1), BF16)
    relay_a = gather_relay("gather_a0_r", started[0][0], started[0][1]["token"])
    wa = weights_a(relay_a, 0, xn)
    saved = []
    for l in range(depth):
        vec = lambda p: p[l].reshape(1, -1)
        wt_in, wf_out, cw_full = wa
        ahead = [started[l][1]["token"]]
        if l + 1 < depth:
            started[l + 1] = gather_layer(l + 1, ahead)
            ahead = [started[l + 1][1]["token"]]
        proj = matmul("proj", xn, wt_in, "nt", BF16, s, 5 * D_ATTN, D_MODEL, 1024, 1280, D_MODEL, deps=ahead)
        relay_b = gather_relay(f"gather_b{l}_r", started[l][1], proj)
        bd = _block_diag(pool_w[l]).astype(BF16)
        bias = attention_bias(rel_bias[l])
        gg = vec(group_gain)
        y_ab = conv_pool_fwd("conv_pool_fwd", proj, cw_full, bd, vec(pool_scale), gg[:, :2 * D_CONV],
                             deps=[relay_b["token"]])
        y, o, lse = attention_fwd("attention_fwd", proj, bias, gg[:, 2 * D_CONV:], y_ab)
        wt_gu, wf_dn = weights_b(relay_b, l, y)
        if l + 1 < depth:
            relay_a = gather_relay(f"gather_a{l + 1}_r", started[l + 1][0], y)
        mix, h_mid, hn = matmul_norm("mix_out", y, wf_out, h, vec(post_mix_g), then=(vec(pre_ffn_g), BF16),
                                     deps=[relay_a["token"]] if l + 1 < depth else [])
        g, u, ff = matmul_swiglu("gate_up", hn, wt_gu)
        if l + 1 < depth:
            wa = weights_a(relay_a, l + 1, ff)
            ffo, h_out, xn_next = matmul_norm("ffn_down", ff, wf_dn, h_mid, vec(post_ffn_g),
                                              then=(pre_mix_g[l + 1].reshape(1, -1), BF16))
        else:
            ffo, h_out = matmul_norm("ffn_down_last", ff, wf_dn, h_mid, vec(post_ffn_g))
            xn_next = None
        saved.append((h, xn, proj, None, bd, bias, y, o, lse, mix, h_mid, hn, g, u, ff, ffo,
                      wt_in, wf_out, cw_full, wt_gu, wf_dn))
        h, xn = h_out, xn_next

    dh, loss_part = loss_head("loss_head", h, loss_target.reshape(s, D_MODEL))

    def reduce_begin(name, halves):
        return [k for k, _ in halves], exchange_start(name, plan_swap_cores, len(halves), [snd for _, snd in halves],
                                                      [(k.shape, k.dtype) for k, _ in halves])

    def reduce_relay(name, kept, swapped, after):
        _, got = exchange_wait(name + "_w", swapped, after)
        sums = [add_pairs("reduce_add", k, r) for k, r in zip(kept, got)]
        return exchange_start(name + "_s", plan_to_chips, 3 * len(sums), sums, [(a.shape, a.dtype) for a in sums])

    def reduce_finish(name, relayed, after):
        sums, got = exchange_wait(name, relayed, after)
        return list(zip(sums, got))

    small = {k: [None] * depth for k in _SMALL + ("conv_w",)}
    relayed = {}
    dffo, small["post_ffn_g"][depth - 1] = rmsnorm_bwd("norm_bwd_top", saved[-1][15], post_ffn_g[depth - 1].reshape(1, -1),
                                                       dh, None, BF16)
    for l in reversed(range(depth)):
        vec = lambda p: p[l].reshape(1, -1)
        (h_in, xn, proj, _, bd, bias, y, o, lse, mix, h_mid, hn, g, u, ff, ffo,
         wt_in, wf_out, cw_full, wt_gu, wf_dn) = saved[l]
        gg = vec(group_gain)
        gw_dn = matmul("wgrad_down", ff, dffo, "tn", BF16, D_FF, D_MODEL, s, 1408, D_MODEL, 1024)
        dg, du = matmul_swiglu_bwd("dgrad_down", dffo, wf_dn, g, u)
        gw_gu = matmul("wgrad_gu", [dg, du], hn, "tn", BF16, 2 * D_FF, D_MODEL, s, 1408, D_MODEL, 1024)
        kept_b, swap_b = reduce_begin(f"reduce_b{l}_s", [split_rows("split_gu", gw_gu), split_rows("split_down", gw_dn)])
        dhn = matmul("dgrad_gu", [dg, du], wt_gu, "nn", F32, s, D_MODEL, 2 * D_FF, 1024, 1024, 1408,
                     deps=[swap_b["token"]])
        relayed[l, "b"] = reduce_relay(f"reduce_b{l}_r", kept_b, swap_b, dhn)
        dh_mid, small["pre_ffn_g"][l], dmix, small["post_mix_g"][l] = rmsnorm_bwd(
            "norm_bwd_mid", h_mid, vec(pre_ffn_g), dhn, dh, F32, then=(mix, vec(post_mix_g), BF16),
            deps=[relayed[l, "b"]["token"]])
        gw_out = matmul("wgrad_out", y, dmix, "tn", BF16, D_MODEL, D_MODEL, s, D_MODEL, D_MODEL, 1024)
        dy = matmul("dgrad_out", dmix, wf_out, "nt", BF16, s, D_MODEL, D_MODEL, 1024, 512, D_MODEL)
        dpa, dcw, dbd, dps, dgg_ab = conv_pool_bwd("conv_pool_bwd", proj, dy, cw_full, bd, vec(pool_scale),
                                                   gg[:, :2 * D_CONV])
        dq, dk, dv, by_off, dgg_c = attention_bwd("attention_bwd", proj, o, lse, dy, bias, gg[:, 2 * D_CONV:])
        dparts = [dpa, dq, dk, dv]
        gw_in = matmul("wgrad_in", dparts, xn, "tn", BF16, 5 * D_ATTN, D_MODEL, s, 512, D_MODEL, min(s, 2048))
        kept_a, swap_a = reduce_begin(f"reduce_a{l}_s", [split_rows("split_in", gw_in), split_rows("split_out", gw_out)])
        dxn = matmul("dgrad_in", dparts, wt_in, "nn", F32, s, D_MODEL, 5 * D_ATTN, 1024, 1024, 512,
                     deps=[swap_a["token"]])
        relayed[l, "a"] = reduce_relay(f"reduce_a{l}_r", kept_a, swap_a, dxn)
        if l > 0:
            dh, small["pre_mix_g"][l], dffo, small["post_ffn_g"][l - 1] = rmsnorm_bwd(
                "norm_bwd_in", h_in, vec(pre_mix_g), dxn, dh_mid, F32,
                then=(saved[l - 1][15], post_ffn_g[l - 1].reshape(1, -1), BF16), deps=[relayed[l, "a"]["token"]])
        else:
            dh, small["pre_mix_g"][l] = rmsnorm_bwd("norm_bwd_first", h_in, vec(pre_mix_g), dxn, dh_mid, F32,
                                                    deps=[relayed[l, "a"]["token"]])
        small["conv_w"][l] = dcw
        small["pool_w"][l] = jnp.stack([dbd[gi * POOL_GC:(gi + 1) * POOL_GC, gi * POOL_GC:(gi + 1) * POOL_GC]
                                        for gi in range(N_POOL)])
        small["pool_scale"][l] = dps
        small["rel_bias"][l] = rel_bias_grad(by_off)
        small["group_gain"][l] = jnp.concatenate([dgg_ab, dgg_c], axis=1)
    grad_x = dh.reshape(x.shape)

    small_params = dict(pool_w=pool_w, pool_scale=pool_scale, rel_bias=rel_bias, group_gain=group_gain,
                        pre_mix_g=pre_mix_g, post_mix_g=post_mix_g, pre_ffn_g=pre_ffn_g, post_ffn_g=post_ffn_g)
    small_m = dict(pool_w=m_pool_w, pool_scale=m_pool_scale, rel_bias=m_rel_bias, group_gain=m_group_gain,
                   pre_mix_g=m_pre_mix_g, post_mix_g=m_post_mix_g, pre_ffn_g=m_pre_ffn_g, post_ffn_g=m_post_ffn_g)
    small_v = dict(pool_w=v_pool_w, pool_scale=v_pool_scale, rel_bias=v_rel_bias, group_gain=v_group_gain,
                   pre_mix_g=v_pre_mix_g, post_mix_g=v_post_mix_g, pre_ffn_g=v_pre_ffn_g, post_ffn_g=v_post_ffn_g)
    shapes = [small_params[k].shape for k in _SMALL] + [(depth, 3, D_CONV), (1,)]
    n_small = sum(int(np.prod(shp)) for shp in shapes)
    rows = -(-n_small // (8 * D_MODEL)) * 8
    extra = [jnp.zeros((depth, 3, D_CONV), F32), jnp.zeros((1,), F32)]
    grads_packed = _pack([jnp.stack(small[k]) for k in _SMALL] + [jnp.stack(small["conv_w"]), loss_part[0, 0:1]], rows)
    small_started = gather_start("gather_small_s", [grads_packed])

    big = dict(w_in=(w_in_t, tr(m_w_in), tr(v_w_in)), w_out=(w_out, m_w_out, v_w_out),
               w_gate_up=(w_gu_t, tr(m_w_gate_up), tr(v_w_gate_up)), w_down=(w_down, m_w_down, v_w_down))
    results = dict(w_in=None, w_out=None, w_gate_up=None, w_down=None)
    after = small_started["token"]
    small_relay = None
    for l in reversed(range(depth)):
        slabs_gu, slabs_dn = reduce_finish(f"reduce_b{l}_f", relayed[l, "b"], after)
        slabs_in, slabs_out = reduce_finish(f"reduce_a{l}_f", relayed[l, "a"], slabs_dn[1])
        for k, slabs in (("w_gate_up", slabs_gu), ("w_down", slabs_dn), ("w_in", slabs_in), ("w_out", slabs_out)):
            results[k] = adamw_layer("adamw_" + k, l, *big[k], *slabs, results[k])
        after = results["w_out"][0]
        if small_relay is None:
            small_relay = gather_relay("gather_small_r", small_started, after)
    small_parts = gather_finish("gather_small_f", small_relay, after)[0]
    all_small = assemble_rows("assemble_small", *small_parts).reshape(8, rows, D_MODEL)
    res_small = adamw("adamw_small", _pack([small_params[k] for k in _SMALL] + extra, rows),
                      _pack([small_m[k] for k in _SMALL] + extra, rows),
                      _pack([small_v[k] for k in _SMALL] + extra, rows), all_small)
    g_s, d_s, m_s, v_s = (_unpack(r, shapes) for r in res_small)
    loss = g_s[-1][0]
    g_conv = lax.dynamic_slice_in_dim(g_s[-2], dev * conv_w.shape[2], conv_w.shape[2], axis=2)
    results["conv_w"] = adamw("adamw_conv", conv_w, m_conv_w, v_conv_w, g_conv[None])

    names = ("w_in", "w_out", "conv_w") + _SMALL + ("w_gate_up", "w_down")
    for k in ("w_in", "w_gate_up"):
        results[k] = tuple(tr(r) for r in results[k])
    for j, k in enumerate(_SMALL):
        results[k] = (g_s[j], d_s[j], m_s[j], v_s[j])
    return (loss, grad_x, *[results[k][0] for k in names], *[results[k][1] for k in names],
            *[results[k][2] for k in names], *[results[k][3] for k in names])
```

```python
import numpy as np
import jax
import jax.numpy as jnp
from jax import lax
from jax.experimental import pallas as pl
from jax.experimental.pallas import tpu as pltpu

F32 = jnp.float32
BF16 = jnp.bfloat16

CHUNK = 64
D_MODEL = 1024
D_CONV = 256
D_POOL = 256
D_ATTN = 512
HEAD_DIM = 64
N_HEADS = 8
N_POOL = 4
POOL_GC = 64
POOL_WINDOWS = (2, 4, 8, 16)
LEFT_CHUNKS = 8
REL_CLIP = 128
D_FF = 2816
EPS = 1e-6
ADAM_LR, ADAM_B1, ADAM_B2, ADAM_EPS, ADAM_WD, ADAM_STEP = 0.001, 0.9, 0.999, 1e-08, 0.01, 10

QB = 256
KB = 3 * QB
HALO = 16
T_CP = 512
T_ROW = 512
QKV_COL = 2
NEG = -1e30
VMEM_MB = 1 << 20
MESH = pl.DeviceIdType.MESH


def _call(body, **kw):
    call = pl.pallas_call(body, **kw)
    return lambda *args: call(*[_in_hbm(a) for a in args])


def _in_hbm(a):
    return pltpu.with_memory_space_constraint(a, pltpu.HBM) if jnp.issubdtype(a.dtype, jnp.number) else a


def _call_after(deps, n_in, body, **kw):
    deps = tuple(deps)
    if not deps:
        return _call(body, **kw)

    def ordered(*refs):
        body(*refs[:n_in], *refs[n_in + len(deps):])

    kw["in_specs"] = list(kw["in_specs"]) + [pl.BlockSpec(memory_space=pl.ANY)] * len(deps)
    call = _call(ordered, **kw)
    return lambda *args: call(*args, *deps)


def _params(sem, vmem_mb=48):
    return pltpu.CompilerParams(dimension_semantics=sem, vmem_limit_bytes=vmem_mb * VMEM_MB)


_CHIP_FLIPS = ((1, 0), (0, 1), (1, 1))
_HBM = pl.BlockSpec(memory_space=pltpu.HBM)
_SEM = pl.BlockSpec(memory_space=pltpu.SEMAPHORE)
_EFFECT = pltpu.SideEffectType.DATAFLOW_SIDE_EFFECTING


def _flip(v, f):
    return 1 - v if f else v


def _descriptors(plan, srcs, lands, send_sems, recv_sems, sending):
    x, y, c = lax.axis_index("x"), lax.axis_index("y"), lax.axis_index("c")
    return [pltpu.make_async_remote_copy(src_ref=src, dst_ref=dst if sending else land, send_sem=send_sems.at[k],
                                         recv_sem=recv_sems.at[k], device_id=peer, device_id_type=MESH)
            for k, (src, dst, peer, land) in enumerate(plan(srcs, lands, x, y, c))]


def exchange_start(name, plan, n_copies, srcs, land_shapes, deps=()):
    ns, nl = len(srcs), len(land_shapes)

    def body(*refs):
        src_refs, land_refs = refs[:ns], refs[ns:ns + nl]
        send_sems, recv_sems = refs[ns + nl], refs[ns + nl + 1]
        for send in _descriptors(plan, src_refs, land_refs, send_sems, recv_sems, True):
            send.start()
        refs[-1][...] = jnp.zeros_like(refs[-1])

    lands = [lax.empty(shape, dtype) for shape, dtype in land_shapes]
    outs = _call_after(
        deps, ns + nl, body, name=name,
        out_shape=(pltpu.SemaphoreType.DMA((n_copies,)), pltpu.SemaphoreType.DMA((n_copies,)),
                   *[pltpu.HBM(a.shape, a.dtype) for a in srcs], *[pltpu.HBM(shape, dtype) for shape, dtype in land_shapes],
                   jax.ShapeDtypeStruct((8, 128), F32)),
        in_specs=[_HBM] * (ns + nl),
        out_specs=(_SEM, _SEM, *[_HBM] * (ns + nl), pl.BlockSpec(memory_space=pltpu.VMEM)),
        input_output_aliases={j: 2 + j for j in range(ns + nl)},
        compiler_params=pltpu.CompilerParams(has_side_effects=_EFFECT),
    )(*srcs, *lands)
    return dict(plan=plan, sems=outs[:2], srcs=outs[2:2 + ns], lands=outs[2 + ns:2 + ns + nl], token=outs[-1])


def exchange_wait(name, started, after):
    srcs, lands = started["srcs"], started["lands"]
    ns, nl = len(srcs), len(lands)

    def body(*refs):
        src_refs, land_refs = refs[:ns], refs[ns:ns + nl]
        send_sems, recv_sems = refs[ns + nl], refs[ns + nl + 1]
        for wait in _descriptors(started["plan"], src_refs, land_refs, send_sems, recv_sems, False):
            wait.wait_send()
            wait.wait_recv()

    outs = _call(
        body, name=name,
        out_shape=tuple(pltpu.HBM(a.shape, a.dtype) for a in (*srcs, *lands)),
        in_specs=[_HBM] * (ns + nl) + [_SEM, _SEM, pl.BlockSpec(memory_space=pl.ANY)],
        out_specs=tuple([_HBM] * (ns + nl)),
        input_output_aliases={j: j for j in range(ns + nl)},
        compiler_params=pltpu.CompilerParams(has_side_effects=_EFFECT),
    )(*srcs, *lands, *started["sems"], after)
    return list(outs[:ns]), list(outs[ns:])


def plan_from_chips(srcs, lands, x, y, c):
    q = 2 * x + y
    out = []
    for src, land in zip(srcs, lands):
        for fx, fy in _CHIP_FLIPS:
            px, py = _flip(x, fx), _flip(y, fy)
            out.append((src, land.at[q], (px, py, c), land.at[2 * px + py]))
    return out


def plan_to_core(srcs, lands, x, y, c):
    n = len(lands)
    q = 2 * x + y
    out = []
    for own, chips, land in zip(srcs[:n], srcs[n:], lands):
        out.append((own, land.at[q], (x, y, 1 - c), land.at[q]))
        for fx, fy in _CHIP_FLIPS:
            qp = 2 * _flip(x, fx) + _flip(y, fy)
            out.append((chips.at[qp], land.at[qp], (x, y, 1 - c), land.at[qp]))
    return out


def plan_swap_cores(srcs, lands, x, y, c):
    return [(src, land, (x, y, 1 - c), land) for src, land in zip(srcs, lands)]


def plan_to_chips(srcs, lands, x, y, c):
    q = 2 * x + y
    out = []
    for src, land in zip(srcs, lands):
        for fx, fy in _CHIP_FLIPS:
            px, py = _flip(x, fx), _flip(y, fy)
            out.append((src.at[2 * px + py], land.at[q], (px, py, c), land.at[2 * px + py]))
    return out


def _slots(a):
    return ((4,) + a.shape, a.dtype)


def gather_start(name, arrays, deps=()):
    return exchange_start(name, plan_from_chips, 3 * len(arrays), arrays, [_slots(a) for a in arrays], deps)


def gather_relay(name, started, after):
    own, chips = exchange_wait(name + "_w", started, after)
    return exchange_start(name + "_s", plan_to_core, 4 * len(own), own + chips, [_slots(a) for a in own])


def gather_finish(name, relayed, after):
    srcs, cores = exchange_wait(name, relayed, after)
    n = len(cores)
    return list(zip(srcs[:n], srcs[n:], cores))


def _place():
    return 2 * lax.axis_index("x") + lax.axis_index("y"), lax.axis_index("c")


def assemble_cols(name, own, chips, core):
    r, c = own.shape
    t = _row_tile(r, 256)

    def body(own_ref, chips_ref, core_ref, o_ref):
        q_me, c_me = _place()
        for q in range(4):
            mine = jnp.where(q == q_me, own_ref[...], chips_ref[q])
            other = core_ref[q]
            o_ref[:, 2 * q * c:(2 * q + 1) * c] = jnp.where(c_me == 0, mine, other)
            o_ref[:, (2 * q + 1) * c:(2 * q + 2) * c] = jnp.where(c_me == 0, other, mine)

    slots = pl.BlockSpec((4, t, c), lambda i: (0, i, 0))
    return _call(
        body, name=name, grid=(r // t,),
        in_specs=[pl.BlockSpec((t, c), lambda i: (i, 0)), slots, slots],
        out_specs=pl.BlockSpec((t, 8 * c), lambda i: (i, 0)),
        out_shape=jax.ShapeDtypeStruct((r, 8 * c), own.dtype), compiler_params=_params(("parallel",)),
    )(own, chips, core)


def assemble_rows(name, own, chips, core):
    r, c = own.shape

    def body(own_ref, chips_ref, core_ref, o_ref):
        q_me, c_me = _place()
        d = pl.program_id(0)
        mine = jnp.where(d // 2 == q_me, own_ref[...], chips_ref[...])
        o_ref[...] = jnp.where(d % 2 == c_me, mine, core_ref[...])

    slot = pl.BlockSpec((None, r, c), lambda d: (d // 2, 0, 0))
    return _call(
        body, name=name, grid=(8,),
        in_specs=[pl.BlockSpec((r, c), lambda d: (0, 0)), slot, slot],
        out_specs=pl.BlockSpec((r, c), lambda d: (d, 0)),
        out_shape=jax.ShapeDtypeStruct((8 * r, c), own.dtype), compiler_params=_params(("parallel",)),
    )(own, chips, core)


def split_rows(name, dw):
    r8, c = dw.shape
    r = r8 // 8

    def body(dw_ref, keep_ref, send_ref):
        _, c_me = _place()
        d = pl.program_id(0)

        @pl.when(d % 2 == c_me)
        def _():
            keep_ref[...] = dw_ref[...]

        @pl.when(d % 2 != c_me)
        def _():
            send_ref[...] = dw_ref[...]

    slot = pl.BlockSpec((None, r, c), lambda d: (d // 2, 0, 0))
    return _call(
        body, name=name, grid=(8,),
        in_specs=[pl.BlockSpec((r, c), lambda d: (d, 0))], out_specs=[slot, slot],
        out_shape=[jax.ShapeDtypeStruct((4, r, c), dw.dtype)] * 2, compiler_params=_params(("arbitrary",)),
    )(dw)


def _rms(v):
    return lax.rsqrt(jnp.mean(v * v, axis=-1, keepdims=True) + EPS)


def rmsnorm_fwd(name, x, g, out_dtype):
    s, d = x.shape

    def body(x_ref, g_ref, o_ref):
        xv = x_ref[...]
        o_ref[...] = ((xv * _rms(xv)) * g_ref[...]).astype(o_ref.dtype)

    row = pl.BlockSpec((T_ROW, d), lambda i: (i, 0))
    return _call(
        body, name=name, grid=(s // T_ROW,),
        in_specs=[row, pl.BlockSpec((1, d), lambda i: (0, 0))], out_specs=row,
        out_shape=jax.ShapeDtypeStruct((s, d), out_dtype), compiler_params=_params(("parallel",)),
    )(x, g)


def matmul_norm(name, a, b, res, g, then=None, deps=()):
    s, k = a.shape
    d = b.shape[1]
    tm = 512

    def body(a_ref, b_ref, r_ref, g_ref, *rest):
        z = _dot(a_ref[...], b_ref[...])
        z_ref, y_ref = rest[-3:-1] if then else rest
        z_ref[...] = z.astype(z_ref.dtype)
        y = r_ref[...] + (z * _rms(z)) * g_ref[...]
        y_ref[...] = y
        if then:
            rest[-1][...] = ((y * _rms(y)) * rest[0][...]).astype(rest[-1].dtype)

    row = pl.BlockSpec((tm, d), lambda i: (i, 0))
    vec = pl.BlockSpec((1, d), lambda i: (0, 0))
    args = (a, b, res, g) + ((then[0],) if then else ())
    shapes = [jax.ShapeDtypeStruct((s, d), BF16), jax.ShapeDtypeStruct((s, d), F32)]
    return _call_after(
        deps, len(args), body, name=name, grid=(s // tm,),
        in_specs=[pl.BlockSpec((tm, k), lambda i: (i, 0)), pl.BlockSpec((k, d), lambda i: (0, 0)), row, vec]
        + ([vec] if then else []),
        out_specs=[row] * (3 if then else 2),
        out_shape=shapes + ([jax.ShapeDtypeStruct((s, d), then[1])] if then else []),
        compiler_params=_params(("parallel",)),
    )(*args)


def matmul_norm_bwd(name, a_parts, b, tk, x, g, res, then=None, deps=()):
    a_parts = list(a_parts)
    na = len(a_parts)
    s, d = x.shape
    tm = 512
    spans, at = [], 0
    for p in a_parts:
        spans.append((at, p.shape[1] // tk))
        at += p.shape[1] // tk
    gk = at
    assert gk * tk == b.shape[0] and all(p.shape[1] % tk == 0 for p in a_parts)
    n_in = na + 4 + (2 if then else 0)

    def body(*refs):
        a_refs, b_ref = refs[:na], refs[na]
        x_ref, g_ref, r_ref = refs[na + 1:na + 4]
        outs, acc_ref = refs[n_in:-1], refs[-1]
        i, kk = pl.program_id(0), pl.program_id(1)

        @pl.when(kk == 0)
        def _():
            acc_ref[...] = jnp.zeros_like(acc_ref)

        @pl.when((kk == 0) & (i == 0))
        def _():
            for dg_ref in outs[1::2]:
                dg_ref[...] = jnp.zeros_like(dg_ref)

        for pa, (lo, nblk) in enumerate(spans):
            def add(pa=pa):
                acc_ref[...] += _dot(a_refs[pa][...], b_ref[...])
            pl.when((kk >= lo) & (kk < lo + nblk))(add) if na > 1 else add()

        @pl.when(kk == gk - 1)
        def _():
            xv = x_ref[...].astype(F32)
            dx, dgt = _norm_bwd(xv, _rms(xv), g_ref[...], acc_ref[...])
            dx = dx + r_ref[...]
            outs[0][...] = dx
            outs[1][...] += jnp.sum(dgt, axis=0, keepdims=True)
            if then is not None:
                x2 = refs[n_in - 2][...].astype(F32)
                dx2, dgt2 = _norm_bwd(x2, _rms(x2), refs[n_in - 1][...], dx)
                outs[2][...] = dx2.astype(outs[2].dtype)
                outs[3][...] += jnp.sum(dgt2, axis=0, keepdims=True)

    row = pl.BlockSpec((tm, d), lambda i, kk: (i, 0))
    vec = pl.BlockSpec((1, d), lambda i, kk: (0, 0))
    a_specs = [pl.BlockSpec((tm, tk), lambda i, kk, lo=lo, nblk=nblk: (i, jnp.clip(kk - lo, 0, nblk - 1)))
               for lo, nblk in spans]
    shapes = [jax.ShapeDtypeStruct((s, d), F32), jax.ShapeDtypeStruct((1, d), F32)]
    if then:
        shapes += [jax.ShapeDtypeStruct((s, d), then[2]), jax.ShapeDtypeStruct((1, d), F32)]
    args = (*a_parts, b, x, g, res) + ((then[0], then[1]) if then else ())
    return _call_after(
        deps, len(args), body, name=name, grid=(s // tm, gk),
        in_specs=a_specs + [pl.BlockSpec((tk, d), lambda i, kk: (kk, 0)), row, vec, row] + ([row, vec] if then else []),
        out_specs=[row, vec] * (2 if then else 1), out_shape=shapes,
        scratch_shapes=[pltpu.VMEM((tm, d), F32)],
        compiler_params=_params(("arbitrary", "arbitrary")),
    )(*args)


def _norm_bwd(xv, r, g, dy):
    a = dy * g
    dx = r * (a - xv * ((r * r) * jnp.mean(a * xv, axis=-1, keepdims=True)))
    return dx, dy * (xv * r)


def rmsnorm_bwd(name, x, g, dy, res, out_dtype, then=None, deps=()):
    s, d = x.shape
    has_res = res is not None
    n_in = 3 + has_res + (2 if then else 0)

    def body(*refs):
        x_ref, g_ref, dy_ref = refs[:3]
        outs = refs[n_in:]
        xv = x_ref[...].astype(F32)
        dx, dgt = _norm_bwd(xv, _rms(xv), g_ref[...], dy_ref[...].astype(F32))
        if has_res:
            dx = dx + refs[3][...]
        outs[0][...] = dx.astype(outs[0].dtype)
        sums = [(outs[1], dgt)]
        if then is not None:
            x2 = refs[n_in - 2][...].astype(F32)
            dx2, dgt2 = _norm_bwd(x2, _rms(x2), refs[n_in - 1][...], dx)
            outs[2][...] = dx2.astype(outs[2].dtype)
            sums.append((outs[3], dgt2))

        @pl.when(pl.program_id(0) == 0)
        def _():
            for dg_ref, _ in sums:
                dg_ref[...] = jnp.zeros_like(dg_ref)
        for dg_ref, terms in sums:
            dg_ref[...] += jnp.sum(terms, axis=0, keepdims=True)

    row = pl.BlockSpec((T_ROW, d), lambda i: (i, 0))
    vec = pl.BlockSpec((1, d), lambda i: (0, 0))
    args = (x, g, dy) + ((res,) if has_res else ()) + ((then[0], then[1]) if then else ())
    shapes = [jax.ShapeDtypeStruct((s, d), out_dtype), jax.ShapeDtypeStruct((1, d), F32)]
    if then:
        shapes += [jax.ShapeDtypeStruct((s, d), then[2]), jax.ShapeDtypeStruct((1, d), F32)]
    return _call_after(
        deps, len(args), body, name=name, grid=(s // T_ROW,),
        in_specs=[row, vec, row] + ([row] if has_res else []) + ([row, vec] if then else []),
        out_specs=[row, vec] * (2 if then else 1), out_shape=shapes,
        compiler_params=_params(("arbitrary",)),
    )(*args)


def loss_head(name, h, tgt):
    s, d = h.shape

    def body(h_ref, t_ref, dh_ref, l_ref):
        e = h_ref[...] - t_ref[...]
        dh_ref[...] = e * (1.0 / d)

        @pl.when(pl.program_id(0) == 0)
        def _():
            l_ref[...] = jnp.zeros_like(l_ref)
        part = 0.5 * jnp.sum(jnp.mean(e * e, axis=-1, keepdims=True), axis=0, keepdims=True)
        l_ref[...] += jnp.broadcast_to(part, l_ref.shape)

    row = pl.BlockSpec((T_ROW, d), lambda i: (i, 0))
    return _call(
        body, name=name, grid=(s // T_ROW,),
        in_specs=[row, row], out_specs=[row, pl.BlockSpec((1, 128), lambda i: (0, 0))],
        out_shape=[jax.ShapeDtypeStruct((s, d), F32), jax.ShapeDtypeStruct((1, 128), F32)],
        compiler_params=_params(("arbitrary",)),
    )(h, tgt)


def _row_tile(rows, limit=512):
    t = min(rows, limit)
    while rows % t or (t % 8 and t != rows):
        t -= 1
    return t


def add_pairs(name, a, b):
    shape = a.shape
    a2, b2 = a.reshape(-1, shape[-1]), b.reshape(-1, shape[-1])
    rows, cols = a2.shape
    t = _row_tile(rows)

    def body(a_ref, b_ref, o_ref):
        o_ref[...] = (a_ref[...].astype(F32) + b_ref[...].astype(F32)).astype(o_ref.dtype)

    blk = pl.BlockSpec((t, cols), lambda i: (i, 0))
    out = _call(
        body, name=name, grid=(rows // t,), in_specs=[blk, blk], out_specs=blk,
        out_shape=jax.ShapeDtypeStruct((rows, cols), a.dtype), compiler_params=_params(("parallel",)),
    )(a2, b2)
    return out.reshape(shape)


def _adamw_math(w, m, v, g):
    c1 = 1.0 - ADAM_B1 ** ADAM_STEP
    c2 = 1.0 - ADAM_B2 ** ADAM_STEP
    mn = ADAM_B1 * m + (1.0 - ADAM_B1) * g
    vn = ADAM_B2 * v + (1.0 - ADAM_B2) * (g * g)
    delta = -ADAM_LR * ((mn / c1) / (jnp.sqrt(vn / c2) + ADAM_EPS) + ADAM_WD * w)
    return delta, mn, vn


def _slab_sum(g_ref, n):
    g = g_ref[0].astype(F32)
    for j in range(1, n):
        g = g + g_ref[j].astype(F32)
    return g


def adamw(name, w, m, v, g_slabs, deps=()):
    shape = w.shape
    n = g_slabs.shape[0]
    w2, m2, v2 = (t.reshape(-1, shape[-1]) for t in (w, m, v))
    g3 = g_slabs.reshape(n, -1, shape[-1])
    rows, cols = w2.shape
    t = _row_tile(rows, 256)

    def body(w_ref, m_ref, v_ref, g_ref, go_ref, d_ref, mo_ref, vo_ref):
        g = _slab_sum(g_ref, n)
        go_ref[...] = g
        d_ref[...], mo_ref[...], vo_ref[...] = _adamw_math(w_ref[...], m_ref[...], v_ref[...], g)

    blk = pl.BlockSpec((t, cols), lambda i: (i, 0))
    outs = _call_after(
        deps, 4, body, name=name, grid=(rows // t,),
        in_specs=[blk, blk, blk, pl.BlockSpec((n, t, cols), lambda i: (0, i, 0))], out_specs=[blk] * 4,
        out_shape=[jax.ShapeDtypeStruct((rows, cols), F32)] * 4, compiler_params=_params(("parallel",)),
    )(w2, m2, v2, g3)
    return tuple(o.reshape(shape) for o in outs)


def adamw_layer(name, l, w, m, v, own_sums, chip_sums, into):
    _, rows, cols = w.shape
    t = _row_tile(rows, 256)
    if into is None:
        into = tuple(lax.empty(w.shape, F32) for _ in range(4))

    def body(w_ref, m_ref, v_ref, own_ref, far_ref, a0, a1, a2, a3, go_ref, d_ref, mo_ref, vo_ref):
        del a0, a1, a2, a3
        q_me, _ = _place()
        g = jnp.zeros((t, cols), F32)
        for q in range(4):
            g = g + jnp.where(q == q_me, own_ref[q], far_ref[q]).astype(F32)
        go_ref[...] = g
        d_ref[...], mo_ref[...], vo_ref[...] = _adamw_math(w_ref[...], m_ref[...], v_ref[...], g)

    blk = pl.BlockSpec((None, t, cols), lambda i: (l, i, 0))
    slabs = pl.BlockSpec((4, t, cols), lambda i: (0, i, 0))
    anyw = pl.BlockSpec(memory_space=pl.ANY)
    return _call(
        body, name=name, grid=(rows // t,),
        in_specs=[blk, blk, blk, slabs, slabs] + [anyw] * 4, out_specs=[blk] * 4,
        out_shape=[jax.ShapeDtypeStruct(w.shape, F32)] * 4, input_output_aliases={5: 0, 6: 1, 7: 2, 8: 3},
        compiler_params=_params(("parallel",)),
    )(w, m, v, own_sums, chip_sums, *into)


_DIMS = {"nn": (((1,), (0,)), ((), ())), "nt": (((1,), (1,)), ((), ())), "tn": (((0,), (0,)), ((), ()))}


def _dot(a, b, mode="nn"):
    return lax.dot_general(a, b, _DIMS[mode], preferred_element_type=F32)


def matmul(name, a, b, mode, out_dtype, m, n, k, tm, tn, tk, b_off=(0, 0), deps=()):
    gm, gn, gk = m // tm, n // tn, k // tk
    assert gm * tm == m and gn * tn == n and gk * tk == k
    r0, c0 = b_off
    a_parts = list(a) if isinstance(a, (list, tuple)) else [a]
    na = len(a_parts)
    tile = tm if mode == "tn" else tk
    spans, at = [], 0
    for p in a_parts:
        nblk = p.shape[1] // tile
        assert nblk * tile == p.shape[1]
        spans.append((at, nblk))
        at += nblk
    assert at == (gm if mode == "tn" else gk)

    def within(t, span):
        return (t >= span[0]) & (t < span[0] + span[1])

    def local(t, span):
        return jnp.clip(t - span[0], 0, span[1] - 1)

    a_specs = []
    for sp in spans:
        if mode == "tn":
            a_specs.append(pl.BlockSpec((tk, tm), lambda i, j, kk, sp=sp: (jnp.where(within(i, sp), kk, 0), local(i, sp))))
        else:
            a_specs.append(pl.BlockSpec((tm, tk), lambda i, j, kk, sp=sp: (i, local(kk, sp))))
    if mode == "nt":
        b_spec = pl.BlockSpec((tn, tk), lambda i, j, kk: (j + r0, kk + c0))
    else:
        b_spec = pl.BlockSpec((tk, tn), lambda i, j, kk: (kk + r0, j + c0))
    o_spec = pl.BlockSpec((tm, tn), lambda i, j, kk: (i, j))
    single = na == 1 and gk == 1

    def body(*refs):
        a_refs, b_ref, o_ref = refs[:na], refs[na], refs[na + 1]
        if single:
            o_ref[...] = _dot(a_refs[0][...], b_ref[...], mode).astype(o_ref.dtype)
            return
        acc_ref = refs[-1]
        i, kk = pl.program_id(0), pl.program_id(2)

        @pl.when(kk == 0)
        def _():
            acc_ref[...] = jnp.zeros_like(acc_ref)

        for pa, sp in enumerate(spans):
            def add(pa=pa):
                acc_ref[...] += _dot(a_refs[pa][...], b_ref[...], mode)
            if na > 1:
                pl.when(within(i if mode == "tn" else kk, sp))(add)
            else:
                add()

        @pl.when(kk == gk - 1)
        def _():
            o_ref[...] = acc_ref[...].astype(o_ref.dtype)

    return _call_after(
        deps, na + 1, body, name=name, grid=(gm, gn, gk),
        in_specs=a_specs + [b_spec], out_specs=o_spec,
        out_shape=jax.ShapeDtypeStruct((m, n), out_dtype),
        scratch_shapes=[] if single else [pltpu.VMEM((tm, tn), F32)],
        compiler_params=_params(("parallel", "parallel", "arbitrary")),
    )(*a_parts, b)


def matmul_swiglu(name, a, w_gu_t, deps=()):
    s, d = a.shape
    tm, tn = 512, 1408
    gn = D_FF // tn

    def body(a_ref, wg_ref, wu_ref, g_ref, u_ref, f_ref):
        av = a_ref[...]
        g = _dot(av, wg_ref[...], "nt")
        u = _dot(av, wu_ref[...], "nt")
        g_ref[...] = g.astype(g_ref.dtype)
        u_ref[...] = u.astype(u_ref.dtype)
        f_ref[...] = ((g * (1.0 / (1.0 + jnp.exp(-g)))) * u).astype(f_ref.dtype)

    o_spec = pl.BlockSpec((tm, tn), lambda j, i: (i, j))
    return _call_after(
        deps, 3, body, name=name, grid=(gn, s // tm),
        in_specs=[pl.BlockSpec((tm, d), lambda j, i: (i, 0)), pl.BlockSpec((tn, d), lambda j, i: (j, 0)),
                  pl.BlockSpec((tn, d), lambda j, i: (j + gn, 0))],
        out_specs=[o_spec] * 3, out_shape=[jax.ShapeDtypeStruct((s, D_FF), BF16)] * 3,
        compiler_params=_params(("parallel", "parallel")),
    )(a, w_gu_t, w_gu_t)


def matmul_swiglu_bwd(name, dffo, w_down, g, u):
    s, d = dffo.shape
    tm, tn = 512, 1408

    def body(a_ref, w_ref, g_ref, u_ref, dg_ref, du_ref):
        av = a_ref[...]
        for lo in range(0, tn, 256):
            cols = slice(lo, min(lo + 256, tn))
            dff = _dot(av, w_ref[cols, :], "nt")
            gv = g_ref[:, cols].astype(F32)
            sig = 1.0 / (1.0 + jnp.exp(-gv))
            silu = gv * sig
            du_ref[:, cols] = (dff * silu).astype(du_ref.dtype)
            dg_ref[:, cols] = ((dff * u_ref[:, cols].astype(F32)) * (sig + silu * (1.0 - sig))).astype(dg_ref.dtype)

    o_spec = pl.BlockSpec((tm, tn), lambda j, i: (i, j))
    return _call(
        body, name=name, grid=(D_FF // tn, s // tm),
        in_specs=[pl.BlockSpec((tm, d), lambda j, i: (i, 0)), pl.BlockSpec((tn, d), lambda j, i: (j, 0)), o_spec, o_spec],
        out_specs=[o_spec] * 2, out_shape=[jax.ShapeDtypeStruct((s, D_FF), BF16)] * 2,
        compiler_params=_params(("parallel", "parallel")),
    )(dffo, w_down, g, u)


def _down(v, n):
    return pltpu.roll(v, n, 0)


def _up(v, n):
    return pltpu.roll(v, v.shape[0] - n, 0)


def _by_window(lane, v2, v4, v8, v16):
    return jnp.where(lane < POOL_GC, v2, jnp.where(lane < 2 * POOL_GC, v4, jnp.where(lane < 3 * POOL_GC, v8, v16)))


def _taps(cw_ref):
    return cw_ref[0:1, :], cw_ref[1:2, :], cw_ref[2:3, :]


def _conv_pool_forward(ext, t0, cw, bd):
    n_out = ext.shape[0] - HALO
    gb = ext[HALO:, 0:D_CONV]
    z = ext[:, D_CONV:2 * D_CONV] * ext[:, 2 * D_CONV:3 * D_CONV]
    z0, z1, z2 = z[HALO:], _down(z, 1)[HALO:], _down(z, 2)[HALO:]
    conv = cw[2] * z0 + cw[1] * z1 + cw[0] * z2
    x = ext[:, 3 * D_CONV:]
    w2 = x + _down(x, 1)
    w4 = w2 + _down(w2, 2)
    w8 = w4 + _down(w4, 4)
    w16 = w8 + _down(w8, 8)
    lane = lax.broadcasted_iota(jnp.int32, (1, D_POOL), 1)
    win = _by_window(lane, 2.0, 4.0, 8.0, 16.0)
    pos = (t0 + lax.broadcasted_iota(jnp.int32, (n_out, 1), 0) + 1).astype(F32)
    cnt = jnp.minimum(pos, win)
    d = _by_window(lane, w2, w4, w8, w16)[HALO:] / cnt - x[HALO:]
    ybp = _dot(d.astype(BF16), bd)
    return gb, z0, z1, z2, conv, d, cnt, ybp, lane


def conv_pool_fwd(name, proj_a, cw, bd, ps, gg, deps=()):
    s = proj_a.shape[0]
    t = T_CP
    hb = t // HALO

    def body(main_ref, prev_ref, cw_ref, bd_ref, ps_ref, gg_ref, y_ref):
        i = pl.program_id(0)
        prev = jnp.where(i > 0, prev_ref[...].astype(F32), 0.0)
        ext = jnp.concatenate([prev, main_ref[...].astype(F32)], axis=0)
        gb, _, _, _, conv, _, _, ybp, _ = _conv_pool_forward(ext, i * t, _taps(cw_ref), bd_ref[...])
        ya = gb * conv
        yb = ybp * ps_ref[...]
        ggv = gg_ref[...]
        y_ref[:, 0:D_CONV] = ((ya * _rms(ya)) * ggv[:, 0:D_CONV]).astype(y_ref.dtype)
        y_ref[:, D_CONV:] = ((yb * _rms(yb)) * ggv[:, D_CONV:]).astype(y_ref.dtype)

    full = lambda shape: pl.BlockSpec(shape, lambda i: (0,) * len(shape))
    return _call_after(
        deps, 6, body, name=name, grid=(s // t,),
        in_specs=[pl.BlockSpec((t, D_MODEL), lambda i: (i, 0)),
                  pl.BlockSpec((HALO, D_MODEL), lambda i: (jnp.maximum(i * hb - 1, 0), 0)),
                  full((3, D_CONV)), full((D_POOL, D_POOL)), full((1, D_POOL)), full((1, 2 * D_CONV))],
        out_specs=pl.BlockSpec((t, 2 * D_CONV), lambda i: (i, 0)),
        out_shape=jax.ShapeDtypeStruct((s, D_MODEL), BF16),
        compiler_params=_params(("parallel",)),
    )(proj_a, proj_a, cw, bd, ps, gg)


def conv_pool_bwd(name, proj_a, dy, cw, bd, ps, gg):
    s = proj_a.shape[0]
    t = T_CP
    hb = t // HALO
    nblk = s // t
    last_halo = s // HALO - 1

    def body(main_ref, prev_ref, next_ref, dy_ref, dyn_ref, cw_ref, bd_ref, ps_ref, gg_ref,
             dp_ref, dcw_ref, dbd_ref, dps_ref, dgg_ref):
        i = pl.program_id(0)
        prev = jnp.where(i > 0, prev_ref[...].astype(F32), 0.0)
        main = main_ref[...].astype(F32)
        ext = jnp.concatenate([prev, main, next_ref[...].astype(F32)], axis=0)
        cwv, bdv, psv, ggv = _taps(cw_ref), bd_ref[...], ps_ref[...], gg_ref[...]
        gb, z0, z1, z2, conv, d, cnt, ybp, lane = _conv_pool_forward(ext, i * t, cwv, bdv)
        dyn = jnp.where(i < nblk - 1, dyn_ref[...].astype(F32), 0.0)
        dyv = jnp.concatenate([dy_ref[...].astype(F32), dyn], axis=0)
        ya = gb * conv
        yb = ybp * psv
        dya, dgg_a = _norm_bwd(ya, _rms(ya), ggv[:, 0:D_CONV], dyv[:, 0:D_CONV])
        dyb, dgg_b = _norm_bwd(yb, _rms(yb), ggv[:, D_CONV:], dyv[:, D_CONV:])

        dconv = dya * gb
        dz = (cwv[2] * dconv + cwv[1] * _up(dconv, 1) + cwv[0] * _up(dconv, 2))[:t]
        dp_ref[:, 0:D_CONV] = (dya * conv)[:t].astype(dp_ref.dtype)
        dp_ref[:, D_CONV:2 * D_CONV] = (dz * main[:, 2 * D_CONV:3 * D_CONV]).astype(dp_ref.dtype)
        dp_ref[:, 2 * D_CONV:3 * D_CONV] = (dz * main[:, D_CONV:2 * D_CONV]).astype(dp_ref.dtype)

        dybs = dyb * psv
        dd = _dot(dybs.astype(BF16), bdv, "nt")
        e = dd / cnt
        a2 = e + _up(e, 1)
        a4 = a2 + _up(a2, 2)
        a8 = a4 + _up(a4, 4)
        a16 = a8 + _up(a8, 8)
        dp_ref[:, 3 * D_CONV:] = (_by_window(lane, a2, a4, a8, a16) - dd)[:t].astype(dp_ref.dtype)

        @pl.when(i == 0)
        def _():
            dcw_ref[...] = jnp.zeros_like(dcw_ref)
            dbd_ref[...] = jnp.zeros_like(dbd_ref)
            dps_ref[...] = jnp.zeros_like(dps_ref)
            dgg_ref[...] = jnp.zeros_like(dgg_ref)

        rsum = lambda v: jnp.sum(v[:t], axis=0, keepdims=True)
        dcw_ref[0:1, :] += rsum(dconv * z2)
        dcw_ref[1:2, :] += rsum(dconv * z1)
        dcw_ref[2:3, :] += rsum(dconv * z0)
        dbd_ref[...] += _dot(d[:t].astype(BF16), dybs[:t].astype(BF16), "tn")
        dps_ref[...] += rsum(dyb * ybp)
        dgg_ref[:, 0:D_CONV] += rsum(dgg_a)
        dgg_ref[:, D_CONV:] += rsum(dgg_b)

    full = lambda shape: pl.BlockSpec(shape, lambda i: (0,) * len(shape))
    next_halo = lambda i: (jnp.minimum((i + 1) * hb, last_halo), 0)
    return _call(
        body, name=name, grid=(nblk,),
        in_specs=[pl.BlockSpec((t, D_MODEL), lambda i: (i, 0)),
                  pl.BlockSpec((HALO, D_MODEL), lambda i: (jnp.maximum(i * hb - 1, 0), 0)),
                  pl.BlockSpec((HALO, D_MODEL), next_halo),
                  pl.BlockSpec((t, 2 * D_CONV), lambda i: (i, 0)),
                  pl.BlockSpec((HALO, 2 * D_CONV), next_halo),
                  full((3, D_CONV)), full((D_POOL, D_POOL)), full((1, D_POOL)), full((1, 2 * D_CONV))],
        out_specs=[pl.BlockSpec((t, D_MODEL), lambda i: (i, 0)),
                   full((3, D_CONV)), full((D_POOL, D_POOL)), full((1, D_POOL)), full((1, 2 * D_CONV))],
        out_shape=[jax.ShapeDtypeStruct((s, D_MODEL), BF16), jax.ShapeDtypeStruct((3, D_CONV), F32),
                   jax.ShapeDtypeStruct((D_POOL, D_POOL), F32), jax.ShapeDtypeStruct((1, D_POOL), F32),
                   jax.ShapeDtypeStruct((1, 2 * D_CONV), F32)],
        compiler_params=_params(("arbitrary",)),
    )(proj_a, proj_a, proj_a, dy, dy, cw, bd, ps, gg)


def _band_mask():
    qa = np.arange(QB)[:, None] // CHUNK
    km = np.arange(KB)[None, :] // CHUNK
    return (km - qa >= 0) & (km - qa <= LEFT_CHUNKS)


def attention_bias(rel_bias):
    n_far = 2 * QB - REL_CLIP + 1
    far = jnp.broadcast_to(rel_bias[:, 2 * REL_CLIP:], (N_HEADS, n_far))
    mid = rel_bias[:, 1:2 * REL_CLIP][:, ::-1]
    near = jnp.broadcast_to(rel_bias[:, 0:1], (N_HEADS, KB - n_far - (2 * REL_CLIP - 1)))
    wrap = jnp.broadcast_to(rel_bias[:, 2 * REL_CLIP:], (N_HEADS, 4 * QB - KB))
    by_offset = jnp.concatenate([far, mid, near, wrap], axis=1)
    width = 4 * QB
    toeplitz = jnp.tile(by_offset, (1, QB))[:, :QB * (width - 1)].reshape(N_HEADS, QB, width - 1)[:, :, :KB]
    return jnp.where(jnp.asarray(_band_mask())[None], toeplitz, NEG)


def rel_bias_grad(by_offset):
    n_far = 2 * QB - REL_CLIP + 1
    hi = jnp.sum(by_offset[:, :n_far], axis=1, keepdims=True) + jnp.sum(by_offset[:, KB:], axis=1, keepdims=True)
    mid = by_offset[:, n_far:n_far + 2 * REL_CLIP - 1][:, ::-1]
    lo = jnp.sum(by_offset[:, n_far + 2 * REL_CLIP - 1:KB], axis=1, keepdims=True)
    return jnp.concatenate([lo, mid, hi], axis=1)


def _head_masks():
    first = lax.broadcasted_iota(jnp.int32, (1, 2 * HEAD_DIM), 1) < HEAD_DIM
    return first, jnp.logical_not(first)


def _pick_lane(tile, h):
    lane = lax.broadcasted_iota(jnp.int32, (1, tile.shape[1]), 1)
    return jnp.sum(jnp.where(lane == h, tile, 0.0), axis=-1, keepdims=True)


def attention_fwd(name, qkv, bias, gg, y_ab):
    s = qkv.shape[0]
    nq = s // QB
    scale = HEAD_DIM ** -0.5

    def body(q_ref, k0, k1, k2, v0, v1, v2, b_ref, gg_ref, y_in, y_ref, o_ref, lse_ref):
        del y_in
        i = pl.program_id(0)
        kb = jnp.concatenate([k0[...], k1[...], k2[...]], axis=0)
        vb = jnp.concatenate([v0[...], v1[...], v2[...]], axis=0)
        valid = lax.broadcasted_iota(jnp.int32, (1, KB), 1) >= (2 - i) * QB
        lane = lax.broadcasted_iota(jnp.int32, (1, 128), 1)
        masks = _head_masks()
        lse = jnp.zeros((QB, 128), F32)
        outs = []
        for hp in range(N_HEADS // 2):
            sl = slice(2 * HEAD_DIM * hp, 2 * HEAD_DIM * (hp + 1))
            q_p, k_p, v_p = q_ref[:, sl] * scale, kb[:, sl], vb[:, sl]
            o_pair = jnp.zeros((QB, 2 * HEAD_DIM), F32)
            for a in range(2):
                h = 2 * hp + a
                sc = _dot(jnp.where(masks[a], q_p, 0), k_p, "nt") + b_ref[h]
                sc = jnp.where(valid, sc, NEG)
                mx = jnp.max(sc, axis=-1, keepdims=True)
                e = jnp.exp(sc - mx)
                l = jnp.sum(e, axis=-1, keepdims=True)
                o_pair = o_pair + _dot(e.astype(BF16), jnp.where(masks[a], v_p, 0)) * (1.0 / l)
                lse = jnp.where(lane == h, mx + jnp.log(l), lse)
            outs.append(o_pair)
        o = jnp.concatenate(outs, axis=1)
        o_ref[...] = o.astype(o_ref.dtype)
        lse_ref[...] = lse
        y_ref[...] = ((o * _rms(o)) * gg_ref[...]).astype(y_ref.dtype)

    blk = lambda col, back: pl.BlockSpec((QB, D_ATTN), lambda i: (jnp.maximum(i - back, 0), QKV_COL + col))
    return _call(
        body, name=name, grid=(nq,),
        in_specs=[blk(0, 0), blk(1, 2), blk(1, 1), blk(1, 0), blk(2, 2), blk(2, 1), blk(2, 0),
                  pl.BlockSpec((N_HEADS, QB, KB), lambda i: (0, 0, 0)), pl.BlockSpec((1, D_ATTN), lambda i: (0, 0)),
                  pl.BlockSpec(memory_space=pl.ANY)],
        out_specs=[pl.BlockSpec((QB, D_ATTN), lambda i: (i, 1)), pl.BlockSpec((QB, D_ATTN), lambda i: (i, 0)),
                   pl.BlockSpec((QB, 128), lambda i: (i, 0))],
        out_shape=[jax.ShapeDtypeStruct((s, D_MODEL), BF16), jax.ShapeDtypeStruct((s, D_ATTN), BF16),
                   jax.ShapeDtypeStruct((s, 128), F32)],
        input_output_aliases={9: 0},
        compiler_params=_params(("arbitrary",), 56),
    )(qkv, qkv, qkv, qkv, qkv, qkv, qkv, bias, gg, y_ab)


def attention_bwd(name, qkv, o, lse, dy, bias, gg):
    s = qkv.shape[0]
    nq = s // QB
    scale = HEAD_DIM ** -0.5
    width = 4 * QB

    def body(q_ref, k0, k1, k2, v0, v1, v2, o_ref, lse_ref, dy_ref, b_ref, gg_ref,
             dq_ref, dk_ref, dv_ref, off_ref, dgg_ref, dk_acc, dv_acc, db_acc):
        i = pl.program_id(0)

        @pl.when(i == 0)
        def _():
            dk_acc[...] = jnp.zeros_like(dk_acc)
            dv_acc[...] = jnp.zeros_like(dv_acc)
            db_acc[...] = jnp.zeros_like(db_acc)
            dgg_ref[...] = jnp.zeros_like(dgg_ref)

        @pl.when(i > 0)
        def _():
            for acc in (dk_acc, dv_acc):
                kept = acc[QB:, :]
                acc[0:2 * QB, :] = kept
                acc[2 * QB:, :] = jnp.zeros((QB, D_ATTN), F32)

        @pl.when(i < nq)
        def _():
            ov = o_ref[...].astype(F32)
            dyv = dy_ref[...].astype(F32)
            do, dgg_t = _norm_bwd(ov, _rms(ov), gg_ref[...], dyv)
            dgg_ref[...] += jnp.sum(dgg_t, axis=0, keepdims=True)
            kb = jnp.concatenate([k0[...], k1[...], k2[...]], axis=0)
            vb = jnp.concatenate([v0[...], v1[...], v2[...]], axis=0)
            valid = lax.broadcasted_iota(jnp.int32, (1, KB), 1) >= (2 - i) * QB
            masks = _head_masks()
            lse_t = lse_ref[...]
            for hp in range(N_HEADS // 2):
                sl = slice(2 * HEAD_DIM * hp, 2 * HEAD_DIM * (hp + 1))
                q_p, k_p, v_p = q_ref[:, sl] * scale, kb[:, sl], vb[:, sl]
                do_p = do[:, sl]
                prod = do_p * ov[:, sl]
                do_b = do_p.astype(BF16)
                dq_pair = jnp.zeros((QB, 2 * HEAD_DIM), F32)
                dk_pair = jnp.zeros((KB, 2 * HEAD_DIM), F32)
                dv_pair = jnp.zeros((KB, 2 * HEAD_DIM), F32)
                for a in range(2):
                    h = 2 * hp + a
                    q_m = jnp.where(masks[a], q_p, 0)
                    do_m = jnp.where(masks[a], do_b, 0)
                    sc = _dot(q_m, k_p, "nt") + b_ref[h]
                    sc = jnp.where(valid, sc, NEG)
                    p = jnp.exp(sc - _pick_lane(lse_t, h))
                    dp = _dot(do_m, v_p, "nt")
                    delta = jnp.sum(jnp.where(masks[a], prod, 0.0), axis=-1, keepdims=True)
                    ds = p * (dp - delta)
                    db_acc[h] += ds
                    ds_b = ds.astype(BF16)
                    dq_pair = dq_pair + _dot(ds_b, jnp.where(masks[a], k_p, 0))
                    dk_pair = dk_pair + _dot(ds_b, q_m, "tn")
                    dv_pair = dv_pair + _dot(p.astype(BF16), do_m, "tn")
                dq_ref[:, sl] = (dq_pair * scale).astype(dq_ref.dtype)
                dk_acc[:, sl] += dk_pair
                dv_acc[:, sl] += dv_pair

        dk_ref[...] = dk_acc[0:QB, :].astype(dk_ref.dtype)
        dv_ref[...] = dv_acc[0:QB, :].astype(dv_ref.dtype)

        @pl.when(i == nq + 1)
        def _():
            sub = lax.broadcasted_iota(jnp.int32, (8, 1), 0)
            pad = jnp.zeros((8, width - KB), F32)
            for h in range(N_HEADS):
                v = jnp.concatenate([db_acc[h, 0:8, :], pad], axis=1)
                for a in range(1, QB // 8):
                    grp = jnp.concatenate([db_acc[h, 8 * a:8 * a + 8, :], pad], axis=1)
                    v = v + pltpu.roll(grp, width - 8 * a, 1)
                for bit in range(3):
                    v = jnp.where(((sub >> bit) & 1) == 1, pltpu.roll(v, width - (1 << bit), 1), v)
                off_ref[h:h + 1, :] = jnp.sum(v, axis=0, keepdims=True)

    qi = lambda i: jnp.minimum(i, nq - 1)
    kblk = lambda col, back: pl.BlockSpec((QB, D_ATTN), lambda i: (jnp.clip(i - back, 0, nq - 1), QKV_COL + col))
    qblk = lambda col: pl.BlockSpec((QB, D_ATTN), lambda i: (qi(i), col))
    done = pl.BlockSpec((QB, D_ATTN), lambda i: (jnp.clip(i - 2, 0, nq - 1), 0))
    return _call(
        body, name=name, grid=(nq + 2,),
        in_specs=[qblk(QKV_COL), kblk(1, 2), kblk(1, 1), kblk(1, 0), kblk(2, 2), kblk(2, 1), kblk(2, 0),
                  qblk(0), pl.BlockSpec((QB, 128), lambda i: (qi(i), 0)), qblk(1),
                  pl.BlockSpec((N_HEADS, QB, KB), lambda i: (0, 0, 0)), pl.BlockSpec((1, D_ATTN), lambda i: (0, 0))],
        out_specs=[qblk(0), done, done, pl.BlockSpec((N_HEADS, width), lambda i: (0, 0)),
                   pl.BlockSpec((1, D_ATTN), lambda i: (0, 0))],
        out_shape=[jax.ShapeDtypeStruct((s, D_ATTN), BF16)] * 3
        + [jax.ShapeDtypeStruct((N_HEADS, width), F32), jax.ShapeDtypeStruct((1, D_ATTN), F32)],
        scratch_shapes=[pltpu.VMEM((KB, D_ATTN), F32), pltpu.VMEM((KB, D_ATTN), F32), pltpu.VMEM((N_HEADS, QB, KB), F32)],
        compiler_params=_params(("arbitrary",), 56),
    )(qkv, qkv, qkv, qkv, qkv, qkv, qkv, o, lse, dy, bias, gg)


def _block_diag(pw):
    out = jnp.zeros((D_POOL, D_POOL), pw.dtype)
    for gi in range(N_POOL):
        out = lax.dynamic_update_slice(out, pw[gi], (gi * POOL_GC, gi * POOL_GC))
    return out


_SMALL = ("pool_w", "pool_scale", "rel_bias", "group_gain", "pre_mix_g", "post_mix_g", "pre_ffn_g", "post_ffn_g")


def _pack(parts, rows):
    flat = jnp.concatenate([p.reshape(-1).astype(F32) for p in parts])
    return jnp.pad(flat, (0, rows * D_MODEL - flat.shape[0])).reshape(rows, D_MODEL)


def _unpack(packed, shapes):
    flat = packed.reshape(-1)
    out, at = [], 0
    for shp in shapes:
        size = int(np.prod(shp))
        out.append(flat[at:at + size].reshape(shp))
        at += size
    return out


def kernel(x, w_in, w_out, conv_w, pool_w, pool_scale, rel_bias, group_gain, pre_mix_g, post_mix_g, pre_ffn_g, post_ffn_g, w_gate_up, w_down, loss_target, m_w_in, m_w_out, m_conv_w, m_pool_w, m_pool_scale, m_rel_bias, m_group_gain, m_pre_mix_g, m_post_mix_g, m_pre_ffn_g, m_post_ffn_g, m_w_gate_up, m_w_down, v_w_in, v_w_out, v_conv_w, v_pool_w, v_pool_scale, v_rel_bias, v_group_gain, v_pre_mix_g, v_post_mix_g, v_pre_ffn_g, v_post_ffn_g, v_w_gate_up, v_w_down):
    depth = w_in.shape[0]
    s = x.shape[1]
    c_me = lax.axis_index("c")
    q_me = 2 * lax.axis_index("x") + lax.axis_index("y")
    dev = 2 * q_me + c_me

    tr = lambda a: jnp.swapaxes(a, 1, 2)
    w_in_t, w_gu_t = tr(w_in), tr(w_gate_up)

    def gather_layer(l, deps):
        first = gather_start(f"gather_a{l}_s", [w_in_t[l].astype(BF16), w_out[l].astype(BF16), conv_w[l]], deps)
        return first, gather_start(f"gather_b{l}_s", [w_gu_t[l].astype(BF16), w_down[l].astype(BF16)], [first["token"]])

    def weights_a(relayed, l, after):
        g_in, g_out, g_cw = gather_finish(f"gather_a{l}_f", relayed, after)
        return (assemble_rows("assemble_in", *g_in), assemble_rows("assemble_out", *g_out),
                assemble_cols("assemble_conv", *g_cw))

    def weights_b(relayed, l, after):
        g_gu, g_dn = gather_finish(f"gather_b{l}_f", relayed, after)
        return assemble_rows("assemble_gu", *g_gu), assemble_rows("assemble_down", *g_dn)

    h = x.reshape(s, D_MODEL)
    started = {0: gather_layer(0, [])}
    xn = rmsnorm_fwd("norm_mix", h, pre_mix_g[0].reshape(1, -1), BF16)
    relay_a = gather_relay("gather_a0_r", started[0][0], started[0][1]["token"])
    wa = weights_a(relay_a, 0, xn)
    saved = []
    for l in range(depth):
        vec = lambda p: p[l].reshape(1, -1)
        wt_in, wf_out, cw_full = wa
        ahead = [started[l][1]["token"]]
        if l + 1 < depth:
            started[l + 1] = gather_layer(l + 1, ahead)
            ahead = [started[l + 1][1]["token"]]
        proj = matmul("proj", xn, wt_in, "nt", BF16, s, 5 * D_ATTN, D_MODEL, 1024, 1280, D_MODEL, deps=ahead)
        relay_b = gather_relay(f"gather_b{l}_r", started[l][1], proj)
        bd = _block_diag(pool_w[l]).astype(BF16)
        bias = attention_bias(rel_bias[l])
        gg = vec(group_gain)
        y_ab = conv_pool_fwd("conv_pool_fwd", proj, cw_full, bd, vec(pool_scale), gg[:, :2 * D_CONV],
                             deps=[relay_b["token"]])
        y, o, lse = attention_fwd("attention_fwd", proj, bias, gg[:, 2 * D_CONV:], y_ab)
        wt_gu, wf_dn = weights_b(relay_b, l, y)
        if l + 1 < depth:
            relay_a = gather_relay(f"gather_a{l + 1}_r", started[l + 1][0], y)
        mix, h_mid, hn = matmul_norm("mix_out", y, wf_out, h, vec(post_mix_g), then=(vec(pre_ffn_g), BF16),
                                     deps=[relay_a["token"]] if l + 1 < depth else [])
        g, u, ff = matmul_swiglu("gate_up", hn, wt_gu)
        if l + 1 < depth:
            wa = weights_a(relay_a, l + 1, ff)
            ffo, h_out, xn_next = matmul_norm("ffn_down", ff, wf_dn, h_mid, vec(post_ffn_g),
                                              then=(pre_mix_g[l + 1].reshape(1, -1), BF16))
        else:
            ffo, h_out = matmul_norm("ffn_down_last", ff, wf_dn, h_mid, vec(post_ffn_g))
            xn_next = None
        saved.append((h, xn, proj, None, bd, bias, y, o, lse, mix, h_mid, hn, g, u, ff, ffo,
                      wt_in, wf_out, cw_full, wt_gu, wf_dn))
        h, xn = h_out, xn_next

    dh, loss_part = loss_head("loss_head", h, loss_target.reshape(s, D_MODEL))

    def reduce_begin(name, halves):
        return [k for k, _ in halves], exchange_start(name, plan_swap_cores, len(halves), [snd for _, snd in halves],
                                                      [(k.shape, k.dtype) for k, _ in halves])

    def reduce_relay(name, kept, swapped, after):
        _, got = exchange_wait(name + "_w", swapped, after)
        sums = [add_pairs("reduce_add", k, r) for k, r in zip(kept, got)]
        return exchange_start(name + "_s", plan_to_chips, 3 * len(sums), sums, [(a.shape, a.dtype) for a in sums])

    def reduce_finish(name, relayed, after):
        sums, got = exchange_wait(name, relayed, after)
        return list(zip(sums, got))

    small = {k: [None] * depth for k in _SMALL + ("conv_w",)}
    relayed = {}
    carried = []
    dffo, small["post_ffn_g"][depth - 1] = rmsnorm_bwd("norm_bwd_top", saved[-1][15], post_ffn_g[depth - 1].reshape(1, -1),
                                                       dh, None, BF16)
    for l in reversed(range(depth)):
        vec = lambda p: p[l].reshape(1, -1)
        (h_in, xn, proj, _, bd, bias, y, o, lse, mix, h_mid, hn, g, u, ff, ffo,
         wt_in, wf_out, cw_full, wt_gu, wf_dn) = saved[l]
        gg = vec(group_gain)
        gw_dn = matmul("wgrad_down", ff, dffo, "tn", BF16, D_FF, D_MODEL, s, 1408, D_MODEL, 1024, deps=carried)
        dg, du = matmul_swiglu_bwd("dgrad_down", dffo, wf_dn, g, u)
        gw_gu = matmul("wgrad_gu", [dg, du], hn, "tn", BF16, 2 * D_FF, D_MODEL, s, 1408, D_MODEL, 1024)
        kept_b, swap_b = reduce_begin(f"reduce_b{l}_s", [split_rows("split_gu", gw_gu), split_rows("split_down", gw_dn)])
        dh_mid, small["pre_ffn_g"][l], dmix, small["post_mix_g"][l] = matmul_norm_bwd(
            "dgrad_gu", [dg, du], wt_gu, 1408, h_mid, vec(pre_ffn_g), dh, then=(mix, vec(post_mix_g), BF16),
            deps=[swap_b["token"]])
        relayed[l, "b"] = reduce_relay(f"reduce_b{l}_r", kept_b, swap_b, dmix)
        gw_out = matmul("wgrad_out", y, dmix, "tn", BF16, D_MODEL, D_MODEL, s, D_MODEL, D_MODEL, 1024,
                        deps=[relayed[l, "b"]["token"]])
        dy = matmul("dgrad_out", dmix, wf_out, "nt", BF16, s, D_MODEL, D_MODEL, 1024, 512, D_MODEL)
        dpa, dcw, dbd, dps, dgg_ab = conv_pool_bwd("conv_pool_bwd", proj, dy, cw_full, bd, vec(pool_scale),
                                                   gg[:, :2 * D_CONV])
        dq, dk, dv, by_off, dgg_c = attention_bwd("attention_bwd", proj, o, lse, dy, bias, gg[:, 2 * D_CONV:])
        dparts = [dpa, dq, dk, dv]
        gw_in = matmul("wgrad_in", dparts, xn, "tn", BF16, 5 * D_ATTN, D_MODEL, s, 512, D_MODEL, min(s, 2048))
        kept_a, swap_a = reduce_begin(f"reduce_a{l}_s", [split_rows("split_in", gw_in), split_rows("split_out", gw_out)])
        if l > 0:
            dh, small["pre_mix_g"][l], dffo, small["post_ffn_g"][l - 1] = matmul_norm_bwd(
                "dgrad_in", dparts, wt_in, 512, h_in, vec(pre_mix_g), dh_mid,
                then=(saved[l - 1][15], post_ffn_g[l - 1].reshape(1, -1), BF16), deps=[swap_a["token"]])
        else:
            dh, small["pre_mix_g"][l] = matmul_norm_bwd("dgrad_in_first", dparts, wt_in, 512, h_in, vec(pre_mix_g),
                                                        dh_mid, deps=[swap_a["token"]])
        relayed[l, "a"] = reduce_relay(f"reduce_a{l}_r", kept_a, swap_a, dh)
        carried = [relayed[l, "a"]["token"]]
        small["conv_w"][l] = dcw
        small["pool_w"][l] = jnp.stack([dbd[gi * POOL_GC:(gi + 1) * POOL_GC, gi * POOL_GC:(gi + 1) * POOL_GC]
                                        for gi in range(N_POOL)])
        small["pool_scale"][l] = dps
        small["rel_bias"][l] = rel_bias_grad(by_off)
        small["group_gain"][l] = jnp.concatenate([dgg_ab, dgg_c], axis=1)
    grad_x = dh.reshape(x.shape)

    small_params = dict(pool_w=pool_w, pool_scale=pool_scale, rel_bias=rel_bias, group_gain=group_gain,
                        pre_mix_g=pre_mix_g, post_mix_g=post_mix_g, pre_ffn_g=pre_ffn_g, post_ffn_g=post_ffn_g)
    small_m = dict(pool_w=m_pool_w, pool_scale=m_pool_scale, rel_bias=m_rel_bias, group_gain=m_group_gain,
                   pre_mix_g=m_pre_mix_g, post_mix_g=m_post_mix_g, pre_ffn_g=m_pre_ffn_g, post_ffn_g=m_post_ffn_g)
    small_v = dict(pool_w=v_pool_w, pool_scale=v_pool_scale, rel_bias=v_rel_bias, group_gain=v_group_gain,
                   pre_mix_g=v_pre_mix_g, post_mix_g=v_post_mix_g, pre_ffn_g=v_pre_ffn_g, post_ffn_g=v_post_ffn_g)
    shapes = [small_params[k].shape for k in _SMALL] + [(depth, 3, D_CONV), (1,)]
    n_small = sum(int(np.prod(shp)) for shp in shapes)
    rows = -(-n_small // (8 * D_MODEL)) * 8
    extra = [jnp.zeros((depth, 3, D_CONV), F32), jnp.zeros((1,), F32)]
    grads_packed = _pack([jnp.stack(small[k]) for k in _SMALL] + [jnp.stack(small["conv_w"]), loss_part[0, 0:1]], rows)
    small_started = gather_start("gather_small_s", [grads_packed], carried)

    big = dict(w_in=(w_in_t, tr(m_w_in), tr(v_w_in)), w_out=(w_out, m_w_out, v_w_out),
               w_gate_up=(w_gu_t, tr(m_w_gate_up), tr(v_w_gate_up)), w_down=(w_down, m_w_down, v_w_down))
    results = dict(w_in=None, w_out=None, w_gate_up=None, w_down=None)
    after = small_started["token"]
    small_relay = None
    for l in reversed(range(depth)):
        slabs_gu, slabs_dn = reduce_finish(f"reduce_b{l}_f", relayed[l, "b"], after)
        slabs_in, slabs_out = reduce_finish(f"reduce_a{l}_f", relayed[l, "a"], slabs_dn[1])
        for k, slabs in (("w_gate_up", slabs_gu), ("w_down", slabs_dn), ("w_in", slabs_in), ("w_out", slabs_out)):
            results[k] = adamw_layer("adamw_" + k, l, *big[k], *slabs, results[k])
        after = results["w_out"][0]
        if small_relay is None:
            small_relay = gather_relay("gather_small_r", small_started, after)
    small_parts = gather_finish("gather_small_f", small_relay, after)[0]
    all_small = assemble_rows("assemble_small", *small_parts).reshape(8, rows, D_MODEL)
    res_small = adamw("adamw_small", _pack([small_params[k] for k in _SMALL] + extra, rows),
                      _pack([small_m[k] for k in _SMALL] + extra, rows),
                      _pack([small_v[k] for k in _SMALL] + extra, rows), all_small)
    g_s, d_s, m_s, v_s = (_unpack(r, shapes) for r in res_small)
    loss = g_s[-1][0]
    g_conv = lax.dynamic_slice_in_dim(g_s[-2], dev * conv_w.shape[2], conv_w.shape[2], axis=2)
    results["conv_w"] = adamw("adamw_conv", conv_w, m_conv_w, v_conv_w, g_conv[None])

    names = ("w_in", "w_out", "conv_w") + _SMALL + ("w_gate_up", "w_down")
    for k in ("w_in", "w_gate_up"):
        results[k] = tuple(tr(r) for r in results[k])
    for j, k in enumerate(_SMALL):
        results[k] = (g_s[j], d_s[j], m_s[j], v_s[j])
    return (loss, grad_x, *[results[k][0] for k in names], *[results[k][1] for k in names],
            *[results[k][2] for k in names], *[results[k][3] for k in names])
```

```python
import numpy as np
import jax
import jax.numpy as jnp
from jax import lax
from jax.experimental import pallas as pl
from jax.experimental.pallas import tpu as pltpu

F32 = jnp.float32
BF16 = jnp.bfloat16

CHUNK = 64
D_MODEL = 1024
D_CONV = 256
D_POOL = 256
D_ATTN = 512
HEAD_DIM = 64
N_HEADS = 8
N_POOL = 4
POOL_GC = 64
POOL_WINDOWS = (2, 4, 8, 16)
LEFT_CHUNKS = 8
REL_CLIP = 128
D_FF = 2816
EPS = 1e-6
ADAM_LR, ADAM_B1, ADAM_B2, ADAM_EPS, ADAM_WD, ADAM_STEP = 0.001, 0.9, 0.999, 1e-08, 0.01, 10

QB = 256
KB = 3 * QB
HALO = 16
T_CP = 512
T_ROW = 512
QKV_COL = 2
NEG = -1e30
VMEM_MB = 1 << 20
MESH = pl.DeviceIdType.MESH


def _call(body, **kw):
    call = pl.pallas_call(body, **kw)
    return lambda *args: call(*[_in_hbm(a) for a in args])


def _in_hbm(a):
    return pltpu.with_memory_space_constraint(a, pltpu.HBM) if jnp.issubdtype(a.dtype, jnp.number) else a


def _call_after(deps, n_in, body, **kw):
    deps = tuple(deps)
    if not deps:
        return _call(body, **kw)

    def ordered(*refs):
        body(*refs[:n_in], *refs[n_in + len(deps):])

    kw["in_specs"] = list(kw["in_specs"]) + [pl.BlockSpec(memory_space=pl.ANY)] * len(deps)
    call = _call(ordered, **kw)
    return lambda *args: call(*args, *deps)


def _params(sem, vmem_mb=48):
    return pltpu.CompilerParams(dimension_semantics=sem, vmem_limit_bytes=vmem_mb * VMEM_MB)


_CHIP_FLIPS = ((1, 0), (0, 1), (1, 1))
_HBM = pl.BlockSpec(memory_space=pltpu.HBM)
_SEM = pl.BlockSpec(memory_space=pltpu.SEMAPHORE)
_EFFECT = pltpu.SideEffectType.DATAFLOW_SIDE_EFFECTING


def _flip(v, f):
    return 1 - v if f else v


def _descriptors(plan, srcs, lands, send_sems, recv_sems, sending):
    x, y, c = lax.axis_index("x"), lax.axis_index("y"), lax.axis_index("c")
    return [pltpu.make_async_remote_copy(src_ref=src, dst_ref=dst if sending else land, send_sem=send_sems.at[k],
                                         recv_sem=recv_sems.at[k], device_id=peer, device_id_type=MESH)
            for k, (src, dst, peer, land) in enumerate(plan(srcs, lands, x, y, c))]


def exchange_start(name, plan, n_copies, srcs, land_shapes, deps=()):
    ns, nl = len(srcs), len(land_shapes)

    def body(*refs):
        src_refs, land_refs = refs[:ns], refs[ns:ns + nl]
        send_sems, recv_sems = refs[ns + nl], refs[ns + nl + 1]
        for send in _descriptors(plan, src_refs, land_refs, send_sems, recv_sems, True):
            send.start()
        refs[-1][...] = jnp.zeros_like(refs[-1])

    lands = [lax.empty(shape, dtype) for shape, dtype in land_shapes]
    outs = _call_after(
        deps, ns + nl, body, name=name,
        out_shape=(pltpu.SemaphoreType.DMA((n_copies,)), pltpu.SemaphoreType.DMA((n_copies,)),
                   *[pltpu.HBM(a.shape, a.dtype) for a in srcs], *[pltpu.HBM(shape, dtype) for shape, dtype in land_shapes],
                   jax.ShapeDtypeStruct((8, 128), F32)),
        in_specs=[_HBM] * (ns + nl),
        out_specs=(_SEM, _SEM, *[_HBM] * (ns + nl), pl.BlockSpec(memory_space=pltpu.VMEM)),
        input_output_aliases={j: 2 + j for j in range(ns + nl)},
        compiler_params=pltpu.CompilerParams(has_side_effects=_EFFECT),
    )(*srcs, *lands)
    return dict(plan=plan, sems=outs[:2], srcs=outs[2:2 + ns], lands=outs[2 + ns:2 + ns + nl], token=outs[-1])


def exchange_wait(name, started, after):
    srcs, lands = started["srcs"], started["lands"]
    ns, nl = len(srcs), len(lands)

    def body(*refs):
        src_refs, land_refs = refs[:ns], refs[ns:ns + nl]
        send_sems, recv_sems = refs[ns + nl], refs[ns + nl + 1]
        for wait in _descriptors(started["plan"], src_refs, land_refs, send_sems, recv_sems, False):
            wait.wait_send()
            wait.wait_recv()

    outs = _call(
        body, name=name,
        out_shape=tuple(pltpu.HBM(a.shape, a.dtype) for a in (*srcs, *lands)),
        in_specs=[_HBM] * (ns + nl) + [_SEM, _SEM, pl.BlockSpec(memory_space=pl.ANY)],
        out_specs=tuple([_HBM] * (ns + nl)),
        input_output_aliases={j: j for j in range(ns + nl)},
        compiler_params=pltpu.CompilerParams(has_side_effects=_EFFECT),
    )(*srcs, *lands, *started["sems"], after)
    return list(outs[:ns]), list(outs[ns:])


def plan_from_chips(srcs, lands, x, y, c):
    q = 2 * x + y
    out = []
    for src, land in zip(srcs, lands):
        for fx, fy in _CHIP_FLIPS:
            px, py = _flip(x, fx), _flip(y, fy)
            out.append((src, land.at[q], (px, py, c), land.at[2 * px + py]))
    return out


def plan_to_core(srcs, lands, x, y, c):
    n = len(lands)
    q = 2 * x + y
    out = []
    for own, chips, land in zip(srcs[:n], srcs[n:], lands):
        out.append((own, land.at[q], (x, y, 1 - c), land.at[q]))
        for fx, fy in _CHIP_FLIPS:
            qp = 2 * _flip(x, fx) + _flip(y, fy)
            out.append((chips.at[qp], land.at[qp], (x, y, 1 - c), land.at[qp]))
    return out


def plan_swap_cores(srcs, lands, x, y, c):
    return [(src, land, (x, y, 1 - c), land) for src, land in zip(srcs, lands)]


def plan_to_chips(srcs, lands, x, y, c):
    q = 2 * x + y
    out = []
    for src, land in zip(srcs, lands):
        for fx, fy in _CHIP_FLIPS:
            px, py = _flip(x, fx), _flip(y, fy)
            out.append((src.at[2 * px + py], land.at[q], (px, py, c), land.at[2 * px + py]))
    return out


def _slots(a):
    return ((4,) + a.shape, a.dtype)


def gather_start(name, arrays, deps=()):
    return exchange_start(name, plan_from_chips, 3 * len(arrays), arrays, [_slots(a) for a in arrays], deps)


def gather_relay(name, started, after):
    own, chips = exchange_wait(name + "_w", started, after)
    return exchange_start(name + "_s", plan_to_core, 4 * len(own), own + chips, [_slots(a) for a in own])


def gather_finish(name, relayed, after):
    srcs, cores = exchange_wait(name, relayed, after)
    n = len(cores)
    return list(zip(srcs[:n], srcs[n:], cores))


def _place():
    return 2 * lax.axis_index("x") + lax.axis_index("y"), lax.axis_index("c")


def assemble_cols(name, own, chips, core):
    r, c = own.shape
    t = _row_tile(r, 256)

    def body(own_ref, chips_ref, core_ref, o_ref):
        q_me, c_me = _place()
        for q in range(4):
            mine = jnp.where(q == q_me, own_ref[...], chips_ref[q])
            other = core_ref[q]
            o_ref[:, 2 * q * c:(2 * q + 1) * c] = jnp.where(c_me == 0, mine, other)
            o_ref[:, (2 * q + 1) * c:(2 * q + 2) * c] = jnp.where(c_me == 0, other, mine)

    slots = pl.BlockSpec((4, t, c), lambda i: (0, i, 0))
    return _call(
        body, name=name, grid=(r // t,),
        in_specs=[pl.BlockSpec((t, c), lambda i: (i, 0)), slots, slots],
        out_specs=pl.BlockSpec((t, 8 * c), lambda i: (i, 0)),
        out_shape=jax.ShapeDtypeStruct((r, 8 * c), own.dtype), compiler_params=_params(("parallel",)),
    )(own, chips, core)


def assemble_rows(name, own, chips, core):
    r, c = own.shape

    def body(own_ref, chips_ref, core_ref, o_ref):
        q_me, c_me = _place()
        d = pl.program_id(0)
        mine = jnp.where(d // 2 == q_me, own_ref[...], chips_ref[...])
        o_ref[...] = jnp.where(d % 2 == c_me, mine, core_ref[...])

    slot = pl.BlockSpec((None, r, c), lambda d: (d // 2, 0, 0))
    return _call(
        body, name=name, grid=(8,),
        in_specs=[pl.BlockSpec((r, c), lambda d: (0, 0)), slot, slot],
        out_specs=pl.BlockSpec((r, c), lambda d: (d, 0)),
        out_shape=jax.ShapeDtypeStruct((8 * r, c), own.dtype), compiler_params=_params(("parallel",)),
    )(own, chips, core)


def split_rows(name, dw):
    r8, c = dw.shape
    r = r8 // 8

    def body(dw_ref, keep_ref, send_ref):
        _, c_me = _place()
        d = pl.program_id(0)

        @pl.when(d % 2 == c_me)
        def _():
            keep_ref[...] = dw_ref[...]

        @pl.when(d % 2 != c_me)
        def _():
            send_ref[...] = dw_ref[...]

    slot = pl.BlockSpec((None, r, c), lambda d: (d // 2, 0, 0))
    return _call(
        body, name=name, grid=(8,),
        in_specs=[pl.BlockSpec((r, c), lambda d: (d, 0))], out_specs=[slot, slot],
        out_shape=[jax.ShapeDtypeStruct((4, r, c), dw.dtype)] * 2, compiler_params=_params(("arbitrary",)),
    )(dw)


def _rms(v):
    return lax.rsqrt(jnp.mean(v * v, axis=-1, keepdims=True) + EPS)


def rmsnorm_fwd(name, x, g, out_dtype):
    s, d = x.shape

    def body(x_ref, g_ref, o_ref):
        xv = x_ref[...]
        o_ref[...] = ((xv * _rms(xv)) * g_ref[...]).astype(o_ref.dtype)

    row = pl.BlockSpec((T_ROW, d), lambda i: (i, 0))
    return _call(
        body, name=name, grid=(s // T_ROW,),
        in_specs=[row, pl.BlockSpec((1, d), lambda i: (0, 0))], out_specs=row,
        out_shape=jax.ShapeDtypeStruct((s, d), out_dtype), compiler_params=_params(("parallel",)),
    )(x, g)


def matmul_then(name, a_parts, b, mode, tm, rows_in, vecs_in, rows_out, n_sums, then, deps=()):
    a_parts = list(a_parts)
    na, nr, nv, no = len(a_parts), len(rows_in), len(vecs_in), len(rows_out)
    s = a_parts[0].shape[0]
    n = s // tm
    d = b.shape[1] if mode == "nn" else b.shape[0]
    offs = [0]
    for p in a_parts:
        offs.append(offs[-1] + p.shape[1])
    n_in = na + 1 + nr + nv

    def body(*refs):
        a_refs, b_ref = refs[:na], refs[na]
        row_refs, vec_refs = refs[na + 1:na + 1 + nr], refs[na + 1 + nr:n_in]
        out_refs, sum_refs = refs[n_in:n_in + no], refs[n_in + no:n_in + no + n_sums]
        acc = refs[-2:]
        i = pl.program_id(0)

        @pl.when(i == 0)
        def _():
            acc[1][...] = jnp.zeros_like(acc[1])
            for s_ref in sum_refs:
                s_ref[...] = jnp.zeros_like(s_ref)

        def step(write, read):
            t = None
            for p in range(na):
                lo, hi = offs[p], offs[p + 1]
                part = _dot(a_refs[p][...], b_ref[lo:hi, :] if mode == "nn" else b_ref[:, lo:hi], mode)
                t = part if t is None else t + part
            write[...] = t
            results, sums = then(read[...], row_refs, vec_refs)
            for o_ref, val in zip(out_refs, results):
                o_ref[...] = val.astype(o_ref.dtype)
            for s_ref, val in zip(sum_refs, sums):
                s_ref[...] += jnp.where(i > 0, val, 0.0)

        pl.when(i % 2 == 0)(lambda: step(acc[0], acc[1]))
        pl.when(i % 2 == 1)(lambda: step(acc[1], acc[0]))

    ahead = lambda i: (jnp.minimum(i, n - 1), 0)
    behind = lambda i: (jnp.maximum(i - 1, 0), 0)
    fixed = lambda i: (0, 0)
    shapes = [jax.ShapeDtypeStruct((s, d), dt) for dt in rows_out] + [jax.ShapeDtypeStruct((1, d), F32)] * n_sums
    return _call_after(
        deps, n_in, body, name=name, grid=(n + 1,),
        in_specs=[pl.BlockSpec((tm, p.shape[1]), ahead) for p in a_parts] + [pl.BlockSpec(b.shape, fixed)]
        + [pl.BlockSpec((tm, r.shape[1]), behind) for r in rows_in] + [pl.BlockSpec(v.shape, fixed) for v in vecs_in],
        out_specs=[pl.BlockSpec((tm, d), behind)] * no + [pl.BlockSpec((1, d), fixed)] * n_sums,
        out_shape=shapes, scratch_shapes=[pltpu.VMEM((tm, d), F32)] * 2,
        compiler_params=_params(("arbitrary",), 56),
    )(*a_parts, b, *rows_in, *vecs_in)


def matmul_norm(name, a, b, res, g, then=None, deps=()):
    def norm(z, rows, vecs):
        y = rows[0][...] + (z * _rms(z)) * vecs[0][...]
        return [z, y] + ([(y * _rms(y)) * vecs[1][...]] if then else []), []

    return matmul_then(name, [a], b, "nn", 512, [res], [g] + ([then[0]] if then else []),
                       [BF16, F32] + ([then[1]] if then else []), 0, norm, deps)


def matmul_norm_bwd(name, a_parts, b, tm, x, g, res, then=None, deps=()):
    def norms(t, rows, vecs):
        xv = rows[0][...].astype(F32)
        dx, dgt = _norm_bwd(xv, _rms(xv), vecs[0][...], t)
        dx = dx + rows[1][...]
        results, sums = [dx], [jnp.sum(dgt, axis=0, keepdims=True)]
        if then:
            x2 = rows[2][...].astype(F32)
            dx2, dgt2 = _norm_bwd(x2, _rms(x2), vecs[1][...], dx)
            results.append(dx2)
            sums.append(jnp.sum(dgt2, axis=0, keepdims=True))
        return results, sums

    return matmul_then(name, a_parts, b, "nn", tm, [x, res] + ([then[0]] if then else []),
                       [g] + ([then[1]] if then else []), [F32] + ([then[2]] if then else []),
                       2 if then else 1, norms, deps)


def _norm_bwd(xv, r, g, dy):
    a = dy * g
    dx = r * (a - xv * ((r * r) * jnp.mean(a * xv, axis=-1, keepdims=True)))
    return dx, dy * (xv * r)


def rmsnorm_bwd(name, x, g, dy, res, out_dtype, then=None, deps=()):
    s, d = x.shape
    has_res = res is not None
    n_in = 3 + has_res + (2 if then else 0)

    def body(*refs):
        x_ref, g_ref, dy_ref = refs[:3]
        outs = refs[n_in:]
        xv = x_ref[...].astype(F32)
        dx, dgt = _norm_bwd(xv, _rms(xv), g_ref[...], dy_ref[...].astype(F32))
        if has_res:
            dx = dx + refs[3][...]
        outs[0][...] = dx.astype(outs[0].dtype)
        sums = [(outs[1], dgt)]
        if then is not None:
            x2 = refs[n_in - 2][...].astype(F32)
            dx2, dgt2 = _norm_bwd(x2, _rms(x2), refs[n_in - 1][...], dx)
            outs[2][...] = dx2.astype(outs[2].dtype)
            sums.append((outs[3], dgt2))

        @pl.when(pl.program_id(0) == 0)
        def _():
            for dg_ref, _ in sums:
                dg_ref[...] = jnp.zeros_like(dg_ref)
        for dg_ref, terms in sums:
            dg_ref[...] += jnp.sum(terms, axis=0, keepdims=True)

    row = pl.BlockSpec((T_ROW, d), lambda i: (i, 0))
    vec = pl.BlockSpec((1, d), lambda i: (0, 0))
    args = (x, g, dy) + ((res,) if has_res else ()) + ((then[0], then[1]) if then else ())
    shapes = [jax.ShapeDtypeStruct((s, d), out_dtype), jax.ShapeDtypeStruct((1, d), F32)]
    if then:
        shapes += [jax.ShapeDtypeStruct((s, d), then[2]), jax.ShapeDtypeStruct((1, d), F32)]
    return _call_after(
        deps, len(args), body, name=name, grid=(s // T_ROW,),
        in_specs=[row, vec, row] + ([row] if has_res else []) + ([row, vec] if then else []),
        out_specs=[row, vec] * (2 if then else 1), out_shape=shapes,
        compiler_params=_params(("arbitrary",)),
    )(*args)


def loss_head(name, h, tgt):
    s, d = h.shape

    def body(h_ref, t_ref, dh_ref, l_ref):
        e = h_ref[...] - t_ref[...]
        dh_ref[...] = e * (1.0 / d)

        @pl.when(pl.program_id(0) == 0)
        def _():
            l_ref[...] = jnp.zeros_like(l_ref)
        part = 0.5 * jnp.sum(jnp.mean(e * e, axis=-1, keepdims=True), axis=0, keepdims=True)
        l_ref[...] += jnp.broadcast_to(part, l_ref.shape)

    row = pl.BlockSpec((T_ROW, d), lambda i: (i, 0))
    return _call(
        body, name=name, grid=(s // T_ROW,),
        in_specs=[row, row], out_specs=[row, pl.BlockSpec((1, 128), lambda i: (0, 0))],
        out_shape=[jax.ShapeDtypeStruct((s, d), F32), jax.ShapeDtypeStruct((1, 128), F32)],
        compiler_params=_params(("arbitrary",)),
    )(h, tgt)


def _row_tile(rows, limit=512):
    t = min(rows, limit)
    while rows % t or (t % 8 and t != rows):
        t -= 1
    return t


def add_pairs(name, a, b):
    shape = a.shape
    a2, b2 = a.reshape(-1, shape[-1]), b.reshape(-1, shape[-1])
    rows, cols = a2.shape
    t = _row_tile(rows)

    def body(a_ref, b_ref, o_ref):
        o_ref[...] = (a_ref[...].astype(F32) + b_ref[...].astype(F32)).astype(o_ref.dtype)

    blk = pl.BlockSpec((t, cols), lambda i: (i, 0))
    out = _call(
        body, name=name, grid=(rows // t,), in_specs=[blk, blk], out_specs=blk,
        out_shape=jax.ShapeDtypeStruct((rows, cols), a.dtype), compiler_params=_params(("parallel",)),
    )(a2, b2)
    return out.reshape(shape)


def _adamw_math(w, m, v, g):
    c1 = 1.0 - ADAM_B1 ** ADAM_STEP
    c2 = 1.0 - ADAM_B2 ** ADAM_STEP
    mn = ADAM_B1 * m + (1.0 - ADAM_B1) * g
    vn = ADAM_B2 * v + (1.0 - ADAM_B2) * (g * g)
    delta = -ADAM_LR * ((mn / c1) / (jnp.sqrt(vn / c2) + ADAM_EPS) + ADAM_WD * w)
    return delta, mn, vn


def _slab_sum(g_ref, n):
    g = g_ref[0].astype(F32)
    for j in range(1, n):
        g = g + g_ref[j].astype(F32)
    return g


def adamw(name, w, m, v, g_slabs, deps=()):
    shape = w.shape
    n = g_slabs.shape[0]
    w2, m2, v2 = (t.reshape(-1, shape[-1]) for t in (w, m, v))
    g3 = g_slabs.reshape(n, -1, shape[-1])
    rows, cols = w2.shape
    t = _row_tile(rows, 256)

    def body(w_ref, m_ref, v_ref, g_ref, go_ref, d_ref, mo_ref, vo_ref):
        g = _slab_sum(g_ref, n)
        go_ref[...] = g
        d_ref[...], mo_ref[...], vo_ref[...] = _adamw_math(w_ref[...], m_ref[...], v_ref[...], g)

    blk = pl.BlockSpec((t, cols), lambda i: (i, 0))
    outs = _call_after(
        deps, 4, body, name=name, grid=(rows // t,),
        in_specs=[blk, blk, blk, pl.BlockSpec((n, t, cols), lambda i: (0, i, 0))], out_specs=[blk] * 4,
        out_shape=[jax.ShapeDtypeStruct((rows, cols), F32)] * 4, compiler_params=_params(("parallel",)),
    )(w2, m2, v2, g3)
    return tuple(o.reshape(shape) for o in outs)


def adamw_layer(name, l, w, m, v, own_sums, chip_sums, into):
    _, rows, cols = w.shape
    t = _row_tile(rows, 256)
    if into is None:
        into = tuple(lax.empty(w.shape, F32) for _ in range(4))

    def body(w_ref, m_ref, v_ref, own_ref, far_ref, a0, a1, a2, a3, go_ref, d_ref, mo_ref, vo_ref):
        del a0, a1, a2, a3
        q_me, _ = _place()
        g = jnp.zeros((t, cols), F32)
        for q in range(4):
            g = g + jnp.where(q == q_me, own_ref[q], far_ref[q]).astype(F32)
        go_ref[...] = g
        d_ref[...], mo_ref[...], vo_ref[...] = _adamw_math(w_ref[...], m_ref[...], v_ref[...], g)

    blk = pl.BlockSpec((None, t, cols), lambda i: (l, i, 0))
    slabs = pl.BlockSpec((4, t, cols), lambda i: (0, i, 0))
    anyw = pl.BlockSpec(memory_space=pl.ANY)
    return _call(
        body, name=name, grid=(rows // t,),
        in_specs=[blk, blk, blk, slabs, slabs] + [anyw] * 4, out_specs=[blk] * 4,
        out_shape=[jax.ShapeDtypeStruct(w.shape, F32)] * 4, input_output_aliases={5: 0, 6: 1, 7: 2, 8: 3},
        compiler_params=_params(("parallel",)),
    )(w, m, v, own_sums, chip_sums, *into)


_DIMS = {"nn": (((1,), (0,)), ((), ())), "nt": (((1,), (1,)), ((), ())), "tn": (((0,), (0,)), ((), ()))}


def _dot(a, b, mode="nn"):
    return lax.dot_general(a, b, _DIMS[mode], preferred_element_type=F32)


def matmul(name, a, b, mode, out_dtype, m, n, k, tm, tn, tk, b_off=(0, 0), deps=()):
    gm, gn, gk = m // tm, n // tn, k // tk
    assert gm * tm == m and gn * tn == n and gk * tk == k
    r0, c0 = b_off
    a_parts = list(a) if isinstance(a, (list, tuple)) else [a]
    na = len(a_parts)
    tile = tm if mode == "tn" else tk
    spans, at = [], 0
    for p in a_parts:
        nblk = p.shape[1] // tile
        assert nblk * tile == p.shape[1]
        spans.append((at, nblk))
        at += nblk
    assert at == (gm if mode == "tn" else gk)

    def within(t, span):
        return (t >= span[0]) & (t < span[0] + span[1])

    def local(t, span):
        return jnp.clip(t - span[0], 0, span[1] - 1)

    a_specs = []
    for sp in spans:
        if mode == "tn":
            a_specs.append(pl.BlockSpec((tk, tm), lambda i, j, kk, sp=sp: (jnp.where(within(i, sp), kk, 0), local(i, sp))))
        else:
            a_specs.append(pl.BlockSpec((tm, tk), lambda i, j, kk, sp=sp: (i, local(kk, sp))))
    if mode == "nt":
        b_spec = pl.BlockSpec((tn, tk), lambda i, j, kk: (j + r0, kk + c0))
    else:
        b_spec = pl.BlockSpec((tk, tn), lambda i, j, kk: (kk + r0, j + c0))
    o_spec = pl.BlockSpec((tm, tn), lambda i, j, kk: (i, j))
    single = na == 1 and gk == 1

    def body(*refs):
        a_refs, b_ref, o_ref = refs[:na], refs[na], refs[na + 1]
        if single:
            o_ref[...] = _dot(a_refs[0][...], b_ref[...], mode).astype(o_ref.dtype)
            return
        acc_ref = refs[-1]
        i, kk = pl.program_id(0), pl.program_id(2)

        @pl.when(kk == 0)
        def _():
            acc_ref[...] = jnp.zeros_like(acc_ref)

        for pa, sp in enumerate(spans):
            def add(pa=pa):
                acc_ref[...] += _dot(a_refs[pa][...], b_ref[...], mode)
            if na > 1:
                pl.when(within(i if mode == "tn" else kk, sp))(add)
            else:
                add()

        @pl.when(kk == gk - 1)
        def _():
            o_ref[...] = acc_ref[...].astype(o_ref.dtype)

    return _call_after(
        deps, na + 1, body, name=name, grid=(gm, gn, gk),
        in_specs=a_specs + [b_spec], out_specs=o_spec,
        out_shape=jax.ShapeDtypeStruct((m, n), out_dtype),
        scratch_shapes=[] if single else [pltpu.VMEM((tm, tn), F32)],
        compiler_params=_params(("parallel", "parallel", "arbitrary")),
    )(*a_parts, b)


def matmul_swiglu(name, a, w_gu_t, deps=()):
    s, d = a.shape
    tm, tn = 512, 1408
    gn = D_FF // tn

    def body(a_ref, wg_ref, wu_ref, g_ref, u_ref, f_ref):
        av = a_ref[...]
        g = _dot(av, wg_ref[...], "nt")
        u = _dot(av, wu_ref[...], "nt")
        g_ref[...] = g.astype(g_ref.dtype)
        u_ref[...] = u.astype(u_ref.dtype)
        f_ref[...] = ((g * (1.0 / (1.0 + jnp.exp(-g)))) * u).astype(f_ref.dtype)

    o_spec = pl.BlockSpec((tm, tn), lambda j, i: (i, j))
    return _call_after(
        deps, 3, body, name=name, grid=(gn, s // tm),
        in_specs=[pl.BlockSpec((tm, d), lambda j, i: (i, 0)), pl.BlockSpec((tn, d), lambda j, i: (j, 0)),
                  pl.BlockSpec((tn, d), lambda j, i: (j + gn, 0))],
        out_specs=[o_spec] * 3, out_shape=[jax.ShapeDtypeStruct((s, D_FF), BF16)] * 3,
        compiler_params=_params(("parallel", "parallel")),
    )(a, w_gu_t, w_gu_t)


def matmul_swiglu_bwd(name, dffo, w_down, g, u):
    s, d = dffo.shape
    tm, tn = 512, 1408

    def body(a_ref, w_ref, g_ref, u_ref, dg_ref, du_ref):
        av = a_ref[...]
        for lo in range(0, tn, 256):
            cols = slice(lo, min(lo + 256, tn))
            dff = _dot(av, w_ref[cols, :], "nt")
            gv = g_ref[:, cols].astype(F32)
            sig = 1.0 / (1.0 + jnp.exp(-gv))
            silu = gv * sig
            du_ref[:, cols] = (dff * silu).astype(du_ref.dtype)
            dg_ref[:, cols] = ((dff * u_ref[:, cols].astype(F32)) * (sig + silu * (1.0 - sig))).astype(dg_ref.dtype)

    o_spec = pl.BlockSpec((tm, tn), lambda j, i: (i, j))
    return _call(
        body, name=name, grid=(D_FF // tn, s // tm),
        in_specs=[pl.BlockSpec((tm, d), lambda j, i: (i, 0)), pl.BlockSpec((tn, d), lambda j, i: (j, 0)), o_spec, o_spec],
        out_specs=[o_spec] * 2, out_shape=[jax.ShapeDtypeStruct((s, D_FF), BF16)] * 2,
        compiler_params=_params(("parallel", "parallel")),
    )(dffo, w_down, g, u)


def _down(v, n):
    return pltpu.roll(v, n, 0)


def _up(v, n):
    return pltpu.roll(v, v.shape[0] - n, 0)


def _by_window(lane, v2, v4, v8, v16):
    return jnp.where(lane < POOL_GC, v2, jnp.where(lane < 2 * POOL_GC, v4, jnp.where(lane < 3 * POOL_GC, v8, v16)))


def _taps(cw_ref):
    return cw_ref[0:1, :], cw_ref[1:2, :], cw_ref[2:3, :]


def _conv_pool_forward(ext, t0, cw, bd):
    n_out = ext.shape[0] - HALO
    gb = ext[HALO:, 0:D_CONV]
    z = ext[:, D_CONV:2 * D_CONV] * ext[:, 2 * D_CONV:3 * D_CONV]
    z0, z1, z2 = z[HALO:], _down(z, 1)[HALO:], _down(z, 2)[HALO:]
    conv = cw[2] * z0 + cw[1] * z1 + cw[0] * z2
    x = ext[:, 3 * D_CONV:]
    w2 = x + _down(x, 1)
    w4 = w2 + _down(w2, 2)
    w8 = w4 + _down(w4, 4)
    w16 = w8 + _down(w8, 8)
    lane = lax.broadcasted_iota(jnp.int32, (1, D_POOL), 1)
    win = _by_window(lane, 2.0, 4.0, 8.0, 16.0)
    pos = (t0 + lax.broadcasted_iota(jnp.int32, (n_out, 1), 0) + 1).astype(F32)
    cnt = jnp.minimum(pos, win)
    d = _by_window(lane, w2, w4, w8, w16)[HALO:] / cnt - x[HALO:]
    ybp = _dot(d.astype(BF16), bd)
    return gb, z0, z1, z2, conv, d, cnt, ybp, lane


def conv_pool_fwd(name, proj_a, cw, bd, ps, gg, deps=()):
    s = proj_a.shape[0]
    t = T_CP
    hb = t // HALO

    def body(main_ref, prev_ref, cw_ref, bd_ref, ps_ref, gg_ref, y_ref):
        i = pl.program_id(0)
        prev = jnp.where(i > 0, prev_ref[...].astype(F32), 0.0)
        ext = jnp.concatenate([prev, main_ref[...].astype(F32)], axis=0)
        gb, _, _, _, conv, _, _, ybp, _ = _conv_pool_forward(ext, i * t, _taps(cw_ref), bd_ref[...])
        ya = gb * conv
        yb = ybp * ps_ref[...]
        ggv = gg_ref[...]
        y_ref[:, 0:D_CONV] = ((ya * _rms(ya)) * ggv[:, 0:D_CONV]).astype(y_ref.dtype)
        y_ref[:, D_CONV:] = ((yb * _rms(yb)) * ggv[:, D_CONV:]).astype(y_ref.dtype)

    full = lambda shape: pl.BlockSpec(shape, lambda i: (0,) * len(shape))
    return _call_after(
        deps, 6, body, name=name, grid=(s // t,),
        in_specs=[pl.BlockSpec((t, D_MODEL), lambda i: (i, 0)),
                  pl.BlockSpec((HALO, D_MODEL), lambda i: (jnp.maximum(i * hb - 1, 0), 0)),
                  full((3, D_CONV)), full((D_POOL, D_POOL)), full((1, D_POOL)), full((1, 2 * D_CONV))],
        out_specs=pl.BlockSpec((t, 2 * D_CONV), lambda i: (i, 0)),
        out_shape=jax.ShapeDtypeStruct((s, D_MODEL), BF16),
        compiler_params=_params(("parallel",)),
    )(proj_a, proj_a, cw, bd, ps, gg)


def conv_pool_bwd(name, proj_a, dy, cw, bd, ps, gg):
    s = proj_a.shape[0]
    t = T_CP
    hb = t // HALO
    nblk = s // t
    last_halo = s // HALO - 1

    def body(main_ref, prev_ref, next_ref, dy_ref, dyn_ref, cw_ref, bd_ref, ps_ref, gg_ref,
             dp_ref, dcw_ref, dbd_ref, dps_ref, dgg_ref):
        i = pl.program_id(0)
        prev = jnp.where(i > 0, prev_ref[...].astype(F32), 0.0)
        main = main_ref[...].astype(F32)
        ext = jnp.concatenate([prev, main, next_ref[...].astype(F32)], axis=0)
        cwv, bdv, psv, ggv = _taps(cw_ref), bd_ref[...], ps_ref[...], gg_ref[...]
        gb, z0, z1, z2, conv, d, cnt, ybp, lane = _conv_pool_forward(ext, i * t, cwv, bdv)
        dyn = jnp.where(i < nblk - 1, dyn_ref[...].astype(F32), 0.0)
        dyv = jnp.concatenate([dy_ref[...].astype(F32), dyn], axis=0)
        ya = gb * conv
        yb = ybp * psv
        dya, dgg_a = _norm_bwd(ya, _rms(ya), ggv[:, 0:D_CONV], dyv[:, 0:D_CONV])
        dyb, dgg_b = _norm_bwd(yb, _rms(yb), ggv[:, D_CONV:], dyv[:, D_CONV:])

        dconv = dya * gb
        dz = (cwv[2] * dconv + cwv[1] * _up(dconv, 1) + cwv[0] * _up(dconv, 2))[:t]
        dp_ref[:, 0:D_CONV] = (dya * conv)[:t].astype(dp_ref.dtype)
        dp_ref[:, D_CONV:2 * D_CONV] = (dz * main[:, 2 * D_CONV:3 * D_CONV]).astype(dp_ref.dtype)
        dp_ref[:, 2 * D_CONV:3 * D_CONV] = (dz * main[:, D_CONV:2 * D_CONV]).astype(dp_ref.dtype)

        dybs = dyb * psv
        dd = _dot(dybs.astype(BF16), bdv, "nt")
        e = dd / cnt
        a2 = e + _up(e, 1)
        a4 = a2 + _up(a2, 2)
        a8 = a4 + _up(a4, 4)
        a16 = a8 + _up(a8, 8)
        dp_ref[:, 3 * D_CONV:] = (_by_window(lane, a2, a4, a8, a16) - dd)[:t].astype(dp_ref.dtype)

        @pl.when(i == 0)
        def _():
            dcw_ref[...] = jnp.zeros_like(dcw_ref)
            dbd_ref[...] = jnp.zeros_like(dbd_ref)
            dps_ref[...] = jnp.zeros_like(dps_ref)
            dgg_ref[...] = jnp.zeros_like(dgg_ref)

        rsum = lambda v: jnp.sum(v[:t], axis=0, keepdims=True)
        dcw_ref[0:1, :] += rsum(dconv * z2)
        dcw_ref[1:2, :] += rsum(dconv * z1)
        dcw_ref[2:3, :] += rsum(dconv * z0)
        dbd_ref[...] += _dot(d[:t].astype(BF16), dybs[:t].astype(BF16), "tn")
        dps_ref[...] += rsum(dyb * ybp)
        dgg_ref[:, 0:D_CONV] += rsum(dgg_a)
        dgg_ref[:, D_CONV:] += rsum(dgg_b)

    full = lambda shape: pl.BlockSpec(shape, lambda i: (0,) * len(shape))
    next_halo = lambda i: (jnp.minimum((i + 1) * hb, last_halo), 0)
    return _call(
        body, name=name, grid=(nblk,),
        in_specs=[pl.BlockSpec((t, D_MODEL), lambda i: (i, 0)),
                  pl.BlockSpec((HALO, D_MODEL), lambda i: (jnp.maximum(i * hb - 1, 0), 0)),
                  pl.BlockSpec((HALO, D_MODEL), next_halo),
                  pl.BlockSpec((t, 2 * D_CONV), lambda i: (i, 0)),
                  pl.BlockSpec((HALO, 2 * D_CONV), next_halo),
                  full((3, D_CONV)), full((D_POOL, D_POOL)), full((1, D_POOL)), full((1, 2 * D_CONV))],
        out_specs=[pl.BlockSpec((t, D_MODEL), lambda i: (i, 0)),
                   full((3, D_CONV)), full((D_POOL, D_POOL)), full((1, D_POOL)), full((1, 2 * D_CONV))],
        out_shape=[jax.ShapeDtypeStruct((s, D_MODEL), BF16), jax.ShapeDtypeStruct((3, D_CONV), F32),
                   jax.ShapeDtypeStruct((D_POOL, D_POOL), F32), jax.ShapeDtypeStruct((1, D_POOL), F32),
                   jax.ShapeDtypeStruct((1, 2 * D_CONV), F32)],
        compiler_params=_params(("arbitrary",)),
    )(proj_a, proj_a, proj_a, dy, dy, cw, bd, ps, gg)


def _band_mask():
    qa = np.arange(QB)[:, None] // CHUNK
    km = np.arange(KB)[None, :] // CHUNK
    return (km - qa >= 0) & (km - qa <= LEFT_CHUNKS)


def attention_bias(rel_bias):
    n_far = 2 * QB - REL_CLIP + 1
    far = jnp.broadcast_to(rel_bias[:, 2 * REL_CLIP:], (N_HEADS, n_far))
    mid = rel_bias[:, 1:2 * REL_CLIP][:, ::-1]
    near = jnp.broadcast_to(rel_bias[:, 0:1], (N_HEADS, KB - n_far - (2 * REL_CLIP - 1)))
    wrap = jnp.broadcast_to(rel_bias[:, 2 * REL_CLIP:], (N_HEADS, 4 * QB - KB))
    by_offset = jnp.concatenate([far, mid, near, wrap], axis=1)
    width = 4 * QB
    toeplitz = jnp.tile(by_offset, (1, QB))[:, :QB * (width - 1)].reshape(N_HEADS, QB, width - 1)[:, :, :KB]
    return jnp.where(jnp.asarray(_band_mask())[None], toeplitz, NEG)


def rel_bias_grad(by_offset):
    n_far = 2 * QB - REL_CLIP + 1
    hi = jnp.sum(by_offset[:, :n_far], axis=1, keepdims=True) + jnp.sum(by_offset[:, KB:], axis=1, keepdims=True)
    mid = by_offset[:, n_far:n_far + 2 * REL_CLIP - 1][:, ::-1]
    lo = jnp.sum(by_offset[:, n_far + 2 * REL_CLIP - 1:KB], axis=1, keepdims=True)
    return jnp.concatenate([lo, mid, hi], axis=1)


def _head_masks():
    first = lax.broadcasted_iota(jnp.int32, (1, 2 * HEAD_DIM), 1) < HEAD_DIM
    return first, jnp.logical_not(first)


def _pick_lane(tile, h):
    lane = lax.broadcasted_iota(jnp.int32, (1, tile.shape[1]), 1)
    return jnp.sum(jnp.where(lane == h, tile, 0.0), axis=-1, keepdims=True)


def attention_fwd(name, qkv, bias, gg, y_ab):
    s = qkv.shape[0]
    nq = s // QB
    scale = HEAD_DIM ** -0.5

    def body(q_ref, k0, k1, k2, v0, v1, v2, b_ref, gg_ref, y_in, y_ref, o_ref, lse_ref):
        del y_in
        i = pl.program_id(0)
        kb = jnp.concatenate([k0[...], k1[...], k2[...]], axis=0)
        vb = jnp.concatenate([v0[...], v1[...], v2[...]], axis=0)
        valid = lax.broadcasted_iota(jnp.int32, (1, KB), 1) >= (2 - i) * QB
        lane = lax.broadcasted_iota(jnp.int32, (1, 128), 1)
        masks = _head_masks()
        lse = jnp.zeros((QB, 128), F32)
        outs = []
        for hp in range(N_HEADS // 2):
            sl = slice(2 * HEAD_DIM * hp, 2 * HEAD_DIM * (hp + 1))
            q_p, k_p, v_p = q_ref[:, sl] * scale, kb[:, sl], vb[:, sl]
            o_pair = jnp.zeros((QB, 2 * HEAD_DIM), F32)
            for a in range(2):
                h = 2 * hp + a
                sc = _dot(jnp.where(masks[a], q_p, 0), k_p, "nt") + b_ref[h]
                sc = jnp.where(valid, sc, NEG)
                mx = jnp.max(sc, axis=-1, keepdims=True)
                e = jnp.exp(sc - mx)
                l = jnp.sum(e, axis=-1, keepdims=True)
                o_pair = o_pair + _dot(e.astype(BF16), jnp.where(masks[a], v_p, 0)) * (1.0 / l)
                lse = jnp.where(lane == h, mx + jnp.log(l), lse)
            outs.append(o_pair)
        o = jnp.concatenate(outs, axis=1)
        o_ref[...] = o.astype(o_ref.dtype)
        lse_ref[...] = lse
        y_ref[...] = ((o * _rms(o)) * gg_ref[...]).astype(y_ref.dtype)

    blk = lambda col, back: pl.BlockSpec((QB, D_ATTN), lambda i: (jnp.maximum(i - back, 0), QKV_COL + col))
    return _call(
        body, name=name, grid=(nq,),
        in_specs=[blk(0, 0), blk(1, 2), blk(1, 1), blk(1, 0), blk(2, 2), blk(2, 1), blk(2, 0),
                  pl.BlockSpec((N_HEADS, QB, KB), lambda i: (0, 0, 0)), pl.BlockSpec((1, D_ATTN), lambda i: (0, 0)),
                  pl.BlockSpec(memory_space=pl.ANY)],
        out_specs=[pl.BlockSpec((QB, D_ATTN), lambda i: (i, 1)), pl.BlockSpec((QB, D_ATTN), lambda i: (i, 0)),
                   pl.BlockSpec((QB, 128), lambda i: (i, 0))],
        out_shape=[jax.ShapeDtypeStruct((s, D_MODEL), BF16), jax.ShapeDtypeStruct((s, D_ATTN), BF16),
                   jax.ShapeDtypeStruct((s, 128), F32)],
        input_output_aliases={9: 0},
        compiler_params=_params(("arbitrary",), 56),
    )(qkv, qkv, qkv, qkv, qkv, qkv, qkv, bias, gg, y_ab)


def attention_bwd(name, qkv, o, lse, dy, bias, gg):
    s = qkv.shape[0]
    nq = s // QB
    scale = HEAD_DIM ** -0.5
    width = 4 * QB

    def body(q_ref, k0, k1, k2, v0, v1, v2, o_ref, lse_ref, dy_ref, b_ref, gg_ref,
             dq_ref, dk_ref, dv_ref, off_ref, dgg_ref, dk_acc, dv_acc, db_acc):
        i = pl.program_id(0)

        @pl.when(i == 0)
        def _():
            dk_acc[...] = jnp.zeros_like(dk_acc)
            dv_acc[...] = jnp.zeros_like(dv_acc)
            db_acc[...] = jnp.zeros_like(db_acc)
            dgg_ref[...] = jnp.zeros_like(dgg_ref)

        @pl.when(i > 0)
        def _():
            for acc in (dk_acc, dv_acc):
                kept = acc[QB:, :]
                acc[0:2 * QB, :] = kept
                acc[2 * QB:, :] = jnp.zeros((QB, D_ATTN), F32)

        @pl.when(i < nq)
        def _():
            ov = o_ref[...].astype(F32)
            dyv = dy_ref[...].astype(F32)
            do, dgg_t = _norm_bwd(ov, _rms(ov), gg_ref[...], dyv)
            dgg_ref[...] += jnp.sum(dgg_t, axis=0, keepdims=True)
            kb = jnp.concatenate([k0[...], k1[...], k2[...]], axis=0)
            vb = jnp.concatenate([v0[...], v1[...], v2[...]], axis=0)
            valid = lax.broadcasted_iota(jnp.int32, (1, KB), 1) >= (2 - i) * QB
            masks = _head_masks()
            lse_t = lse_ref[...]
            for hp in range(N_HEADS // 2):
                sl = slice(2 * HEAD_DIM * hp, 2 * HEAD_DIM * (hp + 1))
                q_p, k_p, v_p = q_ref[:, sl] * scale, kb[:, sl], vb[:, sl]
                do_p = do[:, sl]
                prod = do_p * ov[:, sl]
                do_b = do_p.astype(BF16)
                dq_pair = jnp.zeros((QB, 2 * HEAD_DIM), F32)
                dk_pair = jnp.zeros((KB, 2 * HEAD_DIM), F32)
                dv_pair = jnp.zeros((KB, 2 * HEAD_DIM), F32)
                for a in range(2):
                    h = 2 * hp + a
                    q_m = jnp.where(masks[a], q_p, 0)
                    do_m = jnp.where(masks[a], do_b, 0)
                    sc = _dot(q_m, k_p, "nt") + b_ref[h]
                    sc = jnp.where(valid, sc, NEG)
                    p = jnp.exp(sc - _pick_lane(lse_t, h))
                    dp = _dot(do_m, v_p, "nt")
                    delta = jnp.sum(jnp.where(masks[a], prod, 0.0), axis=-1, keepdims=True)
                    ds = p * (dp - delta)
                    db_acc[h] += ds
                    ds_b = ds.astype(BF16)
                    dq_pair = dq_pair + _dot(ds_b, jnp.where(masks[a], k_p, 0))
                    dk_pair = dk_pair + _dot(ds_b, q_m, "tn")
                    dv_pair = dv_pair + _dot(p.astype(BF16), do_m, "tn")
                dq_ref[:, sl] = (dq_pair * scale).astype(dq_ref.dtype)
                dk_acc[:, sl] += dk_pair
                dv_acc[:, sl] += dv_pair

        dk_ref[...] = dk_acc[0:QB, :].astype(dk_ref.dtype)
        dv_ref[...] = dv_acc[0:QB, :].astype(dv_ref.dtype)

        @pl.when(i == nq + 1)
        def _():
            sub = lax.broadcasted_iota(jnp.int32, (8, 1), 0)
            pad = jnp.zeros((8, width - KB), F32)
            for h in range(N_HEADS):
                v = jnp.concatenate([db_acc[h, 0:8, :], pad], axis=1)
                for a in range(1, QB // 8):
                    grp = jnp.concatenate([db_acc[h, 8 * a:8 * a + 8, :], pad], axis=1)
                    v = v + pltpu.roll(grp, width - 8 * a, 1)
                for bit in range(3):
                    v = jnp.where(((sub >> bit) & 1) == 1, pltpu.roll(v, width - (1 << bit), 1), v)
                off_ref[h:h + 1, :] = jnp.sum(v, axis=0, keepdims=True)

    qi = lambda i: jnp.minimum(i, nq - 1)
    kblk = lambda col, back: pl.BlockSpec((QB, D_ATTN), lambda i: (jnp.clip(i - back, 0, nq - 1), QKV_COL + col))
    qblk = lambda col: pl.BlockSpec((QB, D_ATTN), lambda i: (qi(i), col))
    done = pl.BlockSpec((QB, D_ATTN), lambda i: (jnp.clip(i - 2, 0, nq - 1), 0))
    return _call(
        body, name=name, grid=(nq + 2,),
        in_specs=[qblk(QKV_COL), kblk(1, 2), kblk(1, 1), kblk(1, 0), kblk(2, 2), kblk(2, 1), kblk(2, 0),
                  qblk(0), pl.BlockSpec((QB, 128), lambda i: (qi(i), 0)), qblk(1),
                  pl.BlockSpec((N_HEADS, QB, KB), lambda i: (0, 0, 0)), pl.BlockSpec((1, D_ATTN), lambda i: (0, 0))],
        out_specs=[qblk(0), done, done, pl.BlockSpec((N_HEADS, width), lambda i: (0, 0)),
                   pl.BlockSpec((1, D_ATTN), lambda i: (0, 0))],
        out_shape=[jax.ShapeDtypeStruct((s, D_ATTN), BF16)] * 3
        + [jax.ShapeDtypeStruct((N_HEADS, width), F32), jax.ShapeDtypeStruct((1, D_ATTN), F32)],
        scratch_shapes=[pltpu.VMEM((KB, D_ATTN), F32), pltpu.VMEM((KB, D_ATTN), F32), pltpu.VMEM((N_HEADS, QB, KB), F32)],
        compiler_params=_params(("arbitrary",), 56),
    )(qkv, qkv, qkv, qkv, qkv, qkv, qkv, o, lse, dy, bias, gg)


def _block_diag(pw):
    out = jnp.zeros((D_POOL, D_POOL), pw.dtype)
    for gi in range(N_POOL):
        out = lax.dynamic_update_slice(out, pw[gi], (gi * POOL_GC, gi * POOL_GC))
    return out


_SMALL = ("pool_w", "pool_scale", "rel_bias", "group_gain", "pre_mix_g", "post_mix_g", "pre_ffn_g", "post_ffn_g")


def _pack(parts, rows):
    flat = jnp.concatenate([p.reshape(-1).astype(F32) for p in parts])
    return jnp.pad(flat, (0, rows * D_MODEL - flat.shape[0])).reshape(rows, D_MODEL)


def _unpack(packed, shapes):
    flat = packed.reshape(-1)
    out, at = [], 0
    for shp in shapes:
        size = int(np.prod(shp))
        out.append(flat[at:at + size].reshape(shp))
        at += size
    return out


def kernel(x, w_in, w_out, conv_w, pool_w, pool_scale, rel_bias, group_gain, pre_mix_g, post_mix_g, pre_ffn_g, post_ffn_g, w_gate_up, w_down, loss_target, m_w_in, m_w_out, m_conv_w, m_pool_w, m_pool_scale, m_rel_bias, m_group_gain, m_pre_mix_g, m_post_mix_g, m_pre_ffn_g, m_post_ffn_g, m_w_gate_up, m_w_down, v_w_in, v_w_out, v_conv_w, v_pool_w, v_pool_scale, v_rel_bias, v_group_gain, v_pre_mix_g, v_post_mix_g, v_pre_ffn_g, v_post_ffn_g, v_w_gate_up, v_w_down):
    depth = w_in.shape[0]
    s = x.shape[1]
    c_me = lax.axis_index("c")
    q_me = 2 * lax.axis_index("x") + lax.axis_index("y")
    dev = 2 * q_me + c_me

    tr = lambda a: jnp.swapaxes(a, 1, 2)
    w_in_t, w_gu_t = tr(w_in), tr(w_gate_up)

    def gather_layer(l, deps):
        first = gather_start(f"gather_a{l}_s", [w_in_t[l].astype(BF16), w_out[l].astype(BF16), conv_w[l]], deps)
        return first, gather_start(f"gather_b{l}_s", [w_gu_t[l].astype(BF16), w_down[l].astype(BF16)], [first["token"]])

    def weights_a(relayed, l, after):
        g_in, g_out, g_cw = gather_finish(f"gather_a{l}_f", relayed, after)
        return (assemble_rows("assemble_in", *g_in), assemble_rows("assemble_out", *g_out),
                assemble_cols("assemble_conv", *g_cw))

    def weights_b(relayed, l, after):
        g_gu, g_dn = gather_finish(f"gather_b{l}_f", relayed, after)
        return assemble_rows("assemble_gu", *g_gu), assemble_rows("assemble_down", *g_dn)

    h = x.reshape(s, D_MODEL)
    started = {0: gather_layer(0, [])}
    xn = rmsnorm_fwd("norm_mix", h, pre_mix_g[0].reshape(1, -1), BF16)
    relay_a = gather_relay("gather_a0_r", started[0][0], started[0][1]["token"])
    wa = weights_a(relay_a, 0, xn)
    saved = []
    for l in range(depth):
        vec = lambda p: p[l].reshape(1, -1)
        wt_in, wf_out, cw_full = wa
        ahead = [started[l][1]["token"]]
        if l + 1 < depth:
            started[l + 1] = gather_layer(l + 1, ahead)
            ahead = [started[l + 1][1]["token"]]
        proj = matmul("proj", xn, wt_in, "nt", BF16, s, 5 * D_ATTN, D_MODEL, 1024, 1280, D_MODEL, deps=ahead)
        relay_b = gather_relay(f"gather_b{l}_r", started[l][1], proj)
        bd = _block_diag(pool_w[l]).astype(BF16)
        bias = attention_bias(rel_bias[l])
        gg = vec(group_gain)
        y_ab = conv_pool_fwd("conv_pool_fwd", proj, cw_full, bd, vec(pool_scale), gg[:, :2 * D_CONV],
                             deps=[relay_b["token"]])
        y, o, lse = attention_fwd("attention_fwd", proj, bias, gg[:, 2 * D_CONV:], y_ab)
        wt_gu, wf_dn = weights_b(relay_b, l, y)
        if l + 1 < depth:
            relay_a = gather_relay(f"gather_a{l + 1}_r", started[l + 1][0], y)
        mix, h_mid, hn = matmul_norm("mix_out", y, wf_out, h, vec(post_mix_g), then=(vec(pre_ffn_g), BF16),
                                     deps=[relay_a["token"]] if l + 1 < depth else [])
        g, u, ff = matmul_swiglu("gate_up", hn, wt_gu)
        if l + 1 < depth:
            wa = weights_a(relay_a, l + 1, ff)
            ffo, h_out, xn_next = matmul_norm("ffn_down", ff, wf_dn, h_mid, vec(post_ffn_g),
                                              then=(pre_mix_g[l + 1].reshape(1, -1), BF16))
        else:
            ffo, h_out = matmul_norm("ffn_down_last", ff, wf_dn, h_mid, vec(post_ffn_g))
            xn_next = None
        saved.append((h, xn, proj, None, bd, bias, y, o, lse, mix, h_mid, hn, g, u, ff, ffo,
                      wt_in, wf_out, cw_full, wt_gu, wf_dn))
        h, xn = h_out, xn_next

    dh, loss_part = loss_head("loss_head", h, loss_target.reshape(s, D_MODEL))

    def reduce_begin(name, halves):
        return [k for k, _ in halves], exchange_start(name, plan_swap_cores, len(halves), [snd for _, snd in halves],
                                                      [(k.shape, k.dtype) for k, _ in halves])

    def reduce_relay(name, kept, swapped, after):
        _, got = exchange_wait(name + "_w", swapped, after)
        sums = [add_pairs("reduce_add", k, r) for k, r in zip(kept, got)]
        return exchange_start(name + "_s", plan_to_chips, 3 * len(sums), sums, [(a.shape, a.dtype) for a in sums])

    def reduce_finish(name, relayed, after):
        sums, got = exchange_wait(name, relayed, after)
        return list(zip(sums, got))

    small = {k: [None] * depth for k in _SMALL + ("conv_w",)}
    relayed = {}
    carried = []
    dffo, small["post_ffn_g"][depth - 1] = rmsnorm_bwd("norm_bwd_top", saved[-1][15], post_ffn_g[depth - 1].reshape(1, -1),
                                                       dh, None, BF16)
    for l in reversed(range(depth)):
        vec = lambda p: p[l].reshape(1, -1)
        (h_in, xn, proj, _, bd, bias, y, o, lse, mix, h_mid, hn, g, u, ff, ffo,
         wt_in, wf_out, cw_full, wt_gu, wf_dn) = saved[l]
        gg = vec(group_gain)
        gw_dn = matmul("wgrad_down", ff, dffo, "tn", BF16, D_FF, D_MODEL, s, 1408, D_MODEL, 1024, deps=carried)
        dg, du = matmul_swiglu_bwd("dgrad_down", dffo, wf_dn, g, u)
        gw_gu = matmul("wgrad_gu", [dg, du], hn, "tn", BF16, 2 * D_FF, D_MODEL, s, 1408, D_MODEL, 1024)
        kept_b, swap_b = reduce_begin(f"reduce_b{l}_s", [split_rows("split_gu", gw_gu), split_rows("split_down", gw_dn)])
        dh_mid, dmix, small["pre_ffn_g"][l], small["post_mix_g"][l] = matmul_norm_bwd(
            "dgrad_gu", [dg, du], wt_gu, 256, h_mid, vec(pre_ffn_g), dh, then=(mix, vec(post_mix_g), BF16),
            deps=[swap_b["token"]])
        relayed[l, "b"] = reduce_relay(f"reduce_b{l}_r", kept_b, swap_b, dmix)
        gw_out = matmul("wgrad_out", y, dmix, "tn", BF16, D_MODEL, D_MODEL, s, D_MODEL, D_MODEL, 1024,
                        deps=[relayed[l, "b"]["token"]])
        dy = matmul("dgrad_out", dmix, wf_out, "nt", BF16, s, D_MODEL, D_MODEL, 1024, 512, D_MODEL)
        dpa, dcw, dbd, dps, dgg_ab = conv_pool_bwd("conv_pool_bwd", proj, dy, cw_full, bd, vec(pool_scale),
                                                   gg[:, :2 * D_CONV])
        dq, dk, dv, by_off, dgg_c = attention_bwd("attention_bwd", proj, o, lse, dy, bias, gg[:, 2 * D_CONV:])
        dparts = [dpa, dq, dk, dv]
        gw_in = matmul("wgrad_in", dparts, xn, "tn", BF16, 5 * D_ATTN, D_MODEL, s, 512, D_MODEL, min(s, 2048))
        kept_a, swap_a = reduce_begin(f"reduce_a{l}_s", [split_rows("split_in", gw_in), split_rows("split_out", gw_out)])
        if l > 0:
            dh, dffo, small["pre_mix_g"][l], small["post_ffn_g"][l - 1] = matmul_norm_bwd(
                "dgrad_in", dparts, wt_in, 512, h_in, vec(pre_mix_g), dh_mid,
                then=(saved[l - 1][15], post_ffn_g[l - 1].reshape(1, -1), BF16), deps=[swap_a["token"]])
        else:
            dh, small["pre_mix_g"][l] = matmul_norm_bwd("dgrad_in_first", dparts, wt_in, 512, h_in, vec(pre_mix_g),
                                                        dh_mid, deps=[swap_a["token"]])
        relayed[l, "a"] = reduce_relay(f"reduce_a{l}_r", kept_a, swap_a, dh)
        carried = [relayed[l, "a"]["token"]]
        small["conv_w"][l] = dcw
        small["pool_w"][l] = jnp.stack([dbd[gi * POOL_GC:(gi + 1) * POOL_GC, gi * POOL_GC:(gi + 1) * POOL_GC]
                                        for gi in range(N_POOL)])
        small["pool_scale"][l] = dps
        small["rel_bias"][l] = rel_bias_grad(by_off)
        small["group_gain"][l] = jnp.concatenate([dgg_ab, dgg_c], axis=1)
    grad_x = dh.reshape(x.shape)

    small_params = dict(pool_w=pool_w, pool_scale=pool_scale, rel_bias=rel_bias, group_gain=group_gain,
                        pre_mix_g=pre_mix_g, post_mix_g=post_mix_g, pre_ffn_g=pre_ffn_g, post_ffn_g=post_ffn_g)
    small_m = dict(pool_w=m_pool_w, pool_scale=m_pool_scale, rel_bias=m_rel_bias, group_gain=m_group_gain,
                   pre_mix_g=m_pre_mix_g, post_mix_g=m_post_mix_g, pre_ffn_g=m_pre_ffn_g, post_ffn_g=m_post_ffn_g)
    small_v = dict(pool_w=v_pool_w, pool_scale=v_pool_scale, rel_bias=v_rel_bias, group_gain=v_group_gain,
                   pre_mix_g=v_pre_mix_g, post_mix_g=v_post_mix_g, pre_ffn_g=v_pre_ffn_g, post_ffn_g=v_post_ffn_g)
    shapes = [small_params[k].shape for k in _SMALL] + [(depth, 3, D_CONV), (1,)]
    n_small = sum(int(np.prod(shp)) for shp in shapes)
    rows = -(-n_small // (8 * D_MODEL)) * 8
    extra = [jnp.zeros((depth, 3, D_CONV), F32), jnp.zeros((1,), F32)]
    grads_packed = _pack([jnp.stack(small[k]) for k in _SMALL] + [jnp.stack(small["conv_w"]), loss_part[0, 0:1]], rows)
    small_started = gather_start("gather_small_s", [grads_packed], carried)

    big = dict(w_in=(w_in_t, tr(m_w_in), tr(v_w_in)), w_out=(w_out, m_w_out, v_w_out),
               w_gate_up=(w_gu_t, tr(m_w_gate_up), tr(v_w_gate_up)), w_down=(w_down, m_w_down, v_w_down))
    results = dict(w_in=None, w_out=None, w_gate_up=None, w_down=None)
    after = small_started["token"]
    small_relay = None
    for l in reversed(range(depth)):
        slabs_gu, slabs_dn = reduce_finish(f"reduce_b{l}_f", relayed[l, "b"], after)
        slabs_in, slabs_out = reduce_finish(f"reduce_a{l}_f", relayed[l, "a"], slabs_dn[1])
        for k, slabs in (("w_gate_up", slabs_gu), ("w_down", slabs_dn), ("w_in", slabs_in), ("w_out", slabs_out)):
            results[k] = adamw_layer("adamw_" + k, l, *big[k], *slabs, results[k])
        after = results["w_out"][0]
        if small_relay is None:
            small_relay = gather_relay("gather_small_r", small_started, after)
    small_parts = gather_finish("gather_small_f", small_relay, after)[0]
    all_small = assemble_rows("assemble_small", *small_parts).reshape(8, rows, D_MODEL)
    res_small = adamw("adamw_small", _pack([small_params[k] for k in _SMALL] + extra, rows),
                      _pack([small_m[k] for k in _SMALL] + extra, rows),
                      _pack([small_v[k] for k in _SMALL] + extra, rows), all_small)
    g_s, d_s, m_s, v_s = (_unpack(r, shapes) for r in res_small)
    loss = g_s[-1][0]
    g_conv = lax.dynamic_slice_in_dim(g_s[-2], dev * conv_w.shape[2], conv_w.shape[2], axis=2)
    results["conv_w"] = adamw("adamw_conv", conv_w, m_conv_w, v_conv_w, g_conv[None])

    names = ("w_in", "w_out", "conv_w") + _SMALL + ("w_gate_up", "w_down")
    for k in ("w_in", "w_gate_up"):
        results[k] = tuple(tr(r) for r in results[k])
    for j, k in enumerate(_SMALL):
        results[k] = (g_s[j], d_s[j], m_s[j], v_s[j])
    return (loss, grad_x, *[results[k][0] for k in names], *[results[k][1] for k in names],
            *[results[k][2] for k in names], *[results[k][3] for k in names])
```

```python
import numpy as np
import jax
import jax.numpy as jnp
from jax import lax
from jax.experimental import pallas as pl
from jax.experimental.pallas import tpu as pltpu

F32 = jnp.float32
BF16 = jnp.bfloat16

CHUNK = 64
D_MODEL = 1024
D_CONV = 256
D_POOL = 256
D_ATTN = 512
HEAD_DIM = 64
N_HEADS = 8
N_POOL = 4
POOL_GC = 64
POOL_WINDOWS = (2, 4, 8, 16)
LEFT_CHUNKS = 8
REL_CLIP = 128
D_FF = 2816
EPS = 1e-6
ADAM_LR, ADAM_B1, ADAM_B2, ADAM_EPS, ADAM_WD, ADAM_STEP = 0.001, 0.9, 0.999, 1e-08, 0.01, 10

QB = 256
KB = 3 * QB
HALO = 16
T_CP = 512
T_ROW = 512
QKV_COL = 2
NEG = -1e30
VMEM_MB = 1 << 20
MESH = pl.DeviceIdType.MESH


def _call(body, **kw):
    call = pl.pallas_call(body, **kw)
    return lambda *args: call(*[_in_hbm(a) for a in args])


def _in_hbm(a):
    return pltpu.with_memory_space_constraint(a, pltpu.HBM) if jnp.issubdtype(a.dtype, jnp.number) else a


def _call_after(deps, n_in, body, **kw):
    deps = tuple(deps)
    if not deps:
        return _call(body, **kw)

    def ordered(*refs):
        body(*refs[:n_in], *refs[n_in + len(deps):])

    kw["in_specs"] = list(kw["in_specs"]) + [pl.BlockSpec(memory_space=pl.ANY)] * len(deps)
    call = _call(ordered, **kw)
    return lambda *args: call(*args, *deps)


def _params(sem, vmem_mb=48):
    return pltpu.CompilerParams(dimension_semantics=sem, vmem_limit_bytes=vmem_mb * VMEM_MB)


_CHIP_FLIPS = ((1, 0), (0, 1), (1, 1))
_HBM = pl.BlockSpec(memory_space=pltpu.HBM)
_SEM = pl.BlockSpec(memory_space=pltpu.SEMAPHORE)
_EFFECT = pltpu.SideEffectType.DATAFLOW_SIDE_EFFECTING


def _flip(v, f):
    return 1 - v if f else v


def _descriptors(plan, srcs, lands, send_sems, recv_sems, sending):
    x, y, c = lax.axis_index("x"), lax.axis_index("y"), lax.axis_index("c")
    return [pltpu.make_async_remote_copy(src_ref=src, dst_ref=dst if sending else land, send_sem=send_sems.at[k],
                                         recv_sem=recv_sems.at[k], device_id=peer, device_id_type=MESH)
            for k, (src, dst, peer, land) in enumerate(plan(srcs, lands, x, y, c))]


def exchange_start(name, plan, n_copies, srcs, land_shapes, deps=()):
    ns, nl = len(srcs), len(land_shapes)

    def body(*refs):
        src_refs, land_refs = refs[:ns], refs[ns:ns + nl]
        send_sems, recv_sems = refs[ns + nl], refs[ns + nl + 1]
        for send in _descriptors(plan, src_refs, land_refs, send_sems, recv_sems, True):
            send.start()
        refs[-1][...] = jnp.zeros_like(refs[-1])

    lands = [lax.empty(shape, dtype) for shape, dtype in land_shapes]
    outs = _call_after(
        deps, ns + nl, body, name=name,
        out_shape=(pltpu.SemaphoreType.DMA((n_copies,)), pltpu.SemaphoreType.DMA((n_copies,)),
                   *[pltpu.HBM(a.shape, a.dtype) for a in srcs], *[pltpu.HBM(shape, dtype) for shape, dtype in land_shapes],
                   jax.ShapeDtypeStruct((8, 128), F32)),
        in_specs=[_HBM] * (ns + nl),
        out_specs=(_SEM, _SEM, *[_HBM] * (ns + nl), pl.BlockSpec(memory_space=pltpu.VMEM)),
        input_output_aliases={j: 2 + j for j in range(ns + nl)},
        compiler_params=pltpu.CompilerParams(has_side_effects=_EFFECT),
    )(*srcs, *lands)
    return dict(plan=plan, sems=outs[:2], srcs=outs[2:2 + ns], lands=outs[2 + ns:2 + ns + nl], token=outs[-1])


def exchange_wait(name, started, after):
    srcs, lands = started["srcs"], started["lands"]
    ns, nl = len(srcs), len(lands)

    def body(*refs):
        src_refs, land_refs = refs[:ns], refs[ns:ns + nl]
        send_sems, recv_sems = refs[ns + nl], refs[ns + nl + 1]
        for wait in _descriptors(started["plan"], src_refs, land_refs, send_sems, recv_sems, False):
            wait.wait_send()
            wait.wait_recv()

    outs = _call(
        body, name=name,
        out_shape=tuple(pltpu.HBM(a.shape, a.dtype) for a in (*srcs, *lands)),
        in_specs=[_HBM] * (ns + nl) + [_SEM, _SEM, pl.BlockSpec(memory_space=pl.ANY)],
        out_specs=tuple([_HBM] * (ns + nl)),
        input_output_aliases={j: j for j in range(ns + nl)},
        compiler_params=pltpu.CompilerParams(has_side_effects=_EFFECT),
    )(*srcs, *lands, *started["sems"], after)
    return list(outs[:ns]), list(outs[ns:])


def plan_from_chips(srcs, lands, x, y, c):
    q = 2 * x + y
    out = []
    for src, land in zip(srcs, lands):
        for fx, fy in _CHIP_FLIPS:
            px, py = _flip(x, fx), _flip(y, fy)
            out.append((src, land.at[q], (px, py, c), land.at[2 * px + py]))
    return out


def plan_to_core(srcs, lands, x, y, c):
    n = len(lands)
    q = 2 * x + y
    out = []
    for own, chips, land in zip(srcs[:n], srcs[n:], lands):
        out.append((own, land.at[q], (x, y, 1 - c), land.at[q]))
        for fx, fy in _CHIP_FLIPS:
            qp = 2 * _flip(x, fx) + _flip(y, fy)
            out.append((chips.at[qp], land.at[qp], (x, y, 1 - c), land.at[qp]))
    return out


def plan_swap_cores(srcs, lands, x, y, c):
    return [(src, land, (x, y, 1 - c), land) for src, land in zip(srcs, lands)]


def plan_to_chips(srcs, lands, x, y, c):
    q = 2 * x + y
    out = []
    for src, land in zip(srcs, lands):
        for fx, fy in _CHIP_FLIPS:
            px, py = _flip(x, fx), _flip(y, fy)
            out.append((src.at[2 * px + py], land.at[q], (px, py, c), land.at[2 * px + py]))
    return out


def _slots(a):
    return ((4,) + a.shape, a.dtype)


def gather_start(name, arrays, deps=()):
    return exchange_start(name, plan_from_chips, 3 * len(arrays), arrays, [_slots(a) for a in arrays], deps)


def gather_relay(name, started, after):
    own, chips = exchange_wait(name + "_w", started, after)
    return exchange_start(name + "_s", plan_to_core, 4 * len(own), own + chips, [_slots(a) for a in own])


def gather_finish(name, relayed, after):
    srcs, cores = exchange_wait(name, relayed, after)
    n = len(cores)
    return list(zip(srcs[:n], srcs[n:], cores))


def _place():
    return 2 * lax.axis_index("x") + lax.axis_index("y"), lax.axis_index("c")


def assemble_cols(name, own, chips, core):
    r, c = own.shape
    t = _row_tile(r, 256)

    def body(own_ref, chips_ref, core_ref, o_ref):
        q_me, c_me = _place()
        for q in range(4):
            mine = jnp.where(q == q_me, own_ref[...], chips_ref[q])
            other = core_ref[q]
            o_ref[:, 2 * q * c:(2 * q + 1) * c] = jnp.where(c_me == 0, mine, other)
            o_ref[:, (2 * q + 1) * c:(2 * q + 2) * c] = jnp.where(c_me == 0, other, mine)

    slots = pl.BlockSpec((4, t, c), lambda i: (0, i, 0))
    return _call(
        body, name=name, grid=(r // t,),
        in_specs=[pl.BlockSpec((t, c), lambda i: (i, 0)), slots, slots],
        out_specs=pl.BlockSpec((t, 8 * c), lambda i: (i, 0)),
        out_shape=jax.ShapeDtypeStruct((r, 8 * c), own.dtype), compiler_params=_params(("parallel",)),
    )(own, chips, core)


def assemble_rows(name, own, chips, core):
    r, c = own.shape

    def body(own_ref, chips_ref, core_ref, o_ref):
        q_me, c_me = _place()
        d = pl.program_id(0)
        mine = jnp.where(d // 2 == q_me, own_ref[...], chips_ref[...])
        o_ref[...] = jnp.where(d % 2 == c_me, mine, core_ref[...])

    slot = pl.BlockSpec((None, r, c), lambda d: (d // 2, 0, 0))
    return _call(
        body, name=name, grid=(8,),
        in_specs=[pl.BlockSpec((r, c), lambda d: (0, 0)), slot, slot],
        out_specs=pl.BlockSpec((r, c), lambda d: (d, 0)),
        out_shape=jax.ShapeDtypeStruct((8 * r, c), own.dtype), compiler_params=_params(("parallel",)),
    )(own, chips, core)


def split_rows(name, dw):
    r8, c = dw.shape
    r = r8 // 8

    def body(dw_ref, keep_ref, send_ref):
        _, c_me = _place()
        d = pl.program_id(0)

        @pl.when(d % 2 == c_me)
        def _():
            keep_ref[...] = dw_ref[...]

        @pl.when(d % 2 != c_me)
        def _():
            send_ref[...] = dw_ref[...]

    slot = pl.BlockSpec((None, r, c), lambda d: (d // 2, 0, 0))
    return _call(
        body, name=name, grid=(8,),
        in_specs=[pl.BlockSpec((r, c), lambda d: (d, 0))], out_specs=[slot, slot],
        out_shape=[jax.ShapeDtypeStruct((4, r, c), dw.dtype)] * 2, compiler_params=_params(("arbitrary",)),
    )(dw)


def _rms(v):
    return lax.rsqrt(jnp.mean(v * v, axis=-1, keepdims=True) + EPS)


def rmsnorm_fwd(name, x, g, out_dtype):
    s, d = x.shape

    def body(x_ref, g_ref, o_ref):
        xv = x_ref[...]
        o_ref[...] = ((xv * _rms(xv)) * g_ref[...]).astype(o_ref.dtype)

    row = pl.BlockSpec((T_ROW, d), lambda i: (i, 0))
    return _call(
        body, name=name, grid=(s // T_ROW,),
        in_specs=[row, pl.BlockSpec((1, d), lambda i: (0, 0))], out_specs=row,
        out_shape=jax.ShapeDtypeStruct((s, d), out_dtype), compiler_params=_params(("parallel",)),
    )(x, g)


def matmul_then(name, a_parts, b, mode, tm, chunk, rows_in, vecs_in, rows_out, n_sums, then, deps=()):
    a_parts = list(a_parts)
    na, nr, nv, no = len(a_parts), len(rows_in), len(vecs_in), len(rows_out)
    s = a_parts[0].shape[0]
    n = s // tm
    d = b.shape[1] if mode == "nn" else b.shape[0]
    offs = [0]
    for p in a_parts:
        offs.append(offs[-1] + p.shape[1])
    n_in = na + 1 + nr + nv

    def body(*refs):
        a_refs, b_ref = refs[:na], refs[na]
        row_refs, vec_refs = refs[na + 1:na + 1 + nr], refs[na + 1 + nr:n_in]
        out_refs, sum_refs = refs[n_in:n_in + no], refs[n_in + no:n_in + no + n_sums]
        acc = refs[-2:]
        i = pl.program_id(0)

        @pl.when(i == 0)
        def _():
            acc[1][...] = jnp.zeros_like(acc[1])
            for s_ref in sum_refs:
                s_ref[...] = jnp.zeros_like(s_ref)

        def step(write, read):
            t = None
            for p in range(na):
                lo, hi = offs[p], offs[p + 1]
                part = _dot(a_refs[p][...], b_ref[lo:hi, :] if mode == "nn" else b_ref[:, lo:hi], mode)
                t = part if t is None else t + part
            write[...] = t
            totals = [0.0] * n_sums
            for r0 in range(0, tm, chunk):
                rs = pl.ds(r0, chunk)
                results, sums = then(read[rs, :], [r.at[rs, :] for r in row_refs], vec_refs)
                for o_ref, val in zip(out_refs, results):
                    o_ref[rs, :] = val.astype(o_ref.dtype)
                totals = [tot + val for tot, val in zip(totals, sums)]
            for s_ref, val in zip(sum_refs, totals):
                s_ref[...] += jnp.where(i > 0, val, 0.0)

        pl.when(i % 2 == 0)(lambda: step(acc[0], acc[1]))
        pl.when(i % 2 == 1)(lambda: step(acc[1], acc[0]))

    ahead = lambda i: (jnp.minimum(i, n - 1), 0)
    behind = lambda i: (jnp.maximum(i - 1, 0), 0)
    fixed = lambda i: (0, 0)
    shapes = [jax.ShapeDtypeStruct((s, d), dt) for dt in rows_out] + [jax.ShapeDtypeStruct((1, d), F32)] * n_sums
    return _call_after(
        deps, n_in, body, name=name, grid=(n + 1,),
        in_specs=[pl.BlockSpec((tm, p.shape[1]), ahead) for p in a_parts] + [pl.BlockSpec(b.shape, fixed)]
        + [pl.BlockSpec((tm, r.shape[1]), behind) for r in rows_in] + [pl.BlockSpec(v.shape, fixed) for v in vecs_in],
        out_specs=[pl.BlockSpec((tm, d), behind)] * no + [pl.BlockSpec((1, d), fixed)] * n_sums,
        out_shape=shapes, scratch_shapes=[pltpu.VMEM((tm, d), F32)] * 2,
        compiler_params=_params(("arbitrary",), 56),
    )(*a_parts, b, *rows_in, *vecs_in)


def matmul_norm(name, a, b, res, g, then=None, deps=()):
    def norm(z, rows, vecs):
        y = rows[0][...] + (z * _rms(z)) * vecs[0][...]
        return [z, y] + ([(y * _rms(y)) * vecs[1][...]] if then else []), []

    return matmul_then(name, [a], b, "nn", 512, 512, [res], [g] + ([then[0]] if then else []),
                       [BF16, F32] + ([then[1]] if then else []), 0, norm, deps)


def matmul_norm_bwd(name, a_parts, b, tm, x, g, res, then=None, chunk=None, deps=()):
    def norms(t, rows, vecs):
        xv = rows[0][...].astype(F32)
        dx, dgt = _norm_bwd(xv, _rms(xv), vecs[0][...], t)
        dx = dx + rows[1][...]
        results, sums = [dx], [jnp.sum(dgt, axis=0, keepdims=True)]
        if then:
            x2 = rows[2][...].astype(F32)
            dx2, dgt2 = _norm_bwd(x2, _rms(x2), vecs[1][...], dx)
            results.append(dx2)
            sums.append(jnp.sum(dgt2, axis=0, keepdims=True))
        return results, sums

    return matmul_then(name, a_parts, b, "nn", tm, chunk or tm, [x, res] + ([then[0]] if then else []),
                       [g] + ([then[1]] if then else []), [F32] + ([then[2]] if then else []),
                       2 if then else 1, norms, deps)


def _norm_bwd(xv, r, g, dy):
    a = dy * g
    dx = r * (a - xv * ((r * r) * jnp.mean(a * xv, axis=-1, keepdims=True)))
    return dx, dy * (xv * r)


def rmsnorm_bwd(name, x, g, dy, res, out_dtype, then=None, deps=()):
    s, d = x.shape
    has_res = res is not None
    n_in = 3 + has_res + (2 if then else 0)

    def body(*refs):
        x_ref, g_ref, dy_ref = refs[:3]
        outs = refs[n_in:]
        xv = x_ref[...].astype(F32)
        dx, dgt = _norm_bwd(xv, _rms(xv), g_ref[...], dy_ref[...].astype(F32))
        if has_res:
            dx = dx + refs[3][...]
        outs[0][...] = dx.astype(outs[0].dtype)
        sums = [(outs[1], dgt)]
        if then is not None:
            x2 = refs[n_in - 2][...].astype(F32)
            dx2, dgt2 = _norm_bwd(x2, _rms(x2), refs[n_in - 1][...], dx)
            outs[2][...] = dx2.astype(outs[2].dtype)
            sums.append((outs[3], dgt2))

        @pl.when(pl.program_id(0) == 0)
        def _():
            for dg_ref, _ in sums:
                dg_ref[...] = jnp.zeros_like(dg_ref)
        for dg_ref, terms in sums:
            dg_ref[...] += jnp.sum(terms, axis=0, keepdims=True)

    row = pl.BlockSpec((T_ROW, d), lambda i: (i, 0))
    vec = pl.BlockSpec((1, d), lambda i: (0, 0))
    args = (x, g, dy) + ((res,) if has_res else ()) + ((then[0], then[1]) if then else ())
    shapes = [jax.ShapeDtypeStruct((s, d), out_dtype), jax.ShapeDtypeStruct((1, d), F32)]
    if then:
        shapes += [jax.ShapeDtypeStruct((s, d), then[2]), jax.ShapeDtypeStruct((1, d), F32)]
    return _call_after(
        deps, len(args), body, name=name, grid=(s // T_ROW,),
        in_specs=[row, vec, row] + ([row] if has_res else []) + ([row, vec] if then else []),
        out_specs=[row, vec] * (2 if then else 1), out_shape=shapes,
        compiler_params=_params(("arbitrary",)),
    )(*args)


def loss_head(name, h, tgt):
    s, d = h.shape

    def body(h_ref, t_ref, dh_ref, l_ref):
        e = h_ref[...] - t_ref[...]
        dh_ref[...] = e * (1.0 / d)

        @pl.when(pl.program_id(0) == 0)
        def _():
            l_ref[...] = jnp.zeros_like(l_ref)
        part = 0.5 * jnp.sum(jnp.mean(e * e, axis=-1, keepdims=True), axis=0, keepdims=True)
        l_ref[...] += jnp.broadcast_to(part, l_ref.shape)

    row = pl.BlockSpec((T_ROW, d), lambda i: (i, 0))
    return _call(
        body, name=name, grid=(s // T_ROW,),
        in_specs=[row, row], out_specs=[row, pl.BlockSpec((1, 128), lambda i: (0, 0))],
        out_shape=[jax.ShapeDtypeStruct((s, d), F32), jax.ShapeDtypeStruct((1, 128), F32)],
        compiler_params=_params(("arbitrary",)),
    )(h, tgt)


def _row_tile(rows, limit=512):
    t = min(rows, limit)
    while rows % t or (t % 8 and t != rows):
        t -= 1
    return t


def add_pairs(name, a, b):
    shape = a.shape
    a2, b2 = a.reshape(-1, shape[-1]), b.reshape(-1, shape[-1])
    rows, cols = a2.shape
    t = _row_tile(rows)

    def body(a_ref, b_ref, o_ref):
        o_ref[...] = (a_ref[...].astype(F32) + b_ref[...].astype(F32)).astype(o_ref.dtype)

    blk = pl.BlockSpec((t, cols), lambda i: (i, 0))
    out = _call(
        body, name=name, grid=(rows // t,), in_specs=[blk, blk], out_specs=blk,
        out_shape=jax.ShapeDtypeStruct((rows, cols), a.dtype), compiler_params=_params(("parallel",)),
    )(a2, b2)
    return out.reshape(shape)


def _adamw_math(w, m, v, g):
    c1 = 1.0 - ADAM_B1 ** ADAM_STEP
    c2 = 1.0 - ADAM_B2 ** ADAM_STEP
    mn = ADAM_B1 * m + (1.0 - ADAM_B1) * g
    vn = ADAM_B2 * v + (1.0 - ADAM_B2) * (g * g)
    delta = -ADAM_LR * ((mn / c1) / (jnp.sqrt(vn / c2) + ADAM_EPS) + ADAM_WD * w)
    return delta, mn, vn


def _slab_sum(g_ref, n):
    g = g_ref[0].astype(F32)
    for j in range(1, n):
        g = g + g_ref[j].astype(F32)
    return g


def adamw(name, w, m, v, g_slabs, deps=()):
    shape = w.shape
    n = g_slabs.shape[0]
    w2, m2, v2 = (t.reshape(-1, shape[-1]) for t in (w, m, v))
    g3 = g_slabs.reshape(n, -1, shape[-1])
    rows, cols = w2.shape
    t = _row_tile(rows, 256)

    def body(w_ref, m_ref, v_ref, g_ref, go_ref, d_ref, mo_ref, vo_ref):
        g = _slab_sum(g_ref, n)
        go_ref[...] = g
        d_ref[...], mo_ref[...], vo_ref[...] = _adamw_math(w_ref[...], m_ref[...], v_ref[...], g)

    blk = pl.BlockSpec((t, cols), lambda i: (i, 0))
    outs = _call_after(
        deps, 4, body, name=name, grid=(rows // t,),
        in_specs=[blk, blk, blk, pl.BlockSpec((n, t, cols), lambda i: (0, i, 0))], out_specs=[blk] * 4,
        out_shape=[jax.ShapeDtypeStruct((rows, cols), F32)] * 4, compiler_params=_params(("parallel",)),
    )(w2, m2, v2, g3)
    return tuple(o.reshape(shape) for o in outs)


def adamw_layer(name, l, w, m, v, own_sums, chip_sums, into):
    _, rows, cols = w.shape
    t = _row_tile(rows, 256)
    if into is None:
        into = tuple(lax.empty(w.shape, F32) for _ in range(4))

    def body(w_ref, m_ref, v_ref, own_ref, far_ref, a0, a1, a2, a3, go_ref, d_ref, mo_ref, vo_ref):
        del a0, a1, a2, a3
        q_me, _ = _place()
        g = jnp.zeros((t, cols), F32)
        for q in range(4):
            g = g + jnp.where(q == q_me, own_ref[q], far_ref[q]).astype(F32)
        go_ref[...] = g
        d_ref[...], mo_ref[...], vo_ref[...] = _adamw_math(w_ref[...], m_ref[...], v_ref[...], g)

    blk = pl.BlockSpec((None, t, cols), lambda i: (l, i, 0))
    slabs = pl.BlockSpec((4, t, cols), lambda i: (0, i, 0))
    anyw = pl.BlockSpec(memory_space=pl.ANY)
    return _call(
        body, name=name, grid=(rows // t,),
        in_specs=[blk, blk, blk, slabs, slabs] + [anyw] * 4, out_specs=[blk] * 4,
        out_shape=[jax.ShapeDtypeStruct(w.shape, F32)] * 4, input_output_aliases={5: 0, 6: 1, 7: 2, 8: 3},
        compiler_params=_params(("parallel",)),
    )(w, m, v, own_sums, chip_sums, *into)


_DIMS = {"nn": (((1,), (0,)), ((), ())), "nt": (((1,), (1,)), ((), ())), "tn": (((0,), (0,)), ((), ()))}


def _dot(a, b, mode="nn"):
    return lax.dot_general(a, b, _DIMS[mode], preferred_element_type=F32)


def matmul(name, a, b, mode, out_dtype, m, n, k, tm, tn, tk, b_off=(0, 0), deps=()):
    gm, gn, gk = m // tm, n // tn, k // tk
    assert gm * tm == m and gn * tn == n and gk * tk == k
    r0, c0 = b_off
    a_parts = list(a) if isinstance(a, (list, tuple)) else [a]
    na = len(a_parts)
    tile = tm if mode == "tn" else tk
    spans, at = [], 0
    for p in a_parts:
        nblk = p.shape[1] // tile
        assert nblk * tile == p.shape[1]
        spans.append((at, nblk))
        at += nblk
    assert at == (gm if mode == "tn" else gk)

    def within(t, span):
        return (t >= span[0]) & (t < span[0] + span[1])

    def local(t, span):
        return jnp.clip(t - span[0], 0, span[1] - 1)

    a_specs = []
    for sp in spans:
        if mode == "tn":
            a_specs.append(pl.BlockSpec((tk, tm), lambda i, j, kk, sp=sp: (jnp.where(within(i, sp), kk, 0), local(i, sp))))
        else:
            a_specs.append(pl.BlockSpec((tm, tk), lambda i, j, kk, sp=sp: (i, local(kk, sp))))
    if mode == "nt":
        b_spec = pl.BlockSpec((tn, tk), lambda i, j, kk: (j + r0, kk + c0))
    else:
        b_spec = pl.BlockSpec((tk, tn), lambda i, j, kk: (kk + r0, j + c0))
    o_spec = pl.BlockSpec((tm, tn), lambda i, j, kk: (i, j))
    single = na == 1 and gk == 1

    def body(*refs):
        a_refs, b_ref, o_ref = refs[:na], refs[na], refs[na + 1]
        if single:
            o_ref[...] = _dot(a_refs[0][...], b_ref[...], mode).astype(o_ref.dtype)
            return
        acc_ref = refs[-1]
        i, kk = pl.program_id(0), pl.program_id(2)

        @pl.when(kk == 0)
        def _():
            acc_ref[...] = jnp.zeros_like(acc_ref)

        for pa, sp in enumerate(spans):
            def add(pa=pa):
                acc_ref[...] += _dot(a_refs[pa][...], b_ref[...], mode)
            if na > 1:
                pl.when(within(i if mode == "tn" else kk, sp))(add)
            else:
                add()

        @pl.when(kk == gk - 1)
        def _():
            o_ref[...] = acc_ref[...].astype(o_ref.dtype)

    return _call_after(
        deps, na + 1, body, name=name, grid=(gm, gn, gk),
        in_specs=a_specs + [b_spec], out_specs=o_spec,
        out_shape=jax.ShapeDtypeStruct((m, n), out_dtype),
        scratch_shapes=[] if single else [pltpu.VMEM((tm, tn), F32)],
        compiler_params=_params(("parallel", "parallel", "arbitrary")),
    )(*a_parts, b)


def matmul_swiglu(name, a, w_gu_t, deps=()):
    s, d = a.shape
    tm, tn = 512, 1408
    gn = D_FF // tn

    def body(a_ref, wg_ref, wu_ref, g_ref, u_ref, f_ref):
        av = a_ref[...]
        g = _dot(av, wg_ref[...], "nt")
        u = _dot(av, wu_ref[...], "nt")
        g_ref[...] = g.astype(g_ref.dtype)
        u_ref[...] = u.astype(u_ref.dtype)
        f_ref[...] = ((g * (1.0 / (1.0 + jnp.exp(-g)))) * u).astype(f_ref.dtype)

    o_spec = pl.BlockSpec((tm, tn), lambda j, i: (i, j))
    return _call_after(
        deps, 3, body, name=name, grid=(gn, s // tm),
        in_specs=[pl.BlockSpec((tm, d), lambda j, i: (i, 0)), pl.BlockSpec((tn, d), lambda j, i: (j, 0)),
                  pl.BlockSpec((tn, d), lambda j, i: (j + gn, 0))],
        out_specs=[o_spec] * 3, out_shape=[jax.ShapeDtypeStruct((s, D_FF), BF16)] * 3,
        compiler_params=_params(("parallel", "parallel")),
    )(a, w_gu_t, w_gu_t)


def matmul_swiglu_bwd(name, dffo, w_down, g, u):
    s, d = dffo.shape
    tm, tn = 512, 1408

    def body(a_ref, w_ref, g_ref, u_ref, dg_ref, du_ref):
        av = a_ref[...]
        for lo in range(0, tn, 256):
            cols = slice(lo, min(lo + 256, tn))
            dff = _dot(av, w_ref[cols, :], "nt")
            gv = g_ref[:, cols].astype(F32)
            sig = 1.0 / (1.0 + jnp.exp(-gv))
            silu = gv * sig
            du_ref[:, cols] = (dff * silu).astype(du_ref.dtype)
            dg_ref[:, cols] = ((dff * u_ref[:, cols].astype(F32)) * (sig + silu * (1.0 - sig))).astype(dg_ref.dtype)

    o_spec = pl.BlockSpec((tm, tn), lambda j, i: (i, j))
    return _call(
        body, name=name, grid=(D_FF // tn, s // tm),
        in_specs=[pl.BlockSpec((tm, d), lambda j, i: (i, 0)), pl.BlockSpec((tn, d), lambda j, i: (j, 0)), o_spec, o_spec],
        out_specs=[o_spec] * 2, out_shape=[jax.ShapeDtypeStruct((s, D_FF), BF16)] * 2,
        compiler_params=_params(("parallel", "parallel")),
    )(dffo, w_down, g, u)


def _down(v, n):
    return pltpu.roll(v, n, 0)


def _up(v, n):
    return pltpu.roll(v, v.shape[0] - n, 0)


def _by_window(lane, v2, v4, v8, v16):
    return jnp.where(lane < POOL_GC, v2, jnp.where(lane < 2 * POOL_GC, v4, jnp.where(lane < 3 * POOL_GC, v8, v16)))


def _taps(cw_ref):
    return cw_ref[0:1, :], cw_ref[1:2, :], cw_ref[2:3, :]


def _conv_pool_forward(ext, t0, cw, bd):
    n_out = ext.shape[0] - HALO
    gb = ext[HALO:, 0:D_CONV]
    z = ext[:, D_CONV:2 * D_CONV] * ext[:, 2 * D_CONV:3 * D_CONV]
    z0, z1, z2 = z[HALO:], _down(z, 1)[HALO:], _down(z, 2)[HALO:]
    conv = cw[2] * z0 + cw[1] * z1 + cw[0] * z2
    x = ext[:, 3 * D_CONV:]
    w2 = x + _down(x, 1)
    w4 = w2 + _down(w2, 2)
    w8 = w4 + _down(w4, 4)
    w16 = w8 + _down(w8, 8)
    lane = lax.broadcasted_iota(jnp.int32, (1, D_POOL), 1)
    win = _by_window(lane, 2.0, 4.0, 8.0, 16.0)
    pos = (t0 + lax.broadcasted_iota(jnp.int32, (n_out, 1), 0) + 1).astype(F32)
    cnt = jnp.minimum(pos, win)
    d = _by_window(lane, w2, w4, w8, w16)[HALO:] / cnt - x[HALO:]
    ybp = _dot(d.astype(BF16), bd)
    return gb, z0, z1, z2, conv, d, cnt, ybp, lane


def conv_pool_fwd(name, proj_a, cw, bd, ps, gg, deps=()):
    s = proj_a.shape[0]
    t = T_CP
    hb = t // HALO

    def body(main_ref, prev_ref, cw_ref, bd_ref, ps_ref, gg_ref, y_ref):
        i = pl.program_id(0)
        prev = jnp.where(i > 0, prev_ref[...].astype(F32), 0.0)
        ext = jnp.concatenate([prev, main_ref[...].astype(F32)], axis=0)
        gb, _, _, _, conv, _, _, ybp, _ = _conv_pool_forward(ext, i * t, _taps(cw_ref), bd_ref[...])
        ya = gb * conv
        yb = ybp * ps_ref[...]
        ggv = gg_ref[...]
        y_ref[:, 0:D_CONV] = ((ya * _rms(ya)) * ggv[:, 0:D_CONV]).astype(y_ref.dtype)
        y_ref[:, D_CONV:] = ((yb * _rms(yb)) * ggv[:, D_CONV:]).astype(y_ref.dtype)

    full = lambda shape: pl.BlockSpec(shape, lambda i: (0,) * len(shape))
    return _call_after(
        deps, 6, body, name=name, grid=(s // t,),
        in_specs=[pl.BlockSpec((t, D_MODEL), lambda i: (i, 0)),
                  pl.BlockSpec((HALO, D_MODEL), lambda i: (jnp.maximum(i * hb - 1, 0), 0)),
                  full((3, D_CONV)), full((D_POOL, D_POOL)), full((1, D_POOL)), full((1, 2 * D_CONV))],
        out_specs=pl.BlockSpec((t, 2 * D_CONV), lambda i: (i, 0)),
        out_shape=jax.ShapeDtypeStruct((s, D_MODEL), BF16),
        compiler_params=_params(("parallel",)),
    )(proj_a, proj_a, cw, bd, ps, gg)


def conv_pool_bwd(name, proj_a, dy, cw, bd, ps, gg):
    s = proj_a.shape[0]
    t = T_CP
    hb = t // HALO
    nblk = s // t
    last_halo = s // HALO - 1

    def body(main_ref, prev_ref, next_ref, dy_ref, dyn_ref, cw_ref, bd_ref, ps_ref, gg_ref,
             dp_ref, dcw_ref, dbd_ref, dps_ref, dgg_ref):
        i = pl.program_id(0)
        prev = jnp.where(i > 0, prev_ref[...].astype(F32), 0.0)
        main = main_ref[...].astype(F32)
        ext = jnp.concatenate([prev, main, next_ref[...].astype(F32)], axis=0)
        cwv, bdv, psv, ggv = _taps(cw_ref), bd_ref[...], ps_ref[...], gg_ref[...]
        gb, z0, z1, z2, conv, d, cnt, ybp, lane = _conv_pool_forward(ext, i * t, cwv, bdv)
        dyn = jnp.where(i < nblk - 1, dyn_ref[...].astype(F32), 0.0)
        dyv = jnp.concatenate([dy_ref[...].astype(F32), dyn], axis=0)
        ya = gb * conv
        yb = ybp * psv
        dya, dgg_a = _norm_bwd(ya, _rms(ya), ggv[:, 0:D_CONV], dyv[:, 0:D_CONV])
        dyb, dgg_b = _norm_bwd(yb, _rms(yb), ggv[:, D_CONV:], dyv[:, D_CONV:])

        dconv = dya * gb
        dz = (cwv[2] * dconv + cwv[1] * _up(dconv, 1) + cwv[0] * _up(dconv, 2))[:t]
        dp_ref[:, 0:D_CONV] = (dya * conv)[:t].astype(dp_ref.dtype)
        dp_ref[:, D_CONV:2 * D_CONV] = (dz * main[:, 2 * D_CONV:3 * D_CONV]).astype(dp_ref.dtype)
        dp_ref[:, 2 * D_CONV:3 * D_CONV] = (dz * main[:, D_CONV:2 * D_CONV]).astype(dp_ref.dtype)

        dybs = dyb * psv
        dd = _dot(dybs.astype(BF16), bdv, "nt")
        e = dd / cnt
        a2 = e + _up(e, 1)
        a4 = a2 + _up(a2, 2)
        a8 = a4 + _up(a4, 4)
        a16 = a8 + _up(a8, 8)
        dp_ref[:, 3 * D_CONV:] = (_by_window(lane, a2, a4, a8, a16) - dd)[:t].astype(dp_ref.dtype)

        @pl.when(i == 0)
        def _():
            dcw_ref[...] = jnp.zeros_like(dcw_ref)
            dbd_ref[...] = jnp.zeros_like(dbd_ref)
            dps_ref[...] = jnp.zeros_like(dps_ref)
            dgg_ref[...] = jnp.zeros_like(dgg_ref)

        rsum = lambda v: jnp.sum(v[:t], axis=0, keepdims=True)
        dcw_ref[0:1, :] += rsum(dconv * z2)
        dcw_ref[1:2, :] += rsum(dconv * z1)
        dcw_ref[2:3, :] += rsum(dconv * z0)
        dbd_ref[...] += _dot(d[:t].astype(BF16), dybs[:t].astype(BF16), "tn")
        dps_ref[...] += rsum(dyb * ybp)
        dgg_ref[:, 0:D_CONV] += rsum(dgg_a)
        dgg_ref[:, D_CONV:] += rsum(dgg_b)

    full = lambda shape: pl.BlockSpec(shape, lambda i: (0,) * len(shape))
    next_halo = lambda i: (jnp.minimum((i + 1) * hb, last_halo), 0)
    return _call(
        body, name=name, grid=(nblk,),
        in_specs=[pl.BlockSpec((t, D_MODEL), lambda i: (i, 0)),
                  pl.BlockSpec((HALO, D_MODEL), lambda i: (jnp.maximum(i * hb - 1, 0), 0)),
                  pl.BlockSpec((HALO, D_MODEL), next_halo),
                  pl.BlockSpec((t, 2 * D_CONV), lambda i: (i, 0)),
                  pl.BlockSpec((HALO, 2 * D_CONV), next_halo),
                  full((3, D_CONV)), full((D_POOL, D_POOL)), full((1, D_POOL)), full((1, 2 * D_CONV))],
        out_specs=[pl.BlockSpec((t, D_MODEL), lambda i: (i, 0)),
                   full((3, D_CONV)), full((D_POOL, D_POOL)), full((1, D_POOL)), full((1, 2 * D_CONV))],
        out_shape=[jax.ShapeDtypeStruct((s, D_MODEL), BF16), jax.ShapeDtypeStruct((3, D_CONV), F32),
                   jax.ShapeDtypeStruct((D_POOL, D_POOL), F32), jax.ShapeDtypeStruct((1, D_POOL), F32),
                   jax.ShapeDtypeStruct((1, 2 * D_CONV), F32)],
        compiler_params=_params(("arbitrary",)),
    )(proj_a, proj_a, proj_a, dy, dy, cw, bd, ps, gg)


def bias_by_offset(rel_bias):
    n_far = 2 * QB - REL_CLIP + 1
    far = jnp.broadcast_to(rel_bias[:, 2 * REL_CLIP:], (N_HEADS, n_far))
    mid = rel_bias[:, 1:2 * REL_CLIP][:, ::-1]
    near = jnp.broadcast_to(rel_bias[:, 0:1], (N_HEADS, KB - n_far - (2 * REL_CLIP - 1)))
    wrap = jnp.broadcast_to(rel_bias[:, 2 * REL_CLIP:], (N_HEADS, 4 * QB - KB))
    return jnp.concatenate([far, mid, near, wrap], axis=1)


def _fill_bias(off_ref, b_ref):
    width = 4 * QB
    sub = lax.broadcasted_iota(jnp.int32, (8, 1), 0)
    col = lax.broadcasted_iota(jnp.int32, (1, KB), 1)
    for h in range(N_HEADS):
        base = jnp.broadcast_to(off_ref[h:h + 1, :], (8, width))
        for bit in range(3):
            base = jnp.where(((sub >> bit) & 1) == 1, pltpu.roll(base, 1 << bit, 1), base)
        for a in range(QB // 8):
            first = CHUNK * (8 * a // CHUNK)
            rows = (pltpu.roll(base, 8 * a, 1) if a else base)[:, :KB]
            band = (col >= first) & (col < first + (LEFT_CHUNKS + 1) * CHUNK)
            b_ref[h, 8 * a:8 * a + 8, :] = jnp.where(band, rows, NEG)


def rel_bias_grad(by_offset):
    n_far = 2 * QB - REL_CLIP + 1
    hi = jnp.sum(by_offset[:, :n_far], axis=1, keepdims=True) + jnp.sum(by_offset[:, KB:], axis=1, keepdims=True)
    mid = by_offset[:, n_far:n_far + 2 * REL_CLIP - 1][:, ::-1]
    lo = jnp.sum(by_offset[:, n_far + 2 * REL_CLIP - 1:KB], axis=1, keepdims=True)
    return jnp.concatenate([lo, mid, hi], axis=1)


def _head_masks():
    first = lax.broadcasted_iota(jnp.int32, (1, 2 * HEAD_DIM), 1) < HEAD_DIM
    return first, jnp.logical_not(first)


def _pick_lane(tile, h):
    lane = lax.broadcasted_iota(jnp.int32, (1, tile.shape[1]), 1)
    return jnp.sum(jnp.where(lane == h, tile, 0.0), axis=-1, keepdims=True)


def attention_fwd(name, qkv, by_offset, gg, y_ab):
    s = qkv.shape[0]
    nq = s // QB
    scale = HEAD_DIM ** -0.5

    def body(q_ref, k0, k1, k2, v0, v1, v2, off_ref, gg_ref, y_in, y_ref, o_ref, lse_ref, b_ref):
        del y_in
        i = pl.program_id(0)

        @pl.when(i == 0)
        def _():
            _fill_bias(off_ref, b_ref)
        def block(at_start):
            kb = jnp.concatenate([k0[...], k1[...], k2[...]], axis=0)
            vb = jnp.concatenate([v0[...], v1[...], v2[...]], axis=0)
            valid = lax.broadcasted_iota(jnp.int32, (1, KB), 1) >= (2 - i) * QB
            lane = lax.broadcasted_iota(jnp.int32, (1, 128), 1)
            masks = _head_masks()
            lse = jnp.zeros((QB, 128), F32)
            outs = []
            for hp in range(N_HEADS // 2):
                sl = slice(2 * HEAD_DIM * hp, 2 * HEAD_DIM * (hp + 1))
                q_p, k_p, v_p = q_ref[:, sl] * scale, kb[:, sl], vb[:, sl]
                o_pair = jnp.zeros((QB, 2 * HEAD_DIM), F32)
                for a in range(2):
                    h = 2 * hp + a
                    sc = _dot(jnp.where(masks[a], q_p, 0), k_p, "nt") + b_ref[h]
                    if at_start:
                        sc = jnp.where(valid, sc, NEG)
                    mx = jnp.max(sc, axis=-1, keepdims=True)
                    e = jnp.exp(sc - mx)
                    l = jnp.sum(e, axis=-1, keepdims=True)
                    o_pair = o_pair + _dot(e.astype(BF16), jnp.where(masks[a], v_p, 0)) * (1.0 / l)
                    lse = jnp.where(lane == h, mx + jnp.log(l), lse)
                outs.append(o_pair)
            o = jnp.concatenate(outs, axis=1)
            o_ref[...] = o.astype(o_ref.dtype)
            lse_ref[...] = lse
            y_ref[...] = ((o * _rms(o)) * gg_ref[...]).astype(y_ref.dtype)

        pl.when(i < 2)(lambda: block(True))
        pl.when(i >= 2)(lambda: block(False))

    blk = lambda col, back: pl.BlockSpec((QB, D_ATTN), lambda i: (jnp.maximum(i - back, 0), QKV_COL + col))
    return _call(
        body, name=name, grid=(nq,),
        in_specs=[blk(0, 0), blk(1, 2), blk(1, 1), blk(1, 0), blk(2, 2), blk(2, 1), blk(2, 0),
                  pl.BlockSpec((N_HEADS, 4 * QB), lambda i: (0, 0)), pl.BlockSpec((1, D_ATTN), lambda i: (0, 0)),
                  pl.BlockSpec(memory_space=pl.ANY)],
        out_specs=[pl.BlockSpec((QB, D_ATTN), lambda i: (i, 1)), pl.BlockSpec((QB, D_ATTN), lambda i: (i, 0)),
                   pl.BlockSpec((QB, 128), lambda i: (i, 0)), pl.BlockSpec((N_HEADS, QB, KB), lambda i: (0, 0, 0))],
        out_shape=[jax.ShapeDtypeStruct((s, D_MODEL), BF16), jax.ShapeDtypeStruct((s, D_ATTN), BF16),
                   jax.ShapeDtypeStruct((s, 128), F32), jax.ShapeDtypeStruct((N_HEADS, QB, KB), F32)],
        input_output_aliases={9: 0},
        compiler_params=_params(("arbitrary",), 56),
    )(qkv, qkv, qkv, qkv, qkv, qkv, qkv, by_offset, gg, y_ab)


def attention_bwd(name, qkv, o, lse, dy, bias, gg):
    s = qkv.shape[0]
    nq = s // QB
    scale = HEAD_DIM ** -0.5
    width = 4 * QB

    def body(q_ref, k0, k1, k2, v0, v1, v2, o_ref, lse_ref, dy_ref, b_ref, gg_ref,
             dq_ref, dk_ref, dv_ref, off_ref, dgg_ref, dk_acc, dv_acc, db_acc):
        i = pl.program_id(0)

        @pl.when(i == 0)
        def _():
            dk_acc[...] = jnp.zeros_like(dk_acc)
            dv_acc[...] = jnp.zeros_like(dv_acc)
            db_acc[...] = jnp.zeros_like(db_acc)
            dgg_ref[...] = jnp.zeros_like(dgg_ref)

        @pl.when(i > 0)
        def _():
            for acc in (dk_acc, dv_acc):
                kept = acc[QB:, :]
                acc[0:2 * QB, :] = kept
                acc[2 * QB:, :] = jnp.zeros((QB, D_ATTN), F32)

        def block(at_start):
            ov = o_ref[...].astype(F32)
            dyv = dy_ref[...].astype(F32)
            do, dgg_t = _norm_bwd(ov, _rms(ov), gg_ref[...], dyv)
            dgg_ref[...] += jnp.sum(dgg_t, axis=0, keepdims=True)
            kb = jnp.concatenate([k0[...], k1[...], k2[...]], axis=0)
            vb = jnp.concatenate([v0[...], v1[...], v2[...]], axis=0)
            valid = lax.broadcasted_iota(jnp.int32, (1, KB), 1) >= (2 - i) * QB
            masks = _head_masks()
            lse_t = lse_ref[...]
            for hp in range(N_HEADS // 2):
                sl = slice(2 * HEAD_DIM * hp, 2 * HEAD_DIM * (hp + 1))
                q_p, k_p, v_p = q_ref[:, sl] * scale, kb[:, sl], vb[:, sl]
                do_p = do[:, sl]
                prod = do_p * ov[:, sl]
                do_b = do_p.astype(BF16)
                dq_pair = jnp.zeros((QB, 2 * HEAD_DIM), F32)
                dk_pair = jnp.zeros((KB, 2 * HEAD_DIM), F32)
                dv_pair = jnp.zeros((KB, 2 * HEAD_DIM), F32)
                for a in range(2):
                    h = 2 * hp + a
                    q_m = jnp.where(masks[a], q_p, 0)
                    do_m = jnp.where(masks[a], do_b, 0)
                    sc = _dot(q_m, k_p, "nt") + b_ref[h]
                    if at_start:
                        sc = jnp.where(valid, sc, NEG)
                    p = jnp.exp(sc - _pick_lane(lse_t, h))
                    dp = _dot(do_m, v_p, "nt")
                    delta = jnp.sum(jnp.where(masks[a], prod, 0.0), axis=-1, keepdims=True)
                    ds = p * (dp - delta)
                    db_acc[h] += ds
                    ds_b = ds.astype(BF16)
                    dq_pair = dq_pair + _dot(ds_b, jnp.where(masks[a], k_p, 0))
                    dk_pair = dk_pair + _dot(ds_b, q_m, "tn")
                    dv_pair = dv_pair + _dot(p.astype(BF16), do_m, "tn")
                dq_ref[:, sl] = (dq_pair * scale).astype(dq_ref.dtype)
                dk_acc[:, sl] += dk_pair
                dv_acc[:, sl] += dv_pair

        pl.when(i < 2)(lambda: block(True))
        pl.when((i >= 2) & (i < nq))(lambda: block(False))

        dk_ref[...] = dk_acc[0:QB, :].astype(dk_ref.dtype)
        dv_ref[...] = dv_acc[0:QB, :].astype(dv_ref.dtype)

        @pl.when(i == nq + 1)
        def _():
            sub = lax.broadcasted_iota(jnp.int32, (8, 1), 0)
            pad = jnp.zeros((8, width - KB), F32)
            for h in range(N_HEADS):
                v = jnp.concatenate([db_acc[h, 0:8, :], pad], axis=1)
                for a in range(1, QB // 8):
                    grp = jnp.concatenate([db_acc[h, 8 * a:8 * a + 8, :], pad], axis=1)
                    v = v + pltpu.roll(grp, width - 8 * a, 1)
                for bit in range(3):
                    v = jnp.where(((sub >> bit) & 1) == 1, pltpu.roll(v, width - (1 << bit), 1), v)
                off_ref[h:h + 1, :] = jnp.sum(v, axis=0, keepdims=True)

    qi = lambda i: jnp.minimum(i, nq - 1)
    kblk = lambda col, back: pl.BlockSpec((QB, D_ATTN), lambda i: (jnp.clip(i - back, 0, nq - 1), QKV_COL + col))
    qblk = lambda col: pl.BlockSpec((QB, D_ATTN), lambda i: (qi(i), col))
    done = pl.BlockSpec((QB, D_ATTN), lambda i: (jnp.clip(i - 2, 0, nq - 1), 0))
    return _call(
        body, name=name, grid=(nq + 2,),
        in_specs=[qblk(QKV_COL), kblk(1, 2), kblk(1, 1), kblk(1, 0), kblk(2, 2), kblk(2, 1), kblk(2, 0),
                  qblk(0), pl.BlockSpec((QB, 128), lambda i: (qi(i), 0)), qblk(1),
                  pl.BlockSpec((N_HEADS, QB, KB), lambda i: (0, 0, 0)), pl.BlockSpec((1, D_ATTN), lambda i: (0, 0))],
        out_specs=[qblk(0), done, done, pl.BlockSpec((N_HEADS, width), lambda i: (0, 0)),
                   pl.BlockSpec((1, D_ATTN), lambda i: (0, 0))],
        out_shape=[jax.ShapeDtypeStruct((s, D_ATTN), BF16)] * 3
        + [jax.ShapeDtypeStruct((N_HEADS, width), F32), jax.ShapeDtypeStruct((1, D_ATTN), F32)],
        scratch_shapes=[pltpu.VMEM((KB, D_ATTN), F32), pltpu.VMEM((KB, D_ATTN), F32), pltpu.VMEM((N_HEADS, QB, KB), F32)],
        compiler_params=_params(("arbitrary",), 56),
    )(qkv, qkv, qkv, qkv, qkv, qkv, qkv, o, lse, dy, bias, gg)


def _block_diag(pw):
    out = jnp.zeros((D_POOL, D_POOL), pw.dtype)
    for gi in range(N_POOL):
        out = lax.dynamic_update_slice(out, pw[gi], (gi * POOL_GC, gi * POOL_GC))
    return out


_SMALL = ("pool_w", "pool_scale", "rel_bias", "group_gain", "pre_mix_g", "post_mix_g", "pre_ffn_g", "post_ffn_g")


def _pack(parts, rows):
    flat = jnp.concatenate([p.reshape(-1).astype(F32) for p in parts])
    return jnp.pad(flat, (0, rows * D_MODEL - flat.shape[0])).reshape(rows, D_MODEL)


def _unpack(packed, shapes):
    flat = packed.reshape(-1)
    out, at = [], 0
    for shp in shapes:
        size = int(np.prod(shp))
        out.append(flat[at:at + size].reshape(shp))
        at += size
    return out


def kernel(x, w_in, w_out, conv_w, pool_w, pool_scale, rel_bias, group_gain, pre_mix_g, post_mix_g, pre_ffn_g, post_ffn_g, w_gate_up, w_down, loss_target, m_w_in, m_w_out, m_conv_w, m_pool_w, m_pool_scale, m_rel_bias, m_group_gain, m_pre_mix_g, m_post_mix_g, m_pre_ffn_g, m_post_ffn_g, m_w_gate_up, m_w_down, v_w_in, v_w_out, v_conv_w, v_pool_w, v_pool_scale, v_rel_bias, v_group_gain, v_pre_mix_g, v_post_mix_g, v_pre_ffn_g, v_post_ffn_g, v_w_gate_up, v_w_down):
    depth = w_in.shape[0]
    s = x.shape[1]
    c_me = lax.axis_index("c")
    q_me = 2 * lax.axis_index("x") + lax.axis_index("y")
    dev = 2 * q_me + c_me

    tr = lambda a: jnp.swapaxes(a, 1, 2)
    w_in_t, w_gu_t = tr(w_in), tr(w_gate_up)

    def gather_layer(l, deps):
        first = gather_start(f"gather_a{l}_s", [w_in_t[l].astype(BF16), w_out[l].astype(BF16), conv_w[l]], deps)
        return first, gather_start(f"gather_b{l}_s", [w_gu_t[l].astype(BF16), w_down[l].astype(BF16)], [first["token"]])

    def weights_a(relayed, l, after):
        g_in, g_out, g_cw = gather_finish(f"gather_a{l}_f", relayed, after)
        return (assemble_rows("assemble_in", *g_in), assemble_rows("assemble_out", *g_out),
                assemble_cols("assemble_conv", *g_cw))

    def weights_b(relayed, l, after):
        g_gu, g_dn = gather_finish(f"gather_b{l}_f", relayed, after)
        return assemble_rows("assemble_gu", *g_gu), assemble_rows("assemble_down", *g_dn)

    h = x.reshape(s, D_MODEL)
    started = {0: gather_layer(0, [])}
    xn = rmsnorm_fwd("norm_mix", h, pre_mix_g[0].reshape(1, -1), BF16)
    relay_a = gather_relay("gather_a0_r", started[0][0], started[0][1]["token"])
    wa = weights_a(relay_a, 0, xn)
    saved = []
    for l in range(depth):
        vec = lambda p: p[l].reshape(1, -1)
        wt_in, wf_out, cw_full = wa
        ahead = [started[l][1]["token"]]
        if l + 1 < depth:
            started[l + 1] = gather_layer(l + 1, ahead)
            ahead = [started[l + 1][1]["token"]]
        proj = matmul("proj", xn, wt_in, "nt", BF16, s, 5 * D_ATTN, D_MODEL, 512, 5 * D_ATTN, D_MODEL, deps=ahead)
        relay_b = gather_relay(f"gather_b{l}_r", started[l][1], proj)
        bd = _block_diag(pool_w[l]).astype(BF16)
        gg = vec(group_gain)
        y_ab = conv_pool_fwd("conv_pool_fwd", proj, cw_full, bd, vec(pool_scale), gg[:, :2 * D_CONV],
                             deps=[relay_b["token"]])
        y, o, lse, bias = attention_fwd("attention_fwd", proj, bias_by_offset(rel_bias[l]), gg[:, 2 * D_CONV:], y_ab)
        wt_gu, wf_dn = weights_b(relay_b, l, y)
        if l + 1 < depth:
            relay_a = gather_relay(f"gather_a{l + 1}_r", started[l + 1][0], y)
        mix, h_mid, hn = matmul_norm("mix_out", y, wf_out, h, vec(post_mix_g), then=(vec(pre_ffn_g), BF16),
                                     deps=[relay_a["token"]] if l + 1 < depth else [])
        g, u, ff = matmul_swiglu("gate_up", hn, wt_gu)
        if l + 1 < depth:
            wa = weights_a(relay_a, l + 1, ff)
            ffo, h_out, xn_next = matmul_norm("ffn_down", ff, wf_dn, h_mid, vec(post_ffn_g),
                                              then=(pre_mix_g[l + 1].reshape(1, -1), BF16))
        else:
            ffo, h_out = matmul_norm("ffn_down_last", ff, wf_dn, h_mid, vec(post_ffn_g))
            xn_next = None
        saved.append((h, xn, proj, None, bd, bias, y, o, lse, mix, h_mid, hn, g, u, ff, ffo,
                      wt_in, wf_out, cw_full, wt_gu, wf_dn))
        h, xn = h_out, xn_next

    dh, loss_part = loss_head("loss_head", h, loss_target.reshape(s, D_MODEL))

    def reduce_begin(name, halves):
        return [k for k, _ in halves], exchange_start(name, plan_swap_cores, len(halves), [snd for _, snd in halves],
                                                      [(k.shape, k.dtype) for k, _ in halves])

    def reduce_relay(name, kept, swapped, after):
        _, got = exchange_wait(name + "_w", swapped, after)
        sums = [add_pairs("reduce_add", k, r) for k, r in zip(kept, got)]
        return exchange_start(name + "_s", plan_to_chips, 3 * len(sums), sums, [(a.shape, a.dtype) for a in sums])

    def reduce_finish(name, relayed, after):
        sums, got = exchange_wait(name, relayed, after)
        return list(zip(sums, got))

    small = {k: [None] * depth for k in _SMALL + ("conv_w",)}
    relayed = {}
    carried = []
    dffo, small["post_ffn_g"][depth - 1] = rmsnorm_bwd("norm_bwd_top", saved[-1][15], post_ffn_g[depth - 1].reshape(1, -1),
                                                       dh, None, BF16)
    for l in reversed(range(depth)):
        vec = lambda p: p[l].reshape(1, -1)
        (h_in, xn, proj, _, bd, bias, y, o, lse, mix, h_mid, hn, g, u, ff, ffo,
         wt_in, wf_out, cw_full, wt_gu, wf_dn) = saved[l]
        gg = vec(group_gain)
        gw_dn = matmul("wgrad_down", ff, dffo, "tn", BF16, D_FF, D_MODEL, s, 1408, D_MODEL, 1024, deps=carried)
        dg, du = matmul_swiglu_bwd("dgrad_down", dffo, wf_dn, g, u)
        gw_gu = matmul("wgrad_gu", [dg, du], hn, "tn", BF16, 2 * D_FF, D_MODEL, s, 1408, D_MODEL, 1024)
        kept_b, swap_b = reduce_begin(f"reduce_b{l}_s", [split_rows("split_gu", gw_gu), split_rows("split_down", gw_dn)])
        dh_mid, dmix, small["pre_ffn_g"][l], small["post_mix_g"][l] = matmul_norm_bwd(
            "dgrad_gu", [dg, du], wt_gu, 256, h_mid, vec(pre_ffn_g), dh, then=(mix, vec(post_mix_g), BF16),
            deps=[swap_b["token"]])
        relayed[l, "b"] = reduce_relay(f"reduce_b{l}_r", kept_b, swap_b, dmix)
        gw_out = matmul("wgrad_out", y, dmix, "tn", BF16, D_MODEL, D_MODEL, s, D_MODEL, D_MODEL, 1024,
                        deps=[relayed[l, "b"]["token"]])
        dy = matmul("dgrad_out", dmix, wf_out, "nt", BF16, s, D_MODEL, D_MODEL, 512, D_MODEL, D_MODEL)
        dpa, dcw, dbd, dps, dgg_ab = conv_pool_bwd("conv_pool_bwd", proj, dy, cw_full, bd, vec(pool_scale),
                                                   gg[:, :2 * D_CONV])
        dq, dk, dv, by_off, dgg_c = attention_bwd("attention_bwd", proj, o, lse, dy, bias, gg[:, 2 * D_CONV:])
        dparts = [dpa, dq, dk, dv]
        gw_in = matmul("wgrad_in", dparts, xn, "tn", BF16, 5 * D_ATTN, D_MODEL, s, 512, D_MODEL, min(s, 2048))
        kept_a, swap_a = reduce_begin(f"reduce_a{l}_s", [split_rows("split_in", gw_in), split_rows("split_out", gw_out)])
        if l > 0:
            dh, dffo, small["pre_mix_g"][l], small["post_ffn_g"][l - 1] = matmul_norm_bwd(
                "dgrad_in", dparts, wt_in, 512, h_in, vec(pre_mix_g), dh_mid,
                then=(saved[l - 1][15], post_ffn_g[l - 1].reshape(1, -1), BF16), chunk=32, deps=[swap_a["token"]])
        else:
            dh, small["pre_mix_g"][l] = matmul_norm_bwd("dgrad_in_first", dparts, wt_in, 512, h_in, vec(pre_mix_g),
                                                        dh_mid, chunk=32, deps=[swap_a["token"]])
        relayed[l, "a"] = reduce_relay(f"reduce_a{l}_r", kept_a, swap_a, dh)
        carried = [relayed[l, "a"]["token"]]
        small["conv_w"][l] = dcw
        small["pool_w"][l] = jnp.stack([dbd[gi * POOL_GC:(gi + 1) * POOL_GC, gi * POOL_GC:(gi + 1) * POOL_GC]
                                        for gi in range(N_POOL)])
        small["pool_scale"][l] = dps
        small["rel_bias"][l] = rel_bias_grad(by_off)
        small["group_gain"][l] = jnp.concatenate([dgg_ab, dgg_c], axis=1)
    grad_x = dh.reshape(x.shape)

    small_params = dict(pool_w=pool_w, pool_scale=pool_scale, rel_bias=rel_bias, group_gain=group_gain,
                        pre_mix_g=pre_mix_g, post_mix_g=post_mix_g, pre_ffn_g=pre_ffn_g, post_ffn_g=post_ffn_g)
    small_m = dict(pool_w=m_pool_w, pool_scale=m_pool_scale, rel_bias=m_rel_bias, group_gain=m_group_gain,
                   pre_mix_g=m_pre_mix_g, post_mix_g=m_post_mix_g, pre_ffn_g=m_pre_ffn_g, post_ffn_g=m_post_ffn_g)
    small_v = dict(pool_w=v_pool_w, pool_scale=v_pool_scale, rel_bias=v_rel_bias, group_gain=v_group_gain,
                   pre_mix_g=v_pre_mix_g, post_mix_g=v_post_mix_g, pre_ffn_g=v_pre_ffn_g, post_ffn_g=v_post_ffn_g)
    shapes = [small_params[k].shape for k in _SMALL] + [(depth, 3, D_CONV), (1,)]
    n_small = sum(int(np.prod(shp)) for shp in shapes)
    rows = -(-n_small // (8 * D_MODEL)) * 8
    extra = [jnp.zeros((depth, 3, D_CONV), F32), jnp.zeros((1,), F32)]
    grads_packed = _pack([jnp.stack(small[k]) for k in _SMALL] + [jnp.stack(small["conv_w"]), loss_part[0, 0:1]], rows)
    small_started = gather_start("gather_small_s", [grads_packed], carried)

    big = dict(w_in=(w_in_t, tr(m_w_in), tr(v_w_in)), w_out=(w_out, m_w_out, v_w_out),
               w_gate_up=(w_gu_t, tr(m_w_gate_up), tr(v_w_gate_up)), w_down=(w_down, m_w_down, v_w_down))
    results = dict(w_in=None, w_out=None, w_gate_up=None, w_down=None)
    after = small_started["token"]
    small_relay = None
    for l in reversed(range(depth)):
        slabs_gu, slabs_dn = reduce_finish(f"reduce_b{l}_f", relayed[l, "b"], after)
        slabs_in, slabs_out = reduce_finish(f"reduce_a{l}_f", relayed[l, "a"], slabs_dn[1])
        for k, slabs in (("w_gate_up", slabs_gu), ("w_down", slabs_dn), ("w_in", slabs_in), ("w_out", slabs_out)):
            results[k] = adamw_layer("adamw_" + k, l, *big[k], *slabs, results[k])
        after = results["w_out"][0]
        if small_relay is None:
            small_relay = gather_relay("gather_small_r", small_started, after)
    small_parts = gather_finish("gather_small_f", small_relay, after)[0]
    all_small = assemble_rows("assemble_small", *small_parts).reshape(8, rows, D_MODEL)
    res_small = adamw("adamw_small", _pack([small_params[k] for k in _SMALL] + extra, rows),
                      _pack([small_m[k] for k in _SMALL] + extra, rows),
                      _pack([small_v[k] for k in _SMALL] + extra, rows), all_small)
    g_s, d_s, m_s, v_s = (_unpack(r, shapes) for r in res_small)
    loss = g_s[-1][0]
    g_conv = lax.dynamic_slice_in_dim(g_s[-2], dev * conv_w.shape[2], conv_w.shape[2], axis=2)
    results["conv_w"] = adamw("adamw_conv", conv_w, m_conv_w, v_conv_w, g_conv[None])

    names = ("w_in", "w_out", "conv_w") + _SMALL + ("w_gate_up", "w_down")
    for k in ("w_in", "w_gate_up"):
        results[k] = tuple(tr(r) for r in results[k])
    for j, k in enumerate(_SMALL):
        results[k] = (g_s[j], d_s[j], m_s[j], v_s[j])
    return (loss, grad_x, *[results[k][0] for k in names], *[results[k][1] for k in names],
            *[results[k][2] for k in names], *[results[k][3] for k in names])
```

```python
import numpy as np
import jax
import jax.numpy as jnp
from jax import lax
from jax.experimental import pallas as pl
from jax.experimental.pallas import tpu as pltpu

F32 = jnp.float32
BF16 = jnp.bfloat16

CHUNK = 64
D_MODEL = 1024
D_CONV = 256
D_POOL = 256
D_ATTN = 512
HEAD_DIM = 64
N_HEADS = 8
N_POOL = 4
POOL_GC = 64
POOL_WINDOWS = (2, 4, 8, 16)
LEFT_CHUNKS = 8
REL_CLIP = 128
D_FF = 2816
EPS = 1e-6
ADAM_LR, ADAM_B1, ADAM_B2, ADAM_EPS, ADAM_WD, ADAM_STEP = 0.001, 0.9, 0.999, 1e-08, 0.01, 10

QB = 256
KB = 3 * QB
HALO = 16
T_CP = 512
T_ROW = 512
QKV_COL = 2
NEG = -1e30
VMEM_MB = 1 << 20
MESH = pl.DeviceIdType.MESH


def _call(body, **kw):
    call = pl.pallas_call(body, **kw)
    return lambda *args: call(*[_in_hbm(a) for a in args])


def _in_hbm(a):
    return pltpu.with_memory_space_constraint(a, pltpu.HBM) if jnp.issubdtype(a.dtype, jnp.number) else a


def _call_after(deps, n_in, body, **kw):
    deps = tuple(deps)
    if not deps:
        return _call(body, **kw)

    def ordered(*refs):
        body(*refs[:n_in], *refs[n_in + len(deps):])

    kw["in_specs"] = list(kw["in_specs"]) + [pl.BlockSpec(memory_space=pl.ANY)] * len(deps)
    call = _call(ordered, **kw)
    return lambda *args: call(*args, *deps)


def _params(sem, vmem_mb=48):
    return pltpu.CompilerParams(dimension_semantics=sem, vmem_limit_bytes=vmem_mb * VMEM_MB)


_CHIP_FLIPS = ((1, 0), (0, 1), (1, 1))
_HBM = pl.BlockSpec(memory_space=pltpu.HBM)
_SEM = pl.BlockSpec(memory_space=pltpu.SEMAPHORE)
_EFFECT = pltpu.SideEffectType.DATAFLOW_SIDE_EFFECTING


def _flip(v, f):
    return 1 - v if f else v


def _descriptors(plan, srcs, lands, send_sems, recv_sems, sending):
    x, y, c = lax.axis_index("x"), lax.axis_index("y"), lax.axis_index("c")
    return [pltpu.make_async_remote_copy(src_ref=src, dst_ref=dst if sending else land, send_sem=send_sems.at[k],
                                         recv_sem=recv_sems.at[k], device_id=peer, device_id_type=MESH)
            for k, (src, dst, peer, land) in enumerate(plan(srcs, lands, x, y, c))]


def exchange_start(name, plan, n_copies, srcs, land_shapes, deps=()):
    ns, nl = len(srcs), len(land_shapes)

    def body(*refs):
        src_refs, land_refs = refs[:ns], refs[ns:ns + nl]
        send_sems, recv_sems = refs[ns + nl], refs[ns + nl + 1]
        for send in _descriptors(plan, src_refs, land_refs, send_sems, recv_sems, True):
            send.start()
        refs[-1][...] = jnp.zeros_like(refs[-1])

    lands = [lax.empty(shape, dtype) for shape, dtype in land_shapes]
    outs = _call_after(
        deps, ns + nl, body, name=name,
        out_shape=(pltpu.SemaphoreType.DMA((n_copies,)), pltpu.SemaphoreType.DMA((n_copies,)),
                   *[pltpu.HBM(a.shape, a.dtype) for a in srcs], *[pltpu.HBM(shape, dtype) for shape, dtype in land_shapes],
                   jax.ShapeDtypeStruct((8, 128), F32)),
        in_specs=[_HBM] * (ns + nl),
        out_specs=(_SEM, _SEM, *[_HBM] * (ns + nl), pl.BlockSpec(memory_space=pltpu.VMEM)),
        input_output_aliases={j: 2 + j for j in range(ns + nl)},
        compiler_params=pltpu.CompilerParams(has_side_effects=_EFFECT),
    )(*srcs, *lands)
    return dict(plan=plan, sems=outs[:2], srcs=outs[2:2 + ns], lands=outs[2 + ns:2 + ns + nl], token=outs[-1])


def exchange_wait(name, started, after):
    srcs, lands = started["srcs"], started["lands"]
    ns, nl = len(srcs), len(lands)

    def body(*refs):
        src_refs, land_refs = refs[:ns], refs[ns:ns + nl]
        send_sems, recv_sems = refs[ns + nl], refs[ns + nl + 1]
        for wait in _descriptors(started["plan"], src_refs, land_refs, send_sems, recv_sems, False):
            wait.wait_send()
            wait.wait_recv()

    outs = _call(
        body, name=name,
        out_shape=tuple(pltpu.HBM(a.shape, a.dtype) for a in (*srcs, *lands)),
        in_specs=[_HBM] * (ns + nl) + [_SEM, _SEM, pl.BlockSpec(memory_space=pl.ANY)],
        out_specs=tuple([_HBM] * (ns + nl)),
        input_output_aliases={j: j for j in range(ns + nl)},
        compiler_params=pltpu.CompilerParams(has_side_effects=_EFFECT),
    )(*srcs, *lands, *started["sems"], after)
    return list(outs[:ns]), list(outs[ns:])


def plan_from_chips(srcs, lands, x, y, c):
    q = 2 * x + y
    out = []
    for src, land in zip(srcs, lands):
        for fx, fy in _CHIP_FLIPS:
            px, py = _flip(x, fx), _flip(y, fy)
            out.append((src, land.at[q], (px, py, c), land.at[2 * px + py]))
    return out


def plan_to_core(srcs, lands, x, y, c):
    n = len(lands)
    q = 2 * x + y
    out = []
    for own, chips, land in zip(srcs[:n], srcs[n:], lands):
        out.append((own, land.at[q], (x, y, 1 - c), land.at[q]))
        for fx, fy in _CHIP_FLIPS:
            qp = 2 * _flip(x, fx) + _flip(y, fy)
            out.append((chips.at[qp], land.at[qp], (x, y, 1 - c), land.at[qp]))
    return out


def plan_swap_cores(srcs, lands, x, y, c):
    return [(src, land, (x, y, 1 - c), land) for src, land in zip(srcs, lands)]


def plan_to_chips(srcs, lands, x, y, c):
    q = 2 * x + y
    out = []
    for src, land in zip(srcs, lands):
        for fx, fy in _CHIP_FLIPS:
            px, py = _flip(x, fx), _flip(y, fy)
            out.append((src.at[2 * px + py], land.at[q], (px, py, c), land.at[2 * px + py]))
    return out


def _slots(a):
    return ((4,) + a.shape, a.dtype)


def gather_start(name, arrays, deps=()):
    return exchange_start(name, plan_from_chips, 3 * len(arrays), arrays, [_slots(a) for a in arrays], deps)


def gather_relay(name, started, after):
    own, chips = exchange_wait(name + "_w", started, after)
    return exchange_start(name + "_s", plan_to_core, 4 * len(own), own + chips, [_slots(a) for a in own])


def gather_finish(name, relayed, after):
    srcs, cores = exchange_wait(name, relayed, after)
    n = len(cores)
    return list(zip(srcs[:n], srcs[n:], cores))


def _place():
    return 2 * lax.axis_index("x") + lax.axis_index("y"), lax.axis_index("c")


def assemble_cols(name, own, chips, core):
    r, c = own.shape
    t = _row_tile(r, 256)

    def body(own_ref, chips_ref, core_ref, o_ref):
        q_me, c_me = _place()
        for q in range(4):
            mine = jnp.where(q == q_me, own_ref[...], chips_ref[q])
            other = core_ref[q]
            o_ref[:, 2 * q * c:(2 * q + 1) * c] = jnp.where(c_me == 0, mine, other)
            o_ref[:, (2 * q + 1) * c:(2 * q + 2) * c] = jnp.where(c_me == 0, other, mine)

    slots = pl.BlockSpec((4, t, c), lambda i: (0, i, 0))
    return _call(
        body, name=name, grid=(r // t,),
        in_specs=[pl.BlockSpec((t, c), lambda i: (i, 0)), slots, slots],
        out_specs=pl.BlockSpec((t, 8 * c), lambda i: (i, 0)),
        out_shape=jax.ShapeDtypeStruct((r, 8 * c), own.dtype), compiler_params=_params(("parallel",)),
    )(own, chips, core)


def assemble_rows(name, own, chips, core):
    r, c = own.shape

    def body(own_ref, chips_ref, core_ref, o_ref):
        q_me, c_me = _place()
        d = pl.program_id(0)
        mine = jnp.where(d // 2 == q_me, own_ref[...], chips_ref[...])
        o_ref[...] = jnp.where(d % 2 == c_me, mine, core_ref[...])

    slot = pl.BlockSpec((None, r, c), lambda d: (d // 2, 0, 0))
    return _call(
        body, name=name, grid=(8,),
        in_specs=[pl.BlockSpec((r, c), lambda d: (0, 0)), slot, slot],
        out_specs=pl.BlockSpec((r, c), lambda d: (d, 0)),
        out_shape=jax.ShapeDtypeStruct((8 * r, c), own.dtype), compiler_params=_params(("parallel",)),
    )(own, chips, core)


def split_rows(name, dw):
    r8, c = dw.shape
    r = r8 // 8

    def body(dw_ref, keep_ref, send_ref):
        _, c_me = _place()
        d = pl.program_id(0)

        @pl.when(d % 2 == c_me)
        def _():
            keep_ref[...] = dw_ref[...]

        @pl.when(d % 2 != c_me)
        def _():
            send_ref[...] = dw_ref[...]

    slot = pl.BlockSpec((None, r, c), lambda d: (d // 2, 0, 0))
    return _call(
        body, name=name, grid=(8,),
        in_specs=[pl.BlockSpec((r, c), lambda d: (d, 0))], out_specs=[slot, slot],
        out_shape=[jax.ShapeDtypeStruct((4, r, c), dw.dtype)] * 2, compiler_params=_params(("arbitrary",)),
    )(dw)


def _rms(v):
    return lax.rsqrt(jnp.mean(v * v, axis=-1, keepdims=True) + EPS)


def rmsnorm_fwd(name, x, g, out_dtype):
    s, d = x.shape

    def body(x_ref, g_ref, o_ref):
        xv = x_ref[...]
        o_ref[...] = ((xv * _rms(xv)) * g_ref[...]).astype(o_ref.dtype)

    row = pl.BlockSpec((T_ROW, d), lambda i: (i, 0))
    return _call(
        body, name=name, grid=(s // T_ROW,),
        in_specs=[row, pl.BlockSpec((1, d), lambda i: (0, 0))], out_specs=row,
        out_shape=jax.ShapeDtypeStruct((s, d), out_dtype), compiler_params=_params(("parallel",)),
    )(x, g)


def matmul_then(name, a_parts, b, mode, tm, chunk, rows_in, vecs_in, rows_out, n_sums, then, deps=()):
    a_parts = list(a_parts)
    na, nr, nv, no = len(a_parts), len(rows_in), len(vecs_in), len(rows_out)
    s = a_parts[0].shape[0]
    n = s // tm
    d = b.shape[1] if mode == "nn" else b.shape[0]
    offs = [0]
    for p in a_parts:
        offs.append(offs[-1] + p.shape[1])
    n_in = na + 1 + nr + nv

    def body(*refs):
        a_refs, b_ref = refs[:na], refs[na]
        row_refs, vec_refs = refs[na + 1:na + 1 + nr], refs[na + 1 + nr:n_in]
        out_refs, sum_refs = refs[n_in:n_in + no], refs[n_in + no:n_in + no + n_sums]
        acc = refs[-2:]
        i = pl.program_id(0)

        @pl.when(i == 0)
        def _():
            acc[1][...] = jnp.zeros_like(acc[1])
            for s_ref in sum_refs:
                s_ref[...] = jnp.zeros_like(s_ref)

        def step(write, read):
            t = None
            for p in range(na):
                lo, hi = offs[p], offs[p + 1]
                part = _dot(a_refs[p][...], b_ref[lo:hi, :] if mode == "nn" else b_ref[:, lo:hi], mode)
                t = part if t is None else t + part
            write[...] = t
            totals = [0.0] * n_sums
            for r0 in range(0, tm, chunk):
                rs = pl.ds(r0, chunk)
                results, sums = then(read[rs, :], [r.at[rs, :] for r in row_refs], vec_refs)
                for o_ref, val in zip(out_refs, results):
                    o_ref[rs, :] = val.astype(o_ref.dtype)
                totals = [tot + val for tot, val in zip(totals, sums)]
            for s_ref, val in zip(sum_refs, totals):
                s_ref[...] += jnp.where(i > 0, val, 0.0)

        pl.when(i % 2 == 0)(lambda: step(acc[0], acc[1]))
        pl.when(i % 2 == 1)(lambda: step(acc[1], acc[0]))

    ahead = lambda i: (jnp.minimum(i, n - 1), 0)
    behind = lambda i: (jnp.maximum(i - 1, 0), 0)
    fixed = lambda i: (0, 0)
    shapes = [jax.ShapeDtypeStruct((s, d), dt) for dt in rows_out] + [jax.ShapeDtypeStruct((1, d), F32)] * n_sums
    return _call_after(
        deps, n_in, body, name=name, grid=(n + 1,),
        in_specs=[pl.BlockSpec((tm, p.shape[1]), ahead) for p in a_parts] + [pl.BlockSpec(b.shape, fixed)]
        + [pl.BlockSpec((tm, r.shape[1]), behind) for r in rows_in] + [pl.BlockSpec(v.shape, fixed) for v in vecs_in],
        out_specs=[pl.BlockSpec((tm, d), behind)] * no + [pl.BlockSpec((1, d), fixed)] * n_sums,
        out_shape=shapes, scratch_shapes=[pltpu.VMEM((tm, d), F32)] * 2,
        compiler_params=_params(("arbitrary",), 56),
    )(*a_parts, b, *rows_in, *vecs_in)


def matmul_norm(name, a, b, res, g, then=None, deps=()):
    def norm(z, rows, vecs):
        y = rows[0][...] + (z * _rms(z)) * vecs[0][...]
        return [z, y] + ([(y * _rms(y)) * vecs[1][...]] if then else []), []

    return matmul_then(name, [a], b, "nn", 512, 512, [res], [g] + ([then[0]] if then else []),
                       [BF16, F32] + ([then[1]] if then else []), 0, norm, deps)


def matmul_norm_bwd(name, a_parts, b, tm, x, g, res, then=None, chunk=None, deps=()):
    def norms(t, rows, vecs):
        xv = rows[0][...].astype(F32)
        dx, dgt = _norm_bwd(xv, _rms(xv), vecs[0][...], t)
        dx = dx + rows[1][...]
        results, sums = [dx], [jnp.sum(dgt, axis=0, keepdims=True)]
        if then:
            x2 = rows[2][...].astype(F32)
            dx2, dgt2 = _norm_bwd(x2, _rms(x2), vecs[1][...], dx)
            results.append(dx2)
            sums.append(jnp.sum(dgt2, axis=0, keepdims=True))
        return results, sums

    return matmul_then(name, a_parts, b, "nn", tm, chunk or tm, [x, res] + ([then[0]] if then else []),
                       [g] + ([then[1]] if then else []), [F32] + ([then[2]] if then else []),
                       2 if then else 1, norms, deps)


def _norm_bwd(xv, r, g, dy):
    a = dy * g
    dx = r * (a - xv * ((r * r) * jnp.mean(a * xv, axis=-1, keepdims=True)))
    return dx, dy * (xv * r)


def rmsnorm_bwd(name, x, g, dy, res, out_dtype, then=None, deps=()):
    s, d = x.shape
    has_res = res is not None
    n_in = 3 + has_res + (2 if then else 0)

    def body(*refs):
        x_ref, g_ref, dy_ref = refs[:3]
        outs = refs[n_in:]
        xv = x_ref[...].astype(F32)
        dx, dgt = _norm_bwd(xv, _rms(xv), g_ref[...], dy_ref[...].astype(F32))
        if has_res:
            dx = dx + refs[3][...]
        outs[0][...] = dx.astype(outs[0].dtype)
        sums = [(outs[1], dgt)]
        if then is not None:
            x2 = refs[n_in - 2][...].astype(F32)
            dx2, dgt2 = _norm_bwd(x2, _rms(x2), refs[n_in - 1][...], dx)
            outs[2][...] = dx2.astype(outs[2].dtype)
            sums.append((outs[3], dgt2))

        @pl.when(pl.program_id(0) == 0)
        def _():
            for dg_ref, _ in sums:
                dg_ref[...] = jnp.zeros_like(dg_ref)
        for dg_ref, terms in sums:
            dg_ref[...] += jnp.sum(terms, axis=0, keepdims=True)

    row = pl.BlockSpec((T_ROW, d), lambda i: (i, 0))
    vec = pl.BlockSpec((1, d), lambda i: (0, 0))
    args = (x, g, dy) + ((res,) if has_res else ()) + ((then[0], then[1]) if then else ())
    shapes = [jax.ShapeDtypeStruct((s, d), out_dtype), jax.ShapeDtypeStruct((1, d), F32)]
    if then:
        shapes += [jax.ShapeDtypeStruct((s, d), then[2]), jax.ShapeDtypeStruct((1, d), F32)]
    return _call_after(
        deps, len(args), body, name=name, grid=(s // T_ROW,),
        in_specs=[row, vec, row] + ([row] if has_res else []) + ([row, vec] if then else []),
        out_specs=[row, vec] * (2 if then else 1), out_shape=shapes,
        compiler_params=_params(("arbitrary",)),
    )(*args)


def loss_head(name, h, tgt):
    s, d = h.shape

    def body(h_ref, t_ref, dh_ref, l_ref):
        e = h_ref[...] - t_ref[...]
        dh_ref[...] = e * (1.0 / d)

        @pl.when(pl.program_id(0) == 0)
        def _():
            l_ref[...] = jnp.zeros_like(l_ref)
        part = 0.5 * jnp.sum(jnp.mean(e * e, axis=-1, keepdims=True), axis=0, keepdims=True)
        l_ref[...] += jnp.broadcast_to(part, l_ref.shape)

    row = pl.BlockSpec((T_ROW, d), lambda i: (i, 0))
    return _call(
        body, name=name, grid=(s // T_ROW,),
        in_specs=[row, row], out_specs=[row, pl.BlockSpec((1, 128), lambda i: (0, 0))],
        out_shape=[jax.ShapeDtypeStruct((s, d), F32), jax.ShapeDtypeStruct((1, 128), F32)],
        compiler_params=_params(("arbitrary",)),
    )(h, tgt)


def _row_tile(rows, limit=512):
    t = min(rows, limit)
    while rows % t or (t % 8 and t != rows):
        t -= 1
    return t


def add_pairs(name, a, b):
    shape = a.shape
    a2, b2 = a.reshape(-1, shape[-1]), b.reshape(-1, shape[-1])
    rows, cols = a2.shape
    t = _row_tile(rows)

    def body(a_ref, b_ref, o_ref):
        o_ref[...] = (a_ref[...].astype(F32) + b_ref[...].astype(F32)).astype(o_ref.dtype)

    blk = pl.BlockSpec((t, cols), lambda i: (i, 0))
    out = _call(
        body, name=name, grid=(rows // t,), in_specs=[blk, blk], out_specs=blk,
        out_shape=jax.ShapeDtypeStruct((rows, cols), a.dtype), compiler_params=_params(("parallel",)),
    )(a2, b2)
    return out.reshape(shape)


def _adamw_math(w, m, v, g):
    c1 = 1.0 - ADAM_B1 ** ADAM_STEP
    c2 = 1.0 - ADAM_B2 ** ADAM_STEP
    mn = ADAM_B1 * m + (1.0 - ADAM_B1) * g
    vn = ADAM_B2 * v + (1.0 - ADAM_B2) * (g * g)
    delta = -ADAM_LR * ((mn / c1) / (jnp.sqrt(vn / c2) + ADAM_EPS) + ADAM_WD * w)
    return delta, mn, vn


def _slab_sum(g_ref, n):
    g = g_ref[0].astype(F32)
    for j in range(1, n):
        g = g + g_ref[j].astype(F32)
    return g


def adamw(name, w, m, v, g_slabs, deps=()):
    shape = w.shape
    n = g_slabs.shape[0]
    w2, m2, v2 = (t.reshape(-1, shape[-1]) for t in (w, m, v))
    g3 = g_slabs.reshape(n, -1, shape[-1])
    rows, cols = w2.shape
    t = _row_tile(rows, 256)

    def body(w_ref, m_ref, v_ref, g_ref, go_ref, d_ref, mo_ref, vo_ref):
        g = _slab_sum(g_ref, n)
        go_ref[...] = g
        d_ref[...], mo_ref[...], vo_ref[...] = _adamw_math(w_ref[...], m_ref[...], v_ref[...], g)

    blk = pl.BlockSpec((t, cols), lambda i: (i, 0))
    outs = _call_after(
        deps, 4, body, name=name, grid=(rows // t,),
        in_specs=[blk, blk, blk, pl.BlockSpec((n, t, cols), lambda i: (0, i, 0))], out_specs=[blk] * 4,
        out_shape=[jax.ShapeDtypeStruct((rows, cols), F32)] * 4, compiler_params=_params(("parallel",)),
    )(w2, m2, v2, g3)
    return tuple(o.reshape(shape) for o in outs)


def adamw_layer(name, l, w, m, v, own_sums, chip_sums, into):
    _, rows, cols = w.shape
    t = _row_tile(rows, 256)
    if into is None:
        into = tuple(lax.empty(w.shape, F32) for _ in range(4))

    def body(w_ref, m_ref, v_ref, own_ref, far_ref, a0, a1, a2, a3, go_ref, d_ref, mo_ref, vo_ref):
        del a0, a1, a2, a3
        q_me, _ = _place()
        g = jnp.zeros((t, cols), F32)
        for q in range(4):
            g = g + jnp.where(q == q_me, own_ref[q], far_ref[q]).astype(F32)
        go_ref[...] = g
        d_ref[...], mo_ref[...], vo_ref[...] = _adamw_math(w_ref[...], m_ref[...], v_ref[...], g)

    blk = pl.BlockSpec((None, t, cols), lambda i: (l, i, 0))
    slabs = pl.BlockSpec((4, t, cols), lambda i: (0, i, 0))
    anyw = pl.BlockSpec(memory_space=pl.ANY)
    return _call(
        body, name=name, grid=(rows // t,),
        in_specs=[blk, blk, blk, slabs, slabs] + [anyw] * 4, out_specs=[blk] * 4,
        out_shape=[jax.ShapeDtypeStruct(w.shape, F32)] * 4, input_output_aliases={5: 0, 6: 1, 7: 2, 8: 3},
        compiler_params=_params(("parallel",)),
    )(w, m, v, own_sums, chip_sums, *into)


_DIMS = {"nn": (((1,), (0,)), ((), ())), "nt": (((1,), (1,)), ((), ())), "tn": (((0,), (0,)), ((), ()))}


def _dot(a, b, mode="nn"):
    return lax.dot_general(a, b, _DIMS[mode], preferred_element_type=F32)


def matmul(name, a, b, mode, out_dtype, m, n, k, tm, tn, tk, b_off=(0, 0), shard_rows=0, deps=()):
    gm, gn, gk = m // tm, n // tn, k // tk
    assert gm * tm == m and gn * tn == n and gk * tk == k
    r0, c0 = b_off
    a_parts = list(a) if isinstance(a, (list, tuple)) else [a]
    na = len(a_parts)
    tile = tm if mode == "tn" else tk
    spans, at = [], 0
    for p in a_parts:
        nblk = p.shape[1] // tile
        assert nblk * tile == p.shape[1]
        spans.append((at, nblk))
        at += nblk
    assert at == (gm if mode == "tn" else gk)

    def within(t, span):
        return (t >= span[0]) & (t < span[0] + span[1])

    def local(t, span):
        return jnp.clip(t - span[0], 0, span[1] - 1)

    a_specs = []
    for sp in spans:
        if mode == "tn":
            a_specs.append(pl.BlockSpec((tk, tm), lambda i, j, kk, sp=sp: (jnp.where(within(i, sp), kk, 0), local(i, sp))))
        else:
            a_specs.append(pl.BlockSpec((tm, tk), lambda i, j, kk, sp=sp: (i, local(kk, sp))))
    if mode == "nt":
        b_spec = pl.BlockSpec((tn, tk), lambda i, j, kk: (j + r0, kk + c0))
    else:
        b_spec = pl.BlockSpec((tk, tn), lambda i, j, kk: (kk + r0, j + c0))
    o_spec = pl.BlockSpec((tm, tn), lambda i, j, kk: (i, j))
    single = na == 1 and gk == 1 and not shard_rows
    pairs = tm // (2 * shard_rows) if shard_rows else 0
    assert not shard_rows or (pairs * 2 * shard_rows == tm and m == 8 * shard_rows)

    def body(*refs):
        a_refs, b_ref, o_ref = refs[:na], refs[na], refs[na + 1]
        if single:
            o_ref[...] = _dot(a_refs[0][...], b_ref[...], mode).astype(o_ref.dtype)
            return
        acc_ref = refs[-1]
        i, kk = pl.program_id(0), pl.program_id(2)

        @pl.when(kk == 0)
        def _():
            acc_ref[...] = jnp.zeros_like(acc_ref)

        for pa, sp in enumerate(spans):
            def add(pa=pa):
                acc_ref[...] += _dot(a_refs[pa][...], b_ref[...], mode)
            if na > 1:
                pl.when(within(i if mode == "tn" else kk, sp))(add)
            else:
                add()

        @pl.when(kk == gk - 1)
        def _():
            if not shard_rows:
                o_ref[...] = acc_ref[...].astype(o_ref.dtype)
                return
            send_ref, mine = refs[na + 2], lax.axis_index("c") == 0
            for pq in range(pairs):
                even = acc_ref[2 * shard_rows * pq:2 * shard_rows * pq + shard_rows, :]
                odd = acc_ref[2 * shard_rows * pq + shard_rows:2 * shard_rows * (pq + 1), :]
                o_ref[pq] = jnp.where(mine, even, odd).astype(o_ref.dtype)
                send_ref[pq] = jnp.where(mine, odd, even).astype(send_ref.dtype)

    if shard_rows:
        o_spec = pl.BlockSpec((pairs, shard_rows, tn), lambda i, j, kk: (i, 0, j))
    shard = jax.ShapeDtypeStruct((4, shard_rows, n), out_dtype)
    return _call_after(
        deps, na + 1, body, name=name, grid=(gm, gn, gk),
        in_specs=a_specs + [b_spec], out_specs=[o_spec, o_spec] if shard_rows else o_spec,
        out_shape=[shard, shard] if shard_rows else jax.ShapeDtypeStruct((m, n), out_dtype),
        scratch_shapes=[] if single else [pltpu.VMEM((tm, tn), F32)],
        compiler_params=_params(("parallel", "parallel", "arbitrary")),
    )(*a_parts, b)


def matmul_swiglu(name, a, w_gu_t, deps=()):
    s, d = a.shape
    tm, tn = 512, 1408
    gn = D_FF // tn

    def body(a_ref, wg_ref, wu_ref, g_ref, u_ref, f_ref):
        av = a_ref[...]
        g = _dot(av, wg_ref[...], "nt")
        u = _dot(av, wu_ref[...], "nt")
        g_ref[...] = g.astype(g_ref.dtype)
        u_ref[...] = u.astype(u_ref.dtype)
        f_ref[...] = ((g * (1.0 / (1.0 + jnp.exp(-g)))) * u).astype(f_ref.dtype)

    o_spec = pl.BlockSpec((tm, tn), lambda j, i: (i, j))
    return _call_after(
        deps, 3, body, name=name, grid=(gn, s // tm),
        in_specs=[pl.BlockSpec((tm, d), lambda j, i: (i, 0)), pl.BlockSpec((tn, d), lambda j, i: (j, 0)),
                  pl.BlockSpec((tn, d), lambda j, i: (j + gn, 0))],
        out_specs=[o_spec] * 3, out_shape=[jax.ShapeDtypeStruct((s, D_FF), BF16)] * 3,
        compiler_params=_params(("parallel", "parallel")),
    )(a, w_gu_t, w_gu_t)


def matmul_swiglu_bwd(name, dffo, w_down, g, u):
    s, d = dffo.shape
    tm, tn = 512, 1408

    def body(a_ref, w_ref, g_ref, u_ref, dg_ref, du_ref):
        av = a_ref[...]
        for lo in range(0, tn, 256):
            cols = slice(lo, min(lo + 256, tn))
            dff = _dot(av, w_ref[cols, :], "nt")
            gv = g_ref[:, cols].astype(F32)
            sig = 1.0 / (1.0 + jnp.exp(-gv))
            silu = gv * sig
            du_ref[:, cols] = (dff * silu).astype(du_ref.dtype)
            dg_ref[:, cols] = ((dff * u_ref[:, cols].astype(F32)) * (sig + silu * (1.0 - sig))).astype(dg_ref.dtype)

    o_spec = pl.BlockSpec((tm, tn), lambda j, i: (i, j))
    return _call(
        body, name=name, grid=(D_FF // tn, s // tm),
        in_specs=[pl.BlockSpec((tm, d), lambda j, i: (i, 0)), pl.BlockSpec((tn, d), lambda j, i: (j, 0)), o_spec, o_spec],
        out_specs=[o_spec] * 2, out_shape=[jax.ShapeDtypeStruct((s, D_FF), BF16)] * 2,
        compiler_params=_params(("parallel", "parallel")),
    )(dffo, w_down, g, u)


def _down(v, n):
    return pltpu.roll(v, n, 0)


def _up(v, n):
    return pltpu.roll(v, v.shape[0] - n, 0)


def _by_window(lane, v2, v4, v8, v16):
    return jnp.where(lane < POOL_GC, v2, jnp.where(lane < 2 * POOL_GC, v4, jnp.where(lane < 3 * POOL_GC, v8, v16)))


def _taps(cw_ref):
    return cw_ref[0:1, :], cw_ref[1:2, :], cw_ref[2:3, :]


def _conv_pool_forward(ext, t0, cw, bd):
    n_out = ext.shape[0] - HALO
    gb = ext[HALO:, 0:D_CONV]
    z = ext[:, D_CONV:2 * D_CONV] * ext[:, 2 * D_CONV:3 * D_CONV]
    z0, z1, z2 = z[HALO:], _down(z, 1)[HALO:], _down(z, 2)[HALO:]
    conv = cw[2] * z0 + cw[1] * z1 + cw[0] * z2
    x = ext[:, 3 * D_CONV:]
    w2 = x + _down(x, 1)
    w4 = w2 + _down(w2, 2)
    w8 = w4 + _down(w4, 4)
    w16 = w8 + _down(w8, 8)
    lane = lax.broadcasted_iota(jnp.int32, (1, D_POOL), 1)
    win = _by_window(lane, 2.0, 4.0, 8.0, 16.0)
    pos = (t0 + lax.broadcasted_iota(jnp.int32, (n_out, 1), 0) + 1).astype(F32)
    cnt = jnp.minimum(pos, win)
    d = _by_window(lane, w2, w4, w8, w16)[HALO:] / cnt - x[HALO:]
    ybp = _dot(d.astype(BF16), bd)
    return gb, z0, z1, z2, conv, d, cnt, ybp, lane


def conv_pool_fwd(name, proj_a, cw, bd, ps, gg, deps=()):
    s = proj_a.shape[0]
    t = T_CP
    hb = t // HALO

    def body(main_ref, prev_ref, cw_ref, bd_ref, ps_ref, gg_ref, y_ref):
        i = pl.program_id(0)
        prev = jnp.where(i > 0, prev_ref[...].astype(F32), 0.0)
        ext = jnp.concatenate([prev, main_ref[...].astype(F32)], axis=0)
        gb, _, _, _, conv, _, _, ybp, _ = _conv_pool_forward(ext, i * t, _taps(cw_ref), bd_ref[...])
        ya = gb * conv
        yb = ybp * ps_ref[...]
        ggv = gg_ref[...]
        y_ref[:, 0:D_CONV] = ((ya * _rms(ya)) * ggv[:, 0:D_CONV]).astype(y_ref.dtype)
        y_ref[:, D_CONV:] = ((yb * _rms(yb)) * ggv[:, D_CONV:]).astype(y_ref.dtype)

    full = lambda shape: pl.BlockSpec(shape, lambda i: (0,) * len(shape))
    return _call_after(
        deps, 6, body, name=name, grid=(s // t,),
        in_specs=[pl.BlockSpec((t, D_MODEL), lambda i: (i, 0)),
                  pl.BlockSpec((HALO, D_MODEL), lambda i: (jnp.maximum(i * hb - 1, 0), 0)),
                  full((3, D_CONV)), full((D_POOL, D_POOL)), full((1, D_POOL)), full((1, 2 * D_CONV))],
        out_specs=pl.BlockSpec((t, 2 * D_CONV), lambda i: (i, 0)),
        out_shape=jax.ShapeDtypeStruct((s, D_MODEL), BF16),
        compiler_params=_params(("parallel",)),
    )(proj_a, proj_a, cw, bd, ps, gg)


def conv_pool_bwd(name, proj_a, dy, cw, bd, ps, gg):
    s = proj_a.shape[0]
    t = T_CP
    hb = t // HALO
    nblk = s // t
    last_halo = s // HALO - 1

    def body(main_ref, prev_ref, next_ref, dy_ref, dyn_ref, cw_ref, bd_ref, ps_ref, gg_ref,
             dp_ref, dcw_ref, dbd_ref, dps_ref, dgg_ref):
        i = pl.program_id(0)
        prev = jnp.where(i > 0, prev_ref[...].astype(F32), 0.0)
        main = main_ref[...].astype(F32)
        ext = jnp.concatenate([prev, main, next_ref[...].astype(F32)], axis=0)
        cwv, bdv, psv, ggv = _taps(cw_ref), bd_ref[...], ps_ref[...], gg_ref[...]
        gb, z0, z1, z2, conv, d, cnt, ybp, lane = _conv_pool_forward(ext, i * t, cwv, bdv)
        dyn = jnp.where(i < nblk - 1, dyn_ref[...].astype(F32), 0.0)
        dyv = jnp.concatenate([dy_ref[...].astype(F32), dyn], axis=0)
        ya = gb * conv
        yb = ybp * psv
        dya, dgg_a = _norm_bwd(ya, _rms(ya), ggv[:, 0:D_CONV], dyv[:, 0:D_CONV])
        dyb, dgg_b = _norm_bwd(yb, _rms(yb), ggv[:, D_CONV:], dyv[:, D_CONV:])

        dconv = dya * gb
        dz = (cwv[2] * dconv + cwv[1] * _up(dconv, 1) + cwv[0] * _up(dconv, 2))[:t]
        dp_ref[:, 0:D_CONV] = (dya * conv)[:t].astype(dp_ref.dtype)
        dp_ref[:, D_CONV:2 * D_CONV] = (dz * main[:, 2 * D_CONV:3 * D_CONV]).astype(dp_ref.dtype)
        dp_ref[:, 2 * D_CONV:3 * D_CONV] = (dz * main[:, D_CONV:2 * D_CONV]).astype(dp_ref.dtype)

        dybs = dyb * psv
        dd = _dot(dybs.astype(BF16), bdv, "nt")
        e = dd / cnt
        a2 = e + _up(e, 1)
        a4 = a2 + _up(a2, 2)
        a8 = a4 + _up(a4, 4)
        a16 = a8 + _up(a8, 8)
        dp_ref[:, 3 * D_CONV:] = (_by_window(lane, a2, a4, a8, a16) - dd)[:t].astype(dp_ref.dtype)

        @pl.when(i == 0)
        def _():
            dcw_ref[...] = jnp.zeros_like(dcw_ref)
            dbd_ref[...] = jnp.zeros_like(dbd_ref)
            dps_ref[...] = jnp.zeros_like(dps_ref)
            dgg_ref[...] = jnp.zeros_like(dgg_ref)

        rsum = lambda v: jnp.sum(v[:t], axis=0, keepdims=True)
        dcw_ref[0:1, :] += rsum(dconv * z2)
        dcw_ref[1:2, :] += rsum(dconv * z1)
        dcw_ref[2:3, :] += rsum(dconv * z0)
        dbd_ref[...] += _dot(d[:t].astype(BF16), dybs[:t].astype(BF16), "tn")
        dps_ref[...] += rsum(dyb * ybp)
        dgg_ref[:, 0:D_CONV] += rsum(dgg_a)
        dgg_ref[:, D_CONV:] += rsum(dgg_b)

    full = lambda shape: pl.BlockSpec(shape, lambda i: (0,) * len(shape))
    next_halo = lambda i: (jnp.minimum((i + 1) * hb, last_halo), 0)
    return _call(
        body, name=name, grid=(nblk,),
        in_specs=[pl.BlockSpec((t, D_MODEL), lambda i: (i, 0)),
                  pl.BlockSpec((HALO, D_MODEL), lambda i: (jnp.maximum(i * hb - 1, 0), 0)),
                  pl.BlockSpec((HALO, D_MODEL), next_halo),
                  pl.BlockSpec((t, 2 * D_CONV), lambda i: (i, 0)),
                  pl.BlockSpec((HALO, 2 * D_CONV), next_halo),
                  full((3, D_CONV)), full((D_POOL, D_POOL)), full((1, D_POOL)), full((1, 2 * D_CONV))],
        out_specs=[pl.BlockSpec((t, D_MODEL), lambda i: (i, 0)),
                   full((3, D_CONV)), full((D_POOL, D_POOL)), full((1, D_POOL)), full((1, 2 * D_CONV))],
        out_shape=[jax.ShapeDtypeStruct((s, D_MODEL), BF16), jax.ShapeDtypeStruct((3, D_CONV), F32),
                   jax.ShapeDtypeStruct((D_POOL, D_POOL), F32), jax.ShapeDtypeStruct((1, D_POOL), F32),
                   jax.ShapeDtypeStruct((1, 2 * D_CONV), F32)],
        compiler_params=_params(("arbitrary",)),
    )(proj_a, proj_a, proj_a, dy, dy, cw, bd, ps, gg)


def bias_by_offset(rel_bias):
    n_far = 2 * QB - REL_CLIP + 1
    far = jnp.broadcast_to(rel_bias[:, 2 * REL_CLIP:], (N_HEADS, n_far))
    mid = rel_bias[:, 1:2 * REL_CLIP][:, ::-1]
    near = jnp.broadcast_to(rel_bias[:, 0:1], (N_HEADS, KB - n_far - (2 * REL_CLIP - 1)))
    wrap = jnp.broadcast_to(rel_bias[:, 2 * REL_CLIP:], (N_HEADS, 4 * QB - KB))
    return jnp.concatenate([far, mid, near, wrap], axis=1)


def _fill_bias(off_ref, b_ref):
    width = 4 * QB
    sub = lax.broadcasted_iota(jnp.int32, (8, 1), 0)
    col = lax.broadcasted_iota(jnp.int32, (1, KB), 1)
    for h in range(N_HEADS):
        base = jnp.broadcast_to(off_ref[h:h + 1, :], (8, width))
        for bit in range(3):
            base = jnp.where(((sub >> bit) & 1) == 1, pltpu.roll(base, 1 << bit, 1), base)
        for a in range(QB // 8):
            first = CHUNK * (8 * a // CHUNK)
            rows = (pltpu.roll(base, 8 * a, 1) if a else base)[:, :KB]
            band = (col >= first) & (col < first + (LEFT_CHUNKS + 1) * CHUNK)
            b_ref[h, 8 * a:8 * a + 8, :] = jnp.where(band, rows, NEG)


def rel_bias_grad(by_offset):
    n_far = 2 * QB - REL_CLIP + 1
    hi = jnp.sum(by_offset[:, :n_far], axis=1, keepdims=True) + jnp.sum(by_offset[:, KB:], axis=1, keepdims=True)
    mid = by_offset[:, n_far:n_far + 2 * REL_CLIP - 1][:, ::-1]
    lo = jnp.sum(by_offset[:, n_far + 2 * REL_CLIP - 1:KB], axis=1, keepdims=True)
    return jnp.concatenate([lo, mid, hi], axis=1)


def _head_masks():
    first = lax.broadcasted_iota(jnp.int32, (1, 2 * HEAD_DIM), 1) < HEAD_DIM
    return first, jnp.logical_not(first)


def _pick_lane(tile, h):
    lane = lax.broadcasted_iota(jnp.int32, (1, tile.shape[1]), 1)
    return jnp.sum(jnp.where(lane == h, tile, 0.0), axis=-1, keepdims=True)


def attention_fwd(name, qkv, by_offset, gg, y_ab):
    s = qkv.shape[0]
    nq = s // QB
    scale = HEAD_DIM ** -0.5

    def body(q_ref, k0, k1, k2, v0, v1, v2, off_ref, gg_ref, y_in, y_ref, o_ref, lse_ref, b_ref):
        del y_in
        i = pl.program_id(0)

        @pl.when(i == 0)
        def _():
            _fill_bias(off_ref, b_ref)
        def block(at_start):
            kb = jnp.concatenate([k0[...], k1[...], k2[...]], axis=0)
            vb = jnp.concatenate([v0[...], v1[...], v2[...]], axis=0)
            valid = lax.broadcasted_iota(jnp.int32, (1, KB), 1) >= (2 - i) * QB
            lane = lax.broadcasted_iota(jnp.int32, (1, 128), 1)
            masks = _head_masks()
            lse = jnp.zeros((QB, 128), F32)
            outs = []
            for hp in range(N_HEADS // 2):
                sl = slice(2 * HEAD_DIM * hp, 2 * HEAD_DIM * (hp + 1))
                q_p, k_p, v_p = q_ref[:, sl] * scale, kb[:, sl], vb[:, sl]
                o_pair = jnp.zeros((QB, 2 * HEAD_DIM), F32)
                for a in range(2):
                    h = 2 * hp + a
                    sc = _dot(jnp.where(masks[a], q_p, 0), k_p, "nt") + b_ref[h]
                    if at_start:
                        sc = jnp.where(valid, sc, NEG)
                    mx = jnp.max(sc, axis=-1, keepdims=True)
                    e = jnp.exp(sc - mx)
                    l = jnp.sum(e, axis=-1, keepdims=True)
                    o_pair = o_pair + _dot(e.astype(BF16), jnp.where(masks[a], v_p, 0)) * (1.0 / l)
                    lse = jnp.where(lane == h, mx + jnp.log(l), lse)
                outs.append(o_pair)
            o = jnp.concatenate(outs, axis=1)
            o_ref[...] = o.astype(o_ref.dtype)
            lse_ref[...] = lse
            y_ref[...] = ((o * _rms(o)) * gg_ref[...]).astype(y_ref.dtype)

        pl.when(i < 2)(lambda: block(True))
        pl.when(i >= 2)(lambda: block(False))

    blk = lambda col, back: pl.BlockSpec((QB, D_ATTN), lambda i: (jnp.maximum(i - back, 0), QKV_COL + col))
    return _call(
        body, name=name, grid=(nq,),
        in_specs=[blk(0, 0), blk(1, 2), blk(1, 1), blk(1, 0), blk(2, 2), blk(2, 1), blk(2, 0),
                  pl.BlockSpec((N_HEADS, 4 * QB), lambda i: (0, 0)), pl.BlockSpec((1, D_ATTN), lambda i: (0, 0)),
                  pl.BlockSpec(memory_space=pl.ANY)],
        out_specs=[pl.BlockSpec((QB, D_ATTN), lambda i: (i, 1)), pl.BlockSpec((QB, D_ATTN), lambda i: (i, 0)),
                   pl.BlockSpec((QB, 128), lambda i: (i, 0)), pl.BlockSpec((N_HEADS, QB, KB), lambda i: (0, 0, 0))],
        out_shape=[jax.ShapeDtypeStruct((s, D_MODEL), BF16), jax.ShapeDtypeStruct((s, D_ATTN), BF16),
                   jax.ShapeDtypeStruct((s, 128), F32), jax.ShapeDtypeStruct((N_HEADS, QB, KB), F32)],
        input_output_aliases={9: 0},
        compiler_params=_params(("arbitrary",), 56),
    )(qkv, qkv, qkv, qkv, qkv, qkv, qkv, by_offset, gg, y_ab)


def attention_bwd(name, qkv, o, lse, dy, bias, gg):
    s = qkv.shape[0]
    nq = s // QB
    scale = HEAD_DIM ** -0.5
    width = 4 * QB

    def body(q_ref, k0, k1, k2, v0, v1, v2, o_ref, lse_ref, dy_ref, b_ref, gg_ref,
             dq_ref, dk_ref, dv_ref, off_ref, dgg_ref, dk_acc, dv_acc, db_acc):
        i = pl.program_id(0)

        @pl.when(i == 0)
        def _():
            dk_acc[...] = jnp.zeros_like(dk_acc)
            dv_acc[...] = jnp.zeros_like(dv_acc)
            db_acc[...] = jnp.zeros_like(db_acc)
            dgg_ref[...] = jnp.zeros_like(dgg_ref)

        rows_of = lambda j: pl.ds(pl.multiple_of((j % 3) * QB, QB), QB)

        @pl.when(i > 0)
        def _():
            dk_acc[rows_of(i), :] = jnp.zeros((QB, D_ATTN), F32)
            dv_acc[rows_of(i), :] = jnp.zeros((QB, D_ATTN), F32)

        def block(at_start):
            ov = o_ref[...].astype(F32)
            dyv = dy_ref[...].astype(F32)
            do, dgg_t = _norm_bwd(ov, _rms(ov), gg_ref[...], dyv)
            dgg_ref[...] += jnp.sum(dgg_t, axis=0, keepdims=True)
            kb = jnp.concatenate([k0[...], k1[...], k2[...]], axis=0)
            vb = jnp.concatenate([v0[...], v1[...], v2[...]], axis=0)
            valid = lax.broadcasted_iota(jnp.int32, (1, KB), 1) >= (2 - i) * QB
            masks = _head_masks()
            lse_t = lse_ref[...]
            for hp in range(N_HEADS // 2):
                sl = slice(2 * HEAD_DIM * hp, 2 * HEAD_DIM * (hp + 1))
                q_p, k_p, v_p = q_ref[:, sl] * scale, kb[:, sl], vb[:, sl]
                do_p = do[:, sl]
                prod = do_p * ov[:, sl]
                do_b = do_p.astype(BF16)
                dq_pair = jnp.zeros((QB, 2 * HEAD_DIM), F32)
                dk_pair = jnp.zeros((KB, 2 * HEAD_DIM), F32)
                dv_pair = jnp.zeros((KB, 2 * HEAD_DIM), F32)
                for a in range(2):
                    h = 2 * hp + a
                    q_m = jnp.where(masks[a], q_p, 0)
                    do_m = jnp.where(masks[a], do_b, 0)
                    sc = _dot(q_m, k_p, "nt") + b_ref[h]
                    if at_start:
                        sc = jnp.where(valid, sc, NEG)
                    p = jnp.exp(sc - _pick_lane(lse_t, h))
                    dp = _dot(do_m, v_p, "nt")
                    delta = jnp.sum(jnp.where(masks[a], prod, 0.0), axis=-1, keepdims=True)
                    ds = p * (dp - delta)
                    db_acc[h] += ds
                    ds_b = ds.astype(BF16)
                    dq_pair = dq_pair + _dot(ds_b, jnp.where(masks[a], k_p, 0))
                    dk_pair = dk_pair + _dot(ds_b, q_m, "tn")
                    dv_pair = dv_pair + _dot(p.astype(BF16), do_m, "tn")
                dq_ref[:, sl] = (dq_pair * scale).astype(dq_ref.dtype)
                for t in range(3):
                    dk_acc[rows_of(i + 1 + t), sl] += dk_pair[QB * t:QB * (t + 1)]
                    dv_acc[rows_of(i + 1 + t), sl] += dv_pair[QB * t:QB * (t + 1)]

        pl.when(i < 2)(lambda: block(True))
        pl.when((i >= 2) & (i < nq))(lambda: block(False))

        dk_ref[...] = dk_acc[rows_of(i + 1), :].astype(dk_ref.dtype)
        dv_ref[...] = dv_acc[rows_of(i + 1), :].astype(dv_ref.dtype)

        @pl.when(i == nq + 1)
        def _():
            sub = lax.broadcasted_iota(jnp.int32, (8, 1), 0)
            pad = jnp.zeros((8, width - KB), F32)
            for h in range(N_HEADS):
                v = jnp.concatenate([db_acc[h, 0:8, :], pad], axis=1)
                for a in range(1, QB // 8):
                    grp = jnp.concatenate([db_acc[h, 8 * a:8 * a + 8, :], pad], axis=1)
                    v = v + pltpu.roll(grp, width - 8 * a, 1)
                for bit in range(3):
                    v = jnp.where(((sub >> bit) & 1) == 1, pltpu.roll(v, width - (1 << bit), 1), v)
                off_ref[h:h + 1, :] = jnp.sum(v, axis=0, keepdims=True)

    qi = lambda i: jnp.minimum(i, nq - 1)
    kblk = lambda col, back: pl.BlockSpec((QB, D_ATTN), lambda i: (jnp.clip(i - back, 0, nq - 1), QKV_COL + col))
    qblk = lambda col: pl.BlockSpec((QB, D_ATTN), lambda i: (qi(i), col))
    done = pl.BlockSpec((QB, D_ATTN), lambda i: (jnp.clip(i - 2, 0, nq - 1), 0))
    return _call(
        body, name=name, grid=(nq + 2,),
        in_specs=[qblk(QKV_COL), kblk(1, 2), kblk(1, 1), kblk(1, 0), kblk(2, 2), kblk(2, 1), kblk(2, 0),
                  qblk(0), pl.BlockSpec((QB, 128), lambda i: (qi(i), 0)), qblk(1),
                  pl.BlockSpec((N_HEADS, QB, KB), lambda i: (0, 0, 0)), pl.BlockSpec((1, D_ATTN), lambda i: (0, 0))],
        out_specs=[qblk(0), done, done, pl.BlockSpec((N_HEADS, width), lambda i: (0, 0)),
                   pl.BlockSpec((1, D_ATTN), lambda i: (0, 0))],
        out_shape=[jax.ShapeDtypeStruct((s, D_ATTN), BF16)] * 3
        + [jax.ShapeDtypeStruct((N_HEADS, width), F32), jax.ShapeDtypeStruct((1, D_ATTN), F32)],
        scratch_shapes=[pltpu.VMEM((KB, D_ATTN), F32), pltpu.VMEM((KB, D_ATTN), F32), pltpu.VMEM((N_HEADS, QB, KB), F32)],
        compiler_params=_params(("arbitrary",), 56),
    )(qkv, qkv, qkv, qkv, qkv, qkv, qkv, o, lse, dy, bias, gg)


def _block_diag(pw):
    out = jnp.zeros((D_POOL, D_POOL), pw.dtype)
    for gi in range(N_POOL):
        out = lax.dynamic_update_slice(out, pw[gi], (gi * POOL_GC, gi * POOL_GC))
    return out


_SMALL = ("pool_w", "pool_scale", "rel_bias", "group_gain", "pre_mix_g", "post_mix_g", "pre_ffn_g", "post_ffn_g")


def _pack(parts, rows):
    flat = jnp.concatenate([p.reshape(-1).astype(F32) for p in parts])
    return jnp.pad(flat, (0, rows * D_MODEL - flat.shape[0])).reshape(rows, D_MODEL)


def _unpack(packed, shapes):
    flat = packed.reshape(-1)
    out, at = [], 0
    for shp in shapes:
        size = int(np.prod(shp))
        out.append(flat[at:at + size].reshape(shp))
        at += size
    return out


def kernel(x, w_in, w_out, conv_w, pool_w, pool_scale, rel_bias, group_gain, pre_mix_g, post_mix_g, pre_ffn_g, post_ffn_g, w_gate_up, w_down, loss_target, m_w_in, m_w_out, m_conv_w, m_pool_w, m_pool_scale, m_rel_bias, m_group_gain, m_pre_mix_g, m_post_mix_g, m_pre_ffn_g, m_post_ffn_g, m_w_gate_up, m_w_down, v_w_in, v_w_out, v_conv_w, v_pool_w, v_pool_scale, v_rel_bias, v_group_gain, v_pre_mix_g, v_post_mix_g, v_pre_ffn_g, v_post_ffn_g, v_w_gate_up, v_w_down):
    depth = w_in.shape[0]
    s = x.shape[1]
    c_me = lax.axis_index("c")
    q_me = 2 * lax.axis_index("x") + lax.axis_index("y")
    dev = 2 * q_me + c_me

    tr = lambda a: jnp.swapaxes(a, 1, 2)
    w_in_t, w_gu_t = tr(w_in), tr(w_gate_up)

    def gather_layer(l, deps):
        first = gather_start(f"gather_a{l}_s", [w_in_t[l].astype(BF16), w_out[l].astype(BF16), conv_w[l]], deps)
        return first, gather_start(f"gather_b{l}_s", [w_gu_t[l].astype(BF16), w_down[l].astype(BF16)], [first["token"]])

    def weights_a(relayed, l, after):
        g_in, g_out, g_cw = gather_finish(f"gather_a{l}_f", relayed, after)
        return (assemble_rows("assemble_in", *g_in), assemble_rows("assemble_out", *g_out),
                assemble_cols("assemble_conv", *g_cw))

    def weights_b(relayed, l, after):
        g_gu, g_dn = gather_finish(f"gather_b{l}_f", relayed, after)
        return assemble_rows("assemble_gu", *g_gu), assemble_rows("assemble_down", *g_dn)

    h = x.reshape(s, D_MODEL)
    started = {0: gather_layer(0, [])}
    xn = rmsnorm_fwd("norm_mix", h, pre_mix_g[0].reshape(1, -1), BF16)
    relay_a = gather_relay("gather_a0_r", started[0][0], started[0][1]["token"])
    wa = weights_a(relay_a, 0, xn)
    saved = []
    for l in range(depth):
        vec = lambda p: p[l].reshape(1, -1)
        wt_in, wf_out, cw_full = wa
        ahead = [started[l][1]["token"]]
        if l + 1 < depth:
            started[l + 1] = gather_layer(l + 1, ahead)
            ahead = [started[l + 1][1]["token"]]
        proj = matmul("proj", xn, wt_in, "nt", BF16, s, 5 * D_ATTN, D_MODEL, 512, 5 * D_ATTN, D_MODEL, deps=ahead)
        relay_b = gather_relay(f"gather_b{l}_r", started[l][1], proj)
        bd = _block_diag(pool_w[l]).astype(BF16)
        gg = vec(group_gain)
        y_ab = conv_pool_fwd("conv_pool_fwd", proj, cw_full, bd, vec(pool_scale), gg[:, :2 * D_CONV],
                             deps=[relay_b["token"]])
        y, o, lse, bias = attention_fwd("attention_fwd", proj, bias_by_offset(rel_bias[l]), gg[:, 2 * D_CONV:], y_ab)
        wt_gu, wf_dn = weights_b(relay_b, l, y)
        if l + 1 < depth:
            relay_a = gather_relay(f"gather_a{l + 1}_r", started[l + 1][0], y)
        mix, h_mid, hn = matmul_norm("mix_out", y, wf_out, h, vec(post_mix_g), then=(vec(pre_ffn_g), BF16),
                                     deps=[relay_a["token"]] if l + 1 < depth else [])
        g, u, ff = matmul_swiglu("gate_up", hn, wt_gu)
        if l + 1 < depth:
            wa = weights_a(relay_a, l + 1, ff)
            ffo, h_out, xn_next = matmul_norm("ffn_down", ff, wf_dn, h_mid, vec(post_ffn_g),
                                              then=(pre_mix_g[l + 1].reshape(1, -1), BF16))
        else:
            ffo, h_out = matmul_norm("ffn_down_last", ff, wf_dn, h_mid, vec(post_ffn_g))
            xn_next = None
        saved.append((h, xn, proj, None, bd, bias, y, o, lse, mix, h_mid, hn, g, u, ff, ffo,
                      wt_in, wf_out, cw_full, wt_gu, wf_dn))
        h, xn = h_out, xn_next

    dh, loss_part = loss_head("loss_head", h, loss_target.reshape(s, D_MODEL))

    def reduce_begin(name, halves):
        return [k for k, _ in halves], exchange_start(name, plan_swap_cores, len(halves), [snd for _, snd in halves],
                                                      [(k.shape, k.dtype) for k, _ in halves])

    def reduce_relay(name, kept, swapped, after):
        _, got = exchange_wait(name + "_w", swapped, after)
        sums = [add_pairs("reduce_add", k, r) for k, r in zip(kept, got)]
        return exchange_start(name + "_s", plan_to_chips, 3 * len(sums), sums, [(a.shape, a.dtype) for a in sums])

    def reduce_finish(name, relayed, after):
        sums, got = exchange_wait(name, relayed, after)
        return list(zip(sums, got))

    small = {k: [None] * depth for k in _SMALL + ("conv_w",)}
    relayed = {}
    carried = []
    dffo, small["post_ffn_g"][depth - 1] = rmsnorm_bwd("norm_bwd_top", saved[-1][15], post_ffn_g[depth - 1].reshape(1, -1),
                                                       dh, None, BF16)
    for l in reversed(range(depth)):
        vec = lambda p: p[l].reshape(1, -1)
        (h_in, xn, proj, _, bd, bias, y, o, lse, mix, h_mid, hn, g, u, ff, ffo,
         wt_in, wf_out, cw_full, wt_gu, wf_dn) = saved[l]
        gg = vec(group_gain)
        gw_dn = matmul("wgrad_down", ff, dffo, "tn", BF16, D_FF, D_MODEL, s, 1408, D_MODEL, 1024,
                       shard_rows=D_FF // 8, deps=carried)
        dg, du = matmul_swiglu_bwd("dgrad_down", dffo, wf_dn, g, u)
        gw_gu = matmul("wgrad_gu", [dg, du], hn, "tn", BF16, 2 * D_FF, D_MODEL, s, 1408, D_MODEL, 1024,
                       shard_rows=2 * D_FF // 8)
        kept_b, swap_b = reduce_begin(f"reduce_b{l}_s", [gw_gu, gw_dn])
        dh_mid, dmix, small["pre_ffn_g"][l], small["post_mix_g"][l] = matmul_norm_bwd(
            "dgrad_gu", [dg, du], wt_gu, 256, h_mid, vec(pre_ffn_g), dh, then=(mix, vec(post_mix_g), BF16),
            deps=[swap_b["token"]])
        relayed[l, "b"] = reduce_relay(f"reduce_b{l}_r", kept_b, swap_b, dmix)
        gw_out = matmul("wgrad_out", y, dmix, "tn", BF16, D_MODEL, D_MODEL, s, D_MODEL, D_MODEL, 1024,
                        shard_rows=D_MODEL // 8, deps=[relayed[l, "b"]["token"]])
        dy = matmul("dgrad_out", dmix, wf_out, "nt", BF16, s, D_MODEL, D_MODEL, 512, D_MODEL, D_MODEL)
        dpa, dcw, dbd, dps, dgg_ab = conv_pool_bwd("conv_pool_bwd", proj, dy, cw_full, bd, vec(pool_scale),
                                                   gg[:, :2 * D_CONV])
        dq, dk, dv, by_off, dgg_c = attention_bwd("attention_bwd", proj, o, lse, dy, bias, gg[:, 2 * D_CONV:])
        dparts = [dpa, dq, dk, dv]
        gw_in = matmul("wgrad_in", dparts, xn, "tn", BF16, 5 * D_ATTN, D_MODEL, s, 512, D_MODEL, min(s, 2048))
        kept_a, swap_a = reduce_begin(f"reduce_a{l}_s", [split_rows("split_in", gw_in), gw_out])
        if l > 0:
            dh, dffo, small["pre_mix_g"][l], small["post_ffn_g"][l - 1] = matmul_norm_bwd(
                "dgrad_in", dparts, wt_in, 512, h_in, vec(pre_mix_g), dh_mid,
                then=(saved[l - 1][15], post_ffn_g[l - 1].reshape(1, -1), BF16), chunk=32, deps=[swap_a["token"]])
        else:
            dh, small["pre_mix_g"][l] = matmul_norm_bwd("dgrad_in_first", dparts, wt_in, 512, h_in, vec(pre_mix_g),
                                                        dh_mid, chunk=32, deps=[swap_a["token"]])
        relayed[l, "a"] = reduce_relay(f"reduce_a{l}_r", kept_a, swap_a, dh)
        carried = [relayed[l, "a"]["token"]]
        small["conv_w"][l] = dcw
        small["pool_w"][l] = jnp.stack([dbd[gi * POOL_GC:(gi + 1) * POOL_GC, gi * POOL_GC:(gi + 1) * POOL_GC]
                                        for gi in range(N_POOL)])
        small["pool_scale"][l] = dps
        small["rel_bias"][l] = rel_bias_grad(by_off)
        small["group_gain"][l] = jnp.concatenate([dgg_ab, dgg_c], axis=1)
    grad_x = dh.reshape(x.shape)

    small_params = dict(pool_w=pool_w, pool_scale=pool_scale, rel_bias=rel_bias, group_gain=group_gain,
                        pre_mix_g=pre_mix_g, post_mix_g=post_mix_g, pre_ffn_g=pre_ffn_g, post_ffn_g=post_ffn_g)
    small_m = dict(pool_w=m_pool_w, pool_scale=m_pool_scale, rel_bias=m_rel_bias, group_gain=m_group_gain,
                   pre_mix_g=m_pre_mix_g, post_mix_g=m_post_mix_g, pre_ffn_g=m_pre_ffn_g, post_ffn_g=m_post_ffn_g)
    small_v = dict(pool_w=v_pool_w, pool_scale=v_pool_scale, rel_bias=v_rel_bias, group_gain=v_group_gain,
                   pre_mix_g=v_pre_mix_g, post_mix_g=v_post_mix_g, pre_ffn_g=v_pre_ffn_g, post_ffn_g=v_post_ffn_g)
    shapes = [small_params[k].shape for k in _SMALL] + [(depth, 3, D_CONV), (1,)]
    n_small = sum(int(np.prod(shp)) for shp in shapes)
    rows = -(-n_small // (8 * D_MODEL)) * 8
    extra = [jnp.zeros((depth, 3, D_CONV), F32), jnp.zeros((1,), F32)]
    grads_packed = _pack([jnp.stack(small[k]) for k in _SMALL] + [jnp.stack(small["conv_w"]), loss_part[0, 0:1]], rows)
    small_started = gather_start("gather_small_s", [grads_packed], carried)

    big = dict(w_in=(w_in_t, tr(m_w_in), tr(v_w_in)), w_out=(w_out, m_w_out, v_w_out),
               w_gate_up=(w_gu_t, tr(m_w_gate_up), tr(v_w_gate_up)), w_down=(w_down, m_w_down, v_w_down))
    results = dict(w_in=None, w_out=None, w_gate_up=None, w_down=None)
    after = small_started["token"]
    small_relay = None
    for l in reversed(range(depth)):
        slabs_gu, slabs_dn = reduce_finish(f"reduce_b{l}_f", relayed[l, "b"], after)
        slabs_in, slabs_out = reduce_finish(f"reduce_a{l}_f", relayed[l, "a"], slabs_dn[1])
        for k, slabs in (("w_gate_up", slabs_gu), ("w_down", slabs_dn), ("w_in", slabs_in), ("w_out", slabs_out)):
            results[k] = adamw_layer("adamw_" + k, l, *big[k], *slabs, results[k])
        after = results["w_out"][0]
        if l == min(1, depth - 1):
            small_relay = gather_relay("gather_small_r", small_started, after)
    small_parts = gather_finish("gather_small_f", small_relay, after)[0]
    all_small = assemble_rows("assemble_small", *small_parts).reshape(8, rows, D_MODEL)
    res_small = adamw("adamw_small", _pack([small_params[k] for k in _SMALL] + extra, rows),
                      _pack([small_m[k] for k in _SMALL] + extra, rows),
                      _pack([small_v[k] for k in _SMALL] + extra, rows), all_small)
    g_s, d_s, m_s, v_s = (_unpack(r, shapes) for r in res_small)
    loss = g_s[-1][0]
    g_conv = lax.dynamic_slice_in_dim(g_s[-2], dev * conv_w.shape[2], conv_w.shape[2], axis=2)
    results["conv_w"] = adamw("adamw_conv", conv_w, m_conv_w, v_conv_w, g_conv[None])

    names = ("w_in", "w_out", "conv_w") + _SMALL + ("w_gate_up", "w_down")
    for k in ("w_in", "w_gate_up"):
        results[k] = tuple(tr(r) for r in results[k])
    for j, k in enumerate(_SMALL):
        results[k] = (g_s[j], d_s[j], m_s[j], v_s[j])
    return (loss, grad_x, *[results[k][0] for k in names], *[results[k][1] for k in names],
            *[results[k][2] for k in names], *[results[k][3] for k in names])
```

```python
import numpy as np
import jax
import jax.numpy as jnp
from jax import lax
from jax.experimental import pallas as pl
from jax.experimental.pallas import tpu as pltpu

F32 = jnp.float32
BF16 = jnp.bfloat16

CHUNK = 64
D_MODEL = 1024
D_CONV = 256
D_POOL = 256
D_ATTN = 512
HEAD_DIM = 64
N_HEADS = 8
N_POOL = 4
POOL_GC = 64
POOL_WINDOWS = (2, 4, 8, 16)
LEFT_CHUNKS = 8
REL_CLIP = 128
D_FF = 2816
EPS = 1e-6
ADAM_LR, ADAM_B1, ADAM_B2, ADAM_EPS, ADAM_WD, ADAM_STEP = 0.001, 0.9, 0.999, 1e-08, 0.01, 10

QB = 256
KB = 3 * QB
HALO = 16
T_CP = 512
T_ROW = 512
QKV_COL = 2
NEG = -1e30
VMEM_MB = 1 << 20
MESH = pl.DeviceIdType.MESH


def _call(body, **kw):
    call = pl.pallas_call(body, **kw)
    return lambda *args: call(*[_in_hbm(a) for a in args])


def _in_hbm(a):
    return pltpu.with_memory_space_constraint(a, pltpu.HBM) if jnp.issubdtype(a.dtype, jnp.number) else a


def _call_after(deps, n_in, body, **kw):
    deps = tuple(deps)
    if not deps:
        return _call(body, **kw)

    def ordered(*refs):
        body(*refs[:n_in], *refs[n_in + len(deps):])

    kw["in_specs"] = list(kw["in_specs"]) + [pl.BlockSpec(memory_space=pl.ANY)] * len(deps)
    call = _call(ordered, **kw)
    return lambda *args: call(*args, *deps)


def _params(sem, vmem_mb=48):
    return pltpu.CompilerParams(dimension_semantics=sem, vmem_limit_bytes=vmem_mb * VMEM_MB)


_CHIP_FLIPS = ((1, 0), (0, 1), (1, 1))
_HBM = pl.BlockSpec(memory_space=pltpu.HBM)
_SEM = pl.BlockSpec(memory_space=pltpu.SEMAPHORE)
_EFFECT = pltpu.SideEffectType.DATAFLOW_SIDE_EFFECTING


def _flip(v, f):
    return 1 - v if f else v


def _descriptors(plan, srcs, lands, send_sems, recv_sems, sending):
    x, y, c = lax.axis_index("x"), lax.axis_index("y"), lax.axis_index("c")
    return [pltpu.make_async_remote_copy(src_ref=src, dst_ref=dst if sending else land, send_sem=send_sems.at[k],
                                         recv_sem=recv_sems.at[k], device_id=peer, device_id_type=MESH)
            for k, (src, dst, peer, land) in enumerate(plan(srcs, lands, x, y, c))]


def exchange_start(name, plan, n_copies, srcs, land_shapes, deps=()):
    ns, nl = len(srcs), len(land_shapes)

    def body(*refs):
        src_refs, land_refs = refs[:ns], refs[ns:ns + nl]
        send_sems, recv_sems = refs[ns + nl], refs[ns + nl + 1]
        for send in _descriptors(plan, src_refs, land_refs, send_sems, recv_sems, True):
            send.start()
        refs[-1][...] = jnp.zeros_like(refs[-1])

    lands = [lax.empty(shape, dtype) for shape, dtype in land_shapes]
    outs = _call_after(
        deps, ns + nl, body, name=name,
        out_shape=(pltpu.SemaphoreType.DMA((n_copies,)), pltpu.SemaphoreType.DMA((n_copies,)),
                   *[pltpu.HBM(a.shape, a.dtype) for a in srcs], *[pltpu.HBM(shape, dtype) for shape, dtype in land_shapes],
                   jax.ShapeDtypeStruct((8, 128), F32)),
        in_specs=[_HBM] * (ns + nl),
        out_specs=(_SEM, _SEM, *[_HBM] * (ns + nl), pl.BlockSpec(memory_space=pltpu.VMEM)),
        input_output_aliases={j: 2 + j for j in range(ns + nl)},
        compiler_params=pltpu.CompilerParams(has_side_effects=_EFFECT),
    )(*srcs, *lands)
    return dict(plan=plan, sems=outs[:2], srcs=outs[2:2 + ns], lands=outs[2 + ns:2 + ns + nl], token=outs[-1])


def exchange_wait(name, started, after):
    srcs, lands = started["srcs"], started["lands"]
    ns, nl = len(srcs), len(lands)

    def body(*refs):
        src_refs, land_refs = refs[:ns], refs[ns:ns + nl]
        send_sems, recv_sems = refs[ns + nl], refs[ns + nl + 1]
        for wait in _descriptors(started["plan"], src_refs, land_refs, send_sems, recv_sems, False):
            wait.wait_send()
            wait.wait_recv()

    outs = _call(
        body, name=name,
        out_shape=tuple(pltpu.HBM(a.shape, a.dtype) for a in (*srcs, *lands)),
        in_specs=[_HBM] * (ns + nl) + [_SEM, _SEM, pl.BlockSpec(memory_space=pl.ANY)],
        out_specs=tuple([_HBM] * (ns + nl)),
        input_output_aliases={j: j for j in range(ns + nl)},
        compiler_params=pltpu.CompilerParams(has_side_effects=_EFFECT),
    )(*srcs, *lands, *started["sems"], after)
    return list(outs[:ns]), list(outs[ns:])


def plan_from_chips(srcs, lands, x, y, c):
    q = 2 * x + y
    out = []
    for src, land in zip(srcs, lands):
        for fx, fy in _CHIP_FLIPS:
            px, py = _flip(x, fx), _flip(y, fy)
            out.append((src, land.at[q], (px, py, c), land.at[2 * px + py]))
    return out


def plan_to_core(srcs, lands, x, y, c):
    n = len(lands)
    q = 2 * x + y
    out = []
    for own, chips, land in zip(srcs[:n], srcs[n:], lands):
        out.append((own, land.at[q], (x, y, 1 - c), land.at[q]))
        for fx, fy in _CHIP_FLIPS:
            qp = 2 * _flip(x, fx) + _flip(y, fy)
            out.append((chips.at[qp], land.at[qp], (x, y, 1 - c), land.at[qp]))
    return out


def plan_swap_cores(srcs, lands, x, y, c):
    return [(src, land, (x, y, 1 - c), land) for src, land in zip(srcs, lands)]


def plan_to_chips(srcs, lands, x, y, c):
    q = 2 * x + y
    out = []
    for src, land in zip(srcs, lands):
        for fx, fy in _CHIP_FLIPS:
            px, py = _flip(x, fx), _flip(y, fy)
            out.append((src.at[2 * px + py], land.at[q], (px, py, c), land.at[2 * px + py]))
    return out


def _slots(a):
    return ((4,) + a.shape, a.dtype)


def gather_start(name, arrays, deps=()):
    return exchange_start(name, plan_from_chips, 3 * len(arrays), arrays, [_slots(a) for a in arrays], deps)


def gather_relay(name, started, after):
    own, chips = exchange_wait(name + "_w", started, after)
    return exchange_start(name + "_s", plan_to_core, 4 * len(own), own + chips, [_slots(a) for a in own])


def gather_finish(name, relayed, after):
    srcs, cores = exchange_wait(name, relayed, after)
    n = len(cores)
    return list(zip(srcs[:n], srcs[n:], cores))


def _place():
    return 2 * lax.axis_index("x") + lax.axis_index("y"), lax.axis_index("c")


def assemble_cols(name, own, chips, core):
    r, c = own.shape
    t = _row_tile(r, 256)

    def body(own_ref, chips_ref, core_ref, o_ref):
        q_me, c_me = _place()
        for q in range(4):
            mine = jnp.where(q == q_me, own_ref[...], chips_ref[q])
            other = core_ref[q]
            o_ref[:, 2 * q * c:(2 * q + 1) * c] = jnp.where(c_me == 0, mine, other)
            o_ref[:, (2 * q + 1) * c:(2 * q + 2) * c] = jnp.where(c_me == 0, other, mine)

    slots = pl.BlockSpec((4, t, c), lambda i: (0, i, 0))
    return _call(
        body, name=name, grid=(r // t,),
        in_specs=[pl.BlockSpec((t, c), lambda i: (i, 0)), slots, slots],
        out_specs=pl.BlockSpec((t, 8 * c), lambda i: (i, 0)),
        out_shape=jax.ShapeDtypeStruct((r, 8 * c), own.dtype), compiler_params=_params(("parallel",)),
    )(own, chips, core)


def assemble_rows(name, own, chips, core):
    r, c = own.shape

    def body(own_ref, chips_ref, core_ref, o_ref):
        q_me, c_me = _place()
        d = pl.program_id(0)
        mine = jnp.where(d // 2 == q_me, own_ref[...], chips_ref[...])
        o_ref[...] = jnp.where(d % 2 == c_me, mine, core_ref[...])

    slot = pl.BlockSpec((None, r, c), lambda d: (d // 2, 0, 0))
    return _call(
        body, name=name, grid=(8,),
        in_specs=[pl.BlockSpec((r, c), lambda d: (0, 0)), slot, slot],
        out_specs=pl.BlockSpec((r, c), lambda d: (d, 0)),
        out_shape=jax.ShapeDtypeStruct((8 * r, c), own.dtype), compiler_params=_params(("parallel",)),
    )(own, chips, core)


def split_rows(name, dw):
    r8, c = dw.shape
    r = r8 // 8

    def body(dw_ref, keep_ref, send_ref):
        _, c_me = _place()
        d = pl.program_id(0)

        @pl.when(d % 2 == c_me)
        def _():
            keep_ref[...] = dw_ref[...]

        @pl.when(d % 2 != c_me)
        def _():
            send_ref[...] = dw_ref[...]

    slot = pl.BlockSpec((None, r, c), lambda d: (d // 2, 0, 0))
    return _call(
        body, name=name, grid=(8,),
        in_specs=[pl.BlockSpec((r, c), lambda d: (d, 0))], out_specs=[slot, slot],
        out_shape=[jax.ShapeDtypeStruct((4, r, c), dw.dtype)] * 2, compiler_params=_params(("arbitrary",)),
    )(dw)


def _rms(v):
    return lax.rsqrt(jnp.mean(v * v, axis=-1, keepdims=True) + EPS)


def rmsnorm_fwd(name, x, g, out_dtype):
    s, d = x.shape

    def body(x_ref, g_ref, o_ref):
        xv = x_ref[...]
        o_ref[...] = ((xv * _rms(xv)) * g_ref[...]).astype(o_ref.dtype)

    row = pl.BlockSpec((T_ROW, d), lambda i: (i, 0))
    return _call(
        body, name=name, grid=(s // T_ROW,),
        in_specs=[row, pl.BlockSpec((1, d), lambda i: (0, 0))], out_specs=row,
        out_shape=jax.ShapeDtypeStruct((s, d), out_dtype), compiler_params=_params(("parallel",)),
    )(x, g)


def matmul_then(name, a_parts, b, mode, tm, chunk, rows_in, vecs_in, rows_out, n_sums, then, deps=()):
    a_parts = list(a_parts)
    na, nr, nv, no = len(a_parts), len(rows_in), len(vecs_in), len(rows_out)
    s = a_parts[0].shape[0]
    n = s // tm
    d = b.shape[1] if mode == "nn" else b.shape[0]
    offs = [0]
    for p in a_parts:
        offs.append(offs[-1] + p.shape[1])
    n_in = na + 1 + nr + nv

    def body(*refs):
        a_refs, b_ref = refs[:na], refs[na]
        row_refs, vec_refs = refs[na + 1:na + 1 + nr], refs[na + 1 + nr:n_in]
        out_refs, sum_refs = refs[n_in:n_in + no], refs[n_in + no:n_in + no + n_sums]
        acc = refs[-2:]
        i = pl.program_id(0)

        @pl.when(i == 0)
        def _():
            acc[1][...] = jnp.zeros_like(acc[1])
            for s_ref in sum_refs:
                s_ref[...] = jnp.zeros_like(s_ref)

        def step(write, read):
            t = None
            for p in range(na):
                lo, hi = offs[p], offs[p + 1]
                part = _dot(a_refs[p][...], b_ref[lo:hi, :] if mode == "nn" else b_ref[:, lo:hi], mode)
                t = part if t is None else t + part
            write[...] = t
            totals = [0.0] * n_sums
            for r0 in range(0, tm, chunk):
                rs = pl.ds(r0, chunk)
                results, sums = then(read[rs, :], [r.at[rs, :] for r in row_refs], vec_refs)
                for o_ref, val in zip(out_refs, results):
                    o_ref[rs, :] = val.astype(o_ref.dtype)
                totals = [tot + val for tot, val in zip(totals, sums)]
            for s_ref, val in zip(sum_refs, totals):
                s_ref[...] += jnp.where(i > 0, val, 0.0)

        pl.when(i % 2 == 0)(lambda: step(acc[0], acc[1]))
        pl.when(i % 2 == 1)(lambda: step(acc[1], acc[0]))

    ahead = lambda i: (jnp.minimum(i, n - 1), 0)
    behind = lambda i: (jnp.maximum(i - 1, 0), 0)
    fixed = lambda i: (0, 0)
    shapes = [jax.ShapeDtypeStruct((s, d), dt) for dt in rows_out] + [jax.ShapeDtypeStruct((1, d), F32)] * n_sums
    return _call_after(
        deps, n_in, body, name=name, grid=(n + 1,),
        in_specs=[pl.BlockSpec((tm, p.shape[1]), ahead) for p in a_parts] + [pl.BlockSpec(b.shape, fixed)]
        + [pl.BlockSpec((tm, r.shape[1]), behind) for r in rows_in] + [pl.BlockSpec(v.shape, fixed) for v in vecs_in],
        out_specs=[pl.BlockSpec((tm, d), behind)] * no + [pl.BlockSpec((1, d), fixed)] * n_sums,
        out_shape=shapes, scratch_shapes=[pltpu.VMEM((tm, d), F32)] * 2,
        compiler_params=_params(("arbitrary",), 56),
    )(*a_parts, b, *rows_in, *vecs_in)


def matmul_norm(name, a, b, res, g, then=None, deps=()):
    def norm(z, rows, vecs):
        y = rows[0][...] + (z * _rms(z)) * vecs[0][...]
        return [z, y] + ([(y * _rms(y)) * vecs[1][...]] if then else []), []

    return matmul_then(name, [a], b, "nn", 512, 512, [res], [g] + ([then[0]] if then else []),
                       [BF16, F32] + ([then[1]] if then else []), 0, norm, deps)


def matmul_norm_bwd(name, a_parts, b, tm, x, g, res, then=None, chunk=None, deps=()):
    def norms(t, rows, vecs):
        xv = rows[0][...].astype(F32)
        dx, dgt = _norm_bwd(xv, _rms(xv), vecs[0][...], t)
        dx = dx + rows[1][...]
        results, sums = [dx], [jnp.sum(dgt, axis=0, keepdims=True)]
        if then:
            x2 = rows[2][...].astype(F32)
            dx2, dgt2 = _norm_bwd(x2, _rms(x2), vecs[1][...], dx)
            results.append(dx2)
            sums.append(jnp.sum(dgt2, axis=0, keepdims=True))
        return results, sums

    return matmul_then(name, a_parts, b, "nn", tm, chunk or tm, [x, res] + ([then[0]] if then else []),
                       [g] + ([then[1]] if then else []), [F32] + ([then[2]] if then else []),
                       2 if then else 1, norms, deps)


def _norm_bwd(xv, r, g, dy):
    a = dy * g
    dx = r * (a - xv * ((r * r) * jnp.mean(a * xv, axis=-1, keepdims=True)))
    return dx, dy * (xv * r)


def loss_head(name, h, tgt, z, g):
    s, d = h.shape

    def body(h_ref, t_ref, z_ref, g_ref, dh_ref, l_ref, dz_ref, dg_ref):
        e = h_ref[...] - t_ref[...]
        dh = e * (1.0 / d)
        dh_ref[...] = dh
        zv = z_ref[...].astype(F32)
        dz, dgt = _norm_bwd(zv, _rms(zv), g_ref[...], dh)
        dz_ref[...] = dz.astype(dz_ref.dtype)

        @pl.when(pl.program_id(0) == 0)
        def _():
            l_ref[...] = jnp.zeros_like(l_ref)
            dg_ref[...] = jnp.zeros_like(dg_ref)
        part = 0.5 * jnp.sum(jnp.mean(e * e, axis=-1, keepdims=True), axis=0, keepdims=True)
        l_ref[...] += jnp.broadcast_to(part, l_ref.shape)
        dg_ref[...] += jnp.sum(dgt, axis=0, keepdims=True)

    row = pl.BlockSpec((T_ROW, d), lambda i: (i, 0))
    vec = pl.BlockSpec((1, d), lambda i: (0, 0))
    return _call(
        body, name=name, grid=(s // T_ROW,),
        in_specs=[row, row, row, vec], out_specs=[row, pl.BlockSpec((1, 128), lambda i: (0, 0)), row, vec],
        out_shape=[jax.ShapeDtypeStruct((s, d), F32), jax.ShapeDtypeStruct((1, 128), F32),
                   jax.ShapeDtypeStruct((s, d), BF16), jax.ShapeDtypeStruct((1, d), F32)],
        compiler_params=_params(("arbitrary",)),
    )(h, tgt, z, g)


def _row_tile(rows, limit=512):
    t = min(rows, limit)
    while rows % t or (t % 8 and t != rows):
        t -= 1
    return t


def add_pairs(name, a, b):
    shape = a.shape
    a2, b2 = a.reshape(-1, shape[-1]), b.reshape(-1, shape[-1])
    rows, cols = a2.shape
    t = _row_tile(rows)

    def body(a_ref, b_ref, o_ref):
        o_ref[...] = (a_ref[...].astype(F32) + b_ref[...].astype(F32)).astype(o_ref.dtype)

    blk = pl.BlockSpec((t, cols), lambda i: (i, 0))
    out = _call(
        body, name=name, grid=(rows // t,), in_specs=[blk, blk], out_specs=blk,
        out_shape=jax.ShapeDtypeStruct((rows, cols), a.dtype), compiler_params=_params(("parallel",)),
    )(a2, b2)
    return out.reshape(shape)


def _adamw_math(w, m, v, g):
    c1 = 1.0 - ADAM_B1 ** ADAM_STEP
    c2 = 1.0 - ADAM_B2 ** ADAM_STEP
    mn = ADAM_B1 * m + (1.0 - ADAM_B1) * g
    vn = ADAM_B2 * v + (1.0 - ADAM_B2) * (g * g)
    delta = -ADAM_LR * ((mn / c1) / (jnp.sqrt(vn / c2) + ADAM_EPS) + ADAM_WD * w)
    return delta, mn, vn


def _slab_sum(g_ref, n):
    g = g_ref[0].astype(F32)
    for j in range(1, n):
        g = g + g_ref[j].astype(F32)
    return g


def adamw(name, w, m, v, g_slabs, deps=()):
    shape = w.shape
    n = g_slabs.shape[0]
    w2, m2, v2 = (t.reshape(-1, shape[-1]) for t in (w, m, v))
    g3 = g_slabs.reshape(n, -1, shape[-1])
    rows, cols = w2.shape
    t = _row_tile(rows, 256)

    def body(w_ref, m_ref, v_ref, g_ref, go_ref, d_ref, mo_ref, vo_ref):
        g = _slab_sum(g_ref, n)
        go_ref[...] = g
        d_ref[...], mo_ref[...], vo_ref[...] = _adamw_math(w_ref[...], m_ref[...], v_ref[...], g)

    blk = pl.BlockSpec((t, cols), lambda i: (i, 0))
    outs = _call_after(
        deps, 4, body, name=name, grid=(rows // t,),
        in_specs=[blk, blk, blk, pl.BlockSpec((n, t, cols), lambda i: (0, i, 0))], out_specs=[blk] * 4,
        out_shape=[jax.ShapeDtypeStruct((rows, cols), F32)] * 4, compiler_params=_params(("parallel",)),
    )(w2, m2, v2, g3)
    return tuple(o.reshape(shape) for o in outs)


def adamw_layer(name, l, w, m, v, own_sums, chip_sums, into):
    _, rows, cols = w.shape
    t = _row_tile(rows, 256)
    if into is None:
        into = tuple(lax.empty(w.shape, F32) for _ in range(4))

    def body(w_ref, m_ref, v_ref, own_ref, far_ref, a0, a1, a2, a3, go_ref, d_ref, mo_ref, vo_ref):
        del a0, a1, a2, a3
        q_me, _ = _place()
        g = jnp.zeros((t, cols), F32)
        for q in range(4):
            g = g + jnp.where(q == q_me, own_ref[q], far_ref[q]).astype(F32)
        go_ref[...] = g
        d_ref[...], mo_ref[...], vo_ref[...] = _adamw_math(w_ref[...], m_ref[...], v_ref[...], g)

    blk = pl.BlockSpec((None, t, cols), lambda i: (l, i, 0))
    slabs = pl.BlockSpec((4, t, cols), lambda i: (0, i, 0))
    anyw = pl.BlockSpec(memory_space=pl.ANY)
    return _call(
        body, name=name, grid=(rows // t,),
        in_specs=[blk, blk, blk, slabs, slabs] + [anyw] * 4, out_specs=[blk] * 4,
        out_shape=[jax.ShapeDtypeStruct(w.shape, F32)] * 4, input_output_aliases={5: 0, 6: 1, 7: 2, 8: 3},
        compiler_params=_params(("parallel",)),
    )(w, m, v, own_sums, chip_sums, *into)


_DIMS = {"nn": (((1,), (0,)), ((), ())), "nt": (((1,), (1,)), ((), ())), "tn": (((0,), (0,)), ((), ()))}


def _dot(a, b, mode="nn"):
    return lax.dot_general(a, b, _DIMS[mode], preferred_element_type=F32)


def matmul(name, a, b, mode, out_dtype, m, n, k, tm, tn, tk, b_off=(0, 0), shard_rows=0, deps=()):
    gm, gn, gk = m // tm, n // tn, k // tk
    assert gm * tm == m and gn * tn == n and gk * tk == k
    r0, c0 = b_off
    a_parts = list(a) if isinstance(a, (list, tuple)) else [a]
    na = len(a_parts)
    tile = tm if mode == "tn" else tk
    spans, at = [], 0
    for p in a_parts:
        nblk = p.shape[1] // tile
        assert nblk * tile == p.shape[1]
        spans.append((at, nblk))
        at += nblk
    assert at == (gm if mode == "tn" else gk)

    def within(t, span):
        return (t >= span[0]) & (t < span[0] + span[1])

    def local(t, span):
        return jnp.clip(t - span[0], 0, span[1] - 1)

    a_specs = []
    for sp in spans:
        if mode == "tn":
            a_specs.append(pl.BlockSpec((tk, tm), lambda i, j, kk, sp=sp: (jnp.where(within(i, sp), kk, 0), local(i, sp))))
        else:
            a_specs.append(pl.BlockSpec((tm, tk), lambda i, j, kk, sp=sp: (i, local(kk, sp))))
    if mode == "nt":
        b_spec = pl.BlockSpec((tn, tk), lambda i, j, kk: (j + r0, kk + c0))
    else:
        b_spec = pl.BlockSpec((tk, tn), lambda i, j, kk: (kk + r0, j + c0))
    o_spec = pl.BlockSpec((tm, tn), lambda i, j, kk: (i, j))
    single = na == 1 and gk == 1 and not shard_rows
    pairs = tm // (2 * shard_rows) if shard_rows else 0
    assert not shard_rows or (pairs * 2 * shard_rows == tm and m == 8 * shard_rows)

    def body(*refs):
        a_refs, b_ref, o_ref = refs[:na], refs[na], refs[na + 1]
        if single:
            o_ref[...] = _dot(a_refs[0][...], b_ref[...], mode).astype(o_ref.dtype)
            return
        acc_ref = refs[-1]
        i, kk = pl.program_id(0), pl.program_id(2)

        @pl.when(kk == 0)
        def _():
            acc_ref[...] = jnp.zeros_like(acc_ref)

        for pa, sp in enumerate(spans):
            def add(pa=pa):
                acc_ref[...] += _dot(a_refs[pa][...], b_ref[...], mode)
            if na > 1:
                pl.when(within(i if mode == "tn" else kk, sp))(add)
            else:
                add()

        @pl.when(kk == gk - 1)
        def _():
            if not shard_rows:
                o_ref[...] = acc_ref[...].astype(o_ref.dtype)
                return
            send_ref, mine = refs[na + 2], lax.axis_index("c") == 0
            for pq in range(pairs):
                even = acc_ref[2 * shard_rows * pq:2 * shard_rows * pq + shard_rows, :]
                odd = acc_ref[2 * shard_rows * pq + shard_rows:2 * shard_rows * (pq + 1), :]
                o_ref[pq] = jnp.where(mine, even, odd).astype(o_ref.dtype)
                send_ref[pq] = jnp.where(mine, odd, even).astype(send_ref.dtype)

    if shard_rows:
        o_spec = pl.BlockSpec((pairs, shard_rows, tn), lambda i, j, kk: (i, 0, j))
    shard = jax.ShapeDtypeStruct((4, shard_rows, n), out_dtype)
    return _call_after(
        deps, na + 1, body, name=name, grid=(gm, gn, gk),
        in_specs=a_specs + [b_spec], out_specs=[o_spec, o_spec] if shard_rows else o_spec,
        out_shape=[shard, shard] if shard_rows else jax.ShapeDtypeStruct((m, n), out_dtype),
        scratch_shapes=[] if single else [pltpu.VMEM((tm, tn), F32)],
        compiler_params=_params(("parallel", "parallel", "arbitrary")),
    )(*a_parts, b)


def matmul_swiglu(name, a, w_gu_t, deps=()):
    s, d = a.shape
    tm, tn = 512, 1408
    gn = D_FF // tn

    def body(a_ref, wg_ref, wu_ref, g_ref, u_ref, f_ref):
        av = a_ref[...]
        g = _dot(av, wg_ref[...], "nt")
        u = _dot(av, wu_ref[...], "nt")
        g_ref[...] = g.astype(g_ref.dtype)
        u_ref[...] = u.astype(u_ref.dtype)
        f_ref[...] = ((g * (1.0 / (1.0 + jnp.exp(-g)))) * u).astype(f_ref.dtype)

    o_spec = pl.BlockSpec((tm, tn), lambda j, i: (i, j))
    return _call_after(
        deps, 3, body, name=name, grid=(gn, s // tm),
        in_specs=[pl.BlockSpec((tm, d), lambda j, i: (i, 0)), pl.BlockSpec((tn, d), lambda j, i: (j, 0)),
                  pl.BlockSpec((tn, d), lambda j, i: (j + gn, 0))],
        out_specs=[o_spec] * 3, out_shape=[jax.ShapeDtypeStruct((s, D_FF), BF16)] * 3,
        compiler_params=_params(("parallel", "parallel")),
    )(a, w_gu_t, w_gu_t)


def matmul_swiglu_bwd(name, dffo, w_down, g, u):
    s, d = dffo.shape
    tm, tn = 512, 1408

    def body(a_ref, w_ref, g_ref, u_ref, dg_ref, du_ref):
        av = a_ref[...]
        for lo in range(0, tn, 256):
            cols = slice(lo, min(lo + 256, tn))
            dff = _dot(av, w_ref[cols, :], "nt")
            gv = g_ref[:, cols].astype(F32)
            sig = 1.0 / (1.0 + jnp.exp(-gv))
            silu = gv * sig
            du_ref[:, cols] = (dff * silu).astype(du_ref.dtype)
            dg_ref[:, cols] = ((dff * u_ref[:, cols].astype(F32)) * (sig + silu * (1.0 - sig))).astype(dg_ref.dtype)

    o_spec = pl.BlockSpec((tm, tn), lambda j, i: (i, j))
    return _call(
        body, name=name, grid=(D_FF // tn, s // tm),
        in_specs=[pl.BlockSpec((tm, d), lambda j, i: (i, 0)), pl.BlockSpec((tn, d), lambda j, i: (j, 0)), o_spec, o_spec],
        out_specs=[o_spec] * 2, out_shape=[jax.ShapeDtypeStruct((s, D_FF), BF16)] * 2,
        compiler_params=_params(("parallel", "parallel")),
    )(dffo, w_down, g, u)


def _down(v, n):
    return pltpu.roll(v, n, 0)


def _up(v, n):
    return pltpu.roll(v, v.shape[0] - n, 0)


def _by_window(lane, v2, v4, v8, v16):
    return jnp.where(lane < POOL_GC, v2, jnp.where(lane < 2 * POOL_GC, v4, jnp.where(lane < 3 * POOL_GC, v8, v16)))


def _taps(cw_ref):
    return cw_ref[0:1, :], cw_ref[1:2, :], cw_ref[2:3, :]


def _conv_pool_forward(ext, t0, cw, bd):
    n_out = ext.shape[0] - HALO
    gb = ext[HALO:, 0:D_CONV]
    z = ext[:, D_CONV:2 * D_CONV] * ext[:, 2 * D_CONV:3 * D_CONV]
    z0, z1, z2 = z[HALO:], _down(z, 1)[HALO:], _down(z, 2)[HALO:]
    conv = cw[2] * z0 + cw[1] * z1 + cw[0] * z2
    x = ext[:, 3 * D_CONV:]
    w2 = x + _down(x, 1)
    w4 = w2 + _down(w2, 2)
    w8 = w4 + _down(w4, 4)
    w16 = w8 + _down(w8, 8)
    lane = lax.broadcasted_iota(jnp.int32, (1, D_POOL), 1)
    win = _by_window(lane, 2.0, 4.0, 8.0, 16.0)
    pos = (t0 + lax.broadcasted_iota(jnp.int32, (n_out, 1), 0) + 1).astype(F32)
    cnt = jnp.minimum(pos, win)
    d = _by_window(lane, w2, w4, w8, w16)[HALO:] / cnt - x[HALO:]
    ybp = _dot(d.astype(BF16), bd)
    return gb, z0, z1, z2, conv, d, cnt, ybp, lane


def conv_pool_fwd(name, proj_a, cw, bd, ps, gg, deps=()):
    s = proj_a.shape[0]
    t = T_CP
    hb = t // HALO

    def body(main_ref, prev_ref, cw_ref, bd_ref, ps_ref, gg_ref, y_ref):
        i = pl.program_id(0)
        prev = jnp.where(i > 0, prev_ref[...].astype(F32), 0.0)
        ext = jnp.concatenate([prev, main_ref[...].astype(F32)], axis=0)
        gb, _, _, _, conv, _, _, ybp, _ = _conv_pool_forward(ext, i * t, _taps(cw_ref), bd_ref[...])
        ya = gb * conv
        yb = ybp * ps_ref[...]
        ggv = gg_ref[...]
        y_ref[:, 0:D_CONV] = ((ya * _rms(ya)) * ggv[:, 0:D_CONV]).astype(y_ref.dtype)
        y_ref[:, D_CONV:] = ((yb * _rms(yb)) * ggv[:, D_CONV:]).astype(y_ref.dtype)

    full = lambda shape: pl.BlockSpec(shape, lambda i: (0,) * len(shape))
    return _call_after(
        deps, 6, body, name=name, grid=(s // t,),
        in_specs=[pl.BlockSpec((t, D_MODEL), lambda i: (i, 0)),
                  pl.BlockSpec((HALO, D_MODEL), lambda i: (jnp.maximum(i * hb - 1, 0), 0)),
                  full((3, D_CONV)), full((D_POOL, D_POOL)), full((1, D_POOL)), full((1, 2 * D_CONV))],
        out_specs=pl.BlockSpec((t, 2 * D_CONV), lambda i: (i, 0)),
        out_shape=jax.ShapeDtypeStruct((s, D_MODEL), BF16),
        compiler_params=_params(("parallel",)),
    )(proj_a, proj_a, cw, bd, ps, gg)


def conv_pool_bwd(name, proj_a, dy, cw, bd, ps, gg):
    s = proj_a.shape[0]
    t = T_CP
    hb = t // HALO
    nblk = s // t
    last_halo = s // HALO - 1

    def body(main_ref, prev_ref, next_ref, dy_ref, dyn_ref, cw_ref, bd_ref, ps_ref, gg_ref,
             dp_ref, dcw_ref, dbd_ref, dps_ref, dgg_ref):
        i = pl.program_id(0)
        prev = jnp.where(i > 0, prev_ref[...].astype(F32), 0.0)
        main = main_ref[...].astype(F32)
        ext = jnp.concatenate([prev, main, next_ref[...].astype(F32)], axis=0)
        cwv, bdv, psv, ggv = _taps(cw_ref), bd_ref[...], ps_ref[...], gg_ref[...]
        gb, z0, z1, z2, conv, d, cnt, ybp, lane = _conv_pool_forward(ext, i * t, cwv, bdv)
        dyn = jnp.where(i < nblk - 1, dyn_ref[...].astype(F32), 0.0)
        dyv = jnp.concatenate([dy_ref[...].astype(F32), dyn], axis=0)
        ya = gb * conv
        yb = ybp * psv
        dya, dgg_a = _norm_bwd(ya, _rms(ya), ggv[:, 0:D_CONV], dyv[:, 0:D_CONV])
        dyb, dgg_b = _norm_bwd(yb, _rms(yb), ggv[:, D_CONV:], dyv[:, D_CONV:])

        dconv = dya * gb
        dz = (cwv[2] * dconv + cwv[1] * _up(dconv, 1) + cwv[0] * _up(dconv, 2))[:t]
        dp_ref[:, 0:D_CONV] = (dya * conv)[:t].astype(dp_ref.dtype)
        dp_ref[:, D_CONV:2 * D_CONV] = (dz * main[:, 2 * D_CONV:3 * D_CONV]).astype(dp_ref.dtype)
        dp_ref[:, 2 * D_CONV:3 * D_CONV] = (dz * main[:, D_CONV:2 * D_CONV]).astype(dp_ref.dtype)

        dybs = dyb * psv
        dd = _dot(dybs.astype(BF16), bdv, "nt")
        e = dd / cnt
        a2 = e + _up(e, 1)
        a4 = a2 + _up(a2, 2)
        a8 = a4 + _up(a4, 4)
        a16 = a8 + _up(a8, 8)
        dp_ref[:, 3 * D_CONV:] = (_by_window(lane, a2, a4, a8, a16) - dd)[:t].astype(dp_ref.dtype)

        @pl.when(i == 0)
        def _():
            dcw_ref[...] = jnp.zeros_like(dcw_ref)
            dbd_ref[...] = jnp.zeros_like(dbd_ref)
            dps_ref[...] = jnp.zeros_like(dps_ref)
            dgg_ref[...] = jnp.zeros_like(dgg_ref)

        rsum = lambda v: jnp.sum(v[:t], axis=0, keepdims=True)
        dcw_ref[0:1, :] += rsum(dconv * z2)
        dcw_ref[1:2, :] += rsum(dconv * z1)
        dcw_ref[2:3, :] += rsum(dconv * z0)
        dbd_ref[...] += _dot(d[:t].astype(BF16), dybs[:t].astype(BF16), "tn")
        dps_ref[...] += rsum(dyb * ybp)
        dgg_ref[:, 0:D_CONV] += rsum(dgg_a)
        dgg_ref[:, D_CONV:] += rsum(dgg_b)

    full = lambda shape: pl.BlockSpec(shape, lambda i: (0,) * len(shape))
    next_halo = lambda i: (jnp.minimum((i + 1) * hb, last_halo), 0)
    return _call(
        body, name=name, grid=(nblk,),
        in_specs=[pl.BlockSpec((t, D_MODEL), lambda i: (i, 0)),
                  pl.BlockSpec((HALO, D_MODEL), lambda i: (jnp.maximum(i * hb - 1, 0), 0)),
                  pl.BlockSpec((HALO, D_MODEL), next_halo),
                  pl.BlockSpec((t, 2 * D_CONV), lambda i: (i, 0)),
                  pl.BlockSpec((HALO, 2 * D_CONV), next_halo),
                  full((3, D_CONV)), full((D_POOL, D_POOL)), full((1, D_POOL)), full((1, 2 * D_CONV))],
        out_specs=[pl.BlockSpec((t, D_MODEL), lambda i: (i, 0)),
                   full((3, D_CONV)), full((D_POOL, D_POOL)), full((1, D_POOL)), full((1, 2 * D_CONV))],
        out_shape=[jax.ShapeDtypeStruct((s, D_MODEL), BF16), jax.ShapeDtypeStruct((3, D_CONV), F32),
                   jax.ShapeDtypeStruct((D_POOL, D_POOL), F32), jax.ShapeDtypeStruct((1, D_POOL), F32),
                   jax.ShapeDtypeStruct((1, 2 * D_CONV), F32)],
        compiler_params=_params(("arbitrary",)),
    )(proj_a, proj_a, proj_a, dy, dy, cw, bd, ps, gg)


def bias_by_offset(rel_bias):
    n_far = 2 * QB - REL_CLIP + 1
    far = jnp.broadcast_to(rel_bias[:, 2 * REL_CLIP:], (N_HEADS, n_far))
    mid = rel_bias[:, 1:2 * REL_CLIP][:, ::-1]
    near = jnp.broadcast_to(rel_bias[:, 0:1], (N_HEADS, KB - n_far - (2 * REL_CLIP - 1)))
    wrap = jnp.broadcast_to(rel_bias[:, 2 * REL_CLIP:], (N_HEADS, 4 * QB - KB))
    return jnp.concatenate([far, mid, near, wrap], axis=1)


def _fill_bias(off_ref, b_ref):
    width = 4 * QB
    sub = lax.broadcasted_iota(jnp.int32, (8, 1), 0)
    col = lax.broadcasted_iota(jnp.int32, (1, KB), 1)
    for h in range(N_HEADS):
        base = jnp.broadcast_to(off_ref[h:h + 1, :], (8, width))
        for bit in range(3):
            base = jnp.where(((sub >> bit) & 1) == 1, pltpu.roll(base, 1 << bit, 1), base)
        for a in range(QB // 8):
            first = CHUNK * (8 * a // CHUNK)
            rows = (pltpu.roll(base, 8 * a, 1) if a else base)[:, :KB]
            band = (col >= first) & (col < first + (LEFT_CHUNKS + 1) * CHUNK)
            b_ref[h, 8 * a:8 * a + 8, :] = jnp.where(band, rows, NEG)


def rel_bias_grad(by_offset):
    n_far = 2 * QB - REL_CLIP + 1
    hi = jnp.sum(by_offset[:, :n_far], axis=1, keepdims=True) + jnp.sum(by_offset[:, KB:], axis=1, keepdims=True)
    mid = by_offset[:, n_far:n_far + 2 * REL_CLIP - 1][:, ::-1]
    lo = jnp.sum(by_offset[:, n_far + 2 * REL_CLIP - 1:KB], axis=1, keepdims=True)
    return jnp.concatenate([lo, mid, hi], axis=1)


def _head_masks():
    first = lax.broadcasted_iota(jnp.int32, (1, 2 * HEAD_DIM), 1) < HEAD_DIM
    return first, jnp.logical_not(first)


def _pick_lane(tile, h):
    lane = lax.broadcasted_iota(jnp.int32, (1, tile.shape[1]), 1)
    return jnp.sum(jnp.where(lane == h, tile, 0.0), axis=-1, keepdims=True)


def attention_fwd(name, qkv, by_offset, gg, y_ab):
    s = qkv.shape[0]
    nq = s // QB
    scale = HEAD_DIM ** -0.5

    def body(q_ref, k0, k1, k2, v0, v1, v2, off_ref, gg_ref, y_in, y_ref, o_ref, lse_ref, b_ref):
        del y_in
        i = pl.program_id(0)

        @pl.when(i == 0)
        def _():
            _fill_bias(off_ref, b_ref)
        def block(at_start):
            kb = jnp.concatenate([k0[...], k1[...], k2[...]], axis=0)
            vb = jnp.concatenate([v0[...], v1[...], v2[...]], axis=0)
            valid = lax.broadcasted_iota(jnp.int32, (1, KB), 1) >= (2 - i) * QB
            lane = lax.broadcasted_iota(jnp.int32, (1, 128), 1)
            masks = _head_masks()
            lse = jnp.zeros((QB, 128), F32)
            outs = []
            for hp in range(N_HEADS // 2):
                sl = slice(2 * HEAD_DIM * hp, 2 * HEAD_DIM * (hp + 1))
                q_p, k_p, v_p = q_ref[:, sl] * scale, kb[:, sl], vb[:, sl]
                o_pair = jnp.zeros((QB, 2 * HEAD_DIM), F32)
                for a in range(2):
                    h = 2 * hp + a
                    sc = _dot(jnp.where(masks[a], q_p, 0), k_p, "nt") + b_ref[h]
                    if at_start:
                        sc = jnp.where(valid, sc, NEG)
                    mx = jnp.max(sc, axis=-1, keepdims=True)
                    e = jnp.exp(sc - mx)
                    l = jnp.sum(e, axis=-1, keepdims=True)
                    o_pair = o_pair + _dot(e.astype(BF16), jnp.where(masks[a], v_p, 0)) * (1.0 / l)
                    lse = jnp.where(lane == h, mx + jnp.log(l), lse)
                outs.append(o_pair)
            o = jnp.concatenate(outs, axis=1)
            o_ref[...] = o.astype(o_ref.dtype)
            lse_ref[...] = lse
            y_ref[...] = ((o * _rms(o)) * gg_ref[...]).astype(y_ref.dtype)

        pl.when(i < 2)(lambda: block(True))
        pl.when(i >= 2)(lambda: block(False))

    blk = lambda col, back: pl.BlockSpec((QB, D_ATTN), lambda i: (jnp.maximum(i - back, 0), QKV_COL + col))
    return _call(
        body, name=name, grid=(nq,),
        in_specs=[blk(0, 0), blk(1, 2), blk(1, 1), blk(1, 0), blk(2, 2), blk(2, 1), blk(2, 0),
                  pl.BlockSpec((N_HEADS, 4 * QB), lambda i: (0, 0)), pl.BlockSpec((1, D_ATTN), lambda i: (0, 0)),
                  pl.BlockSpec(memory_space=pl.ANY)],
        out_specs=[pl.BlockSpec((QB, D_ATTN), lambda i: (i, 1)), pl.BlockSpec((QB, D_ATTN), lambda i: (i, 0)),
                   pl.BlockSpec((QB, 128), lambda i: (i, 0)), pl.BlockSpec((N_HEADS, QB, KB), lambda i: (0, 0, 0))],
        out_shape=[jax.ShapeDtypeStruct((s, D_MODEL), BF16), jax.ShapeDtypeStruct((s, D_ATTN), BF16),
                   jax.ShapeDtypeStruct((s, 128), F32), jax.ShapeDtypeStruct((N_HEADS, QB, KB), F32)],
        input_output_aliases={9: 0},
        compiler_params=_params(("arbitrary",), 56),
    )(qkv, qkv, qkv, qkv, qkv, qkv, qkv, by_offset, gg, y_ab)


def attention_bwd(name, qkv, o, lse, dy, bias, gg):
    s = qkv.shape[0]
    nq = s // QB
    scale = HEAD_DIM ** -0.5
    width = 4 * QB

    def body(q_ref, k0, k1, k2, v0, v1, v2, o_ref, lse_ref, dy_ref, b_ref, gg_ref,
             dq_ref, dk_ref, dv_ref, off_ref, dgg_ref, dk_acc, dv_acc, db_acc):
        i = pl.program_id(0)

        @pl.when(i == 0)
        def _():
            dk_acc[...] = jnp.zeros_like(dk_acc)
            dv_acc[...] = jnp.zeros_like(dv_acc)
            db_acc[...] = jnp.zeros_like(db_acc)
            dgg_ref[...] = jnp.zeros_like(dgg_ref)

        rows_of = lambda j: pl.ds(pl.multiple_of((j % 3) * QB, QB), QB)

        @pl.when(i > 0)
        def _():
            dk_acc[rows_of(i), :] = jnp.zeros((QB, D_ATTN), F32)
            dv_acc[rows_of(i), :] = jnp.zeros((QB, D_ATTN), F32)

        def block(at_start):
            ov = o_ref[...].astype(F32)
            dyv = dy_ref[...].astype(F32)
            do, dgg_t = _norm_bwd(ov, _rms(ov), gg_ref[...], dyv)
            dgg_ref[...] += jnp.sum(dgg_t, axis=0, keepdims=True)
            kb = jnp.concatenate([k0[...], k1[...], k2[...]], axis=0)
            vb = jnp.concatenate([v0[...], v1[...], v2[...]], axis=0)
            valid = lax.broadcasted_iota(jnp.int32, (1, KB), 1) >= (2 - i) * QB
            masks = _head_masks()
            lse_t = lse_ref[...]
            for hp in range(N_HEADS // 2):
                sl = slice(2 * HEAD_DIM * hp, 2 * HEAD_DIM * (hp + 1))
                q_p, k_p, v_p = q_ref[:, sl] * scale, kb[:, sl], vb[:, sl]
                do_p = do[:, sl]
                prod = do_p * ov[:, sl]
                do_b = do_p.astype(BF16)
                dq_pair = jnp.zeros((QB, 2 * HEAD_DIM), F32)
                dk_pair = jnp.zeros((KB, 2 * HEAD_DIM), F32)
                dv_pair = jnp.zeros((KB, 2 * HEAD_DIM), F32)
                for a in range(2):
                    h = 2 * hp + a
                    q_m = jnp.where(masks[a], q_p, 0)
                    do_m = jnp.where(masks[a], do_b, 0)
                    sc = _dot(q_m, k_p, "nt") + b_ref[h]
                    if at_start:
                        sc = jnp.where(valid, sc, NEG)
                    p = jnp.exp(sc - _pick_lane(lse_t, h))
                    dp = _dot(do_m, v_p, "nt")
                    delta = jnp.sum(jnp.where(masks[a], prod, 0.0), axis=-1, keepdims=True)
                    ds = p * (dp - delta)
                    db_acc[h] += ds
                    ds_b = ds.astype(BF16)
                    dq_pair = dq_pair + _dot(ds_b, jnp.where(masks[a], k_p, 0))
                    dk_pair = dk_pair + _dot(ds_b, q_m, "tn")
                    dv_pair = dv_pair + _dot(p.astype(BF16), do_m, "tn")
                dq_ref[:, sl] = (dq_pair * scale).astype(dq_ref.dtype)
                for t in range(3):
                    dk_acc[rows_of(i + 1 + t), sl] += dk_pair[QB * t:QB * (t + 1)]
                    dv_acc[rows_of(i + 1 + t), sl] += dv_pair[QB * t:QB * (t + 1)]

        pl.when(i < 2)(lambda: block(True))
        pl.when((i >= 2) & (i < nq))(lambda: block(False))

        dk_ref[...] = dk_acc[rows_of(i + 1), :].astype(dk_ref.dtype)
        dv_ref[...] = dv_acc[rows_of(i + 1), :].astype(dv_ref.dtype)

        @pl.when(i == nq + 1)
        def _():
            sub = lax.broadcasted_iota(jnp.int32, (8, 1), 0)
            pad = jnp.zeros((8, width - KB), F32)
            for h in range(N_HEADS):
                v = jnp.concatenate([db_acc[h, 0:8, :], pad], axis=1)
                for a in range(1, QB // 8):
                    grp = jnp.concatenate([db_acc[h, 8 * a:8 * a + 8, :], pad], axis=1)
                    v = v + pltpu.roll(grp, width - 8 * a, 1)
                for bit in range(3):
                    v = jnp.where(((sub >> bit) & 1) == 1, pltpu.roll(v, width - (1 << bit), 1), v)
                off_ref[h:h + 1, :] = jnp.sum(v, axis=0, keepdims=True)

    qi = lambda i: jnp.minimum(i, nq - 1)
    kblk = lambda col, back: pl.BlockSpec((QB, D_ATTN), lambda i: (jnp.clip(i - back, 0, nq - 1), QKV_COL + col))
    qblk = lambda col: pl.BlockSpec((QB, D_ATTN), lambda i: (qi(i), col))
    done = pl.BlockSpec((QB, D_ATTN), lambda i: (jnp.clip(i - 2, 0, nq - 1), 0))
    return _call(
        body, name=name, grid=(nq + 2,),
        in_specs=[qblk(QKV_COL), kblk(1, 2), kblk(1, 1), kblk(1, 0), kblk(2, 2), kblk(2, 1), kblk(2, 0),
                  qblk(0), pl.BlockSpec((QB, 128), lambda i: (qi(i), 0)), qblk(1),
                  pl.BlockSpec((N_HEADS, QB, KB), lambda i: (0, 0, 0)), pl.BlockSpec((1, D_ATTN), lambda i: (0, 0))],
        out_specs=[qblk(0), done, done, pl.BlockSpec((N_HEADS, width), lambda i: (0, 0)),
                   pl.BlockSpec((1, D_ATTN), lambda i: (0, 0))],
        out_shape=[jax.ShapeDtypeStruct((s, D_ATTN), BF16)] * 3
        + [jax.ShapeDtypeStruct((N_HEADS, width), F32), jax.ShapeDtypeStruct((1, D_ATTN), F32)],
        scratch_shapes=[pltpu.VMEM((KB, D_ATTN), F32), pltpu.VMEM((KB, D_ATTN), F32), pltpu.VMEM((N_HEADS, QB, KB), F32)],
        compiler_params=_params(("arbitrary",), 56),
    )(qkv, qkv, qkv, qkv, qkv, qkv, qkv, o, lse, dy, bias, gg)


def _block_diag(pw):
    out = jnp.zeros((D_POOL, D_POOL), pw.dtype)
    for gi in range(N_POOL):
        out = lax.dynamic_update_slice(out, pw[gi], (gi * POOL_GC, gi * POOL_GC))
    return out


_SMALL = ("pool_w", "pool_scale", "rel_bias", "group_gain", "pre_mix_g", "post_mix_g", "pre_ffn_g", "post_ffn_g")


def _pack(parts, rows):
    flat = jnp.concatenate([p.reshape(-1).astype(F32) for p in parts])
    return jnp.pad(flat, (0, rows * D_MODEL - flat.shape[0])).reshape(rows, D_MODEL)


def _unpack(packed, shapes):
    flat = packed.reshape(-1)
    out, at = [], 0
    for shp in shapes:
        size = int(np.prod(shp))
        out.append(flat[at:at + size].reshape(shp))
        at += size
    return out


def kernel(x, w_in, w_out, conv_w, pool_w, pool_scale, rel_bias, group_gain, pre_mix_g, post_mix_g, pre_ffn_g, post_ffn_g, w_gate_up, w_down, loss_target, m_w_in, m_w_out, m_conv_w, m_pool_w, m_pool_scale, m_rel_bias, m_group_gain, m_pre_mix_g, m_post_mix_g, m_pre_ffn_g, m_post_ffn_g, m_w_gate_up, m_w_down, v_w_in, v_w_out, v_conv_w, v_pool_w, v_pool_scale, v_rel_bias, v_group_gain, v_pre_mix_g, v_post_mix_g, v_pre_ffn_g, v_post_ffn_g, v_w_gate_up, v_w_down):
    depth = w_in.shape[0]
    s = x.shape[1]
    c_me = lax.axis_index("c")
    q_me = 2 * lax.axis_index("x") + lax.axis_index("y")
    dev = 2 * q_me + c_me

    tr = lambda a: jnp.swapaxes(a, 1, 2)
    w_in_t, w_gu_t = tr(w_in), tr(w_gate_up)

    def gather_layer(l, deps):
        first = gather_start(f"gather_a{l}_s", [w_in_t[l].astype(BF16), w_out[l].astype(BF16), conv_w[l]], deps)
        return first, gather_start(f"gather_b{l}_s", [w_gu_t[l].astype(BF16), w_down[l].astype(BF16)], [first["token"]])

    def weights_a(relayed, l, after):
        g_in, g_out, g_cw = gather_finish(f"gather_a{l}_f", relayed, after)
        return (assemble_rows("assemble_in", *g_in), assemble_rows("assemble_out", *g_out),
                assemble_cols("assemble_conv", *g_cw))

    def weights_b(relayed, l, after):
        g_gu, g_dn = gather_finish(f"gather_b{l}_f", relayed, after)
        return assemble_rows("assemble_gu", *g_gu), assemble_rows("assemble_down", *g_dn)

    h = x.reshape(s, D_MODEL)
    started = {0: gather_layer(0, [])}
    xn = rmsnorm_fwd("norm_mix", h, pre_mix_g[0].reshape(1, -1), BF16)
    relay_a = gather_relay("gather_a0_r", started[0][0], started[0][1]["token"])
    wa = weights_a(relay_a, 0, xn)
    saved = []
    for l in range(depth):
        vec = lambda p: p[l].reshape(1, -1)
        wt_in, wf_out, cw_full = wa
        ahead = [started[l][1]["token"]]
        if l + 1 < depth:
            started[l + 1] = gather_layer(l + 1, ahead)
            ahead = [started[l + 1][1]["token"]]
        proj = matmul("proj", xn, wt_in, "nt", BF16, s, 5 * D_ATTN, D_MODEL, 512, 5 * D_ATTN, D_MODEL, deps=ahead)
        relay_b = gather_relay(f"gather_b{l}_r", started[l][1], proj)
        bd = _block_diag(pool_w[l]).astype(BF16)
        gg = vec(group_gain)
        y_ab = conv_pool_fwd("conv_pool_fwd", proj, cw_full, bd, vec(pool_scale), gg[:, :2 * D_CONV],
                             deps=[relay_b["token"]])
        y, o, lse, bias = attention_fwd("attention_fwd", proj, bias_by_offset(rel_bias[l]), gg[:, 2 * D_CONV:], y_ab)
        wt_gu, wf_dn = weights_b(relay_b, l, y)
        if l + 1 < depth:
            relay_a = gather_relay(f"gather_a{l + 1}_r", started[l + 1][0], y)
        mix, h_mid, hn = matmul_norm("mix_out", y, wf_out, h, vec(post_mix_g), then=(vec(pre_ffn_g), BF16),
                                     deps=[relay_a["token"]] if l + 1 < depth else [])
        g, u, ff = matmul_swiglu("gate_up", hn, wt_gu)
        if l + 1 < depth:
            wa = weights_a(relay_a, l + 1, ff)
            ffo, h_out, xn_next = matmul_norm("ffn_down", ff, wf_dn, h_mid, vec(post_ffn_g),
                                              then=(pre_mix_g[l + 1].reshape(1, -1), BF16))
        else:
            ffo, h_out = matmul_norm("ffn_down_last", ff, wf_dn, h_mid, vec(post_ffn_g))
            xn_next = None
        saved.append((h, xn, proj, None, bd, bias, y, o, lse, mix, h_mid, hn, g, u, ff, ffo,
                      wt_in, wf_out, cw_full, wt_gu, wf_dn))
        h, xn = h_out, xn_next

    dh, loss_part, dffo, dg_last = loss_head("loss_head", h, loss_target.reshape(s, D_MODEL), saved[-1][15],
                                             post_ffn_g[depth - 1].reshape(1, -1))

    def reduce_begin(name, halves):
        return [k for k, _ in halves], exchange_start(name, plan_swap_cores, len(halves), [snd for _, snd in halves],
                                                      [(k.shape, k.dtype) for k, _ in halves])

    def reduce_relay(name, kept, swapped, after):
        _, got = exchange_wait(name + "_w", swapped, after)
        sums = [add_pairs("reduce_add", k, r) for k, r in zip(kept, got)]
        return exchange_start(name + "_s", plan_to_chips, 3 * len(sums), sums, [(a.shape, a.dtype) for a in sums])

    def reduce_finish(name, relayed, after):
        sums, got = exchange_wait(name, relayed, after)
        return list(zip(sums, got))

    small = {k: [None] * depth for k in _SMALL + ("conv_w",)}
    relayed = {}
    carried = []
    small["post_ffn_g"][depth - 1] = dg_last
    for l in reversed(range(depth)):
        vec = lambda p: p[l].reshape(1, -1)
        (h_in, xn, proj, _, bd, bias, y, o, lse, mix, h_mid, hn, g, u, ff, ffo,
         wt_in, wf_out, cw_full, wt_gu, wf_dn) = saved[l]
        gg = vec(group_gain)
        tk_w = min(s, 2048)
        gw_dn = matmul("wgrad_down", ff, dffo, "tn", BF16, D_FF, D_MODEL, s, 1408, D_MODEL, tk_w,
                       shard_rows=D_FF // 8, deps=carried)
        dg, du = matmul_swiglu_bwd("dgrad_down", dffo, wf_dn, g, u)
        gw_gu = matmul("wgrad_gu", [dg, du], hn, "tn", BF16, 2 * D_FF, D_MODEL, s, 1408, D_MODEL, tk_w,
                       shard_rows=2 * D_FF // 8)
        kept_b, swap_b = reduce_begin(f"reduce_b{l}_s", [gw_gu, gw_dn])
        dh_mid, dmix, small["pre_ffn_g"][l], small["post_mix_g"][l] = matmul_norm_bwd(
            "dgrad_gu", [dg, du], wt_gu, 256, h_mid, vec(pre_ffn_g), dh, then=(mix, vec(post_mix_g), BF16),
            deps=[swap_b["token"]])
        relayed[l, "b"] = reduce_relay(f"reduce_b{l}_r", kept_b, swap_b, dmix)
        gw_out = matmul("wgrad_out", y, dmix, "tn", BF16, D_MODEL, D_MODEL, s, D_MODEL, D_MODEL, tk_w,
                        shard_rows=D_MODEL // 8, deps=[relayed[l, "b"]["token"]])
        dy = matmul("dgrad_out", dmix, wf_out, "nt", BF16, s, D_MODEL, D_MODEL, 512, D_MODEL, D_MODEL)
        dpa, dcw, dbd, dps, dgg_ab = conv_pool_bwd("conv_pool_bwd", proj, dy, cw_full, bd, vec(pool_scale),
                                                   gg[:, :2 * D_CONV])
        dq, dk, dv, by_off, dgg_c = attention_bwd("attention_bwd", proj, o, lse, dy, bias, gg[:, 2 * D_CONV:])
        dparts = [dpa, dq, dk, dv]
        gw_in = matmul("wgrad_in", dparts, xn, "tn", BF16, 5 * D_ATTN, D_MODEL, s, 512, D_MODEL, min(s, 2048))
        kept_a, swap_a = reduce_begin(f"reduce_a{l}_s", [split_rows("split_in", gw_in), gw_out])
        if l > 0:
            dh, dffo, small["pre_mix_g"][l], small["post_ffn_g"][l - 1] = matmul_norm_bwd(
                "dgrad_in", dparts, wt_in, 512, h_in, vec(pre_mix_g), dh_mid,
                then=(saved[l - 1][15], post_ffn_g[l - 1].reshape(1, -1), BF16), chunk=32, deps=[swap_a["token"]])
        else:
            dh, small["pre_mix_g"][l] = matmul_norm_bwd("dgrad_in_first", dparts, wt_in, 512, h_in, vec(pre_mix_g),
                                                        dh_mid, chunk=32, deps=[swap_a["token"]])
        relayed[l, "a"] = reduce_relay(f"reduce_a{l}_r", kept_a, swap_a, dh)
        carried = [relayed[l, "a"]["token"]]
        small["conv_w"][l] = dcw
        small["pool_w"][l] = jnp.stack([dbd[gi * POOL_GC:(gi + 1) * POOL_GC, gi * POOL_GC:(gi + 1) * POOL_GC]
                                        for gi in range(N_POOL)])
        small["pool_scale"][l] = dps
        small["rel_bias"][l] = rel_bias_grad(by_off)
        small["group_gain"][l] = jnp.concatenate([dgg_ab, dgg_c], axis=1)
    grad_x = dh.reshape(x.shape)

    small_params = dict(pool_w=pool_w, pool_scale=pool_scale, rel_bias=rel_bias, group_gain=group_gain,
                        pre_mix_g=pre_mix_g, post_mix_g=post_mix_g, pre_ffn_g=pre_ffn_g, post_ffn_g=post_ffn_g)
    small_m = dict(pool_w=m_pool_w, pool_scale=m_pool_scale, rel_bias=m_rel_bias, group_gain=m_group_gain,
                   pre_mix_g=m_pre_mix_g, post_mix_g=m_post_mix_g, pre_ffn_g=m_pre_ffn_g, post_ffn_g=m_post_ffn_g)
    small_v = dict(pool_w=v_pool_w, pool_scale=v_pool_scale, rel_bias=v_rel_bias, group_gain=v_group_gain,
                   pre_mix_g=v_pre_mix_g, post_mix_g=v_post_mix_g, pre_ffn_g=v_pre_ffn_g, post_ffn_g=v_post_ffn_g)
    shapes = [small_params[k].shape for k in _SMALL] + [(depth, 3, D_CONV), (1,)]
    n_small = sum(int(np.prod(shp)) for shp in shapes)
    rows = -(-n_small // (8 * D_MODEL)) * 8
    extra = [jnp.zeros((depth, 3, D_CONV), F32), jnp.zeros((1,), F32)]
    grads_packed = _pack([jnp.stack(small[k]) for k in _SMALL] + [jnp.stack(small["conv_w"]), loss_part[0, 0:1]], rows)
    small_started = gather_start("gather_small_s", [grads_packed], carried)

    big = dict(w_in=(w_in_t, tr(m_w_in), tr(v_w_in)), w_out=(w_out, m_w_out, v_w_out),
               w_gate_up=(w_gu_t, tr(m_w_gate_up), tr(v_w_gate_up)), w_down=(w_down, m_w_down, v_w_down))
    results = dict(w_in=None, w_out=None, w_gate_up=None, w_down=None)
    after = small_started["token"]
    small_relay = None
    for l in reversed(range(depth)):
        slabs_gu, slabs_dn = reduce_finish(f"reduce_b{l}_f", relayed[l, "b"], after)
        slabs_in, slabs_out = reduce_finish(f"reduce_a{l}_f", relayed[l, "a"], slabs_dn[1])
        for k, slabs in (("w_gate_up", slabs_gu), ("w_down", slabs_dn), ("w_in", slabs_in), ("w_out", slabs_out)):
            results[k] = adamw_layer("adamw_" + k, l, *big[k], *slabs, results[k])
        after = results["w_out"][0]
        if l == min(1, depth - 1):
            small_relay = gather_relay("gather_small_r", small_started, after)
    small_parts = gather_finish("gather_small_f", small_relay, after)[0]
    all_small = assemble_rows("assemble_small", *small_parts).reshape(8, rows, D_MODEL)
    res_small = adamw("adamw_small", _pack([small_params[k] for k in _SMALL] + extra, rows),
                      _pack([small_m[k] for k in _SMALL] + extra, rows),
                      _pack([small_v[k] for k in _SMALL] + extra, rows), all_small)
    g_s, d_s, m_s, v_s = (_unpack(r, shapes) for r in res_small)
    loss = g_s[-1][0]
    g_conv = lax.dynamic_slice_in_dim(g_s[-2], dev * conv_w.shape[2], conv_w.shape[2], axis=2)
    results["conv_w"] = adamw("adamw_conv", conv_w, m_conv_w, v_conv_w, g_conv[None])

    names = ("w_in", "w_out", "conv_w") + _SMALL + ("w_gate_up", "w_down")
    for k in ("w_in", "w_gate_up"):
        results[k] = tuple(tr(r) for r in results[k])
    for j, k in enumerate(_SMALL):
        results[k] = (g_s[j], d_s[j], m_s[j], v_s[j])
    return (loss, grad_x, *[results[k][0] for k in names], *[results[k][1] for k in names],
            *[results[k][2] for k in names], *[results[k][3] for k in names])
```

```python
import numpy as np
import jax
import jax.numpy as jnp
from jax import lax
from jax.experimental import pallas as pl
from jax.experimental.pallas import tpu as pltpu

F32 = jnp.float32
BF16 = jnp.bfloat16

CHUNK = 64
D_MODEL = 1024
D_CONV = 256
D_POOL = 256
D_ATTN = 512
HEAD_DIM = 64
N_HEADS = 8
N_POOL = 4
POOL_GC = 64
POOL_WINDOWS = (2, 4, 8, 16)
LEFT_CHUNKS = 8
REL_CLIP = 128
D_FF = 2816
EPS = 1e-6
ADAM_LR, ADAM_B1, ADAM_B2, ADAM_EPS, ADAM_WD, ADAM_STEP = 0.001, 0.9, 0.999, 1e-08, 0.01, 10

QB = 256
KB = 3 * QB
HALO = 16
T_CP = 512
T_ROW = 512
QKV_COL = 2
NEG = -1e30
VMEM_MB = 1 << 20
MESH = pl.DeviceIdType.MESH


def _call(body, **kw):
    call = pl.pallas_call(body, **kw)
    return lambda *args: call(*[_in_hbm(a) for a in args])


def _in_hbm(a):
    return pltpu.with_memory_space_constraint(a, pltpu.HBM) if jnp.issubdtype(a.dtype, jnp.number) else a


def _call_after(deps, n_in, body, **kw):
    deps = tuple(deps)
    if not deps:
        return _call(body, **kw)

    def ordered(*refs):
        body(*refs[:n_in], *refs[n_in + len(deps):])

    kw["in_specs"] = list(kw["in_specs"]) + [pl.BlockSpec(memory_space=pl.ANY)] * len(deps)
    call = _call(ordered, **kw)
    return lambda *args: call(*args, *deps)


def _params(sem, vmem_mb=48):
    return pltpu.CompilerParams(dimension_semantics=sem, vmem_limit_bytes=vmem_mb * VMEM_MB)


_CHIP_FLIPS = ((1, 0), (0, 1), (1, 1))
_HBM = pl.BlockSpec(memory_space=pltpu.HBM)
_SEM = pl.BlockSpec(memory_space=pltpu.SEMAPHORE)
_EFFECT = pltpu.SideEffectType.DATAFLOW_SIDE_EFFECTING


def _flip(v, f):
    return 1 - v if f else v


def _descriptors(plan, srcs, lands, send_sems, recv_sems, sending):
    x, y, c = lax.axis_index("x"), lax.axis_index("y"), lax.axis_index("c")
    return [pltpu.make_async_remote_copy(src_ref=src, dst_ref=dst if sending else land, send_sem=send_sems.at[k],
                                         recv_sem=recv_sems.at[k], device_id=peer, device_id_type=MESH)
            for k, (src, dst, peer, land) in enumerate(plan(srcs, lands, x, y, c))]


def exchange_start(name, plan, n_copies, srcs, land_shapes, deps=()):
    ns, nl = len(srcs), len(land_shapes)

    def body(*refs):
        src_refs, land_refs = refs[:ns], refs[ns:ns + nl]
        send_sems, recv_sems = refs[ns + nl], refs[ns + nl + 1]
        for send in _descriptors(plan, src_refs, land_refs, send_sems, recv_sems, True):
            send.start()
        refs[-1][...] = jnp.zeros_like(refs[-1])

    lands = [lax.empty(shape, dtype) for shape, dtype in land_shapes]
    outs = _call_after(
        deps, ns + nl, body, name=name,
        out_shape=(pltpu.SemaphoreType.DMA((n_copies,)), pltpu.SemaphoreType.DMA((n_copies,)),
                   *[pltpu.HBM(a.shape, a.dtype) for a in srcs], *[pltpu.HBM(shape, dtype) for shape, dtype in land_shapes],
                   jax.ShapeDtypeStruct((8, 128), F32)),
        in_specs=[_HBM] * (ns + nl),
        out_specs=(_SEM, _SEM, *[_HBM] * (ns + nl), pl.BlockSpec(memory_space=pltpu.VMEM)),
        input_output_aliases={j: 2 + j for j in range(ns + nl)},
        compiler_params=pltpu.CompilerParams(has_side_effects=_EFFECT),
    )(*srcs, *lands)
    return dict(plan=plan, sems=outs[:2], srcs=outs[2:2 + ns], lands=outs[2 + ns:2 + ns + nl], token=outs[-1])


def exchange_wait(name, started, after):
    srcs, lands = started["srcs"], started["lands"]
    ns, nl = len(srcs), len(lands)

    def body(*refs):
        src_refs, land_refs = refs[:ns], refs[ns:ns + nl]
        send_sems, recv_sems = refs[ns + nl], refs[ns + nl + 1]
        for wait in _descriptors(started["plan"], src_refs, land_refs, send_sems, recv_sems, False):
            wait.wait_send()
            wait.wait_recv()

    outs = _call(
        body, name=name,
        out_shape=tuple(pltpu.HBM(a.shape, a.dtype) for a in (*srcs, *lands)),
        in_specs=[_HBM] * (ns + nl) + [_SEM, _SEM, pl.BlockSpec(memory_space=pl.ANY)],
        out_specs=tuple([_HBM] * (ns + nl)),
        input_output_aliases={j: j for j in range(ns + nl)},
        compiler_params=pltpu.CompilerParams(has_side_effects=_EFFECT),
    )(*srcs, *lands, *started["sems"], after)
    return list(outs[:ns]), list(outs[ns:])


def plan_from_chips(srcs, lands, x, y, c):
    q = 2 * x + y
    out = []
    for src, land in zip(srcs, lands):
        for fx, fy in _CHIP_FLIPS:
            px, py = _flip(x, fx), _flip(y, fy)
            out.append((src, land.at[q], (px, py, c), land.at[2 * px + py]))
    return out


def plan_to_core(srcs, lands, x, y, c):
    n = len(lands)
    q = 2 * x + y
    out = []
    for own, chips, land in zip(srcs[:n], srcs[n:], lands):
        out.append((own, land.at[q], (x, y, 1 - c), land.at[q]))
        for fx, fy in _CHIP_FLIPS:
            qp = 2 * _flip(x, fx) + _flip(y, fy)
            out.append((chips.at[qp], land.at[qp], (x, y, 1 - c), land.at[qp]))
    return out


def plan_swap_cores(srcs, lands, x, y, c):
    return [(src, land, (x, y, 1 - c), land) for src, land in zip(srcs, lands)]


def plan_to_chips(srcs, lands, x, y, c):
    q = 2 * x + y
    out = []
    for src, land in zip(srcs, lands):
        for fx, fy in _CHIP_FLIPS:
            px, py = _flip(x, fx), _flip(y, fy)
            out.append((src.at[2 * px + py], land.at[q], (px, py, c), land.at[2 * px + py]))
    return out


def _slots(a):
    return ((4,) + a.shape, a.dtype)


def gather_start(name, arrays, deps=()):
    return exchange_start(name, plan_from_chips, 3 * len(arrays), arrays, [_slots(a) for a in arrays], deps)


def gather_relay(name, started, after):
    own, chips = exchange_wait(name + "_w", started, after)
    return exchange_start(name + "_s", plan_to_core, 4 * len(own), own + chips, [_slots(a) for a in own])


def gather_finish(name, relayed, after):
    srcs, cores = exchange_wait(name, relayed, after)
    n = len(cores)
    return list(zip(srcs[:n], srcs[n:], cores))


def _place():
    return 2 * lax.axis_index("x") + lax.axis_index("y"), lax.axis_index("c")


def assemble_cols(name, own, chips, core):
    r, c = own.shape
    t = _row_tile(r, 256)

    def body(own_ref, chips_ref, core_ref, o_ref):
        q_me, c_me = _place()
        for q in range(4):
            mine = jnp.where(q == q_me, own_ref[...], chips_ref[q])
            other = core_ref[q]
            o_ref[:, 2 * q * c:(2 * q + 1) * c] = jnp.where(c_me == 0, mine, other)
            o_ref[:, (2 * q + 1) * c:(2 * q + 2) * c] = jnp.where(c_me == 0, other, mine)

    slots = pl.BlockSpec((4, t, c), lambda i: (0, i, 0))
    return _call(
        body, name=name, grid=(r // t,),
        in_specs=[pl.BlockSpec((t, c), lambda i: (i, 0)), slots, slots],
        out_specs=pl.BlockSpec((t, 8 * c), lambda i: (i, 0)),
        out_shape=jax.ShapeDtypeStruct((r, 8 * c), own.dtype), compiler_params=_params(("parallel",)),
    )(own, chips, core)


def assemble_rows(name, own, chips, core):
    r, c = own.shape

    def body(own_ref, chips_ref, core_ref, o_ref):
        q_me, c_me = _place()
        d = pl.program_id(0)
        mine = jnp.where(d // 2 == q_me, own_ref[...], chips_ref[...])
        o_ref[...] = jnp.where(d % 2 == c_me, mine, core_ref[...])

    slot = pl.BlockSpec((None, r, c), lambda d: (d // 2, 0, 0))
    return _call(
        body, name=name, grid=(8,),
        in_specs=[pl.BlockSpec((r, c), lambda d: (0, 0)), slot, slot],
        out_specs=pl.BlockSpec((r, c), lambda d: (d, 0)),
        out_shape=jax.ShapeDtypeStruct((8 * r, c), own.dtype), compiler_params=_params(("parallel",)),
    )(own, chips, core)


def split_rows(name, dw):
    r8, c = dw.shape
    r = r8 // 8

    def body(dw_ref, keep_ref, send_ref):
        _, c_me = _place()
        d = pl.program_id(0)

        @pl.when(d % 2 == c_me)
        def _():
            keep_ref[...] = dw_ref[...]

        @pl.when(d % 2 != c_me)
        def _():
            send_ref[...] = dw_ref[...]

    slot = pl.BlockSpec((None, r, c), lambda d: (d // 2, 0, 0))
    return _call(
        body, name=name, grid=(8,),
        in_specs=[pl.BlockSpec((r, c), lambda d: (d, 0))], out_specs=[slot, slot],
        out_shape=[jax.ShapeDtypeStruct((4, r, c), dw.dtype)] * 2, compiler_params=_params(("arbitrary",)),
    )(dw)


def _rms(v):
    return lax.rsqrt(jnp.mean(v * v, axis=-1, keepdims=True) + EPS)


def rmsnorm_fwd(name, x, g, out_dtype):
    s, d = x.shape

    def body(x_ref, g_ref, o_ref):
        xv = x_ref[...]
        o_ref[...] = ((xv * _rms(xv)) * g_ref[...]).astype(o_ref.dtype)

    row = pl.BlockSpec((T_ROW, d), lambda i: (i, 0))
    return _call(
        body, name=name, grid=(s // T_ROW,),
        in_specs=[row, pl.BlockSpec((1, d), lambda i: (0, 0))], out_specs=row,
        out_shape=jax.ShapeDtypeStruct((s, d), out_dtype), compiler_params=_params(("parallel",)),
    )(x, g)


def matmul_then(name, a_parts, b, mode, tm, chunk, rows_in, vecs_in, rows_out, n_sums, then, deps=()):
    a_parts = list(a_parts)
    na, nr, nv, no = len(a_parts), len(rows_in), len(vecs_in), len(rows_out)
    s = a_parts[0].shape[0]
    n = s // tm
    d = b.shape[1] if mode == "nn" else b.shape[0]
    offs = [0]
    for p in a_parts:
        offs.append(offs[-1] + p.shape[1])
    n_in = na + 1 + nr + nv

    def body(*refs):
        a_refs, b_ref = refs[:na], refs[na]
        row_refs, vec_refs = refs[na + 1:na + 1 + nr], refs[na + 1 + nr:n_in]
        out_refs, sum_refs = refs[n_in:n_in + no], refs[n_in + no:n_in + no + n_sums]
        acc = refs[-2:]
        i = pl.program_id(0)

        @pl.when(i == 0)
        def _():
            acc[1][...] = jnp.zeros_like(acc[1])
            for s_ref in sum_refs:
                s_ref[...] = jnp.zeros_like(s_ref)

        def step(write, read):
            t = None
            for p in range(na):
                lo, hi = offs[p], offs[p + 1]
                part = _dot(a_refs[p][...], b_ref[lo:hi, :] if mode == "nn" else b_ref[:, lo:hi], mode)
                t = part if t is None else t + part
            write[...] = t
            totals = [0.0] * n_sums
            for r0 in range(0, tm, chunk):
                rs = pl.ds(r0, chunk)
                results, sums = then(read[rs, :], [r.at[rs, :] for r in row_refs], vec_refs)
                for o_ref, val in zip(out_refs, results):
                    o_ref[rs, :] = val.astype(o_ref.dtype)
                totals = [tot + val for tot, val in zip(totals, sums)]
            for s_ref, val in zip(sum_refs, totals):
                s_ref[...] += jnp.where(i > 0, val, 0.0)

        pl.when(i % 2 == 0)(lambda: step(acc[0], acc[1]))
        pl.when(i % 2 == 1)(lambda: step(acc[1], acc[0]))

    ahead = lambda i: (jnp.minimum(i, n - 1), 0)
    behind = lambda i: (jnp.maximum(i - 1, 0), 0)
    fixed = lambda i: (0, 0)
    shapes = [jax.ShapeDtypeStruct((s, d), dt) for dt in rows_out] + [jax.ShapeDtypeStruct((1, d), F32)] * n_sums
    return _call_after(
        deps, n_in, body, name=name, grid=(n + 1,),
        in_specs=[pl.BlockSpec((tm, p.shape[1]), ahead) for p in a_parts] + [pl.BlockSpec(b.shape, fixed)]
        + [pl.BlockSpec((tm, r.shape[1]), behind) for r in rows_in] + [pl.BlockSpec(v.shape, fixed) for v in vecs_in],
        out_specs=[pl.BlockSpec((tm, d), behind)] * no + [pl.BlockSpec((1, d), fixed)] * n_sums,
        out_shape=shapes, scratch_shapes=[pltpu.VMEM((tm, d), F32)] * 2,
        compiler_params=_params(("arbitrary",), 56),
    )(*a_parts, b, *rows_in, *vecs_in)


def matmul_norm(name, a, b, res, g, then=None, deps=()):
    def norm(z, rows, vecs):
        y = rows[0][...] + (z * _rms(z)) * vecs[0][...]
        return [z, y] + ([(y * _rms(y)) * vecs[1][...]] if then else []), []

    return matmul_then(name, [a], b, "nn", 512, 512, [res], [g] + ([then[0]] if then else []),
                       [BF16, F32] + ([then[1]] if then else []), 0, norm, deps)


def matmul_norm_bwd(name, a_parts, b, tm, x, g, res, then=None, chunk=None, deps=()):
    def norms(t, rows, vecs):
        xv = rows[0][...].astype(F32)
        dx, dgt = _norm_bwd(xv, _rms(xv), vecs[0][...], t)
        dx = dx + rows[1][...]
        results, sums = [dx], [jnp.sum(dgt, axis=0, keepdims=True)]
        if then:
            x2 = rows[2][...].astype(F32)
            dx2, dgt2 = _norm_bwd(x2, _rms(x2), vecs[1][...], dx)
            results.append(dx2)
            sums.append(jnp.sum(dgt2, axis=0, keepdims=True))
        return results, sums

    return matmul_then(name, a_parts, b, "nn", tm, chunk or tm, [x, res] + ([then[0]] if then else []),
                       [g] + ([then[1]] if then else []), [F32] + ([then[2]] if then else []),
                       2 if then else 1, norms, deps)


def _norm_bwd(xv, r, g, dy):
    a = dy * g
    dx = r * (a - xv * ((r * r) * jnp.mean(a * xv, axis=-1, keepdims=True)))
    return dx, dy * (xv * r)


def loss_head(name, h, tgt, z, g):
    s, d = h.shape

    def body(h_ref, t_ref, z_ref, g_ref, dh_ref, l_ref, dz_ref, dg_ref):
        e = h_ref[...] - t_ref[...]
        dh = e * (1.0 / d)
        dh_ref[...] = dh
        zv = z_ref[...].astype(F32)
        dz, dgt = _norm_bwd(zv, _rms(zv), g_ref[...], dh)
        dz_ref[...] = dz.astype(dz_ref.dtype)

        @pl.when(pl.program_id(0) == 0)
        def _():
            l_ref[...] = jnp.zeros_like(l_ref)
            dg_ref[...] = jnp.zeros_like(dg_ref)
        part = 0.5 * jnp.sum(jnp.mean(e * e, axis=-1, keepdims=True), axis=0, keepdims=True)
        l_ref[...] += jnp.broadcast_to(part, l_ref.shape)
        dg_ref[...] += jnp.sum(dgt, axis=0, keepdims=True)

    row = pl.BlockSpec((T_ROW, d), lambda i: (i, 0))
    vec = pl.BlockSpec((1, d), lambda i: (0, 0))
    return _call(
        body, name=name, grid=(s // T_ROW,),
        in_specs=[row, row, row, vec], out_specs=[row, pl.BlockSpec((1, 128), lambda i: (0, 0)), row, vec],
        out_shape=[jax.ShapeDtypeStruct((s, d), F32), jax.ShapeDtypeStruct((1, 128), F32),
                   jax.ShapeDtypeStruct((s, d), BF16), jax.ShapeDtypeStruct((1, d), F32)],
        compiler_params=_params(("arbitrary",)),
    )(h, tgt, z, g)


def _row_tile(rows, limit=512):
    t = min(rows, limit)
    while rows % t or (t % 8 and t != rows):
        t -= 1
    return t


def add_pairs(name, a, b):
    shape = a.shape
    a2, b2 = a.reshape(-1, shape[-1]), b.reshape(-1, shape[-1])
    rows, cols = a2.shape
    t = _row_tile(rows)

    def body(a_ref, b_ref, o_ref):
        o_ref[...] = (a_ref[...].astype(F32) + b_ref[...].astype(F32)).astype(o_ref.dtype)

    blk = pl.BlockSpec((t, cols), lambda i: (i, 0))
    out = _call(
        body, name=name, grid=(rows // t,), in_specs=[blk, blk], out_specs=blk,
        out_shape=jax.ShapeDtypeStruct((rows, cols), a.dtype), compiler_params=_params(("parallel",)),
    )(a2, b2)
    return out.reshape(shape)


def _adamw_math(w, m, v, g):
    c1 = 1.0 - ADAM_B1 ** ADAM_STEP
    c2 = 1.0 - ADAM_B2 ** ADAM_STEP
    mn = ADAM_B1 * m + (1.0 - ADAM_B1) * g
    vn = ADAM_B2 * v + (1.0 - ADAM_B2) * (g * g)
    delta = -ADAM_LR * ((mn / c1) / (jnp.sqrt(vn / c2) + ADAM_EPS) + ADAM_WD * w)
    return delta, mn, vn


def _slab_sum(g_ref, n):
    g = g_ref[0].astype(F32)
    for j in range(1, n):
        g = g + g_ref[j].astype(F32)
    return g


def adamw(name, w, m, v, g_slabs, deps=()):
    shape = w.shape
    n = g_slabs.shape[0]
    w2, m2, v2 = (t.reshape(-1, shape[-1]) for t in (w, m, v))
    g3 = g_slabs.reshape(n, -1, shape[-1])
    rows, cols = w2.shape
    t = _row_tile(rows, 256)

    def body(w_ref, m_ref, v_ref, g_ref, go_ref, d_ref, mo_ref, vo_ref):
        g = _slab_sum(g_ref, n)
        go_ref[...] = g
        d_ref[...], mo_ref[...], vo_ref[...] = _adamw_math(w_ref[...], m_ref[...], v_ref[...], g)

    blk = pl.BlockSpec((t, cols), lambda i: (i, 0))
    outs = _call_after(
        deps, 4, body, name=name, grid=(rows // t,),
        in_specs=[blk, blk, blk, pl.BlockSpec((n, t, cols), lambda i: (0, i, 0))], out_specs=[blk] * 4,
        out_shape=[jax.ShapeDtypeStruct((rows, cols), F32)] * 4, compiler_params=_params(("parallel",)),
    )(w2, m2, v2, g3)
    return tuple(o.reshape(shape) for o in outs)


def adamw_layer(name, l, w, m, v, own_sums, chip_sums, into):
    _, rows, cols = w.shape
    t = _row_tile(rows, 256)
    if into is None:
        into = tuple(lax.empty(w.shape, F32) for _ in range(4))

    def body(w_ref, m_ref, v_ref, own_ref, far_ref, a0, a1, a2, a3, go_ref, d_ref, mo_ref, vo_ref):
        del a0, a1, a2, a3
        q_me, _ = _place()
        g = jnp.zeros((t, cols), F32)
        for q in range(4):
            g = g + jnp.where(q == q_me, own_ref[q], far_ref[q]).astype(F32)
        go_ref[...] = g
        d_ref[...], mo_ref[...], vo_ref[...] = _adamw_math(w_ref[...], m_ref[...], v_ref[...], g)

    blk = pl.BlockSpec((None, t, cols), lambda i: (l, i, 0))
    slabs = pl.BlockSpec((4, t, cols), lambda i: (0, i, 0))
    anyw = pl.BlockSpec(memory_space=pl.ANY)
    return _call(
        body, name=name, grid=(rows // t,),
        in_specs=[blk, blk, blk, slabs, slabs] + [anyw] * 4, out_specs=[blk] * 4,
        out_shape=[jax.ShapeDtypeStruct(w.shape, F32)] * 4, input_output_aliases={5: 0, 6: 1, 7: 2, 8: 3},
        compiler_params=_params(("parallel",)),
    )(w, m, v, own_sums, chip_sums, *into)


_DIMS = {"nn": (((1,), (0,)), ((), ())), "nt": (((1,), (1,)), ((), ())), "tn": (((0,), (0,)), ((), ()))}


def _dot(a, b, mode="nn"):
    return lax.dot_general(a, b, _DIMS[mode], preferred_element_type=F32)


def matmul(name, a, b, mode, out_dtype, m, n, k, tm, tn, tk, b_off=(0, 0), shard_rows=0, deps=()):
    gm, gn, gk = m // tm, n // tn, k // tk
    assert gm * tm == m and gn * tn == n and gk * tk == k
    r0, c0 = b_off
    a_parts = list(a) if isinstance(a, (list, tuple)) else [a]
    na = len(a_parts)
    tile = tm if mode == "tn" else tk
    spans, at = [], 0
    for p in a_parts:
        nblk = p.shape[1] // tile
        assert nblk * tile == p.shape[1]
        spans.append((at, nblk))
        at += nblk
    assert at == (gm if mode == "tn" else gk)

    def within(t, span):
        return (t >= span[0]) & (t < span[0] + span[1])

    def local(t, span):
        return jnp.clip(t - span[0], 0, span[1] - 1)

    a_specs = []
    for sp in spans:
        if mode == "tn":
            a_specs.append(pl.BlockSpec((tk, tm), lambda i, j, kk, sp=sp: (jnp.where(within(i, sp), kk, 0), local(i, sp))))
        else:
            a_specs.append(pl.BlockSpec((tm, tk), lambda i, j, kk, sp=sp: (i, local(kk, sp))))
    if mode == "nt":
        b_spec = pl.BlockSpec((tn, tk), lambda i, j, kk: (j + r0, kk + c0))
    else:
        b_spec = pl.BlockSpec((tk, tn), lambda i, j, kk: (kk + r0, j + c0))
    o_spec = pl.BlockSpec((tm, tn), lambda i, j, kk: (i, j))
    single = na == 1 and gk == 1 and not shard_rows
    pairs = tm // (2 * shard_rows) if shard_rows else 0
    assert not shard_rows or (pairs * 2 * shard_rows == tm and m == 8 * shard_rows)

    def body(*refs):
        a_refs, b_ref, o_ref = refs[:na], refs[na], refs[na + 1]
        if single:
            o_ref[...] = _dot(a_refs[0][...], b_ref[...], mode).astype(o_ref.dtype)
            return
        acc_ref = refs[-1]
        i, kk = pl.program_id(0), pl.program_id(2)

        @pl.when(kk == 0)
        def _():
            acc_ref[...] = jnp.zeros_like(acc_ref)

        for pa, sp in enumerate(spans):
            def add(pa=pa):
                acc_ref[...] += _dot(a_refs[pa][...], b_ref[...], mode)
            if na > 1:
                pl.when(within(i if mode == "tn" else kk, sp))(add)
            else:
                add()

        @pl.when(kk == gk - 1)
        def _():
            if not shard_rows:
                o_ref[...] = acc_ref[...].astype(o_ref.dtype)
                return
            send_ref, mine = refs[na + 2], lax.axis_index("c") == 0
            for pq in range(pairs):
                even = acc_ref[2 * shard_rows * pq:2 * shard_rows * pq + shard_rows, :]
                odd = acc_ref[2 * shard_rows * pq + shard_rows:2 * shard_rows * (pq + 1), :]
                o_ref[pq] = jnp.where(mine, even, odd).astype(o_ref.dtype)
                send_ref[pq] = jnp.where(mine, odd, even).astype(send_ref.dtype)

    if shard_rows:
        o_spec = pl.BlockSpec((pairs, shard_rows, tn), lambda i, j, kk: (i, 0, j))
    shard = jax.ShapeDtypeStruct((4, shard_rows, n), out_dtype)
    return _call_after(
        deps, na + 1, body, name=name, grid=(gm, gn, gk),
        in_specs=a_specs + [b_spec], out_specs=[o_spec, o_spec] if shard_rows else o_spec,
        out_shape=[shard, shard] if shard_rows else jax.ShapeDtypeStruct((m, n), out_dtype),
        scratch_shapes=[] if single else [pltpu.VMEM((tm, tn), F32)],
        compiler_params=_params(("parallel", "parallel", "arbitrary")),
    )(*a_parts, b)


def matmul_swiglu(name, a, w_gu_t, deps=()):
    s, d = a.shape
    tm, tn = 512, 1408
    gn = D_FF // tn

    def body(a_ref, wg_ref, wu_ref, du_ref, dg_ref, f_ref):
        av = a_ref[...]
        g = _dot(av, wg_ref[...], "nt")
        u = _dot(av, wu_ref[...], "nt")
        sig = 1.0 / (1.0 + jnp.exp(-g))
        silu = g * sig
        du_ref[...] = silu.astype(du_ref.dtype)
        dg_ref[...] = (u * (sig + silu * (1.0 - sig))).astype(dg_ref.dtype)
        f_ref[...] = (silu * u).astype(f_ref.dtype)

    o_spec = pl.BlockSpec((tm, tn), lambda j, i: (i, j))
    return _call_after(
        deps, 3, body, name=name, grid=(gn, s // tm),
        in_specs=[pl.BlockSpec((tm, d), lambda j, i: (i, 0)), pl.BlockSpec((tn, d), lambda j, i: (j, 0)),
                  pl.BlockSpec((tn, d), lambda j, i: (j + gn, 0))],
        out_specs=[o_spec] * 3, out_shape=[jax.ShapeDtypeStruct((s, D_FF), BF16)] * 3,
        compiler_params=_params(("parallel", "parallel")),
    )(a, w_gu_t, w_gu_t)


def matmul_swiglu_bwd(name, dffo, w_down, ff_du, ff_dg):
    s, d = dffo.shape
    tm, tn = 512, 1408

    def body(a_ref, w_ref, pu_ref, pg_ref, dg_ref, du_ref):
        dff = _dot(a_ref[...], w_ref[...], "nt")
        du_ref[...] = (dff * pu_ref[...].astype(F32)).astype(du_ref.dtype)
        dg_ref[...] = (dff * pg_ref[...].astype(F32)).astype(dg_ref.dtype)

    o_spec = pl.BlockSpec((tm, tn), lambda j, i: (i, j))
    return _call(
        body, name=name, grid=(D_FF // tn, s // tm),
        in_specs=[pl.BlockSpec((tm, d), lambda j, i: (i, 0)), pl.BlockSpec((tn, d), lambda j, i: (j, 0)), o_spec, o_spec],
        out_specs=[o_spec] * 2, out_shape=[jax.ShapeDtypeStruct((s, D_FF), BF16)] * 2,
        compiler_params=_params(("parallel", "parallel")),
    )(dffo, w_down, ff_du, ff_dg)


def _down(v, n):
    return pltpu.roll(v, n, 0)


def _up(v, n):
    return pltpu.roll(v, v.shape[0] - n, 0)


def _by_window(lane, v2, v4, v8, v16):
    return jnp.where(lane < POOL_GC, v2, jnp.where(lane < 2 * POOL_GC, v4, jnp.where(lane < 3 * POOL_GC, v8, v16)))


def _taps(cw_ref):
    return cw_ref[0:1, :], cw_ref[1:2, :], cw_ref[2:3, :]


def _conv_pool_forward(ext, t0, cw, bd):
    n_out = ext.shape[0] - HALO
    gb = ext[HALO:, 0:D_CONV]
    z = ext[:, D_CONV:2 * D_CONV] * ext[:, 2 * D_CONV:3 * D_CONV]
    z0, z1, z2 = z[HALO:], _down(z, 1)[HALO:], _down(z, 2)[HALO:]
    conv = cw[2] * z0 + cw[1] * z1 + cw[0] * z2
    x = ext[:, 3 * D_CONV:]
    w2 = x + _down(x, 1)
    w4 = w2 + _down(w2, 2)
    w8 = w4 + _down(w4, 4)
    w16 = w8 + _down(w8, 8)
    lane = lax.broadcasted_iota(jnp.int32, (1, D_POOL), 1)
    win = _by_window(lane, 2.0, 4.0, 8.0, 16.0)
    pos = (t0 + lax.broadcasted_iota(jnp.int32, (n_out, 1), 0) + 1).astype(F32)
    cnt = jnp.minimum(pos, win)
    d = _by_window(lane, w2, w4, w8, w16)[HALO:] / cnt - x[HALO:]
    ybp = _dot(d.astype(BF16), bd)
    return gb, z0, z1, z2, conv, d, cnt, ybp, lane


def conv_pool_fwd(name, proj_a, cw, bd, ps, gg, deps=()):
    s = proj_a.shape[0]
    t = T_CP
    hb = t // HALO

    def body(main_ref, prev_ref, cw_ref, bd_ref, ps_ref, gg_ref, y_ref):
        i = pl.program_id(0)
        prev = jnp.where(i > 0, prev_ref[...].astype(F32), 0.0)
        ext = jnp.concatenate([prev, main_ref[...].astype(F32)], axis=0)
        gb, _, _, _, conv, _, _, ybp, _ = _conv_pool_forward(ext, i * t, _taps(cw_ref), bd_ref[...])
        ya = gb * conv
        yb = ybp * ps_ref[...]
        ggv = gg_ref[...]
        y_ref[:, 0:D_CONV] = ((ya * _rms(ya)) * ggv[:, 0:D_CONV]).astype(y_ref.dtype)
        y_ref[:, D_CONV:] = ((yb * _rms(yb)) * ggv[:, D_CONV:]).astype(y_ref.dtype)

    full = lambda shape: pl.BlockSpec(shape, lambda i: (0,) * len(shape))
    return _call_after(
        deps, 6, body, name=name, grid=(s // t,),
        in_specs=[pl.BlockSpec((t, D_MODEL), lambda i: (i, 0)),
                  pl.BlockSpec((HALO, D_MODEL), lambda i: (jnp.maximum(i * hb - 1, 0), 0)),
                  full((3, D_CONV)), full((D_POOL, D_POOL)), full((1, D_POOL)), full((1, 2 * D_CONV))],
        out_specs=pl.BlockSpec((t, 2 * D_CONV), lambda i: (i, 0)),
        out_shape=jax.ShapeDtypeStruct((s, D_MODEL), BF16),
        compiler_params=_params(("parallel",)),
    )(proj_a, proj_a, cw, bd, ps, gg)


def conv_pool_bwd(name, proj_a, dy, cw, bd, ps, gg):
    s = proj_a.shape[0]
    t = T_CP
    hb = t // HALO
    nblk = s // t
    last_halo = s // HALO - 1

    def body(main_ref, prev_ref, next_ref, dy_ref, dyn_ref, cw_ref, bd_ref, ps_ref, gg_ref,
             dp_ref, dcw_ref, dbd_ref, dps_ref, dgg_ref):
        i = pl.program_id(0)
        prev = jnp.where(i > 0, prev_ref[...].astype(F32), 0.0)
        main = main_ref[...].astype(F32)
        ext = jnp.concatenate([prev, main, next_ref[...].astype(F32)], axis=0)
        cwv, bdv, psv, ggv = _taps(cw_ref), bd_ref[...], ps_ref[...], gg_ref[...]
        gb, z0, z1, z2, conv, d, cnt, ybp, lane = _conv_pool_forward(ext, i * t, cwv, bdv)
        dyn = jnp.where(i < nblk - 1, dyn_ref[...].astype(F32), 0.0)
        dyv = jnp.concatenate([dy_ref[...].astype(F32), dyn], axis=0)
        ya = gb * conv
        yb = ybp * psv
        dya, dgg_a = _norm_bwd(ya, _rms(ya), ggv[:, 0:D_CONV], dyv[:, 0:D_CONV])
        dyb, dgg_b = _norm_bwd(yb, _rms(yb), ggv[:, D_CONV:], dyv[:, D_CONV:])

        dconv = dya * gb
        dz = (cwv[2] * dconv + cwv[1] * _up(dconv, 1) + cwv[0] * _up(dconv, 2))[:t]
        dp_ref[:, 0:D_CONV] = (dya * conv)[:t].astype(dp_ref.dtype)
        dp_ref[:, D_CONV:2 * D_CONV] = (dz * main[:, 2 * D_CONV:3 * D_CONV]).astype(dp_ref.dtype)
        dp_ref[:, 2 * D_CONV:3 * D_CONV] = (dz * main[:, D_CONV:2 * D_CONV]).astype(dp_ref.dtype)

        dybs = dyb * psv
        dd = _dot(dybs.astype(BF16), bdv, "nt")
        e = dd / cnt
        a2 = e + _up(e, 1)
        a4 = a2 + _up(a2, 2)
        a8 = a4 + _up(a4, 4)
        a16 = a8 + _up(a8, 8)
        dp_ref[:, 3 * D_CONV:] = (_by_window(lane, a2, a4, a8, a16) - dd)[:t].astype(dp_ref.dtype)

        @pl.when(i == 0)
        def _():
            dcw_ref[...] = jnp.zeros_like(dcw_ref)
            dbd_ref[...] = jnp.zeros_like(dbd_ref)
            dps_ref[...] = jnp.zeros_like(dps_ref)
            dgg_ref[...] = jnp.zeros_like(dgg_ref)

        rsum = lambda v: jnp.sum(v[:t], axis=0, keepdims=True)
        dcw_ref[0:1, :] += rsum(dconv * z2)
        dcw_ref[1:2, :] += rsum(dconv * z1)
        dcw_ref[2:3, :] += rsum(dconv * z0)
        dbd_ref[...] += _dot(d[:t].astype(BF16), dybs[:t].astype(BF16), "tn")
        dps_ref[...] += rsum(dyb * ybp)
        dgg_ref[:, 0:D_CONV] += rsum(dgg_a)
        dgg_ref[:, D_CONV:] += rsum(dgg_b)

    full = lambda shape: pl.BlockSpec(shape, lambda i: (0,) * len(shape))
    next_halo = lambda i: (jnp.minimum((i + 1) * hb, last_halo), 0)
    return _call(
        body, name=name, grid=(nblk,),
        in_specs=[pl.BlockSpec((t, D_MODEL), lambda i: (i, 0)),
                  pl.BlockSpec((HALO, D_MODEL), lambda i: (jnp.maximum(i * hb - 1, 0), 0)),
                  pl.BlockSpec((HALO, D_MODEL), next_halo),
                  pl.BlockSpec((t, 2 * D_CONV), lambda i: (i, 0)),
                  pl.BlockSpec((HALO, 2 * D_CONV), next_halo),
                  full((3, D_CONV)), full((D_POOL, D_POOL)), full((1, D_POOL)), full((1, 2 * D_CONV))],
        out_specs=[pl.BlockSpec((t, D_MODEL), lambda i: (i, 0)),
                   full((3, D_CONV)), full((D_POOL, D_POOL)), full((1, D_POOL)), full((1, 2 * D_CONV))],
        out_shape=[jax.ShapeDtypeStruct((s, D_MODEL), BF16), jax.ShapeDtypeStruct((3, D_CONV), F32),
                   jax.ShapeDtypeStruct((D_POOL, D_POOL), F32), jax.ShapeDtypeStruct((1, D_POOL), F32),
                   jax.ShapeDtypeStruct((1, 2 * D_CONV), F32)],
        compiler_params=_params(("arbitrary",)),
    )(proj_a, proj_a, proj_a, dy, dy, cw, bd, ps, gg)


def bias_by_offset(rel_bias):
    n_far = 2 * QB - REL_CLIP + 1
    far = jnp.broadcast_to(rel_bias[:, 2 * REL_CLIP:], (N_HEADS, n_far))
    mid = rel_bias[:, 1:2 * REL_CLIP][:, ::-1]
    near = jnp.broadcast_to(rel_bias[:, 0:1], (N_HEADS, KB - n_far - (2 * REL_CLIP - 1)))
    wrap = jnp.broadcast_to(rel_bias[:, 2 * REL_CLIP:], (N_HEADS, 4 * QB - KB))
    return jnp.concatenate([far, mid, near, wrap], axis=1)


def _fill_bias(off_ref, b_ref):
    width = 4 * QB
    sub = lax.broadcasted_iota(jnp.int32, (8, 1), 0)
    col = lax.broadcasted_iota(jnp.int32, (1, KB), 1)
    for h in range(N_HEADS):
        base = jnp.broadcast_to(off_ref[h:h + 1, :], (8, width))
        for bit in range(3):
            base = jnp.where(((sub >> bit) & 1) == 1, pltpu.roll(base, 1 << bit, 1), base)
        for a in range(QB // 8):
            first = CHUNK * (8 * a // CHUNK)
            rows = (pltpu.roll(base, 8 * a, 1) if a else base)[:, :KB]
            band = (col >= first) & (col < first + (LEFT_CHUNKS + 1) * CHUNK)
            b_ref[h, 8 * a:8 * a + 8, :] = jnp.where(band, rows, NEG)


def rel_bias_grad(by_offset):
    n_far = 2 * QB - REL_CLIP + 1
    hi = jnp.sum(by_offset[:, :n_far], axis=1, keepdims=True) + jnp.sum(by_offset[:, KB:], axis=1, keepdims=True)
    mid = by_offset[:, n_far:n_far + 2 * REL_CLIP - 1][:, ::-1]
    lo = jnp.sum(by_offset[:, n_far + 2 * REL_CLIP - 1:KB], axis=1, keepdims=True)
    return jnp.concatenate([lo, mid, hi], axis=1)


def _head_masks():
    first = lax.broadcasted_iota(jnp.int32, (1, 2 * HEAD_DIM), 1) < HEAD_DIM
    return first, jnp.logical_not(first)


def _pick_lane(tile, h):
    lane = lax.broadcasted_iota(jnp.int32, (1, tile.shape[1]), 1)
    return jnp.sum(jnp.where(lane == h, tile, 0.0), axis=-1, keepdims=True)


def attention_fwd(name, qkv, by_offset, gg, y_ab):
    s = qkv.shape[0]
    nq = s // QB
    scale = HEAD_DIM ** -0.5

    def body(q_ref, k0, k1, k2, v0, v1, v2, off_ref, gg_ref, y_in, y_ref, o_ref, lse_ref, b_ref):
        del y_in
        i = pl.program_id(0)

        @pl.when(i == 0)
        def _():
            _fill_bias(off_ref, b_ref)
        def block(at_start):
            kb = jnp.concatenate([k0[...], k1[...], k2[...]], axis=0)
            vb = jnp.concatenate([v0[...], v1[...], v2[...]], axis=0)
            valid = lax.broadcasted_iota(jnp.int32, (1, KB), 1) >= (2 - i) * QB
            lane = lax.broadcasted_iota(jnp.int32, (1, 128), 1)
            masks = _head_masks()
            lse = jnp.zeros((QB, 128), F32)
            outs = []
            for hp in range(N_HEADS // 2):
                sl = slice(2 * HEAD_DIM * hp, 2 * HEAD_DIM * (hp + 1))
                q_p, k_p, v_p = q_ref[:, sl] * scale, kb[:, sl], vb[:, sl]
                o_pair = jnp.zeros((QB, 2 * HEAD_DIM), F32)
                for a in range(2):
                    h = 2 * hp + a
                    sc = _dot(jnp.where(masks[a], q_p, 0), k_p, "nt") + b_ref[h]
                    if at_start:
                        sc = jnp.where(valid, sc, NEG)
                    mx = jnp.max(sc, axis=-1, keepdims=True)
                    e = jnp.exp(sc - mx)
                    l = jnp.sum(e, axis=-1, keepdims=True)
                    o_pair = o_pair + _dot(e.astype(BF16), jnp.where(masks[a], v_p, 0)) * (1.0 / l)
                    lse = jnp.where(lane == h, mx + jnp.log(l), lse)
                outs.append(o_pair)
            o = jnp.concatenate(outs, axis=1)
            o_ref[...] = o.astype(o_ref.dtype)
            lse_ref[...] = lse
            y_ref[...] = ((o * _rms(o)) * gg_ref[...]).astype(y_ref.dtype)

        pl.when(i < 2)(lambda: block(True))
        pl.when(i >= 2)(lambda: block(False))

    blk = lambda col, back: pl.BlockSpec((QB, D_ATTN), lambda i: (jnp.maximum(i - back, 0), QKV_COL + col))
    return _call(
        body, name=name, grid=(nq,),
        in_specs=[blk(0, 0), blk(1, 2), blk(1, 1), blk(1, 0), blk(2, 2), blk(2, 1), blk(2, 0),
                  pl.BlockSpec((N_HEADS, 4 * QB), lambda i: (0, 0)), pl.BlockSpec((1, D_ATTN), lambda i: (0, 0)),
                  pl.BlockSpec(memory_space=pl.ANY)],
        out_specs=[pl.BlockSpec((QB, D_ATTN), lambda i: (i, 1)), pl.BlockSpec((QB, D_ATTN), lambda i: (i, 0)),
                   pl.BlockSpec((QB, 128), lambda i: (i, 0)), pl.BlockSpec((N_HEADS, QB, KB), lambda i: (0, 0, 0))],
        out_shape=[jax.ShapeDtypeStruct((s, D_MODEL), BF16), jax.ShapeDtypeStruct((s, D_ATTN), BF16),
                   jax.ShapeDtypeStruct((s, 128), F32), jax.ShapeDtypeStruct((N_HEADS, QB, KB), F32)],
        input_output_aliases={9: 0},
        compiler_params=_params(("arbitrary",), 56),
    )(qkv, qkv, qkv, qkv, qkv, qkv, qkv, by_offset, gg, y_ab)


def attention_bwd(name, qkv, o, lse, dy, bias, gg):
    s = qkv.shape[0]
    nq = s // QB
    scale = HEAD_DIM ** -0.5
    width = 4 * QB

    def body(q_ref, k0, k1, k2, v0, v1, v2, o_ref, lse_ref, dy_ref, b_ref, gg_ref,
             dq_ref, dk_ref, dv_ref, off_ref, dgg_ref, dk_acc, dv_acc, db_acc):
        i = pl.program_id(0)

        @pl.when(i == 0)
        def _():
            dk_acc[...] = jnp.zeros_like(dk_acc)
            dv_acc[...] = jnp.zeros_like(dv_acc)
            db_acc[...] = jnp.zeros_like(db_acc)
            dgg_ref[...] = jnp.zeros_like(dgg_ref)

        rows_of = lambda j: pl.ds(pl.multiple_of((j % 3) * QB, QB), QB)

        @pl.when(i > 0)
        def _():
            dk_acc[rows_of(i), :] = jnp.zeros((QB, D_ATTN), F32)
            dv_acc[rows_of(i), :] = jnp.zeros((QB, D_ATTN), F32)

        def block(at_start):
            ov = o_ref[...].astype(F32)
            dyv = dy_ref[...].astype(F32)
            do, dgg_t = _norm_bwd(ov, _rms(ov), gg_ref[...], dyv)
            dgg_ref[...] += jnp.sum(dgg_t, axis=0, keepdims=True)
            kb = jnp.concatenate([k0[...], k1[...], k2[...]], axis=0)
            vb = jnp.concatenate([v0[...], v1[...], v2[...]], axis=0)
            valid = lax.broadcasted_iota(jnp.int32, (1, KB), 1) >= (2 - i) * QB
            masks = _head_masks()
            lse_t = lse_ref[...]
            for hp in range(N_HEADS // 2):
                sl = slice(2 * HEAD_DIM * hp, 2 * HEAD_DIM * (hp + 1))
                q_p, k_p, v_p = q_ref[:, sl] * scale, kb[:, sl], vb[:, sl]
                do_p = do[:, sl]
                prod = do_p * ov[:, sl]
                do_b = do_p.astype(BF16)
                dq_pair = jnp.zeros((QB, 2 * HEAD_DIM), F32)
                dk_pair = jnp.zeros((KB, 2 * HEAD_DIM), F32)
                dv_pair = jnp.zeros((KB, 2 * HEAD_DIM), F32)
                for a in range(2):
                    h = 2 * hp + a
                    q_m = jnp.where(masks[a], q_p, 0)
                    do_m = jnp.where(masks[a], do_b, 0)
                    sc = _dot(q_m, k_p, "nt") + b_ref[h]
                    if at_start:
                        sc = jnp.where(valid, sc, NEG)
                    p = jnp.exp(sc - _pick_lane(lse_t, h))
                    dp = _dot(do_m, v_p, "nt")
                    delta = jnp.sum(jnp.where(masks[a], prod, 0.0), axis=-1, keepdims=True)
                    ds = p * (dp - delta)
                    db_acc[h] += ds
                    ds_b = ds.astype(BF16)
                    dq_pair = dq_pair + _dot(ds_b, jnp.where(masks[a], k_p, 0))
                    dk_pair = dk_pair + _dot(ds_b, q_m, "tn")
                    dv_pair = dv_pair + _dot(p.astype(BF16), do_m, "tn")
                dq_ref[:, sl] = (dq_pair * scale).astype(dq_ref.dtype)
                for t in range(3):
                    dk_acc[rows_of(i + 1 + t), sl] += dk_pair[QB * t:QB * (t + 1)]
                    dv_acc[rows_of(i + 1 + t), sl] += dv_pair[QB * t:QB * (t + 1)]

        pl.when(i < 2)(lambda: block(True))
        pl.when((i >= 2) & (i < nq))(lambda: block(False))

        dk_ref[...] = dk_acc[rows_of(i + 1), :].astype(dk_ref.dtype)
        dv_ref[...] = dv_acc[rows_of(i + 1), :].astype(dv_ref.dtype)

        @pl.when(i == nq + 1)
        def _():
            sub = lax.broadcasted_iota(jnp.int32, (8, 1), 0)
            pad = jnp.zeros((8, width - KB), F32)
            for h in range(N_HEADS):
                v = jnp.concatenate([db_acc[h, 0:8, :], pad], axis=1)
                for a in range(1, QB // 8):
                    grp = jnp.concatenate([db_acc[h, 8 * a:8 * a + 8, :], pad], axis=1)
                    v = v + pltpu.roll(grp, width - 8 * a, 1)
                for bit in range(3):
                    v = jnp.where(((sub >> bit) & 1) == 1, pltpu.roll(v, width - (1 << bit), 1), v)
                off_ref[h:h + 1, :] = jnp.sum(v, axis=0, keepdims=True)

    qi = lambda i: jnp.minimum(i, nq - 1)
    kblk = lambda col, back: pl.BlockSpec((QB, D_ATTN), lambda i: (jnp.clip(i - back, 0, nq - 1), QKV_COL + col))
    qblk = lambda col: pl.BlockSpec((QB, D_ATTN), lambda i: (qi(i), col))
    done = pl.BlockSpec((QB, D_ATTN), lambda i: (jnp.clip(i - 2, 0, nq - 1), 0))
    return _call(
        body, name=name, grid=(nq + 2,),
        in_specs=[qblk(QKV_COL), kblk(1, 2), kblk(1, 1), kblk(1, 0), kblk(2, 2), kblk(2, 1), kblk(2, 0),
                  qblk(0), pl.BlockSpec((QB, 128), lambda i: (qi(i), 0)), qblk(1),
                  pl.BlockSpec((N_HEADS, QB, KB), lambda i: (0, 0, 0)), pl.BlockSpec((1, D_ATTN), lambda i: (0, 0))],
        out_specs=[qblk(0), done, done, pl.BlockSpec((N_HEADS, width), lambda i: (0, 0)),
                   pl.BlockSpec((1, D_ATTN), lambda i: (0, 0))],
        out_shape=[jax.ShapeDtypeStruct((s, D_ATTN), BF16)] * 3
        + [jax.ShapeDtypeStruct((N_HEADS, width), F32), jax.ShapeDtypeStruct((1, D_ATTN), F32)],
        scratch_shapes=[pltpu.VMEM((KB, D_ATTN), F32), pltpu.VMEM((KB, D_ATTN), F32), pltpu.VMEM((N_HEADS, QB, KB), F32)],
        compiler_params=_params(("arbitrary",), 56),
    )(qkv, qkv, qkv, qkv, qkv, qkv, qkv, o, lse, dy, bias, gg)


def _block_diag(pw):
    out = jnp.zeros((D_POOL, D_POOL), pw.dtype)
    for gi in range(N_POOL):
        out = lax.dynamic_update_slice(out, pw[gi], (gi * POOL_GC, gi * POOL_GC))
    return out


_SMALL = ("pool_w", "pool_scale", "rel_bias", "group_gain", "pre_mix_g", "post_mix_g", "pre_ffn_g", "post_ffn_g")


def _pack(parts, rows):
    flat = jnp.concatenate([p.reshape(-1).astype(F32) for p in parts])
    return jnp.pad(flat, (0, rows * D_MODEL - flat.shape[0])).reshape(rows, D_MODEL)


def _unpack(packed, shapes):
    flat = packed.reshape(-1)
    out, at = [], 0
    for shp in shapes:
        size = int(np.prod(shp))
        out.append(flat[at:at + size].reshape(shp))
        at += size
    return out


def kernel(x, w_in, w_out, conv_w, pool_w, pool_scale, rel_bias, group_gain, pre_mix_g, post_mix_g, pre_ffn_g, post_ffn_g, w_gate_up, w_down, loss_target, m_w_in, m_w_out, m_conv_w, m_pool_w, m_pool_scale, m_rel_bias, m_group_gain, m_pre_mix_g, m_post_mix_g, m_pre_ffn_g, m_post_ffn_g, m_w_gate_up, m_w_down, v_w_in, v_w_out, v_conv_w, v_pool_w, v_pool_scale, v_rel_bias, v_group_gain, v_pre_mix_g, v_post_mix_g, v_pre_ffn_g, v_post_ffn_g, v_w_gate_up, v_w_down):
    depth = w_in.shape[0]
    s = x.shape[1]
    c_me = lax.axis_index("c")
    q_me = 2 * lax.axis_index("x") + lax.axis_index("y")
    dev = 2 * q_me + c_me

    tr = lambda a: jnp.swapaxes(a, 1, 2)
    w_in_t, w_gu_t = tr(w_in), tr(w_gate_up)

    def gather_layer(l, deps):
        first = gather_start(f"gather_a{l}_s", [w_in_t[l].astype(BF16), w_out[l].astype(BF16), conv_w[l]], deps)
        return first, gather_start(f"gather_b{l}_s", [w_gu_t[l].astype(BF16), w_down[l].astype(BF16)], [first["token"]])

    def weights_a(relayed, l, after):
        g_in, g_out, g_cw = gather_finish(f"gather_a{l}_f", relayed, after)
        return (assemble_rows("assemble_in", *g_in), assemble_rows("assemble_out", *g_out),
                assemble_cols("assemble_conv", *g_cw))

    def weights_b(relayed, l, after):
        g_gu, g_dn = gather_finish(f"gather_b{l}_f", relayed, after)
        return assemble_rows("assemble_gu", *g_gu), assemble_rows("assemble_down", *g_dn)

    h = x.reshape(s, D_MODEL)
    started = {0: gather_layer(0, [])}
    xn = rmsnorm_fwd("norm_mix", h, pre_mix_g[0].reshape(1, -1), BF16)
    relay_a = gather_relay("gather_a0_r", started[0][0], started[0][1]["token"])
    wa = weights_a(relay_a, 0, xn)
    saved = []
    for l in range(depth):
        vec = lambda p: p[l].reshape(1, -1)
        wt_in, wf_out, cw_full = wa
        ahead = [started[l][1]["token"]]
        if l + 1 < depth:
            started[l + 1] = gather_layer(l + 1, ahead)
            ahead = [started[l + 1][1]["token"]]
        proj = matmul("proj", xn, wt_in, "nt", BF16, s, 5 * D_ATTN, D_MODEL, 512, 5 * D_ATTN, D_MODEL, deps=ahead)
        relay_b = gather_relay(f"gather_b{l}_r", started[l][1], proj)
        bd = _block_diag(pool_w[l]).astype(BF16)
        gg = vec(group_gain)
        y_ab = conv_pool_fwd("conv_pool_fwd", proj, cw_full, bd, vec(pool_scale), gg[:, :2 * D_CONV],
                             deps=[relay_b["token"]])
        y, o, lse, bias = attention_fwd("attention_fwd", proj, bias_by_offset(rel_bias[l]), gg[:, 2 * D_CONV:], y_ab)
        wt_gu, wf_dn = weights_b(relay_b, l, y)
        if l + 1 < depth:
            relay_a = gather_relay(f"gather_a{l + 1}_r", started[l + 1][0], y)
        mix, h_mid, hn = matmul_norm("mix_out", y, wf_out, h, vec(post_mix_g), then=(vec(pre_ffn_g), BF16),
                                     deps=[relay_a["token"]] if l + 1 < depth else [])
        ff_du, ff_dg, ff = matmul_swiglu("gate_up", hn, wt_gu)
        if l + 1 < depth:
            wa = weights_a(relay_a, l + 1, ff)
            ffo, h_out, xn_next = matmul_norm("ffn_down", ff, wf_dn, h_mid, vec(post_ffn_g),
                                              then=(pre_mix_g[l + 1].reshape(1, -1), BF16))
        else:
            ffo, h_out = matmul_norm("ffn_down_last", ff, wf_dn, h_mid, vec(post_ffn_g))
            xn_next = None
        saved.append((h, xn, proj, None, bd, bias, y, o, lse, mix, h_mid, hn, ff_du, ff_dg, ff, ffo,
                      wt_in, wf_out, cw_full, wt_gu, wf_dn))
        h, xn = h_out, xn_next

    dh, loss_part, dffo, dg_last = loss_head("loss_head", h, loss_target.reshape(s, D_MODEL), saved[-1][15],
                                             post_ffn_g[depth - 1].reshape(1, -1))

    def reduce_begin(name, halves):
        return [k for k, _ in halves], exchange_start(name, plan_swap_cores, len(halves), [snd for _, snd in halves],
                                                      [(k.shape, k.dtype) for k, _ in halves])

    def reduce_relay(name, kept, swapped, after):
        _, got = exchange_wait(name + "_w", swapped, after)
        sums = [add_pairs("reduce_add", k, r) for k, r in zip(kept, got)]
        return exchange_start(name + "_s", plan_to_chips, 3 * len(sums), sums, [(a.shape, a.dtype) for a in sums])

    def reduce_finish(name, relayed, after):
        sums, got = exchange_wait(name, relayed, after)
        return list(zip(sums, got))

    small = {k: [None] * depth for k in _SMALL + ("conv_w",)}
    relayed = {}
    carried = []
    small["post_ffn_g"][depth - 1] = dg_last
    for l in reversed(range(depth)):
        vec = lambda p: p[l].reshape(1, -1)
        (h_in, xn, proj, _, bd, bias, y, o, lse, mix, h_mid, hn, ff_du, ff_dg, ff, ffo,
         wt_in, wf_out, cw_full, wt_gu, wf_dn) = saved[l]
        gg = vec(group_gain)
        tk_w = min(s, 2048)
        gw_dn = matmul("wgrad_down", ff, dffo, "tn", BF16, D_FF, D_MODEL, s, 1408, D_MODEL, tk_w,
                       shard_rows=D_FF // 8, deps=carried)
        dg, du = matmul_swiglu_bwd("dgrad_down", dffo, wf_dn, ff_du, ff_dg)
        gw_gu = matmul("wgrad_gu", [dg, du], hn, "tn", BF16, 2 * D_FF, D_MODEL, s, 1408, D_MODEL, tk_w,
                       shard_rows=2 * D_FF // 8)
        kept_b, swap_b = reduce_begin(f"reduce_b{l}_s", [gw_gu, gw_dn])
        dh_mid, dmix, small["pre_ffn_g"][l], small["post_mix_g"][l] = matmul_norm_bwd(
            "dgrad_gu", [dg, du], wt_gu, 256, h_mid, vec(pre_ffn_g), dh, then=(mix, vec(post_mix_g), BF16),
            deps=[swap_b["token"]])
        relayed[l, "b"] = reduce_relay(f"reduce_b{l}_r", kept_b, swap_b, dmix)
        gw_out = matmul("wgrad_out", y, dmix, "tn", BF16, D_MODEL, D_MODEL, s, D_MODEL, D_MODEL, tk_w,
                        shard_rows=D_MODEL // 8, deps=[relayed[l, "b"]["token"]])
        dy = matmul("dgrad_out", dmix, wf_out, "nt", BF16, s, D_MODEL, D_MODEL, 512, D_MODEL, D_MODEL)
        dpa, dcw, dbd, dps, dgg_ab = conv_pool_bwd("conv_pool_bwd", proj, dy, cw_full, bd, vec(pool_scale),
                                                   gg[:, :2 * D_CONV])
        dq, dk, dv, by_off, dgg_c = attention_bwd("attention_bwd", proj, o, lse, dy, bias, gg[:, 2 * D_CONV:])
        dparts = [dpa, dq, dk, dv]
        gw_in = matmul("wgrad_in", dparts, xn, "tn", BF16, 5 * D_ATTN, D_MODEL, s, 512, D_MODEL, min(s, 2048))
        kept_a, swap_a = reduce_begin(f"reduce_a{l}_s", [split_rows("split_in", gw_in), gw_out])
        if l > 0:
            dh, dffo, small["pre_mix_g"][l], small["post_ffn_g"][l - 1] = matmul_norm_bwd(
                "dgrad_in", dparts, wt_in, 512, h_in, vec(pre_mix_g), dh_mid,
                then=(saved[l - 1][15], post_ffn_g[l - 1].reshape(1, -1), BF16), chunk=32, deps=[swap_a["token"]])
        else:
            dh, small["pre_mix_g"][l] = matmul_norm_bwd("dgrad_in_first", dparts, wt_in, 512, h_in, vec(pre_mix_g),
                                                        dh_mid, chunk=32, deps=[swap_a["token"]])
        relayed[l, "a"] = reduce_relay(f"reduce_a{l}_r", kept_a, swap_a, dh)
        carried = [relayed[l, "a"]["token"]]
        small["conv_w"][l] = dcw
        small["pool_w"][l] = jnp.stack([dbd[gi * POOL_GC:(gi + 1) * POOL_GC, gi * POOL_GC:(gi + 1) * POOL_GC]
                                        for gi in range(N_POOL)])
        small["pool_scale"][l] = dps
        small["rel_bias"][l] = rel_bias_grad(by_off)
        small["group_gain"][l] = jnp.concatenate([dgg_ab, dgg_c], axis=1)
    grad_x = dh.reshape(x.shape)

    small_params = dict(pool_w=pool_w, pool_scale=pool_scale, rel_bias=rel_bias, group_gain=group_gain,
                        pre_mix_g=pre_mix_g, post_mix_g=post_mix_g, pre_ffn_g=pre_ffn_g, post_ffn_g=post_ffn_g)
    small_m = dict(pool_w=m_pool_w, pool_scale=m_pool_scale, rel_bias=m_rel_bias, group_gain=m_group_gain,
                   pre_mix_g=m_pre_mix_g, post_mix_g=m_post_mix_g, pre_ffn_g=m_pre_ffn_g, post_ffn_g=m_post_ffn_g)
    small_v = dict(pool_w=v_pool_w, pool_scale=v_pool_scale, rel_bias=v_rel_bias, group_gain=v_group_gain,
                   pre_mix_g=v_pre_mix_g, post_mix_g=v_post_mix_g, pre_ffn_g=v_pre_ffn_g, post_ffn_g=v_post_ffn_g)
    shapes = [small_params[k].shape for k in _SMALL] + [(depth, 3, D_CONV), (1,)]
    n_small = sum(int(np.prod(shp)) for shp in shapes)
    rows = -(-n_small // (8 * D_MODEL)) * 8
    extra = [jnp.zeros((depth, 3, D_CONV), F32), jnp.zeros((1,), F32)]
    grads_packed = _pack([jnp.stack(small[k]) for k in _SMALL] + [jnp.stack(small["conv_w"]), loss_part[0, 0:1]], rows)
    small_started = gather_start("gather_small_s", [grads_packed], carried)

    big = dict(w_in=(w_in_t, tr(m_w_in), tr(v_w_in)), w_out=(w_out, m_w_out, v_w_out),
               w_gate_up=(w_gu_t, tr(m_w_gate_up), tr(v_w_gate_up)), w_down=(w_down, m_w_down, v_w_down))
    results = dict(w_in=None, w_out=None, w_gate_up=None, w_down=None)
    after = small_started["token"]
    small_relay = None
    for l in reversed(range(depth)):
        slabs_gu, slabs_dn = reduce_finish(f"reduce_b{l}_f", relayed[l, "b"], after)
        slabs_in, slabs_out = reduce_finish(f"reduce_a{l}_f", relayed[l, "a"], slabs_dn[1])
        for k, slabs in (("w_gate_up", slabs_gu), ("w_down", slabs_dn), ("w_in", slabs_in), ("w_out", slabs_out)):
            results[k] = adamw_layer("adamw_" + k, l, *big[k], *slabs, results[k])
        after = results["w_out"][0]
        if l == min(1, depth - 1):
            small_relay = gather_relay("gather_small_r", small_started, after)
    small_parts = gather_finish("gather_small_f", small_relay, after)[0]
    all_small = assemble_rows("assemble_small", *small_parts).reshape(8, rows, D_MODEL)
    res_small = adamw("adamw_small", _pack([small_params[k] for k in _SMALL] + extra, rows),
                      _pack([small_m[k] for k in _SMALL] + extra, rows),
                      _pack([small_v[k] for k in _SMALL] + extra, rows), all_small)
    g_s, d_s, m_s, v_s = (_unpack(r, shapes) for r in res_small)
    loss = g_s[-1][0]
    g_conv = lax.dynamic_slice_in_dim(g_s[-2], dev * conv_w.shape[2], conv_w.shape[2], axis=2)
    results["conv_w"] = adamw("adamw_conv", conv_w, m_conv_w, v_conv_w, g_conv[None])

    names = ("w_in", "w_out", "conv_w") + _SMALL + ("w_gate_up", "w_down")
    for k in ("w_in", "w_gate_up"):
        results[k] = tuple(tr(r) for r in results[k])
    for j, k in enumerate(_SMALL):
        results[k] = (g_s[j], d_s[j], m_s[j], v_s[j])
    return (loss, grad_x, *[results[k][0] for k in names], *[results[k][1] for k in names],
            *[results[k][2] for k in names], *[results[k][3] for k in names])
```

```python
import numpy as np
import jax
import jax.numpy as jnp
from jax import lax
from jax.experimental import pallas as pl
from jax.experimental.pallas import tpu as pltpu

F32 = jnp.float32
BF16 = jnp.bfloat16

CHUNK = 64
D_MODEL = 1024
D_CONV = 256
D_POOL = 256
D_ATTN = 512
HEAD_DIM = 64
N_HEADS = 8
N_POOL = 4
POOL_GC = 64
POOL_WINDOWS = (2, 4, 8, 16)
LEFT_CHUNKS = 8
REL_CLIP = 128
D_FF = 2816
EPS = 1e-6
ADAM_LR, ADAM_B1, ADAM_B2, ADAM_EPS, ADAM_WD, ADAM_STEP = 0.001, 0.9, 0.999, 1e-08, 0.01, 10

QB = 256
KB = 3 * QB
HALO = 16
T_CP = 512
T_ROW = 512
QKV_COL = 2
NEG = -1e30
VMEM_MB = 1 << 20
MESH = pl.DeviceIdType.MESH


def _call(body, **kw):
    call = pl.pallas_call(body, **kw)
    return lambda *args: call(*[_in_hbm(a) for a in args])


def _in_hbm(a):
    return pltpu.with_memory_space_constraint(a, pltpu.HBM) if jnp.issubdtype(a.dtype, jnp.number) else a


def _call_after(deps, n_in, body, **kw):
    deps = tuple(deps)
    if not deps:
        return _call(body, **kw)

    def ordered(*refs):
        body(*refs[:n_in], *refs[n_in + len(deps):])

    kw["in_specs"] = list(kw["in_specs"]) + [pl.BlockSpec(memory_space=pl.ANY)] * len(deps)
    call = _call(ordered, **kw)
    return lambda *args: call(*args, *deps)


def _params(sem, vmem_mb=48):
    return pltpu.CompilerParams(dimension_semantics=sem, vmem_limit_bytes=vmem_mb * VMEM_MB)


_CHIP_FLIPS = ((1, 0), (0, 1), (1, 1))
_HBM = pl.BlockSpec(memory_space=pltpu.HBM)
_SEM = pl.BlockSpec(memory_space=pltpu.SEMAPHORE)
_EFFECT = pltpu.SideEffectType.DATAFLOW_SIDE_EFFECTING


def _flip(v, f):
    return 1 - v if f else v


def _descriptors(plan, srcs, lands, send_sems, recv_sems, sending):
    x, y, c = lax.axis_index("x"), lax.axis_index("y"), lax.axis_index("c")
    return [pltpu.make_async_remote_copy(src_ref=src, dst_ref=dst if sending else land, send_sem=send_sems.at[k],
                                         recv_sem=recv_sems.at[k], device_id=peer, device_id_type=MESH)
            for k, (src, dst, peer, land) in enumerate(plan(srcs, lands, x, y, c))]


def exchange_start(name, plan, n_copies, srcs, land_shapes, deps=()):
    ns, nl = len(srcs), len(land_shapes)

    def body(*refs):
        src_refs, land_refs = refs[:ns], refs[ns:ns + nl]
        send_sems, recv_sems = refs[ns + nl], refs[ns + nl + 1]
        for send in _descriptors(plan, src_refs, land_refs, send_sems, recv_sems, True):
            send.start()
        refs[-1][...] = jnp.zeros_like(refs[-1])

    lands = [lax.empty(shape, dtype) for shape, dtype in land_shapes]
    outs = _call_after(
        deps, ns + nl, body, name=name,
        out_shape=(pltpu.SemaphoreType.DMA((n_copies,)), pltpu.SemaphoreType.DMA((n_copies,)),
                   *[pltpu.HBM(a.shape, a.dtype) for a in srcs], *[pltpu.HBM(shape, dtype) for shape, dtype in land_shapes],
                   jax.ShapeDtypeStruct((8, 128), F32)),
        in_specs=[_HBM] * (ns + nl),
        out_specs=(_SEM, _SEM, *[_HBM] * (ns + nl), pl.BlockSpec(memory_space=pltpu.VMEM)),
        input_output_aliases={j: 2 + j for j in range(ns + nl)},
        compiler_params=pltpu.CompilerParams(has_side_effects=_EFFECT),
    )(*srcs, *lands)
    return dict(plan=plan, sems=outs[:2], srcs=outs[2:2 + ns], lands=outs[2 + ns:2 + ns + nl], token=outs[-1])


def exchange_wait(name, started, after):
    srcs, lands = started["srcs"], started["lands"]
    ns, nl = len(srcs), len(lands)

    def body(*refs):
        src_refs, land_refs = refs[:ns], refs[ns:ns + nl]
        send_sems, recv_sems = refs[ns + nl], refs[ns + nl + 1]
        for wait in _descriptors(started["plan"], src_refs, land_refs, send_sems, recv_sems, False):
            wait.wait_send()
            wait.wait_recv()

    outs = _call(
        body, name=name,
        out_shape=tuple(pltpu.HBM(a.shape, a.dtype) for a in (*srcs, *lands)),
        in_specs=[_HBM] * (ns + nl) + [_SEM, _SEM, pl.BlockSpec(memory_space=pl.ANY)],
        out_specs=tuple([_HBM] * (ns + nl)),
        input_output_aliases={j: j for j in range(ns + nl)},
        compiler_params=pltpu.CompilerParams(has_side_effects=_EFFECT),
    )(*srcs, *lands, *started["sems"], after)
    return list(outs[:ns]), list(outs[ns:])


def plan_from_chips(srcs, lands, x, y, c):
    q = 2 * x + y
    out = []
    for src, land in zip(srcs, lands):
        for fx, fy in _CHIP_FLIPS:
            px, py = _flip(x, fx), _flip(y, fy)
            out.append((src, land.at[q], (px, py, c), land.at[2 * px + py]))
    return out


def plan_to_core(srcs, lands, x, y, c):
    n = len(lands)
    q = 2 * x + y
    out = []
    for own, chips, land in zip(srcs[:n], srcs[n:], lands):
        out.append((own, land.at[q], (x, y, 1 - c), land.at[q]))
        for fx, fy in _CHIP_FLIPS:
            qp = 2 * _flip(x, fx) + _flip(y, fy)
            out.append((chips.at[qp], land.at[qp], (x, y, 1 - c), land.at[qp]))
    return out


def plan_swap_cores(srcs, lands, x, y, c):
    return [(src, land, (x, y, 1 - c), land) for src, land in zip(srcs, lands)]


def plan_to_chips(srcs, lands, x, y, c):
    q = 2 * x + y
    out = []
    for src, land in zip(srcs, lands):
        for fx, fy in _CHIP_FLIPS:
            px, py = _flip(x, fx), _flip(y, fy)
            out.append((src.at[2 * px + py], land.at[q], (px, py, c), land.at[2 * px + py]))
    return out


def _slots(a):
    return ((4,) + a.shape, a.dtype)


def gather_start(name, arrays, deps=()):
    return exchange_start(name, plan_from_chips, 3 * len(arrays), arrays, [_slots(a) for a in arrays], deps)


def gather_relay(name, started, after):
    own, chips = exchange_wait(name + "_w", started, after)
    return exchange_start(name + "_s", plan_to_core, 4 * len(own), own + chips, [_slots(a) for a in own])


def gather_finish(name, relayed, after):
    srcs, cores = exchange_wait(name, relayed, after)
    n = len(cores)
    return list(zip(srcs[:n], srcs[n:], cores))


def _place():
    return 2 * lax.axis_index("x") + lax.axis_index("y"), lax.axis_index("c")


def assemble_cols(name, own, chips, core):
    r, c = own.shape
    t = _row_tile(r, 256)

    def body(own_ref, chips_ref, core_ref, o_ref):
        q_me, c_me = _place()
        for q in range(4):
            mine = jnp.where(q == q_me, own_ref[...], chips_ref[q])
            other = core_ref[q]
            o_ref[:, 2 * q * c:(2 * q + 1) * c] = jnp.where(c_me == 0, mine, other)
            o_ref[:, (2 * q + 1) * c:(2 * q + 2) * c] = jnp.where(c_me == 0, other, mine)

    slots = pl.BlockSpec((4, t, c), lambda i: (0, i, 0))
    return _call(
        body, name=name, grid=(r // t,),
        in_specs=[pl.BlockSpec((t, c), lambda i: (i, 0)), slots, slots],
        out_specs=pl.BlockSpec((t, 8 * c), lambda i: (i, 0)),
        out_shape=jax.ShapeDtypeStruct((r, 8 * c), own.dtype), compiler_params=_params(("parallel",)),
    )(own, chips, core)


def assemble_rows(name, own, chips, core):
    r, c = own.shape

    def body(own_ref, chips_ref, core_ref, o_ref):
        q_me, c_me = _place()
        d = pl.program_id(0)
        mine = jnp.where(d // 2 == q_me, own_ref[...], chips_ref[...])
        o_ref[...] = jnp.where(d % 2 == c_me, mine, core_ref[...])

    slot = pl.BlockSpec((None, r, c), lambda d: (d // 2, 0, 0))
    return _call(
        body, name=name, grid=(8,),
        in_specs=[pl.BlockSpec((r, c), lambda d: (0, 0)), slot, slot],
        out_specs=pl.BlockSpec((r, c), lambda d: (d, 0)),
        out_shape=jax.ShapeDtypeStruct((8 * r, c), own.dtype), compiler_params=_params(("parallel",)),
    )(own, chips, core)


def split_rows(name, dw):
    r8, c = dw.shape
    r = r8 // 8

    def body(dw_ref, keep_ref, send_ref):
        _, c_me = _place()
        d = pl.program_id(0)

        @pl.when(d % 2 == c_me)
        def _():
            keep_ref[...] = dw_ref[...]

        @pl.when(d % 2 != c_me)
        def _():
            send_ref[...] = dw_ref[...]

    slot = pl.BlockSpec((None, r, c), lambda d: (d // 2, 0, 0))
    return _call(
        body, name=name, grid=(8,),
        in_specs=[pl.BlockSpec((r, c), lambda d: (d, 0))], out_specs=[slot, slot],
        out_shape=[jax.ShapeDtypeStruct((4, r, c), dw.dtype)] * 2, compiler_params=_params(("arbitrary",)),
    )(dw)


def _rms(v):
    return lax.rsqrt(jnp.mean(v * v, axis=-1, keepdims=True) + EPS)


def rmsnorm_fwd(name, x, g, out_dtype):
    s, d = x.shape

    def body(x_ref, g_ref, o_ref):
        xv = x_ref[...]
        o_ref[...] = ((xv * _rms(xv)) * g_ref[...]).astype(o_ref.dtype)

    row = pl.BlockSpec((T_ROW, d), lambda i: (i, 0))
    return _call(
        body, name=name, grid=(s // T_ROW,),
        in_specs=[row, pl.BlockSpec((1, d), lambda i: (0, 0))], out_specs=row,
        out_shape=jax.ShapeDtypeStruct((s, d), out_dtype), compiler_params=_params(("parallel",)),
    )(x, g)


def matmul_then(name, a_parts, b, mode, tm, chunk, rows_in, vecs_in, rows_out, n_sums, then, deps=()):
    a_parts = list(a_parts)
    na, nr, nv, no = len(a_parts), len(rows_in), len(vecs_in), len(rows_out)
    s = a_parts[0].shape[0]
    n = s // tm
    d = b.shape[1] if mode == "nn" else b.shape[0]
    offs = [0]
    for p in a_parts:
        offs.append(offs[-1] + p.shape[1])
    n_in = na + 1 + nr + nv

    def body(*refs):
        a_refs, b_ref = refs[:na], refs[na]
        row_refs, vec_refs = refs[na + 1:na + 1 + nr], refs[na + 1 + nr:n_in]
        out_refs, sum_refs = refs[n_in:n_in + no], refs[n_in + no:n_in + no + n_sums]
        acc = refs[-2:]
        i = pl.program_id(0)

        @pl.when(i == 0)
        def _():
            acc[1][...] = jnp.zeros_like(acc[1])
            for s_ref in sum_refs:
                s_ref[...] = jnp.zeros_like(s_ref)

        def step(write, read):
            t = None
            for p in range(na):
                lo, hi = offs[p], offs[p + 1]
                part = _dot(a_refs[p][...], b_ref[lo:hi, :] if mode == "nn" else b_ref[:, lo:hi], mode)
                t = part if t is None else t + part
            write[...] = t
            totals = [0.0] * n_sums
            for r0 in range(0, tm, chunk):
                rs = pl.ds(r0, chunk)
                results, sums = then(read[rs, :], [r.at[rs, :] for r in row_refs], vec_refs)
                for o_ref, val in zip(out_refs, results):
                    o_ref[rs, :] = val.astype(o_ref.dtype)
                totals = [tot + val for tot, val in zip(totals, sums)]
            for s_ref, val in zip(sum_refs, totals):
                s_ref[...] += jnp.where(i > 0, val, 0.0)

        pl.when(i % 2 == 0)(lambda: step(acc[0], acc[1]))
        pl.when(i % 2 == 1)(lambda: step(acc[1], acc[0]))

    ahead = lambda i: (jnp.minimum(i, n - 1), 0)
    behind = lambda i: (jnp.maximum(i - 1, 0), 0)
    fixed = lambda i: (0, 0)
    shapes = [jax.ShapeDtypeStruct((s, d), dt) for dt in rows_out] + [jax.ShapeDtypeStruct((1, d), F32)] * n_sums
    return _call_after(
        deps, n_in, body, name=name, grid=(n + 1,),
        in_specs=[pl.BlockSpec((tm, p.shape[1]), ahead) for p in a_parts] + [pl.BlockSpec(b.shape, fixed)]
        + [pl.BlockSpec((tm, r.shape[1]), behind) for r in rows_in] + [pl.BlockSpec(v.shape, fixed) for v in vecs_in],
        out_specs=[pl.BlockSpec((tm, d), behind)] * no + [pl.BlockSpec((1, d), fixed)] * n_sums,
        out_shape=shapes, scratch_shapes=[pltpu.VMEM((tm, d), F32)] * 2,
        compiler_params=_params(("arbitrary",), 56),
    )(*a_parts, b, *rows_in, *vecs_in)


def matmul_norm(name, a, b, res, g, then=None, deps=()):
    def norm(z, rows, vecs):
        y = rows[0][...] + (z * _rms(z)) * vecs[0][...]
        return [z, y] + ([(y * _rms(y)) * vecs[1][...]] if then else []), []

    return matmul_then(name, [a], b, "nn", 512, 512, [res], [g] + ([then[0]] if then else []),
                       [BF16, F32] + ([then[1]] if then else []), 0, norm, deps)


def matmul_norm_bwd(name, a_parts, b, tm, x, g, res, then=None, chunk=None, deps=()):
    def norms(t, rows, vecs):
        xv = rows[0][...].astype(F32)
        dx, dgt = _norm_bwd(xv, _rms(xv), vecs[0][...], t)
        dx = dx + rows[1][...]
        results, sums = [dx], [jnp.sum(dgt, axis=0, keepdims=True)]
        if then:
            x2 = rows[2][...].astype(F32)
            dx2, dgt2 = _norm_bwd(x2, _rms(x2), vecs[1][...], dx)
            results.append(dx2)
            sums.append(jnp.sum(dgt2, axis=0, keepdims=True))
        return results, sums

    return matmul_then(name, a_parts, b, "nn", tm, chunk or tm, [x, res] + ([then[0]] if then else []),
                       [g] + ([then[1]] if then else []), [F32] + ([then[2]] if then else []),
                       2 if then else 1, norms, deps)


def _norm_bwd(xv, r, g, dy):
    a = dy * g
    dx = r * (a - xv * ((r * r) * jnp.mean(a * xv, axis=-1, keepdims=True)))
    return dx, dy * (xv * r)


def loss_head(name, h, tgt, z, g):
    s, d = h.shape

    def body(h_ref, t_ref, z_ref, g_ref, dh_ref, l_ref, dz_ref, dg_ref):
        e = h_ref[...] - t_ref[...]
        dh = e * (1.0 / d)
        dh_ref[...] = dh
        zv = z_ref[...].astype(F32)
        dz, dgt = _norm_bwd(zv, _rms(zv), g_ref[...], dh)
        dz_ref[...] = dz.astype(dz_ref.dtype)

        @pl.when(pl.program_id(0) == 0)
        def _():
            l_ref[...] = jnp.zeros_like(l_ref)
            dg_ref[...] = jnp.zeros_like(dg_ref)
        part = 0.5 * jnp.sum(jnp.mean(e * e, axis=-1, keepdims=True), axis=0, keepdims=True)
        l_ref[...] += jnp.broadcast_to(part, l_ref.shape)
        dg_ref[...] += jnp.sum(dgt, axis=0, keepdims=True)

    row = pl.BlockSpec((T_ROW, d), lambda i: (i, 0))
    vec = pl.BlockSpec((1, d), lambda i: (0, 0))
    return _call(
        body, name=name, grid=(s // T_ROW,),
        in_specs=[row, row, row, vec], out_specs=[row, pl.BlockSpec((1, 128), lambda i: (0, 0)), row, vec],
        out_shape=[jax.ShapeDtypeStruct((s, d), F32), jax.ShapeDtypeStruct((1, 128), F32),
                   jax.ShapeDtypeStruct((s, d), BF16), jax.ShapeDtypeStruct((1, d), F32)],
        compiler_params=_params(("arbitrary",)),
    )(h, tgt, z, g)


def _row_tile(rows, limit=512):
    t = min(rows, limit)
    while rows % t or (t % 8 and t != rows):
        t -= 1
    return t


def add_pairs(name, a, b):
    shape = a.shape
    a2, b2 = a.reshape(-1, shape[-1]), b.reshape(-1, shape[-1])
    rows, cols = a2.shape
    t = _row_tile(rows)

    def body(a_ref, b_ref, o_ref):
        o_ref[...] = (a_ref[...].astype(F32) + b_ref[...].astype(F32)).astype(o_ref.dtype)

    blk = pl.BlockSpec((t, cols), lambda i: (i, 0))
    out = _call(
        body, name=name, grid=(rows // t,), in_specs=[blk, blk], out_specs=blk,
        out_shape=jax.ShapeDtypeStruct((rows, cols), a.dtype), compiler_params=_params(("parallel",)),
    )(a2, b2)
    return out.reshape(shape)


def _adamw_math(w, m, v, g):
    c1 = 1.0 - ADAM_B1 ** ADAM_STEP
    c2 = 1.0 - ADAM_B2 ** ADAM_STEP
    mn = ADAM_B1 * m + (1.0 - ADAM_B1) * g
    vn = ADAM_B2 * v + (1.0 - ADAM_B2) * (g * g)
    delta = -ADAM_LR * ((mn / c1) / (jnp.sqrt(vn / c2) + ADAM_EPS) + ADAM_WD * w)
    return delta, mn, vn


def _slab_sum(g_ref, n):
    g = g_ref[0].astype(F32)
    for j in range(1, n):
        g = g + g_ref[j].astype(F32)
    return g


def adamw(name, w, m, v, g_slabs, deps=()):
    shape = w.shape
    n = g_slabs.shape[0]
    w2, m2, v2 = (t.reshape(-1, shape[-1]) for t in (w, m, v))
    g3 = g_slabs.reshape(n, -1, shape[-1])
    rows, cols = w2.shape
    t = _row_tile(rows, 256)

    def body(w_ref, m_ref, v_ref, g_ref, go_ref, d_ref, mo_ref, vo_ref):
        g = _slab_sum(g_ref, n)
        go_ref[...] = g
        d_ref[...], mo_ref[...], vo_ref[...] = _adamw_math(w_ref[...], m_ref[...], v_ref[...], g)

    blk = pl.BlockSpec((t, cols), lambda i: (i, 0))
    outs = _call_after(
        deps, 4, body, name=name, grid=(rows // t,),
        in_specs=[blk, blk, blk, pl.BlockSpec((n, t, cols), lambda i: (0, i, 0))], out_specs=[blk] * 4,
        out_shape=[jax.ShapeDtypeStruct((rows, cols), F32)] * 4, compiler_params=_params(("parallel",)),
    )(w2, m2, v2, g3)
    return tuple(o.reshape(shape) for o in outs)


def adamw_layer(name, l, w, m, v, own_sums, chip_sums, into):
    _, rows, cols = w.shape
    t = _row_tile(rows, 256)
    if into is None:
        into = tuple(lax.empty(w.shape, F32) for _ in range(4))

    def body(w_ref, m_ref, v_ref, own_ref, far_ref, a0, a1, a2, a3, go_ref, d_ref, mo_ref, vo_ref):
        del a0, a1, a2, a3
        q_me, _ = _place()
        g = jnp.zeros((t, cols), F32)
        for q in range(4):
            g = g + jnp.where(q == q_me, own_ref[q], far_ref[q]).astype(F32)
        go_ref[...] = g
        d_ref[...], mo_ref[...], vo_ref[...] = _adamw_math(w_ref[...], m_ref[...], v_ref[...], g)

    blk = pl.BlockSpec((None, t, cols), lambda i: (l, i, 0))
    slabs = pl.BlockSpec((4, t, cols), lambda i: (0, i, 0))
    anyw = pl.BlockSpec(memory_space=pl.ANY)
    return _call(
        body, name=name, grid=(rows // t,),
        in_specs=[blk, blk, blk, slabs, slabs] + [anyw] * 4, out_specs=[blk] * 4,
        out_shape=[jax.ShapeDtypeStruct(w.shape, F32)] * 4, input_output_aliases={5: 0, 6: 1, 7: 2, 8: 3},
        compiler_params=_params(("parallel",)),
    )(w, m, v, own_sums, chip_sums, *into)


_DIMS = {"nn": (((1,), (0,)), ((), ())), "nt": (((1,), (1,)), ((), ())), "tn": (((0,), (0,)), ((), ()))}


def _dot(a, b, mode="nn"):
    return lax.dot_general(a, b, _DIMS[mode], preferred_element_type=F32)


def matmul(name, a, b, mode, out_dtype, m, n, k, tm, tn, tk, b_off=(0, 0), shard_rows=0, deps=()):
    gm, gn, gk = m // tm, n // tn, k // tk
    assert gm * tm == m and gn * tn == n and gk * tk == k
    r0, c0 = b_off
    a_parts = list(a) if isinstance(a, (list, tuple)) else [a]
    na = len(a_parts)
    tile = tm if mode == "tn" else tk
    spans, at = [], 0
    for p in a_parts:
        nblk = p.shape[1] // tile
        assert nblk * tile == p.shape[1]
        spans.append((at, nblk))
        at += nblk
    assert at == (gm if mode == "tn" else gk)

    def within(t, span):
        return (t >= span[0]) & (t < span[0] + span[1])

    def local(t, span):
        return jnp.clip(t - span[0], 0, span[1] - 1)

    a_specs = []
    for sp in spans:
        if mode == "tn":
            a_specs.append(pl.BlockSpec((tk, tm), lambda i, j, kk, sp=sp: (jnp.where(within(i, sp), kk, 0), local(i, sp))))
        else:
            a_specs.append(pl.BlockSpec((tm, tk), lambda i, j, kk, sp=sp: (i, local(kk, sp))))
    if mode == "nt":
        b_spec = pl.BlockSpec((tn, tk), lambda i, j, kk: (j + r0, kk + c0))
    else:
        b_spec = pl.BlockSpec((tk, tn), lambda i, j, kk: (kk + r0, j + c0))
    o_spec = pl.BlockSpec((tm, tn), lambda i, j, kk: (i, j))
    single = na == 1 and gk == 1 and not shard_rows
    pairs = tm // (2 * shard_rows) if shard_rows else 0
    assert not shard_rows or (pairs * 2 * shard_rows == tm and m == 8 * shard_rows)

    def body(*refs):
        a_refs, b_ref, o_ref = refs[:na], refs[na], refs[na + 1]
        if single:
            o_ref[...] = _dot(a_refs[0][...], b_ref[...], mode).astype(o_ref.dtype)
            return
        acc_ref = refs[-1]
        i, kk = pl.program_id(0), pl.program_id(2)

        @pl.when(kk == 0)
        def _():
            acc_ref[...] = jnp.zeros_like(acc_ref)

        for pa, sp in enumerate(spans):
            def add(pa=pa):
                acc_ref[...] += _dot(a_refs[pa][...], b_ref[...], mode)
            if na > 1:
                pl.when(within(i if mode == "tn" else kk, sp))(add)
            else:
                add()

        @pl.when(kk == gk - 1)
        def _():
            if not shard_rows:
                o_ref[...] = acc_ref[...].astype(o_ref.dtype)
                return
            send_ref, mine = refs[na + 2], lax.axis_index("c") == 0
            for pq in range(pairs):
                even = acc_ref[2 * shard_rows * pq:2 * shard_rows * pq + shard_rows, :]
                odd = acc_ref[2 * shard_rows * pq + shard_rows:2 * shard_rows * (pq + 1), :]
                o_ref[pq] = jnp.where(mine, even, odd).astype(o_ref.dtype)
                send_ref[pq] = jnp.where(mine, odd, even).astype(send_ref.dtype)

    if shard_rows:
        o_spec = pl.BlockSpec((pairs, shard_rows, tn), lambda i, j, kk: (i, 0, j))
    shard = jax.ShapeDtypeStruct((4, shard_rows, n), out_dtype)
    return _call_after(
        deps, na + 1, body, name=name, grid=(gm, gn, gk),
        in_specs=a_specs + [b_spec], out_specs=[o_spec, o_spec] if shard_rows else o_spec,
        out_shape=[shard, shard] if shard_rows else jax.ShapeDtypeStruct((m, n), out_dtype),
        scratch_shapes=[] if single else [pltpu.VMEM((tm, tn), F32)],
        compiler_params=_params(("parallel", "parallel", "arbitrary")),
    )(*a_parts, b)


def matmul_swiglu(name, a, w_gu_t, deps=()):
    s, d = a.shape
    tm, tn = 512, 1408
    gn = D_FF // tn

    def body(a_ref, wg_ref, wu_ref, du_ref, dg_ref, f_ref):
        av = a_ref[...]
        g = _dot(av, wg_ref[...], "nt")
        u = _dot(av, wu_ref[...], "nt")
        sig = 1.0 / (1.0 + jnp.exp(-g))
        silu = g * sig
        du_ref[...] = silu.astype(du_ref.dtype)
        dg_ref[...] = (u * (sig + silu * (1.0 - sig))).astype(dg_ref.dtype)
        f_ref[...] = (silu * u).astype(f_ref.dtype)

    o_spec = pl.BlockSpec((tm, tn), lambda j, i: (i, j))
    return _call_after(
        deps, 3, body, name=name, grid=(gn, s // tm),
        in_specs=[pl.BlockSpec((tm, d), lambda j, i: (i, 0)), pl.BlockSpec((tn, d), lambda j, i: (j, 0)),
                  pl.BlockSpec((tn, d), lambda j, i: (j + gn, 0))],
        out_specs=[o_spec] * 3, out_shape=[jax.ShapeDtypeStruct((s, D_FF), BF16)] * 3,
        compiler_params=_params(("parallel", "parallel")),
    )(a, w_gu_t, w_gu_t)


def matmul_swiglu_bwd(name, dffo, w_down, ff_du, ff_dg):
    s, d = dffo.shape
    tm, tn = 512, 1408

    def body(a_ref, w_ref, pu_ref, pg_ref, dg_ref, du_ref):
        dff = _dot(a_ref[...], w_ref[...], "nt")
        du_ref[...] = (dff * pu_ref[...].astype(F32)).astype(du_ref.dtype)
        dg_ref[...] = (dff * pg_ref[...].astype(F32)).astype(dg_ref.dtype)

    o_spec = pl.BlockSpec((tm, tn), lambda j, i: (i, j))
    return _call(
        body, name=name, grid=(D_FF // tn, s // tm),
        in_specs=[pl.BlockSpec((tm, d), lambda j, i: (i, 0)), pl.BlockSpec((tn, d), lambda j, i: (j, 0)), o_spec, o_spec],
        out_specs=[o_spec] * 2, out_shape=[jax.ShapeDtypeStruct((s, D_FF), BF16)] * 2,
        compiler_params=_params(("parallel", "parallel")),
    )(dffo, w_down, ff_du, ff_dg)


def _down(v, n):
    return pltpu.roll(v, n, 0)


def _up(v, n):
    return pltpu.roll(v, v.shape[0] - n, 0)


def _by_window(lane, v2, v4, v8, v16):
    return jnp.where(lane < POOL_GC, v2, jnp.where(lane < 2 * POOL_GC, v4, jnp.where(lane < 3 * POOL_GC, v8, v16)))


def _taps(cw_ref):
    return cw_ref[0:1, :], cw_ref[1:2, :], cw_ref[2:3, :]


def _conv_pool_forward(ext, t0, cw, bd):
    n_out = ext.shape[0] - HALO
    gb = ext[HALO:, 0:D_CONV]
    z = ext[:, D_CONV:2 * D_CONV] * ext[:, 2 * D_CONV:3 * D_CONV]
    z0, z1, z2 = z[HALO:], _down(z, 1)[HALO:], _down(z, 2)[HALO:]
    conv = cw[2] * z0 + cw[1] * z1 + cw[0] * z2
    x = ext[:, 3 * D_CONV:]
    w2 = x + _down(x, 1)
    w4 = w2 + _down(w2, 2)
    w8 = w4 + _down(w4, 4)
    w16 = w8 + _down(w8, 8)
    lane = lax.broadcasted_iota(jnp.int32, (1, D_POOL), 1)
    win = _by_window(lane, 2.0, 4.0, 8.0, 16.0)
    pos = (t0 + lax.broadcasted_iota(jnp.int32, (n_out, 1), 0) + 1).astype(F32)
    cnt = jnp.minimum(pos, win)
    d = _by_window(lane, w2, w4, w8, w16)[HALO:] / cnt - x[HALO:]
    ybp = _dot(d.astype(BF16), bd)
    return gb, z0, z1, z2, conv, d, cnt, ybp, lane


def conv_pool_fwd(name, proj_a, cw, bd, ps, gg, deps=()):
    s = proj_a.shape[0]
    t = T_CP
    hb = t // HALO

    def body(main_ref, prev_ref, cw_ref, bd_ref, ps_ref, gg_ref, y_ref):
        i = pl.program_id(0)
        prev = jnp.where(i > 0, prev_ref[...].astype(F32), 0.0)
        ext = jnp.concatenate([prev, main_ref[...].astype(F32)], axis=0)
        gb, _, _, _, conv, _, _, ybp, _ = _conv_pool_forward(ext, i * t, _taps(cw_ref), bd_ref[...])
        ya = gb * conv
        yb = ybp * ps_ref[...]
        ggv = gg_ref[...]
        y_ref[:, 0:D_CONV] = ((ya * _rms(ya)) * ggv[:, 0:D_CONV]).astype(y_ref.dtype)
        y_ref[:, D_CONV:] = ((yb * _rms(yb)) * ggv[:, D_CONV:]).astype(y_ref.dtype)

    full = lambda shape: pl.BlockSpec(shape, lambda i: (0,) * len(shape))
    return _call_after(
        deps, 6, body, name=name, grid=(s // t,),
        in_specs=[pl.BlockSpec((t, D_MODEL), lambda i: (i, 0)),
                  pl.BlockSpec((HALO, D_MODEL), lambda i: (jnp.maximum(i * hb - 1, 0), 0)),
                  full((3, D_CONV)), full((D_POOL, D_POOL)), full((1, D_POOL)), full((1, 2 * D_CONV))],
        out_specs=pl.BlockSpec((t, 2 * D_CONV), lambda i: (i, 0)),
        out_shape=jax.ShapeDtypeStruct((s, D_MODEL), BF16),
        compiler_params=_params(("parallel",)),
    )(proj_a, proj_a, cw, bd, ps, gg)


def conv_pool_bwd(name, proj_a, dy, cw, bd, ps, gg):
    s = proj_a.shape[0]
    t = T_CP
    hb = t // HALO
    nblk = s // t
    last_halo = s // HALO - 1

    def body(main_ref, prev_ref, next_ref, dy_ref, dyn_ref, cw_ref, bd_ref, ps_ref, gg_ref,
             dp_ref, dcw_ref, dbd_ref, dps_ref, dgg_ref):
        i = pl.program_id(0)
        prev = jnp.where(i > 0, prev_ref[...].astype(F32), 0.0)
        main = main_ref[...].astype(F32)
        ext = jnp.concatenate([prev, main, next_ref[...].astype(F32)], axis=0)
        cwv, bdv, psv, ggv = _taps(cw_ref), bd_ref[...], ps_ref[...], gg_ref[...]
        gb, z0, z1, z2, conv, d, cnt, ybp, lane = _conv_pool_forward(ext, i * t, cwv, bdv)
        dyn = jnp.where(i < nblk - 1, dyn_ref[...].astype(F32), 0.0)
        dyv = jnp.concatenate([dy_ref[...].astype(F32), dyn], axis=0)
        ya = gb * conv
        yb = ybp * psv
        dya, dgg_a = _norm_bwd(ya, _rms(ya), ggv[:, 0:D_CONV], dyv[:, 0:D_CONV])
        dyb, dgg_b = _norm_bwd(yb, _rms(yb), ggv[:, D_CONV:], dyv[:, D_CONV:])

        dconv = dya * gb
        dz = (cwv[2] * dconv + cwv[1] * _up(dconv, 1) + cwv[0] * _up(dconv, 2))[:t]
        dp_ref[:, 0:D_CONV] = (dya * conv)[:t].astype(dp_ref.dtype)
        dp_ref[:, D_CONV:2 * D_CONV] = (dz * main[:, 2 * D_CONV:3 * D_CONV]).astype(dp_ref.dtype)
        dp_ref[:, 2 * D_CONV:3 * D_CONV] = (dz * main[:, D_CONV:2 * D_CONV]).astype(dp_ref.dtype)

        dybs = dyb * psv
        dd = _dot(dybs.astype(BF16), bdv, "nt")
        e = dd / cnt
        a2 = e + _up(e, 1)
        a4 = a2 + _up(a2, 2)
        a8 = a4 + _up(a4, 4)
        a16 = a8 + _up(a8, 8)
        dp_ref[:, 3 * D_CONV:] = (_by_window(lane, a2, a4, a8, a16) - dd)[:t].astype(dp_ref.dtype)

        @pl.when(i == 0)
        def _():
            dcw_ref[...] = jnp.zeros_like(dcw_ref)
            dbd_ref[...] = jnp.zeros_like(dbd_ref)
            dps_ref[...] = jnp.zeros_like(dps_ref)
            dgg_ref[...] = jnp.zeros_like(dgg_ref)

        rsum = lambda v: jnp.sum(v[:t], axis=0, keepdims=True)
        dcw_ref[0:1, :] += rsum(dconv * z2)
        dcw_ref[1:2, :] += rsum(dconv * z1)
        dcw_ref[2:3, :] += rsum(dconv * z0)
        dbd_ref[...] += _dot(d[:t].astype(BF16), dybs[:t].astype(BF16), "tn")
        dps_ref[...] += rsum(dyb * ybp)
        dgg_ref[:, 0:D_CONV] += rsum(dgg_a)
        dgg_ref[:, D_CONV:] += rsum(dgg_b)

    full = lambda shape: pl.BlockSpec(shape, lambda i: (0,) * len(shape))
    next_halo = lambda i: (jnp.minimum((i + 1) * hb, last_halo), 0)
    return _call(
        body, name=name, grid=(nblk,),
        in_specs=[pl.BlockSpec((t, D_MODEL), lambda i: (i, 0)),
                  pl.BlockSpec((HALO, D_MODEL), lambda i: (jnp.maximum(i * hb - 1, 0), 0)),
                  pl.BlockSpec((HALO, D_MODEL), next_halo),
                  pl.BlockSpec((t, 2 * D_CONV), lambda i: (i, 0)),
                  pl.BlockSpec((HALO, 2 * D_CONV), next_halo),
                  full((3, D_CONV)), full((D_POOL, D_POOL)), full((1, D_POOL)), full((1, 2 * D_CONV))],
        out_specs=[pl.BlockSpec((t, D_MODEL), lambda i: (i, 0)),
                   full((3, D_CONV)), full((D_POOL, D_POOL)), full((1, D_POOL)), full((1, 2 * D_CONV))],
        out_shape=[jax.ShapeDtypeStruct((s, D_MODEL), BF16), jax.ShapeDtypeStruct((3, D_CONV), F32),
                   jax.ShapeDtypeStruct((D_POOL, D_POOL), F32), jax.ShapeDtypeStruct((1, D_POOL), F32),
                   jax.ShapeDtypeStruct((1, 2 * D_CONV), F32)],
        compiler_params=_params(("arbitrary",)),
    )(proj_a, proj_a, proj_a, dy, dy, cw, bd, ps, gg)


def bias_by_offset(rel_bias):
    n_far = 2 * QB - REL_CLIP + 1
    far = jnp.broadcast_to(rel_bias[:, 2 * REL_CLIP:], (N_HEADS, n_far))
    mid = rel_bias[:, 1:2 * REL_CLIP][:, ::-1]
    near = jnp.broadcast_to(rel_bias[:, 0:1], (N_HEADS, KB - n_far - (2 * REL_CLIP - 1)))
    wrap = jnp.broadcast_to(rel_bias[:, 2 * REL_CLIP:], (N_HEADS, 4 * QB - KB))
    return jnp.concatenate([far, mid, near, wrap], axis=1)


def _fill_bias(off_ref, b_ref):
    width = 4 * QB
    sub = lax.broadcasted_iota(jnp.int32, (8, 1), 0)
    col = lax.broadcasted_iota(jnp.int32, (1, KB), 1)
    for h in range(N_HEADS):
        base = jnp.broadcast_to(off_ref[h:h + 1, :], (8, width))
        for bit in range(3):
            base = jnp.where(((sub >> bit) & 1) == 1, pltpu.roll(base, 1 << bit, 1), base)
        for a in range(QB // 8):
            first = CHUNK * (8 * a // CHUNK)
            rows = (pltpu.roll(base, 8 * a, 1) if a else base)[:, :KB]
            band = (col >= first) & (col < first + (LEFT_CHUNKS + 1) * CHUNK)
            b_ref[h, 8 * a:8 * a + 8, :] = jnp.where(band, rows, NEG)


def rel_bias_grad(by_offset):
    n_far = 2 * QB - REL_CLIP + 1
    hi = jnp.sum(by_offset[:, :n_far], axis=1, keepdims=True) + jnp.sum(by_offset[:, KB:], axis=1, keepdims=True)
    mid = by_offset[:, n_far:n_far + 2 * REL_CLIP - 1][:, ::-1]
    lo = jnp.sum(by_offset[:, n_far + 2 * REL_CLIP - 1:KB], axis=1, keepdims=True)
    return jnp.concatenate([lo, mid, hi], axis=1)


def _head_masks():
    first = lax.broadcasted_iota(jnp.int32, (1, 2 * HEAD_DIM), 1) < HEAD_DIM
    return first, jnp.logical_not(first)


def _pick_lane(tile, h):
    lane = lax.broadcasted_iota(jnp.int32, (1, tile.shape[1]), 1)
    return jnp.sum(jnp.where(lane == h, tile, 0.0), axis=-1, keepdims=True)


def attention_fwd(name, qkv, by_offset, gg, y_ab):
    s = qkv.shape[0]
    nq = s // QB
    scale = HEAD_DIM ** -0.5

    def body(q_ref, k0, k1, k2, v0, v1, v2, off_ref, gg_ref, y_in, y_ref, o_ref, lse_ref, b_ref):
        del y_in
        i = pl.program_id(0)

        @pl.when(i == 0)
        def _():
            _fill_bias(off_ref, b_ref)
        def block(at_start):
            kb = jnp.concatenate([k0[...], k1[...], k2[...]], axis=0)
            vb = jnp.concatenate([v0[...], v1[...], v2[...]], axis=0)
            valid = lax.broadcasted_iota(jnp.int32, (1, KB), 1) >= (2 - i) * QB
            lane = lax.broadcasted_iota(jnp.int32, (1, 128), 1)
            masks = _head_masks()
            lse = jnp.zeros((QB, 128), F32)
            outs = []
            for hp in range(N_HEADS // 2):
                sl = slice(2 * HEAD_DIM * hp, 2 * HEAD_DIM * (hp + 1))
                q_p, k_p, v_p = q_ref[:, sl] * scale, kb[:, sl], vb[:, sl]
                o_pair = jnp.zeros((QB, 2 * HEAD_DIM), F32)
                for a in range(2):
                    h = 2 * hp + a
                    sc = _dot(jnp.where(masks[a], q_p, 0), k_p, "nt") + b_ref[h]
                    if at_start:
                        sc = jnp.where(valid, sc, NEG)
                    mx = jnp.max(sc, axis=-1, keepdims=True)
                    e = jnp.exp(sc - mx).astype(BF16)
                    ones_at = HEAD_DIM * (1 - a)
                    ev = _dot(e, jnp.where(masks[a], v_p, jnp.where(lane == ones_at, 1, 0).astype(v_p.dtype)))
                    l = jnp.sum(jnp.where(lane == ones_at, ev, 0.0), axis=-1, keepdims=True)
                    o_pair = o_pair + jnp.where(masks[a], ev, 0.0) * (1.0 / l)
                    lse = jnp.where(lane == h, mx + jnp.log(l), lse)
                outs.append(o_pair)
            o = jnp.concatenate(outs, axis=1)
            o_ref[...] = o.astype(o_ref.dtype)
            lse_ref[...] = lse
            y_ref[...] = ((o * _rms(o)) * gg_ref[...]).astype(y_ref.dtype)

        pl.when(i < 2)(lambda: block(True))
        pl.when(i >= 2)(lambda: block(False))

    blk = lambda col, back: pl.BlockSpec((QB, D_ATTN), lambda i: (jnp.maximum(i - back, 0), QKV_COL + col))
    return _call(
        body, name=name, grid=(nq,),
        in_specs=[blk(0, 0), blk(1, 2), blk(1, 1), blk(1, 0), blk(2, 2), blk(2, 1), blk(2, 0),
                  pl.BlockSpec((N_HEADS, 4 * QB), lambda i: (0, 0)), pl.BlockSpec((1, D_ATTN), lambda i: (0, 0)),
                  pl.BlockSpec(memory_space=pl.ANY)],
        out_specs=[pl.BlockSpec((QB, D_ATTN), lambda i: (i, 1)), pl.BlockSpec((QB, D_ATTN), lambda i: (i, 0)),
                   pl.BlockSpec((QB, 128), lambda i: (i, 0)), pl.BlockSpec((N_HEADS, QB, KB), lambda i: (0, 0, 0))],
        out_shape=[jax.ShapeDtypeStruct((s, D_MODEL), BF16), jax.ShapeDtypeStruct((s, D_ATTN), BF16),
                   jax.ShapeDtypeStruct((s, 128), F32), jax.ShapeDtypeStruct((N_HEADS, QB, KB), F32)],
        input_output_aliases={9: 0},
        compiler_params=_params(("arbitrary",), 56),
    )(qkv, qkv, qkv, qkv, qkv, qkv, qkv, by_offset, gg, y_ab)


def attention_bwd(name, qkv, o, lse, dy, bias, gg):
    s = qkv.shape[0]
    nq = s // QB
    scale = HEAD_DIM ** -0.5
    width = 4 * QB

    def body(q_ref, k0, k1, k2, v0, v1, v2, o_ref, lse_ref, dy_ref, b_ref, gg_ref,
             dq_ref, dk_ref, dv_ref, off_ref, dgg_ref, dk_acc, dv_acc, db_acc):
        i = pl.program_id(0)

        @pl.when(i == 0)
        def _():
            dk_acc[...] = jnp.zeros_like(dk_acc)
            dv_acc[...] = jnp.zeros_like(dv_acc)
            db_acc[...] = jnp.zeros_like(db_acc)
            dgg_ref[...] = jnp.zeros_like(dgg_ref)

        rows_of = lambda j: pl.ds(pl.multiple_of((j % 3) * QB, QB), QB)

        @pl.when(i > 0)
        def _():
            dk_acc[rows_of(i), :] = jnp.zeros((QB, D_ATTN), F32)
            dv_acc[rows_of(i), :] = jnp.zeros((QB, D_ATTN), F32)

        def block(at_start):
            ov = o_ref[...].astype(F32)
            dyv = dy_ref[...].astype(F32)
            do, dgg_t = _norm_bwd(ov, _rms(ov), gg_ref[...], dyv)
            dgg_ref[...] += jnp.sum(dgg_t, axis=0, keepdims=True)
            kb = jnp.concatenate([k0[...], k1[...], k2[...]], axis=0)
            vb = jnp.concatenate([v0[...], v1[...], v2[...]], axis=0)
            valid = lax.broadcasted_iota(jnp.int32, (1, KB), 1) >= (2 - i) * QB
            masks = _head_masks()
            lse_t = lse_ref[...]
            for hp in range(N_HEADS // 2):
                sl = slice(2 * HEAD_DIM * hp, 2 * HEAD_DIM * (hp + 1))
                q_p, k_p, v_p = q_ref[:, sl] * scale, kb[:, sl], vb[:, sl]
                do_p = do[:, sl]
                prod = do_p * ov[:, sl]
                do_b = do_p.astype(BF16)
                dq_pair = jnp.zeros((QB, 2 * HEAD_DIM), F32)
                dk_pair = jnp.zeros((KB, 2 * HEAD_DIM), F32)
                dv_pair = jnp.zeros((KB, 2 * HEAD_DIM), F32)
                for a in range(2):
                    h = 2 * hp + a
                    q_m = jnp.where(masks[a], q_p, 0)
                    do_m = jnp.where(masks[a], do_b, 0)
                    sc = _dot(q_m, k_p, "nt") + b_ref[h]
                    if at_start:
                        sc = jnp.where(valid, sc, NEG)
                    p = jnp.exp(sc - _pick_lane(lse_t, h))
                    dp = _dot(do_m, v_p, "nt")
                    delta = jnp.sum(jnp.where(masks[a], prod, 0.0), axis=-1, keepdims=True)
                    ds = p * (dp - delta)
                    db_acc[h] += ds
                    ds_b = ds.astype(BF16)
                    dq_pair = dq_pair + _dot(ds_b, jnp.where(masks[a], k_p, 0))
                    dk_pair = dk_pair + _dot(ds_b, q_m, "tn")
                    dv_pair = dv_pair + _dot(p.astype(BF16), do_m, "tn")
                dq_ref[:, sl] = (dq_pair * scale).astype(dq_ref.dtype)
                for t in range(3):
                    dk_acc[rows_of(i + 1 + t), sl] += dk_pair[QB * t:QB * (t + 1)]
                    dv_acc[rows_of(i + 1 + t), sl] += dv_pair[QB * t:QB * (t + 1)]

        pl.when(i < 2)(lambda: block(True))
        pl.when((i >= 2) & (i < nq))(lambda: block(False))

        dk_ref[...] = dk_acc[rows_of(i + 1), :].astype(dk_ref.dtype)
        dv_ref[...] = dv_acc[rows_of(i + 1), :].astype(dv_ref.dtype)

        @pl.when(i == nq + 1)
        def _():
            sub = lax.broadcasted_iota(jnp.int32, (8, 1), 0)
            pad = jnp.zeros((8, width - KB), F32)
            for h in range(N_HEADS):
                v = jnp.concatenate([db_acc[h, 0:8, :], pad], axis=1)
                for a in range(1, QB // 8):
                    grp = jnp.concatenate([db_acc[h, 8 * a:8 * a + 8, :], pad], axis=1)
                    v = v + pltpu.roll(grp, width - 8 * a, 1)
                for bit in range(3):
                    v = jnp.where(((sub >> bit) & 1) == 1, pltpu.roll(v, width - (1 << bit), 1), v)
                off_ref[h:h + 1, :] = jnp.sum(v, axis=0, keepdims=True)

    qi = lambda i: jnp.minimum(i, nq - 1)
    kblk = lambda col, back: pl.BlockSpec((QB, D_ATTN), lambda i: (jnp.clip(i - back, 0, nq - 1), QKV_COL + col))
    qblk = lambda col: pl.BlockSpec((QB, D_ATTN), lambda i: (qi(i), col))
    done = pl.BlockSpec((QB, D_ATTN), lambda i: (jnp.clip(i - 2, 0, nq - 1), 0))
    return _call(
        body, name=name, grid=(nq + 2,),
        in_specs=[qblk(QKV_COL), kblk(1, 2), kblk(1, 1), kblk(1, 0), kblk(2, 2), kblk(2, 1), kblk(2, 0),
                  qblk(0), pl.BlockSpec((QB, 128), lambda i: (qi(i), 0)), qblk(1),
                  pl.BlockSpec((N_HEADS, QB, KB), lambda i: (0, 0, 0)), pl.BlockSpec((1, D_ATTN), lambda i: (0, 0))],
        out_specs=[qblk(0), done, done, pl.BlockSpec((N_HEADS, width), lambda i: (0, 0)),
                   pl.BlockSpec((1, D_ATTN), lambda i: (0, 0))],
        out_shape=[jax.ShapeDtypeStruct((s, D_ATTN), BF16)] * 3
        + [jax.ShapeDtypeStruct((N_HEADS, width), F32), jax.ShapeDtypeStruct((1, D_ATTN), F32)],
        scratch_shapes=[pltpu.VMEM((KB, D_ATTN), F32), pltpu.VMEM((KB, D_ATTN), F32), pltpu.VMEM((N_HEADS, QB, KB), F32)],
        compiler_params=_params(("arbitrary",), 56),
    )(qkv, qkv, qkv, qkv, qkv, qkv, qkv, o, lse, dy, bias, gg)


def _block_diag(pw):
    out = jnp.zeros((D_POOL, D_POOL), pw.dtype)
    for gi in range(N_POOL):
        out = lax.dynamic_update_slice(out, pw[gi], (gi * POOL_GC, gi * POOL_GC))
    return out


_SMALL = ("pool_w", "pool_scale", "rel_bias", "group_gain", "pre_mix_g", "post_mix_g", "pre_ffn_g", "post_ffn_g")


def _pack(parts, rows):
    flat = jnp.concatenate([p.reshape(-1).astype(F32) for p in parts])
    return jnp.pad(flat, (0, rows * D_MODEL - flat.shape[0])).reshape(rows, D_MODEL)


def _unpack(packed, shapes):
    flat = packed.reshape(-1)
    out, at = [], 0
    for shp in shapes:
        size = int(np.prod(shp))
        out.append(flat[at:at + size].reshape(shp))
        at += size
    return out


def kernel(x, w_in, w_out, conv_w, pool_w, pool_scale, rel_bias, group_gain, pre_mix_g, post_mix_g, pre_ffn_g, post_ffn_g, w_gate_up, w_down, loss_target, m_w_in, m_w_out, m_conv_w, m_pool_w, m_pool_scale, m_rel_bias, m_group_gain, m_pre_mix_g, m_post_mix_g, m_pre_ffn_g, m_post_ffn_g, m_w_gate_up, m_w_down, v_w_in, v_w_out, v_conv_w, v_pool_w, v_pool_scale, v_rel_bias, v_group_gain, v_pre_mix_g, v_post_mix_g, v_pre_ffn_g, v_post_ffn_g, v_w_gate_up, v_w_down):
    depth = w_in.shape[0]
    s = x.shape[1]
    c_me = lax.axis_index("c")
    q_me = 2 * lax.axis_index("x") + lax.axis_index("y")
    dev = 2 * q_me + c_me

    tr = lambda a: jnp.swapaxes(a, 1, 2)
    w_in_t, w_gu_t = tr(w_in), tr(w_gate_up)

    def group_a(l):
        return [w_in_t[l].astype(BF16), w_out[l].astype(BF16), conv_w[l]]

    def group_b(l):
        return [w_gu_t[l].astype(BF16), w_down[l].astype(BF16)]

    def weights_a(parts):
        g_in, g_out, g_cw = parts
        return (assemble_rows("assemble_in", *g_in), assemble_rows("assemble_out", *g_out),
                assemble_cols("assemble_conv", *g_cw))

    def weights_b(parts):
        g_gu, g_dn = parts
        return assemble_rows("assemble_gu", *g_gu), assemble_rows("assemble_down", *g_dn)

    h = x.reshape(s, D_MODEL)
    start_a = gather_start("gather_a0_s", group_a(0))
    start_b = gather_start("gather_b0_s", group_b(0), [start_a["token"]])
    xn = rmsnorm_fwd("norm_mix", h, pre_mix_g[0].reshape(1, -1), BF16)
    relay_a = gather_relay("gather_a0_r", start_a, start_b["token"])
    wa = weights_a(gather_finish("gather_a0_f", relay_a, xn))
    wb = None
    ahead = [start_b["token"]]
    saved = []
    for l in range(depth):
        vec = lambda p: p[l].reshape(1, -1)
        wt_in, wf_out, cw_full = wa
        if l + 1 < depth:
            start_next = gather_start(f"gather_{l + 1}_s", group_a(l + 1) + group_b(l + 1), ahead)
            ahead = [start_next["token"]]
        proj = matmul("proj", xn, wt_in, "nt", BF16, s, 5 * D_ATTN, D_MODEL, 1024, 5 * D_ATTN, D_MODEL, deps=ahead)
        if l == 0:
            relay_b = gather_relay("gather_b0_r", start_b, proj)
            ahead = [relay_b["token"]]
        bd = _block_diag(pool_w[l]).astype(BF16)
        gg = vec(group_gain)
        y_ab = conv_pool_fwd("conv_pool_fwd", proj, cw_full, bd, vec(pool_scale), gg[:, :2 * D_CONV], deps=ahead)
        y, o, lse, bias = attention_fwd("attention_fwd", proj, bias_by_offset(rel_bias[l]), gg[:, 2 * D_CONV:], y_ab)
        if l == 0:
            wb = weights_b(gather_finish("gather_b0_f", relay_b, y))
        wt_gu, wf_dn = wb
        if l + 1 < depth:
            relay_next = gather_relay(f"gather_{l + 1}_r", start_next, y)
        mix, h_mid, hn = matmul_norm("mix_out", y, wf_out, h, vec(post_mix_g), then=(vec(pre_ffn_g), BF16),
                                     deps=[relay_next["token"]] if l + 1 < depth else [])
        ff_du, ff_dg, ff = matmul_swiglu("gate_up", hn, wt_gu)
        if l + 1 < depth:
            parts = gather_finish(f"gather_{l + 1}_f", relay_next, ff)
            wa, wb = weights_a(parts[:3]), weights_b(parts[3:])
            ffo, h_out, xn_next = matmul_norm("ffn_down", ff, wf_dn, h_mid, vec(post_ffn_g),
                                              then=(pre_mix_g[l + 1].reshape(1, -1), BF16))
        else:
            ffo, h_out = matmul_norm("ffn_down_last", ff, wf_dn, h_mid, vec(post_ffn_g))
            xn_next = None
        saved.append((h, xn, proj, None, bd, bias, y, o, lse, mix, h_mid, hn, ff_du, ff_dg, ff, ffo,
                      wt_in, wf_out, cw_full, wt_gu, wf_dn))
        h, xn = h_out, xn_next

    dh, loss_part, dffo, dg_last = loss_head("loss_head", h, loss_target.reshape(s, D_MODEL), saved[-1][15],
                                             post_ffn_g[depth - 1].reshape(1, -1))

    def reduce_begin(name, halves):
        return [k for k, _ in halves], exchange_start(name, plan_swap_cores, len(halves), [snd for _, snd in halves],
                                                      [(k.shape, k.dtype) for k, _ in halves])

    def reduce_relay(name, kept, swapped, after):
        _, got = exchange_wait(name + "_w", swapped, after)
        sums = [add_pairs("reduce_add", k, r) for k, r in zip(kept, got)]
        return exchange_start(name + "_s", plan_to_chips, 3 * len(sums), sums, [(a.shape, a.dtype) for a in sums])

    def reduce_finish(name, relayed, after):
        sums, got = exchange_wait(name, relayed, after)
        return list(zip(sums, got))

    small = {k: [None] * depth for k in _SMALL + ("conv_w",)}
    relayed = {}
    carried = []
    small["post_ffn_g"][depth - 1] = dg_last
    for l in reversed(range(depth)):
        vec = lambda p: p[l].reshape(1, -1)
        (h_in, xn, proj, _, bd, bias, y, o, lse, mix, h_mid, hn, ff_du, ff_dg, ff, ffo,
         wt_in, wf_out, cw_full, wt_gu, wf_dn) = saved[l]
        gg = vec(group_gain)
        tk_w = min(s, 2048)
        gw_dn = matmul("wgrad_down", ff, dffo, "tn", BF16, D_FF, D_MODEL, s, 1408, D_MODEL, tk_w,
                       shard_rows=D_FF // 8, deps=carried)
        dg, du = matmul_swiglu_bwd("dgrad_down", dffo, wf_dn, ff_du, ff_dg)
        gw_gu = matmul("wgrad_gu", [dg, du], hn, "tn", BF16, 2 * D_FF, D_MODEL, s, 1408, D_MODEL, tk_w,
                       shard_rows=2 * D_FF // 8)
        early = [gw_gu, gw_dn] if l == 0 else []
        if early:
            kept_b, swap_b = reduce_begin("reduce_b0_s", early)
            carried = [swap_b["token"]]
        dh_mid, dmix, small["pre_ffn_g"][l], small["post_mix_g"][l] = matmul_norm_bwd(
            "dgrad_gu", [dg, du], wt_gu, 256, h_mid, vec(pre_ffn_g), dh, then=(mix, vec(post_mix_g), BF16),
            deps=carried)
        if early:
            relayed[l, "b"] = reduce_relay("reduce_b0_r", kept_b, swap_b, dmix)
            carried = [relayed[l, "b"]["token"]]
        gw_out = matmul("wgrad_out", y, dmix, "tn", BF16, D_MODEL, D_MODEL, s, D_MODEL, D_MODEL, tk_w,
                        shard_rows=D_MODEL // 8, deps=carried)
        dy = matmul("dgrad_out", dmix, wf_out, "nt", BF16, s, D_MODEL, D_MODEL, 512, D_MODEL, D_MODEL)
        dpa, dcw, dbd, dps, dgg_ab = conv_pool_bwd("conv_pool_bwd", proj, dy, cw_full, bd, vec(pool_scale),
                                                   gg[:, :2 * D_CONV])
        dq, dk, dv, by_off, dgg_c = attention_bwd("attention_bwd", proj, o, lse, dy, bias, gg[:, 2 * D_CONV:])
        dparts = [dpa, dq, dk, dv]
        gw_in = matmul("wgrad_in", dparts, xn, "tn", BF16, 5 * D_ATTN, D_MODEL, s, 512, D_MODEL, min(s, 2048))
        late = [split_rows("split_in", gw_in), gw_out] + ([] if early else [gw_gu, gw_dn])
        kept_a, swap_a = reduce_begin(f"reduce_a{l}_s", late)
        if l > 0:
            dh, dffo, small["pre_mix_g"][l], small["post_ffn_g"][l - 1] = matmul_norm_bwd(
                "dgrad_in", dparts, wt_in, 512, h_in, vec(pre_mix_g), dh_mid,
                then=(saved[l - 1][15], post_ffn_g[l - 1].reshape(1, -1), BF16), chunk=32, deps=[swap_a["token"]])
        else:
            dh, small["pre_mix_g"][l] = matmul_norm_bwd("dgrad_in_first", dparts, wt_in, 512, h_in, vec(pre_mix_g),
                                                        dh_mid, chunk=32, deps=[swap_a["token"]])
        relayed[l, "a"] = reduce_relay(f"reduce_a{l}_r", kept_a, swap_a, dh)
        carried = [relayed[l, "a"]["token"]]
        small["conv_w"][l] = dcw
        small["pool_w"][l] = jnp.stack([dbd[gi * POOL_GC:(gi + 1) * POOL_GC, gi * POOL_GC:(gi + 1) * POOL_GC]
                                        for gi in range(N_POOL)])
        small["pool_scale"][l] = dps
        small["rel_bias"][l] = rel_bias_grad(by_off)
        small["group_gain"][l] = jnp.concatenate([dgg_ab, dgg_c], axis=1)
    grad_x = dh.reshape(x.shape)

    small_params = dict(pool_w=pool_w, pool_scale=pool_scale, rel_bias=rel_bias, group_gain=group_gain,
                        pre_mix_g=pre_mix_g, post_mix_g=post_mix_g, pre_ffn_g=pre_ffn_g, post_ffn_g=post_ffn_g)
    small_m = dict(pool_w=m_pool_w, pool_scale=m_pool_scale, rel_bias=m_rel_bias, group_gain=m_group_gain,
                   pre_mix_g=m_pre_mix_g, post_mix_g=m_post_mix_g, pre_ffn_g=m_pre_ffn_g, post_ffn_g=m_post_ffn_g)
    small_v = dict(pool_w=v_pool_w, pool_scale=v_pool_scale, rel_bias=v_rel_bias, group_gain=v_group_gain,
                   pre_mix_g=v_pre_mix_g, post_mix_g=v_post_mix_g, pre_ffn_g=v_pre_ffn_g, post_ffn_g=v_post_ffn_g)
    shapes = [small_params[k].shape for k in _SMALL] + [(depth, 3, D_CONV), (1,)]
    n_small = sum(int(np.prod(shp)) for shp in shapes)
    rows = -(-n_small // (8 * D_MODEL)) * 8
    extra = [jnp.zeros((depth, 3, D_CONV), F32), jnp.zeros((1,), F32)]
    grads_packed = _pack([jnp.stack(small[k]) for k in _SMALL] + [jnp.stack(small["conv_w"]), loss_part[0, 0:1]], rows)
    small_started = gather_start("gather_small_s", [grads_packed], carried)

    big = dict(w_in=(w_in_t, tr(m_w_in), tr(v_w_in)), w_out=(w_out, m_w_out, v_w_out),
               w_gate_up=(w_gu_t, tr(m_w_gate_up), tr(v_w_gate_up)), w_down=(w_down, m_w_down, v_w_down))
    results = dict(w_in=None, w_out=None, w_gate_up=None, w_down=None)
    after = small_started["token"]
    small_relay = None
    for l in reversed(range(depth)):
        if (l, "b") in relayed:
            slabs_gu, slabs_dn = reduce_finish(f"reduce_b{l}_f", relayed[l, "b"], after)
            slabs_in, slabs_out = reduce_finish(f"reduce_a{l}_f", relayed[l, "a"], slabs_dn[1])
        else:
            slabs_in, slabs_out, slabs_gu, slabs_dn = reduce_finish(f"reduce_a{l}_f", relayed[l, "a"], after)
        for k, slabs in (("w_gate_up", slabs_gu), ("w_down", slabs_dn), ("w_in", slabs_in), ("w_out", slabs_out)):
            results[k] = adamw_layer("adamw_" + k, l, *big[k], *slabs, results[k])
        after = results["w_out"][0]
        if l == min(1, depth - 1):
            small_relay = gather_relay("gather_small_r", small_started, after)
    small_parts = gather_finish("gather_small_f", small_relay, after)[0]
    all_small = assemble_rows("assemble_small", *small_parts).reshape(8, rows, D_MODEL)
    res_small = adamw("adamw_small", _pack([small_params[k] for k in _SMALL] + extra, rows),
                      _pack([small_m[k] for k in _SMALL] + extra, rows),
                      _pack([small_v[k] for k in _SMALL] + extra, rows), all_small)
    g_s, d_s, m_s, v_s = (_unpack(r, shapes) for r in res_small)
    loss = g_s[-1][0]
    g_conv = lax.dynamic_slice_in_dim(g_s[-2], dev * conv_w.shape[2], conv_w.shape[2], axis=2)
    results["conv_w"] = adamw("adamw_conv", conv_w, m_conv_w, v_conv_w, g_conv[None])

    names = ("w_in", "w_out", "conv_w") + _SMALL + ("w_gate_up", "w_down")
    for k in ("w_in", "w_gate_up"):
        results[k] = tuple(tr(r) for r in results[k])
    for j, k in enumerate(_SMALL):
        results[k] = (g_s[j], d_s[j], m_s[j], v_s[j])
    return (loss, grad_x, *[results[k][0] for k in names], *[results[k][1] for k in names],
            *[results[k][2] for k in names], *[results[k][3] for k in names])
```

```python
import numpy as np
import jax
import jax.numpy as jnp
from jax import lax
from jax.experimental import pallas as pl
from jax.experimental.pallas import tpu as pltpu

F32 = jnp.float32
BF16 = jnp.bfloat16

CHUNK = 64
D_MODEL = 1024
D_CONV = 256
D_POOL = 256
D_ATTN = 512
HEAD_DIM = 64
N_HEADS = 8
N_POOL = 4
POOL_GC = 64
POOL_WINDOWS = (2, 4, 8, 16)
LEFT_CHUNKS = 8
REL_CLIP = 128
D_FF = 2816
EPS = 1e-6
ADAM_LR, ADAM_B1, ADAM_B2, ADAM_EPS, ADAM_WD, ADAM_STEP = 0.001, 0.9, 0.999, 1e-08, 0.01, 10

QB = 256
KB = 3 * QB
HALO = 16
T_CP = 512
T_ROW = 512
QKV_COL = 2
NEG = -1e30
VMEM_MB = 1 << 20
MESH = pl.DeviceIdType.MESH


def _call(body, **kw):
    call = pl.pallas_call(body, **kw)
    return lambda *args: call(*[_in_hbm(a) for a in args])


def _in_hbm(a):
    return pltpu.with_memory_space_constraint(a, pltpu.HBM) if jnp.issubdtype(a.dtype, jnp.number) else a


def _call_after(deps, n_in, body, **kw):
    deps = tuple(deps)
    if not deps:
        return _call(body, **kw)

    def ordered(*refs):
        body(*refs[:n_in], *refs[n_in + len(deps):])

    kw["in_specs"] = list(kw["in_specs"]) + [pl.BlockSpec(memory_space=pl.ANY)] * len(deps)
    call = _call(ordered, **kw)
    return lambda *args: call(*args, *deps)


def _params(sem, vmem_mb=48):
    return pltpu.CompilerParams(dimension_semantics=sem, vmem_limit_bytes=vmem_mb * VMEM_MB)


_CHIP_FLIPS = ((1, 0), (0, 1), (1, 1))
_HBM = pl.BlockSpec(memory_space=pltpu.HBM)
_SEM = pl.BlockSpec(memory_space=pltpu.SEMAPHORE)
_EFFECT = pltpu.SideEffectType.DATAFLOW_SIDE_EFFECTING


def _flip(v, f):
    return 1 - v if f else v


def _descriptors(plan, srcs, lands, send_sems, recv_sems, sending):
    x, y, c = lax.axis_index("x"), lax.axis_index("y"), lax.axis_index("c")
    return [pltpu.make_async_remote_copy(src_ref=src, dst_ref=dst if sending else land, send_sem=send_sems.at[k],
                                         recv_sem=recv_sems.at[k], device_id=peer, device_id_type=MESH)
            for k, (src, dst, peer, land) in enumerate(plan(srcs, lands, x, y, c))]


def exchange_start(name, plan, n_copies, srcs, land_shapes, deps=()):
    ns, nl = len(srcs), len(land_shapes)

    def body(*refs):
        src_refs, land_refs = refs[:ns], refs[ns:ns + nl]
        send_sems, recv_sems = refs[ns + nl], refs[ns + nl + 1]
        for send in _descriptors(plan, src_refs, land_refs, send_sems, recv_sems, True):
            send.start()
        refs[-1][...] = jnp.zeros_like(refs[-1])

    lands = [lax.empty(shape, dtype) for shape, dtype in land_shapes]
    outs = _call_after(
        deps, ns + nl, body, name=name,
        out_shape=(pltpu.SemaphoreType.DMA((n_copies,)), pltpu.SemaphoreType.DMA((n_copies,)),
                   *[pltpu.HBM(a.shape, a.dtype) for a in srcs], *[pltpu.HBM(shape, dtype) for shape, dtype in land_shapes],
                   jax.ShapeDtypeStruct((8, 128), F32)),
        in_specs=[_HBM] * (ns + nl),
        out_specs=(_SEM, _SEM, *[_HBM] * (ns + nl), pl.BlockSpec(memory_space=pltpu.VMEM)),
        input_output_aliases={j: 2 + j for j in range(ns + nl)},
        compiler_params=pltpu.CompilerParams(has_side_effects=_EFFECT),
    )(*srcs, *lands)
    return dict(plan=plan, sems=outs[:2], srcs=outs[2:2 + ns], lands=outs[2 + ns:2 + ns + nl], token=outs[-1])


def exchange_wait(name, started, after):
    srcs, lands = started["srcs"], started["lands"]
    ns, nl = len(srcs), len(lands)

    def body(*refs):
        src_refs, land_refs = refs[:ns], refs[ns:ns + nl]
        send_sems, recv_sems = refs[ns + nl], refs[ns + nl + 1]
        for wait in _descriptors(started["plan"], src_refs, land_refs, send_sems, recv_sems, False):
            wait.wait_send()
            wait.wait_recv()

    outs = _call(
        body, name=name,
        out_shape=tuple(pltpu.HBM(a.shape, a.dtype) for a in (*srcs, *lands)),
        in_specs=[_HBM] * (ns + nl) + [_SEM, _SEM, pl.BlockSpec(memory_space=pl.ANY)],
        out_specs=tuple([_HBM] * (ns + nl)),
        input_output_aliases={j: j for j in range(ns + nl)},
        compiler_params=pltpu.CompilerParams(has_side_effects=_EFFECT),
    )(*srcs, *lands, *started["sems"], after)
    return list(outs[:ns]), list(outs[ns:])


def plan_from_chips(srcs, lands, x, y, c):
    q = 2 * x + y
    out = []
    for src, land in zip(srcs, lands):
        for fx, fy in _CHIP_FLIPS:
            px, py = _flip(x, fx), _flip(y, fy)
            out.append((src, land.at[q], (px, py, c), land.at[2 * px + py]))
    return out


def plan_to_core(srcs, lands, x, y, c):
    n = len(lands)
    q = 2 * x + y
    out = []
    for own, chips, land in zip(srcs[:n], srcs[n:], lands):
        out.append((own, land.at[q], (x, y, 1 - c), land.at[q]))
        for fx, fy in _CHIP_FLIPS:
            qp = 2 * _flip(x, fx) + _flip(y, fy)
            out.append((chips.at[qp], land.at[qp], (x, y, 1 - c), land.at[qp]))
    return out


def plan_swap_cores(srcs, lands, x, y, c):
    return [(src, land, (x, y, 1 - c), land) for src, land in zip(srcs, lands)]


def plan_to_chips(srcs, lands, x, y, c):
    q = 2 * x + y
    out = []
    for src, land in zip(srcs, lands):
        for fx, fy in _CHIP_FLIPS:
            px, py = _flip(x, fx), _flip(y, fy)
            out.append((src.at[2 * px + py], land.at[q], (px, py, c), land.at[2 * px + py]))
    return out


def _slots(a):
    return ((4,) + a.shape, a.dtype)


def gather_start(name, arrays, deps=()):
    return exchange_start(name, plan_from_chips, 3 * len(arrays), arrays, [_slots(a) for a in arrays], deps)


def gather_relay(name, started, after):
    own, chips = exchange_wait(name + "_w", started, after)
    return exchange_start(name + "_s", plan_to_core, 4 * len(own), own + chips, [_slots(a) for a in own])


def gather_finish(name, relayed, after):
    srcs, cores = exchange_wait(name, relayed, after)
    n = len(cores)
    return list(zip(srcs[:n], srcs[n:], cores))


def _place():
    return 2 * lax.axis_index("x") + lax.axis_index("y"), lax.axis_index("c")


def assemble_cols(name, own, chips, core):
    r, c = own.shape
    t = _row_tile(r, 256)

    def body(own_ref, chips_ref, core_ref, o_ref):
        q_me, c_me = _place()
        for q in range(4):
            mine = jnp.where(q == q_me, own_ref[...], chips_ref[q])
            other = core_ref[q]
            o_ref[:, 2 * q * c:(2 * q + 1) * c] = jnp.where(c_me == 0, mine, other)
            o_ref[:, (2 * q + 1) * c:(2 * q + 2) * c] = jnp.where(c_me == 0, other, mine)

    slots = pl.BlockSpec((4, t, c), lambda i: (0, i, 0))
    return _call(
        body, name=name, grid=(r // t,),
        in_specs=[pl.BlockSpec((t, c), lambda i: (i, 0)), slots, slots],
        out_specs=pl.BlockSpec((t, 8 * c), lambda i: (i, 0)),
        out_shape=jax.ShapeDtypeStruct((r, 8 * c), own.dtype), compiler_params=_params(("parallel",)),
    )(own, chips, core)


def assemble_rows(name, own, chips, core):
    r, c = own.shape

    def body(own_ref, chips_ref, core_ref, o_ref):
        q_me, c_me = _place()
        d = pl.program_id(0)
        mine = jnp.where(d // 2 == q_me, own_ref[...], chips_ref[...])
        o_ref[...] = jnp.where(d % 2 == c_me, mine, core_ref[...])

    slot = pl.BlockSpec((None, r, c), lambda d: (d // 2, 0, 0))
    return _call(
        body, name=name, grid=(8,),
        in_specs=[pl.BlockSpec((r, c), lambda d: (0, 0)), slot, slot],
        out_specs=pl.BlockSpec((r, c), lambda d: (d, 0)),
        out_shape=jax.ShapeDtypeStruct((8 * r, c), own.dtype), compiler_params=_params(("parallel",)),
    )(own, chips, core)


def split_rows(name, dw):
    r8, c = dw.shape
    r = r8 // 8

    def body(dw_ref, keep_ref, send_ref):
        _, c_me = _place()
        d = pl.program_id(0)

        @pl.when(d % 2 == c_me)
        def _():
            keep_ref[...] = dw_ref[...]

        @pl.when(d % 2 != c_me)
        def _():
            send_ref[...] = dw_ref[...]

    slot = pl.BlockSpec((None, r, c), lambda d: (d // 2, 0, 0))
    return _call(
        body, name=name, grid=(8,),
        in_specs=[pl.BlockSpec((r, c), lambda d: (d, 0))], out_specs=[slot, slot],
        out_shape=[jax.ShapeDtypeStruct((4, r, c), dw.dtype)] * 2, compiler_params=_params(("arbitrary",)),
    )(dw)


def _rms(v):
    return lax.rsqrt(jnp.mean(v * v, axis=-1, keepdims=True) + EPS)


def rmsnorm_fwd(name, x, g, out_dtype):
    s, d = x.shape

    def body(x_ref, g_ref, o_ref):
        xv = x_ref[...]
        o_ref[...] = ((xv * _rms(xv)) * g_ref[...]).astype(o_ref.dtype)

    row = pl.BlockSpec((T_ROW, d), lambda i: (i, 0))
    return _call(
        body, name=name, grid=(s // T_ROW,),
        in_specs=[row, pl.BlockSpec((1, d), lambda i: (0, 0))], out_specs=row,
        out_shape=jax.ShapeDtypeStruct((s, d), out_dtype), compiler_params=_params(("parallel",)),
    )(x, g)


def matmul_then(name, a_parts, b, mode, tm, chunk, rows_in, vecs_in, rows_out, n_sums, then, deps=()):
    a_parts = list(a_parts)
    na, nr, nv, no = len(a_parts), len(rows_in), len(vecs_in), len(rows_out)
    s = a_parts[0].shape[0]
    n = s // tm
    d = b.shape[1] if mode == "nn" else b.shape[0]
    offs = [0]
    for p in a_parts:
        offs.append(offs[-1] + p.shape[1])
    n_in = na + 1 + nr + nv

    def body(*refs):
        a_refs, b_ref = refs[:na], refs[na]
        row_refs, vec_refs = refs[na + 1:na + 1 + nr], refs[na + 1 + nr:n_in]
        out_refs, sum_refs = refs[n_in:n_in + no], refs[n_in + no:n_in + no + n_sums]
        acc = refs[-2:]
        i = pl.program_id(0)

        @pl.when(i == 0)
        def _():
            acc[1][...] = jnp.zeros_like(acc[1])
            for s_ref in sum_refs:
                s_ref[...] = jnp.zeros_like(s_ref)

        def step(write, read):
            t = None
            for p in range(na):
                lo, hi = offs[p], offs[p + 1]
                part = _dot(a_refs[p][...], b_ref[lo:hi, :] if mode == "nn" else b_ref[:, lo:hi], mode)
                t = part if t is None else t + part
            write[...] = t
            totals = [0.0] * n_sums
            for r0 in range(0, tm, chunk):
                rs = pl.ds(r0, chunk)
                results, sums = then(read[rs, :], [r.at[rs, :] for r in row_refs], vec_refs)
                for o_ref, val in zip(out_refs, results):
                    o_ref[rs, :] = val.astype(o_ref.dtype)
                totals = [tot + val for tot, val in zip(totals, sums)]
            for s_ref, val in zip(sum_refs, totals):
                s_ref[...] += jnp.where(i > 0, val, 0.0)

        pl.when(i % 2 == 0)(lambda: step(acc[0], acc[1]))
        pl.when(i % 2 == 1)(lambda: step(acc[1], acc[0]))

    ahead = lambda i: (jnp.minimum(i, n - 1), 0)
    behind = lambda i: (jnp.maximum(i - 1, 0), 0)
    fixed = lambda i: (0, 0)
    shapes = [jax.ShapeDtypeStruct((s, d), dt) for dt in rows_out] + [jax.ShapeDtypeStruct((1, d), F32)] * n_sums
    return _call_after(
        deps, n_in, body, name=name, grid=(n + 1,),
        in_specs=[pl.BlockSpec((tm, p.shape[1]), ahead) for p in a_parts] + [pl.BlockSpec(b.shape, fixed)]
        + [pl.BlockSpec((tm, r.shape[1]), behind) for r in rows_in] + [pl.BlockSpec(v.shape, fixed) for v in vecs_in],
        out_specs=[pl.BlockSpec((tm, d), behind)] * no + [pl.BlockSpec((1, d), fixed)] * n_sums,
        out_shape=shapes, scratch_shapes=[pltpu.VMEM((tm, d), F32)] * 2,
        compiler_params=_params(("arbitrary",), 56),
    )(*a_parts, b, *rows_in, *vecs_in)


def matmul_norm(name, a, b, res, g, then=None, deps=()):
    def norm(z, rows, vecs):
        y = rows[0][...] + (z * _rms(z)) * vecs[0][...]
        return [z, y] + ([(y * _rms(y)) * vecs[1][...]] if then else []), []

    return matmul_then(name, [a], b, "nn", 512, 512, [res], [g] + ([then[0]] if then else []),
                       [BF16, F32] + ([then[1]] if then else []), 0, norm, deps)


def matmul_norm_bwd(name, a_parts, b, tm, x, g, res, then=None, chunk=None, deps=()):
    def norms(t, rows, vecs):
        xv = rows[0][...].astype(F32)
        dx, dgt = _norm_bwd(xv, _rms(xv), vecs[0][...], t)
        dx = dx + rows[1][...]
        results, sums = [dx], [jnp.sum(dgt, axis=0, keepdims=True)]
        if then:
            x2 = rows[2][...].astype(F32)
            dx2, dgt2 = _norm_bwd(x2, _rms(x2), vecs[1][...], dx)
            results.append(dx2)
            sums.append(jnp.sum(dgt2, axis=0, keepdims=True))
        return results, sums

    return matmul_then(name, a_parts, b, "nn", tm, chunk or tm, [x, res] + ([then[0]] if then else []),
                       [g] + ([then[1]] if then else []), [F32] + ([then[2]] if then else []),
                       2 if then else 1, norms, deps)


def _norm_bwd(xv, r, g, dy):
    a = dy * g
    dx = r * (a - xv * ((r * r) * jnp.mean(a * xv, axis=-1, keepdims=True)))
    return dx, dy * (xv * r)


def loss_head(name, h, tgt, z, g):
    s, d = h.shape

    def body(h_ref, t_ref, z_ref, g_ref, dh_ref, l_ref, dz_ref, dg_ref):
        e = h_ref[...] - t_ref[...]
        dh = e * (1.0 / d)
        dh_ref[...] = dh
        zv = z_ref[...].astype(F32)
        dz, dgt = _norm_bwd(zv, _rms(zv), g_ref[...], dh)
        dz_ref[...] = dz.astype(dz_ref.dtype)

        @pl.when(pl.program_id(0) == 0)
        def _():
            l_ref[...] = jnp.zeros_like(l_ref)
            dg_ref[...] = jnp.zeros_like(dg_ref)
        part = 0.5 * jnp.sum(jnp.mean(e * e, axis=-1, keepdims=True), axis=0, keepdims=True)
        l_ref[...] += jnp.broadcast_to(part, l_ref.shape)
        dg_ref[...] += jnp.sum(dgt, axis=0, keepdims=True)

    row = pl.BlockSpec((T_ROW, d), lambda i: (i, 0))
    vec = pl.BlockSpec((1, d), lambda i: (0, 0))
    return _call(
        body, name=name, grid=(s // T_ROW,),
        in_specs=[row, row, row, vec], out_specs=[row, pl.BlockSpec((1, 128), lambda i: (0, 0)), row, vec],
        out_shape=[jax.ShapeDtypeStruct((s, d), F32), jax.ShapeDtypeStruct((1, 128), F32),
                   jax.ShapeDtypeStruct((s, d), BF16), jax.ShapeDtypeStruct((1, d), F32)],
        compiler_params=_params(("arbitrary",)),
    )(h, tgt, z, g)


def _row_tile(rows, limit=512):
    t = min(rows, limit)
    while rows % t or (t % 8 and t != rows):
        t -= 1
    return t


def add_pairs(name, a, b):
    shape = a.shape
    a2, b2 = a.reshape(-1, shape[-1]), b.reshape(-1, shape[-1])
    rows, cols = a2.shape
    t = _row_tile(rows)

    def body(a_ref, b_ref, o_ref):
        o_ref[...] = (a_ref[...].astype(F32) + b_ref[...].astype(F32)).astype(o_ref.dtype)

    blk = pl.BlockSpec((t, cols), lambda i: (i, 0))
    out = _call(
        body, name=name, grid=(rows // t,), in_specs=[blk, blk], out_specs=blk,
        out_shape=jax.ShapeDtypeStruct((rows, cols), a.dtype), compiler_params=_params(("parallel",)),
    )(a2, b2)
    return out.reshape(shape)


def _adamw_math(w, m, v, g):
    c1 = 1.0 - ADAM_B1 ** ADAM_STEP
    c2 = 1.0 - ADAM_B2 ** ADAM_STEP
    mn = ADAM_B1 * m + (1.0 - ADAM_B1) * g
    vn = ADAM_B2 * v + (1.0 - ADAM_B2) * (g * g)
    delta = -ADAM_LR * ((mn / c1) / (jnp.sqrt(vn / c2) + ADAM_EPS) + ADAM_WD * w)
    return delta, mn, vn


def _slab_sum(g_ref, n):
    g = g_ref[0].astype(F32)
    for j in range(1, n):
        g = g + g_ref[j].astype(F32)
    return g


def adamw(name, w, m, v, g_slabs, deps=()):
    shape = w.shape
    n = g_slabs.shape[0]
    w2, m2, v2 = (t.reshape(-1, shape[-1]) for t in (w, m, v))
    g3 = g_slabs.reshape(n, -1, shape[-1])
    rows, cols = w2.shape
    t = _row_tile(rows, 256)

    def body(w_ref, m_ref, v_ref, g_ref, go_ref, d_ref, mo_ref, vo_ref):
        g = _slab_sum(g_ref, n)
        go_ref[...] = g
        d_ref[...], mo_ref[...], vo_ref[...] = _adamw_math(w_ref[...], m_ref[...], v_ref[...], g)

    blk = pl.BlockSpec((t, cols), lambda i: (i, 0))
    outs = _call_after(
        deps, 4, body, name=name, grid=(rows // t,),
        in_specs=[blk, blk, blk, pl.BlockSpec((n, t, cols), lambda i: (0, i, 0))], out_specs=[blk] * 4,
        out_shape=[jax.ShapeDtypeStruct((rows, cols), F32)] * 4, compiler_params=_params(("parallel",)),
    )(w2, m2, v2, g3)
    return tuple(o.reshape(shape) for o in outs)


def adamw_layer(name, l, w, m, v, own_sums, chip_sums, into):
    _, rows, cols = w.shape
    t = _row_tile(rows, 256)
    if into is None:
        into = tuple(lax.empty(w.shape, F32) for _ in range(4))

    def body(w_ref, m_ref, v_ref, own_ref, far_ref, a0, a1, a2, a3, go_ref, d_ref, mo_ref, vo_ref):
        del a0, a1, a2, a3
        q_me, _ = _place()
        g = jnp.zeros((t, cols), F32)
        for q in range(4):
            g = g + jnp.where(q == q_me, own_ref[q], far_ref[q]).astype(F32)
        go_ref[...] = g
        d_ref[...], mo_ref[...], vo_ref[...] = _adamw_math(w_ref[...], m_ref[...], v_ref[...], g)

    blk = pl.BlockSpec((None, t, cols), lambda i: (l, i, 0))
    slabs = pl.BlockSpec((4, t, cols), lambda i: (0, i, 0))
    anyw = pl.BlockSpec(memory_space=pl.ANY)
    return _call(
        body, name=name, grid=(rows // t,),
        in_specs=[blk, blk, blk, slabs, slabs] + [anyw] * 4, out_specs=[blk] * 4,
        out_shape=[jax.ShapeDtypeStruct(w.shape, F32)] * 4, input_output_aliases={5: 0, 6: 1, 7: 2, 8: 3},
        compiler_params=_params(("parallel",)),
    )(w, m, v, own_sums, chip_sums, *into)


_DIMS = {"nn": (((1,), (0,)), ((), ())), "nt": (((1,), (1,)), ((), ())), "tn": (((0,), (0,)), ((), ()))}


def _dot(a, b, mode="nn"):
    return lax.dot_general(a, b, _DIMS[mode], preferred_element_type=F32)


def matmul(name, a, b, mode, out_dtype, m, n, k, tm, tn, tk, b_off=(0, 0), shard_rows=0, deps=()):
    gm, gn, gk = m // tm, n // tn, k // tk
    assert gm * tm == m and gn * tn == n and gk * tk == k
    r0, c0 = b_off
    a_parts = list(a) if isinstance(a, (list, tuple)) else [a]
    na = len(a_parts)
    tile = tm if mode == "tn" else tk
    spans, at = [], 0
    for p in a_parts:
        nblk = p.shape[1] // tile
        assert nblk * tile == p.shape[1]
        spans.append((at, nblk))
        at += nblk
    assert at == (gm if mode == "tn" else gk)

    def within(t, span):
        return (t >= span[0]) & (t < span[0] + span[1])

    def local(t, span):
        return jnp.clip(t - span[0], 0, span[1] - 1)

    a_specs = []
    for sp in spans:
        if mode == "tn":
            a_specs.append(pl.BlockSpec((tk, tm), lambda i, j, kk, sp=sp: (jnp.where(within(i, sp), kk, 0), local(i, sp))))
        else:
            a_specs.append(pl.BlockSpec((tm, tk), lambda i, j, kk, sp=sp: (i, local(kk, sp))))
    if mode == "nt":
        b_spec = pl.BlockSpec((tn, tk), lambda i, j, kk: (j + r0, kk + c0))
    else:
        b_spec = pl.BlockSpec((tk, tn), lambda i, j, kk: (kk + r0, j + c0))
    o_spec = pl.BlockSpec((tm, tn), lambda i, j, kk: (i, j))
    single = na == 1 and gk == 1 and not shard_rows
    pairs = tm // (2 * shard_rows) if shard_rows else 0
    assert not shard_rows or (pairs * 2 * shard_rows == tm and m == 8 * shard_rows)

    def body(*refs):
        a_refs, b_ref, o_ref = refs[:na], refs[na], refs[na + 1]
        if single:
            o_ref[...] = _dot(a_refs[0][...], b_ref[...], mode).astype(o_ref.dtype)
            return
        acc_ref = refs[-1]
        i, kk = pl.program_id(0), pl.program_id(2)

        @pl.when(kk == 0)
        def _():
            acc_ref[...] = jnp.zeros_like(acc_ref)

        for pa, sp in enumerate(spans):
            def add(pa=pa):
                acc_ref[...] += _dot(a_refs[pa][...], b_ref[...], mode)
            if na > 1:
                pl.when(within(i if mode == "tn" else kk, sp))(add)
            else:
                add()

        @pl.when(kk == gk - 1)
        def _():
            if not shard_rows:
                o_ref[...] = acc_ref[...].astype(o_ref.dtype)
                return
            send_ref, mine = refs[na + 2], lax.axis_index("c") == 0
            for pq in range(pairs):
                even = acc_ref[2 * shard_rows * pq:2 * shard_rows * pq + shard_rows, :]
                odd = acc_ref[2 * shard_rows * pq + shard_rows:2 * shard_rows * (pq + 1), :]
                o_ref[pq] = jnp.where(mine, even, odd).astype(o_ref.dtype)
                send_ref[pq] = jnp.where(mine, odd, even).astype(send_ref.dtype)

    if shard_rows:
        o_spec = pl.BlockSpec((pairs, shard_rows, tn), lambda i, j, kk: (i, 0, j))
    shard = jax.ShapeDtypeStruct((4, shard_rows, n), out_dtype)
    return _call_after(
        deps, na + 1, body, name=name, grid=(gm, gn, gk),
        in_specs=a_specs + [b_spec], out_specs=[o_spec, o_spec] if shard_rows else o_spec,
        out_shape=[shard, shard] if shard_rows else jax.ShapeDtypeStruct((m, n), out_dtype),
        scratch_shapes=[] if single else [pltpu.VMEM((tm, tn), F32)],
        compiler_params=_params(("parallel", "parallel", "arbitrary")),
    )(*a_parts, b)


def matmul_swiglu(name, a, w_gu_t, deps=()):
    s, d = a.shape
    tm, tn = 1024, 1408
    gn = D_FF // tn

    def body(a_ref, wg_ref, wu_ref, du_ref, dg_ref, f_ref):
        av = a_ref[...]
        g = _dot(av, wg_ref[...], "nt")
        u = _dot(av, wu_ref[...], "nt")
        sig = 1.0 / (1.0 + jnp.exp(-g))
        silu = g * sig
        du_ref[...] = silu.astype(du_ref.dtype)
        dg_ref[...] = (u * (sig + silu * (1.0 - sig))).astype(dg_ref.dtype)
        f_ref[...] = (silu * u).astype(f_ref.dtype)

    o_spec = pl.BlockSpec((tm, tn), lambda j, i: (i, j))
    return _call_after(
        deps, 3, body, name=name, grid=(gn, s // tm),
        in_specs=[pl.BlockSpec((tm, d), lambda j, i: (i, 0)), pl.BlockSpec((tn, d), lambda j, i: (j, 0)),
                  pl.BlockSpec((tn, d), lambda j, i: (j + gn, 0))],
        out_specs=[o_spec] * 3, out_shape=[jax.ShapeDtypeStruct((s, D_FF), BF16)] * 3,
        compiler_params=_params(("parallel", "parallel"), 56),
    )(a, w_gu_t, w_gu_t)


def matmul_swiglu_bwd(name, dffo, w_down, ff_du, ff_dg):
    s, d = dffo.shape
    tm, tn = 512, 1408

    def body(a_ref, w_ref, pu_ref, pg_ref, dg_ref, du_ref):
        dff = _dot(a_ref[...], w_ref[...], "nt")
        du_ref[...] = (dff * pu_ref[...].astype(F32)).astype(du_ref.dtype)
        dg_ref[...] = (dff * pg_ref[...].astype(F32)).astype(dg_ref.dtype)

    o_spec = pl.BlockSpec((tm, tn), lambda j, i: (i, j))
    return _call(
        body, name=name, grid=(D_FF // tn, s // tm),
        in_specs=[pl.BlockSpec((tm, d), lambda j, i: (i, 0)), pl.BlockSpec((tn, d), lambda j, i: (j, 0)), o_spec, o_spec],
        out_specs=[o_spec] * 2, out_shape=[jax.ShapeDtypeStruct((s, D_FF), BF16)] * 2,
        compiler_params=_params(("parallel", "parallel")),
    )(dffo, w_down, ff_du, ff_dg)


def _down(v, n):
    return pltpu.roll(v, n, 0)


def _up(v, n):
    return pltpu.roll(v, v.shape[0] - n, 0)


def _by_window(lane, v2, v4, v8, v16):
    return jnp.where(lane < POOL_GC, v2, jnp.where(lane < 2 * POOL_GC, v4, jnp.where(lane < 3 * POOL_GC, v8, v16)))


def _taps(cw_ref):
    return cw_ref[0:1, :], cw_ref[1:2, :], cw_ref[2:3, :]


def _conv_pool_forward(ext, t0, cw, bd):
    n_out = ext.shape[0] - HALO
    gb = ext[HALO:, 0:D_CONV]
    z = ext[:, D_CONV:2 * D_CONV] * ext[:, 2 * D_CONV:3 * D_CONV]
    z0, z1, z2 = z[HALO:], _down(z, 1)[HALO:], _down(z, 2)[HALO:]
    conv = cw[2] * z0 + cw[1] * z1 + cw[0] * z2
    x = ext[:, 3 * D_CONV:]
    w2 = x + _down(x, 1)
    w4 = w2 + _down(w2, 2)
    w8 = w4 + _down(w4, 4)
    w16 = w8 + _down(w8, 8)
    lane = lax.broadcasted_iota(jnp.int32, (1, D_POOL), 1)
    win = _by_window(lane, 2.0, 4.0, 8.0, 16.0)
    pos = (t0 + lax.broadcasted_iota(jnp.int32, (n_out, 1), 0) + 1).astype(F32)
    cnt = jnp.minimum(pos, win)
    d = _by_window(lane, w2, w4, w8, w16)[HALO:] / cnt - x[HALO:]
    ybp = _dot(d.astype(BF16), bd)
    return gb, z0, z1, z2, conv, d, cnt, ybp, lane


def conv_pool_fwd(name, proj_a, cw, bd, ps, gg, deps=()):
    s = proj_a.shape[0]
    t = T_CP
    hb = t // HALO

    def body(main_ref, prev_ref, cw_ref, bd_ref, ps_ref, gg_ref, y_ref):
        i = pl.program_id(0)
        prev = jnp.where(i > 0, prev_ref[...].astype(F32), 0.0)
        ext = jnp.concatenate([prev, main_ref[...].astype(F32)], axis=0)
        gb, _, _, _, conv, _, _, ybp, _ = _conv_pool_forward(ext, i * t, _taps(cw_ref), bd_ref[...])
        ya = gb * conv
        yb = ybp * ps_ref[...]
        ggv = gg_ref[...]
        y_ref[:, 0:D_CONV] = ((ya * _rms(ya)) * ggv[:, 0:D_CONV]).astype(y_ref.dtype)
        y_ref[:, D_CONV:] = ((yb * _rms(yb)) * ggv[:, D_CONV:]).astype(y_ref.dtype)

    full = lambda shape: pl.BlockSpec(shape, lambda i: (0,) * len(shape))
    return _call_after(
        deps, 6, body, name=name, grid=(s // t,),
        in_specs=[pl.BlockSpec((t, D_MODEL), lambda i: (i, 0)),
                  pl.BlockSpec((HALO, D_MODEL), lambda i: (jnp.maximum(i * hb - 1, 0), 0)),
                  full((3, D_CONV)), full((D_POOL, D_POOL)), full((1, D_POOL)), full((1, 2 * D_CONV))],
        out_specs=pl.BlockSpec((t, 2 * D_CONV), lambda i: (i, 0)),
        out_shape=jax.ShapeDtypeStruct((s, D_MODEL), BF16),
        compiler_params=_params(("parallel",)),
    )(proj_a, proj_a, cw, bd, ps, gg)


def conv_pool_bwd(name, proj_a, dy, cw, bd, ps, gg):
    s = proj_a.shape[0]
    t = T_CP
    hb = t // HALO
    nblk = s // t
    last_halo = s // HALO - 1

    def body(main_ref, prev_ref, next_ref, dy_ref, dyn_ref, cw_ref, bd_ref, ps_ref, gg_ref,
             dp_ref, dcw_ref, dbd_ref, dps_ref, dgg_ref):
        i = pl.program_id(0)
        prev = jnp.where(i > 0, prev_ref[...].astype(F32), 0.0)
        main = main_ref[...].astype(F32)
        ext = jnp.concatenate([prev, main, next_ref[...].astype(F32)], axis=0)
        cwv, bdv, psv, ggv = _taps(cw_ref), bd_ref[...], ps_ref[...], gg_ref[...]
        gb, z0, z1, z2, conv, d, cnt, ybp, lane = _conv_pool_forward(ext, i * t, cwv, bdv)
        dyn = jnp.where(i < nblk - 1, dyn_ref[...].astype(F32), 0.0)
        dyv = jnp.concatenate([dy_ref[...].astype(F32), dyn], axis=0)
        ya = gb * conv
        yb = ybp * psv
        dya, dgg_a = _norm_bwd(ya, _rms(ya), ggv[:, 0:D_CONV], dyv[:, 0:D_CONV])
        dyb, dgg_b = _norm_bwd(yb, _rms(yb), ggv[:, D_CONV:], dyv[:, D_CONV:])

        dconv = dya * gb
        dz = (cwv[2] * dconv + cwv[1] * _up(dconv, 1) + cwv[0] * _up(dconv, 2))[:t]
        dp_ref[:, 0:D_CONV] = (dya * conv)[:t].astype(dp_ref.dtype)
        dp_ref[:, D_CONV:2 * D_CONV] = (dz * main[:, 2 * D_CONV:3 * D_CONV]).astype(dp_ref.dtype)
        dp_ref[:, 2 * D_CONV:3 * D_CONV] = (dz * main[:, D_CONV:2 * D_CONV]).astype(dp_ref.dtype)

        dybs = dyb * psv
        dd = _dot(dybs.astype(BF16), bdv, "nt")
        e = dd / cnt
        a2 = e + _up(e, 1)
        a4 = a2 + _up(a2, 2)
        a8 = a4 + _up(a4, 4)
        a16 = a8 + _up(a8, 8)
        dp_ref[:, 3 * D_CONV:] = (_by_window(lane, a2, a4, a8, a16) - dd)[:t].astype(dp_ref.dtype)

        @pl.when(i == 0)
        def _():
            dcw_ref[...] = jnp.zeros_like(dcw_ref)
            dbd_ref[...] = jnp.zeros_like(dbd_ref)
            dps_ref[...] = jnp.zeros_like(dps_ref)
            dgg_ref[...] = jnp.zeros_like(dgg_ref)

        rsum = lambda v: jnp.sum(v[:t], axis=0, keepdims=True)
        dcw_ref[0:1, :] += rsum(dconv * z2)
        dcw_ref[1:2, :] += rsum(dconv * z1)
        dcw_ref[2:3, :] += rsum(dconv * z0)
        dbd_ref[...] += _dot(d[:t].astype(BF16), dybs[:t].astype(BF16), "tn")
        dps_ref[...] += rsum(dyb * ybp)
        dgg_ref[:, 0:D_CONV] += rsum(dgg_a)
        dgg_ref[:, D_CONV:] += rsum(dgg_b)

    full = lambda shape: pl.BlockSpec(shape, lambda i: (0,) * len(shape))
    next_halo = lambda i: (jnp.minimum((i + 1) * hb, last_halo), 0)
    return _call(
        body, name=name, grid=(nblk,),
        in_specs=[pl.BlockSpec((t, D_MODEL), lambda i: (i, 0)),
                  pl.BlockSpec((HALO, D_MODEL), lambda i: (jnp.maximum(i * hb - 1, 0), 0)),
                  pl.BlockSpec((HALO, D_MODEL), next_halo),
                  pl.BlockSpec((t, 2 * D_CONV), lambda i: (i, 0)),
                  pl.BlockSpec((HALO, 2 * D_CONV), next_halo),
                  full((3, D_CONV)), full((D_POOL, D_POOL)), full((1, D_POOL)), full((1, 2 * D_CONV))],
        out_specs=[pl.BlockSpec((t, D_MODEL), lambda i: (i, 0)),
                   full((3, D_CONV)), full((D_POOL, D_POOL)), full((1, D_POOL)), full((1, 2 * D_CONV))],
        out_shape=[jax.ShapeDtypeStruct((s, D_MODEL), BF16), jax.ShapeDtypeStruct((3, D_CONV), F32),
                   jax.ShapeDtypeStruct((D_POOL, D_POOL), F32), jax.ShapeDtypeStruct((1, D_POOL), F32),
                   jax.ShapeDtypeStruct((1, 2 * D_CONV), F32)],
        compiler_params=_params(("arbitrary",)),
    )(proj_a, proj_a, proj_a, dy, dy, cw, bd, ps, gg)


def bias_by_offset(rel_bias):
    n_far = 2 * QB - REL_CLIP + 1
    far = jnp.broadcast_to(rel_bias[:, 2 * REL_CLIP:], (N_HEADS, n_far))
    mid = rel_bias[:, 1:2 * REL_CLIP][:, ::-1]
    near = jnp.broadcast_to(rel_bias[:, 0:1], (N_HEADS, KB - n_far - (2 * REL_CLIP - 1)))
    wrap = jnp.broadcast_to(rel_bias[:, 2 * REL_CLIP:], (N_HEADS, 4 * QB - KB))
    return jnp.concatenate([far, mid, near, wrap], axis=1)


def _fill_bias(off_ref, b_ref):
    width = 4 * QB
    sub = lax.broadcasted_iota(jnp.int32, (8, 1), 0)
    col = lax.broadcasted_iota(jnp.int32, (1, KB), 1)
    for h in range(N_HEADS):
        base = jnp.broadcast_to(off_ref[h:h + 1, :], (8, width))
        for bit in range(3):
            base = jnp.where(((sub >> bit) & 1) == 1, pltpu.roll(base, 1 << bit, 1), base)
        for a in range(QB // 8):
            first = CHUNK * (8 * a // CHUNK)
            rows = (pltpu.roll(base, 8 * a, 1) if a else base)[:, :KB]
            band = (col >= first) & (col < first + (LEFT_CHUNKS + 1) * CHUNK)
            b_ref[h, 8 * a:8 * a + 8, :] = jnp.where(band, rows, NEG)


def rel_bias_grad(by_offset):
    n_far = 2 * QB - REL_CLIP + 1
    hi = jnp.sum(by_offset[:, :n_far], axis=1, keepdims=True) + jnp.sum(by_offset[:, KB:], axis=1, keepdims=True)
    mid = by_offset[:, n_far:n_far + 2 * REL_CLIP - 1][:, ::-1]
    lo = jnp.sum(by_offset[:, n_far + 2 * REL_CLIP - 1:KB], axis=1, keepdims=True)
    return jnp.concatenate([lo, mid, hi], axis=1)


def _head_masks():
    first = lax.broadcasted_iota(jnp.int32, (1, 2 * HEAD_DIM), 1) < HEAD_DIM
    return first, jnp.logical_not(first)


def _pick_lane(tile, h):
    lane = lax.broadcasted_iota(jnp.int32, (1, tile.shape[1]), 1)
    return jnp.sum(jnp.where(lane == h, tile, 0.0), axis=-1, keepdims=True)


def attention_fwd(name, qkv, by_offset, gg, y_ab):
    s = qkv.shape[0]
    nq = s // QB
    scale = HEAD_DIM ** -0.5

    def body(q_ref, k0, k1, k2, v0, v1, v2, off_ref, gg_ref, y_in, y_ref, o_ref, lse_ref, b_ref):
        del y_in
        i = pl.program_id(0)

        @pl.when(i == 0)
        def _():
            _fill_bias(off_ref, b_ref)
        def block(at_start):
            kb = jnp.concatenate([k0[...], k1[...], k2[...]], axis=0)
            vb = jnp.concatenate([v0[...], v1[...], v2[...]], axis=0)
            valid = lax.broadcasted_iota(jnp.int32, (1, KB), 1) >= (2 - i) * QB
            lane = lax.broadcasted_iota(jnp.int32, (1, 128), 1)
            masks = _head_masks()
            lse = jnp.zeros((QB, 128), F32)
            outs = []
            for hp in range(N_HEADS // 2):
                sl = slice(2 * HEAD_DIM * hp, 2 * HEAD_DIM * (hp + 1))
                q_p, k_p, v_p = q_ref[:, sl] * scale, kb[:, sl], vb[:, sl]
                o_pair = jnp.zeros((QB, 2 * HEAD_DIM), F32)
                for a in range(2):
                    h = 2 * hp + a
                    sc = _dot(jnp.where(masks[a], q_p, 0), k_p, "nt") + b_ref[h]
                    if at_start:
                        sc = jnp.where(valid, sc, NEG)
                    mx = jnp.max(sc, axis=-1, keepdims=True)
                    e = jnp.exp(sc - mx).astype(BF16)
                    ones_at = HEAD_DIM * (1 - a)
                    ev = _dot(e, jnp.where(masks[a], v_p, jnp.where(lane == ones_at, 1, 0).astype(v_p.dtype)))
                    l = jnp.sum(jnp.where(lane == ones_at, ev, 0.0), axis=-1, keepdims=True)
                    o_pair = o_pair + jnp.where(masks[a], ev, 0.0) * (1.0 / l)
                    lse = jnp.where(lane == h, mx + jnp.log(l), lse)
                outs.append(o_pair)
            o = jnp.concatenate(outs, axis=1)
            o_ref[...] = o.astype(o_ref.dtype)
            lse_ref[...] = lse
            y_ref[...] = ((o * _rms(o)) * gg_ref[...]).astype(y_ref.dtype)

        pl.when(i < 2)(lambda: block(True))
        pl.when(i >= 2)(lambda: block(False))

    blk = lambda col, back: pl.BlockSpec((QB, D_ATTN), lambda i: (jnp.maximum(i - back, 0), QKV_COL + col))
    return _call(
        body, name=name, grid=(nq,),
        in_specs=[blk(0, 0), blk(1, 2), blk(1, 1), blk(1, 0), blk(2, 2), blk(2, 1), blk(2, 0),
                  pl.BlockSpec((N_HEADS, 4 * QB), lambda i: (0, 0)), pl.BlockSpec((1, D_ATTN), lambda i: (0, 0)),
                  pl.BlockSpec(memory_space=pl.ANY)],
        out_specs=[pl.BlockSpec((QB, D_ATTN), lambda i: (i, 1)), pl.BlockSpec((QB, D_ATTN), lambda i: (i, 0)),
                   pl.BlockSpec((QB, 128), lambda i: (i, 0)), pl.BlockSpec((N_HEADS, QB, KB), lambda i: (0, 0, 0))],
        out_shape=[jax.ShapeDtypeStruct((s, D_MODEL), BF16), jax.ShapeDtypeStruct((s, D_ATTN), BF16),
                   jax.ShapeDtypeStruct((s, 128), F32), jax.ShapeDtypeStruct((N_HEADS, QB, KB), F32)],
        input_output_aliases={9: 0},
        compiler_params=_params(("arbitrary",), 56),
    )(qkv, qkv, qkv, qkv, qkv, qkv, qkv, by_offset, gg, y_ab)


def attention_bwd(name, qkv, o, lse, dy, bias, gg):
    s = qkv.shape[0]
    nq = s // QB
    scale = HEAD_DIM ** -0.5
    width = 4 * QB

    def body(q_ref, k0, k1, k2, v0, v1, v2, o_ref, lse_ref, dy_ref, b_ref, gg_ref,
             dq_ref, dk_ref, dv_ref, off_ref, dgg_ref, dk_acc, dv_acc, db_acc):
        i = pl.program_id(0)

        @pl.when(i == 0)
        def _():
            dk_acc[...] = jnp.zeros_like(dk_acc)
            dv_acc[...] = jnp.zeros_like(dv_acc)
            db_acc[...] = jnp.zeros_like(db_acc)
            dgg_ref[...] = jnp.zeros_like(dgg_ref)

        rows_of = lambda j: pl.ds(pl.multiple_of((j % 3) * QB, QB), QB)

        @pl.when(i > 0)
        def _():
            dk_acc[rows_of(i), :] = jnp.zeros((QB, D_ATTN), F32)
            dv_acc[rows_of(i), :] = jnp.zeros((QB, D_ATTN), F32)

        def block(at_start):
            ov = o_ref[...].astype(F32)
            dyv = dy_ref[...].astype(F32)
            do, dgg_t = _norm_bwd(ov, _rms(ov), gg_ref[...], dyv)
            dgg_ref[...] += jnp.sum(dgg_t, axis=0, keepdims=True)
            kb = jnp.concatenate([k0[...], k1[...], k2[...]], axis=0)
            vb = jnp.concatenate([v0[...], v1[...], v2[...]], axis=0)
            valid = lax.broadcasted_iota(jnp.int32, (1, KB), 1) >= (2 - i) * QB
            masks = _head_masks()
            lse_t = lse_ref[...]
            for hp in range(N_HEADS // 2):
                sl = slice(2 * HEAD_DIM * hp, 2 * HEAD_DIM * (hp + 1))
                q_p, k_p, v_p = q_ref[:, sl] * scale, kb[:, sl], vb[:, sl]
                do_p = do[:, sl]
                prod = do_p * ov[:, sl]
                do_b = do_p.astype(BF16)
                dq_pair = jnp.zeros((QB, 2 * HEAD_DIM), F32)
                dk_pair = jnp.zeros((KB, 2 * HEAD_DIM), F32)
                dv_pair = jnp.zeros((KB, 2 * HEAD_DIM), F32)
                for a in range(2):
                    h = 2 * hp + a
                    q_m = jnp.where(masks[a], q_p, 0)
                    do_m = jnp.where(masks[a], do_b, 0)
                    sc = _dot(q_m, k_p, "nt") + b_ref[h]
                    if at_start:
                        sc = jnp.where(valid, sc, NEG)
                    p = jnp.exp(sc - _pick_lane(lse_t, h))
                    dp = _dot(do_m, v_p, "nt")
                    delta = jnp.sum(jnp.where(masks[a], prod, 0.0), axis=-1, keepdims=True)
                    ds = p * (dp - delta)
                    db_acc[h] += ds
                    ds_b = ds.astype(BF16)
                    dq_pair = dq_pair + _dot(ds_b, jnp.where(masks[a], k_p, 0))
                    dk_pair = dk_pair + _dot(ds_b, q_m, "tn")
                    dv_pair = dv_pair + _dot(p.astype(BF16), do_m, "tn")
                dq_ref[:, sl] = (dq_pair * scale).astype(dq_ref.dtype)
                for t in range(3):
                    dk_acc[rows_of(i + 1 + t), sl] += dk_pair[QB * t:QB * (t + 1)]
                    dv_acc[rows_of(i + 1 + t), sl] += dv_pair[QB * t:QB * (t + 1)]

        pl.when(i < 2)(lambda: block(True))
        pl.when((i >= 2) & (i < nq))(lambda: block(False))

        dk_ref[...] = dk_acc[rows_of(i + 1), :].astype(dk_ref.dtype)
        dv_ref[...] = dv_acc[rows_of(i + 1), :].astype(dv_ref.dtype)

        @pl.when(i == nq + 1)
        def _():
            sub = lax.broadcasted_iota(jnp.int32, (8, 1), 0)
            pad = jnp.zeros((8, width - KB), F32)
            for h in range(N_HEADS):
                v = jnp.concatenate([db_acc[h, 0:8, :], pad], axis=1)
                for a in range(1, QB // 8):
                    grp = jnp.concatenate([db_acc[h, 8 * a:8 * a + 8, :], pad], axis=1)
                    v = v + pltpu.roll(grp, width - 8 * a, 1)
                for bit in range(3):
                    v = jnp.where(((sub >> bit) & 1) == 1, pltpu.roll(v, width - (1 << bit), 1), v)
                off_ref[h:h + 1, :] = jnp.sum(v, axis=0, keepdims=True)

    qi = lambda i: jnp.minimum(i, nq - 1)
    kblk = lambda col, back: pl.BlockSpec((QB, D_ATTN), lambda i: (jnp.clip(i - back, 0, nq - 1), QKV_COL + col))
    qblk = lambda col: pl.BlockSpec((QB, D_ATTN), lambda i: (qi(i), col))
    done = pl.BlockSpec((QB, D_ATTN), lambda i: (jnp.clip(i - 2, 0, nq - 1), 0))
    return _call(
        body, name=name, grid=(nq + 2,),
        in_specs=[qblk(QKV_COL), kblk(1, 2), kblk(1, 1), kblk(1, 0), kblk(2, 2), kblk(2, 1), kblk(2, 0),
                  qblk(0), pl.BlockSpec((QB, 128), lambda i: (qi(i), 0)), qblk(1),
                  pl.BlockSpec((N_HEADS, QB, KB), lambda i: (0, 0, 0)), pl.BlockSpec((1, D_ATTN), lambda i: (0, 0))],
        out_specs=[qblk(0), done, done, pl.BlockSpec((N_HEADS, width), lambda i: (0, 0)),
                   pl.BlockSpec((1, D_ATTN), lambda i: (0, 0))],
        out_shape=[jax.ShapeDtypeStruct((s, D_ATTN), BF16)] * 3
        + [jax.ShapeDtypeStruct((N_HEADS, width), F32), jax.ShapeDtypeStruct((1, D_ATTN), F32)],
        scratch_shapes=[pltpu.VMEM((KB, D_ATTN), F32), pltpu.VMEM((KB, D_ATTN), F32), pltpu.VMEM((N_HEADS, QB, KB), F32)],
        compiler_params=_params(("arbitrary",), 56),
    )(qkv, qkv, qkv, qkv, qkv, qkv, qkv, o, lse, dy, bias, gg)


def _block_diag(pw):
    out = jnp.zeros((D_POOL, D_POOL), pw.dtype)
    for gi in range(N_POOL):
        out = lax.dynamic_update_slice(out, pw[gi], (gi * POOL_GC, gi * POOL_GC))
    return out


_SMALL = ("pool_w", "pool_scale", "rel_bias", "group_gain", "pre_mix_g", "post_mix_g", "pre_ffn_g", "post_ffn_g")


def _pack(parts, rows):
    flat = jnp.concatenate([p.reshape(-1).astype(F32) for p in parts])
    return jnp.pad(flat, (0, rows * D_MODEL - flat.shape[0])).reshape(rows, D_MODEL)


def _unpack(packed, shapes):
    flat = packed.reshape(-1)
    out, at = [], 0
    for shp in shapes:
        size = int(np.prod(shp))
        out.append(flat[at:at + size].reshape(shp))
        at += size
    return out


def kernel(x, w_in, w_out, conv_w, pool_w, pool_scale, rel_bias, group_gain, pre_mix_g, post_mix_g, pre_ffn_g, post_ffn_g, w_gate_up, w_down, loss_target, m_w_in, m_w_out, m_conv_w, m_pool_w, m_pool_scale, m_rel_bias, m_group_gain, m_pre_mix_g, m_post_mix_g, m_pre_ffn_g, m_post_ffn_g, m_w_gate_up, m_w_down, v_w_in, v_w_out, v_conv_w, v_pool_w, v_pool_scale, v_rel_bias, v_group_gain, v_pre_mix_g, v_post_mix_g, v_pre_ffn_g, v_post_ffn_g, v_w_gate_up, v_w_down):
    depth = w_in.shape[0]
    s = x.shape[1]
    c_me = lax.axis_index("c")
    q_me = 2 * lax.axis_index("x") + lax.axis_index("y")
    dev = 2 * q_me + c_me

    tr = lambda a: jnp.swapaxes(a, 1, 2)
    w_in_t, w_gu_t = tr(w_in), tr(w_gate_up)

    def group_a(l):
        return [w_in_t[l].astype(BF16), w_out[l].astype(BF16), conv_w[l]]

    def group_b(l):
        return [w_gu_t[l].astype(BF16), w_down[l].astype(BF16)]

    def weights_a(parts):
        g_in, g_out, g_cw = parts
        return (assemble_rows("assemble_in", *g_in), assemble_rows("assemble_out", *g_out),
                assemble_cols("assemble_conv", *g_cw))

    def weights_b(parts):
        g_gu, g_dn = parts
        return assemble_rows("assemble_gu", *g_gu), assemble_rows("assemble_down", *g_dn)

    h = x.reshape(s, D_MODEL)
    start_a = gather_start("gather_a0_s", group_a(0))
    start_b = gather_start("gather_b0_s", group_b(0), [start_a["token"]])
    xn = rmsnorm_fwd("norm_mix", h, pre_mix_g[0].reshape(1, -1), BF16)
    relay_a = gather_relay("gather_a0_r", start_a, start_b["token"])
    wa = weights_a(gather_finish("gather_a0_f", relay_a, xn))
    wb = None
    ahead = [start_b["token"]]
    saved = []
    for l in range(depth):
        vec = lambda p: p[l].reshape(1, -1)
        wt_in, wf_out, cw_full = wa
        if l + 1 < depth:
            start_next = gather_start(f"gather_{l + 1}_s", group_a(l + 1) + group_b(l + 1), ahead)
            ahead = [start_next["token"]]
        proj = matmul("proj", xn, wt_in, "nt", BF16, s, 5 * D_ATTN, D_MODEL, 1024, 5 * D_ATTN, D_MODEL, deps=ahead)
        if l == 0:
            relay_b = gather_relay("gather_b0_r", start_b, proj)
            ahead = [relay_b["token"]]
        bd = _block_diag(pool_w[l]).astype(BF16)
        gg = vec(group_gain)
        y_ab = conv_pool_fwd("conv_pool_fwd", proj, cw_full, bd, vec(pool_scale), gg[:, :2 * D_CONV], deps=ahead)
        y, o, lse, bias = attention_fwd("attention_fwd", proj, bias_by_offset(rel_bias[l]), gg[:, 2 * D_CONV:], y_ab)
        if l == 0:
            wb = weights_b(gather_finish("gather_b0_f", relay_b, y))
        wt_gu, wf_dn = wb
        if l + 1 < depth:
            relay_next = gather_relay(f"gather_{l + 1}_r", start_next, y)
        mix, h_mid, hn = matmul_norm("mix_out", y, wf_out, h, vec(post_mix_g), then=(vec(pre_ffn_g), BF16),
                                     deps=[relay_next["token"]] if l + 1 < depth else [])
        ff_du, ff_dg, ff = matmul_swiglu("gate_up", hn, wt_gu)
        if l + 1 < depth:
            parts = gather_finish(f"gather_{l + 1}_f", relay_next, ff)
            wa, wb = weights_a(parts[:3]), weights_b(parts[3:])
            ffo, h_out, xn_next = matmul_norm("ffn_down", ff, wf_dn, h_mid, vec(post_ffn_g),
                                              then=(pre_mix_g[l + 1].reshape(1, -1), BF16))
        else:
            ffo, h_out = matmul_norm("ffn_down_last", ff, wf_dn, h_mid, vec(post_ffn_g))
            xn_next = None
        saved.append((h, xn, proj, None, bd, bias, y, o, lse, mix, h_mid, hn, ff_du, ff_dg, ff, ffo,
                      wt_in, wf_out, cw_full, wt_gu, wf_dn))
        h, xn = h_out, xn_next

    dh, loss_part, dffo, dg_last = loss_head("loss_head", h, loss_target.reshape(s, D_MODEL), saved[-1][15],
                                             post_ffn_g[depth - 1].reshape(1, -1))

    def reduce_begin(name, halves):
        return [k for k, _ in halves], exchange_start(name, plan_swap_cores, len(halves), [snd for _, snd in halves],
                                                      [(k.shape, k.dtype) for k, _ in halves])

    def reduce_relay(name, kept, swapped, after):
        _, got = exchange_wait(name + "_w", swapped, after)
        sums = [add_pairs("reduce_add", k, r) for k, r in zip(kept, got)]
        return exchange_start(name + "_s", plan_to_chips, 3 * len(sums), sums, [(a.shape, a.dtype) for a in sums])

    def reduce_finish(name, relayed, after):
        sums, got = exchange_wait(name, relayed, after)
        return list(zip(sums, got))

    small = {k: [None] * depth for k in _SMALL + ("conv_w",)}
    relayed = {}
    carried = []
    small["post_ffn_g"][depth - 1] = dg_last
    for l in reversed(range(depth)):
        vec = lambda p: p[l].reshape(1, -1)
        (h_in, xn, proj, _, bd, bias, y, o, lse, mix, h_mid, hn, ff_du, ff_dg, ff, ffo,
         wt_in, wf_out, cw_full, wt_gu, wf_dn) = saved[l]
        gg = vec(group_gain)
        tk_w = min(s, 2048)
        gw_dn = matmul("wgrad_down", ff, dffo, "tn", BF16, D_FF, D_MODEL, s, 1408, D_MODEL, tk_w,
                       shard_rows=D_FF // 8, deps=carried)
        dg, du = matmul_swiglu_bwd("dgrad_down", dffo, wf_dn, ff_du, ff_dg)
        gw_gu = matmul("wgrad_gu", [dg, du], hn, "tn", BF16, 2 * D_FF, D_MODEL, s, 1408, D_MODEL, tk_w,
                       shard_rows=2 * D_FF // 8)
        early = [gw_gu, gw_dn] if l == 0 else []
        if early:
            kept_b, swap_b = reduce_begin("reduce_b0_s", early)
            carried = [swap_b["token"]]
        dh_mid, dmix, small["pre_ffn_g"][l], small["post_mix_g"][l] = matmul_norm_bwd(
            "dgrad_gu", [dg, du], wt_gu, 256, h_mid, vec(pre_ffn_g), dh, then=(mix, vec(post_mix_g), BF16),
            deps=carried)
        if early:
            relayed[l, "b"] = reduce_relay("reduce_b0_r", kept_b, swap_b, dmix)
            carried = [relayed[l, "b"]["token"]]
        gw_out = matmul("wgrad_out", y, dmix, "tn", BF16, D_MODEL, D_MODEL, s, D_MODEL, D_MODEL, tk_w,
                        shard_rows=D_MODEL // 8, deps=carried)
        dy = matmul("dgrad_out", dmix, wf_out, "nt", BF16, s, D_MODEL, D_MODEL, 512, D_MODEL, D_MODEL)
        dpa, dcw, dbd, dps, dgg_ab = conv_pool_bwd("conv_pool_bwd", proj, dy, cw_full, bd, vec(pool_scale),
                                                   gg[:, :2 * D_CONV])
        dq, dk, dv, by_off, dgg_c = attention_bwd("attention_bwd", proj, o, lse, dy, bias, gg[:, 2 * D_CONV:])
        dparts = [dpa, dq, dk, dv]
        gw_in = matmul("wgrad_in", dparts, xn, "tn", BF16, 5 * D_ATTN, D_MODEL, s, 512, D_MODEL, min(s, 2048))
        late = [split_rows("split_in", gw_in), gw_out] + ([] if early else [gw_gu, gw_dn])
        kept_a, swap_a = reduce_begin(f"reduce_a{l}_s", late)
        if l > 0:
            dh, dffo, small["pre_mix_g"][l], small["post_ffn_g"][l - 1] = matmul_norm_bwd(
                "dgrad_in", dparts, wt_in, 512, h_in, vec(pre_mix_g), dh_mid,
                then=(saved[l - 1][15], post_ffn_g[l - 1].reshape(1, -1), BF16), chunk=32, deps=[swap_a["token"]])
        else:
            dh, small["pre_mix_g"][l] = matmul_norm_bwd("dgrad_in_first", dparts, wt_in, 512, h_in, vec(pre_mix_g),
                                                        dh_mid, chunk=32, deps=[swap_a["token"]])
        relayed[l, "a"] = reduce_relay(f"reduce_a{l}_r", kept_a, swap_a, dh)
        carried = [relayed[l, "a"]["token"]]
        small["conv_w"][l] = dcw
        small["pool_w"][l] = jnp.stack([dbd[gi * POOL_GC:(gi + 1) * POOL_GC, gi * POOL_GC:(gi + 1) * POOL_GC]
                                        for gi in range(N_POOL)])
        small["pool_scale"][l] = dps
        small["rel_bias"][l] = rel_bias_grad(by_off)
        small["group_gain"][l] = jnp.concatenate([dgg_ab, dgg_c], axis=1)
    grad_x = dh.reshape(x.shape)

    small_params = dict(pool_w=pool_w, pool_scale=pool_scale, rel_bias=rel_bias, group_gain=group_gain,
                        pre_mix_g=pre_mix_g, post_mix_g=post_mix_g, pre_ffn_g=pre_ffn_g, post_ffn_g=post_ffn_g)
    small_m = dict(pool_w=m_pool_w, pool_scale=m_pool_scale, rel_bias=m_rel_bias, group_gain=m_group_gain,
                   pre_mix_g=m_pre_mix_g, post_mix_g=m_post_mix_g, pre_ffn_g=m_pre_ffn_g, post_ffn_g=m_post_ffn_g)
    small_v = dict(pool_w=v_pool_w, pool_scale=v_pool_scale, rel_bias=v_rel_bias, group_gain=v_group_gain,
                   pre_mix_g=v_pre_mix_g, post_mix_g=v_post_mix_g, pre_ffn_g=v_pre_ffn_g, post_ffn_g=v_post_ffn_g)
    shapes = [small_params[k].shape for k in _SMALL] + [(depth, 3, D_CONV), (1,)]
    n_small = sum(int(np.prod(shp)) for shp in shapes)
    rows = -(-n_small // (8 * D_MODEL)) * 8
    extra = [jnp.zeros((depth, 3, D_CONV), F32), jnp.zeros((1,), F32)]
    grads_packed = _pack([jnp.stack(small[k]) for k in _SMALL] + [jnp.stack(small["conv_w"]), loss_part[0, 0:1]], rows)
    small_started = gather_start("gather_small_s", [grads_packed], carried)

    big = dict(w_in=(w_in_t, tr(m_w_in), tr(v_w_in)), w_out=(w_out, m_w_out, v_w_out),
               w_gate_up=(w_gu_t, tr(m_w_gate_up), tr(v_w_gate_up)), w_down=(w_down, m_w_down, v_w_down))
    results = dict(w_in=None, w_out=None, w_gate_up=None, w_down=None)
    after = small_started["token"]
    small_relay = None
    for l in reversed(range(depth)):
        if (l, "b") in relayed:
            slabs_gu, slabs_dn = reduce_finish(f"reduce_b{l}_f", relayed[l, "b"], after)
            slabs_in, slabs_out = reduce_finish(f"reduce_a{l}_f", relayed[l, "a"], slabs_dn[1])
        else:
            slabs_in, slabs_out, slabs_gu, slabs_dn = reduce_finish(f"reduce_a{l}_f", relayed[l, "a"], after)
        for k, slabs in (("w_gate_up", slabs_gu), ("w_down", slabs_dn), ("w_in", slabs_in), ("w_out", slabs_out)):
            results[k] = adamw_layer("adamw_" + k, l, *big[k], *slabs, results[k])
        after = results["w_out"][0]
        if l == min(1, depth - 1):
            small_relay = gather_relay("gather_small_r", small_started, after)
    small_parts = gather_finish("gather_small_f", small_relay, after)[0]
    all_small = assemble_rows("assemble_small", *small_parts).reshape(8, rows, D_MODEL)
    res_small = adamw("adamw_small", _pack([small_params[k] for k in _SMALL] + extra, rows),
                      _pack([small_m[k] for k in _SMALL] + extra, rows),
                      _pack([small_v[k] for k in _SMALL] + extra, rows), all_small)
    g_s, d_s, m_s, v_s = (_unpack(r, shapes) for r in res_small)
    loss = g_s[-1][0]
    g_conv = lax.dynamic_slice_in_dim(g_s[-2], dev * conv_w.shape[2], conv_w.shape[2], axis=2)
    results["conv_w"] = adamw("adamw_conv", conv_w, m_conv_w, v_conv_w, g_conv[None])

    names = ("w_in", "w_out", "conv_w") + _SMALL + ("w_gate_up", "w_down")
    for k in ("w_in", "w_gate_up"):
        results[k] = tuple(tr(r) for r in results[k])
    for j, k in enumerate(_SMALL):
        results[k] = (g_s[j], d_s[j], m_s[j], v_s[j])
    return (loss, grad_x, *[results[k][0] for k in names], *[results[k][1] for k in names],
            *[results[k][2] for k in names], *[results[k][3] for k in names])
```

```python
import numpy as np
import jax
import jax.numpy as jnp
from jax import lax
from jax.experimental import pallas as pl
from jax.experimental.pallas import tpu as pltpu

F32 = jnp.float32
BF16 = jnp.bfloat16

CHUNK = 64
D_MODEL = 1024
D_CONV = 256
D_POOL = 256
D_ATTN = 512
HEAD_DIM = 64
N_HEADS = 8
N_POOL = 4
POOL_GC = 64
POOL_WINDOWS = (2, 4, 8, 16)
LEFT_CHUNKS = 8
REL_CLIP = 128
D_FF = 2816
EPS = 1e-6
ADAM_LR, ADAM_B1, ADAM_B2, ADAM_EPS, ADAM_WD, ADAM_STEP = 0.001, 0.9, 0.999, 1e-08, 0.01, 10

QB = 256
KB = 3 * QB
HALO = 16
T_CP = 512
T_ROW = 512
QKV_COL = 2
NEG = -1e30
VMEM_MB = 1 << 20
MESH = pl.DeviceIdType.MESH


def _call(body, **kw):
    call = pl.pallas_call(body, **kw)
    return lambda *args: call(*[_in_hbm(a) for a in args])


def _in_hbm(a):
    return pltpu.with_memory_space_constraint(a, pltpu.HBM) if jnp.issubdtype(a.dtype, jnp.number) else a


def _call_after(deps, n_in, body, **kw):
    deps = tuple(deps)
    if not deps:
        return _call(body, **kw)

    def ordered(*refs):
        body(*refs[:n_in], *refs[n_in + len(deps):])

    kw["in_specs"] = list(kw["in_specs"]) + [pl.BlockSpec(memory_space=pl.ANY)] * len(deps)
    call = _call(ordered, **kw)
    return lambda *args: call(*args, *deps)


def _params(sem, vmem_mb=48):
    return pltpu.CompilerParams(dimension_semantics=sem, vmem_limit_bytes=vmem_mb * VMEM_MB)


_CHIP_FLIPS = ((1, 0), (0, 1), (1, 1))
_HBM = pl.BlockSpec(memory_space=pltpu.HBM)
_SEM = pl.BlockSpec(memory_space=pltpu.SEMAPHORE)
_EFFECT = pltpu.SideEffectType.DATAFLOW_SIDE_EFFECTING


def _flip(v, f):
    return 1 - v if f else v


def _descriptors(plan, srcs, lands, send_sems, recv_sems, sending):
    x, y, c = lax.axis_index("x"), lax.axis_index("y"), lax.axis_index("c")
    return [pltpu.make_async_remote_copy(src_ref=src, dst_ref=dst if sending else land, send_sem=send_sems.at[k],
                                         recv_sem=recv_sems.at[k], device_id=peer, device_id_type=MESH)
            for k, (src, dst, peer, land) in enumerate(plan(srcs, lands, x, y, c))]


def exchange_start(name, plan, n_copies, srcs, land_shapes, deps=()):
    ns, nl = len(srcs), len(land_shapes)

    def body(*refs):
        src_refs, land_refs = refs[:ns], refs[ns:ns + nl]
        send_sems, recv_sems = refs[ns + nl], refs[ns + nl + 1]
        for send in _descriptors(plan, src_refs, land_refs, send_sems, recv_sems, True):
            send.start()
        refs[-1][...] = jnp.zeros_like(refs[-1])

    lands = [lax.empty(shape, dtype) for shape, dtype in land_shapes]
    outs = _call_after(
        deps, ns + nl, body, name=name,
        out_shape=(pltpu.SemaphoreType.DMA((n_copies,)), pltpu.SemaphoreType.DMA((n_copies,)),
                   *[pltpu.HBM(a.shape, a.dtype) for a in srcs], *[pltpu.HBM(shape, dtype) for shape, dtype in land_shapes],
                   jax.ShapeDtypeStruct((8, 128), F32)),
        in_specs=[_HBM] * (ns + nl),
        out_specs=(_SEM, _SEM, *[_HBM] * (ns + nl), pl.BlockSpec(memory_space=pltpu.VMEM)),
        input_output_aliases={j: 2 + j for j in range(ns + nl)},
        compiler_params=pltpu.CompilerParams(has_side_effects=_EFFECT),
    )(*srcs, *lands)
    return dict(plan=plan, sems=outs[:2], srcs=outs[2:2 + ns], lands=outs[2 + ns:2 + ns + nl], token=outs[-1])


def exchange_wait(name, started, after):
    srcs, lands = started["srcs"], started["lands"]
    ns, nl = len(srcs), len(lands)

    def body(*refs):
        src_refs, land_refs = refs[:ns], refs[ns:ns + nl]
        send_sems, recv_sems = refs[ns + nl], refs[ns + nl + 1]
        for wait in _descriptors(started["plan"], src_refs, land_refs, send_sems, recv_sems, False):
            wait.wait_send()
            wait.wait_recv()

    outs = _call(
        body, name=name,
        out_shape=tuple(pltpu.HBM(a.shape, a.dtype) for a in (*srcs, *lands)),
        in_specs=[_HBM] * (ns + nl) + [_SEM, _SEM, pl.BlockSpec(memory_space=pl.ANY)],
        out_specs=tuple([_HBM] * (ns + nl)),
        input_output_aliases={j: j for j in range(ns + nl)},
        compiler_params=pltpu.CompilerParams(has_side_effects=_EFFECT),
    )(*srcs, *lands, *started["sems"], after)
    return list(outs[:ns]), list(outs[ns:])


def plan_from_chips(srcs, lands, x, y, c):
    q = 2 * x + y
    out = []
    for src, land in zip(srcs, lands):
        for fx, fy in _CHIP_FLIPS:
            px, py = _flip(x, fx), _flip(y, fy)
            out.append((src, land.at[q], (px, py, c), land.at[2 * px + py]))
    return out


def plan_to_core(srcs, lands, x, y, c):
    n = len(lands)
    q = 2 * x + y
    out = []
    for own, chips, land in zip(srcs[:n], srcs[n:], lands):
        out.append((own, land.at[q], (x, y, 1 - c), land.at[q]))
        for fx, fy in _CHIP_FLIPS:
            qp = 2 * _flip(x, fx) + _flip(y, fy)
            out.append((chips.at[qp], land.at[qp], (x, y, 1 - c), land.at[qp]))
    return out


def plan_swap_cores(srcs, lands, x, y, c):
    return [(src, land, (x, y, 1 - c), land) for src, land in zip(srcs, lands)]


def plan_to_chips(srcs, lands, x, y, c):
    q = 2 * x + y
    out = []
    for src, land in zip(srcs, lands):
        for fx, fy in _CHIP_FLIPS:
            px, py = _flip(x, fx), _flip(y, fy)
            out.append((src.at[2 * px + py], land.at[q], (px, py, c), land.at[2 * px + py]))
    return out


def _slots(a):
    return ((4,) + a.shape, a.dtype)


def gather_start(name, arrays, deps=()):
    return exchange_start(name, plan_from_chips, 3 * len(arrays), arrays, [_slots(a) for a in arrays], deps)


def gather_relay(name, started, after):
    own, chips = exchange_wait(name + "_w", started, after)
    return exchange_start(name + "_s", plan_to_core, 4 * len(own), own + chips, [_slots(a) for a in own])


def gather_finish(name, relayed, after):
    srcs, cores = exchange_wait(name, relayed, after)
    n = len(cores)
    return list(zip(srcs[:n], srcs[n:], cores))


def _place():
    return 2 * lax.axis_index("x") + lax.axis_index("y"), lax.axis_index("c")


def assemble_cols(name, own, chips, core):
    r, c = own.shape
    t = _row_tile(r, 256)

    def body(own_ref, chips_ref, core_ref, o_ref):
        q_me, c_me = _place()
        for q in range(4):
            mine = jnp.where(q == q_me, own_ref[...], chips_ref[q])
            other = core_ref[q]
            o_ref[:, 2 * q * c:(2 * q + 1) * c] = jnp.where(c_me == 0, mine, other)
            o_ref[:, (2 * q + 1) * c:(2 * q + 2) * c] = jnp.where(c_me == 0, other, mine)

    slots = pl.BlockSpec((4, t, c), lambda i: (0, i, 0))
    return _call(
        body, name=name, grid=(r // t,),
        in_specs=[pl.BlockSpec((t, c), lambda i: (i, 0)), slots, slots],
        out_specs=pl.BlockSpec((t, 8 * c), lambda i: (i, 0)),
        out_shape=jax.ShapeDtypeStruct((r, 8 * c), own.dtype), compiler_params=_params(("parallel",)),
    )(own, chips, core)


def assemble_rows(name, own, chips, core):
    r, c = own.shape

    def body(own_ref, chips_ref, core_ref, o_ref):
        q_me, c_me = _place()
        d = pl.program_id(0)
        mine = jnp.where(d // 2 == q_me, own_ref[...], chips_ref[...])
        o_ref[...] = jnp.where(d % 2 == c_me, mine, core_ref[...])

    slot = pl.BlockSpec((None, r, c), lambda d: (d // 2, 0, 0))
    return _call(
        body, name=name, grid=(8,),
        in_specs=[pl.BlockSpec((r, c), lambda d: (0, 0)), slot, slot],
        out_specs=pl.BlockSpec((r, c), lambda d: (d, 0)),
        out_shape=jax.ShapeDtypeStruct((8 * r, c), own.dtype), compiler_params=_params(("parallel",)),
    )(own, chips, core)


def split_rows(name, dw):
    r8, c = dw.shape
    r = r8 // 8

    def body(dw_ref, keep_ref, send_ref):
        _, c_me = _place()
        d = pl.program_id(0)

        @pl.when(d % 2 == c_me)
        def _():
            keep_ref[...] = dw_ref[...]

        @pl.when(d % 2 != c_me)
        def _():
            send_ref[...] = dw_ref[...]

    slot = pl.BlockSpec((None, r, c), lambda d: (d // 2, 0, 0))
    return _call(
        body, name=name, grid=(8,),
        in_specs=[pl.BlockSpec((r, c), lambda d: (d, 0))], out_specs=[slot, slot],
        out_shape=[jax.ShapeDtypeStruct((4, r, c), dw.dtype)] * 2, compiler_params=_params(("arbitrary",)),
    )(dw)


def _rms(v):
    return lax.rsqrt(jnp.mean(v * v, axis=-1, keepdims=True) + EPS)


def rmsnorm_fwd(name, x, g, out_dtype):
    s, d = x.shape

    def body(x_ref, g_ref, o_ref):
        xv = x_ref[...]
        o_ref[...] = ((xv * _rms(xv)) * g_ref[...]).astype(o_ref.dtype)

    row = pl.BlockSpec((T_ROW, d), lambda i: (i, 0))
    return _call(
        body, name=name, grid=(s // T_ROW,),
        in_specs=[row, pl.BlockSpec((1, d), lambda i: (0, 0))], out_specs=row,
        out_shape=jax.ShapeDtypeStruct((s, d), out_dtype), compiler_params=_params(("parallel",)),
    )(x, g)


def matmul_then(name, a_parts, b, mode, tm, chunk, rows_in, vecs_in, rows_out, n_sums, then, deps=()):
    a_parts = list(a_parts)
    na, nr, nv, no = len(a_parts), len(rows_in), len(vecs_in), len(rows_out)
    s = a_parts[0].shape[0]
    n = s // tm
    d = b.shape[1] if mode == "nn" else b.shape[0]
    offs = [0]
    for p in a_parts:
        offs.append(offs[-1] + p.shape[1])
    n_in = na + 1 + nr + nv

    def body(*refs):
        a_refs, b_ref = refs[:na], refs[na]
        row_refs, vec_refs = refs[na + 1:na + 1 + nr], refs[na + 1 + nr:n_in]
        out_refs, sum_refs = refs[n_in:n_in + no], refs[n_in + no:n_in + no + n_sums]
        acc = refs[-2:]
        i = pl.program_id(0)

        @pl.when(i == 0)
        def _():
            acc[1][...] = jnp.zeros_like(acc[1])
            for s_ref in sum_refs:
                s_ref[...] = jnp.zeros_like(s_ref)

        def step(write, read):
            t = None
            for p in range(na):
                lo, hi = offs[p], offs[p + 1]
                part = _dot(a_refs[p][...], b_ref[lo:hi, :] if mode == "nn" else b_ref[:, lo:hi], mode)
                t = part if t is None else t + part
            write[...] = t
            totals = [0.0] * n_sums
            for r0 in range(0, tm, chunk):
                rs = pl.ds(r0, chunk)
                results, sums = then(read[rs, :], [r.at[rs, :] for r in row_refs], vec_refs)
                for o_ref, val in zip(out_refs, results):
                    o_ref[rs, :] = val.astype(o_ref.dtype)
                totals = [tot + val for tot, val in zip(totals, sums)]
            for s_ref, val in zip(sum_refs, totals):
                s_ref[...] += jnp.where(i > 0, val, 0.0)

        pl.when(i % 2 == 0)(lambda: step(acc[0], acc[1]))
        pl.when(i % 2 == 1)(lambda: step(acc[1], acc[0]))

    ahead = lambda i: (jnp.minimum(i, n - 1), 0)
    behind = lambda i: (jnp.maximum(i - 1, 0), 0)
    fixed = lambda i: (0, 0)
    shapes = [jax.ShapeDtypeStruct((s, d), dt) for dt in rows_out] + [jax.ShapeDtypeStruct((1, d), F32)] * n_sums
    return _call_after(
        deps, n_in, body, name=name, grid=(n + 1,),
        in_specs=[pl.BlockSpec((tm, p.shape[1]), ahead) for p in a_parts] + [pl.BlockSpec(b.shape, fixed)]
        + [pl.BlockSpec((tm, r.shape[1]), behind) for r in rows_in] + [pl.BlockSpec(v.shape, fixed) for v in vecs_in],
        out_specs=[pl.BlockSpec((tm, d), behind)] * no + [pl.BlockSpec((1, d), fixed)] * n_sums,
        out_shape=shapes, scratch_shapes=[pltpu.VMEM((tm, d), F32)] * 2,
        compiler_params=_params(("arbitrary",), 56),
    )(*a_parts, b, *rows_in, *vecs_in)


def matmul_norm(name, a, b, res, g, then=None, tm=512, deps=()):
    def norm(z, rows, vecs):
        y = rows[0][...] + (z * _rms(z)) * vecs[0][...]
        return [z, y] + ([(y * _rms(y)) * vecs[1][...]] if then else []), []

    return matmul_then(name, [a], b, "nn", tm, tm, [res], [g] + ([then[0]] if then else []),
                       [BF16, F32] + ([then[1]] if then else []), 0, norm, deps)


def matmul_norm_bwd(name, a_parts, b, tm, x, g, res, then=None, chunk=None, deps=()):
    def norms(t, rows, vecs):
        xv = rows[0][...].astype(F32)
        dx, dgt = _norm_bwd(xv, _rms(xv), vecs[0][...], t)
        dx = dx + rows[1][...]
        results, sums = [dx], [jnp.sum(dgt, axis=0, keepdims=True)]
        if then:
            x2 = rows[2][...].astype(F32)
            dx2, dgt2 = _norm_bwd(x2, _rms(x2), vecs[1][...], dx)
            results.append(dx2)
            sums.append(jnp.sum(dgt2, axis=0, keepdims=True))
        return results, sums

    return matmul_then(name, a_parts, b, "nn", tm, chunk or tm, [x, res] + ([then[0]] if then else []),
                       [g] + ([then[1]] if then else []), [F32] + ([then[2]] if then else []),
                       2 if then else 1, norms, deps)


def _norm_bwd(xv, r, g, dy):
    a = dy * g
    dx = r * (a - xv * ((r * r) * jnp.mean(a * xv, axis=-1, keepdims=True)))
    return dx, dy * (xv * r)


def loss_head(name, h, tgt, z, g):
    s, d = h.shape

    def body(h_ref, t_ref, z_ref, g_ref, dh_ref, l_ref, dz_ref, dg_ref):
        e = h_ref[...] - t_ref[...]
        dh = e * (1.0 / d)
        dh_ref[...] = dh
        zv = z_ref[...].astype(F32)
        dz, dgt = _norm_bwd(zv, _rms(zv), g_ref[...], dh)
        dz_ref[...] = dz.astype(dz_ref.dtype)

        @pl.when(pl.program_id(0) == 0)
        def _():
            l_ref[...] = jnp.zeros_like(l_ref)
            dg_ref[...] = jnp.zeros_like(dg_ref)
        part = 0.5 * jnp.sum(jnp.mean(e * e, axis=-1, keepdims=True), axis=0, keepdims=True)
        l_ref[...] += jnp.broadcast_to(part, l_ref.shape)
        dg_ref[...] += jnp.sum(dgt, axis=0, keepdims=True)

    row = pl.BlockSpec((T_ROW, d), lambda i: (i, 0))
    vec = pl.BlockSpec((1, d), lambda i: (0, 0))
    return _call(
        body, name=name, grid=(s // T_ROW,),
        in_specs=[row, row, row, vec], out_specs=[row, pl.BlockSpec((1, 128), lambda i: (0, 0)), row, vec],
        out_shape=[jax.ShapeDtypeStruct((s, d), F32), jax.ShapeDtypeStruct((1, 128), F32),
                   jax.ShapeDtypeStruct((s, d), BF16), jax.ShapeDtypeStruct((1, d), F32)],
        compiler_params=_params(("arbitrary",)),
    )(h, tgt, z, g)


def _row_tile(rows, limit=512):
    t = min(rows, limit)
    while rows % t or (t % 8 and t != rows):
        t -= 1
    return t


def add_pairs(name, a, b):
    shape = a.shape
    a2, b2 = a.reshape(-1, shape[-1]), b.reshape(-1, shape[-1])
    rows, cols = a2.shape
    t = _row_tile(rows)

    def body(a_ref, b_ref, o_ref):
        o_ref[...] = (a_ref[...].astype(F32) + b_ref[...].astype(F32)).astype(o_ref.dtype)

    blk = pl.BlockSpec((t, cols), lambda i: (i, 0))
    out = _call(
        body, name=name, grid=(rows // t,), in_specs=[blk, blk], out_specs=blk,
        out_shape=jax.ShapeDtypeStruct((rows, cols), a.dtype), compiler_params=_params(("parallel",)),
    )(a2, b2)
    return out.reshape(shape)


def _adamw_math(w, m, v, g):
    c1 = 1.0 - ADAM_B1 ** ADAM_STEP
    c2 = 1.0 - ADAM_B2 ** ADAM_STEP
    mn = ADAM_B1 * m + (1.0 - ADAM_B1) * g
    vn = ADAM_B2 * v + (1.0 - ADAM_B2) * (g * g)
    delta = -ADAM_LR * ((mn / c1) / (jnp.sqrt(vn / c2) + ADAM_EPS) + ADAM_WD * w)
    return delta, mn, vn


def _slab_sum(g_ref, n):
    g = g_ref[0].astype(F32)
    for j in range(1, n):
        g = g + g_ref[j].astype(F32)
    return g


def adamw(name, w, m, v, g_slabs, deps=()):
    shape = w.shape
    n = g_slabs.shape[0]
    w2, m2, v2 = (t.reshape(-1, shape[-1]) for t in (w, m, v))
    g3 = g_slabs.reshape(n, -1, shape[-1])
    rows, cols = w2.shape
    t = _row_tile(rows, 256)

    def body(w_ref, m_ref, v_ref, g_ref, go_ref, d_ref, mo_ref, vo_ref):
        g = _slab_sum(g_ref, n)
        go_ref[...] = g
        d_ref[...], mo_ref[...], vo_ref[...] = _adamw_math(w_ref[...], m_ref[...], v_ref[...], g)

    blk = pl.BlockSpec((t, cols), lambda i: (i, 0))
    outs = _call_after(
        deps, 4, body, name=name, grid=(rows // t,),
        in_specs=[blk, blk, blk, pl.BlockSpec((n, t, cols), lambda i: (0, i, 0))], out_specs=[blk] * 4,
        out_shape=[jax.ShapeDtypeStruct((rows, cols), F32)] * 4, compiler_params=_params(("parallel",)),
    )(w2, m2, v2, g3)
    return tuple(o.reshape(shape) for o in outs)


def adamw_layer(name, l, w, m, v, own_sums, chip_sums, into):
    _, rows, cols = w.shape
    t = _row_tile(rows, 256)
    if into is None:
        into = tuple(lax.empty(w.shape, F32) for _ in range(4))

    def body(w_ref, m_ref, v_ref, own_ref, far_ref, a0, a1, a2, a3, go_ref, d_ref, mo_ref, vo_ref):
        del a0, a1, a2, a3
        q_me, _ = _place()
        g = jnp.zeros((t, cols), F32)
        for q in range(4):
            g = g + jnp.where(q == q_me, own_ref[q], far_ref[q]).astype(F32)
        go_ref[...] = g
        d_ref[...], mo_ref[...], vo_ref[...] = _adamw_math(w_ref[...], m_ref[...], v_ref[...], g)

    blk = pl.BlockSpec((None, t, cols), lambda i: (l, i, 0))
    slabs = pl.BlockSpec((4, t, cols), lambda i: (0, i, 0))
    anyw = pl.BlockSpec(memory_space=pl.ANY)
    return _call(
        body, name=name, grid=(rows // t,),
        in_specs=[blk, blk, blk, slabs, slabs] + [anyw] * 4, out_specs=[blk] * 4,
        out_shape=[jax.ShapeDtypeStruct(w.shape, F32)] * 4, input_output_aliases={5: 0, 6: 1, 7: 2, 8: 3},
        compiler_params=_params(("parallel",)),
    )(w, m, v, own_sums, chip_sums, *into)


_DIMS = {"nn": (((1,), (0,)), ((), ())), "nt": (((1,), (1,)), ((), ())), "tn": (((0,), (0,)), ((), ()))}


def _dot(a, b, mode="nn"):
    return lax.dot_general(a, b, _DIMS[mode], preferred_element_type=F32)


def matmul(name, a, b, mode, out_dtype, m, n, k, tm, tn, tk, b_off=(0, 0), shard_rows=0, deps=()):
    gm, gn, gk = m // tm, n // tn, k // tk
    assert gm * tm == m and gn * tn == n and gk * tk == k
    r0, c0 = b_off
    a_parts = list(a) if isinstance(a, (list, tuple)) else [a]
    na = len(a_parts)
    tile = tm if mode == "tn" else tk
    spans, at = [], 0
    for p in a_parts:
        nblk = p.shape[1] // tile
        assert nblk * tile == p.shape[1]
        spans.append((at, nblk))
        at += nblk
    assert at == (gm if mode == "tn" else gk)

    def within(t, span):
        return (t >= span[0]) & (t < span[0] + span[1])

    def local(t, span):
        return jnp.clip(t - span[0], 0, span[1] - 1)

    a_specs = []
    for sp in spans:
        if mode == "tn":
            a_specs.append(pl.BlockSpec((tk, tm), lambda i, j, kk, sp=sp: (jnp.where(within(i, sp), kk, 0), local(i, sp))))
        else:
            a_specs.append(pl.BlockSpec((tm, tk), lambda i, j, kk, sp=sp: (i, local(kk, sp))))
    if mode == "nt":
        b_spec = pl.BlockSpec((tn, tk), lambda i, j, kk: (j + r0, kk + c0))
    else:
        b_spec = pl.BlockSpec((tk, tn), lambda i, j, kk: (kk + r0, j + c0))
    o_spec = pl.BlockSpec((tm, tn), lambda i, j, kk: (i, j))
    single = na == 1 and gk == 1 and not shard_rows
    pairs = tm // (2 * shard_rows) if shard_rows else 0
    assert not shard_rows or (pairs * 2 * shard_rows == tm and m == 8 * shard_rows)

    def body(*refs):
        a_refs, b_ref, o_ref = refs[:na], refs[na], refs[na + 1]
        if single:
            o_ref[...] = _dot(a_refs[0][...], b_ref[...], mode).astype(o_ref.dtype)
            return
        acc_ref = refs[-1]
        i, kk = pl.program_id(0), pl.program_id(2)

        @pl.when(kk == 0)
        def _():
            acc_ref[...] = jnp.zeros_like(acc_ref)

        for pa, sp in enumerate(spans):
            def add(pa=pa):
                acc_ref[...] += _dot(a_refs[pa][...], b_ref[...], mode)
            if na > 1:
                pl.when(within(i if mode == "tn" else kk, sp))(add)
            else:
                add()

        @pl.when(kk == gk - 1)
        def _():
            if not shard_rows:
                o_ref[...] = acc_ref[...].astype(o_ref.dtype)
                return
            send_ref, mine = refs[na + 2], lax.axis_index("c") == 0
            for pq in range(pairs):
                even = acc_ref[2 * shard_rows * pq:2 * shard_rows * pq + shard_rows, :]
                odd = acc_ref[2 * shard_rows * pq + shard_rows:2 * shard_rows * (pq + 1), :]
                o_ref[pq] = jnp.where(mine, even, odd).astype(o_ref.dtype)
                send_ref[pq] = jnp.where(mine, odd, even).astype(send_ref.dtype)

    if shard_rows:
        o_spec = pl.BlockSpec((pairs, shard_rows, tn), lambda i, j, kk: (i, 0, j))
    shard = jax.ShapeDtypeStruct((4, shard_rows, n), out_dtype)
    return _call_after(
        deps, na + 1, body, name=name, grid=(gm, gn, gk),
        in_specs=a_specs + [b_spec], out_specs=[o_spec, o_spec] if shard_rows else o_spec,
        out_shape=[shard, shard] if shard_rows else jax.ShapeDtypeStruct((m, n), out_dtype),
        scratch_shapes=[] if single else [pltpu.VMEM((tm, tn), F32)],
        compiler_params=_params(("parallel", "parallel", "arbitrary")),
    )(*a_parts, b)


def matmul_swiglu(name, a, w_gu_t, deps=()):
    s, d = a.shape
    tm, tn = 1024, 1408
    gn = D_FF // tn

    def body(a_ref, wg_ref, wu_ref, du_ref, dg_ref, f_ref):
        av = a_ref[...]
        g = _dot(av, wg_ref[...], "nt")
        u = _dot(av, wu_ref[...], "nt")
        sig = 1.0 / (1.0 + jnp.exp(-g))
        silu = g * sig
        du_ref[...] = silu.astype(du_ref.dtype)
        dg_ref[...] = (u * (sig + silu * (1.0 - sig))).astype(dg_ref.dtype)
        f_ref[...] = (silu * u).astype(f_ref.dtype)

    o_spec = pl.BlockSpec((tm, tn), lambda j, i: (i, j))
    return _call_after(
        deps, 3, body, name=name, grid=(gn, s // tm),
        in_specs=[pl.BlockSpec((tm, d), lambda j, i: (i, 0)), pl.BlockSpec((tn, d), lambda j, i: (j, 0)),
                  pl.BlockSpec((tn, d), lambda j, i: (j + gn, 0))],
        out_specs=[o_spec] * 3, out_shape=[jax.ShapeDtypeStruct((s, D_FF), BF16)] * 3,
        compiler_params=_params(("parallel", "parallel"), 56),
    )(a, w_gu_t, w_gu_t)


def matmul_swiglu_bwd(name, dffo, w_down, ff_du, ff_dg):
    s, d = dffo.shape
    tm, tn = 1024, 1408

    def body(a_ref, w_ref, pu_ref, pg_ref, dg_ref, du_ref):
        dff = _dot(a_ref[...], w_ref[...], "nt")
        du_ref[...] = (dff * pu_ref[...].astype(F32)).astype(du_ref.dtype)
        dg_ref[...] = (dff * pg_ref[...].astype(F32)).astype(dg_ref.dtype)

    o_spec = pl.BlockSpec((tm, tn), lambda j, i: (i, j))
    return _call(
        body, name=name, grid=(D_FF // tn, s // tm),
        in_specs=[pl.BlockSpec((tm, d), lambda j, i: (i, 0)), pl.BlockSpec((tn, d), lambda j, i: (j, 0)), o_spec, o_spec],
        out_specs=[o_spec] * 2, out_shape=[jax.ShapeDtypeStruct((s, D_FF), BF16)] * 2,
        compiler_params=_params(("parallel", "parallel"), 56),
    )(dffo, w_down, ff_du, ff_dg)


def _down(v, n):
    return pltpu.roll(v, n, 0)


def _up(v, n):
    return pltpu.roll(v, v.shape[0] - n, 0)


def _by_window(lane, v2, v4, v8, v16):
    return jnp.where(lane < POOL_GC, v2, jnp.where(lane < 2 * POOL_GC, v4, jnp.where(lane < 3 * POOL_GC, v8, v16)))


def _taps(cw_ref):
    return cw_ref[0:1, :], cw_ref[1:2, :], cw_ref[2:3, :]


def _conv_pool_forward(ext, t0, cw, bd):
    n_out = ext.shape[0] - HALO
    gb = ext[HALO:, 0:D_CONV]
    z = ext[:, D_CONV:2 * D_CONV] * ext[:, 2 * D_CONV:3 * D_CONV]
    z0, z1, z2 = z[HALO:], _down(z, 1)[HALO:], _down(z, 2)[HALO:]
    conv = cw[2] * z0 + cw[1] * z1 + cw[0] * z2
    x = ext[:, 3 * D_CONV:]
    w2 = x + _down(x, 1)
    w4 = w2 + _down(w2, 2)
    w8 = w4 + _down(w4, 4)
    w16 = w8 + _down(w8, 8)
    lane = lax.broadcasted_iota(jnp.int32, (1, D_POOL), 1)
    win = _by_window(lane, 2.0, 4.0, 8.0, 16.0)
    pos = (t0 + lax.broadcasted_iota(jnp.int32, (n_out, 1), 0) + 1).astype(F32)
    cnt = jnp.minimum(pos, win)
    d = _by_window(lane, w2, w4, w8, w16)[HALO:] / cnt - x[HALO:]
    ybp = _dot(d.astype(BF16), bd)
    return gb, z0, z1, z2, conv, d, cnt, ybp, lane


def conv_pool_fwd(name, proj_a, cw, bd, ps, gg, deps=()):
    s = proj_a.shape[0]
    t = T_CP
    hb = t // HALO

    def body(main_ref, prev_ref, cw_ref, bd_ref, ps_ref, gg_ref, y_ref):
        i = pl.program_id(0)
        prev = jnp.where(i > 0, prev_ref[...].astype(F32), 0.0)
        ext = jnp.concatenate([prev, main_ref[...].astype(F32)], axis=0)
        gb, _, _, _, conv, _, _, ybp, _ = _conv_pool_forward(ext, i * t, _taps(cw_ref), bd_ref[...])
        ya = gb * conv
        yb = ybp * ps_ref[...]
        ggv = gg_ref[...]
        y_ref[:, 0:D_CONV] = ((ya * _rms(ya)) * ggv[:, 0:D_CONV]).astype(y_ref.dtype)
        y_ref[:, D_CONV:] = ((yb * _rms(yb)) * ggv[:, D_CONV:]).astype(y_ref.dtype)

    full = lambda shape: pl.BlockSpec(shape, lambda i: (0,) * len(shape))
    return _call_after(
        deps, 6, body, name=name, grid=(s // t,),
        in_specs=[pl.BlockSpec((t, D_MODEL), lambda i: (i, 0)),
                  pl.BlockSpec((HALO, D_MODEL), lambda i: (jnp.maximum(i * hb - 1, 0), 0)),
                  full((3, D_CONV)), full((D_POOL, D_POOL)), full((1, D_POOL)), full((1, 2 * D_CONV))],
        out_specs=pl.BlockSpec((t, 2 * D_CONV), lambda i: (i, 0)),
        out_shape=jax.ShapeDtypeStruct((s, D_MODEL), BF16),
        compiler_params=_params(("parallel",)),
    )(proj_a, proj_a, cw, bd, ps, gg)


def conv_pool_bwd(name, proj_a, dy, cw, bd, ps, gg):
    s = proj_a.shape[0]
    t = T_CP
    hb = t // HALO
    nblk = s // t
    last_halo = s // HALO - 1

    def body(main_ref, prev_ref, next_ref, dy_ref, dyn_ref, cw_ref, bd_ref, ps_ref, gg_ref,
             dp_ref, dcw_ref, dbd_ref, dps_ref, dgg_ref):
        i = pl.program_id(0)
        prev = jnp.where(i > 0, prev_ref[...].astype(F32), 0.0)
        main = main_ref[...].astype(F32)
        ext = jnp.concatenate([prev, main, next_ref[...].astype(F32)], axis=0)
        cwv, bdv, psv, ggv = _taps(cw_ref), bd_ref[...], ps_ref[...], gg_ref[...]
        gb, z0, z1, z2, conv, d, cnt, ybp, lane = _conv_pool_forward(ext, i * t, cwv, bdv)
        dyn = jnp.where(i < nblk - 1, dyn_ref[...].astype(F32), 0.0)
        dyv = jnp.concatenate([dy_ref[...].astype(F32), dyn], axis=0)
        ya = gb * conv
        yb = ybp * psv
        dya, dgg_a = _norm_bwd(ya, _rms(ya), ggv[:, 0:D_CONV], dyv[:, 0:D_CONV])
        dyb, dgg_b = _norm_bwd(yb, _rms(yb), ggv[:, D_CONV:], dyv[:, D_CONV:])

        dconv = dya * gb
        dz = (cwv[2] * dconv + cwv[1] * _up(dconv, 1) + cwv[0] * _up(dconv, 2))[:t]
        dp_ref[:, 0:D_CONV] = (dya * conv)[:t].astype(dp_ref.dtype)
        dp_ref[:, D_CONV:2 * D_CONV] = (dz * main[:, 2 * D_CONV:3 * D_CONV]).astype(dp_ref.dtype)
        dp_ref[:, 2 * D_CONV:3 * D_CONV] = (dz * main[:, D_CONV:2 * D_CONV]).astype(dp_ref.dtype)

        dybs = dyb * psv
        dd = _dot(dybs.astype(BF16), bdv, "nt")
        e = dd / cnt
        a2 = e + _up(e, 1)
        a4 = a2 + _up(a2, 2)
        a8 = a4 + _up(a4, 4)
        a16 = a8 + _up(a8, 8)
        dp_ref[:, 3 * D_CONV:] = (_by_window(lane, a2, a4, a8, a16) - dd)[:t].astype(dp_ref.dtype)

        @pl.when(i == 0)
        def _():
            dcw_ref[...] = jnp.zeros_like(dcw_ref)
            dbd_ref[...] = jnp.zeros_like(dbd_ref)
            dps_ref[...] = jnp.zeros_like(dps_ref)
            dgg_ref[...] = jnp.zeros_like(dgg_ref)

        rsum = lambda v: jnp.sum(v[:t], axis=0, keepdims=True)
        dcw_ref[0:1, :] += rsum(dconv * z2)
        dcw_ref[1:2, :] += rsum(dconv * z1)
        dcw_ref[2:3, :] += rsum(dconv * z0)
        dbd_ref[...] += _dot(d[:t].astype(BF16), dybs[:t].astype(BF16), "tn")
        dps_ref[...] += rsum(dyb * ybp)
        dgg_ref[:, 0:D_CONV] += rsum(dgg_a)
        dgg_ref[:, D_CONV:] += rsum(dgg_b)

    full = lambda shape: pl.BlockSpec(shape, lambda i: (0,) * len(shape))
    next_halo = lambda i: (jnp.minimum((i + 1) * hb, last_halo), 0)
    return _call(
        body, name=name, grid=(nblk,),
        in_specs=[pl.BlockSpec((t, D_MODEL), lambda i: (i, 0)),
                  pl.BlockSpec((HALO, D_MODEL), lambda i: (jnp.maximum(i * hb - 1, 0), 0)),
                  pl.BlockSpec((HALO, D_MODEL), next_halo),
                  pl.BlockSpec((t, 2 * D_CONV), lambda i: (i, 0)),
                  pl.BlockSpec((HALO, 2 * D_CONV), next_halo),
                  full((3, D_CONV)), full((D_POOL, D_POOL)), full((1, D_POOL)), full((1, 2 * D_CONV))],
        out_specs=[pl.BlockSpec((t, D_MODEL), lambda i: (i, 0)),
                   full((3, D_CONV)), full((D_POOL, D_POOL)), full((1, D_POOL)), full((1, 2 * D_CONV))],
        out_shape=[jax.ShapeDtypeStruct((s, D_MODEL), BF16), jax.ShapeDtypeStruct((3, D_CONV), F32),
                   jax.ShapeDtypeStruct((D_POOL, D_POOL), F32), jax.ShapeDtypeStruct((1, D_POOL), F32),
                   jax.ShapeDtypeStruct((1, 2 * D_CONV), F32)],
        compiler_params=_params(("arbitrary",)),
    )(proj_a, proj_a, proj_a, dy, dy, cw, bd, ps, gg)


def bias_by_offset(rel_bias):
    n_far = 2 * QB - REL_CLIP + 1
    far = jnp.broadcast_to(rel_bias[:, 2 * REL_CLIP:], (N_HEADS, n_far))
    mid = rel_bias[:, 1:2 * REL_CLIP][:, ::-1]
    near = jnp.broadcast_to(rel_bias[:, 0:1], (N_HEADS, KB - n_far - (2 * REL_CLIP - 1)))
    wrap = jnp.broadcast_to(rel_bias[:, 2 * REL_CLIP:], (N_HEADS, 4 * QB - KB))
    return jnp.concatenate([far, mid, near, wrap], axis=1)


def _fill_bias(off_ref, b_ref):
    width = 4 * QB
    sub = lax.broadcasted_iota(jnp.int32, (8, 1), 0)
    col = lax.broadcasted_iota(jnp.int32, (1, KB), 1)
    for h in range(N_HEADS):
        base = jnp.broadcast_to(off_ref[h:h + 1, :], (8, width))
        for bit in range(3):
            base = jnp.where(((sub >> bit) & 1) == 1, pltpu.roll(base, 1 << bit, 1), base)
        for a in range(QB // 8):
            first = CHUNK * (8 * a // CHUNK)
            rows = (pltpu.roll(base, 8 * a, 1) if a else base)[:, :KB]
            band = (col >= first) & (col < first + (LEFT_CHUNKS + 1) * CHUNK)
            b_ref[h, 8 * a:8 * a + 8, :] = jnp.where(band, rows, NEG)


def rel_bias_grad(by_offset):
    n_far = 2 * QB - REL_CLIP + 1
    hi = jnp.sum(by_offset[:, :n_far], axis=1, keepdims=True) + jnp.sum(by_offset[:, KB:], axis=1, keepdims=True)
    mid = by_offset[:, n_far:n_far + 2 * REL_CLIP - 1][:, ::-1]
    lo = jnp.sum(by_offset[:, n_far + 2 * REL_CLIP - 1:KB], axis=1, keepdims=True)
    return jnp.concatenate([lo, mid, hi], axis=1)


def _head_masks():
    first = lax.broadcasted_iota(jnp.int32, (1, 2 * HEAD_DIM), 1) < HEAD_DIM
    return first, jnp.logical_not(first)


def _pick_lane(tile, h):
    lane = lax.broadcasted_iota(jnp.int32, (1, tile.shape[1]), 1)
    return jnp.sum(jnp.where(lane == h, tile, 0.0), axis=-1, keepdims=True)


def attention_fwd(name, qkv, by_offset, gg, y_ab):
    s = qkv.shape[0]
    nq = s // QB
    scale = HEAD_DIM ** -0.5

    def body(q_ref, k0, k1, k2, v0, v1, v2, off_ref, gg_ref, y_in, y_ref, o_ref, lse_ref, b_ref):
        del y_in
        i = pl.program_id(0)

        @pl.when(i == 0)
        def _():
            _fill_bias(off_ref, b_ref)
        def block(at_start):
            kb = jnp.concatenate([k0[...], k1[...], k2[...]], axis=0)
            vb = jnp.concatenate([v0[...], v1[...], v2[...]], axis=0)
            valid = lax.broadcasted_iota(jnp.int32, (1, KB), 1) >= (2 - i) * QB
            lane = lax.broadcasted_iota(jnp.int32, (1, 128), 1)
            masks = _head_masks()
            lse = jnp.zeros((QB, 128), F32)
            outs = []
            for hp in range(N_HEADS // 2):
                sl = slice(2 * HEAD_DIM * hp, 2 * HEAD_DIM * (hp + 1))
                q_p, k_p, v_p = q_ref[:, sl] * scale, kb[:, sl], vb[:, sl]
                o_pair = jnp.zeros((QB, 2 * HEAD_DIM), F32)
                for a in range(2):
                    h = 2 * hp + a
                    sc = _dot(jnp.where(masks[a], q_p, 0), k_p, "nt") + b_ref[h]
                    if at_start:
                        sc = jnp.where(valid, sc, NEG)
                    mx = jnp.max(sc, axis=-1, keepdims=True)
                    e = jnp.exp(sc - mx).astype(BF16)
                    ones_at = HEAD_DIM * (1 - a)
                    ev = _dot(e, jnp.where(masks[a], v_p, jnp.where(lane == ones_at, 1, 0).astype(v_p.dtype)))
                    l = jnp.sum(jnp.where(lane == ones_at, ev, 0.0), axis=-1, keepdims=True)
                    o_pair = o_pair + jnp.where(masks[a], ev, 0.0) * (1.0 / l)
                    lse = jnp.where(lane == h, mx + jnp.log(l), lse)
                outs.append(o_pair)
            o = jnp.concatenate(outs, axis=1)
            o_ref[...] = o.astype(o_ref.dtype)
            lse_ref[...] = lse
            y_ref[...] = ((o * _rms(o)) * gg_ref[...]).astype(y_ref.dtype)

        pl.when(i < 2)(lambda: block(True))
        pl.when(i >= 2)(lambda: block(False))

    blk = lambda col, back: pl.BlockSpec((QB, D_ATTN), lambda i: (jnp.maximum(i - back, 0), QKV_COL + col))
    return _call(
        body, name=name, grid=(nq,),
        in_specs=[blk(0, 0), blk(1, 2), blk(1, 1), blk(1, 0), blk(2, 2), blk(2, 1), blk(2, 0),
                  pl.BlockSpec((N_HEADS, 4 * QB), lambda i: (0, 0)), pl.BlockSpec((1, D_ATTN), lambda i: (0, 0)),
                  pl.BlockSpec(memory_space=pl.ANY)],
        out_specs=[pl.BlockSpec((QB, D_ATTN), lambda i: (i, 1)), pl.BlockSpec((QB, D_ATTN), lambda i: (i, 0)),
                   pl.BlockSpec((QB, 128), lambda i: (i, 0)), pl.BlockSpec((N_HEADS, QB, KB), lambda i: (0, 0, 0))],
        out_shape=[jax.ShapeDtypeStruct((s, D_MODEL), BF16), jax.ShapeDtypeStruct((s, D_ATTN), BF16),
                   jax.ShapeDtypeStruct((s, 128), F32), jax.ShapeDtypeStruct((N_HEADS, QB, KB), F32)],
        input_output_aliases={9: 0},
        compiler_params=_params(("arbitrary",), 56),
    )(qkv, qkv, qkv, qkv, qkv, qkv, qkv, by_offset, gg, y_ab)


def attention_bwd(name, qkv, o, lse, dy, bias, gg):
    s = qkv.shape[0]
    nq = s // QB
    scale = HEAD_DIM ** -0.5
    width = 4 * QB

    def body(q_ref, k0, k1, k2, v0, v1, v2, o_ref, lse_ref, dy_ref, b_ref, gg_ref,
             dq_ref, dk_ref, dv_ref, off_ref, dgg_ref, dk_acc, dv_acc, db_acc):
        i = pl.program_id(0)

        @pl.when(i == 0)
        def _():
            dk_acc[...] = jnp.zeros_like(dk_acc)
            dv_acc[...] = jnp.zeros_like(dv_acc)
            db_acc[...] = jnp.zeros_like(db_acc)
            dgg_ref[...] = jnp.zeros_like(dgg_ref)

        rows_of = lambda j: pl.ds(pl.multiple_of((j % 3) * QB, QB), QB)

        @pl.when(i > 0)
        def _():
            dk_acc[rows_of(i), :] = jnp.zeros((QB, D_ATTN), F32)
            dv_acc[rows_of(i), :] = jnp.zeros((QB, D_ATTN), F32)

        def block(at_start):
            ov = o_ref[...].astype(F32)
            dyv = dy_ref[...].astype(F32)
            do, dgg_t = _norm_bwd(ov, _rms(ov), gg_ref[...], dyv)
            dgg_ref[...] += jnp.sum(dgg_t, axis=0, keepdims=True)
            kb = jnp.concatenate([k0[...], k1[...], k2[...]], axis=0)
            vb = jnp.concatenate([v0[...], v1[...], v2[...]], axis=0)
            valid = lax.broadcasted_iota(jnp.int32, (1, KB), 1) >= (2 - i) * QB
            masks = _head_masks()
            lse_t = lse_ref[...]
            for hp in range(N_HEADS // 2):
                sl = slice(2 * HEAD_DIM * hp, 2 * HEAD_DIM * (hp + 1))
                q_p, k_p, v_p = q_ref[:, sl] * scale, kb[:, sl], vb[:, sl]
                do_p = do[:, sl]
                prod = do_p * ov[:, sl]
                do_b = do_p.astype(BF16)
                dq_pair = jnp.zeros((QB, 2 * HEAD_DIM), F32)
                dk_pair = jnp.zeros((KB, 2 * HEAD_DIM), F32)
                dv_pair = jnp.zeros((KB, 2 * HEAD_DIM), F32)
                for a in range(2):
                    h = 2 * hp + a
                    q_m = jnp.where(masks[a], q_p, 0)
                    do_m = jnp.where(masks[a], do_b, 0)
                    sc = _dot(q_m, k_p, "nt") + b_ref[h]
                    if at_start:
                        sc = jnp.where(valid, sc, NEG)
                    p = jnp.exp(sc - _pick_lane(lse_t, h))
                    dp = _dot(do_m, v_p, "nt")
                    delta = jnp.sum(jnp.where(masks[a], prod, 0.0), axis=-1, keepdims=True)
                    ds = p * (dp - delta)
                    db_acc[h] += ds
                    ds_b = ds.astype(BF16)
                    dq_pair = dq_pair + _dot(ds_b, jnp.where(masks[a], k_p, 0))
                    dk_pair = dk_pair + _dot(ds_b, q_m, "tn")
                    dv_pair = dv_pair + _dot(p.astype(BF16), do_m, "tn")
                dq_ref[:, sl] = (dq_pair * scale).astype(dq_ref.dtype)
                for t in range(3):
                    dk_acc[rows_of(i + 1 + t), sl] += dk_pair[QB * t:QB * (t + 1)]
                    dv_acc[rows_of(i + 1 + t), sl] += dv_pair[QB * t:QB * (t + 1)]

        pl.when(i < 2)(lambda: block(True))
        pl.when((i >= 2) & (i < nq))(lambda: block(False))

        dk_ref[...] = dk_acc[rows_of(i + 1), :].astype(dk_ref.dtype)
        dv_ref[...] = dv_acc[rows_of(i + 1), :].astype(dv_ref.dtype)

        @pl.when(i == nq + 1)
        def _():
            sub = lax.broadcasted_iota(jnp.int32, (8, 1), 0)
            pad = jnp.zeros((8, width - KB), F32)
            for h in range(N_HEADS):
                v = jnp.concatenate([db_acc[h, 0:8, :], pad], axis=1)
                for a in range(1, QB // 8):
                    grp = jnp.concatenate([db_acc[h, 8 * a:8 * a + 8, :], pad], axis=1)
                    v = v + pltpu.roll(grp, width - 8 * a, 1)
                for bit in range(3):
                    v = jnp.where(((sub >> bit) & 1) == 1, pltpu.roll(v, width - (1 << bit), 1), v)
                off_ref[h:h + 1, :] = jnp.sum(v, axis=0, keepdims=True)

    qi = lambda i: jnp.minimum(i, nq - 1)
    kblk = lambda col, back: pl.BlockSpec((QB, D_ATTN), lambda i: (jnp.clip(i - back, 0, nq - 1), QKV_COL + col))
    qblk = lambda col: pl.BlockSpec((QB, D_ATTN), lambda i: (qi(i), col))
    done = pl.BlockSpec((QB, D_ATTN), lambda i: (jnp.clip(i - 2, 0, nq - 1), 0))
    return _call(
        body, name=name, grid=(nq + 2,),
        in_specs=[qblk(QKV_COL), kblk(1, 2), kblk(1, 1), kblk(1, 0), kblk(2, 2), kblk(2, 1), kblk(2, 0),
                  qblk(0), pl.BlockSpec((QB, 128), lambda i: (qi(i), 0)), qblk(1),
                  pl.BlockSpec((N_HEADS, QB, KB), lambda i: (0, 0, 0)), pl.BlockSpec((1, D_ATTN), lambda i: (0, 0))],
        out_specs=[qblk(0), done, done, pl.BlockSpec((N_HEADS, width), lambda i: (0, 0)),
                   pl.BlockSpec((1, D_ATTN), lambda i: (0, 0))],
        out_shape=[jax.ShapeDtypeStruct((s, D_ATTN), BF16)] * 3
        + [jax.ShapeDtypeStruct((N_HEADS, width), F32), jax.ShapeDtypeStruct((1, D_ATTN), F32)],
        scratch_shapes=[pltpu.VMEM((KB, D_ATTN), F32), pltpu.VMEM((KB, D_ATTN), F32), pltpu.VMEM((N_HEADS, QB, KB), F32)],
        compiler_params=_params(("arbitrary",), 56),
    )(qkv, qkv, qkv, qkv, qkv, qkv, qkv, o, lse, dy, bias, gg)


def _block_diag(pw):
    out = jnp.zeros((D_POOL, D_POOL), pw.dtype)
    for gi in range(N_POOL):
        out = lax.dynamic_update_slice(out, pw[gi], (gi * POOL_GC, gi * POOL_GC))
    return out


_SMALL = ("pool_w", "pool_scale", "rel_bias", "group_gain", "pre_mix_g", "post_mix_g", "pre_ffn_g", "post_ffn_g")


def _pack(parts, rows):
    flat = jnp.concatenate([p.reshape(-1).astype(F32) for p in parts])
    return jnp.pad(flat, (0, rows * D_MODEL - flat.shape[0])).reshape(rows, D_MODEL)


def _unpack(packed, shapes):
    flat = packed.reshape(-1)
    out, at = [], 0
    for shp in shapes:
        size = int(np.prod(shp))
        out.append(flat[at:at + size].reshape(shp))
        at += size
    return out


def kernel(x, w_in, w_out, conv_w, pool_w, pool_scale, rel_bias, group_gain, pre_mix_g, post_mix_g, pre_ffn_g, post_ffn_g, w_gate_up, w_down, loss_target, m_w_in, m_w_out, m_conv_w, m_pool_w, m_pool_scale, m_rel_bias, m_group_gain, m_pre_mix_g, m_post_mix_g, m_pre_ffn_g, m_post_ffn_g, m_w_gate_up, m_w_down, v_w_in, v_w_out, v_conv_w, v_pool_w, v_pool_scale, v_rel_bias, v_group_gain, v_pre_mix_g, v_post_mix_g, v_pre_ffn_g, v_post_ffn_g, v_w_gate_up, v_w_down):
    depth = w_in.shape[0]
    s = x.shape[1]
    c_me = lax.axis_index("c")
    q_me = 2 * lax.axis_index("x") + lax.axis_index("y")
    dev = 2 * q_me + c_me

    tr = lambda a: jnp.swapaxes(a, 1, 2)
    w_in_t, w_gu_t = tr(w_in), tr(w_gate_up)

    def group_a(l):
        return [w_in_t[l].astype(BF16), w_out[l].astype(BF16), conv_w[l]]

    def group_b(l):
        return [w_gu_t[l].astype(BF16), w_down[l].astype(BF16)]

    def weights_a(parts):
        g_in, g_out, g_cw = parts
        return (assemble_rows("assemble_in", *g_in), assemble_rows("assemble_out", *g_out),
                assemble_cols("assemble_conv", *g_cw))

    def weights_b(parts):
        g_gu, g_dn = parts
        return assemble_rows("assemble_gu", *g_gu), assemble_rows("assemble_down", *g_dn)

    h = x.reshape(s, D_MODEL)
    start_a = gather_start("gather_a0_s", group_a(0))
    start_b = gather_start("gather_b0_s", group_b(0), [start_a["token"]])
    xn = rmsnorm_fwd("norm_mix", h, pre_mix_g[0].reshape(1, -1), BF16)
    relay_a = gather_relay("gather_a0_r", start_a, start_b["token"])
    wa = weights_a(gather_finish("gather_a0_f", relay_a, xn))
    wb = None
    ahead = [start_b["token"]]
    saved = []
    for l in range(depth):
        vec = lambda p: p[l].reshape(1, -1)
        wt_in, wf_out, cw_full = wa
        if l + 1 < depth:
            start_next = gather_start(f"gather_{l + 1}_s", group_a(l + 1) + group_b(l + 1), ahead)
            ahead = [start_next["token"]]
        proj = matmul("proj", xn, wt_in, "nt", BF16, s, 5 * D_ATTN, D_MODEL, 1024, 5 * D_ATTN, D_MODEL, deps=ahead)
        if l == 0:
            relay_b = gather_relay("gather_b0_r", start_b, proj)
            ahead = [relay_b["token"]]
        bd = _block_diag(pool_w[l]).astype(BF16)
        gg = vec(group_gain)
        y_ab = conv_pool_fwd("conv_pool_fwd", proj, cw_full, bd, vec(pool_scale), gg[:, :2 * D_CONV], deps=ahead)
        y, o, lse, bias = attention_fwd("attention_fwd", proj, bias_by_offset(rel_bias[l]), gg[:, 2 * D_CONV:], y_ab)
        if l == 0:
            wb = weights_b(gather_finish("gather_b0_f", relay_b, y))
        wt_gu, wf_dn = wb
        if l + 1 < depth:
            relay_next = gather_relay(f"gather_{l + 1}_r", start_next, y)
        mix, h_mid, hn = matmul_norm("mix_out", y, wf_out, h, vec(post_mix_g), then=(vec(pre_ffn_g), BF16), tm=1024,
                                     deps=[relay_next["token"]] if l + 1 < depth else [])
        ff_du, ff_dg, ff = matmul_swiglu("gate_up", hn, wt_gu)
        if l + 1 < depth:
            parts = gather_finish(f"gather_{l + 1}_f", relay_next, ff)
            wa, wb = weights_a(parts[:3]), weights_b(parts[3:])
            ffo, h_out, xn_next = matmul_norm("ffn_down", ff, wf_dn, h_mid, vec(post_ffn_g),
                                              then=(pre_mix_g[l + 1].reshape(1, -1), BF16))
        else:
            ffo, h_out = matmul_norm("ffn_down_last", ff, wf_dn, h_mid, vec(post_ffn_g))
            xn_next = None
        saved.append((h, xn, proj, None, bd, bias, y, o, lse, mix, h_mid, hn, ff_du, ff_dg, ff, ffo,
                      wt_in, wf_out, cw_full, wt_gu, wf_dn))
        h, xn = h_out, xn_next

    dh, loss_part, dffo, dg_last = loss_head("loss_head", h, loss_target.reshape(s, D_MODEL), saved[-1][15],
                                             post_ffn_g[depth - 1].reshape(1, -1))

    def reduce_begin(name, halves):
        return [k for k, _ in halves], exchange_start(name, plan_swap_cores, len(halves), [snd for _, snd in halves],
                                                      [(k.shape, k.dtype) for k, _ in halves])

    def reduce_relay(name, kept, swapped, after):
        _, got = exchange_wait(name + "_w", swapped, after)
        sums = [add_pairs("reduce_add", k, r) for k, r in zip(kept, got)]
        return exchange_start(name + "_s", plan_to_chips, 3 * len(sums), sums, [(a.shape, a.dtype) for a in sums])

    def reduce_finish(name, relayed, after):
        sums, got = exchange_wait(name, relayed, after)
        return list(zip(sums, got))

    small = {k: [None] * depth for k in _SMALL + ("conv_w",)}
    relayed = {}
    carried = []
    small["post_ffn_g"][depth - 1] = dg_last
    for l in reversed(range(depth)):
        vec = lambda p: p[l].reshape(1, -1)
        (h_in, xn, proj, _, bd, bias, y, o, lse, mix, h_mid, hn, ff_du, ff_dg, ff, ffo,
         wt_in, wf_out, cw_full, wt_gu, wf_dn) = saved[l]
        gg = vec(group_gain)
        tk_w = min(s, 2048)
        gw_dn = matmul("wgrad_down", ff, dffo, "tn", BF16, D_FF, D_MODEL, s, 1408, D_MODEL, tk_w,
                       shard_rows=D_FF // 8, deps=carried)
        dg, du = matmul_swiglu_bwd("dgrad_down", dffo, wf_dn, ff_du, ff_dg)
        gw_gu = matmul("wgrad_gu", [dg, du], hn, "tn", BF16, 2 * D_FF, D_MODEL, s, 1408, D_MODEL, tk_w,
                       shard_rows=2 * D_FF // 8)
        early = [gw_gu, gw_dn] if l == 0 else []
        if early:
            kept_b, swap_b = reduce_begin("reduce_b0_s", early)
            carried = [swap_b["token"]]
        dh_mid, dmix, small["pre_ffn_g"][l], small["post_mix_g"][l] = matmul_norm_bwd(
            "dgrad_gu", [dg, du], wt_gu, 256, h_mid, vec(pre_ffn_g), dh, then=(mix, vec(post_mix_g), BF16),
            deps=carried)
        if early:
            relayed[l, "b"] = reduce_relay("reduce_b0_r", kept_b, swap_b, dmix)
            carried = [relayed[l, "b"]["token"]]
        gw_out = matmul("wgrad_out", y, dmix, "tn", BF16, D_MODEL, D_MODEL, s, D_MODEL, D_MODEL, tk_w,
                        shard_rows=D_MODEL // 8, deps=carried)
        dy = matmul("dgrad_out", dmix, wf_out, "nt", BF16, s, D_MODEL, D_MODEL, 1024, D_MODEL, D_MODEL)
        dpa, dcw, dbd, dps, dgg_ab = conv_pool_bwd("conv_pool_bwd", proj, dy, cw_full, bd, vec(pool_scale),
                                                   gg[:, :2 * D_CONV])
        dq, dk, dv, by_off, dgg_c = attention_bwd("attention_bwd", proj, o, lse, dy, bias, gg[:, 2 * D_CONV:])
        dparts = [dpa, dq, dk, dv]
        gw_in = matmul("wgrad_in", dparts, xn, "tn", BF16, 5 * D_ATTN, D_MODEL, s, 512, D_MODEL, min(s, 2048))
        late = [split_rows("split_in", gw_in), gw_out] + ([] if early else [gw_gu, gw_dn])
        kept_a, swap_a = reduce_begin(f"reduce_a{l}_s", late)
        if l > 0:
            dh, dffo, small["pre_mix_g"][l], small["post_ffn_g"][l - 1] = matmul_norm_bwd(
                "dgrad_in", dparts, wt_in, 512, h_in, vec(pre_mix_g), dh_mid,
                then=(saved[l - 1][15], post_ffn_g[l - 1].reshape(1, -1), BF16), chunk=32, deps=[swap_a["token"]])
        else:
            dh, small["pre_mix_g"][l] = matmul_norm_bwd("dgrad_in_first", dparts, wt_in, 512, h_in, vec(pre_mix_g),
                                                        dh_mid, chunk=32, deps=[swap_a["token"]])
        relayed[l, "a"] = reduce_relay(f"reduce_a{l}_r", kept_a, swap_a, dh)
        carried = [relayed[l, "a"]["token"]]
        small["conv_w"][l] = dcw
        small["pool_w"][l] = jnp.stack([dbd[gi * POOL_GC:(gi + 1) * POOL_GC, gi * POOL_GC:(gi + 1) * POOL_GC]
                                        for gi in range(N_POOL)])
        small["pool_scale"][l] = dps
        small["rel_bias"][l] = rel_bias_grad(by_off)
        small["group_gain"][l] = jnp.concatenate([dgg_ab, dgg_c], axis=1)
    grad_x = dh.reshape(x.shape)

    small_params = dict(pool_w=pool_w, pool_scale=pool_scale, rel_bias=rel_bias, group_gain=group_gain,
                        pre_mix_g=pre_mix_g, post_mix_g=post_mix_g, pre_ffn_g=pre_ffn_g, post_ffn_g=post_ffn_g)
    small_m = dict(pool_w=m_pool_w, pool_scale=m_pool_scale, rel_bias=m_rel_bias, group_gain=m_group_gain,
                   pre_mix_g=m_pre_mix_g, post_mix_g=m_post_mix_g, pre_ffn_g=m_pre_ffn_g, post_ffn_g=m_post_ffn_g)
    small_v = dict(pool_w=v_pool_w, pool_scale=v_pool_scale, rel_bias=v_rel_bias, group_gain=v_group_gain,
                   pre_mix_g=v_pre_mix_g, post_mix_g=v_post_mix_g, pre_ffn_g=v_pre_ffn_g, post_ffn_g=v_post_ffn_g)
    shapes = [small_params[k].shape for k in _SMALL] + [(depth, 3, D_CONV), (1,)]
    n_small = sum(int(np.prod(shp)) for shp in shapes)
    rows = -(-n_small // (8 * D_MODEL)) * 8
    extra = [jnp.zeros((depth, 3, D_CONV), F32), jnp.zeros((1,), F32)]
    grads_packed = _pack([jnp.stack(small[k]) for k in _SMALL] + [jnp.stack(small["conv_w"]), loss_part[0, 0:1]], rows)
    small_started = gather_start("gather_small_s", [grads_packed], carried)

    big = dict(w_in=(w_in_t, tr(m_w_in), tr(v_w_in)), w_out=(w_out, m_w_out, v_w_out),
               w_gate_up=(w_gu_t, tr(m_w_gate_up), tr(v_w_gate_up)), w_down=(w_down, m_w_down, v_w_down))
    results = dict(w_in=None, w_out=None, w_gate_up=None, w_down=None)
    after = small_started["token"]
    small_relay = None
    for l in reversed(range(depth)):
        if (l, "b") in relayed:
            slabs_gu, slabs_dn = reduce_finish(f"reduce_b{l}_f", relayed[l, "b"], after)
            slabs_in, slabs_out = reduce_finish(f"reduce_a{l}_f", relayed[l, "a"], slabs_dn[1])
        else:
            slabs_in, slabs_out, slabs_gu, slabs_dn = reduce_finish(f"reduce_a{l}_f", relayed[l, "a"], after)
        for k, slabs in (("w_gate_up", slabs_gu), ("w_down", slabs_dn), ("w_in", slabs_in), ("w_out", slabs_out)):
            results[k] = adamw_layer("adamw_" + k, l, *big[k], *slabs, results[k])
        after = results["w_out"][0]
        if l == min(1, depth - 1):
            small_relay = gather_relay("gather_small_r", small_started, after)
    small_parts = gather_finish("gather_small_f", small_relay, after)[0]
    all_small = assemble_rows("assemble_small", *small_parts).reshape(8, rows, D_MODEL)
    res_small = adamw("adamw_small", _pack([small_params[k] for k in _SMALL] + extra, rows),
                      _pack([small_m[k] for k in _SMALL] + extra, rows),
                      _pack([small_v[k] for k in _SMALL] + extra, rows), all_small)
    g_s, d_s, m_s, v_s = (_unpack(r, shapes) for r in res_small)
    loss = g_s[-1][0]
    g_conv = lax.dynamic_slice_in_dim(g_s[-2], dev * conv_w.shape[2], conv_w.shape[2], axis=2)
    results["conv_w"] = adamw("adamw_conv", conv_w, m_conv_w, v_conv_w, g_conv[None])

    names = ("w_in", "w_out", "conv_w") + _SMALL + ("w_gate_up", "w_down")
    for k in ("w_in", "w_gate_up"):
        results[k] = tuple(tr(r) for r in results[k])
    for j, k in enumerate(_SMALL):
        results[k] = (g_s[j], d_s[j], m_s[j], v_s[j])
    return (loss, grad_x, *[results[k][0] for k in names], *[results[k][1] for k in names],
            *[results[k][2] for k in names], *[results[k][3] for k in names])
```

```python
import numpy as np
import jax
import jax.numpy as jnp
from jax import lax
from jax.experimental import pallas as pl
from jax.experimental.pallas import tpu as pltpu

F32 = jnp.float32
BF16 = jnp.bfloat16

CHUNK = 64
D_MODEL = 1024
D_CONV = 256
D_POOL = 256
D_ATTN = 512
HEAD_DIM = 64
N_HEADS = 8
N_POOL = 4
POOL_GC = 64
LEFT_CHUNKS = 8
REL_CLIP = 128
D_FF = 2816
EPS = 1e-6
ADAM_LR, ADAM_B1, ADAM_B2, ADAM_EPS, ADAM_WD, ADAM_STEP = 0.001, 0.9, 0.999, 1e-08, 0.01, 10

QB = 256
KB = 3 * QB
HALO = 16
T_CP = 512
T_ROW = 512
QKV_COL = 2
NEG = -1e30
VMEM_MB = 1 << 20
MESH = pl.DeviceIdType.MESH


def _call(body, **kw):
    call = pl.pallas_call(body, **kw)
    return lambda *args: call(*[_in_hbm(a) for a in args])


def _in_hbm(a):
    return pltpu.with_memory_space_constraint(a, pltpu.HBM) if jnp.issubdtype(a.dtype, jnp.number) else a


def _call_after(deps, n_in, body, **kw):
    deps = tuple(deps)
    if not deps:
        return _call(body, **kw)

    def ordered(*refs):
        body(*refs[:n_in], *refs[n_in + len(deps):])

    kw["in_specs"] = list(kw["in_specs"]) + [pl.BlockSpec(memory_space=pl.ANY)] * len(deps)
    call = _call(ordered, **kw)
    return lambda *args: call(*args, *deps)


def _params(sem, vmem_mb=48):
    return pltpu.CompilerParams(dimension_semantics=sem, vmem_limit_bytes=vmem_mb * VMEM_MB)


_CHIP_FLIPS = ((1, 0), (0, 1), (1, 1))
_HBM = pl.BlockSpec(memory_space=pltpu.HBM)
_SEM = pl.BlockSpec(memory_space=pltpu.SEMAPHORE)
_EFFECT = pltpu.SideEffectType.DATAFLOW_SIDE_EFFECTING


def _flip(v, f):
    return 1 - v if f else v


def _descriptors(plan, srcs, lands, send_sems, recv_sems, sending):
    x, y, c = lax.axis_index("x"), lax.axis_index("y"), lax.axis_index("c")
    return [pltpu.make_async_remote_copy(src_ref=src, dst_ref=dst if sending else land, send_sem=send_sems.at[k],
                                         recv_sem=recv_sems.at[k], device_id=peer, device_id_type=MESH)
            for k, (src, dst, peer, land) in enumerate(plan(srcs, lands, x, y, c))]


def exchange_start(name, plan, n_copies, srcs, land_shapes, deps=()):
    ns, nl = len(srcs), len(land_shapes)

    def body(*refs):
        src_refs, land_refs = refs[:ns], refs[ns:ns + nl]
        send_sems, recv_sems = refs[ns + nl], refs[ns + nl + 1]
        for send in _descriptors(plan, src_refs, land_refs, send_sems, recv_sems, True):
            send.start()
        refs[-1][...] = jnp.zeros_like(refs[-1])

    lands = [lax.empty(shape, dtype) for shape, dtype in land_shapes]
    outs = _call_after(
        deps, ns + nl, body, name=name,
        out_shape=(pltpu.SemaphoreType.DMA((n_copies,)), pltpu.SemaphoreType.DMA((n_copies,)),
                   *[pltpu.HBM(a.shape, a.dtype) for a in srcs], *[pltpu.HBM(shape, dtype) for shape, dtype in land_shapes],
                   jax.ShapeDtypeStruct((8, 128), F32)),
        in_specs=[_HBM] * (ns + nl),
        out_specs=(_SEM, _SEM, *[_HBM] * (ns + nl), pl.BlockSpec(memory_space=pltpu.VMEM)),
        input_output_aliases={j: 2 + j for j in range(ns + nl)},
        compiler_params=pltpu.CompilerParams(has_side_effects=_EFFECT),
    )(*srcs, *lands)
    return dict(plan=plan, sems=outs[:2], srcs=outs[2:2 + ns], lands=outs[2 + ns:2 + ns + nl], token=outs[-1])


def exchange_wait(name, started, after):
    srcs, lands = started["srcs"], started["lands"]
    ns, nl = len(srcs), len(lands)

    def body(*refs):
        src_refs, land_refs = refs[:ns], refs[ns:ns + nl]
        send_sems, recv_sems = refs[ns + nl], refs[ns + nl + 1]
        for wait in _descriptors(started["plan"], src_refs, land_refs, send_sems, recv_sems, False):
            wait.wait_send()
            wait.wait_recv()

    outs = _call(
        body, name=name,
        out_shape=tuple(pltpu.HBM(a.shape, a.dtype) for a in (*srcs, *lands)),
        in_specs=[_HBM] * (ns + nl) + [_SEM, _SEM, pl.BlockSpec(memory_space=pl.ANY)],
        out_specs=tuple([_HBM] * (ns + nl)),
        input_output_aliases={j: j for j in range(ns + nl)},
        compiler_params=pltpu.CompilerParams(has_side_effects=_EFFECT),
    )(*srcs, *lands, *started["sems"], after)
    return list(outs[:ns]), list(outs[ns:])


def plan_from_chips(srcs, lands, x, y, c):
    q = 2 * x + y
    out = []
    for src, land in zip(srcs, lands):
        for fx, fy in _CHIP_FLIPS:
            px, py = _flip(x, fx), _flip(y, fy)
            out.append((src, land.at[q], (px, py, c), land.at[2 * px + py]))
    return out


def plan_to_core(srcs, lands, x, y, c):
    n = len(lands)
    q = 2 * x + y
    out = []
    for own, chips, land in zip(srcs[:n], srcs[n:], lands):
        out.append((own, land.at[q], (x, y, 1 - c), land.at[q]))
        for fx, fy in _CHIP_FLIPS:
            qp = 2 * _flip(x, fx) + _flip(y, fy)
            out.append((chips.at[qp], land.at[qp], (x, y, 1 - c), land.at[qp]))
    return out


def plan_swap_cores(srcs, lands, x, y, c):
    return [(src, land, (x, y, 1 - c), land) for src, land in zip(srcs, lands)]


def plan_to_chips(srcs, lands, x, y, c):
    q = 2 * x + y
    out = []
    for src, land in zip(srcs, lands):
        for fx, fy in _CHIP_FLIPS:
            px, py = _flip(x, fx), _flip(y, fy)
            out.append((src.at[2 * px + py], land.at[q], (px, py, c), land.at[2 * px + py]))
    return out


def _slots(a):
    return ((4,) + a.shape, a.dtype)


def gather_start(name, arrays, deps=()):
    return exchange_start(name, plan_from_chips, 3 * len(arrays), arrays, [_slots(a) for a in arrays], deps)


def gather_relay(name, started, after):
    own, chips = exchange_wait(name + "_w", started, after)
    return exchange_start(name + "_s", plan_to_core, 4 * len(own), own + chips, [_slots(a) for a in own])


def gather_finish(name, relayed, after):
    srcs, cores = exchange_wait(name, relayed, after)
    n = len(cores)
    return list(zip(srcs[:n], srcs[n:], cores))


def _place():
    return 2 * lax.axis_index("x") + lax.axis_index("y"), lax.axis_index("c")


def assemble_cols(name, own, chips, core):
    r, c = own.shape
    t = _row_tile(r, 256)

    def body(own_ref, chips_ref, core_ref, o_ref):
        q_me, c_me = _place()
        for q in range(4):
            mine = jnp.where(q == q_me, own_ref[...], chips_ref[q])
            other = core_ref[q]
            o_ref[:, 2 * q * c:(2 * q + 1) * c] = jnp.where(c_me == 0, mine, other)
            o_ref[:, (2 * q + 1) * c:(2 * q + 2) * c] = jnp.where(c_me == 0, other, mine)

    slots = pl.BlockSpec((4, t, c), lambda i: (0, i, 0))
    return _call(
        body, name=name, grid=(r // t,),
        in_specs=[pl.BlockSpec((t, c), lambda i: (i, 0)), slots, slots],
        out_specs=pl.BlockSpec((t, 8 * c), lambda i: (i, 0)),
        out_shape=jax.ShapeDtypeStruct((r, 8 * c), own.dtype), compiler_params=_params(("parallel",)),
    )(own, chips, core)


def assemble_rows(name, own, chips, core):
    r, c = own.shape

    def body(own_ref, chips_ref, core_ref, o_ref):
        q_me, c_me = _place()
        d = pl.program_id(0)
        mine = jnp.where(d // 2 == q_me, own_ref[...], chips_ref[...])
        o_ref[...] = jnp.where(d % 2 == c_me, mine, core_ref[...])

    slot = pl.BlockSpec((None, r, c), lambda d: (d // 2, 0, 0))
    return _call(
        body, name=name, grid=(8,),
        in_specs=[pl.BlockSpec((r, c), lambda d: (0, 0)), slot, slot],
        out_specs=pl.BlockSpec((r, c), lambda d: (d, 0)),
        out_shape=jax.ShapeDtypeStruct((8 * r, c), own.dtype), compiler_params=_params(("parallel",)),
    )(own, chips, core)


def split_rows(name, dw):
    r8, c = dw.shape
    r = r8 // 8

    def body(dw_ref, keep_ref, send_ref):
        _, c_me = _place()
        d = pl.program_id(0)

        @pl.when(d % 2 == c_me)
        def _():
            keep_ref[...] = dw_ref[...]

        @pl.when(d % 2 != c_me)
        def _():
            send_ref[...] = dw_ref[...]

    slot = pl.BlockSpec((None, r, c), lambda d: (d // 2, 0, 0))
    return _call(
        body, name=name, grid=(8,),
        in_specs=[pl.BlockSpec((r, c), lambda d: (d, 0))], out_specs=[slot, slot],
        out_shape=[jax.ShapeDtypeStruct((4, r, c), dw.dtype)] * 2, compiler_params=_params(("arbitrary",)),
    )(dw)


def _rms(v):
    return lax.rsqrt(jnp.mean(v * v, axis=-1, keepdims=True) + EPS)


def rmsnorm_fwd(name, x, g, out_dtype):
    s, d = x.shape

    def body(x_ref, g_ref, o_ref):
        xv = x_ref[...]
        o_ref[...] = ((xv * _rms(xv)) * g_ref[...]).astype(o_ref.dtype)

    row = pl.BlockSpec((T_ROW, d), lambda i: (i, 0))
    return _call(
        body, name=name, grid=(s // T_ROW,),
        in_specs=[row, pl.BlockSpec((1, d), lambda i: (0, 0))], out_specs=row,
        out_shape=jax.ShapeDtypeStruct((s, d), out_dtype), compiler_params=_params(("parallel",)),
    )(x, g)


def matmul_then(name, a_parts, b, mode, tm, chunk, rows_in, vecs_in, rows_out, n_sums, then, deps=()):
    a_parts = list(a_parts)
    na, nr, nv, no = len(a_parts), len(rows_in), len(vecs_in), len(rows_out)
    s = a_parts[0].shape[0]
    n = s // tm
    d = b.shape[1] if mode == "nn" else b.shape[0]
    offs = [0]
    for p in a_parts:
        offs.append(offs[-1] + p.shape[1])
    n_in = na + 1 + nr + nv

    def body(*refs):
        a_refs, b_ref = refs[:na], refs[na]
        row_refs, vec_refs = refs[na + 1:na + 1 + nr], refs[na + 1 + nr:n_in]
        out_refs, sum_refs = refs[n_in:n_in + no], refs[n_in + no:n_in + no + n_sums]
        acc = refs[-2:]
        i = pl.program_id(0)

        @pl.when(i == 0)
        def _():
            acc[1][...] = jnp.zeros_like(acc[1])
            for s_ref in sum_refs:
                s_ref[...] = jnp.zeros_like(s_ref)

        def step(write, read):
            t = None
            for p in range(na):
                lo, hi = offs[p], offs[p + 1]
                part = _dot(a_refs[p][...], b_ref[lo:hi, :] if mode == "nn" else b_ref[:, lo:hi], mode)
                t = part if t is None else t + part
            write[...] = t
            totals = [0.0] * n_sums
            for r0 in range(0, tm, chunk):
                rs = pl.ds(r0, chunk)
                results, sums = then(read[rs, :], [r.at[rs, :] for r in row_refs], vec_refs)
                for o_ref, val in zip(out_refs, results):
                    o_ref[rs, :] = val.astype(o_ref.dtype)
                totals = [tot + val for tot, val in zip(totals, sums)]
            for s_ref, val in zip(sum_refs, totals):
                s_ref[...] += jnp.where(i > 0, val, 0.0)

        pl.when(i % 2 == 0)(lambda: step(acc[0], acc[1]))
        pl.when(i % 2 == 1)(lambda: step(acc[1], acc[0]))

    ahead = lambda i: (jnp.minimum(i, n - 1), 0)
    behind = lambda i: (jnp.maximum(i - 1, 0), 0)
    fixed = lambda i: (0, 0)
    shapes = [jax.ShapeDtypeStruct((s, d), dt) for dt in rows_out] + [jax.ShapeDtypeStruct((1, d), F32)] * n_sums
    return _call_after(
        deps, n_in, body, name=name, grid=(n + 1,),
        in_specs=[pl.BlockSpec((tm, p.shape[1]), ahead) for p in a_parts] + [pl.BlockSpec(b.shape, fixed)]
        + [pl.BlockSpec((tm, r.shape[1]), behind) for r in rows_in] + [pl.BlockSpec(v.shape, fixed) for v in vecs_in],
        out_specs=[pl.BlockSpec((tm, d), behind)] * no + [pl.BlockSpec((1, d), fixed)] * n_sums,
        out_shape=shapes, scratch_shapes=[pltpu.VMEM((tm, d), F32)] * 2,
        compiler_params=_params(("arbitrary",), 56),
    )(*a_parts, b, *rows_in, *vecs_in)


def matmul_norm(name, a, b, res, g, then=None, tm=512, deps=()):
    def norm(z, rows, vecs):
        y = rows[0][...] + (z * _rms(z)) * vecs[0][...]
        return [z, y] + ([(y * _rms(y)) * vecs[1][...]] if then else []), []

    return matmul_then(name, [a], b, "nn", tm, tm, [res], [g] + ([then[0]] if then else []),
                       [BF16, F32] + ([then[1]] if then else []), 0, norm, deps)


def matmul_norm_bwd(name, a_parts, b, tm, x, g, res, then=None, chunk=None, deps=()):
    def norms(t, rows, vecs):
        xv = rows[0][...].astype(F32)
        dx, dgt = _norm_bwd(xv, _rms(xv), vecs[0][...], t)
        dx = dx + rows[1][...]
        results, sums = [dx], [jnp.sum(dgt, axis=0, keepdims=True)]
        if then:
            x2 = rows[2][...].astype(F32)
            dx2, dgt2 = _norm_bwd(x2, _rms(x2), vecs[1][...], dx)
            results.append(dx2)
            sums.append(jnp.sum(dgt2, axis=0, keepdims=True))
        return results, sums

    return matmul_then(name, a_parts, b, "nn", tm, chunk or tm, [x, res] + ([then[0]] if then else []),
                       [g] + ([then[1]] if then else []), [F32] + ([then[2]] if then else []),
                       2 if then else 1, norms, deps)


def _norm_bwd(xv, r, g, dy):
    a = dy * g
    dx = r * (a - xv * ((r * r) * jnp.mean(a * xv, axis=-1, keepdims=True)))
    return dx, dy * (xv * r)


def loss_head(name, h, tgt, z, g):
    s, d = h.shape

    def body(h_ref, t_ref, z_ref, g_ref, dh_ref, l_ref, dz_ref, dg_ref):
        e = h_ref[...] - t_ref[...]
        dh = e * (1.0 / d)
        dh_ref[...] = dh
        zv = z_ref[...].astype(F32)
        dz, dgt = _norm_bwd(zv, _rms(zv), g_ref[...], dh)
        dz_ref[...] = dz.astype(dz_ref.dtype)

        @pl.when(pl.program_id(0) == 0)
        def _():
            l_ref[...] = jnp.zeros_like(l_ref)
            dg_ref[...] = jnp.zeros_like(dg_ref)
        part = 0.5 * jnp.sum(jnp.mean(e * e, axis=-1, keepdims=True), axis=0, keepdims=True)
        l_ref[...] += jnp.broadcast_to(part, l_ref.shape)
        dg_ref[...] += jnp.sum(dgt, axis=0, keepdims=True)

    row = pl.BlockSpec((T_ROW, d), lambda i: (i, 0))
    vec = pl.BlockSpec((1, d), lambda i: (0, 0))
    return _call(
        body, name=name, grid=(s // T_ROW,),
        in_specs=[row, row, row, vec], out_specs=[row, pl.BlockSpec((1, 128), lambda i: (0, 0)), row, vec],
        out_shape=[jax.ShapeDtypeStruct((s, d), F32), jax.ShapeDtypeStruct((1, 128), F32),
                   jax.ShapeDtypeStruct((s, d), BF16), jax.ShapeDtypeStruct((1, d), F32)],
        compiler_params=_params(("arbitrary",)),
    )(h, tgt, z, g)


def _row_tile(rows, limit=512):
    t = min(rows, limit)
    while rows % t or (t % 8 and t != rows):
        t -= 1
    return t


def add_pairs(name, a, b):
    shape = a.shape
    a2, b2 = a.reshape(-1, shape[-1]), b.reshape(-1, shape[-1])
    rows, cols = a2.shape
    t = _row_tile(rows)

    def body(a_ref, b_ref, o_ref):
        o_ref[...] = (a_ref[...].astype(F32) + b_ref[...].astype(F32)).astype(o_ref.dtype)

    blk = pl.BlockSpec((t, cols), lambda i: (i, 0))
    out = _call(
        body, name=name, grid=(rows // t,), in_specs=[blk, blk], out_specs=blk,
        out_shape=jax.ShapeDtypeStruct((rows, cols), a.dtype), compiler_params=_params(("parallel",)),
    )(a2, b2)
    return out.reshape(shape)


def _adamw_math(w, m, v, g):
    c1 = 1.0 - ADAM_B1 ** ADAM_STEP
    c2 = 1.0 - ADAM_B2 ** ADAM_STEP
    mn = ADAM_B1 * m + (1.0 - ADAM_B1) * g
    vn = ADAM_B2 * v + (1.0 - ADAM_B2) * (g * g)
    delta = -ADAM_LR * ((mn / c1) / (jnp.sqrt(vn / c2) + ADAM_EPS) + ADAM_WD * w)
    return delta, mn, vn


def _slab_sum(g_ref, n):
    g = g_ref[0].astype(F32)
    for j in range(1, n):
        g = g + g_ref[j].astype(F32)
    return g


def adamw(name, w, m, v, g_slabs, deps=()):
    shape = w.shape
    n = g_slabs.shape[0]
    w2, m2, v2 = (t.reshape(-1, shape[-1]) for t in (w, m, v))
    g3 = g_slabs.reshape(n, -1, shape[-1])
    rows, cols = w2.shape
    t = _row_tile(rows, 256)

    def body(w_ref, m_ref, v_ref, g_ref, go_ref, d_ref, mo_ref, vo_ref):
        g = _slab_sum(g_ref, n)
        go_ref[...] = g
        d_ref[...], mo_ref[...], vo_ref[...] = _adamw_math(w_ref[...], m_ref[...], v_ref[...], g)

    blk = pl.BlockSpec((t, cols), lambda i: (i, 0))
    outs = _call_after(
        deps, 4, body, name=name, grid=(rows // t,),
        in_specs=[blk, blk, blk, pl.BlockSpec((n, t, cols), lambda i: (0, i, 0))], out_specs=[blk] * 4,
        out_shape=[jax.ShapeDtypeStruct((rows, cols), F32)] * 4, compiler_params=_params(("parallel",)),
    )(w2, m2, v2, g3)
    return tuple(o.reshape(shape) for o in outs)


def adamw_layer(name, l, w, m, v, own_sums, chip_sums, into):
    _, rows, cols = w.shape
    t = _row_tile(rows, 256)
    if into is None:
        into = tuple(lax.empty(w.shape, F32) for _ in range(4))

    def body(w_ref, m_ref, v_ref, own_ref, far_ref, a0, a1, a2, a3, go_ref, d_ref, mo_ref, vo_ref):
        del a0, a1, a2, a3
        q_me, _ = _place()
        g = jnp.zeros((t, cols), F32)
        for q in range(4):
            g = g + jnp.where(q == q_me, own_ref[q], far_ref[q]).astype(F32)
        go_ref[...] = g
        d_ref[...], mo_ref[...], vo_ref[...] = _adamw_math(w_ref[...], m_ref[...], v_ref[...], g)

    blk = pl.BlockSpec((None, t, cols), lambda i: (l, i, 0))
    slabs = pl.BlockSpec((4, t, cols), lambda i: (0, i, 0))
    anyw = pl.BlockSpec(memory_space=pl.ANY)
    return _call(
        body, name=name, grid=(rows // t,),
        in_specs=[blk, blk, blk, slabs, slabs] + [anyw] * 4, out_specs=[blk] * 4,
        out_shape=[jax.ShapeDtypeStruct(w.shape, F32)] * 4, input_output_aliases={5: 0, 6: 1, 7: 2, 8: 3},
        compiler_params=_params(("parallel",)),
    )(w, m, v, own_sums, chip_sums, *into)


_DIMS = {"nn": (((1,), (0,)), ((), ())), "nt": (((1,), (1,)), ((), ())), "tn": (((0,), (0,)), ((), ()))}


def _dot(a, b, mode="nn"):
    return lax.dot_general(a, b, _DIMS[mode], preferred_element_type=F32)


def matmul(name, a, b, mode, out_dtype, m, n, k, tm, tn, tk, b_off=(0, 0), shard_rows=0, deps=()):
    gm, gn, gk = m // tm, n // tn, k // tk
    assert gm * tm == m and gn * tn == n and gk * tk == k
    r0, c0 = b_off
    a_parts = list(a) if isinstance(a, (list, tuple)) else [a]
    na = len(a_parts)
    tile = tm if mode == "tn" else tk
    spans, at = [], 0
    for p in a_parts:
        nblk = p.shape[1] // tile
        assert nblk * tile == p.shape[1]
        spans.append((at, nblk))
        at += nblk
    assert at == (gm if mode == "tn" else gk)

    def within(t, span):
        return (t >= span[0]) & (t < span[0] + span[1])

    def local(t, span):
        return jnp.clip(t - span[0], 0, span[1] - 1)

    a_specs = []
    for sp in spans:
        if mode == "tn":
            a_specs.append(pl.BlockSpec((tk, tm), lambda i, j, kk, sp=sp: (jnp.where(within(i, sp), kk, 0), local(i, sp))))
        else:
            a_specs.append(pl.BlockSpec((tm, tk), lambda i, j, kk, sp=sp: (i, local(kk, sp))))
    if mode == "nt":
        b_spec = pl.BlockSpec((tn, tk), lambda i, j, kk: (j + r0, kk + c0))
    else:
        b_spec = pl.BlockSpec((tk, tn), lambda i, j, kk: (kk + r0, j + c0))
    o_spec = pl.BlockSpec((tm, tn), lambda i, j, kk: (i, j))
    single = na == 1 and gk == 1 and not shard_rows
    pairs = tm // (2 * shard_rows) if shard_rows else 0
    assert not shard_rows or (pairs * 2 * shard_rows == tm and m == 8 * shard_rows)

    def body(*refs):
        a_refs, b_ref, o_ref = refs[:na], refs[na], refs[na + 1]
        if single:
            o_ref[...] = _dot(a_refs[0][...], b_ref[...], mode).astype(o_ref.dtype)
            return
        acc_ref = refs[-1]
        i, kk = pl.program_id(0), pl.program_id(2)

        @pl.when(kk == 0)
        def _():
            acc_ref[...] = jnp.zeros_like(acc_ref)

        for pa, sp in enumerate(spans):
            def add(pa=pa):
                acc_ref[...] += _dot(a_refs[pa][...], b_ref[...], mode)
            if na > 1:
                pl.when(within(i if mode == "tn" else kk, sp))(add)
            else:
                add()

        @pl.when(kk == gk - 1)
        def _():
            if not shard_rows:
                o_ref[...] = acc_ref[...].astype(o_ref.dtype)
                return
            send_ref, mine = refs[na + 2], lax.axis_index("c") == 0
            for pq in range(pairs):
                even = acc_ref[2 * shard_rows * pq:2 * shard_rows * pq + shard_rows, :]
                odd = acc_ref[2 * shard_rows * pq + shard_rows:2 * shard_rows * (pq + 1), :]
                o_ref[pq] = jnp.where(mine, even, odd).astype(o_ref.dtype)
                send_ref[pq] = jnp.where(mine, odd, even).astype(send_ref.dtype)

    if shard_rows:
        o_spec = pl.BlockSpec((pairs, shard_rows, tn), lambda i, j, kk: (i, 0, j))
    shard = jax.ShapeDtypeStruct((4, shard_rows, n), out_dtype)
    return _call_after(
        deps, na + 1, body, name=name, grid=(gm, gn, gk),
        in_specs=a_specs + [b_spec], out_specs=[o_spec, o_spec] if shard_rows else o_spec,
        out_shape=[shard, shard] if shard_rows else jax.ShapeDtypeStruct((m, n), out_dtype),
        scratch_shapes=[] if single else [pltpu.VMEM((tm, tn), F32)],
        compiler_params=_params(("parallel", "parallel", "arbitrary")),
    )(*a_parts, b)


def matmul_swiglu(name, a, w_gu_t, deps=()):
    s, d = a.shape
    tm, tn = 1024, 1408
    gn = D_FF // tn

    def body(a_ref, wg_ref, wu_ref, du_ref, dg_ref, f_ref):
        av = a_ref[...]
        g = _dot(av, wg_ref[...], "nt")
        u = _dot(av, wu_ref[...], "nt")
        sig = 1.0 / (1.0 + jnp.exp(-g))
        silu = g * sig
        du_ref[...] = silu.astype(du_ref.dtype)
        dg_ref[...] = (u * (sig + silu * (1.0 - sig))).astype(dg_ref.dtype)
        f_ref[...] = (silu * u).astype(f_ref.dtype)

    o_spec = pl.BlockSpec((tm, tn), lambda j, i: (i, j))
    return _call_after(
        deps, 3, body, name=name, grid=(gn, s // tm),
        in_specs=[pl.BlockSpec((tm, d), lambda j, i: (i, 0)), pl.BlockSpec((tn, d), lambda j, i: (j, 0)),
                  pl.BlockSpec((tn, d), lambda j, i: (j + gn, 0))],
        out_specs=[o_spec] * 3, out_shape=[jax.ShapeDtypeStruct((s, D_FF), BF16)] * 3,
        compiler_params=_params(("parallel", "parallel"), 56),
    )(a, w_gu_t, w_gu_t)


def matmul_swiglu_bwd(name, dffo, w_down, ff_du, ff_dg):
    s, d = dffo.shape
    tm, tn = 1024, 1408

    def body(a_ref, w_ref, pu_ref, pg_ref, dg_ref, du_ref):
        dff = _dot(a_ref[...], w_ref[...], "nt")
        du_ref[...] = (dff * pu_ref[...].astype(F32)).astype(du_ref.dtype)
        dg_ref[...] = (dff * pg_ref[...].astype(F32)).astype(dg_ref.dtype)

    o_spec = pl.BlockSpec((tm, tn), lambda j, i: (i, j))
    return _call(
        body, name=name, grid=(D_FF // tn, s // tm),
        in_specs=[pl.BlockSpec((tm, d), lambda j, i: (i, 0)), pl.BlockSpec((tn, d), lambda j, i: (j, 0)), o_spec, o_spec],
        out_specs=[o_spec] * 2, out_shape=[jax.ShapeDtypeStruct((s, D_FF), BF16)] * 2,
        compiler_params=_params(("parallel", "parallel"), 56),
    )(dffo, w_down, ff_du, ff_dg)


def _down(v, n):
    return pltpu.roll(v, n, 0)


def _up(v, n):
    return pltpu.roll(v, v.shape[0] - n, 0)


def _by_window(lane, v2, v4, v8, v16):
    return jnp.where(lane < POOL_GC, v2, jnp.where(lane < 2 * POOL_GC, v4, jnp.where(lane < 3 * POOL_GC, v8, v16)))


def _taps(cw_ref):
    return cw_ref[0:1, :], cw_ref[1:2, :], cw_ref[2:3, :]


def _conv_pool_forward(ext, t0, cw, bd):
    n_out = ext.shape[0] - HALO
    gb = ext[HALO:, 0:D_CONV]
    z = ext[:, D_CONV:2 * D_CONV] * ext[:, 2 * D_CONV:3 * D_CONV]
    z0, z1, z2 = z[HALO:], _down(z, 1)[HALO:], _down(z, 2)[HALO:]
    conv = cw[2] * z0 + cw[1] * z1 + cw[0] * z2
    x = ext[:, 3 * D_CONV:]
    w2 = x + _down(x, 1)
    w4 = w2 + _down(w2, 2)
    w8 = w4 + _down(w4, 4)
    w16 = w8 + _down(w8, 8)
    lane = lax.broadcasted_iota(jnp.int32, (1, D_POOL), 1)
    win = _by_window(lane, 2.0, 4.0, 8.0, 16.0)
    pos = (t0 + lax.broadcasted_iota(jnp.int32, (n_out, 1), 0) + 1).astype(F32)
    cnt = jnp.minimum(pos, win)
    d = _by_window(lane, w2, w4, w8, w16)[HALO:] / cnt - x[HALO:]
    ybp = _dot(d.astype(BF16), bd)
    return gb, z0, z1, z2, conv, d, cnt, ybp, lane


def conv_pool_fwd(name, proj_a, cw, bd, ps, gg, deps=()):
    s = proj_a.shape[0]
    t = T_CP
    hb = t // HALO

    def body(main_ref, prev_ref, cw_ref, bd_ref, ps_ref, gg_ref, y_ref):
        i = pl.program_id(0)
        prev = jnp.where(i > 0, prev_ref[...].astype(F32), 0.0)
        ext = jnp.concatenate([prev, main_ref[...].astype(F32)], axis=0)
        gb, _, _, _, conv, _, _, ybp, _ = _conv_pool_forward(ext, i * t, _taps(cw_ref), bd_ref[...])
        ya = gb * conv
        yb = ybp * ps_ref[...]
        ggv = gg_ref[...]
        y_ref[:, 0:D_CONV] = ((ya * _rms(ya)) * ggv[:, 0:D_CONV]).astype(y_ref.dtype)
        y_ref[:, D_CONV:] = ((yb * _rms(yb)) * ggv[:, D_CONV:]).astype(y_ref.dtype)

    full = lambda shape: pl.BlockSpec(shape, lambda i: (0,) * len(shape))
    return _call_after(
        deps, 6, body, name=name, grid=(s // t,),
        in_specs=[pl.BlockSpec((t, D_MODEL), lambda i: (i, 0)),
                  pl.BlockSpec((HALO, D_MODEL), lambda i: (jnp.maximum(i * hb - 1, 0), 0)),
                  full((3, D_CONV)), full((D_POOL, D_POOL)), full((1, D_POOL)), full((1, 2 * D_CONV))],
        out_specs=pl.BlockSpec((t, 2 * D_CONV), lambda i: (i, 0)),
        out_shape=jax.ShapeDtypeStruct((s, D_MODEL), BF16),
        compiler_params=_params(("parallel",)),
    )(proj_a, proj_a, cw, bd, ps, gg)


def conv_pool_bwd(name, proj_a, dy, cw, bd, ps, gg):
    s = proj_a.shape[0]
    t = T_CP
    hb = t // HALO
    nblk = s // t
    last_halo = s // HALO - 1

    def body(main_ref, prev_ref, next_ref, dy_ref, dyn_ref, cw_ref, bd_ref, ps_ref, gg_ref,
             dp_ref, dcw_ref, dbd_ref, dps_ref, dgg_ref):
        i = pl.program_id(0)
        prev = jnp.where(i > 0, prev_ref[...].astype(F32), 0.0)
        main = main_ref[...].astype(F32)
        ext = jnp.concatenate([prev, main, next_ref[...].astype(F32)], axis=0)
        cwv, bdv, psv, ggv = _taps(cw_ref), bd_ref[...], ps_ref[...], gg_ref[...]
        gb, z0, z1, z2, conv, d, cnt, ybp, lane = _conv_pool_forward(ext, i * t, cwv, bdv)
        dyn = jnp.where(i < nblk - 1, dyn_ref[...].astype(F32), 0.0)
        dyv = jnp.concatenate([dy_ref[...].astype(F32), dyn], axis=0)
        ya = gb * conv
        yb = ybp * psv
        dya, dgg_a = _norm_bwd(ya, _rms(ya), ggv[:, 0:D_CONV], dyv[:, 0:D_CONV])
        dyb, dgg_b = _norm_bwd(yb, _rms(yb), ggv[:, D_CONV:], dyv[:, D_CONV:])

        dconv = dya * gb
        dz = (cwv[2] * dconv + cwv[1] * _up(dconv, 1) + cwv[0] * _up(dconv, 2))[:t]
        dp_ref[:, 0:D_CONV] = (dya * conv)[:t].astype(dp_ref.dtype)
        dp_ref[:, D_CONV:2 * D_CONV] = (dz * main[:, 2 * D_CONV:3 * D_CONV]).astype(dp_ref.dtype)
        dp_ref[:, 2 * D_CONV:3 * D_CONV] = (dz * main[:, D_CONV:2 * D_CONV]).astype(dp_ref.dtype)

        dybs = dyb * psv
        dd = _dot(dybs.astype(BF16), bdv, "nt")
        e = dd / cnt
        a2 = e + _up(e, 1)
        a4 = a2 + _up(a2, 2)
        a8 = a4 + _up(a4, 4)
        a16 = a8 + _up(a8, 8)
        dp_ref[:, 3 * D_CONV:] = (_by_window(lane, a2, a4, a8, a16) - dd)[:t].astype(dp_ref.dtype)

        @pl.when(i == 0)
        def _():
            dcw_ref[...] = jnp.zeros_like(dcw_ref)
            dbd_ref[...] = jnp.zeros_like(dbd_ref)
            dps_ref[...] = jnp.zeros_like(dps_ref)
            dgg_ref[...] = jnp.zeros_like(dgg_ref)

        rsum = lambda v: jnp.sum(v[:t], axis=0, keepdims=True)
        dcw_ref[0:1, :] += rsum(dconv * z2)
        dcw_ref[1:2, :] += rsum(dconv * z1)
        dcw_ref[2:3, :] += rsum(dconv * z0)
        dbd_ref[...] += _dot(d[:t].astype(BF16), dybs[:t].astype(BF16), "tn")
        dps_ref[...] += rsum(dyb * ybp)
        dgg_ref[:, 0:D_CONV] += rsum(dgg_a)
        dgg_ref[:, D_CONV:] += rsum(dgg_b)

    full = lambda shape: pl.BlockSpec(shape, lambda i: (0,) * len(shape))
    next_halo = lambda i: (jnp.minimum((i + 1) * hb, last_halo), 0)
    return _call(
        body, name=name, grid=(nblk,),
        in_specs=[pl.BlockSpec((t, D_MODEL), lambda i: (i, 0)),
                  pl.BlockSpec((HALO, D_MODEL), lambda i: (jnp.maximum(i * hb - 1, 0), 0)),
                  pl.BlockSpec((HALO, D_MODEL), next_halo),
                  pl.BlockSpec((t, 2 * D_CONV), lambda i: (i, 0)),
                  pl.BlockSpec((HALO, 2 * D_CONV), next_halo),
                  full((3, D_CONV)), full((D_POOL, D_POOL)), full((1, D_POOL)), full((1, 2 * D_CONV))],
        out_specs=[pl.BlockSpec((t, D_MODEL), lambda i: (i, 0)),
                   full((3, D_CONV)), full((D_POOL, D_POOL)), full((1, D_POOL)), full((1, 2 * D_CONV))],
        out_shape=[jax.ShapeDtypeStruct((s, D_MODEL), BF16), jax.ShapeDtypeStruct((3, D_CONV), F32),
                   jax.ShapeDtypeStruct((D_POOL, D_POOL), F32), jax.ShapeDtypeStruct((1, D_POOL), F32),
                   jax.ShapeDtypeStruct((1, 2 * D_CONV), F32)],
        compiler_params=_params(("arbitrary",)),
    )(proj_a, proj_a, proj_a, dy, dy, cw, bd, ps, gg)


def bias_by_offset(rel_bias):
    n_far = 2 * QB - REL_CLIP + 1
    far = jnp.broadcast_to(rel_bias[:, 2 * REL_CLIP:], (N_HEADS, n_far))
    mid = rel_bias[:, 1:2 * REL_CLIP][:, ::-1]
    near = jnp.broadcast_to(rel_bias[:, 0:1], (N_HEADS, KB - n_far - (2 * REL_CLIP - 1)))
    wrap = jnp.broadcast_to(rel_bias[:, 2 * REL_CLIP:], (N_HEADS, 4 * QB - KB))
    return jnp.concatenate([far, mid, near, wrap], axis=1)


def _fill_bias(off_ref, b_ref):
    width = 4 * QB
    sub = lax.broadcasted_iota(jnp.int32, (8, 1), 0)
    col = lax.broadcasted_iota(jnp.int32, (1, KB), 1)
    for h in range(N_HEADS):
        base = jnp.broadcast_to(off_ref[h:h + 1, :], (8, width))
        for bit in range(3):
            base = jnp.where(((sub >> bit) & 1) == 1, pltpu.roll(base, 1 << bit, 1), base)
        for a in range(QB // 8):
            first = CHUNK * (8 * a // CHUNK)
            rows = (pltpu.roll(base, 8 * a, 1) if a else base)[:, :KB]
            band = (col >= first) & (col < first + (LEFT_CHUNKS + 1) * CHUNK)
            b_ref[h, 8 * a:8 * a + 8, :] = jnp.where(band, rows, NEG)


def rel_bias_grad(by_offset):
    n_far = 2 * QB - REL_CLIP + 1
    hi = jnp.sum(by_offset[:, :n_far], axis=1, keepdims=True) + jnp.sum(by_offset[:, KB:], axis=1, keepdims=True)
    mid = by_offset[:, n_far:n_far + 2 * REL_CLIP - 1][:, ::-1]
    lo = jnp.sum(by_offset[:, n_far + 2 * REL_CLIP - 1:KB], axis=1, keepdims=True)
    return jnp.concatenate([lo, mid, hi], axis=1)


def _head_masks():
    first = lax.broadcasted_iota(jnp.int32, (1, 2 * HEAD_DIM), 1) < HEAD_DIM
    return first, jnp.logical_not(first)


def _pick_lane(tile, h):
    lane = lax.broadcasted_iota(jnp.int32, (1, tile.shape[1]), 1)
    return jnp.sum(jnp.where(lane == h, tile, 0.0), axis=-1, keepdims=True)


def attention_fwd(name, qkv, by_offset, gg, y_ab):
    s = qkv.shape[0]
    nq = s // QB
    scale = HEAD_DIM ** -0.5

    def body(q_ref, k0, k1, k2, v0, v1, v2, off_ref, gg_ref, y_in, y_ref, o_ref, lse_ref, b_ref):
        del y_in
        i = pl.program_id(0)

        @pl.when(i == 0)
        def _():
            _fill_bias(off_ref, b_ref)
        def block(at_start):
            kb = jnp.concatenate([k0[...], k1[...], k2[...]], axis=0)
            vb = jnp.concatenate([v0[...], v1[...], v2[...]], axis=0)
            valid = lax.broadcasted_iota(jnp.int32, (1, KB), 1) >= (2 - i) * QB
            lane = lax.broadcasted_iota(jnp.int32, (1, 128), 1)
            masks = _head_masks()
            lse = jnp.zeros((QB, 128), F32)
            outs = []
            for hp in range(N_HEADS // 2):
                sl = slice(2 * HEAD_DIM * hp, 2 * HEAD_DIM * (hp + 1))
                q_p, k_p, v_p = q_ref[:, sl] * scale, kb[:, sl], vb[:, sl]
                o_pair = jnp.zeros((QB, 2 * HEAD_DIM), F32)
                for a in range(2):
                    h = 2 * hp + a
                    sc = _dot(jnp.where(masks[a], q_p, 0), k_p, "nt") + b_ref[h]
                    if at_start:
                        sc = jnp.where(valid, sc, NEG)
                    mx = jnp.max(sc, axis=-1, keepdims=True)
                    e = jnp.exp(sc - mx).astype(BF16)
                    ones_at = HEAD_DIM * (1 - a)
                    ev = _dot(e, jnp.where(masks[a], v_p, jnp.where(lane == ones_at, 1, 0).astype(v_p.dtype)))
                    l = jnp.sum(jnp.where(lane == ones_at, ev, 0.0), axis=-1, keepdims=True)
                    o_pair = o_pair + jnp.where(masks[a], ev, 0.0) * (1.0 / l)
                    lse = jnp.where(lane == h, mx + jnp.log(l), lse)
                outs.append(o_pair)
            o = jnp.concatenate(outs, axis=1)
            o_ref[...] = o.astype(o_ref.dtype)
            lse_ref[...] = lse
            y_ref[...] = ((o * _rms(o)) * gg_ref[...]).astype(y_ref.dtype)

        pl.when(i < 2)(lambda: block(True))
        pl.when(i >= 2)(lambda: block(False))

    blk = lambda col, back: pl.BlockSpec((QB, D_ATTN), lambda i: (jnp.maximum(i - back, 0), QKV_COL + col))
    return _call(
        body, name=name, grid=(nq,),
        in_specs=[blk(0, 0), blk(1, 2), blk(1, 1), blk(1, 0), blk(2, 2), blk(2, 1), blk(2, 0),
                  pl.BlockSpec((N_HEADS, 4 * QB), lambda i: (0, 0)), pl.BlockSpec((1, D_ATTN), lambda i: (0, 0)),
                  pl.BlockSpec(memory_space=pl.ANY)],
        out_specs=[pl.BlockSpec((QB, D_ATTN), lambda i: (i, 1)), pl.BlockSpec((QB, D_ATTN), lambda i: (i, 0)),
                   pl.BlockSpec((QB, 128), lambda i: (i, 0)), pl.BlockSpec((N_HEADS, QB, KB), lambda i: (0, 0, 0))],
        out_shape=[jax.ShapeDtypeStruct((s, D_MODEL), BF16), jax.ShapeDtypeStruct((s, D_ATTN), BF16),
                   jax.ShapeDtypeStruct((s, 128), F32), jax.ShapeDtypeStruct((N_HEADS, QB, KB), F32)],
        input_output_aliases={9: 0},
        compiler_params=_params(("arbitrary",), 56),
    )(qkv, qkv, qkv, qkv, qkv, qkv, qkv, by_offset, gg, y_ab)


def attention_bwd(name, qkv, o, lse, dy, bias, gg):
    s = qkv.shape[0]
    nq = s // QB
    scale = HEAD_DIM ** -0.5
    width = 4 * QB

    def body(q_ref, k0, k1, k2, v0, v1, v2, o_ref, lse_ref, dy_ref, b_ref, gg_ref,
             dq_ref, dk_ref, dv_ref, off_ref, dgg_ref, dk_acc, dv_acc, db_acc):
        i = pl.program_id(0)

        @pl.when(i == 0)
        def _():
            dk_acc[...] = jnp.zeros_like(dk_acc)
            dv_acc[...] = jnp.zeros_like(dv_acc)
            db_acc[...] = jnp.zeros_like(db_acc)
            dgg_ref[...] = jnp.zeros_like(dgg_ref)

        rows_of = lambda j: pl.ds(pl.multiple_of((j % 3) * QB, QB), QB)

        @pl.when(i > 0)
        def _():
            dk_acc[rows_of(i), :] = jnp.zeros((QB, D_ATTN), F32)
            dv_acc[rows_of(i), :] = jnp.zeros((QB, D_ATTN), F32)

        def block(at_start):
            ov = o_ref[...].astype(F32)
            dyv = dy_ref[...].astype(F32)
            do, dgg_t = _norm_bwd(ov, _rms(ov), gg_ref[...], dyv)
            dgg_ref[...] += jnp.sum(dgg_t, axis=0, keepdims=True)
            kb = jnp.concatenate([k0[...], k1[...], k2[...]], axis=0)
            vb = jnp.concatenate([v0[...], v1[...], v2[...]], axis=0)
            valid = lax.broadcasted_iota(jnp.int32, (1, KB), 1) >= (2 - i) * QB
            masks = _head_masks()
            lse_t = lse_ref[...]
            for hp in range(N_HEADS // 2):
                sl = slice(2 * HEAD_DIM * hp, 2 * HEAD_DIM * (hp + 1))
                q_p, k_p, v_p = q_ref[:, sl] * scale, kb[:, sl], vb[:, sl]
                do_p = do[:, sl]
                prod = do_p * ov[:, sl]
                do_b = do_p.astype(BF16)
                dq_pair = jnp.zeros((QB, 2 * HEAD_DIM), F32)
                dk_pair = jnp.zeros((KB, 2 * HEAD_DIM), F32)
                dv_pair = jnp.zeros((KB, 2 * HEAD_DIM), F32)
                for a in range(2):
                    h = 2 * hp + a
                    q_m = jnp.where(masks[a], q_p, 0)
                    do_m = jnp.where(masks[a], do_b, 0)
                    sc = _dot(q_m, k_p, "nt") + b_ref[h]
                    if at_start:
                        sc = jnp.where(valid, sc, NEG)
                    p = jnp.exp(sc - _pick_lane(lse_t, h))
                    dp = _dot(do_m, v_p, "nt")
                    delta = jnp.sum(jnp.where(masks[a], prod, 0.0), axis=-1, keepdims=True)
                    ds = p * (dp - delta)
                    db_acc[h] += ds
                    ds_b = ds.astype(BF16)
                    dq_pair = dq_pair + _dot(ds_b, jnp.where(masks[a], k_p, 0))
                    dk_pair = dk_pair + _dot(ds_b, q_m, "tn")
                    dv_pair = dv_pair + _dot(p.astype(BF16), do_m, "tn")
                dq_ref[:, sl] = (dq_pair * scale).astype(dq_ref.dtype)
                for t in range(3):
                    dk_acc[rows_of(i + 1 + t), sl] += dk_pair[QB * t:QB * (t + 1)]
                    dv_acc[rows_of(i + 1 + t), sl] += dv_pair[QB * t:QB * (t + 1)]

        pl.when(i < 2)(lambda: block(True))
        pl.when((i >= 2) & (i < nq))(lambda: block(False))

        dk_ref[...] = dk_acc[rows_of(i + 1), :].astype(dk_ref.dtype)
        dv_ref[...] = dv_acc[rows_of(i + 1), :].astype(dv_ref.dtype)

        @pl.when(i == nq + 1)
        def _():
            sub = lax.broadcasted_iota(jnp.int32, (8, 1), 0)
            pad = jnp.zeros((8, width - KB), F32)
            for h in range(N_HEADS):
                v = jnp.concatenate([db_acc[h, 0:8, :], pad], axis=1)
                for a in range(1, QB // 8):
                    grp = jnp.concatenate([db_acc[h, 8 * a:8 * a + 8, :], pad], axis=1)
                    v = v + pltpu.roll(grp, width - 8 * a, 1)
                for bit in range(3):
                    v = jnp.where(((sub >> bit) & 1) == 1, pltpu.roll(v, width - (1 << bit), 1), v)
                off_ref[h:h + 1, :] = jnp.sum(v, axis=0, keepdims=True)

    qi = lambda i: jnp.minimum(i, nq - 1)
    kblk = lambda col, back: pl.BlockSpec((QB, D_ATTN), lambda i: (jnp.clip(i - back, 0, nq - 1), QKV_COL + col))
    qblk = lambda col: pl.BlockSpec((QB, D_ATTN), lambda i: (qi(i), col))
    done = pl.BlockSpec((QB, D_ATTN), lambda i: (jnp.clip(i - 2, 0, nq - 1), 0))
    return _call(
        body, name=name, grid=(nq + 2,),
        in_specs=[qblk(QKV_COL), kblk(1, 2), kblk(1, 1), kblk(1, 0), kblk(2, 2), kblk(2, 1), kblk(2, 0),
                  qblk(0), pl.BlockSpec((QB, 128), lambda i: (qi(i), 0)), qblk(1),
                  pl.BlockSpec((N_HEADS, QB, KB), lambda i: (0, 0, 0)), pl.BlockSpec((1, D_ATTN), lambda i: (0, 0))],
        out_specs=[qblk(0), done, done, pl.BlockSpec((N_HEADS, width), lambda i: (0, 0)),
                   pl.BlockSpec((1, D_ATTN), lambda i: (0, 0))],
        out_shape=[jax.ShapeDtypeStruct((s, D_ATTN), BF16)] * 3
        + [jax.ShapeDtypeStruct((N_HEADS, width), F32), jax.ShapeDtypeStruct((1, D_ATTN), F32)],
        scratch_shapes=[pltpu.VMEM((KB, D_ATTN), F32), pltpu.VMEM((KB, D_ATTN), F32), pltpu.VMEM((N_HEADS, QB, KB), F32)],
        compiler_params=_params(("arbitrary",), 56),
    )(qkv, qkv, qkv, qkv, qkv, qkv, qkv, o, lse, dy, bias, gg)


def _block_diag(pw):
    out = jnp.zeros((D_POOL, D_POOL), pw.dtype)
    for gi in range(N_POOL):
        out = lax.dynamic_update_slice(out, pw[gi], (gi * POOL_GC, gi * POOL_GC))
    return out


_SMALL = ("pool_w", "pool_scale", "rel_bias", "group_gain", "pre_mix_g", "post_mix_g", "pre_ffn_g", "post_ffn_g")


def _pack(parts, rows):
    flat = jnp.concatenate([p.reshape(-1).astype(F32) for p in parts])
    return jnp.pad(flat, (0, rows * D_MODEL - flat.shape[0])).reshape(rows, D_MODEL)


def _unpack(packed, shapes):
    flat = packed.reshape(-1)
    out, at = [], 0
    for shp in shapes:
        size = int(np.prod(shp))
        out.append(flat[at:at + size].reshape(shp))
        at += size
    return out


def kernel(x, w_in, w_out, conv_w, pool_w, pool_scale, rel_bias, group_gain, pre_mix_g, post_mix_g, pre_ffn_g, post_ffn_g, w_gate_up, w_down, loss_target, m_w_in, m_w_out, m_conv_w, m_pool_w, m_pool_scale, m_rel_bias, m_group_gain, m_pre_mix_g, m_post_mix_g, m_pre_ffn_g, m_post_ffn_g, m_w_gate_up, m_w_down, v_w_in, v_w_out, v_conv_w, v_pool_w, v_pool_scale, v_rel_bias, v_group_gain, v_pre_mix_g, v_post_mix_g, v_pre_ffn_g, v_post_ffn_g, v_w_gate_up, v_w_down):
    depth = w_in.shape[0]
    s = x.shape[1]
    c_me = lax.axis_index("c")
    q_me = 2 * lax.axis_index("x") + lax.axis_index("y")
    dev = 2 * q_me + c_me

    tr = lambda a: jnp.swapaxes(a, 1, 2)
    w_in_t, w_gu_t = tr(w_in), tr(w_gate_up)

    def group_a(l):
        return [w_in_t[l].astype(BF16), conv_w[l]]

    def group_b(l):
        return [w_out[l].astype(BF16), w_gu_t[l].astype(BF16), w_down[l].astype(BF16)]

    def weights_a(parts):
        g_in, g_cw = parts
        return assemble_rows("assemble_in", *g_in), assemble_cols("assemble_conv", *g_cw)

    def weights_b(parts):
        g_out, g_gu, g_dn = parts
        return (assemble_rows("assemble_out", *g_out), assemble_rows("assemble_gu", *g_gu),
                assemble_rows("assemble_down", *g_dn))

    h = x.reshape(s, D_MODEL)
    start_a = gather_start("gather_a0_s", group_a(0))
    start_b = gather_start("gather_b0_s", group_b(0), [start_a["token"]])
    xn = rmsnorm_fwd("norm_mix", h, pre_mix_g[0].reshape(1, -1), BF16)
    relay_a = gather_relay("gather_a0_r", start_a, start_b["token"])
    wa = weights_a(gather_finish("gather_a0_f", relay_a, xn))
    wb = None
    ahead = [start_b["token"]]
    saved = []
    for l in range(depth):
        vec = lambda p: p[l].reshape(1, -1)
        wt_in, cw_full = wa
        if l + 1 < depth:
            start_next = gather_start(f"gather_{l + 1}_s", group_a(l + 1) + group_b(l + 1), ahead)
            ahead = [start_next["token"]]
        proj = matmul("proj", xn, wt_in, "nt", BF16, s, 5 * D_ATTN, D_MODEL, 1024, 5 * D_ATTN, D_MODEL, deps=ahead)
        if l == 0:
            relay_b = gather_relay("gather_b0_r", start_b, proj)
            ahead = [relay_b["token"]]
        bd = _block_diag(pool_w[l]).astype(BF16)
        gg = vec(group_gain)
        y_ab = conv_pool_fwd("conv_pool_fwd", proj, cw_full, bd, vec(pool_scale), gg[:, :2 * D_CONV], deps=ahead)
        y, o, lse, bias = attention_fwd("attention_fwd", proj, bias_by_offset(rel_bias[l]), gg[:, 2 * D_CONV:], y_ab)
        if l == 0:
            wb = weights_b(gather_finish("gather_b0_f", relay_b, y))
        wf_out, wt_gu, wf_dn = wb
        if l + 1 < depth:
            relay_next = gather_relay(f"gather_{l + 1}_r", start_next, y)
        mix, h_mid, hn = matmul_norm("mix_out", y, wf_out, h, vec(post_mix_g), then=(vec(pre_ffn_g), BF16), tm=1024,
                                     deps=[relay_next["token"]] if l + 1 < depth else [])
        ff_du, ff_dg, ff = matmul_swiglu("gate_up", hn, wt_gu)
        if l + 1 < depth:
            parts = gather_finish(f"gather_{l + 1}_f", relay_next, ff)
            wa, wb = weights_a(parts[:2]), weights_b(parts[2:])
            ffo, h_out, xn_next = matmul_norm("ffn_down", ff, wf_dn, h_mid, vec(post_ffn_g),
                                              then=(pre_mix_g[l + 1].reshape(1, -1), BF16))
        else:
            ffo, h_out = matmul_norm("ffn_down_last", ff, wf_dn, h_mid, vec(post_ffn_g))
            xn_next = None
        saved.append((h, xn, proj, None, bd, bias, y, o, lse, mix, h_mid, hn, ff_du, ff_dg, ff, ffo,
                      wt_in, wf_out, cw_full, wt_gu, wf_dn))
        h, xn = h_out, xn_next

    dh, loss_part, dffo, dg_last = loss_head("loss_head", h, loss_target.reshape(s, D_MODEL), saved[-1][15],
                                             post_ffn_g[depth - 1].reshape(1, -1))

    def reduce_begin(name, halves):
        return [k for k, _ in halves], exchange_start(name, plan_swap_cores, len(halves), [snd for _, snd in halves],
                                                      [(k.shape, k.dtype) for k, _ in halves])

    def reduce_relay(name, kept, swapped, after):
        _, got = exchange_wait(name + "_w", swapped, after)
        sums = [add_pairs("reduce_add", k, r) for k, r in zip(kept, got)]
        return exchange_start(name + "_s", plan_to_chips, 3 * len(sums), sums, [(a.shape, a.dtype) for a in sums])

    def reduce_finish(name, relayed, after):
        sums, got = exchange_wait(name, relayed, after)
        return list(zip(sums, got))

    small = {k: [None] * depth for k in _SMALL + ("conv_w",)}
    relayed = {}
    carried = []
    small["post_ffn_g"][depth - 1] = dg_last
    for l in reversed(range(depth)):
        vec = lambda p: p[l].reshape(1, -1)
        (h_in, xn, proj, _, bd, bias, y, o, lse, mix, h_mid, hn, ff_du, ff_dg, ff, ffo,
         wt_in, wf_out, cw_full, wt_gu, wf_dn) = saved[l]
        gg = vec(group_gain)
        tk_w = min(s, 2048)
        gw_dn = matmul("wgrad_down", ff, dffo, "tn", BF16, D_FF, D_MODEL, s, 1408, D_MODEL, tk_w,
                       shard_rows=D_FF // 8, deps=carried)
        dg, du = matmul_swiglu_bwd("dgrad_down", dffo, wf_dn, ff_du, ff_dg)
        gw_gu = matmul("wgrad_gu", [dg, du], hn, "tn", BF16, 2 * D_FF, D_MODEL, s, 1408, D_MODEL, tk_w,
                       shard_rows=2 * D_FF // 8)
        early = [gw_gu, gw_dn] if l == 0 else []
        if early:
            kept_b, swap_b = reduce_begin("reduce_b0_s", early)
            carried = [swap_b["token"]]
        dh_mid, dmix, small["pre_ffn_g"][l], small["post_mix_g"][l] = matmul_norm_bwd(
            "dgrad_gu", [dg, du], wt_gu, 256, h_mid, vec(pre_ffn_g), dh, then=(mix, vec(post_mix_g), BF16),
            deps=carried)
        if early:
            relayed[l, "b"] = reduce_relay("reduce_b0_r", kept_b, swap_b, dmix)
            carried = [relayed[l, "b"]["token"]]
        gw_out = matmul("wgrad_out", y, dmix, "tn", BF16, D_MODEL, D_MODEL, s, D_MODEL, D_MODEL, tk_w,
                        shard_rows=D_MODEL // 8, deps=carried)
        dy = matmul("dgrad_out", dmix, wf_out, "nt", BF16, s, D_MODEL, D_MODEL, 1024, D_MODEL, D_MODEL)
        dpa, dcw, dbd, dps, dgg_ab = conv_pool_bwd("conv_pool_bwd", proj, dy, cw_full, bd, vec(pool_scale),
                                                   gg[:, :2 * D_CONV])
        dq, dk, dv, by_off, dgg_c = attention_bwd("attention_bwd", proj, o, lse, dy, bias, gg[:, 2 * D_CONV:])
        dparts = [dpa, dq, dk, dv]
        gw_in = matmul("wgrad_in", dparts, xn, "tn", BF16, 5 * D_ATTN, D_MODEL, s, 512, D_MODEL, min(s, 2048))
        late = [split_rows("split_in", gw_in), gw_out] + ([] if early else [gw_gu, gw_dn])
        kept_a, swap_a = reduce_begin(f"reduce_a{l}_s", late)
        if l > 0:
            dh, dffo, small["pre_mix_g"][l], small["post_ffn_g"][l - 1] = matmul_norm_bwd(
                "dgrad_in", dparts, wt_in, 512, h_in, vec(pre_mix_g), dh_mid,
                then=(saved[l - 1][15], post_ffn_g[l - 1].reshape(1, -1), BF16), chunk=32, deps=[swap_a["token"]])
        else:
            dh, small["pre_mix_g"][l] = matmul_norm_bwd("dgrad_in_first", dparts, wt_in, 512, h_in, vec(pre_mix_g),
                                                        dh_mid, chunk=32, deps=[swap_a["token"]])
        relayed[l, "a"] = reduce_relay(f"reduce_a{l}_r", kept_a, swap_a, dh)
        carried = [relayed[l, "a"]["token"]]
        small["conv_w"][l] = dcw
        small["pool_w"][l] = jnp.stack([dbd[gi * POOL_GC:(gi + 1) * POOL_GC, gi * POOL_GC:(gi + 1) * POOL_GC]
                                        for gi in range(N_POOL)])
        small["pool_scale"][l] = dps
        small["rel_bias"][l] = rel_bias_grad(by_off)
        small["group_gain"][l] = jnp.concatenate([dgg_ab, dgg_c], axis=1)
    grad_x = dh.reshape(x.shape)

    small_params = dict(pool_w=pool_w, pool_scale=pool_scale, rel_bias=rel_bias, group_gain=group_gain,
                        pre_mix_g=pre_mix_g, post_mix_g=post_mix_g, pre_ffn_g=pre_ffn_g, post_ffn_g=post_ffn_g)
    small_m = dict(pool_w=m_pool_w, pool_scale=m_pool_scale, rel_bias=m_rel_bias, group_gain=m_group_gain,
                   pre_mix_g=m_pre_mix_g, post_mix_g=m_post_mix_g, pre_ffn_g=m_pre_ffn_g, post_ffn_g=m_post_ffn_g)
    small_v = dict(pool_w=v_pool_w, pool_scale=v_pool_scale, rel_bias=v_rel_bias, group_gain=v_group_gain,
                   pre_mix_g=v_pre_mix_g, post_mix_g=v_post_mix_g, pre_ffn_g=v_pre_ffn_g, post_ffn_g=v_post_ffn_g)
    shapes = [small_params[k].shape for k in _SMALL] + [(depth, 3, D_CONV), (1,)]
    n_small = sum(int(np.prod(shp)) for shp in shapes)
    rows = -(-n_small // (8 * D_MODEL)) * 8
    extra = [jnp.zeros((depth, 3, D_CONV), F32), jnp.zeros((1,), F32)]
    grads_packed = _pack([jnp.stack(small[k]) for k in _SMALL] + [jnp.stack(small["conv_w"]), loss_part[0, 0:1]], rows)
    small_started = gather_start("gather_small_s", [grads_packed], carried)

    big = dict(w_in=(w_in_t, tr(m_w_in), tr(v_w_in)), w_out=(w_out, m_w_out, v_w_out),
               w_gate_up=(w_gu_t, tr(m_w_gate_up), tr(v_w_gate_up)), w_down=(w_down, m_w_down, v_w_down))
    results = dict(w_in=None, w_out=None, w_gate_up=None, w_down=None)
    after = small_started["token"]
    small_relay = None
    for l in reversed(range(depth)):
        if (l, "b") in relayed:
            slabs_gu, slabs_dn = reduce_finish(f"reduce_b{l}_f", relayed[l, "b"], after)
            slabs_in, slabs_out = reduce_finish(f"reduce_a{l}_f", relayed[l, "a"], slabs_dn[1])
        else:
            slabs_in, slabs_out, slabs_gu, slabs_dn = reduce_finish(f"reduce_a{l}_f", relayed[l, "a"], after)
        for k, slabs in (("w_gate_up", slabs_gu), ("w_down", slabs_dn), ("w_in", slabs_in), ("w_out", slabs_out)):
            results[k] = adamw_layer("adamw_" + k, l, *big[k], *slabs, results[k])
        after = results["w_out"][0]
        if l == min(1, depth - 1):
            small_relay = gather_relay("gather_small_r", small_started, after)
    small_parts = gather_finish("gather_small_f", small_relay, after)[0]
    all_small = assemble_rows("assemble_small", *small_parts).reshape(8, rows, D_MODEL)
    res_small = adamw("adamw_small", _pack([small_params[k] for k in _SMALL] + extra, rows),
                      _pack([small_m[k] for k in _SMALL] + extra, rows),
                      _pack([small_v[k] for k in _SMALL] + extra, rows), all_small)
    g_s, d_s, m_s, v_s = (_unpack(r, shapes) for r in res_small)
    loss = g_s[-1][0]
    g_conv = lax.dynamic_slice_in_dim(g_s[-2], dev * conv_w.shape[2], conv_w.shape[2], axis=2)
    results["conv_w"] = adamw("adamw_conv", conv_w, m_conv_w, v_conv_w, g_conv[None])

    names = ("w_in", "w_out", "conv_w") + _SMALL + ("w_gate_up", "w_down")
    for k in ("w_in", "w_gate_up"):
        results[k] = tuple(tr(r) for r in results[k])
    for j, k in enumerate(_SMALL):
        results[k] = (g_s[j], d_s[j], m_s[j], v_s[j])
    return (loss, grad_x, *[results[k][0] for k in names], *[results[k][1] for k in names],
            *[results[k][2] for k in names], *[results[k][3] for k in names])
```

```python
import numpy as np
import jax
import jax.numpy as jnp
from jax import lax
from jax.experimental import pallas as pl
from jax.experimental.pallas import tpu as pltpu

F32 = jnp.float32
BF16 = jnp.bfloat16

CHUNK = 64
D_MODEL = 1024
D_CONV = 256
D_POOL = 256
D_ATTN = 512
HEAD_DIM = 64
N_HEADS = 8
N_POOL = 4
POOL_GC = 64
LEFT_CHUNKS = 8
REL_CLIP = 128
D_FF = 2816
EPS = 1e-6
ADAM_LR, ADAM_B1, ADAM_B2, ADAM_EPS, ADAM_WD, ADAM_STEP = 0.001, 0.9, 0.999, 1e-08, 0.01, 10

QB = 256
KB = 3 * QB
HALO = 16
T_CP = 512
T_ROW = 512
QKV_COL = 2
NEG = -1e30
VMEM_MB = 1 << 20
MESH = pl.DeviceIdType.MESH


def _call(body, **kw):
    call = pl.pallas_call(body, **kw)
    return lambda *args: call(*[_in_hbm(a) for a in args])


def _in_hbm(a):
    return pltpu.with_memory_space_constraint(a, pltpu.HBM) if jnp.issubdtype(a.dtype, jnp.number) else a


def _call_after(deps, n_in, body, **kw):
    deps = tuple(deps)
    if not deps:
        return _call(body, **kw)

    def ordered(*refs):
        body(*refs[:n_in], *refs[n_in + len(deps):])

    kw["in_specs"] = list(kw["in_specs"]) + [pl.BlockSpec(memory_space=pl.ANY)] * len(deps)
    call = _call(ordered, **kw)
    return lambda *args: call(*args, *deps)


def _params(sem, vmem_mb=48):
    return pltpu.CompilerParams(dimension_semantics=sem, vmem_limit_bytes=vmem_mb * VMEM_MB)


_CHIP_FLIPS = ((1, 0), (0, 1), (1, 1))
_HBM = pl.BlockSpec(memory_space=pltpu.HBM)
_SEM = pl.BlockSpec(memory_space=pltpu.SEMAPHORE)
_EFFECT = pltpu.SideEffectType.DATAFLOW_SIDE_EFFECTING


def _flip(v, f):
    return 1 - v if f else v


def _descriptors(plan, srcs, lands, send_sems, recv_sems, sending):
    x, y, c = lax.axis_index("x"), lax.axis_index("y"), lax.axis_index("c")
    return [pltpu.make_async_remote_copy(src_ref=src, dst_ref=dst if sending else land, send_sem=send_sems.at[k],
                                         recv_sem=recv_sems.at[k], device_id=peer, device_id_type=MESH)
            for k, (src, dst, peer, land) in enumerate(plan(srcs, lands, x, y, c))]


def exchange_start(name, plan, n_copies, srcs, land_shapes, deps=()):
    ns, nl = len(srcs), len(land_shapes)

    def body(*refs):
        src_refs, land_refs = refs[:ns], refs[ns:ns + nl]
        send_sems, recv_sems = refs[ns + nl], refs[ns + nl + 1]
        for send in _descriptors(plan, src_refs, land_refs, send_sems, recv_sems, True):
            send.start()
        refs[-1][...] = jnp.zeros_like(refs[-1])

    lands = [lax.empty(shape, dtype) for shape, dtype in land_shapes]
    outs = _call_after(
        deps, ns + nl, body, name=name,
        out_shape=(pltpu.SemaphoreType.DMA((n_copies,)), pltpu.SemaphoreType.DMA((n_copies,)),
                   *[pltpu.HBM(a.shape, a.dtype) for a in srcs], *[pltpu.HBM(shape, dtype) for shape, dtype in land_shapes],
                   jax.ShapeDtypeStruct((8, 128), F32)),
        in_specs=[_HBM] * (ns + nl),
        out_specs=(_SEM, _SEM, *[_HBM] * (ns + nl), pl.BlockSpec(memory_space=pltpu.VMEM)),
        input_output_aliases={j: 2 + j for j in range(ns + nl)},
        compiler_params=pltpu.CompilerParams(has_side_effects=_EFFECT),
    )(*srcs, *lands)
    return dict(plan=plan, sems=outs[:2], srcs=outs[2:2 + ns], lands=outs[2 + ns:2 + ns + nl], token=outs[-1])


def exchange_wait(name, started, after):
    srcs, lands = started["srcs"], started["lands"]
    ns, nl = len(srcs), len(lands)

    def body(*refs):
        src_refs, land_refs = refs[:ns], refs[ns:ns + nl]
        send_sems, recv_sems = refs[ns + nl], refs[ns + nl + 1]
        for wait in _descriptors(started["plan"], src_refs, land_refs, send_sems, recv_sems, False):
            wait.wait_send()
            wait.wait_recv()

    outs = _call(
        body, name=name,
        out_shape=tuple(pltpu.HBM(a.shape, a.dtype) for a in (*srcs, *lands)),
        in_specs=[_HBM] * (ns + nl) + [_SEM, _SEM, pl.BlockSpec(memory_space=pl.ANY)],
        out_specs=tuple([_HBM] * (ns + nl)),
        input_output_aliases={j: j for j in range(ns + nl)},
        compiler_params=pltpu.CompilerParams(has_side_effects=_EFFECT),
    )(*srcs, *lands, *started["sems"], after)
    return list(outs[:ns]), list(outs[ns:])


def plan_from_chips(srcs, lands, x, y, c):
    q = 2 * x + y
    out = []
    for src, land in zip(srcs, lands):
        for fx, fy in _CHIP_FLIPS:
            px, py = _flip(x, fx), _flip(y, fy)
            out.append((src, land.at[q], (px, py, c), land.at[2 * px + py]))
    return out


def plan_to_core(srcs, lands, x, y, c):
    n = len(lands)
    q = 2 * x + y
    out = []
    for own, chips, land in zip(srcs[:n], srcs[n:], lands):
        out.append((own, land.at[q], (x, y, 1 - c), land.at[q]))
        for fx, fy in _CHIP_FLIPS:
            qp = 2 * _flip(x, fx) + _flip(y, fy)
            out.append((chips.at[qp], land.at[qp], (x, y, 1 - c), land.at[qp]))
    return out


def plan_swap_cores(srcs, lands, x, y, c):
    return [(src, land, (x, y, 1 - c), land) for src, land in zip(srcs, lands)]


def plan_to_chips(srcs, lands, x, y, c):
    q = 2 * x + y
    out = []
    for src, land in zip(srcs, lands):
        for fx, fy in _CHIP_FLIPS:
            px, py = _flip(x, fx), _flip(y, fy)
            out.append((src.at[2 * px + py], land.at[q], (px, py, c), land.at[2 * px + py]))
    return out


def _slots(a):
    return ((4,) + a.shape, a.dtype)


def gather_start(name, arrays, deps=()):
    return exchange_start(name, plan_from_chips, 3 * len(arrays), arrays, [_slots(a) for a in arrays], deps)


def gather_relay(name, started, after):
    own, chips = exchange_wait(name + "_w", started, after)
    return exchange_start(name + "_s", plan_to_core, 4 * len(own), own + chips, [_slots(a) for a in own])


def gather_finish(name, relayed, after):
    srcs, cores = exchange_wait(name, relayed, after)
    n = len(cores)
    return list(zip(srcs[:n], srcs[n:], cores))


def _place():
    return 2 * lax.axis_index("x") + lax.axis_index("y"), lax.axis_index("c")


def assemble_cols(name, own, chips, core):
    r, c = own.shape
    t = _row_tile(r, 256)

    def body(own_ref, chips_ref, core_ref, o_ref):
        q_me, c_me = _place()
        for q in range(4):
            mine = jnp.where(q == q_me, own_ref[...], chips_ref[q])
            other = core_ref[q]
            o_ref[:, 2 * q * c:(2 * q + 1) * c] = jnp.where(c_me == 0, mine, other)
            o_ref[:, (2 * q + 1) * c:(2 * q + 2) * c] = jnp.where(c_me == 0, other, mine)

    slots = pl.BlockSpec((4, t, c), lambda i: (0, i, 0))
    return _call(
        body, name=name, grid=(r // t,),
        in_specs=[pl.BlockSpec((t, c), lambda i: (i, 0)), slots, slots],
        out_specs=pl.BlockSpec((t, 8 * c), lambda i: (i, 0)),
        out_shape=jax.ShapeDtypeStruct((r, 8 * c), own.dtype), compiler_params=_params(("parallel",)),
    )(own, chips, core)


def assemble_rows(name, own, chips, core):
    r, c = own.shape

    def body(own_ref, chips_ref, core_ref, o_ref):
        q_me, c_me = _place()
        d = pl.program_id(0)
        mine = jnp.where(d // 2 == q_me, own_ref[...], chips_ref[...])
        o_ref[...] = jnp.where(d % 2 == c_me, mine, core_ref[...])

    slot = pl.BlockSpec((None, r, c), lambda d: (d // 2, 0, 0))
    return _call(
        body, name=name, grid=(8,),
        in_specs=[pl.BlockSpec((r, c), lambda d: (0, 0)), slot, slot],
        out_specs=pl.BlockSpec((r, c), lambda d: (d, 0)),
        out_shape=jax.ShapeDtypeStruct((8 * r, c), own.dtype), compiler_params=_params(("parallel",)),
    )(own, chips, core)


def split_rows(name, dw):
    r8, c = dw.shape
    r = r8 // 8

    def body(dw_ref, keep_ref, send_ref):
        _, c_me = _place()
        d = pl.program_id(0)

        @pl.when(d % 2 == c_me)
        def _():
            keep_ref[...] = dw_ref[...]

        @pl.when(d % 2 != c_me)
        def _():
            send_ref[...] = dw_ref[...]

    slot = pl.BlockSpec((None, r, c), lambda d: (d // 2, 0, 0))
    return _call(
        body, name=name, grid=(8,),
        in_specs=[pl.BlockSpec((r, c), lambda d: (d, 0))], out_specs=[slot, slot],
        out_shape=[jax.ShapeDtypeStruct((4, r, c), dw.dtype)] * 2, compiler_params=_params(("arbitrary",)),
    )(dw)


def _rms(v):
    return lax.rsqrt(jnp.mean(v * v, axis=-1, keepdims=True) + EPS)


def rmsnorm_fwd(name, x, g, out_dtype):
    s, d = x.shape

    def body(x_ref, g_ref, o_ref):
        xv = x_ref[...]
        o_ref[...] = ((xv * _rms(xv)) * g_ref[...]).astype(o_ref.dtype)

    row = pl.BlockSpec((T_ROW, d), lambda i: (i, 0))
    return _call(
        body, name=name, grid=(s // T_ROW,),
        in_specs=[row, pl.BlockSpec((1, d), lambda i: (0, 0))], out_specs=row,
        out_shape=jax.ShapeDtypeStruct((s, d), out_dtype), compiler_params=_params(("parallel",)),
    )(x, g)


def matmul_then(name, a_parts, b, mode, tm, chunk, rows_in, vecs_in, rows_out, n_sums, then, deps=()):
    a_parts = list(a_parts)
    na, nr, nv, no = len(a_parts), len(rows_in), len(vecs_in), len(rows_out)
    s = a_parts[0].shape[0]
    n = s // tm
    d = b.shape[1] if mode == "nn" else b.shape[0]
    offs = [0]
    for p in a_parts:
        offs.append(offs[-1] + p.shape[1])
    n_in = na + 1 + nr + nv

    def body(*refs):
        a_refs, b_ref = refs[:na], refs[na]
        row_refs, vec_refs = refs[na + 1:na + 1 + nr], refs[na + 1 + nr:n_in]
        out_refs, sum_refs = refs[n_in:n_in + no], refs[n_in + no:n_in + no + n_sums]
        acc = refs[-2:]
        i = pl.program_id(0)

        @pl.when(i == 0)
        def _():
            acc[1][...] = jnp.zeros_like(acc[1])
            for s_ref in sum_refs:
                s_ref[...] = jnp.zeros_like(s_ref)

        def step(write, read):
            t = None
            for p in range(na):
                lo, hi = offs[p], offs[p + 1]
                part = _dot(a_refs[p][...], b_ref[lo:hi, :] if mode == "nn" else b_ref[:, lo:hi], mode)
                t = part if t is None else t + part
            write[...] = t
            totals = [0.0] * n_sums
            for r0 in range(0, tm, chunk):
                rs = pl.ds(r0, chunk)
                results, sums = then(read[rs, :], [r.at[rs, :] for r in row_refs], vec_refs)
                for o_ref, val in zip(out_refs, results):
                    o_ref[rs, :] = val.astype(o_ref.dtype)
                totals = [tot + val for tot, val in zip(totals, sums)]
            for s_ref, val in zip(sum_refs, totals):
                s_ref[...] += jnp.where(i > 0, val, 0.0)

        pl.when(i % 2 == 0)(lambda: step(acc[0], acc[1]))
        pl.when(i % 2 == 1)(lambda: step(acc[1], acc[0]))

    ahead = lambda i: (jnp.minimum(i, n - 1), 0)
    behind = lambda i: (jnp.maximum(i - 1, 0), 0)
    fixed = lambda i: (0, 0)
    shapes = [jax.ShapeDtypeStruct((s, d), dt) for dt in rows_out] + [jax.ShapeDtypeStruct((1, d), F32)] * n_sums
    return _call_after(
        deps, n_in, body, name=name, grid=(n + 1,),
        in_specs=[pl.BlockSpec((tm, p.shape[1]), ahead) for p in a_parts] + [pl.BlockSpec(b.shape, fixed)]
        + [pl.BlockSpec((tm, r.shape[1]), behind) for r in rows_in] + [pl.BlockSpec(v.shape, fixed) for v in vecs_in],
        out_specs=[pl.BlockSpec((tm, d), behind)] * no + [pl.BlockSpec((1, d), fixed)] * n_sums,
        out_shape=shapes, scratch_shapes=[pltpu.VMEM((tm, d), F32)] * 2,
        compiler_params=_params(("arbitrary",), 56),
    )(*a_parts, b, *rows_in, *vecs_in)


def matmul_norm(name, a, b, res, g, then=None, tm=512, deps=()):
    def norm(z, rows, vecs):
        y = rows[0][...] + (z * _rms(z)) * vecs[0][...]
        return [z, y] + ([(y * _rms(y)) * vecs[1][...]] if then else []), []

    return matmul_then(name, [a], b, "nn", tm, tm, [res], [g] + ([then[0]] if then else []),
                       [BF16, F32] + ([then[1]] if then else []), 0, norm, deps)


def matmul_norm_bwd(name, a_parts, b, tm, x, g, res, then=None, chunk=None, deps=()):
    def norms(t, rows, vecs):
        xv = rows[0][...].astype(F32)
        dx, dgt = _norm_bwd(xv, _rms(xv), vecs[0][...], t)
        dx = dx + rows[1][...]
        results, sums = [dx], [jnp.sum(dgt, axis=0, keepdims=True)]
        if then:
            x2 = rows[2][...].astype(F32)
            dx2, dgt2 = _norm_bwd(x2, _rms(x2), vecs[1][...], dx)
            results.append(dx2)
            sums.append(jnp.sum(dgt2, axis=0, keepdims=True))
        return results, sums

    return matmul_then(name, a_parts, b, "nn", tm, chunk or tm, [x, res] + ([then[0]] if then else []),
                       [g] + ([then[1]] if then else []), [F32] + ([then[2]] if then else []),
                       2 if then else 1, norms, deps)


def _norm_bwd(xv, r, g, dy):
    a = dy * g
    dx = r * (a - xv * ((r * r) * jnp.mean(a * xv, axis=-1, keepdims=True)))
    return dx, dy * (xv * r)


def loss_head(name, h, tgt, z, g):
    s, d = h.shape

    def body(h_ref, t_ref, z_ref, g_ref, dh_ref, l_ref, dz_ref, dg_ref):
        e = h_ref[...] - t_ref[...]
        dh = e * (1.0 / d)
        dh_ref[...] = dh
        zv = z_ref[...].astype(F32)
        dz, dgt = _norm_bwd(zv, _rms(zv), g_ref[...], dh)
        dz_ref[...] = dz.astype(dz_ref.dtype)

        @pl.when(pl.program_id(0) == 0)
        def _():
            l_ref[...] = jnp.zeros_like(l_ref)
            dg_ref[...] = jnp.zeros_like(dg_ref)
        part = 0.5 * jnp.sum(jnp.mean(e * e, axis=-1, keepdims=True), axis=0, keepdims=True)
        l_ref[...] += jnp.broadcast_to(part, l_ref.shape)
        dg_ref[...] += jnp.sum(dgt, axis=0, keepdims=True)

    row = pl.BlockSpec((T_ROW, d), lambda i: (i, 0))
    vec = pl.BlockSpec((1, d), lambda i: (0, 0))
    return _call(
        body, name=name, grid=(s // T_ROW,),
        in_specs=[row, row, row, vec], out_specs=[row, pl.BlockSpec((1, 128), lambda i: (0, 0)), row, vec],
        out_shape=[jax.ShapeDtypeStruct((s, d), F32), jax.ShapeDtypeStruct((1, 128), F32),
                   jax.ShapeDtypeStruct((s, d), BF16), jax.ShapeDtypeStruct((1, d), F32)],
        compiler_params=_params(("arbitrary",)),
    )(h, tgt, z, g)


def _row_tile(rows, limit=512):
    t = min(rows, limit)
    while rows % t or (t % 8 and t != rows):
        t -= 1
    return t


def add_pairs(name, a, b):
    shape = a.shape
    a2, b2 = a.reshape(-1, shape[-1]), b.reshape(-1, shape[-1])
    rows, cols = a2.shape
    t = _row_tile(rows)

    def body(a_ref, b_ref, o_ref):
        o_ref[...] = (a_ref[...].astype(F32) + b_ref[...].astype(F32)).astype(o_ref.dtype)

    blk = pl.BlockSpec((t, cols), lambda i: (i, 0))
    out = _call(
        body, name=name, grid=(rows // t,), in_specs=[blk, blk], out_specs=blk,
        out_shape=jax.ShapeDtypeStruct((rows, cols), a.dtype), compiler_params=_params(("parallel",)),
    )(a2, b2)
    return out.reshape(shape)


def _adamw_math(w, m, v, g):
    c1 = 1.0 - ADAM_B1 ** ADAM_STEP
    c2 = 1.0 - ADAM_B2 ** ADAM_STEP
    mn = ADAM_B1 * m + (1.0 - ADAM_B1) * g
    vn = ADAM_B2 * v + (1.0 - ADAM_B2) * (g * g)
    delta = -ADAM_LR * ((mn / c1) / (jnp.sqrt(vn / c2) + ADAM_EPS) + ADAM_WD * w)
    return delta, mn, vn


def _slab_sum(g_ref, n):
    g = g_ref[0].astype(F32)
    for j in range(1, n):
        g = g + g_ref[j].astype(F32)
    return g


def adamw(name, w, m, v, g_slabs, deps=()):
    shape = w.shape
    n = g_slabs.shape[0]
    w2, m2, v2 = (t.reshape(-1, shape[-1]) for t in (w, m, v))
    g3 = g_slabs.reshape(n, -1, shape[-1])
    rows, cols = w2.shape
    t = _row_tile(rows, 256)

    def body(w_ref, m_ref, v_ref, g_ref, go_ref, d_ref, mo_ref, vo_ref):
        g = _slab_sum(g_ref, n)
        go_ref[...] = g
        d_ref[...], mo_ref[...], vo_ref[...] = _adamw_math(w_ref[...], m_ref[...], v_ref[...], g)

    blk = pl.BlockSpec((t, cols), lambda i: (i, 0))
    outs = _call_after(
        deps, 4, body, name=name, grid=(rows // t,),
        in_specs=[blk, blk, blk, pl.BlockSpec((n, t, cols), lambda i: (0, i, 0))], out_specs=[blk] * 4,
        out_shape=[jax.ShapeDtypeStruct((rows, cols), F32)] * 4, compiler_params=_params(("parallel",)),
    )(w2, m2, v2, g3)
    return tuple(o.reshape(shape) for o in outs)


def adamw_layer(name, l, w, m, v, own_sums, chip_sums, into):
    _, rows, cols = w.shape
    t = _row_tile(rows, 256)
    if into is None:
        into = tuple(lax.empty(w.shape, F32) for _ in range(4))

    def body(w_ref, m_ref, v_ref, own_ref, far_ref, a0, a1, a2, a3, go_ref, d_ref, mo_ref, vo_ref):
        del a0, a1, a2, a3
        q_me, _ = _place()
        g = jnp.zeros((t, cols), F32)
        for q in range(4):
            g = g + jnp.where(q == q_me, own_ref[q], far_ref[q]).astype(F32)
        go_ref[...] = g
        d_ref[...], mo_ref[...], vo_ref[...] = _adamw_math(w_ref[...], m_ref[...], v_ref[...], g)

    blk = pl.BlockSpec((None, t, cols), lambda i: (l, i, 0))
    slabs = pl.BlockSpec((4, t, cols), lambda i: (0, i, 0))
    anyw = pl.BlockSpec(memory_space=pl.ANY)
    return _call(
        body, name=name, grid=(rows // t,),
        in_specs=[blk, blk, blk, slabs, slabs] + [anyw] * 4, out_specs=[blk] * 4,
        out_shape=[jax.ShapeDtypeStruct(w.shape, F32)] * 4, input_output_aliases={5: 0, 6: 1, 7: 2, 8: 3},
        compiler_params=_params(("parallel",)),
    )(w, m, v, own_sums, chip_sums, *into)


_DIMS = {"nn": (((1,), (0,)), ((), ())), "nt": (((1,), (1,)), ((), ())), "tn": (((0,), (0,)), ((), ()))}


def _dot(a, b, mode="nn"):
    return lax.dot_general(a, b, _DIMS[mode], preferred_element_type=F32)


def matmul(name, a, b, mode, out_dtype, m, n, k, tm, tn, tk, b_off=(0, 0), shard_rows=0, deps=()):
    gm, gn, gk = m // tm, n // tn, k // tk
    assert gm * tm == m and gn * tn == n and gk * tk == k
    r0, c0 = b_off
    a_parts = list(a) if isinstance(a, (list, tuple)) else [a]
    na = len(a_parts)
    tile = tm if mode == "tn" else tk
    spans, at = [], 0
    for p in a_parts:
        nblk = p.shape[1] // tile
        assert nblk * tile == p.shape[1]
        spans.append((at, nblk))
        at += nblk
    assert at == (gm if mode == "tn" else gk)

    def within(t, span):
        return (t >= span[0]) & (t < span[0] + span[1])

    def local(t, span):
        return jnp.clip(t - span[0], 0, span[1] - 1)

    a_specs = []
    for sp in spans:
        if mode == "tn":
            a_specs.append(pl.BlockSpec((tk, tm), lambda i, j, kk, sp=sp: (jnp.where(within(i, sp), kk, 0), local(i, sp))))
        else:
            a_specs.append(pl.BlockSpec((tm, tk), lambda i, j, kk, sp=sp: (i, local(kk, sp))))
    if mode == "nt":
        b_spec = pl.BlockSpec((tn, tk), lambda i, j, kk: (j + r0, kk + c0))
    else:
        b_spec = pl.BlockSpec((tk, tn), lambda i, j, kk: (kk + r0, j + c0))
    o_spec = pl.BlockSpec((tm, tn), lambda i, j, kk: (i, j))
    single = na == 1 and gk == 1 and not shard_rows
    pairs = tm // (2 * shard_rows) if shard_rows else 0
    assert not shard_rows or (pairs * 2 * shard_rows == tm and m == 8 * shard_rows)

    def body(*refs):
        a_refs, b_ref, o_ref = refs[:na], refs[na], refs[na + 1]
        if single:
            o_ref[...] = _dot(a_refs[0][...], b_ref[...], mode).astype(o_ref.dtype)
            return
        acc_ref = refs[-1]
        i, kk = pl.program_id(0), pl.program_id(2)

        @pl.when(kk == 0)
        def _():
            acc_ref[...] = jnp.zeros_like(acc_ref)

        for pa, sp in enumerate(spans):
            def add(pa=pa):
                acc_ref[...] += _dot(a_refs[pa][...], b_ref[...], mode)
            if na > 1:
                pl.when(within(i if mode == "tn" else kk, sp))(add)
            else:
                add()

        @pl.when(kk == gk - 1)
        def _():
            if not shard_rows:
                o_ref[...] = acc_ref[...].astype(o_ref.dtype)
                return
            send_ref, mine = refs[na + 2], lax.axis_index("c") == 0
            for pq in range(pairs):
                even = acc_ref[2 * shard_rows * pq:2 * shard_rows * pq + shard_rows, :]
                odd = acc_ref[2 * shard_rows * pq + shard_rows:2 * shard_rows * (pq + 1), :]
                o_ref[pq] = jnp.where(mine, even, odd).astype(o_ref.dtype)
                send_ref[pq] = jnp.where(mine, odd, even).astype(send_ref.dtype)

    if shard_rows:
        o_spec = pl.BlockSpec((pairs, shard_rows, tn), lambda i, j, kk: (i, 0, j))
    shard = jax.ShapeDtypeStruct((4, shard_rows, n), out_dtype)
    return _call_after(
        deps, na + 1, body, name=name, grid=(gm, gn, gk),
        in_specs=a_specs + [b_spec], out_specs=[o_spec, o_spec] if shard_rows else o_spec,
        out_shape=[shard, shard] if shard_rows else jax.ShapeDtypeStruct((m, n), out_dtype),
        scratch_shapes=[] if single else [pltpu.VMEM((tm, tn), F32)],
        compiler_params=_params(("parallel", "parallel", "arbitrary")),
    )(*a_parts, b)


def matmul_swiglu(name, a, w_gu_t, deps=()):
    s, d = a.shape
    tm, tn = 1024, 1408
    gn = D_FF // tn

    def body(a_ref, wg_ref, wu_ref, du_ref, dg_ref, f_ref):
        av = a_ref[...]
        g = _dot(av, wg_ref[...], "nt")
        u = _dot(av, wu_ref[...], "nt")
        sig = 1.0 / (1.0 + jnp.exp(-g))
        silu = g * sig
        du_ref[...] = silu.astype(du_ref.dtype)
        dg_ref[...] = (u * (sig + silu * (1.0 - sig))).astype(dg_ref.dtype)
        f_ref[...] = (silu * u).astype(f_ref.dtype)

    o_spec = pl.BlockSpec((tm, tn), lambda j, i: (i, j))
    return _call_after(
        deps, 3, body, name=name, grid=(gn, s // tm),
        in_specs=[pl.BlockSpec((tm, d), lambda j, i: (i, 0)), pl.BlockSpec((tn, d), lambda j, i: (j, 0)),
                  pl.BlockSpec((tn, d), lambda j, i: (j + gn, 0))],
        out_specs=[o_spec] * 3, out_shape=[jax.ShapeDtypeStruct((s, D_FF), BF16)] * 3,
        compiler_params=_params(("parallel", "parallel"), 56),
    )(a, w_gu_t, w_gu_t)


def matmul_swiglu_bwd(name, dffo, w_down, ff_du, ff_dg):
    s, d = dffo.shape
    tm, tn = 1024, 1408

    def body(a_ref, w_ref, pu_ref, pg_ref, dg_ref, du_ref):
        dff = _dot(a_ref[...], w_ref[...], "nt")
        du_ref[...] = (dff * pu_ref[...].astype(F32)).astype(du_ref.dtype)
        dg_ref[...] = (dff * pg_ref[...].astype(F32)).astype(dg_ref.dtype)

    o_spec = pl.BlockSpec((tm, tn), lambda j, i: (i, j))
    return _call(
        body, name=name, grid=(D_FF // tn, s // tm),
        in_specs=[pl.BlockSpec((tm, d), lambda j, i: (i, 0)), pl.BlockSpec((tn, d), lambda j, i: (j, 0)), o_spec, o_spec],
        out_specs=[o_spec] * 2, out_shape=[jax.ShapeDtypeStruct((s, D_FF), BF16)] * 2,
        compiler_params=_params(("parallel", "parallel"), 56),
    )(dffo, w_down, ff_du, ff_dg)


def _down(v, n):
    return pltpu.roll(v, n, 0)


def _up(v, n):
    return pltpu.roll(v, v.shape[0] - n, 0)


def _by_window(lane, v2, v4, v8, v16):
    return jnp.where(lane < POOL_GC, v2, jnp.where(lane < 2 * POOL_GC, v4, jnp.where(lane < 3 * POOL_GC, v8, v16)))


def _taps(cw_ref):
    return cw_ref[0:1, :], cw_ref[1:2, :], cw_ref[2:3, :]


def _conv_pool_forward(ext, t0, cw, bd):
    n_out = ext.shape[0] - HALO
    gb = ext[HALO:, 0:D_CONV]
    z = ext[:, D_CONV:2 * D_CONV] * ext[:, 2 * D_CONV:3 * D_CONV]
    z0, z1, z2 = z[HALO:], _down(z, 1)[HALO:], _down(z, 2)[HALO:]
    conv = cw[2] * z0 + cw[1] * z1 + cw[0] * z2
    x = ext[:, 3 * D_CONV:]
    w2 = x + _down(x, 1)
    w4 = w2 + _down(w2, 2)
    w8 = w4 + _down(w4, 4)
    w16 = w8 + _down(w8, 8)
    lane = lax.broadcasted_iota(jnp.int32, (1, D_POOL), 1)
    win = _by_window(lane, 2.0, 4.0, 8.0, 16.0)
    pos = (t0 + lax.broadcasted_iota(jnp.int32, (n_out, 1), 0) + 1).astype(F32)
    cnt = jnp.minimum(pos, win)
    d = _by_window(lane, w2, w4, w8, w16)[HALO:] / cnt - x[HALO:]
    ybp = _dot(d.astype(BF16), bd)
    return gb, z0, z1, z2, conv, d, cnt, ybp, lane


def conv_pool_fwd(name, proj_a, cw, bd, ps, gg, deps=()):
    s = proj_a.shape[0]
    t = T_CP
    hb = t // HALO

    def body(main_ref, prev_ref, cw_ref, bd_ref, ps_ref, gg_ref, y_ref):
        i = pl.program_id(0)
        prev = jnp.where(i > 0, prev_ref[...].astype(F32), 0.0)
        ext = jnp.concatenate([prev, main_ref[...].astype(F32)], axis=0)
        gb, _, _, _, conv, _, _, ybp, _ = _conv_pool_forward(ext, i * t, _taps(cw_ref), bd_ref[...])
        ya = gb * conv
        yb = ybp * ps_ref[...]
        ggv = gg_ref[...]
        y_ref[:, 0:D_CONV] = ((ya * _rms(ya)) * ggv[:, 0:D_CONV]).astype(y_ref.dtype)
        y_ref[:, D_CONV:] = ((yb * _rms(yb)) * ggv[:, D_CONV:]).astype(y_ref.dtype)

    full = lambda shape: pl.BlockSpec(shape, lambda i: (0,) * len(shape))
    return _call_after(
        deps, 6, body, name=name, grid=(s // t,),
        in_specs=[pl.BlockSpec((t, D_MODEL), lambda i: (i, 0)),
                  pl.BlockSpec((HALO, D_MODEL), lambda i: (jnp.maximum(i * hb - 1, 0), 0)),
                  full((3, D_CONV)), full((D_POOL, D_POOL)), full((1, D_POOL)), full((1, 2 * D_CONV))],
        out_specs=pl.BlockSpec((t, 2 * D_CONV), lambda i: (i, 0)),
        out_shape=jax.ShapeDtypeStruct((s, D_MODEL), BF16),
        compiler_params=_params(("parallel",)),
    )(proj_a, proj_a, cw, bd, ps, gg)


def conv_pool_bwd(name, proj_a, dy, cw, bd, ps, gg):
    s = proj_a.shape[0]
    t = T_CP
    hb = t // HALO
    nblk = s // t
    last_halo = s // HALO - 1

    def body(main_ref, prev_ref, next_ref, dy_ref, dyn_ref, cw_ref, bd_ref, ps_ref, gg_ref,
             dp_ref, dcw_ref, dbd_ref, dps_ref, dgg_ref):
        i = pl.program_id(0)
        prev = jnp.where(i > 0, prev_ref[...].astype(F32), 0.0)
        main = main_ref[...].astype(F32)
        ext = jnp.concatenate([prev, main, next_ref[...].astype(F32)], axis=0)
        cwv, bdv, psv, ggv = _taps(cw_ref), bd_ref[...], ps_ref[...], gg_ref[...]
        gb, z0, z1, z2, conv, d, cnt, ybp, lane = _conv_pool_forward(ext, i * t, cwv, bdv)
        dyn = jnp.where(i < nblk - 1, dyn_ref[...].astype(F32), 0.0)
        dyv = jnp.concatenate([dy_ref[...].astype(F32), dyn], axis=0)
        ya = gb * conv
        yb = ybp * psv
        dya, dgg_a = _norm_bwd(ya, _rms(ya), ggv[:, 0:D_CONV], dyv[:, 0:D_CONV])
        dyb, dgg_b = _norm_bwd(yb, _rms(yb), ggv[:, D_CONV:], dyv[:, D_CONV:])

        dconv = dya * gb
        dz = (cwv[2] * dconv + cwv[1] * _up(dconv, 1) + cwv[0] * _up(dconv, 2))[:t]
        dp_ref[:, 0:D_CONV] = (dya * conv)[:t].astype(dp_ref.dtype)
        dp_ref[:, D_CONV:2 * D_CONV] = (dz * main[:, 2 * D_CONV:3 * D_CONV]).astype(dp_ref.dtype)
        dp_ref[:, 2 * D_CONV:3 * D_CONV] = (dz * main[:, D_CONV:2 * D_CONV]).astype(dp_ref.dtype)

        dybs = dyb * psv
        dd = _dot(dybs.astype(BF16), bdv, "nt")
        e = dd / cnt
        a2 = e + _up(e, 1)
        a4 = a2 + _up(a2, 2)
        a8 = a4 + _up(a4, 4)
        a16 = a8 + _up(a8, 8)
        dp_ref[:, 3 * D_CONV:] = (_by_window(lane, a2, a4, a8, a16) - dd)[:t].astype(dp_ref.dtype)

        @pl.when(i == 0)
        def _():
            dcw_ref[...] = jnp.zeros_like(dcw_ref)
            dbd_ref[...] = jnp.zeros_like(dbd_ref)
            dps_ref[...] = jnp.zeros_like(dps_ref)
            dgg_ref[...] = jnp.zeros_like(dgg_ref)

        rsum = lambda v: jnp.sum(v[:t], axis=0, keepdims=True)
        dcw_ref[0:1, :] += rsum(dconv * z2)
        dcw_ref[1:2, :] += rsum(dconv * z1)
        dcw_ref[2:3, :] += rsum(dconv * z0)
        dbd_ref[...] += _dot(d[:t].astype(BF16), dybs[:t].astype(BF16), "tn")
        dps_ref[...] += rsum(dyb * ybp)
        dgg_ref[:, 0:D_CONV] += rsum(dgg_a)
        dgg_ref[:, D_CONV:] += rsum(dgg_b)

    full = lambda shape: pl.BlockSpec(shape, lambda i: (0,) * len(shape))
    next_halo = lambda i: (jnp.minimum((i + 1) * hb, last_halo), 0)
    return _call(
        body, name=name, grid=(nblk,),
        in_specs=[pl.BlockSpec((t, D_MODEL), lambda i: (i, 0)),
                  pl.BlockSpec((HALO, D_MODEL), lambda i: (jnp.maximum(i * hb - 1, 0), 0)),
                  pl.BlockSpec((HALO, D_MODEL), next_halo),
                  pl.BlockSpec((t, 2 * D_CONV), lambda i: (i, 0)),
                  pl.BlockSpec((HALO, 2 * D_CONV), next_halo),
                  full((3, D_CONV)), full((D_POOL, D_POOL)), full((1, D_POOL)), full((1, 2 * D_CONV))],
        out_specs=[pl.BlockSpec((t, D_MODEL), lambda i: (i, 0)),
                   full((3, D_CONV)), full((D_POOL, D_POOL)), full((1, D_POOL)), full((1, 2 * D_CONV))],
        out_shape=[jax.ShapeDtypeStruct((s, D_MODEL), BF16), jax.ShapeDtypeStruct((3, D_CONV), F32),
                   jax.ShapeDtypeStruct((D_POOL, D_POOL), F32), jax.ShapeDtypeStruct((1, D_POOL), F32),
                   jax.ShapeDtypeStruct((1, 2 * D_CONV), F32)],
        compiler_params=_params(("arbitrary",)),
    )(proj_a, proj_a, proj_a, dy, dy, cw, bd, ps, gg)


def bias_by_offset(rel_bias):
    n_far = 2 * QB - REL_CLIP + 1
    far = jnp.broadcast_to(rel_bias[:, 2 * REL_CLIP:], (N_HEADS, n_far))
    mid = rel_bias[:, 1:2 * REL_CLIP][:, ::-1]
    near = jnp.broadcast_to(rel_bias[:, 0:1], (N_HEADS, KB - n_far - (2 * REL_CLIP - 1)))
    wrap = jnp.broadcast_to(rel_bias[:, 2 * REL_CLIP:], (N_HEADS, 4 * QB - KB))
    return jnp.concatenate([far, mid, near, wrap], axis=1)


def _fill_bias(off_ref, b_ref):
    width = 4 * QB
    sub = lax.broadcasted_iota(jnp.int32, (8, 1), 0)
    col = lax.broadcasted_iota(jnp.int32, (1, KB), 1)
    for h in range(N_HEADS):
        base = jnp.broadcast_to(off_ref[h:h + 1, :], (8, width))
        for bit in range(3):
            base = jnp.where(((sub >> bit) & 1) == 1, pltpu.roll(base, 1 << bit, 1), base)
        for a in range(QB // 8):
            first = CHUNK * (8 * a // CHUNK)
            rows = (pltpu.roll(base, 8 * a, 1) if a else base)[:, :KB]
            band = (col >= first) & (col < first + (LEFT_CHUNKS + 1) * CHUNK)
            b_ref[h, 8 * a:8 * a + 8, :] = jnp.where(band, rows, NEG)


def rel_bias_grad(by_offset):
    n_far = 2 * QB - REL_CLIP + 1
    hi = jnp.sum(by_offset[:, :n_far], axis=1, keepdims=True) + jnp.sum(by_offset[:, KB:], axis=1, keepdims=True)
    mid = by_offset[:, n_far:n_far + 2 * REL_CLIP - 1][:, ::-1]
    lo = jnp.sum(by_offset[:, n_far + 2 * REL_CLIP - 1:KB], axis=1, keepdims=True)
    return jnp.concatenate([lo, mid, hi], axis=1)


def _head_masks():
    first = lax.broadcasted_iota(jnp.int32, (1, 2 * HEAD_DIM), 1) < HEAD_DIM
    return first, jnp.logical_not(first)


def _pick_lane(tile, h):
    lane = lax.broadcasted_iota(jnp.int32, (1, tile.shape[1]), 1)
    return jnp.sum(jnp.where(lane == h, tile, 0.0), axis=-1, keepdims=True)


def attention_fwd(name, qkv, by_offset, gg, y_ab, deps=()):
    s = qkv.shape[0]
    nq = s // QB
    scale = HEAD_DIM ** -0.5

    def body(q_ref, k0, k1, k2, v0, v1, v2, off_ref, gg_ref, y_in, y_ref, o_ref, lse_ref, b_ref):
        del y_in
        i = pl.program_id(0)

        @pl.when(i == 0)
        def _():
            _fill_bias(off_ref, b_ref)
        def block(at_start):
            kb = jnp.concatenate([k0[...], k1[...], k2[...]], axis=0)
            vb = jnp.concatenate([v0[...], v1[...], v2[...]], axis=0)
            valid = lax.broadcasted_iota(jnp.int32, (1, KB), 1) >= (2 - i) * QB
            lane = lax.broadcasted_iota(jnp.int32, (1, 128), 1)
            masks = _head_masks()
            lse = jnp.zeros((QB, 128), F32)
            outs = []
            for hp in range(N_HEADS // 2):
                sl = slice(2 * HEAD_DIM * hp, 2 * HEAD_DIM * (hp + 1))
                q_p, k_p, v_p = q_ref[:, sl] * scale, kb[:, sl], vb[:, sl]
                o_pair = jnp.zeros((QB, 2 * HEAD_DIM), F32)
                for a in range(2):
                    h = 2 * hp + a
                    sc = _dot(jnp.where(masks[a], q_p, 0), k_p, "nt") + b_ref[h]
                    if at_start:
                        sc = jnp.where(valid, sc, NEG)
                    mx = jnp.max(sc, axis=-1, keepdims=True)
                    e = jnp.exp(sc - mx).astype(BF16)
                    ones_at = HEAD_DIM * (1 - a)
                    ev = _dot(e, jnp.where(masks[a], v_p, jnp.where(lane == ones_at, 1, 0).astype(v_p.dtype)))
                    l = jnp.sum(jnp.where(lane == ones_at, ev, 0.0), axis=-1, keepdims=True)
                    o_pair = o_pair + jnp.where(masks[a], ev, 0.0) * (1.0 / l)
                    lse = jnp.where(lane == h, mx + jnp.log(l), lse)
                outs.append(o_pair)
            o = jnp.concatenate(outs, axis=1)
            o_ref[...] = o.astype(o_ref.dtype)
            lse_ref[...] = lse
            y_ref[...] = ((o * _rms(o)) * gg_ref[...]).astype(y_ref.dtype)

        pl.when(i < 2)(lambda: block(True))
        pl.when(i >= 2)(lambda: block(False))

    blk = lambda col, back: pl.BlockSpec((QB, D_ATTN), lambda i: (jnp.maximum(i - back, 0), QKV_COL + col))
    return _call_after(
        deps, 10, body, name=name, grid=(nq,),
        in_specs=[blk(0, 0), blk(1, 2), blk(1, 1), blk(1, 0), blk(2, 2), blk(2, 1), blk(2, 0),
                  pl.BlockSpec((N_HEADS, 4 * QB), lambda i: (0, 0)), pl.BlockSpec((1, D_ATTN), lambda i: (0, 0)),
                  pl.BlockSpec(memory_space=pl.ANY)],
        out_specs=[pl.BlockSpec((QB, D_ATTN), lambda i: (i, 1)), pl.BlockSpec((QB, D_ATTN), lambda i: (i, 0)),
                   pl.BlockSpec((QB, 128), lambda i: (i, 0)), pl.BlockSpec((N_HEADS, QB, KB), lambda i: (0, 0, 0))],
        out_shape=[jax.ShapeDtypeStruct((s, D_MODEL), BF16), jax.ShapeDtypeStruct((s, D_ATTN), BF16),
                   jax.ShapeDtypeStruct((s, 128), F32), jax.ShapeDtypeStruct((N_HEADS, QB, KB), F32)],
        input_output_aliases={9: 0},
        compiler_params=_params(("arbitrary",), 56),
    )(qkv, qkv, qkv, qkv, qkv, qkv, qkv, by_offset, gg, y_ab)


def attention_bwd(name, qkv, o, lse, dy, bias, gg):
    s = qkv.shape[0]
    nq = s // QB
    scale = HEAD_DIM ** -0.5
    width = 4 * QB

    def body(q_ref, k0, k1, k2, v0, v1, v2, o_ref, lse_ref, dy_ref, b_ref, gg_ref,
             dq_ref, dk_ref, dv_ref, off_ref, dgg_ref, dk_acc, dv_acc, db_acc):
        i = pl.program_id(0)

        @pl.when(i == 0)
        def _():
            dk_acc[...] = jnp.zeros_like(dk_acc)
            dv_acc[...] = jnp.zeros_like(dv_acc)
            db_acc[...] = jnp.zeros_like(db_acc)
            dgg_ref[...] = jnp.zeros_like(dgg_ref)

        rows_of = lambda j: pl.ds(pl.multiple_of((j % 3) * QB, QB), QB)

        @pl.when(i > 0)
        def _():
            dk_acc[rows_of(i), :] = jnp.zeros((QB, D_ATTN), F32)
            dv_acc[rows_of(i), :] = jnp.zeros((QB, D_ATTN), F32)

        def block(at_start):
            ov = o_ref[...].astype(F32)
            dyv = dy_ref[...].astype(F32)
            do, dgg_t = _norm_bwd(ov, _rms(ov), gg_ref[...], dyv)
            dgg_ref[...] += jnp.sum(dgg_t, axis=0, keepdims=True)
            kb = jnp.concatenate([k0[...], k1[...], k2[...]], axis=0)
            vb = jnp.concatenate([v0[...], v1[...], v2[...]], axis=0)
            valid = lax.broadcasted_iota(jnp.int32, (1, KB), 1) >= (2 - i) * QB
            masks = _head_masks()
            lse_t = lse_ref[...]
            for hp in range(N_HEADS // 2):
                sl = slice(2 * HEAD_DIM * hp, 2 * HEAD_DIM * (hp + 1))
                q_p, k_p, v_p = q_ref[:, sl] * scale, kb[:, sl], vb[:, sl]
                do_p = do[:, sl]
                prod = do_p * ov[:, sl]
                do_b = do_p.astype(BF16)
                dq_pair = jnp.zeros((QB, 2 * HEAD_DIM), F32)
                dk_pair = jnp.zeros((KB, 2 * HEAD_DIM), F32)
                dv_pair = jnp.zeros((KB, 2 * HEAD_DIM), F32)
                for a in range(2):
                    h = 2 * hp + a
                    q_m = jnp.where(masks[a], q_p, 0)
                    do_m = jnp.where(masks[a], do_b, 0)
                    sc = _dot(q_m, k_p, "nt") + b_ref[h]
                    if at_start:
                        sc = jnp.where(valid, sc, NEG)
                    p = jnp.exp(sc - _pick_lane(lse_t, h))
                    dp = _dot(do_m, v_p, "nt")
                    delta = jnp.sum(jnp.where(masks[a], prod, 0.0), axis=-1, keepdims=True)
                    ds = p * (dp - delta)
                    db_acc[h] += ds
                    ds_b = ds.astype(BF16)
                    dq_pair = dq_pair + _dot(ds_b, jnp.where(masks[a], k_p, 0))
                    dk_pair = dk_pair + _dot(ds_b, q_m, "tn")
                    dv_pair = dv_pair + _dot(p.astype(BF16), do_m, "tn")
                dq_ref[:, sl] = (dq_pair * scale).astype(dq_ref.dtype)
                for t in range(3):
                    dk_acc[rows_of(i + 1 + t), sl] += dk_pair[QB * t:QB * (t + 1)]
                    dv_acc[rows_of(i + 1 + t), sl] += dv_pair[QB * t:QB * (t + 1)]

        pl.when(i < 2)(lambda: block(True))
        pl.when((i >= 2) & (i < nq))(lambda: block(False))

        dk_ref[...] = dk_acc[rows_of(i + 1), :].astype(dk_ref.dtype)
        dv_ref[...] = dv_acc[rows_of(i + 1), :].astype(dv_ref.dtype)

        @pl.when(i == nq + 1)
        def _():
            sub = lax.broadcasted_iota(jnp.int32, (8, 1), 0)
            pad = jnp.zeros((8, width - KB), F32)
            for h in range(N_HEADS):
                v = jnp.concatenate([db_acc[h, 0:8, :], pad], axis=1)
                for a in range(1, QB // 8):
                    grp = jnp.concatenate([db_acc[h, 8 * a:8 * a + 8, :], pad], axis=1)
                    v = v + pltpu.roll(grp, width - 8 * a, 1)
                for bit in range(3):
                    v = jnp.where(((sub >> bit) & 1) == 1, pltpu.roll(v, width - (1 << bit), 1), v)
                off_ref[h:h + 1, :] = jnp.sum(v, axis=0, keepdims=True)

    qi = lambda i: jnp.minimum(i, nq - 1)
    kblk = lambda col, back: pl.BlockSpec((QB, D_ATTN), lambda i: (jnp.clip(i - back, 0, nq - 1), QKV_COL + col))
    qblk = lambda col: pl.BlockSpec((QB, D_ATTN), lambda i: (qi(i), col))
    done = pl.BlockSpec((QB, D_ATTN), lambda i: (jnp.clip(i - 2, 0, nq - 1), 0))
    return _call(
        body, name=name, grid=(nq + 2,),
        in_specs=[qblk(QKV_COL), kblk(1, 2), kblk(1, 1), kblk(1, 0), kblk(2, 2), kblk(2, 1), kblk(2, 0),
                  qblk(0), pl.BlockSpec((QB, 128), lambda i: (qi(i), 0)), qblk(1),
                  pl.BlockSpec((N_HEADS, QB, KB), lambda i: (0, 0, 0)), pl.BlockSpec((1, D_ATTN), lambda i: (0, 0))],
        out_specs=[qblk(0), done, done, pl.BlockSpec((N_HEADS, width), lambda i: (0, 0)),
                   pl.BlockSpec((1, D_ATTN), lambda i: (0, 0))],
        out_shape=[jax.ShapeDtypeStruct((s, D_ATTN), BF16)] * 3
        + [jax.ShapeDtypeStruct((N_HEADS, width), F32), jax.ShapeDtypeStruct((1, D_ATTN), F32)],
        scratch_shapes=[pltpu.VMEM((KB, D_ATTN), F32), pltpu.VMEM((KB, D_ATTN), F32), pltpu.VMEM((N_HEADS, QB, KB), F32)],
        compiler_params=_params(("arbitrary",), 56),
    )(qkv, qkv, qkv, qkv, qkv, qkv, qkv, o, lse, dy, bias, gg)


def _block_diag(pw):
    out = jnp.zeros((D_POOL, D_POOL), pw.dtype)
    for gi in range(N_POOL):
        out = lax.dynamic_update_slice(out, pw[gi], (gi * POOL_GC, gi * POOL_GC))
    return out


_SMALL = ("pool_w", "pool_scale", "rel_bias", "group_gain", "pre_mix_g", "post_mix_g", "pre_ffn_g", "post_ffn_g")


def _pack(parts, rows):
    flat = jnp.concatenate([p.reshape(-1).astype(F32) for p in parts])
    return jnp.pad(flat, (0, rows * D_MODEL - flat.shape[0])).reshape(rows, D_MODEL)


def _unpack(packed, shapes):
    flat = packed.reshape(-1)
    out, at = [], 0
    for shp in shapes:
        size = int(np.prod(shp))
        out.append(flat[at:at + size].reshape(shp))
        at += size
    return out


def kernel(x, w_in, w_out, conv_w, pool_w, pool_scale, rel_bias, group_gain, pre_mix_g, post_mix_g, pre_ffn_g, post_ffn_g, w_gate_up, w_down, loss_target, m_w_in, m_w_out, m_conv_w, m_pool_w, m_pool_scale, m_rel_bias, m_group_gain, m_pre_mix_g, m_post_mix_g, m_pre_ffn_g, m_post_ffn_g, m_w_gate_up, m_w_down, v_w_in, v_w_out, v_conv_w, v_pool_w, v_pool_scale, v_rel_bias, v_group_gain, v_pre_mix_g, v_post_mix_g, v_pre_ffn_g, v_post_ffn_g, v_w_gate_up, v_w_down):
    depth = w_in.shape[0]
    s = x.shape[1]
    c_me = lax.axis_index("c")
    q_me = 2 * lax.axis_index("x") + lax.axis_index("y")
    dev = 2 * q_me + c_me

    tr = lambda a: jnp.swapaxes(a, 1, 2)
    w_in_t, w_gu_t = tr(w_in), tr(w_gate_up)

    def group_a(l):
        return [w_in_t[l].astype(BF16), conv_w[l]]

    def group_b(l):
        return [w_out[l].astype(BF16), w_gu_t[l].astype(BF16), w_down[l].astype(BF16)]

    def weights_a(parts):
        g_in, g_cw = parts
        return assemble_rows("assemble_in", *g_in), assemble_cols("assemble_conv", *g_cw)

    def weights_b(parts):
        g_out, g_gu, g_dn = parts
        return (assemble_rows("assemble_out", *g_out), assemble_rows("assemble_gu", *g_gu),
                assemble_rows("assemble_down", *g_dn))

    h = x.reshape(s, D_MODEL)
    start_a = gather_start("gather_a0_s", group_a(0))
    start_b = gather_start("gather_b0_s", group_b(0), [start_a["token"]])
    xn = rmsnorm_fwd("norm_mix", h, pre_mix_g[0].reshape(1, -1), BF16)
    relay_a = gather_relay("gather_a0_r", start_a, start_b["token"])
    wa = weights_a(gather_finish("gather_a0_f", relay_a, xn))
    wb = None
    ahead = [start_b["token"]]
    saved = []
    for l in range(depth):
        vec = lambda p: p[l].reshape(1, -1)
        wt_in, cw_full = wa
        if l + 1 < depth:
            start_next = gather_start(f"gather_{l + 1}_s", group_a(l + 1) + group_b(l + 1), ahead)
            ahead = [start_next["token"]]
        proj = matmul("proj", xn, wt_in, "nt", BF16, s, 5 * D_ATTN, D_MODEL, 1024, 5 * D_ATTN, D_MODEL, deps=ahead)
        bd = _block_diag(pool_w[l]).astype(BF16)
        gg = vec(group_gain)
        y_ab = conv_pool_fwd("conv_pool_fwd", proj, cw_full, bd, vec(pool_scale), gg[:, :2 * D_CONV], deps=ahead)
        if l == 0:
            relay_b = gather_relay("gather_b0_r", start_b, y_ab)
            ahead = [relay_b["token"]]
        y, o, lse, bias = attention_fwd("attention_fwd", proj, bias_by_offset(rel_bias[l]), gg[:, 2 * D_CONV:], y_ab,
                                        deps=ahead if l == 0 else [])
        if l == 0:
            wb = weights_b(gather_finish("gather_b0_f", relay_b, y))
        wf_out, wt_gu, wf_dn = wb
        if l + 1 < depth:
            relay_next = gather_relay(f"gather_{l + 1}_r", start_next, y)
        mix, h_mid, hn = matmul_norm("mix_out", y, wf_out, h, vec(post_mix_g), then=(vec(pre_ffn_g), BF16), tm=1024,
                                     deps=[relay_next["token"]] if l + 1 < depth else [])
        ff_du, ff_dg, ff = matmul_swiglu("gate_up", hn, wt_gu)
        if l + 1 < depth:
            parts = gather_finish(f"gather_{l + 1}_f", relay_next, ff)
            wa, wb = weights_a(parts[:2]), weights_b(parts[2:])
            ffo, h_out, xn_next = matmul_norm("ffn_down", ff, wf_dn, h_mid, vec(post_ffn_g),
                                              then=(pre_mix_g[l + 1].reshape(1, -1), BF16))
        else:
            ffo, h_out = matmul_norm("ffn_down_last", ff, wf_dn, h_mid, vec(post_ffn_g))
            xn_next = None
        saved.append((h, xn, proj, None, bd, bias, y, o, lse, mix, h_mid, hn, ff_du, ff_dg, ff, ffo,
                      wt_in, wf_out, cw_full, wt_gu, wf_dn))
        h, xn = h_out, xn_next

    dh, loss_part, dffo, dg_last = loss_head("loss_head", h, loss_target.reshape(s, D_MODEL), saved[-1][15],
                                             post_ffn_g[depth - 1].reshape(1, -1))

    def reduce_begin(name, halves):
        return [k for k, _ in halves], exchange_start(name, plan_swap_cores, len(halves), [snd for _, snd in halves],
                                                      [(k.shape, k.dtype) for k, _ in halves])

    def reduce_relay(name, kept, swapped, after):
        _, got = exchange_wait(name + "_w", swapped, after)
        sums = [add_pairs("reduce_add", k, r) for k, r in zip(kept, got)]
        return exchange_start(name + "_s", plan_to_chips, 3 * len(sums), sums, [(a.shape, a.dtype) for a in sums])

    def reduce_finish(name, relayed, after):
        sums, got = exchange_wait(name, relayed, after)
        return list(zip(sums, got))

    small = {k: [None] * depth for k in _SMALL + ("conv_w",)}
    relayed = {}
    carried = []
    small["post_ffn_g"][depth - 1] = dg_last
    for l in reversed(range(depth)):
        vec = lambda p: p[l].reshape(1, -1)
        (h_in, xn, proj, _, bd, bias, y, o, lse, mix, h_mid, hn, ff_du, ff_dg, ff, ffo,
         wt_in, wf_out, cw_full, wt_gu, wf_dn) = saved[l]
        gg = vec(group_gain)
        tk_w = min(s, 2048)
        gw_dn = matmul("wgrad_down", ff, dffo, "tn", BF16, D_FF, D_MODEL, s, 1408, D_MODEL, tk_w,
                       shard_rows=D_FF // 8, deps=carried)
        dg, du = matmul_swiglu_bwd("dgrad_down", dffo, wf_dn, ff_du, ff_dg)
        gw_gu = matmul("wgrad_gu", [dg, du], hn, "tn", BF16, 2 * D_FF, D_MODEL, s, 1408, D_MODEL, tk_w,
                       shard_rows=2 * D_FF // 8)
        early = [gw_gu, gw_dn] if l == 0 else []
        if early:
            kept_b, swap_b = reduce_begin("reduce_b0_s", early)
            carried = [swap_b["token"]]
        dh_mid, dmix, small["pre_ffn_g"][l], small["post_mix_g"][l] = matmul_norm_bwd(
            "dgrad_gu", [dg, du], wt_gu, 256, h_mid, vec(pre_ffn_g), dh, then=(mix, vec(post_mix_g), BF16),
            deps=carried)
        if early:
            relayed[l, "b"] = reduce_relay("reduce_b0_r", kept_b, swap_b, dmix)
            carried = [relayed[l, "b"]["token"]]
        gw_out = matmul("wgrad_out", y, dmix, "tn", BF16, D_MODEL, D_MODEL, s, D_MODEL, D_MODEL, tk_w,
                        shard_rows=D_MODEL // 8, deps=carried)
        dy = matmul("dgrad_out", dmix, wf_out, "nt", BF16, s, D_MODEL, D_MODEL, 1024, D_MODEL, D_MODEL)
        dpa, dcw, dbd, dps, dgg_ab = conv_pool_bwd("conv_pool_bwd", proj, dy, cw_full, bd, vec(pool_scale),
                                                   gg[:, :2 * D_CONV])
        dq, dk, dv, by_off, dgg_c = attention_bwd("attention_bwd", proj, o, lse, dy, bias, gg[:, 2 * D_CONV:])
        dparts = [dpa, dq, dk, dv]
        gw_in = matmul("wgrad_in", dparts, xn, "tn", BF16, 5 * D_ATTN, D_MODEL, s, 512, D_MODEL, min(s, 2048))
        late = [split_rows("split_in", gw_in), gw_out] + ([] if early else [gw_gu, gw_dn])
        kept_a, swap_a = reduce_begin(f"reduce_a{l}_s", late)
        if l > 0:
            dh, dffo, small["pre_mix_g"][l], small["post_ffn_g"][l - 1] = matmul_norm_bwd(
                "dgrad_in", dparts, wt_in, 512, h_in, vec(pre_mix_g), dh_mid,
                then=(saved[l - 1][15], post_ffn_g[l - 1].reshape(1, -1), BF16), chunk=32, deps=[swap_a["token"]])
        else:
            dh, small["pre_mix_g"][l] = matmul_norm_bwd("dgrad_in_first", dparts, wt_in, 512, h_in, vec(pre_mix_g),
                                                        dh_mid, chunk=32, deps=[swap_a["token"]])
        relayed[l, "a"] = reduce_relay(f"reduce_a{l}_r", kept_a, swap_a, dh)
        carried = [relayed[l, "a"]["token"]]
        small["conv_w"][l] = dcw
        small["pool_w"][l] = jnp.stack([dbd[gi * POOL_GC:(gi + 1) * POOL_GC, gi * POOL_GC:(gi + 1) * POOL_GC]
                                        for gi in range(N_POOL)])
        small["pool_scale"][l] = dps
        small["rel_bias"][l] = rel_bias_grad(by_off)
        small["group_gain"][l] = jnp.concatenate([dgg_ab, dgg_c], axis=1)
    grad_x = dh.reshape(x.shape)

    small_params = dict(pool_w=pool_w, pool_scale=pool_scale, rel_bias=rel_bias, group_gain=group_gain,
                        pre_mix_g=pre_mix_g, post_mix_g=post_mix_g, pre_ffn_g=pre_ffn_g, post_ffn_g=post_ffn_g)
    small_m = dict(pool_w=m_pool_w, pool_scale=m_pool_scale, rel_bias=m_rel_bias, group_gain=m_group_gain,
                   pre_mix_g=m_pre_mix_g, post_mix_g=m_post_mix_g, pre_ffn_g=m_pre_ffn_g, post_ffn_g=m_post_ffn_g)
    small_v = dict(pool_w=v_pool_w, pool_scale=v_pool_scale, rel_bias=v_rel_bias, group_gain=v_group_gain,
                   pre_mix_g=v_pre_mix_g, post_mix_g=v_post_mix_g, pre_ffn_g=v_pre_ffn_g, post_ffn_g=v_post_ffn_g)
    shapes = [small_params[k].shape for k in _SMALL] + [(depth, 3, D_CONV), (1,)]
    n_small = sum(int(np.prod(shp)) for shp in shapes)
    rows = -(-n_small // (8 * D_MODEL)) * 8
    extra = [jnp.zeros((depth, 3, D_CONV), F32), jnp.zeros((1,), F32)]
    grads_packed = _pack([jnp.stack(small[k]) for k in _SMALL] + [jnp.stack(small["conv_w"]), loss_part[0, 0:1]], rows)
    small_started = gather_start("gather_small_s", [grads_packed], carried)

    big = dict(w_in=(w_in_t, tr(m_w_in), tr(v_w_in)), w_out=(w_out, m_w_out, v_w_out),
               w_gate_up=(w_gu_t, tr(m_w_gate_up), tr(v_w_gate_up)), w_down=(w_down, m_w_down, v_w_down))
    results = dict(w_in=None, w_out=None, w_gate_up=None, w_down=None)
    after = small_started["token"]
    small_relay = None
    for l in reversed(range(depth)):
        if (l, "b") in relayed:
            slabs_gu, slabs_dn = reduce_finish(f"reduce_b{l}_f", relayed[l, "b"], after)
            slabs_in, slabs_out = reduce_finish(f"reduce_a{l}_f", relayed[l, "a"], slabs_dn[1])
        else:
            slabs_in, slabs_out, slabs_gu, slabs_dn = reduce_finish(f"reduce_a{l}_f", relayed[l, "a"], after)
        for k, slabs in (("w_gate_up", slabs_gu), ("w_down", slabs_dn), ("w_in", slabs_in), ("w_out", slabs_out)):
            results[k] = adamw_layer("adamw_" + k, l, *big[k], *slabs, results[k])
        after = results["w_out"][0]
        if l == min(1, depth - 1):
            small_relay = gather_relay("gather_small_r", small_started, after)
    small_parts = gather_finish("gather_small_f", small_relay, after)[0]
    all_small = assemble_rows("assemble_small", *small_parts).reshape(8, rows, D_MODEL)
    res_small = adamw("adamw_small", _pack([small_params[k] for k in _SMALL] + extra, rows),
                      _pack([small_m[k] for k in _SMALL] + extra, rows),
                      _pack([small_v[k] for k in _SMALL] + extra, rows), all_small)
    g_s, d_s, m_s, v_s = (_unpack(r, shapes) for r in res_small)
    loss = g_s[-1][0]
    g_conv = lax.dynamic_slice_in_dim(g_s[-2], dev * conv_w.shape[2], conv_w.shape[2], axis=2)
    results["conv_w"] = adamw("adamw_conv", conv_w, m_conv_w, v_conv_w, g_conv[None])

    names = ("w_in", "w_out", "conv_w") + _SMALL + ("w_gate_up", "w_down")
    for k in ("w_in", "w_gate_up"):
        results[k] = tuple(tr(r) for r in results[k])
    for j, k in enumerate(_SMALL):
        results[k] = (g_s[j], d_s[j], m_s[j], v_s[j])
    return (loss, grad_x, *[results[k][0] for k in names], *[results[k][1] for k in names],
            *[results[k][2] for k in names], *[results[k][3] for k in names])
```

```python
import numpy as np
import jax
import jax.numpy as jnp
from jax import lax
from jax.experimental import pallas as pl
from jax.experimental.pallas import tpu as pltpu

F32 = jnp.float32
BF16 = jnp.bfloat16

CHUNK = 64
D_MODEL = 1024
D_CONV = 256
D_POOL = 256
D_ATTN = 512
HEAD_DIM = 64
N_HEADS = 8
N_POOL = 4
POOL_GC = 64
LEFT_CHUNKS = 8
REL_CLIP = 128
D_FF = 2816
EPS = 1e-6
ADAM_LR, ADAM_B1, ADAM_B2, ADAM_EPS, ADAM_WD, ADAM_STEP = 0.001, 0.9, 0.999, 1e-08, 0.01, 10

QB = 256
KB = 3 * QB
HALO = 16
T_CP = 512
T_ROW = 512
QKV_COL = 2
NEG = -1e30
VMEM_MB = 1 << 20
MESH = pl.DeviceIdType.MESH


def _call(body, **kw):
    call = pl.pallas_call(body, **kw)
    return lambda *args: call(*[_in_hbm(a) for a in args])


def _in_hbm(a):
    return pltpu.with_memory_space_constraint(a, pltpu.HBM) if jnp.issubdtype(a.dtype, jnp.number) else a


def _call_after(deps, n_in, body, **kw):
    deps = tuple(deps)
    if not deps:
        return _call(body, **kw)

    def ordered(*refs):
        body(*refs[:n_in], *refs[n_in + len(deps):])

    kw["in_specs"] = list(kw["in_specs"]) + [pl.BlockSpec(memory_space=pl.ANY)] * len(deps)
    call = _call(ordered, **kw)
    return lambda *args: call(*args, *deps)


def _params(sem, vmem_mb=48):
    return pltpu.CompilerParams(dimension_semantics=sem, vmem_limit_bytes=vmem_mb * VMEM_MB)


_CHIP_FLIPS = ((1, 0), (0, 1), (1, 1))
_HBM = pl.BlockSpec(memory_space=pltpu.HBM)
_SEM = pl.BlockSpec(memory_space=pltpu.SEMAPHORE)
_EFFECT = pltpu.SideEffectType.DATAFLOW_SIDE_EFFECTING


def _flip(v, f):
    return 1 - v if f else v


def _descriptors(plan, srcs, lands, send_sems, recv_sems, sending):
    x, y, c = lax.axis_index("x"), lax.axis_index("y"), lax.axis_index("c")
    return [pltpu.make_async_remote_copy(src_ref=src, dst_ref=dst if sending else land, send_sem=send_sems.at[k],
                                         recv_sem=recv_sems.at[k], device_id=peer, device_id_type=MESH)
            for k, (src, dst, peer, land) in enumerate(plan(srcs, lands, x, y, c))]


def exchange_start(name, plan, n_copies, srcs, land_shapes, deps=()):
    ns, nl = len(srcs), len(land_shapes)

    def body(*refs):
        src_refs, land_refs = refs[:ns], refs[ns:ns + nl]
        send_sems, recv_sems = refs[ns + nl], refs[ns + nl + 1]
        for send in _descriptors(plan, src_refs, land_refs, send_sems, recv_sems, True):
            send.start()
        refs[-1][...] = jnp.zeros_like(refs[-1])

    lands = [lax.empty(shape, dtype) for shape, dtype in land_shapes]
    outs = _call_after(
        deps, ns + nl, body, name=name,
        out_shape=(pltpu.SemaphoreType.DMA((n_copies,)), pltpu.SemaphoreType.DMA((n_copies,)),
                   *[pltpu.HBM(a.shape, a.dtype) for a in srcs], *[pltpu.HBM(shape, dtype) for shape, dtype in land_shapes],
                   jax.ShapeDtypeStruct((8, 128), F32)),
        in_specs=[_HBM] * (ns + nl),
        out_specs=(_SEM, _SEM, *[_HBM] * (ns + nl), pl.BlockSpec(memory_space=pltpu.VMEM)),
        input_output_aliases={j: 2 + j for j in range(ns + nl)},
        compiler_params=pltpu.CompilerParams(has_side_effects=_EFFECT),
    )(*srcs, *lands)
    return dict(plan=plan, sems=outs[:2], srcs=outs[2:2 + ns], lands=outs[2 + ns:2 + ns + nl], token=outs[-1])


def exchange_wait(name, started, after):
    srcs, lands = started["srcs"], started["lands"]
    ns, nl = len(srcs), len(lands)

    def body(*refs):
        src_refs, land_refs = refs[:ns], refs[ns:ns + nl]
        send_sems, recv_sems = refs[ns + nl], refs[ns + nl + 1]
        for wait in _descriptors(started["plan"], src_refs, land_refs, send_sems, recv_sems, False):
            wait.wait_send()
            wait.wait_recv()

    outs = _call(
        body, name=name,
        out_shape=tuple(pltpu.HBM(a.shape, a.dtype) for a in (*srcs, *lands)),
        in_specs=[_HBM] * (ns + nl) + [_SEM, _SEM, pl.BlockSpec(memory_space=pl.ANY)],
        out_specs=tuple([_HBM] * (ns + nl)),
        input_output_aliases={j: j for j in range(ns + nl)},
        compiler_params=pltpu.CompilerParams(has_side_effects=_EFFECT),
    )(*srcs, *lands, *started["sems"], after)
    return list(outs[:ns]), list(outs[ns:])


def plan_from_chips(srcs, lands, x, y, c):
    q = 2 * x + y
    out = []
    for src, land in zip(srcs, lands):
        for fx, fy in _CHIP_FLIPS:
            px, py = _flip(x, fx), _flip(y, fy)
            out.append((src, land.at[q], (px, py, c), land.at[2 * px + py]))
    return out


def plan_to_core(srcs, lands, x, y, c):
    n = len(lands)
    q = 2 * x + y
    out = []
    for own, chips, land in zip(srcs[:n], srcs[n:], lands):
        out.append((own, land.at[q], (x, y, 1 - c), land.at[q]))
        for fx, fy in _CHIP_FLIPS:
            qp = 2 * _flip(x, fx) + _flip(y, fy)
            out.append((chips.at[qp], land.at[qp], (x, y, 1 - c), land.at[qp]))
    return out


def plan_swap_cores(srcs, lands, x, y, c):
    return [(src, land, (x, y, 1 - c), land) for src, land in zip(srcs, lands)]


def plan_to_chips(srcs, lands, x, y, c):
    q = 2 * x + y
    out = []
    for src, land in zip(srcs, lands):
        for fx, fy in _CHIP_FLIPS:
            px, py = _flip(x, fx), _flip(y, fy)
            out.append((src.at[2 * px + py], land.at[q], (px, py, c), land.at[2 * px + py]))
    return out


def _slots(a):
    return ((4,) + a.shape, a.dtype)


def gather_start(name, arrays, deps=()):
    return exchange_start(name, plan_from_chips, 3 * len(arrays), arrays, [_slots(a) for a in arrays], deps)


def gather_relay(name, started, after):
    own, chips = exchange_wait(name + "_w", started, after)
    return exchange_start(name + "_s", plan_to_core, 4 * len(own), own + chips, [_slots(a) for a in own])


def gather_finish(name, relayed, after):
    srcs, cores = exchange_wait(name, relayed, after)
    n = len(cores)
    return list(zip(srcs[:n], srcs[n:], cores))


def _place():
    return 2 * lax.axis_index("x") + lax.axis_index("y"), lax.axis_index("c")


def assemble_cols(name, own, chips, core):
    r, c = own.shape
    t = _row_tile(r, 256)

    def body(own_ref, chips_ref, core_ref, o_ref):
        q_me, c_me = _place()
        for q in range(4):
            mine = jnp.where(q == q_me, own_ref[...], chips_ref[q])
            other = core_ref[q]
            o_ref[:, 2 * q * c:(2 * q + 1) * c] = jnp.where(c_me == 0, mine, other)
            o_ref[:, (2 * q + 1) * c:(2 * q + 2) * c] = jnp.where(c_me == 0, other, mine)

    slots = pl.BlockSpec((4, t, c), lambda i: (0, i, 0))
    return _call(
        body, name=name, grid=(r // t,),
        in_specs=[pl.BlockSpec((t, c), lambda i: (i, 0)), slots, slots],
        out_specs=pl.BlockSpec((t, 8 * c), lambda i: (i, 0)),
        out_shape=jax.ShapeDtypeStruct((r, 8 * c), own.dtype), compiler_params=_params(("parallel",)),
    )(own, chips, core)


def assemble_rows(name, own, chips, core):
    r, c = own.shape

    def body(own_ref, chips_ref, core_ref, o_ref):
        q_me, c_me = _place()
        d = pl.program_id(0)
        mine = jnp.where(d // 2 == q_me, own_ref[...], chips_ref[...])
        o_ref[...] = jnp.where(d % 2 == c_me, mine, core_ref[...])

    slot = pl.BlockSpec((None, r, c), lambda d: (d // 2, 0, 0))
    return _call(
        body, name=name, grid=(8,),
        in_specs=[pl.BlockSpec((r, c), lambda d: (0, 0)), slot, slot],
        out_specs=pl.BlockSpec((r, c), lambda d: (d, 0)),
        out_shape=jax.ShapeDtypeStruct((8 * r, c), own.dtype), compiler_params=_params(("parallel",)),
    )(own, chips, core)


def split_rows(name, dw):
    r8, c = dw.shape
    r = r8 // 8

    def body(dw_ref, keep_ref, send_ref):
        _, c_me = _place()
        d = pl.program_id(0)

        @pl.when(d % 2 == c_me)
        def _():
            keep_ref[...] = dw_ref[...]

        @pl.when(d % 2 != c_me)
        def _():
            send_ref[...] = dw_ref[...]

    slot = pl.BlockSpec((None, r, c), lambda d: (d // 2, 0, 0))
    return _call(
        body, name=name, grid=(8,),
        in_specs=[pl.BlockSpec((r, c), lambda d: (d, 0))], out_specs=[slot, slot],
        out_shape=[jax.ShapeDtypeStruct((4, r, c), dw.dtype)] * 2, compiler_params=_params(("arbitrary",)),
    )(dw)


def _rms(v):
    return lax.rsqrt(jnp.mean(v * v, axis=-1, keepdims=True) + EPS)


def rmsnorm_fwd(name, x, g, out_dtype):
    s, d = x.shape

    def body(x_ref, g_ref, o_ref):
        xv = x_ref[...]
        o_ref[...] = ((xv * _rms(xv)) * g_ref[...]).astype(o_ref.dtype)

    row = pl.BlockSpec((T_ROW, d), lambda i: (i, 0))
    return _call(
        body, name=name, grid=(s // T_ROW,),
        in_specs=[row, pl.BlockSpec((1, d), lambda i: (0, 0))], out_specs=row,
        out_shape=jax.ShapeDtypeStruct((s, d), out_dtype), compiler_params=_params(("parallel",)),
    )(x, g)


def matmul_then(name, a_parts, b, mode, tm, chunk, rows_in, vecs_in, rows_out, n_sums, then, deps=()):
    a_parts = list(a_parts)
    na, nr, nv, no = len(a_parts), len(rows_in), len(vecs_in), len(rows_out)
    s = a_parts[0].shape[0]
    n = s // tm
    d = b.shape[1] if mode == "nn" else b.shape[0]
    offs = [0]
    for p in a_parts:
        offs.append(offs[-1] + p.shape[1])
    n_in = na + 1 + nr + nv

    def body(*refs):
        a_refs, b_ref = refs[:na], refs[na]
        row_refs, vec_refs = refs[na + 1:na + 1 + nr], refs[na + 1 + nr:n_in]
        out_refs, sum_refs = refs[n_in:n_in + no], refs[n_in + no:n_in + no + n_sums]
        acc = refs[-2:]
        i = pl.program_id(0)

        @pl.when(i == 0)
        def _():
            acc[1][...] = jnp.zeros_like(acc[1])
            for s_ref in sum_refs:
                s_ref[...] = jnp.zeros_like(s_ref)

        def step(write, read):
            t = None
            for p in range(na):
                lo, hi = offs[p], offs[p + 1]
                part = _dot(a_refs[p][...], b_ref[lo:hi, :] if mode == "nn" else b_ref[:, lo:hi], mode)
                t = part if t is None else t + part
            write[...] = t
            totals = [0.0] * n_sums
            for r0 in range(0, tm, chunk):
                rs = pl.ds(r0, chunk)
                results, sums = then(read[rs, :], [r.at[rs, :] for r in row_refs], vec_refs)
                for o_ref, val in zip(out_refs, results):
                    o_ref[rs, :] = val.astype(o_ref.dtype)
                totals = [tot + val for tot, val in zip(totals, sums)]
            for s_ref, val in zip(sum_refs, totals):
                s_ref[...] += jnp.where(i > 0, val, 0.0)

        pl.when(i % 2 == 0)(lambda: step(acc[0], acc[1]))
        pl.when(i % 2 == 1)(lambda: step(acc[1], acc[0]))

    ahead = lambda i: (jnp.minimum(i, n - 1), 0)
    behind = lambda i: (jnp.maximum(i - 1, 0), 0)
    fixed = lambda i: (0, 0)
    shapes = [jax.ShapeDtypeStruct((s, d), dt) for dt in rows_out] + [jax.ShapeDtypeStruct((1, d), F32)] * n_sums
    return _call_after(
        deps, n_in, body, name=name, grid=(n + 1,),
        in_specs=[pl.BlockSpec((tm, p.shape[1]), ahead) for p in a_parts]
        + [pl.BlockSpec(b.shape, fixed, pipeline_mode=pl.Buffered(1))]
        + [pl.BlockSpec((tm, r.shape[1]), behind) for r in rows_in] + [pl.BlockSpec(v.shape, fixed) for v in vecs_in],
        out_specs=[pl.BlockSpec((tm, d), behind)] * no + [pl.BlockSpec((1, d), fixed)] * n_sums,
        out_shape=shapes, scratch_shapes=[pltpu.VMEM((tm, d), F32)] * 2,
        compiler_params=_params(("arbitrary",), 56),
    )(*a_parts, b, *rows_in, *vecs_in)


def matmul_norm(name, a, b, res, g, then=None, tm=512, deps=()):
    def norm(z, rows, vecs):
        y = rows[0][...] + (z * _rms(z)) * vecs[0][...]
        return [z, y] + ([(y * _rms(y)) * vecs[1][...]] if then else []), []

    return matmul_then(name, [a], b, "nn", tm, tm, [res], [g] + ([then[0]] if then else []),
                       [BF16, F32] + ([then[1]] if then else []), 0, norm, deps)


def matmul_norm_bwd(name, a_parts, b, tm, x, g, res, then=None, chunk=None, deps=()):
    def norms(t, rows, vecs):
        xv = rows[0][...].astype(F32)
        dx, dgt = _norm_bwd(xv, _rms(xv), vecs[0][...], t)
        dx = dx + rows[1][...]
        results, sums = [dx], [jnp.sum(dgt, axis=0, keepdims=True)]
        if then:
            x2 = rows[2][...].astype(F32)
            dx2, dgt2 = _norm_bwd(x2, _rms(x2), vecs[1][...], dx)
            results.append(dx2)
            sums.append(jnp.sum(dgt2, axis=0, keepdims=True))
        return results, sums

    return matmul_then(name, a_parts, b, "nn", tm, chunk or tm, [x, res] + ([then[0]] if then else []),
                       [g] + ([then[1]] if then else []), [F32] + ([then[2]] if then else []),
                       2 if then else 1, norms, deps)


def _norm_bwd(xv, r, g, dy):
    a = dy * g
    dx = r * (a - xv * ((r * r) * jnp.mean(a * xv, axis=-1, keepdims=True)))
    return dx, dy * (xv * r)


def loss_head(name, h, tgt, z, g):
    s, d = h.shape

    def body(h_ref, t_ref, z_ref, g_ref, dh_ref, l_ref, dz_ref, dg_ref):
        e = h_ref[...] - t_ref[...]
        dh = e * (1.0 / d)
        dh_ref[...] = dh
        zv = z_ref[...].astype(F32)
        dz, dgt = _norm_bwd(zv, _rms(zv), g_ref[...], dh)
        dz_ref[...] = dz.astype(dz_ref.dtype)

        @pl.when(pl.program_id(0) == 0)
        def _():
            l_ref[...] = jnp.zeros_like(l_ref)
            dg_ref[...] = jnp.zeros_like(dg_ref)
        part = 0.5 * jnp.sum(jnp.mean(e * e, axis=-1, keepdims=True), axis=0, keepdims=True)
        l_ref[...] += jnp.broadcast_to(part, l_ref.shape)
        dg_ref[...] += jnp.sum(dgt, axis=0, keepdims=True)

    row = pl.BlockSpec((T_ROW, d), lambda i: (i, 0))
    vec = pl.BlockSpec((1, d), lambda i: (0, 0))
    return _call(
        body, name=name, grid=(s // T_ROW,),
        in_specs=[row, row, row, vec], out_specs=[row, pl.BlockSpec((1, 128), lambda i: (0, 0)), row, vec],
        out_shape=[jax.ShapeDtypeStruct((s, d), F32), jax.ShapeDtypeStruct((1, 128), F32),
                   jax.ShapeDtypeStruct((s, d), BF16), jax.ShapeDtypeStruct((1, d), F32)],
        compiler_params=_params(("arbitrary",)),
    )(h, tgt, z, g)


def _row_tile(rows, limit=512):
    t = min(rows, limit)
    while rows % t or (t % 8 and t != rows):
        t -= 1
    return t


def add_pairs(name, a, b):
    shape = a.shape
    a2, b2 = a.reshape(-1, shape[-1]), b.reshape(-1, shape[-1])
    rows, cols = a2.shape
    t = _row_tile(rows)

    def body(a_ref, b_ref, o_ref):
        o_ref[...] = (a_ref[...].astype(F32) + b_ref[...].astype(F32)).astype(o_ref.dtype)

    blk = pl.BlockSpec((t, cols), lambda i: (i, 0))
    out = _call(
        body, name=name, grid=(rows // t,), in_specs=[blk, blk], out_specs=blk,
        out_shape=jax.ShapeDtypeStruct((rows, cols), a.dtype), compiler_params=_params(("parallel",)),
    )(a2, b2)
    return out.reshape(shape)


def _adamw_math(w, m, v, g):
    c1 = 1.0 - ADAM_B1 ** ADAM_STEP
    c2 = 1.0 - ADAM_B2 ** ADAM_STEP
    mn = ADAM_B1 * m + (1.0 - ADAM_B1) * g
    vn = ADAM_B2 * v + (1.0 - ADAM_B2) * (g * g)
    delta = -ADAM_LR * ((mn / c1) / (jnp.sqrt(vn / c2) + ADAM_EPS) + ADAM_WD * w)
    return delta, mn, vn


def _slab_sum(g_ref, n):
    g = g_ref[0].astype(F32)
    for j in range(1, n):
        g = g + g_ref[j].astype(F32)
    return g


def adamw(name, w, m, v, g_slabs, deps=()):
    shape = w.shape
    n = g_slabs.shape[0]
    w2, m2, v2 = (t.reshape(-1, shape[-1]) for t in (w, m, v))
    g3 = g_slabs.reshape(n, -1, shape[-1])
    rows, cols = w2.shape
    t = _row_tile(rows, 256)

    def body(w_ref, m_ref, v_ref, g_ref, go_ref, d_ref, mo_ref, vo_ref):
        g = _slab_sum(g_ref, n)
        go_ref[...] = g
        d_ref[...], mo_ref[...], vo_ref[...] = _adamw_math(w_ref[...], m_ref[...], v_ref[...], g)

    blk = pl.BlockSpec((t, cols), lambda i: (i, 0))
    outs = _call_after(
        deps, 4, body, name=name, grid=(rows // t,),
        in_specs=[blk, blk, blk, pl.BlockSpec((n, t, cols), lambda i: (0, i, 0))], out_specs=[blk] * 4,
        out_shape=[jax.ShapeDtypeStruct((rows, cols), F32)] * 4, compiler_params=_params(("parallel",)),
    )(w2, m2, v2, g3)
    return tuple(o.reshape(shape) for o in outs)


def adamw_layer(name, l, w, m, v, own_sums, chip_sums, into):
    _, rows, cols = w.shape
    t = _row_tile(rows, 256)
    if into is None:
        into = tuple(lax.empty(w.shape, F32) for _ in range(4))

    def body(w_ref, m_ref, v_ref, own_ref, far_ref, a0, a1, a2, a3, go_ref, d_ref, mo_ref, vo_ref):
        del a0, a1, a2, a3
        q_me, _ = _place()
        g = jnp.zeros((t, cols), F32)
        for q in range(4):
            g = g + jnp.where(q == q_me, own_ref[q], far_ref[q]).astype(F32)
        go_ref[...] = g
        d_ref[...], mo_ref[...], vo_ref[...] = _adamw_math(w_ref[...], m_ref[...], v_ref[...], g)

    blk = pl.BlockSpec((None, t, cols), lambda i: (l, i, 0))
    slabs = pl.BlockSpec((4, t, cols), lambda i: (0, i, 0))
    anyw = pl.BlockSpec(memory_space=pl.ANY)
    return _call(
        body, name=name, grid=(rows // t,),
        in_specs=[blk, blk, blk, slabs, slabs] + [anyw] * 4, out_specs=[blk] * 4,
        out_shape=[jax.ShapeDtypeStruct(w.shape, F32)] * 4, input_output_aliases={5: 0, 6: 1, 7: 2, 8: 3},
        compiler_params=_params(("parallel",)),
    )(w, m, v, own_sums, chip_sums, *into)


_DIMS = {"nn": (((1,), (0,)), ((), ())), "nt": (((1,), (1,)), ((), ())), "tn": (((0,), (0,)), ((), ()))}


def _dot(a, b, mode="nn"):
    return lax.dot_general(a, b, _DIMS[mode], preferred_element_type=F32)


def matmul(name, a, b, mode, out_dtype, m, n, k, tm, tn, tk, b_off=(0, 0), shard_rows=0, deps=()):
    gm, gn, gk = m // tm, n // tn, k // tk
    assert gm * tm == m and gn * tn == n and gk * tk == k
    r0, c0 = b_off
    a_parts = list(a) if isinstance(a, (list, tuple)) else [a]
    na = len(a_parts)
    tile = tm if mode == "tn" else tk
    spans, at = [], 0
    for p in a_parts:
        nblk = p.shape[1] // tile
        assert nblk * tile == p.shape[1]
        spans.append((at, nblk))
        at += nblk
    assert at == (gm if mode == "tn" else gk)

    def within(t, span):
        return (t >= span[0]) & (t < span[0] + span[1])

    def local(t, span):
        return jnp.clip(t - span[0], 0, span[1] - 1)

    a_specs = []
    for sp in spans:
        if mode == "tn":
            a_specs.append(pl.BlockSpec((tk, tm), lambda i, j, kk, sp=sp: (jnp.where(within(i, sp), kk, 0), local(i, sp))))
        else:
            a_specs.append(pl.BlockSpec((tm, tk), lambda i, j, kk, sp=sp: (i, local(kk, sp))))
    if mode == "nt":
        b_spec = pl.BlockSpec((tn, tk), lambda i, j, kk: (j + r0, kk + c0))
    else:
        b_spec = pl.BlockSpec((tk, tn), lambda i, j, kk: (kk + r0, j + c0))
    o_spec = pl.BlockSpec((tm, tn), lambda i, j, kk: (i, j))
    single = na == 1 and gk == 1 and not shard_rows
    pairs = tm // (2 * shard_rows) if shard_rows else 0
    assert not shard_rows or (pairs * 2 * shard_rows == tm and m == 8 * shard_rows)

    def body(*refs):
        a_refs, b_ref, o_ref = refs[:na], refs[na], refs[na + 1]
        if single:
            o_ref[...] = _dot(a_refs[0][...], b_ref[...], mode).astype(o_ref.dtype)
            return
        acc_ref = refs[-1]
        i, kk = pl.program_id(0), pl.program_id(2)

        @pl.when(kk == 0)
        def _():
            acc_ref[...] = jnp.zeros_like(acc_ref)

        for pa, sp in enumerate(spans):
            def add(pa=pa):
                acc_ref[...] += _dot(a_refs[pa][...], b_ref[...], mode)
            if na > 1:
                pl.when(within(i if mode == "tn" else kk, sp))(add)
            else:
                add()

        @pl.when(kk == gk - 1)
        def _():
            if not shard_rows:
                o_ref[...] = acc_ref[...].astype(o_ref.dtype)
                return
            send_ref, mine = refs[na + 2], lax.axis_index("c") == 0
            for pq in range(pairs):
                even = acc_ref[2 * shard_rows * pq:2 * shard_rows * pq + shard_rows, :]
                odd = acc_ref[2 * shard_rows * pq + shard_rows:2 * shard_rows * (pq + 1), :]
                o_ref[pq] = jnp.where(mine, even, odd).astype(o_ref.dtype)
                send_ref[pq] = jnp.where(mine, odd, even).astype(send_ref.dtype)

    if shard_rows:
        o_spec = pl.BlockSpec((pairs, shard_rows, tn), lambda i, j, kk: (i, 0, j))
    shard = jax.ShapeDtypeStruct((4, shard_rows, n), out_dtype)
    return _call_after(
        deps, na + 1, body, name=name, grid=(gm, gn, gk),
        in_specs=a_specs + [b_spec], out_specs=[o_spec, o_spec] if shard_rows else o_spec,
        out_shape=[shard, shard] if shard_rows else jax.ShapeDtypeStruct((m, n), out_dtype),
        scratch_shapes=[] if single else [pltpu.VMEM((tm, tn), F32)],
        compiler_params=_params(("parallel", "parallel", "arbitrary")),
    )(*a_parts, b)


def matmul_swiglu(name, a, w_gu_t, deps=()):
    s, d = a.shape
    tm, tn = 1024, 1408
    gn = D_FF // tn

    def body(a_ref, wg_ref, wu_ref, du_ref, dg_ref, f_ref):
        av = a_ref[...]
        g = _dot(av, wg_ref[...], "nt")
        u = _dot(av, wu_ref[...], "nt")
        sig = 1.0 / (1.0 + jnp.exp(-g))
        silu = g * sig
        du_ref[...] = silu.astype(du_ref.dtype)
        dg_ref[...] = (u * (sig + silu * (1.0 - sig))).astype(dg_ref.dtype)
        f_ref[...] = (silu * u).astype(f_ref.dtype)

    o_spec = pl.BlockSpec((tm, tn), lambda j, i: (i, j))
    return _call_after(
        deps, 3, body, name=name, grid=(gn, s // tm),
        in_specs=[pl.BlockSpec((tm, d), lambda j, i: (i, 0)), pl.BlockSpec((tn, d), lambda j, i: (j, 0)),
                  pl.BlockSpec((tn, d), lambda j, i: (j + gn, 0))],
        out_specs=[o_spec] * 3, out_shape=[jax.ShapeDtypeStruct((s, D_FF), BF16)] * 3,
        compiler_params=_params(("parallel", "parallel"), 56),
    )(a, w_gu_t, w_gu_t)


def matmul_swiglu_bwd(name, dffo, w_down, ff_du, ff_dg):
    s, d = dffo.shape
    tm, tn = 1024, 1408

    def body(a_ref, w_ref, pu_ref, pg_ref, dg_ref, du_ref):
        dff = _dot(a_ref[...], w_ref[...], "nt")
        du_ref[...] = (dff * pu_ref[...].astype(F32)).astype(du_ref.dtype)
        dg_ref[...] = (dff * pg_ref[...].astype(F32)).astype(dg_ref.dtype)

    o_spec = pl.BlockSpec((tm, tn), lambda j, i: (i, j))
    return _call(
        body, name=name, grid=(D_FF // tn, s // tm),
        in_specs=[pl.BlockSpec((tm, d), lambda j, i: (i, 0)), pl.BlockSpec((tn, d), lambda j, i: (j, 0)), o_spec, o_spec],
        out_specs=[o_spec] * 2, out_shape=[jax.ShapeDtypeStruct((s, D_FF), BF16)] * 2,
        compiler_params=_params(("parallel", "parallel"), 56),
    )(dffo, w_down, ff_du, ff_dg)


def _down(v, n):
    return pltpu.roll(v, n, 0)


def _up(v, n):
    return pltpu.roll(v, v.shape[0] - n, 0)


def _by_window(lane, v2, v4, v8, v16):
    return jnp.where(lane < POOL_GC, v2, jnp.where(lane < 2 * POOL_GC, v4, jnp.where(lane < 3 * POOL_GC, v8, v16)))


def _taps(cw_ref):
    return cw_ref[0:1, :], cw_ref[1:2, :], cw_ref[2:3, :]


def _conv_pool_forward(ext, t0, cw, bd):
    n_out = ext.shape[0] - HALO
    gb = ext[HALO:, 0:D_CONV]
    z = ext[:, D_CONV:2 * D_CONV] * ext[:, 2 * D_CONV:3 * D_CONV]
    z0, z1, z2 = z[HALO:], _down(z, 1)[HALO:], _down(z, 2)[HALO:]
    conv = cw[2] * z0 + cw[1] * z1 + cw[0] * z2
    x = ext[:, 3 * D_CONV:]
    w2 = x + _down(x, 1)
    w4 = w2 + _down(w2, 2)
    w8 = w4 + _down(w4, 4)
    w16 = w8 + _down(w8, 8)
    lane = lax.broadcasted_iota(jnp.int32, (1, D_POOL), 1)
    win = _by_window(lane, 2.0, 4.0, 8.0, 16.0)
    pos = (t0 + lax.broadcasted_iota(jnp.int32, (n_out, 1), 0) + 1).astype(F32)
    cnt = jnp.minimum(pos, win)
    d = _by_window(lane, w2, w4, w8, w16)[HALO:] / cnt - x[HALO:]
    ybp = _dot(d.astype(BF16), bd)
    return gb, z0, z1, z2, conv, d, cnt, ybp, lane


def conv_pool_fwd(name, proj_a, cw, bd, ps, gg, deps=()):
    s = proj_a.shape[0]
    t = T_CP
    hb = t // HALO

    def body(main_ref, prev_ref, cw_ref, bd_ref, ps_ref, gg_ref, y_ref):
        i = pl.program_id(0)
        prev = jnp.where(i > 0, prev_ref[...].astype(F32), 0.0)
        ext = jnp.concatenate([prev, main_ref[...].astype(F32)], axis=0)
        gb, _, _, _, conv, _, _, ybp, _ = _conv_pool_forward(ext, i * t, _taps(cw_ref), bd_ref[...])
        ya = gb * conv
        yb = ybp * ps_ref[...]
        ggv = gg_ref[...]
        y_ref[:, 0:D_CONV] = ((ya * _rms(ya)) * ggv[:, 0:D_CONV]).astype(y_ref.dtype)
        y_ref[:, D_CONV:] = ((yb * _rms(yb)) * ggv[:, D_CONV:]).astype(y_ref.dtype)

    full = lambda shape: pl.BlockSpec(shape, lambda i: (0,) * len(shape))
    return _call_after(
        deps, 6, body, name=name, grid=(s // t,),
        in_specs=[pl.BlockSpec((t, D_MODEL), lambda i: (i, 0)),
                  pl.BlockSpec((HALO, D_MODEL), lambda i: (jnp.maximum(i * hb - 1, 0), 0)),
                  full((3, D_CONV)), full((D_POOL, D_POOL)), full((1, D_POOL)), full((1, 2 * D_CONV))],
        out_specs=pl.BlockSpec((t, 2 * D_CONV), lambda i: (i, 0)),
        out_shape=jax.ShapeDtypeStruct((s, D_MODEL), BF16),
        compiler_params=_params(("parallel",)),
    )(proj_a, proj_a, cw, bd, ps, gg)


def conv_pool_bwd(name, proj_a, dy, cw, bd, ps, gg):
    s = proj_a.shape[0]
    t = T_CP
    hb = t // HALO
    nblk = s // t
    last_halo = s // HALO - 1

    def body(main_ref, prev_ref, next_ref, dy_ref, dyn_ref, cw_ref, bd_ref, ps_ref, gg_ref,
             dp_ref, dcw_ref, dbd_ref, dps_ref, dgg_ref):
        i = pl.program_id(0)
        prev = jnp.where(i > 0, prev_ref[...].astype(F32), 0.0)
        main = main_ref[...].astype(F32)
        ext = jnp.concatenate([prev, main, next_ref[...].astype(F32)], axis=0)
        cwv, bdv, psv, ggv = _taps(cw_ref), bd_ref[...], ps_ref[...], gg_ref[...]
        gb, z0, z1, z2, conv, d, cnt, ybp, lane = _conv_pool_forward(ext, i * t, cwv, bdv)
        dyn = jnp.where(i < nblk - 1, dyn_ref[...].astype(F32), 0.0)
        dyv = jnp.concatenate([dy_ref[...].astype(F32), dyn], axis=0)
        ya = gb * conv
        yb = ybp * psv
        dya, dgg_a = _norm_bwd(ya, _rms(ya), ggv[:, 0:D_CONV], dyv[:, 0:D_CONV])
        dyb, dgg_b = _norm_bwd(yb, _rms(yb), ggv[:, D_CONV:], dyv[:, D_CONV:])

        dconv = dya * gb
        dz = (cwv[2] * dconv + cwv[1] * _up(dconv, 1) + cwv[0] * _up(dconv, 2))[:t]
        dp_ref[:, 0:D_CONV] = (dya * conv)[:t].astype(dp_ref.dtype)
        dp_ref[:, D_CONV:2 * D_CONV] = (dz * main[:, 2 * D_CONV:3 * D_CONV]).astype(dp_ref.dtype)
        dp_ref[:, 2 * D_CONV:3 * D_CONV] = (dz * main[:, D_CONV:2 * D_CONV]).astype(dp_ref.dtype)

        dybs = dyb * psv
        dd = _dot(dybs.astype(BF16), bdv, "nt")
        e = dd / cnt
        a2 = e + _up(e, 1)
        a4 = a2 + _up(a2, 2)
        a8 = a4 + _up(a4, 4)
        a16 = a8 + _up(a8, 8)
        dp_ref[:, 3 * D_CONV:] = (_by_window(lane, a2, a4, a8, a16) - dd)[:t].astype(dp_ref.dtype)

        @pl.when(i == 0)
        def _():
            dcw_ref[...] = jnp.zeros_like(dcw_ref)
            dbd_ref[...] = jnp.zeros_like(dbd_ref)
            dps_ref[...] = jnp.zeros_like(dps_ref)
            dgg_ref[...] = jnp.zeros_like(dgg_ref)

        rsum = lambda v: jnp.sum(v[:t], axis=0, keepdims=True)
        dcw_ref[0:1, :] += rsum(dconv * z2)
        dcw_ref[1:2, :] += rsum(dconv * z1)
        dcw_ref[2:3, :] += rsum(dconv * z0)
        dbd_ref[...] += _dot(d[:t].astype(BF16), dybs[:t].astype(BF16), "tn")
        dps_ref[...] += rsum(dyb * ybp)
        dgg_ref[:, 0:D_CONV] += rsum(dgg_a)
        dgg_ref[:, D_CONV:] += rsum(dgg_b)

    full = lambda shape: pl.BlockSpec(shape, lambda i: (0,) * len(shape))
    next_halo = lambda i: (jnp.minimum((i + 1) * hb, last_halo), 0)
    return _call(
        body, name=name, grid=(nblk,),
        in_specs=[pl.BlockSpec((t, D_MODEL), lambda i: (i, 0)),
                  pl.BlockSpec((HALO, D_MODEL), lambda i: (jnp.maximum(i * hb - 1, 0), 0)),
                  pl.BlockSpec((HALO, D_MODEL), next_halo),
                  pl.BlockSpec((t, 2 * D_CONV), lambda i: (i, 0)),
                  pl.BlockSpec((HALO, 2 * D_CONV), next_halo),
                  full((3, D_CONV)), full((D_POOL, D_POOL)), full((1, D_POOL)), full((1, 2 * D_CONV))],
        out_specs=[pl.BlockSpec((t, D_MODEL), lambda i: (i, 0)),
                   full((3, D_CONV)), full((D_POOL, D_POOL)), full((1, D_POOL)), full((1, 2 * D_CONV))],
        out_shape=[jax.ShapeDtypeStruct((s, D_MODEL), BF16), jax.ShapeDtypeStruct((3, D_CONV), F32),
                   jax.ShapeDtypeStruct((D_POOL, D_POOL), F32), jax.ShapeDtypeStruct((1, D_POOL), F32),
                   jax.ShapeDtypeStruct((1, 2 * D_CONV), F32)],
        compiler_params=_params(("arbitrary",)),
    )(proj_a, proj_a, proj_a, dy, dy, cw, bd, ps, gg)


def bias_by_offset(rel_bias):
    n_far = 2 * QB - REL_CLIP + 1
    far = jnp.broadcast_to(rel_bias[:, 2 * REL_CLIP:], (N_HEADS, n_far))
    mid = rel_bias[:, 1:2 * REL_CLIP][:, ::-1]
    near = jnp.broadcast_to(rel_bias[:, 0:1], (N_HEADS, KB - n_far - (2 * REL_CLIP - 1)))
    wrap = jnp.broadcast_to(rel_bias[:, 2 * REL_CLIP:], (N_HEADS, 4 * QB - KB))
    return jnp.concatenate([far, mid, near, wrap], axis=1)


def _fill_bias(off_ref, b_ref):
    width = 4 * QB
    sub = lax.broadcasted_iota(jnp.int32, (8, 1), 0)
    col = lax.broadcasted_iota(jnp.int32, (1, KB), 1)
    for h in range(N_HEADS):
        base = jnp.broadcast_to(off_ref[h:h + 1, :], (8, width))
        for bit in range(3):
            base = jnp.where(((sub >> bit) & 1) == 1, pltpu.roll(base, 1 << bit, 1), base)
        for a in range(QB // 8):
            first = CHUNK * (8 * a // CHUNK)
            rows = (pltpu.roll(base, 8 * a, 1) if a else base)[:, :KB]
            band = (col >= first) & (col < first + (LEFT_CHUNKS + 1) * CHUNK)
            b_ref[h, 8 * a:8 * a + 8, :] = jnp.where(band, rows, NEG)


def rel_bias_grad(by_offset):
    n_far = 2 * QB - REL_CLIP + 1
    hi = jnp.sum(by_offset[:, :n_far], axis=1, keepdims=True) + jnp.sum(by_offset[:, KB:], axis=1, keepdims=True)
    mid = by_offset[:, n_far:n_far + 2 * REL_CLIP - 1][:, ::-1]
    lo = jnp.sum(by_offset[:, n_far + 2 * REL_CLIP - 1:KB], axis=1, keepdims=True)
    return jnp.concatenate([lo, mid, hi], axis=1)


def _head_masks():
    first = lax.broadcasted_iota(jnp.int32, (1, 2 * HEAD_DIM), 1) < HEAD_DIM
    return first, jnp.logical_not(first)


def _pick_lane(tile, h):
    lane = lax.broadcasted_iota(jnp.int32, (1, tile.shape[1]), 1)
    return jnp.sum(jnp.where(lane == h, tile, 0.0), axis=-1, keepdims=True)


def attention_fwd(name, qkv, by_offset, gg, y_ab, deps=()):
    s = qkv.shape[0]
    nq = s // QB
    scale = HEAD_DIM ** -0.5

    def body(q_ref, k0, k1, k2, v0, v1, v2, off_ref, gg_ref, y_in, y_ref, o_ref, lse_ref, b_ref):
        del y_in
        i = pl.program_id(0)

        @pl.when(i == 0)
        def _():
            _fill_bias(off_ref, b_ref)
        def block(at_start):
            kb = jnp.concatenate([k0[...], k1[...], k2[...]], axis=0)
            vb = jnp.concatenate([v0[...], v1[...], v2[...]], axis=0)
            valid = lax.broadcasted_iota(jnp.int32, (1, KB), 1) >= (2 - i) * QB
            lane = lax.broadcasted_iota(jnp.int32, (1, 128), 1)
            masks = _head_masks()
            lse = jnp.zeros((QB, 128), F32)
            outs = []
            for hp in range(N_HEADS // 2):
                sl = slice(2 * HEAD_DIM * hp, 2 * HEAD_DIM * (hp + 1))
                q_p, k_p, v_p = q_ref[:, sl] * scale, kb[:, sl], vb[:, sl]
                o_pair = jnp.zeros((QB, 2 * HEAD_DIM), F32)
                for a in range(2):
                    h = 2 * hp + a
                    sc = _dot(jnp.where(masks[a], q_p, 0), k_p, "nt") + b_ref[h]
                    if at_start:
                        sc = jnp.where(valid, sc, NEG)
                    mx = jnp.max(sc, axis=-1, keepdims=True)
                    e = jnp.exp(sc - mx).astype(BF16)
                    ones_at = HEAD_DIM * (1 - a)
                    ev = _dot(e, jnp.where(masks[a], v_p, jnp.where(lane == ones_at, 1, 0).astype(v_p.dtype)))
                    l = jnp.sum(jnp.where(lane == ones_at, ev, 0.0), axis=-1, keepdims=True)
                    o_pair = o_pair + jnp.where(masks[a], ev, 0.0) * (1.0 / l)
                    lse = jnp.where(lane == h, mx + jnp.log(l), lse)
                outs.append(o_pair)
            o = jnp.concatenate(outs, axis=1)
            o_ref[...] = o.astype(o_ref.dtype)
            lse_ref[...] = lse
            y_ref[...] = ((o * _rms(o)) * gg_ref[...]).astype(y_ref.dtype)

        pl.when(i < 2)(lambda: block(True))
        pl.when(i >= 2)(lambda: block(False))

    blk = lambda col, back: pl.BlockSpec((QB, D_ATTN), lambda i: (jnp.maximum(i - back, 0), QKV_COL + col))
    return _call_after(
        deps, 10, body, name=name, grid=(nq,),
        in_specs=[blk(0, 0), blk(1, 2), blk(1, 1), blk(1, 0), blk(2, 2), blk(2, 1), blk(2, 0),
                  pl.BlockSpec((N_HEADS, 4 * QB), lambda i: (0, 0)), pl.BlockSpec((1, D_ATTN), lambda i: (0, 0)),
                  pl.BlockSpec(memory_space=pl.ANY)],
        out_specs=[pl.BlockSpec((QB, D_ATTN), lambda i: (i, 1)), pl.BlockSpec((QB, D_ATTN), lambda i: (i, 0)),
                   pl.BlockSpec((QB, 128), lambda i: (i, 0)), pl.BlockSpec((N_HEADS, QB, KB), lambda i: (0, 0, 0))],
        out_shape=[jax.ShapeDtypeStruct((s, D_MODEL), BF16), jax.ShapeDtypeStruct((s, D_ATTN), BF16),
                   jax.ShapeDtypeStruct((s, 128), F32), jax.ShapeDtypeStruct((N_HEADS, QB, KB), F32)],
        input_output_aliases={9: 0},
        compiler_params=_params(("arbitrary",), 56),
    )(qkv, qkv, qkv, qkv, qkv, qkv, qkv, by_offset, gg, y_ab)


def attention_bwd(name, qkv, o, lse, dy, bias, gg):
    s = qkv.shape[0]
    nq = s // QB
    scale = HEAD_DIM ** -0.5
    width = 4 * QB

    def body(q_ref, k0, k1, k2, v0, v1, v2, o_ref, lse_ref, dy_ref, b_ref, gg_ref,
             dq_ref, dk_ref, dv_ref, off_ref, dgg_ref, dk_acc, dv_acc, db_acc):
        i = pl.program_id(0)

        @pl.when(i == 0)
        def _():
            dk_acc[...] = jnp.zeros_like(dk_acc)
            dv_acc[...] = jnp.zeros_like(dv_acc)
            db_acc[...] = jnp.zeros_like(db_acc)
            dgg_ref[...] = jnp.zeros_like(dgg_ref)

        rows_of = lambda j: pl.ds(pl.multiple_of((j % 3) * QB, QB), QB)

        @pl.when(i > 0)
        def _():
            dk_acc[rows_of(i), :] = jnp.zeros((QB, D_ATTN), F32)
            dv_acc[rows_of(i), :] = jnp.zeros((QB, D_ATTN), F32)

        def block(at_start):
            ov = o_ref[...].astype(F32)
            dyv = dy_ref[...].astype(F32)
            do, dgg_t = _norm_bwd(ov, _rms(ov), gg_ref[...], dyv)
            dgg_ref[...] += jnp.sum(dgg_t, axis=0, keepdims=True)
            kb = jnp.concatenate([k0[...], k1[...], k2[...]], axis=0)
            vb = jnp.concatenate([v0[...], v1[...], v2[...]], axis=0)
            valid = lax.broadcasted_iota(jnp.int32, (1, KB), 1) >= (2 - i) * QB
            masks = _head_masks()
            lse_t = lse_ref[...]
            for hp in range(N_HEADS // 2):
                sl = slice(2 * HEAD_DIM * hp, 2 * HEAD_DIM * (hp + 1))
                q_p, k_p, v_p = q_ref[:, sl] * scale, kb[:, sl], vb[:, sl]
                do_p = do[:, sl]
                prod = do_p * ov[:, sl]
                do_b = do_p.astype(BF16)
                dq_pair = jnp.zeros((QB, 2 * HEAD_DIM), F32)
                dk_pair = jnp.zeros((KB, 2 * HEAD_DIM), F32)
                dv_pair = jnp.zeros((KB, 2 * HEAD_DIM), F32)
                for a in range(2):
                    h = 2 * hp + a
                    q_m = jnp.where(masks[a], q_p, 0)
                    do_m = jnp.where(masks[a], do_b, 0)
                    sc = _dot(q_m, k_p, "nt") + b_ref[h]
                    if at_start:
                        sc = jnp.where(valid, sc, NEG)
                    p = jnp.exp(sc - _pick_lane(lse_t, h))
                    dp = _dot(do_m, v_p, "nt")
                    delta = jnp.sum(jnp.where(masks[a], prod, 0.0), axis=-1, keepdims=True)
                    ds = p * (dp - delta)
                    db_acc[h] += ds
                    ds_b = ds.astype(BF16)
                    dq_pair = dq_pair + _dot(ds_b, jnp.where(masks[a], k_p, 0))
                    dk_pair = dk_pair + _dot(ds_b, q_m, "tn")
                    dv_pair = dv_pair + _dot(p.astype(BF16), do_m, "tn")
                dq_ref[:, sl] = (dq_pair * scale).astype(dq_ref.dtype)
                for t in range(3):
                    dk_acc[rows_of(i + 1 + t), sl] += dk_pair[QB * t:QB * (t + 1)]
                    dv_acc[rows_of(i + 1 + t), sl] += dv_pair[QB * t:QB * (t + 1)]

        pl.when(i < 2)(lambda: block(True))
        pl.when((i >= 2) & (i < nq))(lambda: block(False))

        dk_ref[...] = dk_acc[rows_of(i + 1), :].astype(dk_ref.dtype)
        dv_ref[...] = dv_acc[rows_of(i + 1), :].astype(dv_ref.dtype)

        @pl.when(i == nq + 1)
        def _():
            sub = lax.broadcasted_iota(jnp.int32, (8, 1), 0)
            pad = jnp.zeros((8, width - KB), F32)
            for h in range(N_HEADS):
                v = jnp.concatenate([db_acc[h, 0:8, :], pad], axis=1)
                for a in range(1, QB // 8):
                    grp = jnp.concatenate([db_acc[h, 8 * a:8 * a + 8, :], pad], axis=1)
                    v = v + pltpu.roll(grp, width - 8 * a, 1)
                for bit in range(3):
                    v = jnp.where(((sub >> bit) & 1) == 1, pltpu.roll(v, width - (1 << bit), 1), v)
                off_ref[h:h + 1, :] = jnp.sum(v, axis=0, keepdims=True)

    qi = lambda i: jnp.minimum(i, nq - 1)
    kblk = lambda col, back: pl.BlockSpec((QB, D_ATTN), lambda i: (jnp.clip(i - back, 0, nq - 1), QKV_COL + col))
    qblk = lambda col: pl.BlockSpec((QB, D_ATTN), lambda i: (qi(i), col))
    done = pl.BlockSpec((QB, D_ATTN), lambda i: (jnp.clip(i - 2, 0, nq - 1), 0))
    return _call(
        body, name=name, grid=(nq + 2,),
        in_specs=[qblk(QKV_COL), kblk(1, 2), kblk(1, 1), kblk(1, 0), kblk(2, 2), kblk(2, 1), kblk(2, 0),
                  qblk(0), pl.BlockSpec((QB, 128), lambda i: (qi(i), 0)), qblk(1),
                  pl.BlockSpec((N_HEADS, QB, KB), lambda i: (0, 0, 0)), pl.BlockSpec((1, D_ATTN), lambda i: (0, 0))],
        out_specs=[qblk(0), done, done, pl.BlockSpec((N_HEADS, width), lambda i: (0, 0)),
                   pl.BlockSpec((1, D_ATTN), lambda i: (0, 0))],
        out_shape=[jax.ShapeDtypeStruct((s, D_ATTN), BF16)] * 3
        + [jax.ShapeDtypeStruct((N_HEADS, width), F32), jax.ShapeDtypeStruct((1, D_ATTN), F32)],
        scratch_shapes=[pltpu.VMEM((KB, D_ATTN), F32), pltpu.VMEM((KB, D_ATTN), F32), pltpu.VMEM((N_HEADS, QB, KB), F32)],
        compiler_params=_params(("arbitrary",), 56),
    )(qkv, qkv, qkv, qkv, qkv, qkv, qkv, o, lse, dy, bias, gg)


def _block_diag(pw):
    out = jnp.zeros((D_POOL, D_POOL), pw.dtype)
    for gi in range(N_POOL):
        out = lax.dynamic_update_slice(out, pw[gi], (gi * POOL_GC, gi * POOL_GC))
    return out


_SMALL = ("pool_w", "pool_scale", "rel_bias", "group_gain", "pre_mix_g", "post_mix_g", "pre_ffn_g", "post_ffn_g")


def _pack(parts, rows):
    flat = jnp.concatenate([p.reshape(-1).astype(F32) for p in parts])
    return jnp.pad(flat, (0, rows * D_MODEL - flat.shape[0])).reshape(rows, D_MODEL)


def _unpack(packed, shapes):
    flat = packed.reshape(-1)
    out, at = [], 0
    for shp in shapes:
        size = int(np.prod(shp))
        out.append(flat[at:at + size].reshape(shp))
        at += size
    return out


def kernel(x, w_in, w_out, conv_w, pool_w, pool_scale, rel_bias, group_gain, pre_mix_g, post_mix_g, pre_ffn_g, post_ffn_g, w_gate_up, w_down, loss_target, m_w_in, m_w_out, m_conv_w, m_pool_w, m_pool_scale, m_rel_bias, m_group_gain, m_pre_mix_g, m_post_mix_g, m_pre_ffn_g, m_post_ffn_g, m_w_gate_up, m_w_down, v_w_in, v_w_out, v_conv_w, v_pool_w, v_pool_scale, v_rel_bias, v_group_gain, v_pre_mix_g, v_post_mix_g, v_pre_ffn_g, v_post_ffn_g, v_w_gate_up, v_w_down):
    depth = w_in.shape[0]
    s = x.shape[1]
    c_me = lax.axis_index("c")
    q_me = 2 * lax.axis_index("x") + lax.axis_index("y")
    dev = 2 * q_me + c_me

    tr = lambda a: jnp.swapaxes(a, 1, 2)
    w_in_t, w_gu_t = tr(w_in), tr(w_gate_up)

    def group_a(l):
        return [w_in_t[l].astype(BF16), conv_w[l]]

    def group_b(l):
        return [w_out[l].astype(BF16), w_gu_t[l].astype(BF16), w_down[l].astype(BF16)]

    def weights_a(parts):
        g_in, g_cw = parts
        return assemble_rows("assemble_in", *g_in), assemble_cols("assemble_conv", *g_cw)

    def weights_b(parts):
        g_out, g_gu, g_dn = parts
        return (assemble_rows("assemble_out", *g_out), assemble_rows("assemble_gu", *g_gu),
                assemble_rows("assemble_down", *g_dn))

    h = x.reshape(s, D_MODEL)
    start_a = gather_start("gather_a0_s", group_a(0))
    start_b = gather_start("gather_b0_s", group_b(0), [start_a["token"]])
    xn = rmsnorm_fwd("norm_mix", h, pre_mix_g[0].reshape(1, -1), BF16)
    relay_a = gather_relay("gather_a0_r", start_a, start_b["token"])
    wa = weights_a(gather_finish("gather_a0_f", relay_a, xn))
    wb = None
    ahead = [start_b["token"]]
    saved = []
    for l in range(depth):
        vec = lambda p: p[l].reshape(1, -1)
        wt_in, cw_full = wa
        if l + 1 < depth:
            start_next = gather_start(f"gather_{l + 1}_s", group_a(l + 1) + group_b(l + 1), ahead)
            ahead = [start_next["token"]]
        proj = matmul("proj", xn, wt_in, "nt", BF16, s, 5 * D_ATTN, D_MODEL, 1024, 5 * D_ATTN, D_MODEL, deps=ahead)
        bd = _block_diag(pool_w[l]).astype(BF16)
        gg = vec(group_gain)
        y_ab = conv_pool_fwd("conv_pool_fwd", proj, cw_full, bd, vec(pool_scale), gg[:, :2 * D_CONV], deps=ahead)
        if l == 0:
            relay_b = gather_relay("gather_b0_r", start_b, y_ab)
            ahead = [relay_b["token"]]
        y, o, lse, bias = attention_fwd("attention_fwd", proj, bias_by_offset(rel_bias[l]), gg[:, 2 * D_CONV:], y_ab,
                                        deps=ahead if l == 0 else [])
        if l == 0:
            wb = weights_b(gather_finish("gather_b0_f", relay_b, y))
        wf_out, wt_gu, wf_dn = wb
        if l + 1 < depth:
            relay_next = gather_relay(f"gather_{l + 1}_r", start_next, y)
        mix, h_mid, hn = matmul_norm("mix_out", y, wf_out, h, vec(post_mix_g), then=(vec(pre_ffn_g), BF16), tm=1024,
                                     deps=[relay_next["token"]] if l + 1 < depth else [])
        ff_du, ff_dg, ff = matmul_swiglu("gate_up", hn, wt_gu)
        if l + 1 < depth:
            parts = gather_finish(f"gather_{l + 1}_f", relay_next, ff)
            wa, wb = weights_a(parts[:2]), weights_b(parts[2:])
            ffo, h_out, xn_next = matmul_norm("ffn_down", ff, wf_dn, h_mid, vec(post_ffn_g),
                                              then=(pre_mix_g[l + 1].reshape(1, -1), BF16))
        else:
            ffo, h_out = matmul_norm("ffn_down_last", ff, wf_dn, h_mid, vec(post_ffn_g))
            xn_next = None
        saved.append((h, xn, proj, None, bd, bias, y, o, lse, mix, h_mid, hn, ff_du, ff_dg, ff, ffo,
                      wt_in, wf_out, cw_full, wt_gu, wf_dn))
        h, xn = h_out, xn_next

    dh, loss_part, dffo, dg_last = loss_head("loss_head", h, loss_target.reshape(s, D_MODEL), saved[-1][15],
                                             post_ffn_g[depth - 1].reshape(1, -1))

    def reduce_begin(name, halves):
        return [k for k, _ in halves], exchange_start(name, plan_swap_cores, len(halves), [snd for _, snd in halves],
                                                      [(k.shape, k.dtype) for k, _ in halves])

    def reduce_relay(name, kept, swapped, after):
        _, got = exchange_wait(name + "_w", swapped, after)
        sums = [add_pairs("reduce_add", k, r) for k, r in zip(kept, got)]
        return exchange_start(name + "_s", plan_to_chips, 3 * len(sums), sums, [(a.shape, a.dtype) for a in sums])

    def reduce_finish(name, relayed, after):
        sums, got = exchange_wait(name, relayed, after)
        return list(zip(sums, got))

    small = {k: [None] * depth for k in _SMALL + ("conv_w",)}
    relayed = {}
    carried = []
    small["post_ffn_g"][depth - 1] = dg_last
    for l in reversed(range(depth)):
        vec = lambda p: p[l].reshape(1, -1)
        (h_in, xn, proj, _, bd, bias, y, o, lse, mix, h_mid, hn, ff_du, ff_dg, ff, ffo,
         wt_in, wf_out, cw_full, wt_gu, wf_dn) = saved[l]
        gg = vec(group_gain)
        tk_w = min(s, 2048)
        gw_dn = matmul("wgrad_down", ff, dffo, "tn", BF16, D_FF, D_MODEL, s, 1408, D_MODEL, tk_w,
                       shard_rows=D_FF // 8, deps=carried)
        dg, du = matmul_swiglu_bwd("dgrad_down", dffo, wf_dn, ff_du, ff_dg)
        gw_gu = matmul("wgrad_gu", [dg, du], hn, "tn", BF16, 2 * D_FF, D_MODEL, s, 1408, D_MODEL, tk_w,
                       shard_rows=2 * D_FF // 8)
        early = [gw_gu, gw_dn] if l == 0 else []
        if early:
            kept_b, swap_b = reduce_begin("reduce_b0_s", early)
            carried = [swap_b["token"]]
        dh_mid, dmix, small["pre_ffn_g"][l], small["post_mix_g"][l] = matmul_norm_bwd(
            "dgrad_gu", [dg, du], wt_gu, 512, h_mid, vec(pre_ffn_g), dh, then=(mix, vec(post_mix_g), BF16),
            deps=carried)
        if early:
            relayed[l, "b"] = reduce_relay("reduce_b0_r", kept_b, swap_b, dmix)
            carried = [relayed[l, "b"]["token"]]
        gw_out = matmul("wgrad_out", y, dmix, "tn", BF16, D_MODEL, D_MODEL, s, D_MODEL, D_MODEL, tk_w,
                        shard_rows=D_MODEL // 8, deps=carried)
        dy = matmul("dgrad_out", dmix, wf_out, "nt", BF16, s, D_MODEL, D_MODEL, 1024, D_MODEL, D_MODEL)
        dpa, dcw, dbd, dps, dgg_ab = conv_pool_bwd("conv_pool_bwd", proj, dy, cw_full, bd, vec(pool_scale),
                                                   gg[:, :2 * D_CONV])
        dq, dk, dv, by_off, dgg_c = attention_bwd("attention_bwd", proj, o, lse, dy, bias, gg[:, 2 * D_CONV:])
        dparts = [dpa, dq, dk, dv]
        gw_in = matmul("wgrad_in", dparts, xn, "tn", BF16, 5 * D_ATTN, D_MODEL, s, 512, D_MODEL, min(s, 2048))
        late = [split_rows("split_in", gw_in), gw_out] + ([] if early else [gw_gu, gw_dn])
        kept_a, swap_a = reduce_begin(f"reduce_a{l}_s", late)
        if l > 0:
            dh, dffo, small["pre_mix_g"][l], small["post_ffn_g"][l - 1] = matmul_norm_bwd(
                "dgrad_in", dparts, wt_in, 512, h_in, vec(pre_mix_g), dh_mid,
                then=(saved[l - 1][15], post_ffn_g[l - 1].reshape(1, -1), BF16), chunk=32, deps=[swap_a["token"]])
        else:
            dh, small["pre_mix_g"][l] = matmul_norm_bwd("dgrad_in_first", dparts, wt_in, 512, h_in, vec(pre_mix_g),
                                                        dh_mid, chunk=32, deps=[swap_a["token"]])
        relayed[l, "a"] = reduce_relay(f"reduce_a{l}_r", kept_a, swap_a, dh)
        carried = [relayed[l, "a"]["token"]]
        small["conv_w"][l] = dcw
        small["pool_w"][l] = jnp.stack([dbd[gi * POOL_GC:(gi + 1) * POOL_GC, gi * POOL_GC:(gi + 1) * POOL_GC]
                                        for gi in range(N_POOL)])
        small["pool_scale"][l] = dps
        small["rel_bias"][l] = rel_bias_grad(by_off)
        small["group_gain"][l] = jnp.concatenate([dgg_ab, dgg_c], axis=1)
    grad_x = dh.reshape(x.shape)

    small_params = dict(pool_w=pool_w, pool_scale=pool_scale, rel_bias=rel_bias, group_gain=group_gain,
                        pre_mix_g=pre_mix_g, post_mix_g=post_mix_g, pre_ffn_g=pre_ffn_g, post_ffn_g=post_ffn_g)
    small_m = dict(pool_w=m_pool_w, pool_scale=m_pool_scale, rel_bias=m_rel_bias, group_gain=m_group_gain,
                   pre_mix_g=m_pre_mix_g, post_mix_g=m_post_mix_g, pre_ffn_g=m_pre_ffn_g, post_ffn_g=m_post_ffn_g)
    small_v = dict(pool_w=v_pool_w, pool_scale=v_pool_scale, rel_bias=v_rel_bias, group_gain=v_group_gain,
                   pre_mix_g=v_pre_mix_g, post_mix_g=v_post_mix_g, pre_ffn_g=v_pre_ffn_g, post_ffn_g=v_post_ffn_g)
    shapes = [small_params[k].shape for k in _SMALL] + [(depth, 3, D_CONV), (1,)]
    n_small = sum(int(np.prod(shp)) for shp in shapes)
    rows = -(-n_small // (8 * D_MODEL)) * 8
    extra = [jnp.zeros((depth, 3, D_CONV), F32), jnp.zeros((1,), F32)]
    grads_packed = _pack([jnp.stack(small[k]) for k in _SMALL] + [jnp.stack(small["conv_w"]), loss_part[0, 0:1]], rows)
    small_started = gather_start("gather_small_s", [grads_packed], carried)

    big = dict(w_in=(w_in_t, tr(m_w_in), tr(v_w_in)), w_out=(w_out, m_w_out, v_w_out),
               w_gate_up=(w_gu_t, tr(m_w_gate_up), tr(v_w_gate_up)), w_down=(w_down, m_w_down, v_w_down))
    results = dict(w_in=None, w_out=None, w_gate_up=None, w_down=None)
    after = small_started["token"]
    small_relay = None
    for l in reversed(range(depth)):
        if (l, "b") in relayed:
            slabs_gu, slabs_dn = reduce_finish(f"reduce_b{l}_f", relayed[l, "b"], after)
            slabs_in, slabs_out = reduce_finish(f"reduce_a{l}_f", relayed[l, "a"], slabs_dn[1])
        else:
            slabs_in, slabs_out, slabs_gu, slabs_dn = reduce_finish(f"reduce_a{l}_f", relayed[l, "a"], after)
        for k, slabs in (("w_gate_up", slabs_gu), ("w_down", slabs_dn), ("w_in", slabs_in), ("w_out", slabs_out)):
            results[k] = adamw_layer("adamw_" + k, l, *big[k], *slabs, results[k])
        after = results["w_out"][0]
        if l == min(1, depth - 1):
            small_relay = gather_relay("gather_small_r", small_started, after)
    small_parts = gather_finish("gather_small_f", small_relay, after)[0]
    all_small = assemble_rows("assemble_small", *small_parts).reshape(8, rows, D_MODEL)
    res_small = adamw("adamw_small", _pack([small_params[k] for k in _SMALL] + extra, rows),
                      _pack([small_m[k] for k in _SMALL] + extra, rows),
                      _pack([small_v[k] for k in _SMALL] + extra, rows), all_small)
    g_s, d_s, m_s, v_s = (_unpack(r, shapes) for r in res_small)
    loss = g_s[-1][0]
    g_conv = lax.dynamic_slice_in_dim(g_s[-2], dev * conv_w.shape[2], conv_w.shape[2], axis=2)
    results["conv_w"] = adamw("adamw_conv", conv_w, m_conv_w, v_conv_w, g_conv[None])

    names = ("w_in", "w_out", "conv_w") + _SMALL + ("w_gate_up", "w_down")
    for k in ("w_in", "w_gate_up"):
        results[k] = tuple(tr(r) for r in results[k])
    for j, k in enumerate(_SMALL):
        results[k] = (g_s[j], d_s[j], m_s[j], v_s[j])
    return (loss, grad_x, *[results[k][0] for k in names], *[results[k][1] for k in names],
            *[results[k][2] for k in names], *[results[k][3] for k in names])
```
